```python
import jax, jax.numpy as jnp
from jax import lax
import numpy as np

D_MODEL = 1024
BATCH = 8
SEQ = 4096
DEPTH = 2

CHUNK = 64
POOL_WINDOWS = (2, 4, 8, 16)
POOL_GROUP = 64
D_POOL = POOL_GROUP * len(POOL_WINDOWS)
N_HEADS = 8
HEAD_DIM = 64
D_ATTN = N_HEADS * HEAD_DIM
N_PREV_CHUNKS = 8
BAND = (N_PREV_CHUNKS + 1) * CHUNK
REL_CLIP = 128
N_REL = 2 * REL_CLIP + 1
D_CONV = 256
CONV_WIDTH = 31
D_FF = 4 * D_MODEL
N_BRANCH = 3
IN_SIZES = (D_POOL, D_ATTN, D_ATTN, D_ATTN, 2 * D_CONV, N_BRANCH * D_MODEL)
IN_SPLITS = tuple(int(v) for v in np.cumsum(IN_SIZES)[:-1])
D_IN = int(sum(IN_SIZES))
ALPHA = (2.0 * DEPTH) ** 0.25
BETA = (8.0 * DEPTH) ** -0.25
LN_EPS = 1e-5
NEG_INF = -1e30

kernel_name = "hybrid_chunk_causal_pool_attn_conv_block"


def layer_norm(x, g=None, b=None):
    xf = x.astype(jnp.float32)
    mu = jnp.mean(xf, axis=-1, keepdims=True)
    var = jnp.mean(jnp.square(xf - mu), axis=-1, keepdims=True)
    y = (xf - mu) * lax.rsqrt(var + LN_EPS)
    if g is not None:
        y = y * g.astype(jnp.float32) + b.astype(jnp.float32)
    return y.astype(x.dtype)


def pool_mixer(a, w_pool, pool_scale):
    B, S, _ = a.shape
    t = jnp.arange(S)
    outs = []
    for gi, w in enumerate(POOL_WINDOWS):
        xg = a[..., gi * POOL_GROUP:(gi + 1) * POOL_GROUP].astype(jnp.float32)
        cs = jnp.cumsum(xg, axis=1)
        cs_lag = jnp.pad(cs, ((0, 0), (w, 0), (0, 0)))[:, :S]
        count = jnp.minimum(t + 1, w).astype(jnp.float32)[None, :, None]
        outs.append((cs - cs_lag) / count - xg)
    p = jnp.stack(outs, axis=2).astype(a.dtype)
    p = jnp.einsum('bsgc,gcd->bsgd', p, w_pool).reshape(B, S, D_POOL)
    return p * pool_scale


def chunk_attention(q, k, v, rel_bias):
    B, S, _ = q.shape
    nc = S // CHUNK
    q = q.reshape(B, nc, CHUNK, N_HEADS, HEAD_DIM) * (HEAD_DIM ** -0.5)
    pad = ((0, 0), (N_PREV_CHUNKS * CHUNK, 0), (0, 0))
    kc = jnp.pad(k, pad).reshape(B, nc + N_PREV_CHUNKS, CHUNK, N_HEADS, HEAD_DIM)
    vc = jnp.pad(v, pad).reshape(B, nc + N_PREV_CHUNKS, CHUNK, N_HEADS, HEAD_DIM)
    band_idx = jnp.arange(nc)[:, None] + jnp.arange(N_PREV_CHUNKS + 1)[None, :]
    kb = kc[:, band_idx].reshape(B, nc, BAND, N_HEADS, HEAD_DIM)
    vb = vc[:, band_idx].reshape(B, nc, BAND, N_HEADS, HEAD_DIM)
    s = jnp.einsum('bnqhd,bnkhd->bnhqk', q, kb).astype(jnp.float32)
    qi = jnp.arange(CHUNK)[:, None]
    kj = jnp.arange(BAND)[None, :]
    rel = jnp.clip(N_PREV_CHUNKS * CHUNK + qi - kj, -REL_CLIP, REL_CLIP) + REL_CLIP
    bias = rel_bias[:, rel].astype(jnp.float32)
    key_pos = jnp.arange(nc)[:, None] * CHUNK + kj - N_PREV_CHUNKS * CHUNK
    valid = (key_pos >= 0)[None, :, None, None, :]
    s = jnp.where(valid, s + bias[None, None], NEG_INF)
    p = jax.nn.softmax(s, axis=-1).astype(vb.dtype)
    o = jnp.einsum('bnhqk,bnkhd->bnqhd', p, vb)
    return o.reshape(B, S, D_ATTN)


def conv_module(cin, conv_w, conv_b, ln_g, ln_b):
    h = cin[..., :D_CONV] * jax.nn.sigmoid(cin[..., D_CONV:])
    h = jnp.pad(h, ((0, 0), (CONV_WIDTH - 1, 0), (0, 0)))
    h = lax.conv_general_dilated(h, conv_w[:, None, :].astype(h.dtype), window_strides=(1,),
                                 padding='VALID', dimension_numbers=('NWC', 'WIO', 'NWC'),
                                 feature_group_count=D_CONV) + conv_b
    return jax.nn.silu(layer_norm(h, ln_g, ln_b))


def _fwd_setup_inputs(seed: int = 0) -> dict:
    key = jax.random.key(seed)
    ks = jax.random.split(key, 26)
    L, D = DEPTH, D_MODEL

    def nrm(k, shape, scale):
        return jax.random.normal(k, shape, jnp.float32) * scale

    return {
        'x': nrm(ks[0], (BATCH, SEQ, D), 1.0),
        'c': nrm(ks[1], (BATCH, D), 1.0),
        'w_ada': nrm(ks[2], (L, D, 6 * D), 0.5 * D ** -0.5),
        'b_ada': nrm(ks[3], (L, 6 * D), 0.02),
        'w_in': nrm(ks[4], (L, D, D_IN), D ** -0.5),
        'b_gate': nrm(ks[5], (L, N_BRANCH * D), 0.1),
        'w_pool': nrm(ks[6], (L, len(POOL_WINDOWS), POOL_GROUP, POOL_GROUP), POOL_GROUP ** -0.5),
        'pool_scale': 1.0 + nrm(ks[7], (L, D_POOL), 0.1),
        'rel_bias': nrm(ks[8], (L, N_HEADS, N_REL), 0.1),
        'conv_w': nrm(ks[9], (L, CONV_WIDTH, D_CONV), CONV_WIDTH ** -0.5),
        'conv_b': nrm(ks[10], (L, D_CONV), 0.02),
        'conv_ln_g': 1.0 + nrm(ks[11], (L, D_CONV), 0.05),
        'conv_ln_b': nrm(ks[12], (L, D_CONV), 0.02),
        'w_br_pool': nrm(ks[13], (L, D_POOL, D), BETA * D_POOL ** -0.5),
        'w_br_attn': nrm(ks[14], (L, D_ATTN, D), BETA * D_ATTN ** -0.5),
        'w_br_conv': nrm(ks[15], (L, D_CONV, D), BETA * D_CONV ** -0.5),
        'w_o': nrm(ks[16], (L, D, D), BETA * D ** -0.5),
        'ln_mix_g': 1.0 + nrm(ks[17], (L, D), 0.05),
        'ln_mix_b': nrm(ks[18], (L, D), 0.02),
        'w_ff1': nrm(ks[19], (L, D, D_FF), D ** -0.5),
        'b_ff1': nrm(ks[20], (L, D_FF), 0.02),
        'w_ff2': nrm(ks[21], (L, D_FF, D), BETA * D_FF ** -0.5),
        'b_ff2': nrm(ks[22], (L, D), 0.02),
        'ln_ff_g': 1.0 + nrm(ks[23], (L, D), 0.05),
        'ln_ff_b': nrm(ks[24], (L, D), 0.02),
    }


def _fwd_reference(x, c, w_ada, b_ada, w_in, b_gate, w_pool, pool_scale, rel_bias, conv_w, conv_b,
              conv_ln_g, conv_ln_b, w_br_pool, w_br_attn, w_br_conv, w_o, ln_mix_g, ln_mix_b,
              w_ff1, b_ff1, w_ff2, b_ff2, ln_ff_g, ln_ff_b):
    B, S, _ = x.shape
    c_act = jax.nn.silu(c)
    for l in range(DEPTH):
        mod = (c_act @ w_ada[l] + b_ada[l])[:, None, :]
        sh_m, sc_m, g_m, sh_f, sc_f, g_f = jnp.split(mod, 6, axis=-1)

        u = layer_norm(x) * (1 + sc_m) + sh_m
        z = u @ w_in[l]
        z_pool, z_q, z_k, z_v, z_conv, z_gate = jnp.split(z, IN_SPLITS, axis=-1)
        y_pool = pool_mixer(z_pool, w_pool[l], pool_scale[l]) @ w_br_pool[l]
        y_attn = chunk_attention(z_q, z_k, z_v, rel_bias[l]) @ w_br_attn[l]
        y_conv = conv_module(z_conv, conv_w[l], conv_b[l], conv_ln_g[l], conv_ln_b[l]) @ w_br_conv[l]
        gates = jax.nn.sigmoid(z_gate + b_gate[l]).reshape(B, S, N_BRANCH, D_MODEL)
        merged = gates[:, :, 0] * y_pool + gates[:, :, 1] * y_attn + gates[:, :, 2] * y_conv
        mix_out = merged @ w_o[l]
        x = layer_norm(ALPHA * x + g_m * mix_out, ln_mix_g[l], ln_mix_b[l])

        u = layer_norm(x) * (1 + sc_f) + sh_f
        h = jnp.square(jax.nn.relu(u @ w_ff1[l] + b_ff1[l]))
        ff_out = h @ w_ff2[l] + b_ff2[l]
        x = layer_norm(ALPHA * x + g_f * ff_out, ln_ff_g[l], ln_ff_b[l])
    return x


import jax as _jax
import jax.numpy as _jnp

TWIN_FORMAT = 'train_step'
FWD_PARAMS = ['x', 'c', 'w_ada', 'b_ada', 'w_in', 'b_gate', 'w_pool', 'pool_scale', 'rel_bias', 'conv_w', 'conv_b', 'conv_ln_g', 'conv_ln_b', 'w_br_pool', 'w_br_attn', 'w_br_conv', 'w_o', 'ln_mix_g', 'ln_mix_b', 'w_ff1', 'b_ff1', 'w_ff2', 'b_ff2', 'ln_ff_g', 'ln_ff_b']
TWIN_WEIGHTS = ['w_ada', 'b_ada', 'w_in', 'b_gate', 'w_pool', 'pool_scale', 'rel_bias', 'conv_w', 'conv_b', 'conv_ln_g', 'conv_ln_b', 'w_br_pool', 'w_br_attn', 'w_br_conv', 'w_o', 'ln_mix_g', 'ln_mix_b', 'w_ff1', 'b_ff1', 'w_ff2', 'b_ff2', 'ln_ff_g', 'ln_ff_b']
TWIN_DIFF_INPUT = 'x'
TWIN_INPUTS = ['x', 'c', 'w_ada', 'b_ada', 'w_in', 'b_gate', 'w_pool', 'pool_scale', 'rel_bias', 'conv_w', 'conv_b', 'conv_ln_g', 'conv_ln_b', 'w_br_pool', 'w_br_attn', 'w_br_conv', 'w_o', 'ln_mix_g', 'ln_mix_b', 'w_ff1', 'b_ff1', 'w_ff2', 'b_ff2', 'ln_ff_g', 'ln_ff_b', 'loss_target', 'm_w_ada', 'm_b_ada', 'm_w_in', 'm_b_gate', 'm_w_pool', 'm_pool_scale', 'm_rel_bias', 'm_conv_w', 'm_conv_b', 'm_conv_ln_g', 'm_conv_ln_b', 'm_w_br_pool', 'm_w_br_attn', 'm_w_br_conv', 'm_w_o', 'm_ln_mix_g', 'm_ln_mix_b', 'm_w_ff1', 'm_b_ff1', 'm_w_ff2', 'm_b_ff2', 'm_ln_ff_g', 'm_ln_ff_b', 'v_w_ada', 'v_b_ada', 'v_w_in', 'v_b_gate', 'v_w_pool', 'v_pool_scale', 'v_rel_bias', 'v_conv_w', 'v_conv_b', 'v_conv_ln_g', 'v_conv_ln_b', 'v_w_br_pool', 'v_w_br_attn', 'v_w_br_conv', 'v_w_o', 'v_ln_mix_g', 'v_ln_mix_b', 'v_w_ff1', 'v_b_ff1', 'v_w_ff2', 'v_b_ff2', 'v_ln_ff_g', 'v_ln_ff_b']
TWIN_OUTPUTS = ['loss', 'grad_x', 'grad_w_ada', 'grad_b_ada', 'grad_w_in', 'grad_b_gate', 'grad_w_pool', 'grad_pool_scale', 'grad_rel_bias', 'grad_conv_w', 'grad_conv_b', 'grad_conv_ln_g', 'grad_conv_ln_b', 'grad_w_br_pool', 'grad_w_br_attn', 'grad_w_br_conv', 'grad_w_o', 'grad_ln_mix_g', 'grad_ln_mix_b', 'grad_w_ff1', 'grad_b_ff1', 'grad_w_ff2', 'grad_b_ff2', 'grad_ln_ff_g', 'grad_ln_ff_b', 'delta_w_ada', 'delta_b_ada', 'delta_w_in', 'delta_b_gate', 'delta_w_pool', 'delta_pool_scale', 'delta_rel_bias', 'delta_conv_w', 'delta_conv_b', 'delta_conv_ln_g', 'delta_conv_ln_b', 'delta_w_br_pool', 'delta_w_br_attn', 'delta_w_br_conv', 'delta_w_o', 'delta_ln_mix_g', 'delta_ln_mix_b', 'delta_w_ff1', 'delta_b_ff1', 'delta_w_ff2', 'delta_b_ff2', 'delta_ln_ff_g', 'delta_ln_ff_b', 'new_m_w_ada', 'new_m_b_ada', 'new_m_w_in', 'new_m_b_gate', 'new_m_w_pool', 'new_m_pool_scale', 'new_m_rel_bias', 'new_m_conv_w', 'new_m_conv_b', 'new_m_conv_ln_g', 'new_m_conv_ln_b', 'new_m_w_br_pool', 'new_m_w_br_attn', 'new_m_w_br_conv', 'new_m_w_o', 'new_m_ln_mix_g', 'new_m_ln_mix_b', 'new_m_w_ff1', 'new_m_b_ff1', 'new_m_w_ff2', 'new_m_b_ff2', 'new_m_ln_ff_g', 'new_m_ln_ff_b', 'new_v_w_ada', 'new_v_b_ada', 'new_v_w_in', 'new_v_b_gate', 'new_v_w_pool', 'new_v_pool_scale', 'new_v_rel_bias', 'new_v_conv_w', 'new_v_conv_b', 'new_v_conv_ln_g', 'new_v_conv_ln_b', 'new_v_w_br_pool', 'new_v_w_br_attn', 'new_v_w_br_conv', 'new_v_w_o', 'new_v_ln_mix_g', 'new_v_ln_mix_b', 'new_v_w_ff1', 'new_v_b_ff1', 'new_v_w_ff2', 'new_v_b_ff2', 'new_v_ln_ff_g', 'new_v_ln_ff_b']
TWIN_LEAF_KINDS = {'loss': 'loss', 'grad_x': 'grad_x', 'grad_w_ada': 'grad_w', 'grad_b_ada': 'grad_w', 'grad_w_in': 'grad_w', 'grad_b_gate': 'grad_w', 'grad_w_pool': 'grad_w', 'grad_pool_scale': 'grad_w', 'grad_rel_bias': 'grad_w', 'grad_conv_w': 'grad_w', 'grad_conv_b': 'grad_w', 'grad_conv_ln_g': 'grad_w', 'grad_conv_ln_b': 'grad_w', 'grad_w_br_pool': 'grad_w', 'grad_w_br_attn': 'grad_w', 'grad_w_br_conv': 'grad_w', 'grad_w_o': 'grad_w', 'grad_ln_mix_g': 'grad_w', 'grad_ln_mix_b': 'grad_w', 'grad_w_ff1': 'grad_w', 'grad_b_ff1': 'grad_w', 'grad_w_ff2': 'grad_w', 'grad_b_ff2': 'grad_w', 'grad_ln_ff_g': 'grad_w', 'grad_ln_ff_b': 'grad_w', 'delta_w_ada': 'delta_w', 'delta_b_ada': 'delta_w', 'delta_w_in': 'delta_w', 'delta_b_gate': 'delta_w', 'delta_w_pool': 'delta_w', 'delta_pool_scale': 'delta_w', 'delta_rel_bias': 'delta_w', 'delta_conv_w': 'delta_w', 'delta_conv_b': 'delta_w', 'delta_conv_ln_g': 'delta_w', 'delta_conv_ln_b': 'delta_w', 'delta_w_br_pool': 'delta_w', 'delta_w_br_attn': 'delta_w', 'delta_w_br_conv': 'delta_w', 'delta_w_o': 'delta_w', 'delta_ln_mix_g': 'delta_w', 'delta_ln_mix_b': 'delta_w', 'delta_w_ff1': 'delta_w', 'delta_b_ff1': 'delta_w', 'delta_w_ff2': 'delta_w', 'delta_b_ff2': 'delta_w', 'delta_ln_ff_g': 'delta_w', 'delta_ln_ff_b': 'delta_w', 'new_m_w_ada': 'new_m', 'new_m_b_ada': 'new_m', 'new_m_w_in': 'new_m', 'new_m_b_gate': 'new_m', 'new_m_w_pool': 'new_m', 'new_m_pool_scale': 'new_m', 'new_m_rel_bias': 'new_m', 'new_m_conv_w': 'new_m', 'new_m_conv_b': 'new_m', 'new_m_conv_ln_g': 'new_m', 'new_m_conv_ln_b': 'new_m', 'new_m_w_br_pool': 'new_m', 'new_m_w_br_attn': 'new_m', 'new_m_w_br_conv': 'new_m', 'new_m_w_o': 'new_m', 'new_m_ln_mix_g': 'new_m', 'new_m_ln_mix_b': 'new_m', 'new_m_w_ff1': 'new_m', 'new_m_b_ff1': 'new_m', 'new_m_w_ff2': 'new_m', 'new_m_b_ff2': 'new_m', 'new_m_ln_ff_g': 'new_m', 'new_m_ln_ff_b': 'new_m', 'new_v_w_ada': 'new_v', 'new_v_b_ada': 'new_v', 'new_v_w_in': 'new_v', 'new_v_b_gate': 'new_v', 'new_v_w_pool': 'new_v', 'new_v_pool_scale': 'new_v', 'new_v_rel_bias': 'new_v', 'new_v_conv_w': 'new_v', 'new_v_conv_b': 'new_v', 'new_v_conv_ln_g': 'new_v', 'new_v_conv_ln_b': 'new_v', 'new_v_w_br_pool': 'new_v', 'new_v_w_br_attn': 'new_v', 'new_v_w_br_conv': 'new_v', 'new_v_w_o': 'new_v', 'new_v_ln_mix_g': 'new_v', 'new_v_ln_mix_b': 'new_v', 'new_v_w_ff1': 'new_v', 'new_v_b_ff1': 'new_v', 'new_v_w_ff2': 'new_v', 'new_v_b_ff2': 'new_v', 'new_v_ln_ff_g': 'new_v', 'new_v_ln_ff_b': 'new_v'}


def _forward(args):
    return _fwd_reference(*[args[k] for k in FWD_PARAMS])


def _output_shape():
    out = _jax.eval_shape(lambda: _forward(_fwd_setup_inputs(0)))
    return out.shape, out.dtype

N_MICROBATCH = 1
ADAM_LR = 0.001
ADAM_B1 = 0.9
ADAM_B2 = 0.999
ADAM_EPS = 1e-08
ADAM_WD = 0.01
ADAM_STEP = 10
PER_EXAMPLE_BATCH_AXIS = {'x': 0, 'c': 0, 'loss_target': 0}
SHARED_INPUTS = []
_WEIGHT_DTYPES = {'w_ada': _jnp.float32, 'b_ada': _jnp.float32, 'w_in': _jnp.float32, 'b_gate': _jnp.float32, 'w_pool': _jnp.float32, 'pool_scale': _jnp.float32, 'rel_bias': _jnp.float32, 'conv_w': _jnp.float32, 'conv_b': _jnp.float32, 'conv_ln_g': _jnp.float32, 'conv_ln_b': _jnp.float32, 'w_br_pool': _jnp.float32, 'w_br_attn': _jnp.float32, 'w_br_conv': _jnp.float32, 'w_o': _jnp.float32, 'ln_mix_g': _jnp.float32, 'ln_mix_b': _jnp.float32, 'w_ff1': _jnp.float32, 'b_ff1': _jnp.float32, 'w_ff2': _jnp.float32, 'b_ff2': _jnp.float32, 'ln_ff_g': _jnp.float32, 'ln_ff_b': _jnp.float32}
MOMENT_SCALE = {'w_ada': 3.368313e-02, 'b_ada': 6.285660e-02, 'w_in': 3.079953e-03, 'b_gate': 1.470936e-03, 'w_pool': 1.011248e-02, 'pool_scale': 1.070591e-02, 'rel_bias': 6.070888e-04, 'conv_w': 7.192320e-03, 'conv_b': 1.672420e-02, 'conv_ln_g': 9.051906e-03, 'conv_ln_b': 1.135131e-02, 'w_br_pool': 9.865480e-03, 'w_br_attn': 4.608103e-03, 'w_br_conv': 7.287847e-03, 'w_o': 1.297632e-02, 'ln_mix_g': 2.932166e+00, 'ln_mix_b': 4.408243e-01, 'w_ff1': 1.633324e-02, 'b_ff1': 1.891615e-02, 'w_ff2': 6.449463e-02, 'b_ff2': 6.982010e-02, 'ln_ff_g': 2.304417e+01, 'ln_ff_b': 1.382743e+00}


def _to_microbatches(a, axis):
    t = _jnp.moveaxis(a, axis, 0)
    t = t.reshape((N_MICROBATCH, t.shape[0] // N_MICROBATCH) + t.shape[1:])
    return _jnp.moveaxis(t, 1, axis + 1)


def setup_inputs(seed: int = 0) -> dict:
    inp = _fwd_setup_inputs(seed)
    key = _jax.random.fold_in(_jax.random.key(seed), 7919)
    shape, _ = _output_shape()
    out = dict(inp)
    out["loss_target"] = _jax.random.normal(_jax.random.fold_in(key, 0), shape, _jnp.float32)
    for i, name in enumerate(TWIN_WEIGHTS):
        w = inp[name].astype(_jnp.float32)
        if MOMENT_SCALE is None:
            s = _jnp.sqrt(_jnp.mean(_jnp.square(w)) + 1e-30)
        else:
            s = MOMENT_SCALE[name]
        km, kv = _jax.random.split(_jax.random.fold_in(key, i + 1))
        out[name] = w
        out["m_" + name] = s * _jax.random.normal(km, w.shape, _jnp.float32)
        out["v_" + name] = (s * s) * _jax.random.uniform(kv, w.shape, _jnp.float32, 0.5, 1.5)
    if N_MICROBATCH > 1:
        for name, axis in PER_EXAMPLE_BATCH_AXIS.items():
            out[name] = _to_microbatches(out[name], axis)
    return {'x': out['x'], 'c': out['c'], 'w_ada': out['w_ada'], 'b_ada': out['b_ada'], 'w_in': out['w_in'], 'b_gate': out['b_gate'], 'w_pool': out['w_pool'], 'pool_scale': out['pool_scale'], 'rel_bias': out['rel_bias'], 'conv_w': out['conv_w'], 'conv_b': out['conv_b'], 'conv_ln_g': out['conv_ln_g'], 'conv_ln_b': out['conv_ln_b'], 'w_br_pool': out['w_br_pool'], 'w_br_attn': out['w_br_attn'], 'w_br_conv': out['w_br_conv'], 'w_o': out['w_o'], 'ln_mix_g': out['ln_mix_g'], 'ln_mix_b': out['ln_mix_b'], 'w_ff1': out['w_ff1'], 'b_ff1': out['b_ff1'], 'w_ff2': out['w_ff2'], 'b_ff2': out['b_ff2'], 'ln_ff_g': out['ln_ff_g'], 'ln_ff_b': out['ln_ff_b'], 'loss_target': out['loss_target'], 'm_w_ada': out['m_w_ada'], 'm_b_ada': out['m_b_ada'], 'm_w_in': out['m_w_in'], 'm_b_gate': out['m_b_gate'], 'm_w_pool': out['m_w_pool'], 'm_pool_scale': out['m_pool_scale'], 'm_rel_bias': out['m_rel_bias'], 'm_conv_w': out['m_conv_w'], 'm_conv_b': out['m_conv_b'], 'm_conv_ln_g': out['m_conv_ln_g'], 'm_conv_ln_b': out['m_conv_ln_b'], 'm_w_br_pool': out['m_w_br_pool'], 'm_w_br_attn': out['m_w_br_attn'], 'm_w_br_conv': out['m_w_br_conv'], 'm_w_o': out['m_w_o'], 'm_ln_mix_g': out['m_ln_mix_g'], 'm_ln_mix_b': out['m_ln_mix_b'], 'm_w_ff1': out['m_w_ff1'], 'm_b_ff1': out['m_b_ff1'], 'm_w_ff2': out['m_w_ff2'], 'm_b_ff2': out['m_b_ff2'], 'm_ln_ff_g': out['m_ln_ff_g'], 'm_ln_ff_b': out['m_ln_ff_b'], 'v_w_ada': out['v_w_ada'], 'v_b_ada': out['v_b_ada'], 'v_w_in': out['v_w_in'], 'v_b_gate': out['v_b_gate'], 'v_w_pool': out['v_w_pool'], 'v_pool_scale': out['v_pool_scale'], 'v_rel_bias': out['v_rel_bias'], 'v_conv_w': out['v_conv_w'], 'v_conv_b': out['v_conv_b'], 'v_conv_ln_g': out['v_conv_ln_g'], 'v_conv_ln_b': out['v_conv_ln_b'], 'v_w_br_pool': out['v_w_br_pool'], 'v_w_br_attn': out['v_w_br_attn'], 'v_w_br_conv': out['v_w_br_conv'], 'v_w_o': out['v_w_o'], 'v_ln_mix_g': out['v_ln_mix_g'], 'v_ln_mix_b': out['v_ln_mix_b'], 'v_w_ff1': out['v_w_ff1'], 'v_b_ff1': out['v_b_ff1'], 'v_w_ff2': out['v_w_ff2'], 'v_b_ff2': out['v_b_ff2'], 'v_ln_ff_g': out['v_ln_ff_g'], 'v_ln_ff_b': out['v_ln_ff_b']}


def _loss(weights, diff, rest, loss_target):
    with _jax.named_scope("forward"):
        args = {**rest, TWIN_DIFF_INPUT: diff, **{k: w.astype(_WEIGHT_DTYPES[k]) for k, w in weights.items()}}
        y = _forward(args)
    with _jax.named_scope("loss_head"):
        err = _jnp.square(y.astype(_jnp.float32) - loss_target)
        return 0.5 * _jnp.sum(_jnp.mean(err, axis=-1)) if err.ndim else 0.5 * err


def _adamw(w, g, m, v):
    m = ADAM_B1 * m + (1.0 - ADAM_B1) * g
    v = ADAM_B2 * v + (1.0 - ADAM_B2) * _jnp.square(g)
    m_hat = m / (1.0 - ADAM_B1 ** ADAM_STEP)
    v_hat = v / (1.0 - ADAM_B2 ** ADAM_STEP)
    delta = -ADAM_LR * (m_hat / (_jnp.sqrt(v_hat) + ADAM_EPS) + ADAM_WD * w)
    return delta, m, v


def reference(x, c, w_ada, b_ada, w_in, b_gate, w_pool, pool_scale, rel_bias, conv_w, conv_b, conv_ln_g, conv_ln_b, w_br_pool, w_br_attn, w_br_conv, w_o, ln_mix_g, ln_mix_b, w_ff1, b_ff1, w_ff2, b_ff2, ln_ff_g, ln_ff_b, loss_target, m_w_ada, m_b_ada, m_w_in, m_b_gate, m_w_pool, m_pool_scale, m_rel_bias, m_conv_w, m_conv_b, m_conv_ln_g, m_conv_ln_b, m_w_br_pool, m_w_br_attn, m_w_br_conv, m_w_o, m_ln_mix_g, m_ln_mix_b, m_w_ff1, m_b_ff1, m_w_ff2, m_b_ff2, m_ln_ff_g, m_ln_ff_b, v_w_ada, v_b_ada, v_w_in, v_b_gate, v_w_pool, v_pool_scale, v_rel_bias, v_conv_w, v_conv_b, v_conv_ln_g, v_conv_ln_b, v_w_br_pool, v_w_br_attn, v_w_br_conv, v_w_o, v_ln_mix_g, v_ln_mix_b, v_w_ff1, v_b_ff1, v_w_ff2, v_b_ff2, v_ln_ff_g, v_ln_ff_b):
    given = dict(x=x, c=c, w_ada=w_ada, b_ada=b_ada, w_in=w_in, b_gate=b_gate, w_pool=w_pool, pool_scale=pool_scale, rel_bias=rel_bias, conv_w=conv_w, conv_b=conv_b, conv_ln_g=conv_ln_g, conv_ln_b=conv_ln_b, w_br_pool=w_br_pool, w_br_attn=w_br_attn, w_br_conv=w_br_conv, w_o=w_o, ln_mix_g=ln_mix_g, ln_mix_b=ln_mix_b, w_ff1=w_ff1, b_ff1=b_ff1, w_ff2=w_ff2, b_ff2=b_ff2, ln_ff_g=ln_ff_g, ln_ff_b=ln_ff_b, loss_target=loss_target, m_w_ada=m_w_ada, m_b_ada=m_b_ada, m_w_in=m_w_in, m_b_gate=m_b_gate, m_w_pool=m_w_pool, m_pool_scale=m_pool_scale, m_rel_bias=m_rel_bias, m_conv_w=m_conv_w, m_conv_b=m_conv_b, m_conv_ln_g=m_conv_ln_g, m_conv_ln_b=m_conv_ln_b, m_w_br_pool=m_w_br_pool, m_w_br_attn=m_w_br_attn, m_w_br_conv=m_w_br_conv, m_w_o=m_w_o, m_ln_mix_g=m_ln_mix_g, m_ln_mix_b=m_ln_mix_b, m_w_ff1=m_w_ff1, m_b_ff1=m_b_ff1, m_w_ff2=m_w_ff2, m_b_ff2=m_b_ff2, m_ln_ff_g=m_ln_ff_g, m_ln_ff_b=m_ln_ff_b, v_w_ada=v_w_ada, v_b_ada=v_b_ada, v_w_in=v_w_in, v_b_gate=v_b_gate, v_w_pool=v_w_pool, v_pool_scale=v_pool_scale, v_rel_bias=v_rel_bias, v_conv_w=v_conv_w, v_conv_b=v_conv_b, v_conv_ln_g=v_conv_ln_g, v_conv_ln_b=v_conv_ln_b, v_w_br_pool=v_w_br_pool, v_w_br_attn=v_w_br_attn, v_w_br_conv=v_w_br_conv, v_w_o=v_w_o, v_ln_mix_g=v_ln_mix_g, v_ln_mix_b=v_ln_mix_b, v_w_ff1=v_w_ff1, v_b_ff1=v_b_ff1, v_w_ff2=v_w_ff2, v_b_ff2=v_b_ff2, v_ln_ff_g=v_ln_ff_g, v_ln_ff_b=v_ln_ff_b)
    weights = {n: given[n] for n in TWIN_WEIGHTS}
    shared = {n: given[n] for n in SHARED_INPUTS}
    per_example = {n: given[n] for n in ['x', 'c']}
    grad_fn = _jax.value_and_grad(_loss, argnums=(0, 1))

    def one_microbatch(ex, loss_target):
        ex = dict(ex)
        diff = ex.pop(TWIN_DIFF_INPUT)
        return grad_fn(weights, diff, {**shared, **ex}, loss_target)

    if N_MICROBATCH == 1:
        loss, (grad_w, grad_x) = one_microbatch(per_example, given["loss_target"])
    else:
        def body(carry, xs):
            loss_sum, grad_sum = carry
            l_k, (gw_k, gx_k) = one_microbatch(xs[0], xs[1])
            with _jax.named_scope("update"):
                return (loss_sum + l_k, _jax.tree.map(_jnp.add, grad_sum, gw_k)), gx_k

        init = (_jnp.zeros((), _jnp.float32), _jax.tree.map(_jnp.zeros_like, weights))
        (loss, grad_w), grad_x = _jax.lax.scan(body, init, (per_example, given["loss_target"]))
    with _jax.named_scope("update"):
        delta_w, new_m, new_v = {}, {}, {}
        for n in TWIN_WEIGHTS:
            delta_w[n], new_m[n], new_v[n] = _adamw(weights[n], grad_w[n], given["m_" + n], given["v_" + n])
    return (loss, grad_x, *[grad_w[n] for n in TWIN_WEIGHTS], *[delta_w[n] for n in TWIN_WEIGHTS],
            *[new_m[n] for n in TWIN_WEIGHTS], *[new_v[n] for n in TWIN_WEIGHTS])
```

```python
import functools

import jax
import jax.numpy as jnp
import numpy as np
from jax import lax
from jax.experimental import pallas as pl
from jax.experimental.pallas import tpu as pltpu

F32 = jnp.float32
BF16 = jnp.bfloat16
MESH = pl.DeviceIdType.MESH

D_MODEL = 1024
DEPTH = 2
CHUNK = 64
N_HEADS = 8
HEAD_DIM = 64
D_POOL = 256
D_ATTN = 512
D_CONV = 256
CONV_WIDTH = 31
D_FF = 4096
D_IN = 5376
N_PREV = 8
BAND = (N_PREV + 1) * CHUNK
REL_CLIP = 128
ALPHA = (2.0 * DEPTH) ** 0.25
LN_EPS = 1e-5
NEG_INF = -1e30
N_DEV = 8

ADAM_LR, ADAM_B1, ADAM_B2, ADAM_EPS, ADAM_WD, ADAM_STEP = 0.001, 0.9, 0.999, 1e-08, 0.01, 10

VMEM_LIMIT = 56 * 1024 * 1024

Z_GATE, Z_CONV, Z_POOL, Z_Q, Z_K, Z_V = 0, 3072, 3584, 3840, 4352, 4864
ATT_TILE = 512
LANE = 128


def _dg(a, b, ca, cb):
    return lax.dot_general(a.astype(BF16), b.astype(BF16), (((ca,), (cb,)), ((), ())),
                           preferred_element_type=F32)


@jax.custom_vjp
def mm_nn(a, b):
    return _dg(a, b, 1, 0)


def _mm_nn_fwd(a, b):
    return _dg(a, b, 1, 0), (a, b)


def _mm_nn_bwd(res, g):
    a, b = res
    return _dg(g, b, 1, 1).astype(a.dtype), _dg(a, g, 0, 0).astype(b.dtype)


mm_nn.defvjp(_mm_nn_fwd, _mm_nn_bwd)


@jax.custom_vjp
def mm_nt(a, b):
    return _dg(a, b, 1, 1)


def _mm_nt_fwd(a, b):
    return _dg(a, b, 1, 1), (a, b)


def _mm_nt_bwd(res, g):
    a, b = res
    return _dg(g, b, 1, 0).astype(a.dtype), _dg(g, a, 0, 0).astype(b.dtype)


mm_nt.defvjp(_mm_nt_fwd, _mm_nt_bwd)


def _ln(x):
    mu = jnp.mean(x, axis=-1, keepdims=True)
    xc = x - mu
    var = jnp.mean(xc * xc, axis=-1, keepdims=True)
    return xc * lax.rsqrt(var + LN_EPS)


def _norm_rows(rows):
    return [r if isinstance(r, tuple) else (r, r.shape[1], 0) for r in rows]


def _row_spec(tm, r):
    _, width, cb = r
    return pl.BlockSpec((tm, width), lambda i, cb=cb: (i, cb))


def _full_spec(a):
    nd = a.ndim
    return pl.BlockSpec(a.shape, lambda i, nd=nd: (0,) * nd)


def row_fwd(name, f, rows, params, outs, tm):
    rows = _norm_rows(rows)
    s = rows[0][0].shape[0]
    nr, npar = len(rows), len(params)

    def body(*refs):
        r = [x[...].astype(F32) for x in refs[:nr]]
        p = [x[...] for x in refs[nr:nr + npar]]
        res = f(*r, *p)
        for o_ref, o in zip(refs[nr + npar:], res):
            o_ref[...] = o.astype(o_ref.dtype)

    return pl.pallas_call(
        body, name=name, grid=(s // tm,),
        in_specs=[_row_spec(tm, r) for r in rows] + [_full_spec(p) for p in params],
        out_specs=[pl.BlockSpec((tm, w), lambda i: (i, 0)) for w, _ in outs],
        out_shape=[jax.ShapeDtypeStruct((s, w), dt) for w, dt in outs],
        compiler_params=pltpu.CompilerParams(dimension_semantics=("parallel",), vmem_limit_bytes=VMEM_LIMIT),
    )(*[r[0] for r in rows], *params)


def row_bwd(name, f, rows, params, douts, tm, want_rows, want_params, add_to=None):
    rows = _norm_rows(rows)
    s = rows[0][0].shape[0]
    nr, npar, nd = len(rows), len(params), len(douts)
    nadd = 0 if add_to is None else 1
    n_in = nr + npar + nd + nadd

    def body(*refs):
        i = pl.program_id(0)
        r = [x[...].astype(F32) for x in refs[:nr]]
        p = [x[...].astype(F32) for x in refs[nr:nr + npar]]
        d = [x[...].astype(F32) for x in refs[nr + npar:nr + npar + nd]]
        _, vjp = jax.vjp(f, *r, *p)
        g = vjp(tuple(d))
        out_refs = refs[n_in:]
        for k, (idx, _) in enumerate(want_rows):
            val = g[idx]
            if nadd and k == 0:
                val = val + refs[n_in - 1][...].astype(F32)
            out_refs[k][...] = val.astype(out_refs[k].dtype)
        for k, idx in enumerate(want_params):
            gp = g[nr + idx]
            o_ref = out_refs[len(want_rows) + k]

            @pl.when(i == 0)
            def _():
                o_ref[...] = gp

            @pl.when(i > 0)
            def _():
                o_ref[...] += gp

    in_specs = ([_row_spec(tm, r) for r in rows] + [_full_spec(p) for p in params]
                + [pl.BlockSpec((tm, d.shape[1]), lambda i: (i, 0)) for d in douts])
    args = [r[0] for r in rows] + list(params) + list(douts)
    if nadd:
        in_specs.append(pl.BlockSpec((tm, add_to.shape[1]), lambda i: (i, 0)))
        args.append(add_to)
    out_specs = ([pl.BlockSpec((tm, rows[idx][1]), lambda i: (i, 0)) for idx, _ in want_rows]
                 + [_full_spec(params[idx]) for idx in want_params])
    out_shape = ([jax.ShapeDtypeStruct((s, rows[idx][1]), dt) for idx, dt in want_rows]
                 + [jax.ShapeDtypeStruct(params[idx].shape, F32) for idx in want_params])
    res = pl.pallas_call(
        body, name=name, grid=(s // tm,), in_specs=in_specs, out_specs=out_specs, out_shape=out_shape,
        compiler_params=pltpu.CompilerParams(dimension_semantics=("arbitrary",), vmem_limit_bytes=VMEM_LIMIT),
    )(*args)
    return res[:len(want_rows)], res[len(want_rows):]


def f_lnmod(x, sc, sh):
    return (_ln(x) * (1.0 + sc) + sh,)


def f_merge(p, ao, cv, zg, x, wbd, ps, wbp, wba, wbc, cb, clg, clb, bg, wo, gm, lg, lb):
    pm = mm_nn(p, wbd) * ps
    co = jax.nn.silu(_ln(cv + cb) * clg + clb)
    y_pool = mm_nn(pm, wbp)
    y_attn = mm_nn(ao, wba)
    y_conv = mm_nn(co, wbc)
    gates = jax.nn.sigmoid(zg + bg)
    merged = (gates[:, :D_MODEL] * y_pool + gates[:, D_MODEL:2 * D_MODEL] * y_attn
              + gates[:, 2 * D_MODEL:] * y_conv)
    mix = mm_nn(merged, wo)
    return (_ln(ALPHA * x + gm * mix) * lg + lb,)


def f_relu2(hpre, b1):
    a = jax.nn.relu(hpre + b1)
    return (a * a,)


def f_ffout(x1, ff, b2, gf, lg, lb):
    return (_ln(ALPHA * x1 + gf * (ff + b2)) * lg + lb,)


def mm_big(name, a, b, kind, tiles, out_dtype):
    t0, t1, tr = tiles
    if kind == "nn":
        (m, k), n = a.shape, b.shape[1]
        o0, o1, red = m, n, k
        a_spec = pl.BlockSpec((t0, tr), lambda i, j, r: (i, r))
        b_spec = pl.BlockSpec((tr, t1), lambda i, j, r: (r, j))
        dims = (1, 0)
    elif kind == "nt":
        (m, n), k = a.shape, b.shape[0]
        o0, o1, red = m, k, n
        a_spec = pl.BlockSpec((t0, tr), lambda i, j, r: (i, r))
        b_spec = pl.BlockSpec((t1, tr), lambda i, j, r: (j, r))
        dims = (1, 1)
    else:
        (m, k), n = a.shape, b.shape[1]
        o0, o1, red = k, n, m
        a_spec = pl.BlockSpec((tr, t0), lambda i, j, r: (r, i))
        b_spec = pl.BlockSpec((tr, t1), lambda i, j, r: (r, j))
        dims = (0, 0)
    assert o0 % t0 == 0 and o1 % t1 == 0 and red % tr == 0, (name, a.shape, b.shape, tiles)
    nred = red // tr

    def body(a_ref, b_ref, o_ref, acc_ref):
        r = pl.program_id(2)
        part = _dg(a_ref[...], b_ref[...], *dims)

        @pl.when(r == 0)
        def _():
            acc_ref[...] = part

        @pl.when(r > 0)
        def _():
            acc_ref[...] += part

        @pl.when(r == nred - 1)
        def _():
            o_ref[...] = acc_ref[...].astype(o_ref.dtype)

    return pl.pallas_call(
        body, name=name, grid=(o0 // t0, o1 // t1, nred),
        in_specs=[a_spec, b_spec],
        out_specs=pl.BlockSpec((t0, t1), lambda i, j, r: (i, j)),
        out_shape=jax.ShapeDtypeStruct((o0, o1), out_dtype),
        scratch_shapes=[pltpu.VMEM((t0, t1), F32)],
        compiler_params=pltpu.CompilerParams(dimension_semantics=("parallel", "parallel", "arbitrary"),
                                             vmem_limit_bytes=VMEM_LIMIT),
    )(a, b)


POOL_PAD = 16
POOL_ROWS = 256


def pool_lin(name, x, wvec, transpose, out_dtype):
    arr, width, cb = x
    s = arr.shape[0]
    n_steps = s // POOL_ROWS

    def body(x_ref, w_ref, o_ref, xp_ref):
        wv = w_ref[...]
        zeros = jnp.zeros((POOL_PAD, width), F32)
        xp_ref[0:POOL_PAD, :] = zeros
        xp_ref[s + POOL_PAD:s + 2 * POOL_PAD, :] = zeros

        def count(t0):
            t = lax.broadcasted_iota(jnp.int32, (POOL_ROWS, width), 0) + (t0 + 1)
            return jnp.minimum(t.astype(F32), wv)

        def fill(i, carry):
            t0 = pl.multiple_of(i * POOL_ROWS, POOL_ROWS)
            v = x_ref[pl.ds(t0, POOL_ROWS), :].astype(F32)
            if transpose:
                v = v / count(t0)
            xp_ref[pl.ds(t0 + POOL_PAD, POOL_ROWS), :] = v
            return carry

        lax.fori_loop(0, n_steps, fill, 0)

        def step(i, carry):
            t0 = pl.multiple_of(i * POOL_ROWS, POOL_ROWS)
            win = xp_ref[pl.ds(t0, POOL_ROWS + 2 * POOL_PAD), :]
            acc = jnp.zeros((POOL_ROWS, width), F32)
            for j in range(POOL_PAD):
                off = POOL_PAD + j if transpose else POOL_PAD - j
                acc = acc + jnp.where(wv > j, win[off:off + POOL_ROWS, :], 0.0)
            cur = x_ref[pl.ds(t0, POOL_ROWS), :].astype(F32)
            res = acc - cur if transpose else acc / count(t0) - cur
            o_ref[pl.ds(t0, POOL_ROWS), :] = res.astype(o_ref.dtype)
            return carry

        lax.fori_loop(0, n_steps, step, 0)

    return pl.pallas_call(
        body, name=name, grid=(1,),
        in_specs=[pl.BlockSpec((s, width), lambda i, cb=cb: (0, cb)), pl.BlockSpec((1, width), lambda i: (0, 0))],
        out_specs=pl.BlockSpec((s, width), lambda i: (0, 0)),
        out_shape=jax.ShapeDtypeStruct((s, width), out_dtype),
        scratch_shapes=[pltpu.VMEM((s + 2 * POOL_PAD, width), F32)],
        compiler_params=pltpu.CompilerParams(dimension_semantics=("arbitrary",), vmem_limit_bytes=VMEM_LIMIT),
    )(arr, wvec)


CONV_PAD = 32
CONV_ROWS = 128


def _glu(a, g):
    return a * jax.nn.sigmoid(g)


def conv_fwd(name, zc, w):
    arr, width, cb = zc
    s = arr.shape[0]
    n_steps = s // CONV_ROWS
    lead = CONV_PAD - (CONV_WIDTH - 1)

    def body(z_ref, w_ref, o_ref, hp_ref):
        hp_ref[0:CONV_PAD, :] = jnp.zeros((CONV_PAD, D_CONV), F32)

        def fill(i, carry):
            t0 = pl.multiple_of(i * CONV_ROWS, CONV_ROWS)
            z = z_ref[pl.ds(t0, CONV_ROWS), :]
            hp_ref[pl.ds(t0 + CONV_PAD, CONV_ROWS), :] = _glu(z[:, :D_CONV], z[:, D_CONV:])
            return carry

        lax.fori_loop(0, n_steps, fill, 0)
        wv = w_ref[...]

        def step(i, carry):
            t0 = pl.multiple_of(i * CONV_ROWS, CONV_ROWS)
            win = hp_ref[pl.ds(t0, CONV_ROWS + CONV_PAD), :]
            acc = jnp.zeros((CONV_ROWS, D_CONV), F32)
            for k in range(CONV_WIDTH):
                acc = acc + wv[k:k + 1, :] * win[lead + k:lead + k + CONV_ROWS, :]
            o_ref[pl.ds(t0, CONV_ROWS), :] = acc
            return carry

        lax.fori_loop(0, n_steps, step, 0)

    return pl.pallas_call(
        body, name=name, grid=(1,),
        in_specs=[pl.BlockSpec((s, width), lambda i, cb=cb: (0, cb)), pl.BlockSpec(w.shape, lambda i: (0, 0))],
        out_specs=pl.BlockSpec((s, D_CONV), lambda i: (0, 0)),
        out_shape=jax.ShapeDtypeStruct((s, D_CONV), F32),
        scratch_shapes=[pltpu.VMEM((s + CONV_PAD, D_CONV), F32)],
        compiler_params=pltpu.CompilerParams(dimension_semantics=("arbitrary",), vmem_limit_bytes=VMEM_LIMIT),
    )(arr, w)


def conv_bwd(name, zc, w, dout):
    arr, width, cb = zc
    s = arr.shape[0]
    n_steps = s // CONV_ROWS
    lead = CONV_PAD - (CONV_WIDTH - 1)

    def body(z_ref, w_ref, d_ref, dz_ref, dw_ref, hp_ref, dp_ref):
        hp_ref[0:CONV_PAD, :] = jnp.zeros((CONV_PAD, D_CONV), F32)
        dp_ref[s:s + CONV_PAD, :] = jnp.zeros((CONV_PAD, D_CONV), F32)
        dw_ref[...] = jnp.zeros(dw_ref.shape, F32)

        def fill(i, carry):
            t0 = pl.multiple_of(i * CONV_ROWS, CONV_ROWS)
            z = z_ref[pl.ds(t0, CONV_ROWS), :]
            hp_ref[pl.ds(t0 + CONV_PAD, CONV_ROWS), :] = _glu(z[:, :D_CONV], z[:, D_CONV:])
            dp_ref[pl.ds(t0, CONV_ROWS), :] = d_ref[pl.ds(t0, CONV_ROWS), :]
            return carry

        lax.fori_loop(0, n_steps, fill, 0)
        wv = w_ref[...]

        def step(i, carry):
            t0 = pl.multiple_of(i * CONV_ROWS, CONV_ROWS)
            hwin = hp_ref[pl.ds(t0, CONV_ROWS + CONV_PAD), :]
            dwin = dp_ref[pl.ds(t0, CONV_ROWS + CONV_PAD), :]
            dcur = dwin[0:CONV_ROWS, :]
            dh = jnp.zeros((CONV_ROWS, D_CONV), F32)
            rows = []
            for k in range(CONV_WIDTH):
                rows.append(jnp.sum(dcur * hwin[lead + k:lead + k + CONV_ROWS, :], axis=0, keepdims=True))
                back = CONV_WIDTH - 1 - k
                dh = dh + wv[k:k + 1, :] * dwin[back:back + CONV_ROWS, :]
            rows.append(jnp.zeros((1, D_CONV), F32))
            dw_ref[...] += jnp.concatenate(rows, axis=0)
            z = z_ref[pl.ds(t0, CONV_ROWS), :]
            _, vjp = jax.vjp(_glu, z[:, :D_CONV], z[:, D_CONV:])
            da, dg = vjp(dh)
            dz_ref[pl.ds(t0, CONV_ROWS), :] = jnp.concatenate([da, dg], axis=1).astype(dz_ref.dtype)
            return carry

        lax.fori_loop(0, n_steps, step, 0)

    return pl.pallas_call(
        body, name=name, grid=(1,),
        in_specs=[pl.BlockSpec((s, width), lambda i, cb=cb: (0, cb)), pl.BlockSpec(w.shape, lambda i: (0, 0)),
                  pl.BlockSpec((s, D_CONV), lambda i: (0, 0))],
        out_specs=[pl.BlockSpec((s, width), lambda i: (0, 0)), pl.BlockSpec(w.shape, lambda i: (0, 0))],
        out_shape=[jax.ShapeDtypeStruct((s, width), BF16), jax.ShapeDtypeStruct(w.shape, F32)],
        scratch_shapes=[pltpu.VMEM((s + CONV_PAD, D_CONV), F32), pltpu.VMEM((s + CONV_PAD, D_CONV), F32)],
        compiler_params=pltpu.CompilerParams(dimension_semantics=("arbitrary",), vmem_limit_bytes=VMEM_LIMIT),
    )(arr, w, dout)


HEADS_PER_STEP = LANE // HEAD_DIM
CHUNKS_PER_TILE = ATT_TILE // CHUNK


def _attn_tile(q, kp, kc, vp, vc, bias, first):
    kcat = jnp.concatenate([kp, kc], axis=0)
    vcat = jnp.concatenate([vp, vc], axis=0)
    lane = lax.broadcasted_iota(jnp.int32, (1, LANE), 1)
    kj = lax.broadcasted_iota(jnp.int32, (1, BAND), 1)
    outs = []
    for i in range(CHUNKS_PER_TILE):
        qi = q[i * CHUNK:(i + 1) * CHUNK, :] * (HEAD_DIM ** -0.5)
        kb = kcat[i * CHUNK:i * CHUNK + BAND, :]
        vb = vcat[i * CHUNK:i * CHUNK + BAND, :]
        valid = jnp.logical_or(jnp.logical_not(first), kj >= ATT_TILE - i * CHUNK)
        o = jnp.zeros((CHUNK, LANE), F32)
        for h in range(HEADS_PER_STEP):
            in_head = jnp.logical_and(lane >= h * HEAD_DIM, lane < (h + 1) * HEAD_DIM)
            sc = mm_nt(jnp.where(in_head, qi, 0.0), kb) + bias[h]
            sc = jnp.where(valid, sc, NEG_INF)
            m = jnp.max(sc, axis=-1, keepdims=True)
            e = jnp.exp(sc - lax.stop_gradient(m))
            p = e / jnp.sum(e, axis=-1, keepdims=True)
            o = o + jnp.where(in_head, mm_nn(p, vb), 0.0)
        outs.append(o)
    return jnp.concatenate(outs, axis=0)


def _z_spec(col0, fn):
    return pl.BlockSpec((ATT_TILE, LANE), lambda hp, n, col0=col0, fn=fn: (fn(n), col0 // LANE + hp))


def attn_fwd(name, z, bias):
    s = z.shape[0]
    nt = s // ATT_TILE

    def body(q_ref, kp_ref, kc_ref, vp_ref, vc_ref, b_ref, o_ref):
        first = pl.program_id(1) == 0
        o = _attn_tile(q_ref[...], kp_ref[...], kc_ref[...], vp_ref[...], vc_ref[...], b_ref[...], first)
        o_ref[...] = o.astype(o_ref.dtype)

    cur = lambda n: n
    prev = lambda n: jnp.maximum(n - 1, 0)
    return pl.pallas_call(
        body, name=name, grid=(N_HEADS // HEADS_PER_STEP, nt),
        in_specs=[_z_spec(Z_Q, cur), _z_spec(Z_K, prev), _z_spec(Z_K, cur), _z_spec(Z_V, prev), _z_spec(Z_V, cur),
                  pl.BlockSpec((HEADS_PER_STEP, CHUNK, BAND), lambda hp, n: (hp, 0, 0))],
        out_specs=pl.BlockSpec((ATT_TILE, LANE), lambda hp, n: (n, hp)),
        out_shape=jax.ShapeDtypeStruct((s, D_ATTN), BF16),
        compiler_params=pltpu.CompilerParams(dimension_semantics=("parallel", "parallel"), vmem_limit_bytes=VMEM_LIMIT),
    )(z, z, z, z, z, bias)


def attn_bwd(name, z, bias, do):
    s = z.shape[0]
    nt = s // ATT_TILE

    def body(q_ref, kp_ref, kc_ref, vp_ref, vc_ref, b_ref, do_ref, dq_ref, dk_ref, dv_ref, db_ref, kacc, vacc):
        n = pl.program_id(1)

        @pl.when(n == 0)
        def _():
            db_ref[...] = jnp.zeros(db_ref.shape, F32)

        @pl.when(n < nt)
        def _():
            first = n == 0
            fn = functools.partial(_attn_tile, first=first)
            _, vjp = jax.vjp(fn, q_ref[...], kp_ref[...], kc_ref[...], vp_ref[...], vc_ref[...], b_ref[...])
            dq, dkp, dkc, dvp, dvc, db = vjp(do_ref[...].astype(F32))
            dq_ref[...] = dq.astype(dq_ref.dtype)
            db_ref[...] += db

            @pl.when(n > 0)
            def _():
                dk_ref[...] = (kacc[...] + dkp).astype(dk_ref.dtype)
                dv_ref[...] = (vacc[...] + dvp).astype(dv_ref.dtype)

            kacc[...] = dkc
            vacc[...] = dvc

        @pl.when(n == nt)
        def _():
            dk_ref[...] = kacc[...].astype(dk_ref.dtype)
            dv_ref[...] = vacc[...].astype(dv_ref.dtype)

    cur = lambda n: jnp.minimum(n, nt - 1)
    prev = lambda n: jnp.clip(n - 1, 0, nt - 1)
    o_cur = pl.BlockSpec((ATT_TILE, LANE), lambda hp, n: (jnp.minimum(n, nt - 1), hp))
    o_prev = pl.BlockSpec((ATT_TILE, LANE), lambda hp, n: (jnp.maximum(n - 1, 0), hp))
    return pl.pallas_call(
        body, name=name, grid=(N_HEADS // HEADS_PER_STEP, nt + 1),
        in_specs=[_z_spec(Z_Q, cur), _z_spec(Z_K, prev), _z_spec(Z_K, cur), _z_spec(Z_V, prev), _z_spec(Z_V, cur),
                  pl.BlockSpec((HEADS_PER_STEP, CHUNK, BAND), lambda hp, n: (hp, 0, 0)), o_cur],
        out_specs=[o_cur, o_prev, o_prev, pl.BlockSpec((HEADS_PER_STEP, CHUNK, BAND), lambda hp, n: (hp, 0, 0))],
        out_shape=[jax.ShapeDtypeStruct((s, D_ATTN), BF16)] * 3 + [jax.ShapeDtypeStruct((N_HEADS, CHUNK, BAND), F32)],
        scratch_shapes=[pltpu.VMEM((ATT_TILE, LANE), F32), pltpu.VMEM((ATT_TILE, LANE), F32)],
        compiler_params=pltpu.CompilerParams(dimension_semantics=("parallel", "arbitrary"), vmem_limit_bytes=VMEM_LIMIT),
    )(z, z, z, z, z, bias, do)


def loss_head(name, y, tgt, tm):
    s, d = y.shape

    def body(y_ref, t_ref, l_ref, dy_ref):
        i = pl.program_id(0)
        diff = y_ref[...] - t_ref[...]
        dy_ref[...] = diff * (1.0 / d)
        part = 0.5 * jnp.sum(jnp.mean(diff * diff, axis=-1, keepdims=True), axis=0, keepdims=True)

        @pl.when(i == 0)
        def _():
            l_ref[...] = jnp.zeros(l_ref.shape, F32)

        l_ref[...] += jnp.broadcast_to(part, l_ref.shape)

    row = pl.BlockSpec((tm, d), lambda i: (i, 0))
    return pl.pallas_call(
        body, name=name, grid=(s // tm,), in_specs=[row, row],
        out_specs=[pl.BlockSpec((8, LANE), lambda i: (0, 0)), row],
        out_shape=[jax.ShapeDtypeStruct((8, LANE), F32), jax.ShapeDtypeStruct((s, d), F32)],
        compiler_params=pltpu.CompilerParams(dimension_semantics=("arbitrary",), vmem_limit_bytes=VMEM_LIMIT),
    )(y, tgt)


def adamw(name, w, g, m, v, tr):
    r, c = w.shape
    assert r % tr == 0, (name, w.shape, tr)

    def body(w_ref, g_ref, m_ref, v_ref, d_ref, nm_ref, nv_ref):
        gg = g_ref[...]
        m2 = ADAM_B1 * m_ref[...] + (1.0 - ADAM_B1) * gg
        v2 = ADAM_B2 * v_ref[...] + (1.0 - ADAM_B2) * (gg * gg)
        m_hat = m2 / (1.0 - ADAM_B1 ** ADAM_STEP)
        v_hat = v2 / (1.0 - ADAM_B2 ** ADAM_STEP)
        d_ref[...] = -ADAM_LR * (m_hat / (jnp.sqrt(v_hat) + ADAM_EPS) + ADAM_WD * w_ref[...])
        nm_ref[...] = m2
        nv_ref[...] = v2

    blk = pl.BlockSpec((tr, c), lambda i: (i, 0))
    return pl.pallas_call(
        body, name=name, grid=(r // tr,), in_specs=[blk] * 4, out_specs=[blk] * 3,
        out_shape=[jax.ShapeDtypeStruct((r, c), F32)] * 3,
        compiler_params=pltpu.CompilerParams(dimension_semantics=("parallel",), vmem_limit_bytes=VMEM_LIMIT),
    )(w, g, m, v)


def sum_blocks(name, blocks, tr):
    nb, r, c = blocks.shape
    assert r % tr == 0, (name, blocks.shape, tr)

    def body(b_ref, o_ref):
        acc = b_ref[0].astype(F32)
        for j in range(1, nb):
            acc = acc + b_ref[j].astype(F32)
        o_ref[...] = acc

    return pl.pallas_call(
        body, name=name, grid=(r // tr,),
        in_specs=[pl.BlockSpec((nb, tr, c), lambda i: (0, i, 0))],
        out_specs=pl.BlockSpec((tr, c), lambda i: (i, 0)),
        out_shape=jax.ShapeDtypeStruct((r, c), F32),
        compiler_params=pltpu.CompilerParams(dimension_semantics=("parallel",), vmem_limit_bytes=VMEM_LIMIT),
    )(blocks)


FLIPS = [(0, 0, 1), (1, 0, 0), (0, 1, 0), (1, 1, 0), (1, 0, 1), (0, 1, 1), (1, 1, 1)]
ANY = pl.BlockSpec(memory_space=pl.ANY)


def _me():
    return lax.axis_index("x"), lax.axis_index("y"), lax.axis_index("c")


def _flip(pos, f):
    return tuple((1 - p) if fi else p for p, fi in zip(pos, f))


def _idx(pos):
    return 4 * pos[0] + 2 * pos[1] + pos[2]


def all_gather8(name, shard):
    r, c = shard.shape

    def body(x_ref, out_ref, send_sems, recv_sems, local_sem):
        x, y, cc = _me()
        me, sibling = (x, y, cc), (x, y, 1 - cc)
        chips = [(1 - x, y), (x, 1 - y), (1 - x, 1 - y)]

        def slot(pos):
            return out_ref.at[_idx(pos)]

        def copy(k, block, to, src=None):
            return pltpu.make_async_remote_copy(
                src_ref=slot(block) if src is None else src, dst_ref=slot(block),
                send_sem=send_sems.at[k], recv_sem=recv_sems.at[k], device_id=to, device_id_type=MESH)

        mine = pltpu.make_async_copy(x_ref, slot(me), local_sem)
        mine.start()
        first = [copy(0, me, sibling, src=x_ref)]
        first += [copy(1 + j, me, (*chip, cc), src=x_ref) for j, chip in enumerate(chips)]
        for cp in first:
            cp.start()
        passed = [copy(4 + j, (*chip, cc), sibling) for j, chip in enumerate(chips)]
        for j, chip in enumerate(chips):
            copy(1 + j, (*chip, cc), me).wait_recv()
            passed[j].start()
        copy(0, sibling, me).wait_recv()
        for j, chip in enumerate(chips):
            copy(4 + j, (*chip, 1 - cc), me).wait_recv()
        for cp in first + passed:
            cp.wait_send()
        mine.wait()

    return pl.pallas_call(
        body, name=name, in_specs=[ANY], out_specs=ANY,
        out_shape=jax.ShapeDtypeStruct((N_DEV, r, c), shard.dtype),
        scratch_shapes=[pltpu.SemaphoreType.DMA((7,)), pltpu.SemaphoreType.DMA((7,)), pltpu.SemaphoreType.DMA],
    )(shard)


def all_to_all8(name, blocks):
    nb, r, c = blocks.shape

    def body(in_ref, out_ref, send_sems, recv_sems, local_sem):
        me = _me()
        mine = pltpu.make_async_copy(in_ref.at[_idx(me)], out_ref.at[_idx(me)], local_sem)
        mine.start()
        sends, recvs = [], []
        for k, f in enumerate(FLIPS):
            peer = _flip(me, f)
            sends.append(pltpu.make_async_remote_copy(
                src_ref=in_ref.at[_idx(peer)], dst_ref=out_ref.at[_idx(me)],
                send_sem=send_sems.at[k], recv_sem=recv_sems.at[k], device_id=peer, device_id_type=MESH))
            recvs.append(pltpu.make_async_remote_copy(
                src_ref=in_ref.at[_idx(me)], dst_ref=out_ref.at[_idx(peer)],
                send_sem=send_sems.at[k], recv_sem=recv_sems.at[k], device_id=peer, device_id_type=MESH))
        for cp in sends:
            cp.start()
        for cp in recvs:
            cp.wait_recv()
        for cp in sends:
            cp.wait_send()
        mine.wait()

    return pl.pallas_call(
        body, name=name, in_specs=[ANY], out_specs=ANY,
        out_shape=jax.ShapeDtypeStruct((nb, r, c), blocks.dtype),
        scratch_shapes=[pltpu.SemaphoreType.DMA((7,)), pltpu.SemaphoreType.DMA((7,)), pltpu.SemaphoreType.DMA],
    )(blocks)


def ada_fwd(name, c_row, w_cat, b_lay):
    d = c_row.shape[1]
    ncol = w_cat.shape[1]
    vmem = pl.BlockSpec(memory_space=pltpu.VMEM)

    def body(c_ref, w_ref, b_ref, mod_ref, cact_ref, call, send, land, s1, r1, s2, r2):
        me = _me()
        mi = _idx(me)
        call[mi] = c_ref[...]

        def exchange(src_of, dst_buf, ssem, rsem):
            sends, recvs = [], []
            for k, f in enumerate(FLIPS):
                peer = _flip(me, f)
                sends.append(pltpu.make_async_remote_copy(
                    src_ref=src_of(peer), dst_ref=dst_buf.at[mi], send_sem=ssem.at[k], recv_sem=rsem.at[k],
                    device_id=peer, device_id_type=MESH))
                recvs.append(pltpu.make_async_remote_copy(
                    src_ref=src_of(peer), dst_ref=dst_buf.at[_idx(peer)], send_sem=ssem.at[k], recv_sem=rsem.at[k],
                    device_id=peer, device_id_type=MESH))
            for cp in sends:
                cp.start()
            for cp in recvs:
                cp.wait_recv()
            for cp in sends:
                cp.wait_send()

        exchange(lambda peer: c_ref, call, s1, r1)
        for p in range(N_DEV):
            cact_ref[pl.ds(p, 1), :] = jax.nn.silu(call[p])
        res = _dg(cact_ref[...], w_ref[...], 1, 0)
        for p in range(N_DEV):
            send[p] = res[p:p + 1, :]
        land[mi] = send[mi]
        exchange(lambda peer: send.at[_idx(peer)], land, s2, r2)
        mod_ref[...] = land[...] + b_ref[...]

    return pl.pallas_call(
        body, name=name, in_specs=[vmem, vmem, vmem], out_specs=[vmem, vmem],
        out_shape=[jax.ShapeDtypeStruct((N_DEV, 1, ncol), F32), jax.ShapeDtypeStruct((N_DEV, d), F32)],
        scratch_shapes=[pltpu.VMEM((N_DEV, 1, d), F32), pltpu.VMEM((N_DEV, 1, ncol), F32),
                        pltpu.VMEM((N_DEV, 1, ncol), F32),
                        pltpu.SemaphoreType.DMA((7,)), pltpu.SemaphoreType.DMA((7,)),
                        pltpu.SemaphoreType.DMA((7,)), pltpu.SemaphoreType.DMA((7,))],
        compiler_params=pltpu.CompilerParams(vmem_limit_bytes=VMEM_LIMIT),
    )(c_row, w_cat, b_lay)


POOL_WINDOWS = (2, 4, 8, 16)
POOL_GROUP = 64
N_REL = 2 * REL_CLIP + 1
PACK_COLS = 1024
SMALL_NAMES = ["b_ada", "b_gate", "w_pool", "pool_scale", "rel_bias", "conv_w", "conv_b", "conv_ln_g",
               "conv_ln_b", "ln_mix_g", "ln_mix_b", "b_ff1", "b_ff2", "ln_ff_g", "ln_ff_b"]
BIG_NAMES = ["w_in", "w_br_pool", "w_br_attn", "w_br_conv", "w_o", "w_ff1", "w_ff2"]
ROW_SHARDED = ("w_o", "w_ff2")
WEIGHT_NAMES = ["w_ada", "b_ada", "w_in", "b_gate", "w_pool", "pool_scale", "rel_bias", "conv_w", "conv_b",
                "conv_ln_g", "conv_ln_b", "w_br_pool", "w_br_attn", "w_br_conv", "w_o", "ln_mix_g", "ln_mix_b",
                "w_ff1", "b_ff1", "w_ff2", "b_ff2", "ln_ff_g", "ln_ff_b"]


def _perm_cols(w):
    return jnp.concatenate([w[:, 2304:], w[:, 1792:2304], w[:, :256], w[:, 256:768], w[:, 768:1280],
                            w[:, 1280:1792]], axis=1)


def _unperm_cols(wp):
    return jnp.concatenate([wp[:, Z_POOL:Z_Q], wp[:, Z_Q:Z_K], wp[:, Z_K:Z_V], wp[:, Z_V:],
                            wp[:, Z_CONV:Z_POOL], wp[:, :Z_CONV]], axis=1)


def _bias_table(rel_bias):
    far = jnp.broadcast_to(rel_bias[:, 2 * REL_CLIP:], (N_HEADS, BAND - REL_CLIP))
    near = rel_bias[:, REL_CLIP - CHUNK + 1:2 * REL_CLIP][:, ::-1]
    ext = jnp.concatenate([far, near], axis=1)
    return jnp.stack([ext[:, CHUNK - 1 - qi:CHUNK - 1 - qi + BAND] for qi in range(CHUNK)], axis=1)


def _block_diag(w_pool):
    out = jnp.zeros((D_POOL, D_POOL), F32)
    for g in range(len(POOL_WINDOWS)):
        out = lax.dynamic_update_slice(out, w_pool[g], (g * POOL_GROUP, g * POOL_GROUP))
    return out


def _flat_pad(arrs, mult):
    flat = jnp.concatenate([a.reshape(-1) for a in arrs])
    pad = (-flat.shape[0]) % mult
    return jnp.pad(flat, (0, pad)) if pad else flat


def _unflat(flat, shapes):
    out, off = [], 0
    for shp in shapes:
        n = int(np.prod(shp))
        out.append(flat[off:off + n].reshape(shp))
        off += n
    return out


def _to_blocks(name, full):
    k, n = full.shape
    if name in ROW_SHARDED:
        return full.reshape(N_DEV, k // N_DEV, n)
    return full.reshape(k, N_DEV, n // N_DEV).transpose(1, 0, 2)


def _from_blocks(name, blocks):
    nb, r, c = blocks.shape
    if name in ROW_SHARDED:
        return blocks.reshape(nb * r, c)
    return blocks.transpose(1, 0, 2).reshape(r, nb * c)


class _Layer:
    pass


def _row(v):
    return v.reshape(1, -1)


def _layer_fwd(x, modr, w, wvec):
    sh_m, sc_m, g_m, sh_f, sc_f, g_f = modr
    (u,) = row_fwd("lnmod_mix", f_lnmod, [x], [sc_m, sh_m], [(D_MODEL, BF16)], 512)
    z = mm_big("mm_in", u, w.w_in, "nn", (512, 896, 1024), F32)
    p = pool_lin("pool_fwd", (z, D_POOL, Z_POOL // D_POOL), wvec, False, F32)
    ao = attn_fwd("attn_fwd", z, w.bias)
    cv = conv_fwd("conv_fwd", (z, 2 * D_CONV, Z_CONV // (2 * D_CONV)), w.conv_w)
    mparams = [w.wbd, w.ps, w.wbp, w.wba, w.wbc, w.cb, w.clg, w.clb, w.bg, w.wo, g_m, w.lmg, w.lmb]
    (x1,) = row_fwd("merge", f_merge, [p, ao, cv, (z, 3 * D_MODEL, 0), x], mparams, [(D_MODEL, F32)], 256)
    (u2,) = row_fwd("lnmod_ff", f_lnmod, [x1], [sc_f, sh_f], [(D_MODEL, BF16)], 512)
    hpre = mm_big("mm_ff1", u2, w.w_ff1, "nn", (512, 1024, 1024), F32)
    (h,) = row_fwd("relu2", f_relu2, [hpre], [w.b1], [(D_FF, BF16)], 512)
    ff = mm_big("mm_ff2", h, w.w_ff2, "nn", (512, 1024, 2048), F32)
    (x2,) = row_fwd("ffout", f_ffout, [x1, ff], [w.b2, g_f, w.lfg, w.lfb], [(D_MODEL, F32)], 512)
    return x2, (x, u, z, p, ao, cv, x1, u2, hpre, h, ff, mparams)


def _layer_bwd(dx2, saved, modr, w, wvec):
    x, u, z, p, ao, cv, x1, u2, hpre, h, ff, mparams = saved
    sh_m, sc_m, g_m, sh_f, sc_f, g_f = modr
    g = {}
    (dx1a, dff), (g["b_ff2"], dgf, g["ln_ff_g"], g["ln_ff_b"]) = row_bwd(
        "ffout_bwd", f_ffout, [x1, ff], [w.b2, g_f, w.lfg, w.lfb], [dx2], 512, [(0, F32), (1, BF16)], [0, 1, 2, 3])
    dh = mm_big("mm_dh", dff, w.w_ff2, "nt", (512, 1024, 1024), F32)
    g["w_ff2"] = mm_big("mm_dw_ff2", h, dff, "tn", (1024, 1024, 512), F32)
    (dhpre,), (g["b_ff1"],) = row_bwd("relu2_bwd", f_relu2, [hpre], [w.b1], [dh], 512, [(0, BF16)], [0])
    du2 = mm_big("mm_du2", dhpre, w.w_ff1, "nt", (512, 1024, 2048), F32)
    g["w_ff1"] = mm_big("mm_dw_ff1", u2, dhpre, "tn", (1024, 1024, 512), F32)
    (dx1,), (dscf, dshf) = row_bwd("lnmod_ff_bwd", f_lnmod, [x1], [sc_f, sh_f], [du2], 512, [(0, F32)], [0, 1],
                                   add_to=dx1a)
    (dp, dao, dcv, dzg, dxa), dm = row_bwd(
        "merge_bwd", f_merge, [p, ao, cv, (z, 3 * D_MODEL, 0), x], mparams, [dx1], 256,
        [(0, F32), (1, BF16), (2, F32), (3, BF16), (4, F32)], list(range(13)))
    (dwbd, g["pool_scale"], g["w_br_pool"], g["w_br_attn"], g["w_br_conv"], g["conv_b"], g["conv_ln_g"],
     g["conv_ln_b"], g["b_gate"], g["w_o"], dgm, g["ln_mix_g"], g["ln_mix_b"]) = dm
    g["w_pool"] = jnp.stack([dwbd[i * POOL_GROUP:(i + 1) * POOL_GROUP, i * POOL_GROUP:(i + 1) * POOL_GROUP]
                             for i in range(len(POOL_WINDOWS))])
    dzp = pool_lin("pool_bwd", (dp, D_POOL, 0), wvec, True, BF16)
    dq, dk, dv, dbias = attn_bwd("attn_bwd", z, w.bias, dao)
    (g["rel_bias"],) = w.bias_vjp(dbias)
    dzc, dcw = conv_bwd("conv_bwd", (z, 2 * D_CONV, Z_CONV // (2 * D_CONV)), w.conv_w, dcv)
    g["conv_w"] = dcw[:CONV_WIDTH]
    dz = jnp.concatenate([dzg, dzc, dzp, dq, dk, dv], axis=1)
    du = mm_big("mm_du", dz, w.w_in, "nt", (512, 1024, 896), F32)
    g["w_in"] = _unperm_cols(mm_big("mm_dw_in", u, dz, "tn", (1024, 896, 512), F32))
    (dx,), (dscm, dshm) = row_bwd("lnmod_mix_bwd", f_lnmod, [x], [sc_m, sh_m], [du], 512, [(0, F32)], [0, 1],
                                  add_to=dxa)
    dmod = jnp.concatenate([dshm, dscm, dgm, dshf, dscf, dgf], axis=1)
    return dx, dmod, g


def kernel(x, c, w_ada, b_ada, w_in, b_gate, w_pool, pool_scale, rel_bias, conv_w, conv_b, conv_ln_g, conv_ln_b, w_br_pool, w_br_attn, w_br_conv, w_o, ln_mix_g, ln_mix_b, w_ff1, b_ff1, w_ff2, b_ff2, ln_ff_g, ln_ff_b, loss_target, m_w_ada, m_b_ada, m_w_in, m_b_gate, m_w_pool, m_pool_scale, m_rel_bias, m_conv_w, m_conv_b, m_conv_ln_g, m_conv_ln_b, m_w_br_pool, m_w_br_attn, m_w_br_conv, m_w_o, m_ln_mix_g, m_ln_mix_b, m_w_ff1, m_b_ff1, m_w_ff2, m_b_ff2, m_ln_ff_g, m_ln_ff_b, v_w_ada, v_b_ada, v_w_in, v_b_gate, v_w_pool, v_pool_scale, v_rel_bias, v_conv_w, v_conv_b, v_conv_ln_g, v_conv_ln_b, v_w_br_pool, v_w_br_attn, v_w_br_conv, v_w_o, v_ln_mix_g, v_ln_mix_b, v_w_ff1, v_b_ff1, v_w_ff2, v_b_ff2, v_ln_ff_g, v_ln_ff_b):
    args = dict(locals())
    wts = {n: args[n] for n in WEIGHT_NAMES}
    mom = {n: args["m_" + n] for n in WEIGHT_NAMES}
    var = {n: args["v_" + n] for n in WEIGHT_NAMES}
    me = 4 * lax.axis_index("x") + 2 * lax.axis_index("y") + lax.axis_index("c")
    xs, tgt = x[0], loss_target[0]
    nc_ada = w_ada.shape[2]
    wvec = jnp.asarray(np.repeat(np.array(POOL_WINDOWS, np.float32), POOL_GROUP)[None, :])

    w_cat = jnp.concatenate([w_ada[l] for l in range(DEPTH)], axis=1)
    b_lay = b_ada.reshape(DEPTH, N_DEV, nc_ada).transpose(1, 0, 2).reshape(N_DEV, 1, DEPTH * nc_ada)
    land, cact = ada_fwd("ada_fwd", c, w_cat, b_lay)
    mod = land.reshape(N_DEV, DEPTH, nc_ada).transpose(1, 0, 2).reshape(DEPTH, 6 * D_MODEL)
    modr = [[mod[l:l + 1, i * D_MODEL:(i + 1) * D_MODEL] for i in range(6)] for l in range(DEPTH)]

    shard_shapes = [wts[n].shape[1:] for n in BIG_NAMES]
    pack = _flat_pad([wts[n][l].astype(BF16) for l in range(DEPTH) for n in BIG_NAMES], PACK_COLS)
    gathered = all_gather8("gather_weights", pack.reshape(-1, PACK_COLS)).reshape(N_DEV, -1)
    full, off = [{} for _ in range(DEPTH)], 0
    for l in range(DEPTH):
        for n, shp in zip(BIG_NAMES, shard_shapes):
            cnt = int(np.prod(shp))
            full[l][n] = _from_blocks(n, gathered[:, off:off + cnt].reshape((N_DEV,) + tuple(shp)))
            off += cnt
    cw_pack = _flat_pad([conv_w], 8 * LANE).reshape(-1, LANE)
    cw_all = all_gather8("gather_conv_w", cw_pack).reshape(N_DEV, -1)[:, :conv_w.size]
    conv_full = cw_all.reshape((N_DEV,) + conv_w.shape).transpose(1, 2, 0, 3).reshape(DEPTH, CONV_WIDTH, D_CONV)

    layers = []
    for l in range(DEPTH):
        w = _Layer()
        w.w_in = _perm_cols(full[l]["w_in"])
        w.wbp, w.wba, w.wbc = full[l]["w_br_pool"], full[l]["w_br_attn"], full[l]["w_br_conv"]
        w.wo, w.w_ff1, w.w_ff2 = full[l]["w_o"], full[l]["w_ff1"], full[l]["w_ff2"]
        w.wbd = _block_diag(w_pool[l])
        w.ps, w.cb, w.clg, w.clb = _row(pool_scale[l]), _row(conv_b[l]), _row(conv_ln_g[l]), _row(conv_ln_b[l])
        w.bg, w.lmg, w.lmb = _row(b_gate[l]), _row(ln_mix_g[l]), _row(ln_mix_b[l])
        w.b1, w.b2, w.lfg, w.lfb = _row(b_ff1[l]), _row(b_ff2[l]), _row(ln_ff_g[l]), _row(ln_ff_b[l])
        w.conv_w = jnp.pad(conv_full[l], ((0, CONV_PAD - CONV_WIDTH), (0, 0)))
        w.bias, w.bias_vjp = jax.vjp(_bias_table, rel_bias[l])
        layers.append(w)

    h, saved = xs, []
    for l in range(DEPTH):
        h, sv = _layer_fwd(h, modr[l], layers[l], wvec)
        saved.append(sv)
    lpart, dy = loss_head("loss_head", h, tgt, 512)
    loss = lax.psum(lpart[0, 0], ("x", "y", "c"))
    grads, dmods = [None] * DEPTH, [None] * DEPTH
    for l in reversed(range(DEPTH)):
        dy, dmods[l], grads[l] = _layer_bwd(dy, saved[l], modr[l], layers[l], wvec)
    grad_x = dy[None]
    dmod = jnp.concatenate(dmods, axis=0)

    small_shapes = [wts[n].shape if n != "conv_w" else (DEPTH, CONV_WIDTH, D_CONV) for n in SMALL_NAMES]
    small_local = [dmod] + [jnp.stack([grads[l][n].reshape(shp[1:]) for l in range(DEPTH)])
                            for n, shp in zip(SMALL_NAMES[1:], small_shapes[1:])]
    small_pack = _flat_pad(small_local, 8 * LANE).reshape(-1, LANE)
    small_all = all_gather8("gather_small_grads", small_pack)
    small_sum = sum_blocks("sum_small_grads", small_all, small_pack.shape[0]).reshape(-1)
    gsmall = dict(zip(SMALL_NAMES, _unflat(small_sum, small_shapes)))
    gw = dict(gsmall)
    gw["conv_w"] = lax.dynamic_slice_in_dim(gsmall["conv_w"], me * conv_w.shape[2], conv_w.shape[2], axis=2)

    dmod_all = small_all.reshape(N_DEV, -1)[:, :dmod.size].reshape(N_DEV, DEPTH, N_DEV, nc_ada)
    dm_mine = lax.dynamic_index_in_dim(dmod_all, me, axis=2, keepdims=False).reshape(N_DEV, DEPTH * nc_ada)
    cact_t = jnp.pad(cact.T, ((0, 0), (0, LANE - N_DEV)))
    dm_pad = jnp.pad(dm_mine, ((0, LANE - N_DEV), (0, 0)))
    dw_cat = mm_big("mm_dw_ada", cact_t, dm_pad, "nn", (D_MODEL, DEPTH * nc_ada, LANE), F32)
    gw["w_ada"] = jnp.stack([dw_cat[:, l * nc_ada:(l + 1) * nc_ada] for l in range(DEPTH)])

    blocks = jnp.concatenate([_to_blocks(n, grads[l][n]).reshape(N_DEV, -1).astype(BF16)
                              for l in range(DEPTH) for n in BIG_NAMES], axis=1)
    recv = all_to_all8("scatter_grads", blocks.reshape(N_DEV, -1, PACK_COLS))
    big_sum = sum_blocks("sum_grads", recv, 64).reshape(-1)
    pieces = _unflat(big_sum, [tuple(shp) for _ in range(DEPTH) for shp in shard_shapes])
    for i, n in enumerate(BIG_NAMES):
        gw[n] = jnp.stack([pieces[l * len(BIG_NAMES) + i] for l in range(DEPTH)])

    delta, new_m, new_v = {}, {}, {}
    for n in ["w_ada"] + BIG_NAMES:
        shp = wts[n].shape
        two_d = lambda a, shp=shp: a.reshape(shp[0] * shp[1], shp[2])
        rows = shp[0] * shp[1]
        tr = 256 if rows % 256 == 0 else rows
        res = adamw("adamw_" + n, two_d(wts[n]), two_d(gw[n]), two_d(mom[n]), two_d(var[n]), tr)
        delta[n], new_m[n], new_v[n] = [a.reshape(shp) for a in res]
    packs = [_flat_pad([src[n] for n in SMALL_NAMES], 8 * LANE).reshape(-1, LANE) for src in (wts, gw, mom, var)]
    res = adamw("adamw_small", *packs, packs[0].shape[0])
    shapes = [wts[n].shape for n in SMALL_NAMES]
    for out, flat in zip((delta, new_m, new_v), res):
        out.update(dict(zip(SMALL_NAMES, _unflat(flat.reshape(-1), shapes))))

    return (loss, grad_x, *[gw[n] for n in WEIGHT_NAMES], *[delta[n] for n in WEIGHT_NAMES],
            *[new_m[n] for n in WEIGHT_NAMES], *[new_v[n] for n in WEIGHT_NAMES])
```

```python
import functools

import jax
import jax.numpy as jnp
import numpy as np
from jax import lax
from jax.experimental import pallas as pl
from jax.experimental.pallas import tpu as pltpu

F32 = jnp.float32
BF16 = jnp.bfloat16
MESH = pl.DeviceIdType.MESH

D_MODEL = 1024
DEPTH = 2
CHUNK = 64
N_HEADS = 8
HEAD_DIM = 64
D_POOL = 256
D_ATTN = 512
D_CONV = 256
CONV_WIDTH = 31
D_FF = 4096
D_IN = 5376
N_PREV = 8
BAND = (N_PREV + 1) * CHUNK
REL_CLIP = 128
ALPHA = (2.0 * DEPTH) ** 0.25
LN_EPS = 1e-5
NEG_INF = -1e30
N_DEV = 8

ADAM_LR, ADAM_B1, ADAM_B2, ADAM_EPS, ADAM_WD, ADAM_STEP = 0.001, 0.9, 0.999, 1e-08, 0.01, 10

VMEM_LIMIT = 56 * 1024 * 1024

Z_GATE, Z_CONV, Z_POOL, Z_Q, Z_K, Z_V = 0, 3072, 3584, 3840, 4352, 4864
ATT_TILE = 512
LANE = 128


def _dg(a, b, ca, cb):
    return lax.dot_general(a.astype(BF16), b.astype(BF16), (((ca,), (cb,)), ((), ())),
                           preferred_element_type=F32)


@jax.custom_vjp
def mm_nn(a, b):
    return _dg(a, b, 1, 0)


def _mm_nn_fwd(a, b):
    return _dg(a, b, 1, 0), (a, b)


def _mm_nn_bwd(res, g):
    a, b = res
    return _dg(g, b, 1, 1).astype(a.dtype), _dg(a, g, 0, 0).astype(b.dtype)


mm_nn.defvjp(_mm_nn_fwd, _mm_nn_bwd)


@jax.custom_vjp
def mm_nt(a, b):
    return _dg(a, b, 1, 1)


def _mm_nt_fwd(a, b):
    return _dg(a, b, 1, 1), (a, b)


def _mm_nt_bwd(res, g):
    a, b = res
    return _dg(g, b, 1, 0).astype(a.dtype), _dg(g, a, 0, 0).astype(b.dtype)


mm_nt.defvjp(_mm_nt_fwd, _mm_nt_bwd)


def _ln(x):
    mu = jnp.mean(x, axis=-1, keepdims=True)
    xc = x - mu
    var = jnp.mean(xc * xc, axis=-1, keepdims=True)
    return xc * lax.rsqrt(var + LN_EPS)


def _norm_rows(rows):
    return [r if isinstance(r, tuple) else (r, r.shape[1], 0) for r in rows]


def _row_spec(tm, r):
    _, width, cb = r
    return pl.BlockSpec((tm, width), lambda i, cb=cb: (i, cb))


def _full_spec(a):
    nd = a.ndim
    return pl.BlockSpec(a.shape, lambda i, nd=nd: (0,) * nd)


def row_fwd(name, f, rows, params, outs, tm):
    rows = _norm_rows(rows)
    s = rows[0][0].shape[0]
    nr, npar = len(rows), len(params)

    def body(*refs):
        r = [x[...].astype(F32) for x in refs[:nr]]
        p = [x[...] for x in refs[nr:nr + npar]]
        res = f(*r, *p)
        for o_ref, o in zip(refs[nr + npar:], res):
            o_ref[...] = o.astype(o_ref.dtype)

    return pl.pallas_call(
        body, name=name, grid=(s // tm,),
        in_specs=[_row_spec(tm, r) for r in rows] + [_full_spec(p) for p in params],
        out_specs=[pl.BlockSpec((tm, w), lambda i: (i, 0)) for w, _ in outs],
        out_shape=[jax.ShapeDtypeStruct((s, w), dt) for w, dt in outs],
        compiler_params=pltpu.CompilerParams(dimension_semantics=("parallel",), vmem_limit_bytes=VMEM_LIMIT),
    )(*[r[0] for r in rows], *params)


def row_bwd(name, f, rows, params, douts, tm, want_rows, want_params, add_to=None):
    rows = _norm_rows(rows)
    s = rows[0][0].shape[0]
    nr, npar, nd = len(rows), len(params), len(douts)
    nadd = 0 if add_to is None else 1
    n_in = nr + npar + nd + nadd

    def body(*refs):
        i = pl.program_id(0)
        r = [x[...].astype(F32) for x in refs[:nr]]
        p = [x[...].astype(F32) for x in refs[nr:nr + npar]]
        d = [x[...].astype(F32) for x in refs[nr + npar:nr + npar + nd]]
        _, vjp = jax.vjp(f, *r, *p)
        g = vjp(tuple(d))
        out_refs = refs[n_in:]
        for k, (idx, _) in enumerate(want_rows):
            val = g[idx]
            if nadd and k == 0:
                val = val + refs[n_in - 1][...].astype(F32)
            out_refs[k][...] = val.astype(out_refs[k].dtype)
        for k, idx in enumerate(want_params):
            gp = g[nr + idx]
            o_ref = out_refs[len(want_rows) + k]

            @pl.when(i == 0)
            def _():
                o_ref[...] = gp

            @pl.when(i > 0)
            def _():
                o_ref[...] += gp

    in_specs = ([_row_spec(tm, r) for r in rows] + [_full_spec(p) for p in params]
                + [pl.BlockSpec((tm, d.shape[1]), lambda i: (i, 0)) for d in douts])
    args = [r[0] for r in rows] + list(params) + list(douts)
    if nadd:
        in_specs.append(pl.BlockSpec((tm, add_to.shape[1]), lambda i: (i, 0)))
        args.append(add_to)
    out_specs = ([pl.BlockSpec((tm, rows[idx][1]), lambda i: (i, 0)) for idx, _ in want_rows]
                 + [_full_spec(params[idx]) for idx in want_params])
    out_shape = ([jax.ShapeDtypeStruct((s, rows[idx][1]), dt) for idx, dt in want_rows]
                 + [jax.ShapeDtypeStruct(params[idx].shape, F32) for idx in want_params])
    res = pl.pallas_call(
        body, name=name, grid=(s // tm,), in_specs=in_specs, out_specs=out_specs, out_shape=out_shape,
        compiler_params=pltpu.CompilerParams(dimension_semantics=("arbitrary",), vmem_limit_bytes=VMEM_LIMIT),
    )(*args)
    return res[:len(want_rows)], res[len(want_rows):]


def f_lnmod(x, sc, sh):
    return (_ln(x) * (1.0 + sc) + sh,)


def f_merge(p, ao, cv, zg, x, wbd, ps, wbp, wba, wbc, cb, clg, clb, bg, wo, gm, lg, lb):
    pm = mm_nn(p, wbd) * ps
    co = jax.nn.silu(_ln(cv + cb) * clg + clb)
    y_pool = mm_nn(pm, wbp)
    y_attn = mm_nn(ao, wba)
    y_conv = mm_nn(co, wbc)
    gates = jax.nn.sigmoid(zg + bg)
    merged = (gates[:, :D_MODEL] * y_pool + gates[:, D_MODEL:2 * D_MODEL] * y_attn
              + gates[:, 2 * D_MODEL:] * y_conv)
    mix = mm_nn(merged, wo)
    return (_ln(ALPHA * x + gm * mix) * lg + lb,)


def f_relu2(hpre, b1):
    a = jax.nn.relu(hpre + b1)
    return (a * a,)


def f_ffout(x1, ff, b2, gf, lg, lb):
    return (_ln(ALPHA * x1 + gf * (ff + b2)) * lg + lb,)


def mm_big(name, a, b, kind, tiles, out_dtype):
    t0, t1, tr = tiles
    if kind == "nn":
        (m, k), n = a.shape, b.shape[1]
        o0, o1, red = m, n, k
        a_spec = pl.BlockSpec((t0, tr), lambda i, j, r: (i, r))
        b_spec = pl.BlockSpec((tr, t1), lambda i, j, r: (r, j))
        dims = (1, 0)
    elif kind == "nt":
        (m, n), k = a.shape, b.shape[0]
        o0, o1, red = m, k, n
        a_spec = pl.BlockSpec((t0, tr), lambda i, j, r: (i, r))
        b_spec = pl.BlockSpec((t1, tr), lambda i, j, r: (j, r))
        dims = (1, 1)
    else:
        (m, k), n = a.shape, b.shape[1]
        o0, o1, red = k, n, m
        a_spec = pl.BlockSpec((tr, t0), lambda i, j, r: (r, i))
        b_spec = pl.BlockSpec((tr, t1), lambda i, j, r: (r, j))
        dims = (0, 0)
    assert o0 % t0 == 0 and o1 % t1 == 0 and red % tr == 0, (name, a.shape, b.shape, tiles)
    nred = red // tr

    def body(a_ref, b_ref, o_ref, acc_ref):
        r = pl.program_id(2)
        part = _dg(a_ref[...], b_ref[...], *dims)

        @pl.when(r == 0)
        def _():
            acc_ref[...] = part

        @pl.when(r > 0)
        def _():
            acc_ref[...] += part

        @pl.when(r == nred - 1)
        def _():
            o_ref[...] = acc_ref[...].astype(o_ref.dtype)

    return pl.pallas_call(
        body, name=name, grid=(o0 // t0, o1 // t1, nred),
        in_specs=[a_spec, b_spec],
        out_specs=pl.BlockSpec((t0, t1), lambda i, j, r: (i, j)),
        out_shape=jax.ShapeDtypeStruct((o0, o1), out_dtype),
        scratch_shapes=[pltpu.VMEM((t0, t1), F32)],
        compiler_params=pltpu.CompilerParams(dimension_semantics=("parallel", "parallel", "arbitrary"),
                                             vmem_limit_bytes=VMEM_LIMIT),
    )(a, b)


POOL_PAD = 16
POOL_ROWS = 256


def pool_lin(name, x, wvec, transpose, out_dtype):
    arr, width, cb = x
    s = arr.shape[0]
    n_steps = s // POOL_ROWS

    def body(x_ref, w_ref, o_ref, xp_ref):
        wv = w_ref[...]
        zeros = jnp.zeros((POOL_PAD, width), F32)
        xp_ref[0:POOL_PAD, :] = zeros
        xp_ref[s + POOL_PAD:s + 2 * POOL_PAD, :] = zeros

        def count(t0):
            t = lax.broadcasted_iota(jnp.int32, (POOL_ROWS, width), 0) + (t0 + 1)
            return jnp.minimum(t.astype(F32), wv)

        def fill(i, carry):
            t0 = pl.multiple_of(i * POOL_ROWS, POOL_ROWS)
            v = x_ref[pl.ds(t0, POOL_ROWS), :].astype(F32)
            if transpose:
                v = v / count(t0)
            xp_ref[pl.ds(t0 + POOL_PAD, POOL_ROWS), :] = v
            return carry

        lax.fori_loop(0, n_steps, fill, 0)

        def step(i, carry):
            t0 = pl.multiple_of(i * POOL_ROWS, POOL_ROWS)
            win = xp_ref[pl.ds(t0, POOL_ROWS + 2 * POOL_PAD), :]
            acc = jnp.zeros((POOL_ROWS, width), F32)
            for j in range(POOL_PAD):
                off = POOL_PAD + j if transpose else POOL_PAD - j
                acc = acc + jnp.where(wv > j, win[off:off + POOL_ROWS, :], 0.0)
            cur = x_ref[pl.ds(t0, POOL_ROWS), :].astype(F32)
            res = acc - cur if transpose else acc / count(t0) - cur
            o_ref[pl.ds(t0, POOL_ROWS), :] = res.astype(o_ref.dtype)
            return carry

        lax.fori_loop(0, n_steps, step, 0)

    return pl.pallas_call(
        body, name=name, grid=(1,),
        in_specs=[pl.BlockSpec((s, width), lambda i, cb=cb: (0, cb)), pl.BlockSpec((1, width), lambda i: (0, 0))],
        out_specs=pl.BlockSpec((s, width), lambda i: (0, 0)),
        out_shape=jax.ShapeDtypeStruct((s, width), out_dtype),
        scratch_shapes=[pltpu.VMEM((s + 2 * POOL_PAD, width), F32)],
        compiler_params=pltpu.CompilerParams(dimension_semantics=("arbitrary",), vmem_limit_bytes=VMEM_LIMIT),
    )(arr, wvec)


CONV_PAD = 32
CONV_ROWS = 128


def _glu(a, g):
    return a * jax.nn.sigmoid(g)


def conv_fwd(name, zc, w):
    arr, width, cb = zc
    s = arr.shape[0]
    n_steps = s // CONV_ROWS
    lead = CONV_PAD - (CONV_WIDTH - 1)

    def body(z_ref, w_ref, o_ref, hp_ref):
        hp_ref[0:CONV_PAD, :] = jnp.zeros((CONV_PAD, D_CONV), F32)

        def fill(i, carry):
            t0 = pl.multiple_of(i * CONV_ROWS, CONV_ROWS)
            z = z_ref[pl.ds(t0, CONV_ROWS), :]
            hp_ref[pl.ds(t0 + CONV_PAD, CONV_ROWS), :] = _glu(z[:, :D_CONV], z[:, D_CONV:])
            return carry

        lax.fori_loop(0, n_steps, fill, 0)
        wv = w_ref[...]

        def step(i, carry):
            t0 = pl.multiple_of(i * CONV_ROWS, CONV_ROWS)
            win = hp_ref[pl.ds(t0, CONV_ROWS + CONV_PAD), :]
            acc = jnp.zeros((CONV_ROWS, D_CONV), F32)
            for k in range(CONV_WIDTH):
                acc = acc + wv[k:k + 1, :] * win[lead + k:lead + k + CONV_ROWS, :]
            o_ref[pl.ds(t0, CONV_ROWS), :] = acc
            return carry

        lax.fori_loop(0, n_steps, step, 0)

    return pl.pallas_call(
        body, name=name, grid=(1,),
        in_specs=[pl.BlockSpec((s, width), lambda i, cb=cb: (0, cb)), pl.BlockSpec(w.shape, lambda i: (0, 0))],
        out_specs=pl.BlockSpec((s, D_CONV), lambda i: (0, 0)),
        out_shape=jax.ShapeDtypeStruct((s, D_CONV), F32),
        scratch_shapes=[pltpu.VMEM((s + CONV_PAD, D_CONV), F32)],
        compiler_params=pltpu.CompilerParams(dimension_semantics=("arbitrary",), vmem_limit_bytes=VMEM_LIMIT),
    )(arr, w)


def conv_bwd(name, zc, w, dout):
    arr, width, cb = zc
    s = arr.shape[0]
    n_steps = s // CONV_ROWS
    lead = CONV_PAD - (CONV_WIDTH - 1)

    def body(z_ref, w_ref, d_ref, dz_ref, dw_ref, hp_ref, dp_ref):
        hp_ref[0:CONV_PAD, :] = jnp.zeros((CONV_PAD, D_CONV), F32)
        dp_ref[s:s + CONV_PAD, :] = jnp.zeros((CONV_PAD, D_CONV), F32)
        dw_ref[...] = jnp.zeros(dw_ref.shape, F32)

        def fill(i, carry):
            t0 = pl.multiple_of(i * CONV_ROWS, CONV_ROWS)
            z = z_ref[pl.ds(t0, CONV_ROWS), :]
            hp_ref[pl.ds(t0 + CONV_PAD, CONV_ROWS), :] = _glu(z[:, :D_CONV], z[:, D_CONV:])
            dp_ref[pl.ds(t0, CONV_ROWS), :] = d_ref[pl.ds(t0, CONV_ROWS), :]
            return carry

        lax.fori_loop(0, n_steps, fill, 0)
        wv = w_ref[...]

        def step(i, carry):
            t0 = pl.multiple_of(i * CONV_ROWS, CONV_ROWS)
            hwin = hp_ref[pl.ds(t0, CONV_ROWS + CONV_PAD), :]
            dwin = dp_ref[pl.ds(t0, CONV_ROWS + CONV_PAD), :]
            dcur = dwin[0:CONV_ROWS, :]
            dh = jnp.zeros((CONV_ROWS, D_CONV), F32)
            rows = []
            for k in range(CONV_WIDTH):
                rows.append(jnp.sum(dcur * hwin[lead + k:lead + k + CONV_ROWS, :], axis=0, keepdims=True))
                back = CONV_WIDTH - 1 - k
                dh = dh + wv[k:k + 1, :] * dwin[back:back + CONV_ROWS, :]
            rows.append(jnp.zeros((1, D_CONV), F32))
            dw_ref[...] += jnp.concatenate(rows, axis=0)
            z = z_ref[pl.ds(t0, CONV_ROWS), :]
            _, vjp = jax.vjp(_glu, z[:, :D_CONV], z[:, D_CONV:])
            da, dg = vjp(dh)
            dz_ref[pl.ds(t0, CONV_ROWS), :] = jnp.concatenate([da, dg], axis=1).astype(dz_ref.dtype)
            return carry

        lax.fori_loop(0, n_steps, step, 0)

    return pl.pallas_call(
        body, name=name, grid=(1,),
        in_specs=[pl.BlockSpec((s, width), lambda i, cb=cb: (0, cb)), pl.BlockSpec(w.shape, lambda i: (0, 0)),
                  pl.BlockSpec((s, D_CONV), lambda i: (0, 0))],
        out_specs=[pl.BlockSpec((s, width), lambda i: (0, 0)), pl.BlockSpec(w.shape, lambda i: (0, 0))],
        out_shape=[jax.ShapeDtypeStruct((s, width), BF16), jax.ShapeDtypeStruct(w.shape, F32)],
        scratch_shapes=[pltpu.VMEM((s + CONV_PAD, D_CONV), F32), pltpu.VMEM((s + CONV_PAD, D_CONV), F32)],
        compiler_params=pltpu.CompilerParams(dimension_semantics=("arbitrary",), vmem_limit_bytes=VMEM_LIMIT),
    )(arr, w, dout)


HEADS_PER_STEP = LANE // HEAD_DIM
CHUNKS_PER_TILE = ATT_TILE // CHUNK


def _attn_tile(q, kp, kc, vp, vc, bias, first):
    kcat = jnp.concatenate([kp, kc], axis=0)
    vcat = jnp.concatenate([vp, vc], axis=0)
    lane = lax.broadcasted_iota(jnp.int32, (1, LANE), 1)
    kj = lax.broadcasted_iota(jnp.int32, (1, BAND), 1)
    outs = []
    for i in range(CHUNKS_PER_TILE):
        qi = q[i * CHUNK:(i + 1) * CHUNK, :] * (HEAD_DIM ** -0.5)
        kb = kcat[i * CHUNK:i * CHUNK + BAND, :]
        vb = vcat[i * CHUNK:i * CHUNK + BAND, :]
        valid = jnp.logical_or(jnp.logical_not(first), kj >= ATT_TILE - i * CHUNK)
        o = jnp.zeros((CHUNK, LANE), F32)
        for h in range(HEADS_PER_STEP):
            in_head = jnp.logical_and(lane >= h * HEAD_DIM, lane < (h + 1) * HEAD_DIM)
            sc = mm_nt(jnp.where(in_head, qi, 0.0), kb) + bias[h]
            sc = jnp.where(valid, sc, NEG_INF)
            m = jnp.max(sc, axis=-1, keepdims=True)
            e = jnp.exp(sc - lax.stop_gradient(m))
            p = e / jnp.sum(e, axis=-1, keepdims=True)
            o = o + jnp.where(in_head, mm_nn(p, vb), 0.0)
        outs.append(o)
    return jnp.concatenate(outs, axis=0)


def _z_spec(col0, fn):
    return pl.BlockSpec((ATT_TILE, LANE), lambda hp, n, col0=col0, fn=fn: (fn(n), col0 // LANE + hp))


def attn_fwd(name, z, bias):
    s = z.shape[0]
    nt = s // ATT_TILE

    def body(q_ref, kp_ref, kc_ref, vp_ref, vc_ref, b_ref, o_ref):
        first = pl.program_id(1) == 0
        o = _attn_tile(q_ref[...], kp_ref[...], kc_ref[...], vp_ref[...], vc_ref[...], b_ref[...], first)
        o_ref[...] = o.astype(o_ref.dtype)

    cur = lambda n: n
    prev = lambda n: jnp.maximum(n - 1, 0)
    return pl.pallas_call(
        body, name=name, grid=(N_HEADS // HEADS_PER_STEP, nt),
        in_specs=[_z_spec(Z_Q, cur), _z_spec(Z_K, prev), _z_spec(Z_K, cur), _z_spec(Z_V, prev), _z_spec(Z_V, cur),
                  pl.BlockSpec((HEADS_PER_STEP, CHUNK, BAND), lambda hp, n: (hp, 0, 0))],
        out_specs=pl.BlockSpec((ATT_TILE, LANE), lambda hp, n: (n, hp)),
        out_shape=jax.ShapeDtypeStruct((s, D_ATTN), BF16),
        compiler_params=pltpu.CompilerParams(dimension_semantics=("parallel", "parallel"), vmem_limit_bytes=VMEM_LIMIT),
    )(z, z, z, z, z, bias)


def attn_bwd(name, z, bias, do):
    s = z.shape[0]
    nt = s // ATT_TILE

    def body(q_ref, kp_ref, kc_ref, vp_ref, vc_ref, b_ref, do_ref, dq_ref, dk_ref, dv_ref, db_ref, kacc, vacc):
        n = pl.program_id(1)

        @pl.when(n == 0)
        def _():
            db_ref[...] = jnp.zeros(db_ref.shape, F32)

        @pl.when(n < nt)
        def _():
            first = n == 0
            fn = functools.partial(_attn_tile, first=first)
            _, vjp = jax.vjp(fn, q_ref[...], kp_ref[...], kc_ref[...], vp_ref[...], vc_ref[...], b_ref[...])
            dq, dkp, dkc, dvp, dvc, db = vjp(do_ref[...].astype(F32))
            dq_ref[...] = dq.astype(dq_ref.dtype)
            db_ref[...] += db

            @pl.when(n > 0)
            def _():
                dk_ref[...] = (kacc[...] + dkp).astype(dk_ref.dtype)
                dv_ref[...] = (vacc[...] + dvp).astype(dv_ref.dtype)

            kacc[...] = dkc
            vacc[...] = dvc

        @pl.when(n == nt)
        def _():
            dk_ref[...] = kacc[...].astype(dk_ref.dtype)
            dv_ref[...] = vacc[...].astype(dv_ref.dtype)

    cur = lambda n: jnp.minimum(n, nt - 1)
    prev = lambda n: jnp.clip(n - 1, 0, nt - 1)
    o_cur = pl.BlockSpec((ATT_TILE, LANE), lambda hp, n: (jnp.minimum(n, nt - 1), hp))
    o_prev = pl.BlockSpec((ATT_TILE, LANE), lambda hp, n: (jnp.maximum(n - 1, 0), hp))
    return pl.pallas_call(
        body, name=name, grid=(N_HEADS // HEADS_PER_STEP, nt + 1),
        in_specs=[_z_spec(Z_Q, cur), _z_spec(Z_K, prev), _z_spec(Z_K, cur), _z_spec(Z_V, prev), _z_spec(Z_V, cur),
                  pl.BlockSpec((HEADS_PER_STEP, CHUNK, BAND), lambda hp, n: (hp, 0, 0)), o_cur],
        out_specs=[o_cur, o_prev, o_prev, pl.BlockSpec((HEADS_PER_STEP, CHUNK, BAND), lambda hp, n: (hp, 0, 0))],
        out_shape=[jax.ShapeDtypeStruct((s, D_ATTN), BF16)] * 3 + [jax.ShapeDtypeStruct((N_HEADS, CHUNK, BAND), F32)],
        scratch_shapes=[pltpu.VMEM((ATT_TILE, LANE), F32), pltpu.VMEM((ATT_TILE, LANE), F32)],
        compiler_params=pltpu.CompilerParams(dimension_semantics=("parallel", "arbitrary"), vmem_limit_bytes=VMEM_LIMIT),
    )(z, z, z, z, z, bias, do)


def loss_head(name, y, tgt, tm):
    s, d = y.shape

    def body(y_ref, t_ref, l_ref, dy_ref):
        i = pl.program_id(0)
        diff = y_ref[...] - t_ref[...]
        dy_ref[...] = diff * (1.0 / d)
        part = 0.5 * jnp.sum(jnp.mean(diff * diff, axis=-1, keepdims=True), axis=0, keepdims=True)

        @pl.when(i == 0)
        def _():
            l_ref[...] = jnp.zeros(l_ref.shape, F32)

        l_ref[...] += jnp.broadcast_to(part, l_ref.shape)

    row = pl.BlockSpec((tm, d), lambda i: (i, 0))
    return pl.pallas_call(
        body, name=name, grid=(s // tm,), in_specs=[row, row],
        out_specs=[pl.BlockSpec((8, LANE), lambda i: (0, 0)), row],
        out_shape=[jax.ShapeDtypeStruct((8, LANE), F32), jax.ShapeDtypeStruct((s, d), F32)],
        compiler_params=pltpu.CompilerParams(dimension_semantics=("arbitrary",), vmem_limit_bytes=VMEM_LIMIT),
    )(y, tgt)


def adamw(name, w, g, m, v, tr):
    r, c = w.shape
    assert r % tr == 0, (name, w.shape, tr)

    def body(w_ref, g_ref, m_ref, v_ref, d_ref, nm_ref, nv_ref):
        gg = g_ref[...]
        m2 = ADAM_B1 * m_ref[...] + (1.0 - ADAM_B1) * gg
        v2 = ADAM_B2 * v_ref[...] + (1.0 - ADAM_B2) * (gg * gg)
        m_hat = m2 / (1.0 - ADAM_B1 ** ADAM_STEP)
        v_hat = v2 / (1.0 - ADAM_B2 ** ADAM_STEP)
        d_ref[...] = -ADAM_LR * (m_hat / (jnp.sqrt(v_hat) + ADAM_EPS) + ADAM_WD * w_ref[...])
        nm_ref[...] = m2
        nv_ref[...] = v2

    blk = pl.BlockSpec((tr, c), lambda i: (i, 0))
    return pl.pallas_call(
        body, name=name, grid=(r // tr,), in_specs=[blk] * 4, out_specs=[blk] * 3,
        out_shape=[jax.ShapeDtypeStruct((r, c), F32)] * 3,
        compiler_params=pltpu.CompilerParams(dimension_semantics=("parallel",), vmem_limit_bytes=VMEM_LIMIT),
    )(w, g, m, v)


def adamw_sum(name, w, blocks, m, v, tr):
    r, c = w.shape
    nb = blocks.shape[0]
    assert r % tr == 0, (name, w.shape, tr)

    def body(w_ref, b_ref, m_ref, v_ref, g_ref, d_ref, nm_ref, nv_ref):
        gg = b_ref[0].astype(F32)
        for j in range(1, nb):
            gg = gg + b_ref[j].astype(F32)
        g_ref[...] = gg
        m2 = ADAM_B1 * m_ref[...] + (1.0 - ADAM_B1) * gg
        v2 = ADAM_B2 * v_ref[...] + (1.0 - ADAM_B2) * (gg * gg)
        m_hat = m2 / (1.0 - ADAM_B1 ** ADAM_STEP)
        v_hat = v2 / (1.0 - ADAM_B2 ** ADAM_STEP)
        d_ref[...] = -ADAM_LR * (m_hat / (jnp.sqrt(v_hat) + ADAM_EPS) + ADAM_WD * w_ref[...])
        nm_ref[...] = m2
        nv_ref[...] = v2

    blk = pl.BlockSpec((tr, c), lambda i: (i, 0))
    return pl.pallas_call(
        body, name=name, grid=(r // tr,),
        in_specs=[blk, pl.BlockSpec((nb, tr, c), lambda i: (0, i, 0)), blk, blk], out_specs=[blk] * 4,
        out_shape=[jax.ShapeDtypeStruct((r, c), F32)] * 4,
        compiler_params=pltpu.CompilerParams(dimension_semantics=("parallel",), vmem_limit_bytes=VMEM_LIMIT),
    )(w, blocks, m, v)


def sum_blocks(name, blocks, tr):
    nb, r, c = blocks.shape
    assert r % tr == 0, (name, blocks.shape, tr)

    def body(b_ref, o_ref):
        acc = b_ref[0].astype(F32)
        for j in range(1, nb):
            acc = acc + b_ref[j].astype(F32)
        o_ref[...] = acc

    return pl.pallas_call(
        body, name=name, grid=(r // tr,),
        in_specs=[pl.BlockSpec((nb, tr, c), lambda i: (0, i, 0))],
        out_specs=pl.BlockSpec((tr, c), lambda i: (i, 0)),
        out_shape=jax.ShapeDtypeStruct((r, c), F32),
        compiler_params=pltpu.CompilerParams(dimension_semantics=("parallel",), vmem_limit_bytes=VMEM_LIMIT),
    )(blocks)


FLIPS = [(0, 0, 1), (1, 0, 0), (0, 1, 0), (1, 1, 0), (1, 0, 1), (0, 1, 1), (1, 1, 1)]
ANY = pl.BlockSpec(memory_space=pl.ANY)


def _me():
    return lax.axis_index("x"), lax.axis_index("y"), lax.axis_index("c")


def _flip(pos, f):
    return tuple((1 - p) if fi else p for p, fi in zip(pos, f))


def _idx(pos):
    return 4 * pos[0] + 2 * pos[1] + pos[2]


def all_gather_multi(name, shards):
    n = len(shards)

    def body(*refs):
        x_refs, out_refs = refs[:n], refs[n:2 * n]
        send_sems, recv_sems, local_sems = refs[2 * n:]
        x, y, cc = _me()
        me, sibling = (x, y, cc), (x, y, 1 - cc)
        chips = [(1 - x, y), (x, 1 - y), (1 - x, 1 - y)]

        def copy(a, k, block, to, src=None):
            dst = out_refs[a].at[_idx(block)]
            return pltpu.make_async_remote_copy(
                src_ref=dst if src is None else src, dst_ref=dst, send_sem=send_sems.at[7 * a + k],
                recv_sem=recv_sems.at[7 * a + k], device_id=to, device_id_type=MESH)

        mine = [pltpu.make_async_copy(x_refs[a], out_refs[a].at[_idx(me)], local_sems.at[a]) for a in range(n)]
        for cp in mine:
            cp.start()
        first = []
        for a in range(n):
            first.append(copy(a, 0, me, sibling, src=x_refs[a]))
            first += [copy(a, 1 + j, me, (*chip, cc), src=x_refs[a]) for j, chip in enumerate(chips)]
        for cp in first:
            cp.start()
        passed = []
        for j, chip in enumerate(chips):
            for a in range(n):
                copy(a, 1 + j, (*chip, cc), me).wait_recv()
                fwd = copy(a, 4 + j, (*chip, cc), sibling)
                fwd.start()
                passed.append(fwd)
        for a in range(n):
            copy(a, 0, sibling, me).wait_recv()
            for j, chip in enumerate(chips):
                copy(a, 4 + j, (*chip, 1 - cc), me).wait_recv()
        for cp in first + passed:
            cp.wait_send()
        for cp in mine:
            cp.wait()

    return pl.pallas_call(
        body, name=name, in_specs=[ANY] * n, out_specs=[ANY] * n,
        out_shape=[jax.ShapeDtypeStruct((N_DEV,) + a.shape, a.dtype) for a in shards],
        scratch_shapes=[pltpu.SemaphoreType.DMA((7 * n,)), pltpu.SemaphoreType.DMA((7 * n,)),
                        pltpu.SemaphoreType.DMA((n,))],
    )(*shards)


def all_to_all_multi(name, blocks):
    n = len(blocks)

    def body(*refs):
        in_refs, out_refs = refs[:n], refs[n:2 * n]
        send_sems, recv_sems, local_sems = refs[2 * n:]
        me = _me()
        mi = _idx(me)
        mine = [pltpu.make_async_copy(in_refs[a].at[mi], out_refs[a].at[mi], local_sems.at[a]) for a in range(n)]
        for cp in mine:
            cp.start()
        sends, recvs = [], []
        for k, f in enumerate(FLIPS):
            peer = _flip(me, f)
            pi = _idx(peer)
            for a in range(n):
                sems = dict(send_sem=send_sems.at[7 * a + k], recv_sem=recv_sems.at[7 * a + k],
                            device_id=peer, device_id_type=MESH)
                sends.append(pltpu.make_async_remote_copy(src_ref=in_refs[a].at[pi], dst_ref=out_refs[a].at[mi], **sems))
                recvs.append(pltpu.make_async_remote_copy(src_ref=in_refs[a].at[mi], dst_ref=out_refs[a].at[pi], **sems))
        for cp in sends:
            cp.start()
        for cp in recvs:
            cp.wait_recv()
        for cp in sends:
            cp.wait_send()
        for cp in mine:
            cp.wait()

    return pl.pallas_call(
        body, name=name, in_specs=[ANY] * n, out_specs=[ANY] * n,
        out_shape=[jax.ShapeDtypeStruct(a.shape, a.dtype) for a in blocks],
        scratch_shapes=[pltpu.SemaphoreType.DMA((7 * n,)), pltpu.SemaphoreType.DMA((7 * n,)),
                        pltpu.SemaphoreType.DMA((n,))],
    )(*blocks)


def ada_fwd(name, c_row, w_cat, b_lay):
    d = c_row.shape[1]
    ncol = w_cat.shape[1]
    vmem = pl.BlockSpec(memory_space=pltpu.VMEM)

    def body(c_ref, w_ref, b_ref, mod_ref, cact_ref, call, send, land, s1, r1, s2, r2):
        me = _me()
        mi = _idx(me)
        call[mi] = c_ref[...]

        def exchange(src_of, dst_buf, ssem, rsem):
            sends, recvs = [], []
            for k, f in enumerate(FLIPS):
                peer = _flip(me, f)
                sends.append(pltpu.make_async_remote_copy(
                    src_ref=src_of(peer), dst_ref=dst_buf.at[mi], send_sem=ssem.at[k], recv_sem=rsem.at[k],
                    device_id=peer, device_id_type=MESH))
                recvs.append(pltpu.make_async_remote_copy(
                    src_ref=src_of(peer), dst_ref=dst_buf.at[_idx(peer)], send_sem=ssem.at[k], recv_sem=rsem.at[k],
                    device_id=peer, device_id_type=MESH))
            for cp in sends:
                cp.start()
            for cp in recvs:
                cp.wait_recv()
            for cp in sends:
                cp.wait_send()

        exchange(lambda peer: c_ref, call, s1, r1)
        for p in range(N_DEV):
            cact_ref[pl.ds(p, 1), :] = jax.nn.silu(call[p])
        res = _dg(cact_ref[...], w_ref[...], 1, 0)
        for p in range(N_DEV):
            send[p] = res[p:p + 1, :]
        land[mi] = send[mi]
        exchange(lambda peer: send.at[_idx(peer)], land, s2, r2)
        mod_ref[...] = land[...] + b_ref[...]

    return pl.pallas_call(
        body, name=name, in_specs=[vmem, vmem, vmem], out_specs=[vmem, vmem],
        out_shape=[jax.ShapeDtypeStruct((N_DEV, 1, ncol), F32), jax.ShapeDtypeStruct((N_DEV, d), F32)],
        scratch_shapes=[pltpu.VMEM((N_DEV, 1, d), F32), pltpu.VMEM((N_DEV, 1, ncol), F32),
                        pltpu.VMEM((N_DEV, 1, ncol), F32),
                        pltpu.SemaphoreType.DMA((7,)), pltpu.SemaphoreType.DMA((7,)),
                        pltpu.SemaphoreType.DMA((7,)), pltpu.SemaphoreType.DMA((7,))],
        compiler_params=pltpu.CompilerParams(vmem_limit_bytes=VMEM_LIMIT),
    )(c_row, w_cat, b_lay)


POOL_WINDOWS = (2, 4, 8, 16)
POOL_GROUP = 64
N_REL = 2 * REL_CLIP + 1
PACK_COLS = 1024
SMALL_NAMES = ["b_ada", "b_gate", "w_pool", "pool_scale", "rel_bias", "conv_w", "conv_b", "conv_ln_g",
               "conv_ln_b", "ln_mix_g", "ln_mix_b", "b_ff1", "b_ff2", "ln_ff_g", "ln_ff_b"]
BIG_NAMES = ["w_in", "w_br_pool", "w_br_attn", "w_br_conv", "w_o", "w_ff1", "w_ff2"]
ROW_SHARDED = ("w_o", "w_ff2")
WEIGHT_NAMES = ["w_ada", "b_ada", "w_in", "b_gate", "w_pool", "pool_scale", "rel_bias", "conv_w", "conv_b",
                "conv_ln_g", "conv_ln_b", "w_br_pool", "w_br_attn", "w_br_conv", "w_o", "ln_mix_g", "ln_mix_b",
                "w_ff1", "b_ff1", "w_ff2", "b_ff2", "ln_ff_g", "ln_ff_b"]


def _perm_cols(w):
    return jnp.concatenate([w[:, 2304:], w[:, 1792:2304], w[:, :256], w[:, 256:768], w[:, 768:1280],
                            w[:, 1280:1792]], axis=1)


def _unperm_cols(wp):
    return jnp.concatenate([wp[:, Z_POOL:Z_Q], wp[:, Z_Q:Z_K], wp[:, Z_K:Z_V], wp[:, Z_V:],
                            wp[:, Z_CONV:Z_POOL], wp[:, :Z_CONV]], axis=1)


def _bias_table(rel_bias):
    far = jnp.broadcast_to(rel_bias[:, 2 * REL_CLIP:], (N_HEADS, BAND - REL_CLIP))
    near = rel_bias[:, REL_CLIP - CHUNK + 1:2 * REL_CLIP][:, ::-1]
    ext = jnp.concatenate([far, near], axis=1)
    return jnp.stack([ext[:, CHUNK - 1 - qi:CHUNK - 1 - qi + BAND] for qi in range(CHUNK)], axis=1)


def _block_diag(w_pool):
    out = jnp.zeros((D_POOL, D_POOL), F32)
    for g in range(len(POOL_WINDOWS)):
        out = lax.dynamic_update_slice(out, w_pool[g], (g * POOL_GROUP, g * POOL_GROUP))
    return out


def _flat_pad(arrs, mult):
    flat = jnp.concatenate([a.reshape(-1) for a in arrs])
    pad = (-flat.shape[0]) % mult
    return jnp.pad(flat, (0, pad)) if pad else flat


def _unflat(flat, shapes):
    out, off = [], 0
    for shp in shapes:
        n = int(np.prod(shp))
        out.append(flat[off:off + n].reshape(shp))
        off += n
    return out


def _to_blocks(name, full):
    k, n = full.shape
    if name in ROW_SHARDED:
        return full.reshape(N_DEV, k // N_DEV, n)
    return full.reshape(k, N_DEV, n // N_DEV).transpose(1, 0, 2)


def _from_blocks(name, blocks):
    nb, r, c = blocks.shape
    if name in ROW_SHARDED:
        return blocks.reshape(nb * r, c)
    return blocks.transpose(1, 0, 2).reshape(r, nb * c)


class _Layer:
    pass


def _row(v):
    return v.reshape(1, -1)


def _layer_fwd(x, modr, w, wvec):
    sh_m, sc_m, g_m, sh_f, sc_f, g_f = modr
    (u,) = row_fwd("lnmod_mix", f_lnmod, [x], [sc_m, sh_m], [(D_MODEL, BF16)], 512)
    z = mm_big("mm_in", u, w.w_in, "nn", (512, 896, 1024), F32)
    p = pool_lin("pool_fwd", (z, D_POOL, Z_POOL // D_POOL), wvec, False, F32)
    ao = attn_fwd("attn_fwd", z, w.bias)
    cv = conv_fwd("conv_fwd", (z, 2 * D_CONV, Z_CONV // (2 * D_CONV)), w.conv_w)
    mparams = [w.wbd, w.ps, w.wbp, w.wba, w.wbc, w.cb, w.clg, w.clb, w.bg, w.wo, g_m, w.lmg, w.lmb]
    (x1,) = row_fwd("merge", f_merge, [p, ao, cv, (z, 3 * D_MODEL, 0), x], mparams, [(D_MODEL, F32)], 256)
    (u2,) = row_fwd("lnmod_ff", f_lnmod, [x1], [sc_f, sh_f], [(D_MODEL, BF16)], 512)
    hpre = mm_big("mm_ff1", u2, w.w_ff1, "nn", (512, 1024, 1024), F32)
    (h,) = row_fwd("relu2", f_relu2, [hpre], [w.b1], [(D_FF, BF16)], 512)
    ff = mm_big("mm_ff2", h, w.w_ff2, "nn", (512, 1024, 2048), F32)
    (x2,) = row_fwd("ffout", f_ffout, [x1, ff], [w.b2, g_f, w.lfg, w.lfb], [(D_MODEL, F32)], 512)
    return x2, (x, u, z, p, ao, cv, x1, u2, hpre, h, ff, mparams)


def _layer_bwd(dx2, saved, modr, w, wvec):
    x, u, z, p, ao, cv, x1, u2, hpre, h, ff, mparams = saved
    sh_m, sc_m, g_m, sh_f, sc_f, g_f = modr
    g = {}
    (dx1a, dff), (g["b_ff2"], dgf, g["ln_ff_g"], g["ln_ff_b"]) = row_bwd(
        "ffout_bwd", f_ffout, [x1, ff], [w.b2, g_f, w.lfg, w.lfb], [dx2], 512, [(0, F32), (1, BF16)], [0, 1, 2, 3])
    dh = mm_big("mm_dh", dff, w.w_ff2, "nt", (512, 1024, 1024), F32)
    g["w_ff2"] = mm_big("mm_dw_ff2", h, dff, "tn", (1024, 1024, 512), F32)
    (dhpre,), (g["b_ff1"],) = row_bwd("relu2_bwd", f_relu2, [hpre], [w.b1], [dh], 512, [(0, BF16)], [0])
    du2 = mm_big("mm_du2", dhpre, w.w_ff1, "nt", (512, 1024, 2048), F32)
    g["w_ff1"] = mm_big("mm_dw_ff1", u2, dhpre, "tn", (1024, 1024, 512), F32)
    (dx1,), (dscf, dshf) = row_bwd("lnmod_ff_bwd", f_lnmod, [x1], [sc_f, sh_f], [du2], 512, [(0, F32)], [0, 1],
                                   add_to=dx1a)
    (dp, dao, dcv, dzg, dxa), dm = row_bwd(
        "merge_bwd", f_merge, [p, ao, cv, (z, 3 * D_MODEL, 0), x], mparams, [dx1], 256,
        [(0, F32), (1, BF16), (2, F32), (3, BF16), (4, F32)], list(range(13)))
    (dwbd, g["pool_scale"], g["w_br_pool"], g["w_br_attn"], g["w_br_conv"], g["conv_b"], g["conv_ln_g"],
     g["conv_ln_b"], g["b_gate"], g["w_o"], dgm, g["ln_mix_g"], g["ln_mix_b"]) = dm
    g["w_pool"] = jnp.stack([dwbd[i * POOL_GROUP:(i + 1) * POOL_GROUP, i * POOL_GROUP:(i + 1) * POOL_GROUP]
                             for i in range(len(POOL_WINDOWS))])
    dzp = pool_lin("pool_bwd", (dp, D_POOL, 0), wvec, True, BF16)
    dq, dk, dv, dbias = attn_bwd("attn_bwd", z, w.bias, dao)
    (g["rel_bias"],) = w.bias_vjp(dbias)
    dzc, dcw = conv_bwd("conv_bwd", (z, 2 * D_CONV, Z_CONV // (2 * D_CONV)), w.conv_w, dcv)
    g["conv_w"] = dcw[:CONV_WIDTH]
    dz = jnp.concatenate([dzg, dzc, dzp, dq, dk, dv], axis=1)
    du = mm_big("mm_du", dz, w.w_in, "nt", (512, 1024, 896), F32)
    g["w_in"] = _unperm_cols(mm_big("mm_dw_in", u, dz, "tn", (1024, 896, 512), F32))
    (dx,), (dscm, dshm) = row_bwd("lnmod_mix_bwd", f_lnmod, [x], [sc_m, sh_m], [du], 512, [(0, F32)], [0, 1],
                                  add_to=dxa)
    dmod = jnp.concatenate([dshm, dscm, dgm, dshf, dscf, dgf], axis=1)
    return dx, dmod, g


def kernel(x, c, w_ada, b_ada, w_in, b_gate, w_pool, pool_scale, rel_bias, conv_w, conv_b, conv_ln_g, conv_ln_b, w_br_pool, w_br_attn, w_br_conv, w_o, ln_mix_g, ln_mix_b, w_ff1, b_ff1, w_ff2, b_ff2, ln_ff_g, ln_ff_b, loss_target, m_w_ada, m_b_ada, m_w_in, m_b_gate, m_w_pool, m_pool_scale, m_rel_bias, m_conv_w, m_conv_b, m_conv_ln_g, m_conv_ln_b, m_w_br_pool, m_w_br_attn, m_w_br_conv, m_w_o, m_ln_mix_g, m_ln_mix_b, m_w_ff1, m_b_ff1, m_w_ff2, m_b_ff2, m_ln_ff_g, m_ln_ff_b, v_w_ada, v_b_ada, v_w_in, v_b_gate, v_w_pool, v_pool_scale, v_rel_bias, v_conv_w, v_conv_b, v_conv_ln_g, v_conv_ln_b, v_w_br_pool, v_w_br_attn, v_w_br_conv, v_w_o, v_ln_mix_g, v_ln_mix_b, v_w_ff1, v_b_ff1, v_w_ff2, v_b_ff2, v_ln_ff_g, v_ln_ff_b):
    args = dict(locals())
    wts = {n: args[n] for n in WEIGHT_NAMES}
    mom = {n: args["m_" + n] for n in WEIGHT_NAMES}
    var = {n: args["v_" + n] for n in WEIGHT_NAMES}
    me = 4 * lax.axis_index("x") + 2 * lax.axis_index("y") + lax.axis_index("c")
    xs, tgt = x[0], loss_target[0]
    nc_ada = w_ada.shape[2]
    wvec = jnp.asarray(np.repeat(np.array(POOL_WINDOWS, np.float32), POOL_GROUP)[None, :])

    w_cat = jnp.concatenate([w_ada[l] for l in range(DEPTH)], axis=1)
    b_lay = b_ada.reshape(DEPTH, N_DEV, nc_ada).transpose(1, 0, 2).reshape(N_DEV, 1, DEPTH * nc_ada)
    land, cact = ada_fwd("ada_fwd", c, w_cat, b_lay)
    mod = land.reshape(N_DEV, DEPTH, nc_ada).transpose(1, 0, 2).reshape(DEPTH, 6 * D_MODEL)
    modr = [[mod[l:l + 1, i * D_MODEL:(i + 1) * D_MODEL] for i in range(6)] for l in range(DEPTH)]

    cw_pack = _flat_pad([conv_w], 8 * LANE).reshape(-1, LANE)
    gathered = all_gather_multi("gather_weights", [wts[n].astype(BF16) for n in BIG_NAMES] + [cw_pack])
    full = [{n: _from_blocks(n, g[:, l]) for n, g in zip(BIG_NAMES, gathered)} for l in range(DEPTH)]
    cw_all = gathered[-1].reshape(N_DEV, -1)[:, :conv_w.size]
    conv_full = cw_all.reshape((N_DEV,) + conv_w.shape).transpose(1, 2, 0, 3).reshape(DEPTH, CONV_WIDTH, D_CONV)

    layers = []
    for l in range(DEPTH):
        w = _Layer()
        w.w_in = _perm_cols(full[l]["w_in"])
        w.wbp, w.wba, w.wbc = full[l]["w_br_pool"], full[l]["w_br_attn"], full[l]["w_br_conv"]
        w.wo, w.w_ff1, w.w_ff2 = full[l]["w_o"], full[l]["w_ff1"], full[l]["w_ff2"]
        w.wbd = _block_diag(w_pool[l])
        w.ps, w.cb, w.clg, w.clb = _row(pool_scale[l]), _row(conv_b[l]), _row(conv_ln_g[l]), _row(conv_ln_b[l])
        w.bg, w.lmg, w.lmb = _row(b_gate[l]), _row(ln_mix_g[l]), _row(ln_mix_b[l])
        w.b1, w.b2, w.lfg, w.lfb = _row(b_ff1[l]), _row(b_ff2[l]), _row(ln_ff_g[l]), _row(ln_ff_b[l])
        w.conv_w = jnp.pad(conv_full[l], ((0, CONV_PAD - CONV_WIDTH), (0, 0)))
        w.bias, w.bias_vjp = jax.vjp(_bias_table, rel_bias[l])
        layers.append(w)

    h, saved = xs, []
    for l in range(DEPTH):
        h, sv = _layer_fwd(h, modr[l], layers[l], wvec)
        saved.append(sv)
    lpart, dy = loss_head("loss_head", h, tgt, 512)
    loss = lax.psum(lpart[0, 0], ("x", "y", "c"))
    grads, dmods = [None] * DEPTH, [None] * DEPTH
    for l in reversed(range(DEPTH)):
        dy, dmods[l], grads[l] = _layer_bwd(dy, saved[l], modr[l], layers[l], wvec)
    grad_x = dy[None]
    dmod = jnp.concatenate(dmods, axis=0)

    small_shapes = [wts[n].shape if n != "conv_w" else (DEPTH, CONV_WIDTH, D_CONV) for n in SMALL_NAMES]
    small_local = [dmod] + [jnp.stack([grads[l][n].reshape(shp[1:]) for l in range(DEPTH)])
                            for n, shp in zip(SMALL_NAMES[1:], small_shapes[1:])]
    small_pack = _flat_pad(small_local, 8 * LANE).reshape(-1, LANE)
    (small_all,) = all_gather_multi("gather_small_grads", [small_pack])
    small_sum = sum_blocks("sum_small_grads", small_all, small_pack.shape[0]).reshape(-1)
    gsmall = dict(zip(SMALL_NAMES, _unflat(small_sum, small_shapes)))
    gw = dict(gsmall)
    gw["conv_w"] = lax.dynamic_slice_in_dim(gsmall["conv_w"], me * conv_w.shape[2], conv_w.shape[2], axis=2)

    dmod_all = small_all.reshape(N_DEV, -1)[:, :dmod.size].reshape(N_DEV, DEPTH, N_DEV, nc_ada)
    dm_mine = lax.dynamic_index_in_dim(dmod_all, me, axis=2, keepdims=False).reshape(N_DEV, DEPTH * nc_ada)
    cact_t = jnp.pad(cact.T, ((0, 0), (0, LANE - N_DEV)))
    dm_pad = jnp.pad(dm_mine, ((0, LANE - N_DEV), (0, 0)))
    dw_cat = mm_big("mm_dw_ada", cact_t, dm_pad, "nn", (D_MODEL, DEPTH * nc_ada, LANE), F32)
    gw["w_ada"] = jnp.stack([dw_cat[:, l * nc_ada:(l + 1) * nc_ada] for l in range(DEPTH)])

    blocks = [jnp.stack([_to_blocks(n, grads[l][n]) for l in range(DEPTH)], axis=1).astype(BF16) for n in BIG_NAMES]
    recv = all_to_all_multi("scatter_grads", blocks)

    delta, new_m, new_v = {}, {}, {}
    for n, rb in zip(["w_ada"] + BIG_NAMES, [None] + list(recv)):
        shp = wts[n].shape
        two_d = lambda a, shp=shp: a.reshape(shp[0] * shp[1], shp[2])
        tr = 256
        if rb is None:
            res = (gw[n],) + tuple(adamw("adamw_" + n, two_d(wts[n]), two_d(gw[n]), two_d(mom[n]), two_d(var[n]), tr))
        else:
            res = adamw_sum("adamw_" + n, two_d(wts[n]), rb.reshape(N_DEV, shp[0] * shp[1], shp[2]), two_d(mom[n]),
                            two_d(var[n]), tr)
        gw[n], delta[n], new_m[n], new_v[n] = [a.reshape(shp) for a in res]
    packs = [_flat_pad([src[n] for n in SMALL_NAMES], 8 * LANE).reshape(-1, LANE) for src in (wts, gw, mom, var)]
    res = adamw("adamw_small", *packs, packs[0].shape[0])
    shapes = [wts[n].shape for n in SMALL_NAMES]
    for out, flat in zip((delta, new_m, new_v), res):
        out.update(dict(zip(SMALL_NAMES, _unflat(flat.reshape(-1), shapes))))

    return (loss, grad_x, *[gw[n] for n in WEIGHT_NAMES], *[delta[n] for n in WEIGHT_NAMES],
            *[new_m[n] for n in WEIGHT_NAMES], *[new_v[n] for n in WEIGHT_NAMES])
```

```python
import functools

import jax
import jax.numpy as jnp
import numpy as np
from jax import lax
from jax.experimental import pallas as pl
from jax.experimental.pallas import tpu as pltpu

F32 = jnp.float32
BF16 = jnp.bfloat16
MESH = pl.DeviceIdType.MESH

D_MODEL = 1024
DEPTH = 2
CHUNK = 64
N_HEADS = 8
HEAD_DIM = 64
D_POOL = 256
D_ATTN = 512
D_CONV = 256
CONV_WIDTH = 31
D_FF = 4096
D_IN = 5376
N_PREV = 8
BAND = (N_PREV + 1) * CHUNK
REL_CLIP = 128
ALPHA = (2.0 * DEPTH) ** 0.25
LN_EPS = 1e-5
NEG_INF = -1e30
N_DEV = 8

ADAM_LR, ADAM_B1, ADAM_B2, ADAM_EPS, ADAM_WD, ADAM_STEP = 0.001, 0.9, 0.999, 1e-08, 0.01, 10

VMEM_LIMIT = 56 * 1024 * 1024

Z_GATE, Z_CONV, Z_POOL, Z_Q, Z_K, Z_V = 0, 3072, 3584, 3840, 4352, 4864
ATT_TILE = 512
LANE = 128


def _dg(a, b, ca, cb):
    return lax.dot_general(a.astype(BF16), b.astype(BF16), (((ca,), (cb,)), ((), ())),
                           preferred_element_type=F32)


@jax.custom_vjp
def mm_nn(a, b):
    return _dg(a, b, 1, 0)


def _mm_nn_fwd(a, b):
    return _dg(a, b, 1, 0), (a, b)


def _mm_nn_bwd(res, g):
    a, b = res
    return _dg(g, b, 1, 1).astype(a.dtype), _dg(a, g, 0, 0).astype(b.dtype)


mm_nn.defvjp(_mm_nn_fwd, _mm_nn_bwd)


@jax.custom_vjp
def mm_nt(a, b):
    return _dg(a, b, 1, 1)


def _mm_nt_fwd(a, b):
    return _dg(a, b, 1, 1), (a, b)


def _mm_nt_bwd(res, g):
    a, b = res
    return _dg(g, b, 1, 0).astype(a.dtype), _dg(g, a, 0, 0).astype(b.dtype)


mm_nt.defvjp(_mm_nt_fwd, _mm_nt_bwd)


def _ln(x):
    mu = jnp.mean(x, axis=-1, keepdims=True)
    xc = x - mu
    var = jnp.mean(xc * xc, axis=-1, keepdims=True)
    return xc * lax.rsqrt(var + LN_EPS)


def _norm_rows(rows):
    return [r if isinstance(r, tuple) else (r, r.shape[1], 0) for r in rows]


def _row_spec(tm, r):
    _, width, cb = r
    return pl.BlockSpec((tm, width), lambda i, cb=cb: (i, cb))


def _full_spec(a):
    nd = a.ndim
    return pl.BlockSpec(a.shape, lambda i, nd=nd: (0,) * nd)


def row_fwd(name, f, rows, params, outs, tm):
    rows = _norm_rows(rows)
    s = rows[0][0].shape[0]
    nr, npar = len(rows), len(params)

    def body(*refs):
        r = [x[...].astype(F32) for x in refs[:nr]]
        p = [x[...] for x in refs[nr:nr + npar]]
        res = f(*r, *p)
        for o_ref, o in zip(refs[nr + npar:], res):
            o_ref[...] = o.astype(o_ref.dtype)

    return pl.pallas_call(
        body, name=name, grid=(s // tm,),
        in_specs=[_row_spec(tm, r) for r in rows] + [_full_spec(p) for p in params],
        out_specs=[pl.BlockSpec((tm, w), lambda i: (i, 0)) for w, _ in outs],
        out_shape=[jax.ShapeDtypeStruct((s, w), dt) for w, dt in outs],
        compiler_params=pltpu.CompilerParams(dimension_semantics=("parallel",), vmem_limit_bytes=VMEM_LIMIT),
    )(*[r[0] for r in rows], *params)


def row_bwd(name, f, rows, params, douts, tm, want_rows, want_params, add_to=None):
    rows = _norm_rows(rows)
    s = rows[0][0].shape[0]
    nr, npar, nd = len(rows), len(params), len(douts)
    nadd = 0 if add_to is None else 1
    n_in = nr + npar + nd + nadd

    def body(*refs):
        i = pl.program_id(0)
        r = [x[...].astype(F32) for x in refs[:nr]]
        p = [x[...].astype(F32) for x in refs[nr:nr + npar]]
        d = [x[...].astype(F32) for x in refs[nr + npar:nr + npar + nd]]
        _, vjp = jax.vjp(f, *r, *p)
        g = vjp(tuple(d))
        out_refs = refs[n_in:]
        for k, (idx, _) in enumerate(want_rows):
            val = g[idx]
            if nadd and k == 0:
                val = val + refs[n_in - 1][...].astype(F32)
            out_refs[k][...] = val.astype(out_refs[k].dtype)
        for k, idx in enumerate(want_params):
            gp = g[nr + idx]
            o_ref = out_refs[len(want_rows) + k]

            @pl.when(i == 0)
            def _():
                o_ref[...] = gp

            @pl.when(i > 0)
            def _():
                o_ref[...] += gp

    in_specs = ([_row_spec(tm, r) for r in rows] + [_full_spec(p) for p in params]
                + [pl.BlockSpec((tm, d.shape[1]), lambda i: (i, 0)) for d in douts])
    args = [r[0] for r in rows] + list(params) + list(douts)
    if nadd:
        in_specs.append(pl.BlockSpec((tm, add_to.shape[1]), lambda i: (i, 0)))
        args.append(add_to)
    out_specs = ([pl.BlockSpec((tm, rows[idx][1]), lambda i: (i, 0)) for idx, _ in want_rows]
                 + [_full_spec(params[idx]) for idx in want_params])
    out_shape = ([jax.ShapeDtypeStruct((s, rows[idx][1]), dt) for idx, dt in want_rows]
                 + [jax.ShapeDtypeStruct(params[idx].shape, F32) for idx in want_params])
    res = pl.pallas_call(
        body, name=name, grid=(s // tm,), in_specs=in_specs, out_specs=out_specs, out_shape=out_shape,
        compiler_params=pltpu.CompilerParams(dimension_semantics=("arbitrary",), vmem_limit_bytes=VMEM_LIMIT),
    )(*args)
    return res[:len(want_rows)], res[len(want_rows):]


def f_lnmod(x, sc, sh):
    return (_ln(x) * (1.0 + sc) + sh,)


def f_merge(p, ao, cv, zg, x, wbd, ps, wbp, wba, wbc, cb, clg, clb, bg, wo, gm, lg, lb):
    pm = mm_nn(p, wbd) * ps
    co = jax.nn.silu(_ln(cv + cb) * clg + clb)
    y_pool = mm_nn(pm, wbp)
    y_attn = mm_nn(ao, wba)
    y_conv = mm_nn(co, wbc)
    gates = jax.nn.sigmoid(zg + bg)
    merged = (gates[:, :D_MODEL] * y_pool + gates[:, D_MODEL:2 * D_MODEL] * y_attn
              + gates[:, 2 * D_MODEL:] * y_conv)
    mix = mm_nn(merged, wo)
    return (_ln(ALPHA * x + gm * mix) * lg + lb,)


def f_relu2(hpre, b1):
    a = jax.nn.relu(hpre + b1)
    return (a * a,)


def f_ffout(x1, ff, b2, gf, lg, lb):
    return (_ln(ALPHA * x1 + gf * (ff + b2)) * lg + lb,)


def mm_big(name, a, b, kind, tiles, out_dtype):
    t0, t1, tr = tiles
    if kind == "nn":
        (m, k), n = a.shape, b.shape[1]
        o0, o1, red = m, n, k
        a_spec = pl.BlockSpec((t0, tr), lambda i, j, r: (i, r))
        b_spec = pl.BlockSpec((tr, t1), lambda i, j, r: (r, j))
        dims = (1, 0)
    elif kind == "nt":
        (m, n), k = a.shape, b.shape[0]
        o0, o1, red = m, k, n
        a_spec = pl.BlockSpec((t0, tr), lambda i, j, r: (i, r))
        b_spec = pl.BlockSpec((t1, tr), lambda i, j, r: (j, r))
        dims = (1, 1)
    else:
        (m, k), n = a.shape, b.shape[1]
        o0, o1, red = k, n, m
        a_spec = pl.BlockSpec((tr, t0), lambda i, j, r: (r, i))
        b_spec = pl.BlockSpec((tr, t1), lambda i, j, r: (r, j))
        dims = (0, 0)
    assert o0 % t0 == 0 and o1 % t1 == 0 and red % tr == 0, (name, a.shape, b.shape, tiles)
    nred = red // tr

    def body(a_ref, b_ref, o_ref, acc_ref):
        r = pl.program_id(2)
        part = _dg(a_ref[...], b_ref[...], *dims)

        @pl.when(r == 0)
        def _():
            acc_ref[...] = part

        @pl.when(r > 0)
        def _():
            acc_ref[...] += part

        @pl.when(r == nred - 1)
        def _():
            o_ref[...] = acc_ref[...].astype(o_ref.dtype)

    return pl.pallas_call(
        body, name=name, grid=(o0 // t0, o1 // t1, nred),
        in_specs=[a_spec, b_spec],
        out_specs=pl.BlockSpec((t0, t1), lambda i, j, r: (i, j)),
        out_shape=jax.ShapeDtypeStruct((o0, o1), out_dtype),
        scratch_shapes=[pltpu.VMEM((t0, t1), F32)],
        compiler_params=pltpu.CompilerParams(dimension_semantics=("parallel", "parallel", "arbitrary"),
                                             vmem_limit_bytes=VMEM_LIMIT),
    )(a, b)


POOL_PAD = 16
POOL_ROWS = 256


def pool_lin(name, x, wvec, transpose, out_dtype):
    arr, width, cb = x
    s = arr.shape[0]
    n_steps = s // POOL_ROWS

    def body(x_ref, w_ref, o_ref, xp_ref):
        wv = w_ref[...]
        zeros = jnp.zeros((POOL_PAD, width), F32)
        xp_ref[0:POOL_PAD, :] = zeros
        xp_ref[s + POOL_PAD:s + 2 * POOL_PAD, :] = zeros

        def count(t0):
            t = lax.broadcasted_iota(jnp.int32, (POOL_ROWS, width), 0) + (t0 + 1)
            return jnp.minimum(t.astype(F32), wv)

        def fill(i, carry):
            t0 = pl.multiple_of(i * POOL_ROWS, POOL_ROWS)
            v = x_ref[pl.ds(t0, POOL_ROWS), :].astype(F32)
            if transpose:
                v = v / count(t0)
            xp_ref[pl.ds(t0 + POOL_PAD, POOL_ROWS), :] = v
            return carry

        lax.fori_loop(0, n_steps, fill, 0)

        def step(i, carry):
            t0 = pl.multiple_of(i * POOL_ROWS, POOL_ROWS)
            win = xp_ref[pl.ds(t0, POOL_ROWS + 2 * POOL_PAD), :]
            acc = jnp.zeros((POOL_ROWS, width), F32)
            for j in range(POOL_PAD):
                off = POOL_PAD + j if transpose else POOL_PAD - j
                acc = acc + jnp.where(wv > j, win[off:off + POOL_ROWS, :], 0.0)
            cur = x_ref[pl.ds(t0, POOL_ROWS), :].astype(F32)
            res = acc - cur if transpose else acc / count(t0) - cur
            o_ref[pl.ds(t0, POOL_ROWS), :] = res.astype(o_ref.dtype)
            return carry

        lax.fori_loop(0, n_steps, step, 0)

    return pl.pallas_call(
        body, name=name, grid=(1,),
        in_specs=[pl.BlockSpec((s, width), lambda i, cb=cb: (0, cb)), pl.BlockSpec((1, width), lambda i: (0, 0))],
        out_specs=pl.BlockSpec((s, width), lambda i: (0, 0)),
        out_shape=jax.ShapeDtypeStruct((s, width), out_dtype),
        scratch_shapes=[pltpu.VMEM((s + 2 * POOL_PAD, width), F32)],
        compiler_params=pltpu.CompilerParams(dimension_semantics=("arbitrary",), vmem_limit_bytes=VMEM_LIMIT),
    )(arr, wvec)


CONV_PAD = 32
CONV_ROWS = 128


def _glu(a, g):
    return a * jax.nn.sigmoid(g)


def conv_fwd(name, zc, w):
    arr, width, cb = zc
    s = arr.shape[0]
    n_steps = s // CONV_ROWS
    lead = CONV_PAD - (CONV_WIDTH - 1)

    def body(z_ref, w_ref, o_ref, hp_ref):
        hp_ref[0:CONV_PAD, :] = jnp.zeros((CONV_PAD, D_CONV), F32)

        def fill(i, carry):
            t0 = pl.multiple_of(i * CONV_ROWS, CONV_ROWS)
            z = z_ref[pl.ds(t0, CONV_ROWS), :]
            hp_ref[pl.ds(t0 + CONV_PAD, CONV_ROWS), :] = _glu(z[:, :D_CONV], z[:, D_CONV:])
            return carry

        lax.fori_loop(0, n_steps, fill, 0)
        wv = w_ref[...]

        def step(i, carry):
            t0 = pl.multiple_of(i * CONV_ROWS, CONV_ROWS)
            win = hp_ref[pl.ds(t0, CONV_ROWS + CONV_PAD), :]
            acc = jnp.zeros((CONV_ROWS, D_CONV), F32)
            for k in range(CONV_WIDTH):
                acc = acc + wv[k:k + 1, :] * win[lead + k:lead + k + CONV_ROWS, :]
            o_ref[pl.ds(t0, CONV_ROWS), :] = acc
            return carry

        lax.fori_loop(0, n_steps, step, 0)

    return pl.pallas_call(
        body, name=name, grid=(1,),
        in_specs=[pl.BlockSpec((s, width), lambda i, cb=cb: (0, cb)), pl.BlockSpec(w.shape, lambda i: (0, 0))],
        out_specs=pl.BlockSpec((s, D_CONV), lambda i: (0, 0)),
        out_shape=jax.ShapeDtypeStruct((s, D_CONV), F32),
        scratch_shapes=[pltpu.VMEM((s + CONV_PAD, D_CONV), F32)],
        compiler_params=pltpu.CompilerParams(dimension_semantics=("arbitrary",), vmem_limit_bytes=VMEM_LIMIT),
    )(arr, w)


def conv_bwd(name, zc, w, dout):
    arr, width, cb = zc
    s = arr.shape[0]
    n_steps = s // CONV_ROWS
    lead = CONV_PAD - (CONV_WIDTH - 1)

    def body(z_ref, w_ref, d_ref, dz_ref, dw_ref, hp_ref, dp_ref):
        hp_ref[0:CONV_PAD, :] = jnp.zeros((CONV_PAD, D_CONV), F32)
        dp_ref[s:s + CONV_PAD, :] = jnp.zeros((CONV_PAD, D_CONV), F32)
        dw_ref[...] = jnp.zeros(dw_ref.shape, F32)

        def fill(i, carry):
            t0 = pl.multiple_of(i * CONV_ROWS, CONV_ROWS)
            z = z_ref[pl.ds(t0, CONV_ROWS), :]
            hp_ref[pl.ds(t0 + CONV_PAD, CONV_ROWS), :] = _glu(z[:, :D_CONV], z[:, D_CONV:])
            dp_ref[pl.ds(t0, CONV_ROWS), :] = d_ref[pl.ds(t0, CONV_ROWS), :]
            return carry

        lax.fori_loop(0, n_steps, fill, 0)
        wv = w_ref[...]

        def step(i, carry):
            t0 = pl.multiple_of(i * CONV_ROWS, CONV_ROWS)
            hwin = hp_ref[pl.ds(t0, CONV_ROWS + CONV_PAD), :]
            dwin = dp_ref[pl.ds(t0, CONV_ROWS + CONV_PAD), :]
            dcur = dwin[0:CONV_ROWS, :]
            dh = jnp.zeros((CONV_ROWS, D_CONV), F32)
            rows = []
            for k in range(CONV_WIDTH):
                rows.append(jnp.sum(dcur * hwin[lead + k:lead + k + CONV_ROWS, :], axis=0, keepdims=True))
                back = CONV_WIDTH - 1 - k
                dh = dh + wv[k:k + 1, :] * dwin[back:back + CONV_ROWS, :]
            rows.append(jnp.zeros((1, D_CONV), F32))
            dw_ref[...] += jnp.concatenate(rows, axis=0)
            z = z_ref[pl.ds(t0, CONV_ROWS), :]
            _, vjp = jax.vjp(_glu, z[:, :D_CONV], z[:, D_CONV:])
            da, dg = vjp(dh)
            dz_ref[pl.ds(t0, CONV_ROWS), :] = jnp.concatenate([da, dg], axis=1).astype(dz_ref.dtype)
            return carry

        lax.fori_loop(0, n_steps, step, 0)

    return pl.pallas_call(
        body, name=name, grid=(1,),
        in_specs=[pl.BlockSpec((s, width), lambda i, cb=cb: (0, cb)), pl.BlockSpec(w.shape, lambda i: (0, 0)),
                  pl.BlockSpec((s, D_CONV), lambda i: (0, 0))],
        out_specs=[pl.BlockSpec((s, width), lambda i: (0, 0)), pl.BlockSpec(w.shape, lambda i: (0, 0))],
        out_shape=[jax.ShapeDtypeStruct((s, width), BF16), jax.ShapeDtypeStruct(w.shape, F32)],
        scratch_shapes=[pltpu.VMEM((s + CONV_PAD, D_CONV), F32), pltpu.VMEM((s + CONV_PAD, D_CONV), F32)],
        compiler_params=pltpu.CompilerParams(dimension_semantics=("arbitrary",), vmem_limit_bytes=VMEM_LIMIT),
    )(arr, w, dout)


HEADS_PER_STEP = LANE // HEAD_DIM
CHUNKS_PER_TILE = ATT_TILE // CHUNK


def _attn_tile(q, kp, kc, vp, vc, bias, first):
    kcat = jnp.concatenate([kp, kc], axis=0)
    vcat = jnp.concatenate([vp, vc], axis=0)
    lane = lax.broadcasted_iota(jnp.int32, (1, LANE), 1)
    kj = lax.broadcasted_iota(jnp.int32, (1, BAND), 1)
    outs = []
    for i in range(CHUNKS_PER_TILE):
        qi = q[i * CHUNK:(i + 1) * CHUNK, :] * (HEAD_DIM ** -0.5)
        kb = kcat[i * CHUNK:i * CHUNK + BAND, :]
        vb = vcat[i * CHUNK:i * CHUNK + BAND, :]
        valid = jnp.logical_or(jnp.logical_not(first), kj >= ATT_TILE - i * CHUNK)
        o = jnp.zeros((CHUNK, LANE), F32)
        for h in range(HEADS_PER_STEP):
            in_head = jnp.logical_and(lane >= h * HEAD_DIM, lane < (h + 1) * HEAD_DIM)
            sc = mm_nt(jnp.where(in_head, qi, 0.0), kb) + bias[h]
            sc = jnp.where(valid, sc, NEG_INF)
            m = jnp.max(sc, axis=-1, keepdims=True)
            e = jnp.exp(sc - lax.stop_gradient(m))
            p = e / jnp.sum(e, axis=-1, keepdims=True)
            o = o + jnp.where(in_head, mm_nn(p, vb), 0.0)
        outs.append(o)
    return jnp.concatenate(outs, axis=0)


def _z_spec(col0, fn):
    return pl.BlockSpec((ATT_TILE, LANE), lambda hp, n, col0=col0, fn=fn: (fn(n), col0 // LANE + hp))


def attn_fwd(name, z, bias):
    s = z.shape[0]
    nt = s // ATT_TILE

    def body(q_ref, kp_ref, kc_ref, vp_ref, vc_ref, b_ref, o_ref):
        first = pl.program_id(1) == 0
        o = _attn_tile(q_ref[...], kp_ref[...], kc_ref[...], vp_ref[...], vc_ref[...], b_ref[...], first)
        o_ref[...] = o.astype(o_ref.dtype)

    cur = lambda n: n
    prev = lambda n: jnp.maximum(n - 1, 0)
    return pl.pallas_call(
        body, name=name, grid=(N_HEADS // HEADS_PER_STEP, nt),
        in_specs=[_z_spec(Z_Q, cur), _z_spec(Z_K, prev), _z_spec(Z_K, cur), _z_spec(Z_V, prev), _z_spec(Z_V, cur),
                  pl.BlockSpec((HEADS_PER_STEP, CHUNK, BAND), lambda hp, n: (hp, 0, 0))],
        out_specs=pl.BlockSpec((ATT_TILE, LANE), lambda hp, n: (n, hp)),
        out_shape=jax.ShapeDtypeStruct((s, D_ATTN), BF16),
        compiler_params=pltpu.CompilerParams(dimension_semantics=("parallel", "parallel"), vmem_limit_bytes=VMEM_LIMIT),
    )(z, z, z, z, z, bias)


def attn_bwd(name, z, bias, do):
    s = z.shape[0]
    nt = s // ATT_TILE

    def body(q_ref, kp_ref, kc_ref, vp_ref, vc_ref, b_ref, do_ref, dq_ref, dk_ref, dv_ref, db_ref, kacc, vacc):
        n = pl.program_id(1)

        @pl.when(n == 0)
        def _():
            db_ref[...] = jnp.zeros(db_ref.shape, F32)

        @pl.when(n < nt)
        def _():
            first = n == 0
            fn = functools.partial(_attn_tile, first=first)
            _, vjp = jax.vjp(fn, q_ref[...], kp_ref[...], kc_ref[...], vp_ref[...], vc_ref[...], b_ref[...])
            dq, dkp, dkc, dvp, dvc, db = vjp(do_ref[...].astype(F32))
            dq_ref[...] = dq.astype(dq_ref.dtype)
            db_ref[...] += db

            @pl.when(n > 0)
            def _():
                dk_ref[...] = (kacc[...] + dkp).astype(dk_ref.dtype)
                dv_ref[...] = (vacc[...] + dvp).astype(dv_ref.dtype)

            kacc[...] = dkc
            vacc[...] = dvc

        @pl.when(n == nt)
        def _():
            dk_ref[...] = kacc[...].astype(dk_ref.dtype)
            dv_ref[...] = vacc[...].astype(dv_ref.dtype)

    cur = lambda n: jnp.minimum(n, nt - 1)
    prev = lambda n: jnp.clip(n - 1, 0, nt - 1)
    o_cur = pl.BlockSpec((ATT_TILE, LANE), lambda hp, n: (jnp.minimum(n, nt - 1), hp))
    o_prev = pl.BlockSpec((ATT_TILE, LANE), lambda hp, n: (jnp.maximum(n - 1, 0), hp))
    return pl.pallas_call(
        body, name=name, grid=(N_HEADS // HEADS_PER_STEP, nt + 1),
        in_specs=[_z_spec(Z_Q, cur), _z_spec(Z_K, prev), _z_spec(Z_K, cur), _z_spec(Z_V, prev), _z_spec(Z_V, cur),
                  pl.BlockSpec((HEADS_PER_STEP, CHUNK, BAND), lambda hp, n: (hp, 0, 0)), o_cur],
        out_specs=[o_cur, o_prev, o_prev, pl.BlockSpec((HEADS_PER_STEP, CHUNK, BAND), lambda hp, n: (hp, 0, 0))],
        out_shape=[jax.ShapeDtypeStruct((s, D_ATTN), BF16)] * 3 + [jax.ShapeDtypeStruct((N_HEADS, CHUNK, BAND), F32)],
        scratch_shapes=[pltpu.VMEM((ATT_TILE, LANE), F32), pltpu.VMEM((ATT_TILE, LANE), F32)],
        compiler_params=pltpu.CompilerParams(dimension_semantics=("parallel", "arbitrary"), vmem_limit_bytes=VMEM_LIMIT),
    )(z, z, z, z, z, bias, do)


def loss_head(name, y, tgt, tm):
    s, d = y.shape

    def body(y_ref, t_ref, l_ref, dy_ref):
        i = pl.program_id(0)
        diff = y_ref[...] - t_ref[...]
        dy_ref[...] = diff * (1.0 / d)
        part = 0.5 * jnp.sum(jnp.mean(diff * diff, axis=-1, keepdims=True), axis=0, keepdims=True)

        @pl.when(i == 0)
        def _():
            l_ref[...] = jnp.zeros(l_ref.shape, F32)

        l_ref[...] += jnp.broadcast_to(part, l_ref.shape)

    row = pl.BlockSpec((tm, d), lambda i: (i, 0))
    return pl.pallas_call(
        body, name=name, grid=(s // tm,), in_specs=[row, row],
        out_specs=[pl.BlockSpec((8, LANE), lambda i: (0, 0)), row],
        out_shape=[jax.ShapeDtypeStruct((8, LANE), F32), jax.ShapeDtypeStruct((s, d), F32)],
        compiler_params=pltpu.CompilerParams(dimension_semantics=("arbitrary",), vmem_limit_bytes=VMEM_LIMIT),
    )(y, tgt)


def adamw(name, w, g, m, v, tr):
    r, c = w.shape
    assert r % tr == 0, (name, w.shape, tr)

    def body(w_ref, g_ref, m_ref, v_ref, d_ref, nm_ref, nv_ref):
        gg = g_ref[...]
        m2 = ADAM_B1 * m_ref[...] + (1.0 - ADAM_B1) * gg
        v2 = ADAM_B2 * v_ref[...] + (1.0 - ADAM_B2) * (gg * gg)
        m_hat = m2 / (1.0 - ADAM_B1 ** ADAM_STEP)
        v_hat = v2 / (1.0 - ADAM_B2 ** ADAM_STEP)
        d_ref[...] = -ADAM_LR * (m_hat / (jnp.sqrt(v_hat) + ADAM_EPS) + ADAM_WD * w_ref[...])
        nm_ref[...] = m2
        nv_ref[...] = v2

    blk = pl.BlockSpec((tr, c), lambda i: (i, 0))
    return pl.pallas_call(
        body, name=name, grid=(r // tr,), in_specs=[blk] * 4, out_specs=[blk] * 3,
        out_shape=[jax.ShapeDtypeStruct((r, c), F32)] * 3,
        compiler_params=pltpu.CompilerParams(dimension_semantics=("parallel",), vmem_limit_bytes=VMEM_LIMIT),
    )(w, g, m, v)


def adamw_sum(name, w, layer_blocks, m, v, tr):
    rows, c = w.shape
    nl = len(layer_blocks)
    nb, r, _ = layer_blocks[0].shape
    assert rows == nl * r and r % tr == 0, (name, w.shape, layer_blocks[0].shape, tr)
    per = r // tr

    def body(*refs):
        w_ref, b_refs, (m_ref, v_ref, g_ref, d_ref, nm_ref, nv_ref) = refs[0], refs[1:1 + nl], refs[1 + nl:]
        i = pl.program_id(0)

        def update(b_ref):
            gg = b_ref[0].astype(F32)
            for j in range(1, nb):
                gg = gg + b_ref[j].astype(F32)
            g_ref[...] = gg
            m2 = ADAM_B1 * m_ref[...] + (1.0 - ADAM_B1) * gg
            v2 = ADAM_B2 * v_ref[...] + (1.0 - ADAM_B2) * (gg * gg)
            m_hat = m2 / (1.0 - ADAM_B1 ** ADAM_STEP)
            v_hat = v2 / (1.0 - ADAM_B2 ** ADAM_STEP)
            d_ref[...] = -ADAM_LR * (m_hat / (jnp.sqrt(v_hat) + ADAM_EPS) + ADAM_WD * w_ref[...])
            nm_ref[...] = m2
            nv_ref[...] = v2

        for l in range(nl):
            pl.when(jnp.logical_and(i >= l * per, i < (l + 1) * per))(functools.partial(update, b_refs[l]))

    blk = pl.BlockSpec((tr, c), lambda i: (i, 0))
    b_specs = [pl.BlockSpec((nb, tr, c), lambda i, l=l: (0, jnp.clip(i - l * per, 0, per - 1), 0)) for l in range(nl)]
    return pl.pallas_call(
        body, name=name, grid=(rows // tr,),
        in_specs=[blk] + b_specs + [blk, blk], out_specs=[blk] * 4,
        out_shape=[jax.ShapeDtypeStruct((rows, c), F32)] * 4,
        compiler_params=pltpu.CompilerParams(dimension_semantics=("arbitrary",), vmem_limit_bytes=VMEM_LIMIT),
    )(w, *layer_blocks, m, v)


def sum_blocks(name, blocks, tr):
    nb, r, c = blocks.shape
    assert r % tr == 0, (name, blocks.shape, tr)

    def body(b_ref, o_ref):
        acc = b_ref[0].astype(F32)
        for j in range(1, nb):
            acc = acc + b_ref[j].astype(F32)
        o_ref[...] = acc

    return pl.pallas_call(
        body, name=name, grid=(r // tr,),
        in_specs=[pl.BlockSpec((nb, tr, c), lambda i: (0, i, 0))],
        out_specs=pl.BlockSpec((tr, c), lambda i: (i, 0)),
        out_shape=jax.ShapeDtypeStruct((r, c), F32),
        compiler_params=pltpu.CompilerParams(dimension_semantics=("parallel",), vmem_limit_bytes=VMEM_LIMIT),
    )(blocks)


FLIPS = [(0, 0, 1), (1, 0, 0), (0, 1, 0), (1, 1, 0), (1, 0, 1), (0, 1, 1), (1, 1, 1)]
ANY = pl.BlockSpec(memory_space=pl.ANY)


def _me():
    return lax.axis_index("x"), lax.axis_index("y"), lax.axis_index("c")


def _flip(pos, f):
    return tuple((1 - p) if fi else p for p, fi in zip(pos, f))


def _idx(pos):
    return 4 * pos[0] + 2 * pos[1] + pos[2]


def all_gather_multi(name, shards):
    n = len(shards)

    def body(*refs):
        x_refs, out_refs = refs[:n], refs[n:2 * n]
        send_sems, recv_sems, local_sems = refs[2 * n:]
        x, y, cc = _me()
        me, sibling = (x, y, cc), (x, y, 1 - cc)
        chips = [(1 - x, y), (x, 1 - y), (1 - x, 1 - y)]

        def copy(a, k, block, to, src=None):
            dst = out_refs[a].at[_idx(block)]
            return pltpu.make_async_remote_copy(
                src_ref=dst if src is None else src, dst_ref=dst, send_sem=send_sems.at[7 * a + k],
                recv_sem=recv_sems.at[7 * a + k], device_id=to, device_id_type=MESH)

        mine = [pltpu.make_async_copy(x_refs[a], out_refs[a].at[_idx(me)], local_sems.at[a]) for a in range(n)]
        for cp in mine:
            cp.start()
        first = []
        for a in range(n):
            first.append(copy(a, 0, me, sibling, src=x_refs[a]))
            first += [copy(a, 1 + j, me, (*chip, cc), src=x_refs[a]) for j, chip in enumerate(chips)]
        for cp in first:
            cp.start()
        passed = []
        for j, chip in enumerate(chips):
            for a in range(n):
                copy(a, 1 + j, (*chip, cc), me).wait_recv()
                fwd = copy(a, 4 + j, (*chip, cc), sibling)
                fwd.start()
                passed.append(fwd)
        for a in range(n):
            copy(a, 0, sibling, me).wait_recv()
            for j, chip in enumerate(chips):
                copy(a, 4 + j, (*chip, 1 - cc), me).wait_recv()
        for cp in first + passed:
            cp.wait_send()
        for cp in mine:
            cp.wait()

    return pl.pallas_call(
        body, name=name, in_specs=[ANY] * n, out_specs=[ANY] * n,
        out_shape=[jax.ShapeDtypeStruct((N_DEV,) + a.shape, a.dtype) for a in shards],
        scratch_shapes=[pltpu.SemaphoreType.DMA((7 * n,)), pltpu.SemaphoreType.DMA((7 * n,)),
                        pltpu.SemaphoreType.DMA((n,))],
    )(*shards)


def all_to_all_multi(name, blocks):
    n = len(blocks)

    def body(*refs):
        in_refs, out_refs = refs[:n], refs[n:2 * n]
        send_sems, recv_sems, local_sems = refs[2 * n:]
        me = _me()
        mi = _idx(me)
        mine = [pltpu.make_async_copy(in_refs[a].at[mi], out_refs[a].at[mi], local_sems.at[a]) for a in range(n)]
        for cp in mine:
            cp.start()
        sends, recvs = [], []
        for k, f in enumerate(FLIPS):
            peer = _flip(me, f)
            pi = _idx(peer)
            for a in range(n):
                sems = dict(send_sem=send_sems.at[7 * a + k], recv_sem=recv_sems.at[7 * a + k],
                            device_id=peer, device_id_type=MESH)
                sends.append(pltpu.make_async_remote_copy(src_ref=in_refs[a].at[pi], dst_ref=out_refs[a].at[mi], **sems))
                recvs.append(pltpu.make_async_remote_copy(src_ref=in_refs[a].at[mi], dst_ref=out_refs[a].at[pi], **sems))
        for cp in sends:
            cp.start()
        for cp in recvs:
            cp.wait_recv()
        for cp in sends:
            cp.wait_send()
        for cp in mine:
            cp.wait()

    return pl.pallas_call(
        body, name=name, in_specs=[ANY] * n, out_specs=[ANY] * n,
        out_shape=[jax.ShapeDtypeStruct(a.shape, a.dtype) for a in blocks],
        scratch_shapes=[pltpu.SemaphoreType.DMA((7 * n,)), pltpu.SemaphoreType.DMA((7 * n,)),
                        pltpu.SemaphoreType.DMA((n,))],
    )(*blocks)


HBM = pl.BlockSpec(memory_space=pltpu.HBM)
SEM = pl.BlockSpec(memory_space=pltpu.SEMAPHORE)
DATAFLOW = pltpu.SideEffectType.DATAFLOW_SIDE_EFFECTING


def _exchange_copies(a_refs, l_refs, send_sems, recv_sems, gather):
    me = _me()
    mi = _idx(me)
    out = []
    for k, f in enumerate(FLIPS):
        peer = _flip(me, f)
        for a in range(len(a_refs)):
            src = a_refs[a] if gather else a_refs[a].at[_idx(peer)]
            out.append(pltpu.make_async_remote_copy(
                src_ref=src, dst_ref=l_refs[a].at[mi], send_sem=send_sems.at[7 * a + k],
                recv_sem=recv_sems.at[7 * a + k], device_id=peer, device_id_type=MESH))
    return out


def exchange_start(name, arrays, gather):
    n = len(arrays)
    lands = [lax.empty(((N_DEV,) + a.shape) if gather else a.shape, a.dtype) for a in arrays]

    def body(*refs):
        a_refs, l_refs = refs[:n], refs[n:2 * n]
        send_sems, recv_sems = refs[2 * n], refs[2 * n + 1]
        token = refs[4 * n + 2]
        for cp in _exchange_copies(a_refs, l_refs, send_sems, recv_sems, gather):
            cp.start()
        token[...] = jnp.zeros_like(token)

    hbm = lambda a: pltpu.HBM(a.shape, a.dtype)
    res = pl.pallas_call(
        body, name=name,
        out_shape=(pltpu.SemaphoreType.DMA((7 * n,)), pltpu.SemaphoreType.DMA((7 * n,)),
                   *[hbm(a) for a in arrays], *[hbm(a) for a in lands], jax.ShapeDtypeStruct((8, LANE), F32)),
        in_specs=[HBM] * (2 * n),
        out_specs=(SEM, SEM, *[HBM] * (2 * n), pl.BlockSpec(memory_space=pltpu.VMEM)),
        input_output_aliases={i: i + 2 for i in range(2 * n)},
        compiler_params=pltpu.CompilerParams(has_side_effects=DATAFLOW),
    )(*[pltpu.with_memory_space_constraint(a, pltpu.HBM) for a in arrays],
      *[pltpu.with_memory_space_constraint(a, pltpu.HBM) for a in lands])
    return res[0], res[1], list(res[2:2 + n]), list(res[2 + n:2 + 2 * n]), res[-1]


def exchange_wait(name, send_sems, recv_sems, arrays, lands, after, gather):
    n = len(arrays)

    def body(*refs):
        a_refs, l_refs = refs[:n], refs[n:2 * n]
        ssem, rsem = refs[2 * n], refs[2 * n + 1]
        for cp in _exchange_copies(a_refs, l_refs, ssem, rsem, gather):
            cp.wait_send()
            cp.wait_recv()

    hbm = lambda a: pltpu.HBM(a.shape, a.dtype)
    res = pl.pallas_call(
        body, name=name,
        out_shape=(*[hbm(a) for a in arrays], *[hbm(a) for a in lands]),
        in_specs=[HBM] * (2 * n) + [SEM, SEM, pl.BlockSpec(memory_space=pl.ANY)],
        out_specs=tuple([HBM] * (2 * n)),
        input_output_aliases={i: i for i in range(2 * n)},
        compiler_params=pltpu.CompilerParams(has_side_effects=DATAFLOW),
    )(*arrays, *lands, send_sems, recv_sems, after)
    return list(res[n:])


def ada_fwd(name, c_row, w_cat, b_lay):
    d = c_row.shape[1]
    ncol = w_cat.shape[1]
    vmem = pl.BlockSpec(memory_space=pltpu.VMEM)

    def body(c_ref, w_ref, b_ref, mod_ref, cact_ref, call, send, land, s1, r1, s2, r2):
        me = _me()
        mi = _idx(me)
        call[mi] = c_ref[...]

        def exchange(src_of, dst_buf, ssem, rsem):
            sends, recvs = [], []
            for k, f in enumerate(FLIPS):
                peer = _flip(me, f)
                sends.append(pltpu.make_async_remote_copy(
                    src_ref=src_of(peer), dst_ref=dst_buf.at[mi], send_sem=ssem.at[k], recv_sem=rsem.at[k],
                    device_id=peer, device_id_type=MESH))
                recvs.append(pltpu.make_async_remote_copy(
                    src_ref=src_of(peer), dst_ref=dst_buf.at[_idx(peer)], send_sem=ssem.at[k], recv_sem=rsem.at[k],
                    device_id=peer, device_id_type=MESH))
            for cp in sends:
                cp.start()
            for cp in recvs:
                cp.wait_recv()
            for cp in sends:
                cp.wait_send()

        exchange(lambda peer: c_ref, call, s1, r1)
        for p in range(N_DEV):
            cact_ref[pl.ds(p, 1), :] = jax.nn.silu(call[p])
        res = _dg(cact_ref[...], w_ref[...], 1, 0)
        for p in range(N_DEV):
            send[p] = res[p:p + 1, :]
        land[mi] = send[mi]
        exchange(lambda peer: send.at[_idx(peer)], land, s2, r2)
        mod_ref[...] = land[...] + b_ref[...]

    return pl.pallas_call(
        body, name=name, in_specs=[vmem, vmem, vmem], out_specs=[vmem, vmem],
        out_shape=[jax.ShapeDtypeStruct((N_DEV, 1, ncol), F32), jax.ShapeDtypeStruct((N_DEV, d), F32)],
        scratch_shapes=[pltpu.VMEM((N_DEV, 1, d), F32), pltpu.VMEM((N_DEV, 1, ncol), F32),
                        pltpu.VMEM((N_DEV, 1, ncol), F32),
                        pltpu.SemaphoreType.DMA((7,)), pltpu.SemaphoreType.DMA((7,)),
                        pltpu.SemaphoreType.DMA((7,)), pltpu.SemaphoreType.DMA((7,))],
        compiler_params=pltpu.CompilerParams(vmem_limit_bytes=VMEM_LIMIT),
    )(c_row, w_cat, b_lay)


POOL_WINDOWS = (2, 4, 8, 16)
POOL_GROUP = 64
N_REL = 2 * REL_CLIP + 1
PACK_COLS = 1024
SMALL_NAMES = ["b_ada", "b_gate", "w_pool", "pool_scale", "rel_bias", "conv_w", "conv_b", "conv_ln_g",
               "conv_ln_b", "ln_mix_g", "ln_mix_b", "b_ff1", "b_ff2", "ln_ff_g", "ln_ff_b"]
BIG_NAMES = ["w_in", "w_br_pool", "w_br_attn", "w_br_conv", "w_o", "w_ff1", "w_ff2"]
ROW_SHARDED = ("w_o", "w_ff2")
WEIGHT_NAMES = ["w_ada", "b_ada", "w_in", "b_gate", "w_pool", "pool_scale", "rel_bias", "conv_w", "conv_b",
                "conv_ln_g", "conv_ln_b", "w_br_pool", "w_br_attn", "w_br_conv", "w_o", "ln_mix_g", "ln_mix_b",
                "w_ff1", "b_ff1", "w_ff2", "b_ff2", "ln_ff_g", "ln_ff_b"]


def _perm_cols(w):
    return jnp.concatenate([w[:, 2304:], w[:, 1792:2304], w[:, :256], w[:, 256:768], w[:, 768:1280],
                            w[:, 1280:1792]], axis=1)


def _unperm_cols(wp):
    return jnp.concatenate([wp[:, Z_POOL:Z_Q], wp[:, Z_Q:Z_K], wp[:, Z_K:Z_V], wp[:, Z_V:],
                            wp[:, Z_CONV:Z_POOL], wp[:, :Z_CONV]], axis=1)


def _bias_table(rel_bias):
    far = jnp.broadcast_to(rel_bias[:, 2 * REL_CLIP:], (N_HEADS, BAND - REL_CLIP))
    near = rel_bias[:, REL_CLIP - CHUNK + 1:2 * REL_CLIP][:, ::-1]
    ext = jnp.concatenate([far, near], axis=1)
    return jnp.stack([ext[:, CHUNK - 1 - qi:CHUNK - 1 - qi + BAND] for qi in range(CHUNK)], axis=1)


def _block_diag(w_pool):
    out = jnp.zeros((D_POOL, D_POOL), F32)
    for g in range(len(POOL_WINDOWS)):
        out = lax.dynamic_update_slice(out, w_pool[g], (g * POOL_GROUP, g * POOL_GROUP))
    return out


def _flat_pad(arrs, mult):
    flat = jnp.concatenate([a.reshape(-1) for a in arrs])
    pad = (-flat.shape[0]) % mult
    return jnp.pad(flat, (0, pad)) if pad else flat


def _unflat(flat, shapes):
    out, off = [], 0
    for shp in shapes:
        n = int(np.prod(shp))
        out.append(flat[off:off + n].reshape(shp))
        off += n
    return out


def _to_blocks(name, full):
    k, n = full.shape
    if name in ROW_SHARDED:
        return full.reshape(N_DEV, k // N_DEV, n)
    return full.reshape(k, N_DEV, n // N_DEV).transpose(1, 0, 2)


def _from_blocks(name, blocks):
    nb, r, c = blocks.shape
    if name in ROW_SHARDED:
        return blocks.reshape(nb * r, c)
    return blocks.transpose(1, 0, 2).reshape(r, nb * c)


class _Layer:
    pass


def _row(v):
    return v.reshape(1, -1)


def _layer_fwd(x, modr, w, wvec):
    sh_m, sc_m, g_m, sh_f, sc_f, g_f = modr
    (u,) = row_fwd("lnmod_mix", f_lnmod, [x], [sc_m, sh_m], [(D_MODEL, BF16)], 512)
    z = mm_big("mm_in", u, w.w_in, "nn", (512, 896, 1024), F32)
    p = pool_lin("pool_fwd", (z, D_POOL, Z_POOL // D_POOL), wvec, False, F32)
    ao = attn_fwd("attn_fwd", z, w.bias)
    cv = conv_fwd("conv_fwd", (z, 2 * D_CONV, Z_CONV // (2 * D_CONV)), w.conv_w)
    mparams = [w.wbd, w.ps, w.wbp, w.wba, w.wbc, w.cb, w.clg, w.clb, w.bg, w.wo, g_m, w.lmg, w.lmb]
    (x1,) = row_fwd("merge", f_merge, [p, ao, cv, (z, 3 * D_MODEL, 0), x], mparams, [(D_MODEL, F32)], 256)
    (u2,) = row_fwd("lnmod_ff", f_lnmod, [x1], [sc_f, sh_f], [(D_MODEL, BF16)], 512)
    hpre = mm_big("mm_ff1", u2, w.w_ff1, "nn", (512, 1024, 1024), F32)
    (h,) = row_fwd("relu2", f_relu2, [hpre], [w.b1], [(D_FF, BF16)], 512)
    ff = mm_big("mm_ff2", h, w.w_ff2, "nn", (512, 1024, 2048), F32)
    (x2,) = row_fwd("ffout", f_ffout, [x1, ff], [w.b2, g_f, w.lfg, w.lfb], [(D_MODEL, F32)], 512)
    return x2, (x, u, z, p, ao, cv, x1, u2, hpre, h, ff, mparams)


def _layer_bwd(dx2, saved, modr, w, wvec):
    x, u, z, p, ao, cv, x1, u2, hpre, h, ff, mparams = saved
    sh_m, sc_m, g_m, sh_f, sc_f, g_f = modr
    g = {}
    (dx1a, dff), (g["b_ff2"], dgf, g["ln_ff_g"], g["ln_ff_b"]) = row_bwd(
        "ffout_bwd", f_ffout, [x1, ff], [w.b2, g_f, w.lfg, w.lfb], [dx2], 512, [(0, F32), (1, BF16)], [0, 1, 2, 3])
    dh = mm_big("mm_dh", dff, w.w_ff2, "nt", (512, 1024, 1024), F32)
    g["w_ff2"] = mm_big("mm_dw_ff2", h, dff, "tn", (1024, 1024, 512), F32)
    (dhpre,), (g["b_ff1"],) = row_bwd("relu2_bwd", f_relu2, [hpre], [w.b1], [dh], 512, [(0, BF16)], [0])
    du2 = mm_big("mm_du2", dhpre, w.w_ff1, "nt", (512, 1024, 2048), F32)
    g["w_ff1"] = mm_big("mm_dw_ff1", u2, dhpre, "tn", (1024, 1024, 512), F32)
    (dx1,), (dscf, dshf) = row_bwd("lnmod_ff_bwd", f_lnmod, [x1], [sc_f, sh_f], [du2], 512, [(0, F32)], [0, 1],
                                   add_to=dx1a)
    (dp, dao, dcv, dzg, dxa), dm = row_bwd(
        "merge_bwd", f_merge, [p, ao, cv, (z, 3 * D_MODEL, 0), x], mparams, [dx1], 256,
        [(0, F32), (1, BF16), (2, F32), (3, BF16), (4, F32)], list(range(13)))
    (dwbd, g["pool_scale"], g["w_br_pool"], g["w_br_attn"], g["w_br_conv"], g["conv_b"], g["conv_ln_g"],
     g["conv_ln_b"], g["b_gate"], g["w_o"], dgm, g["ln_mix_g"], g["ln_mix_b"]) = dm
    g["w_pool"] = jnp.stack([dwbd[i * POOL_GROUP:(i + 1) * POOL_GROUP, i * POOL_GROUP:(i + 1) * POOL_GROUP]
                             for i in range(len(POOL_WINDOWS))])
    dzp = pool_lin("pool_bwd", (dp, D_POOL, 0), wvec, True, BF16)
    dq, dk, dv, dbias = attn_bwd("attn_bwd", z, w.bias, dao)
    (g["rel_bias"],) = w.bias_vjp(dbias)
    dzc, dcw = conv_bwd("conv_bwd", (z, 2 * D_CONV, Z_CONV // (2 * D_CONV)), w.conv_w, dcv)
    g["conv_w"] = dcw[:CONV_WIDTH]
    dz = jnp.concatenate([dzg, dzc, dzp, dq, dk, dv], axis=1)
    du = mm_big("mm_du", dz, w.w_in, "nt", (512, 1024, 896), F32)
    g["w_in"] = _unperm_cols(mm_big("mm_dw_in", u, dz, "tn", (1024, 896, 512), F32))
    (dx,), (dscm, dshm) = row_bwd("lnmod_mix_bwd", f_lnmod, [x], [sc_m, sh_m], [du], 512, [(0, F32)], [0, 1],
                                  add_to=dxa)
    dmod = jnp.concatenate([dshm, dscm, dgm, dshf, dscf, dgf], axis=1)
    return dx, dmod, g


def kernel(x, c, w_ada, b_ada, w_in, b_gate, w_pool, pool_scale, rel_bias, conv_w, conv_b, conv_ln_g, conv_ln_b, w_br_pool, w_br_attn, w_br_conv, w_o, ln_mix_g, ln_mix_b, w_ff1, b_ff1, w_ff2, b_ff2, ln_ff_g, ln_ff_b, loss_target, m_w_ada, m_b_ada, m_w_in, m_b_gate, m_w_pool, m_pool_scale, m_rel_bias, m_conv_w, m_conv_b, m_conv_ln_g, m_conv_ln_b, m_w_br_pool, m_w_br_attn, m_w_br_conv, m_w_o, m_ln_mix_g, m_ln_mix_b, m_w_ff1, m_b_ff1, m_w_ff2, m_b_ff2, m_ln_ff_g, m_ln_ff_b, v_w_ada, v_b_ada, v_w_in, v_b_gate, v_w_pool, v_pool_scale, v_rel_bias, v_conv_w, v_conv_b, v_conv_ln_g, v_conv_ln_b, v_w_br_pool, v_w_br_attn, v_w_br_conv, v_w_o, v_ln_mix_g, v_ln_mix_b, v_w_ff1, v_b_ff1, v_w_ff2, v_b_ff2, v_ln_ff_g, v_ln_ff_b):
    args = dict(locals())
    wts = {n: args[n] for n in WEIGHT_NAMES}
    mom = {n: args["m_" + n] for n in WEIGHT_NAMES}
    var = {n: args["v_" + n] for n in WEIGHT_NAMES}
    me = 4 * lax.axis_index("x") + 2 * lax.axis_index("y") + lax.axis_index("c")
    xs, tgt = x[0], loss_target[0]
    nc_ada = w_ada.shape[2]
    wvec = jnp.asarray(np.repeat(np.array(POOL_WINDOWS, np.float32), POOL_GROUP)[None, :])

    w_cat = jnp.concatenate([w_ada[l] for l in range(DEPTH)], axis=1)
    b_lay = b_ada.reshape(DEPTH, N_DEV, nc_ada).transpose(1, 0, 2).reshape(N_DEV, 1, DEPTH * nc_ada)
    land, cact = ada_fwd("ada_fwd", c, w_cat, b_lay)
    mod = land.reshape(N_DEV, DEPTH, nc_ada).transpose(1, 0, 2).reshape(DEPTH, 6 * D_MODEL)
    modr = [[mod[l:l + 1, i * D_MODEL:(i + 1) * D_MODEL] for i in range(6)] for l in range(DEPTH)]

    cw_pack = _flat_pad([conv_w], 8 * LANE).reshape(-1, LANE)
    shards = [[wts[n][l].astype(BF16) for n in BIG_NAMES] for l in range(DEPTH)]
    gathered0 = all_gather_multi("gather_weights_l0", shards[0] + [cw_pack])
    cw_all = gathered0[-1].reshape(N_DEV, -1)[:, :conv_w.size]
    conv_full = cw_all.reshape((N_DEV,) + conv_w.shape).transpose(1, 2, 0, 3).reshape(DEPTH, CONV_WIDTH, D_CONV)
    ssem, rsem, thru, lands, token = exchange_start("gather_weights_l1_start", shards[1], True)
    modr[0] = [r + token[0:1, 0:1] for r in modr[0]]

    def layer_weights(l, blocks):
        full = {n: _from_blocks(n, g) for n, g in zip(BIG_NAMES, blocks)}
        w = _Layer()
        w.w_in = _perm_cols(full["w_in"])
        w.wbp, w.wba, w.wbc = full["w_br_pool"], full["w_br_attn"], full["w_br_conv"]
        w.wo, w.w_ff1, w.w_ff2 = full["w_o"], full["w_ff1"], full["w_ff2"]
        w.wbd = _block_diag(w_pool[l])
        w.ps, w.cb, w.clg, w.clb = _row(pool_scale[l]), _row(conv_b[l]), _row(conv_ln_g[l]), _row(conv_ln_b[l])
        w.bg, w.lmg, w.lmb = _row(b_gate[l]), _row(ln_mix_g[l]), _row(ln_mix_b[l])
        w.b1, w.b2, w.lfg, w.lfb = _row(b_ff1[l]), _row(b_ff2[l]), _row(ln_ff_g[l]), _row(ln_ff_b[l])
        w.conv_w = jnp.pad(conv_full[l], ((0, CONV_PAD - CONV_WIDTH), (0, 0)))
        w.bias, w.bias_vjp = jax.vjp(_bias_table, rel_bias[l])
        return w

    def with_own(lands_, own):
        return [lax.dynamic_update_index_in_dim(ld, o, me, axis=0) for ld, o in zip(lands_, own)]

    layers, saved = [layer_weights(0, gathered0[:-1])], []
    h, sv = _layer_fwd(xs, modr[0], layers[0], wvec)
    saved.append(sv)
    lands = exchange_wait("gather_weights_l1_wait", ssem, rsem, thru, lands, h, True)
    layers.append(layer_weights(1, with_own(lands, shards[1])))
    h, sv = _layer_fwd(h, modr[1], layers[1], wvec)
    saved.append(sv)
    lpart, dy = loss_head("loss_head", h, tgt, 512)
    loss = lax.psum(lpart[0, 0], ("x", "y", "c"))
    grads, dmods = [None] * DEPTH, [None] * DEPTH
    dy, dmods[1], grads[1] = _layer_bwd(dy, saved[1], modr[1], layers[1], wvec)

    gblocks = lambda l: [_to_blocks(n, grads[l][n]).astype(BF16) for n in BIG_NAMES]
    blocks1 = gblocks(1)
    ssem, rsem, thru, lands, token = exchange_start("scatter_grads_l1_start", blocks1, False)
    modr0_b = [r + token[0:1, 0:1] for r in modr[0]]
    dy, dmods[0], grads[0] = _layer_bwd(dy, saved[0], modr0_b, layers[0], wvec)
    lands = exchange_wait("scatter_grads_l1_wait", ssem, rsem, thru, lands, dy, False)
    recv1 = with_own(lands, [lax.dynamic_index_in_dim(b, me, axis=0, keepdims=False) for b in blocks1])
    recv0 = all_to_all_multi("scatter_grads_l0", gblocks(0))
    grad_x = dy[None]
    dmod = jnp.concatenate(dmods, axis=0)

    small_shapes = [wts[n].shape if n != "conv_w" else (DEPTH, CONV_WIDTH, D_CONV) for n in SMALL_NAMES]
    small_local = [dmod] + [jnp.stack([grads[l][n].reshape(shp[1:]) for l in range(DEPTH)])
                            for n, shp in zip(SMALL_NAMES[1:], small_shapes[1:])]
    small_pack = _flat_pad(small_local, 8 * LANE).reshape(-1, LANE)
    (small_all,) = all_gather_multi("gather_small_grads", [small_pack])
    small_sum = sum_blocks("sum_small_grads", small_all, small_pack.shape[0]).reshape(-1)
    gsmall = dict(zip(SMALL_NAMES, _unflat(small_sum, small_shapes)))
    gw = dict(gsmall)
    gw["conv_w"] = lax.dynamic_slice_in_dim(gsmall["conv_w"], me * conv_w.shape[2], conv_w.shape[2], axis=2)

    dmod_all = small_all.reshape(N_DEV, -1)[:, :dmod.size].reshape(N_DEV, DEPTH, N_DEV, nc_ada)
    dm_mine = lax.dynamic_index_in_dim(dmod_all, me, axis=2, keepdims=False).reshape(N_DEV, DEPTH * nc_ada)
    cact_t = jnp.pad(cact.T, ((0, 0), (0, LANE - N_DEV)))
    dm_pad = jnp.pad(dm_mine, ((0, LANE - N_DEV), (0, 0)))
    dw_cat = mm_big("mm_dw_ada", cact_t, dm_pad, "nn", (D_MODEL, DEPTH * nc_ada, LANE), F32)
    gw["w_ada"] = jnp.stack([dw_cat[:, l * nc_ada:(l + 1) * nc_ada] for l in range(DEPTH)])

    delta, new_m, new_v = {}, {}, {}
    for i, n in enumerate(["w_ada"] + BIG_NAMES):
        shp = wts[n].shape
        two_d = lambda a, shp=shp: a.reshape(shp[0] * shp[1], shp[2])
        tr = min(256, shp[1])
        if n == "w_ada":
            res = (gw[n],) + tuple(adamw("adamw_" + n, two_d(wts[n]), two_d(gw[n]), two_d(mom[n]), two_d(var[n]), tr))
        else:
            res = adamw_sum("adamw_" + n, two_d(wts[n]), [recv0[i - 1], recv1[i - 1]], two_d(mom[n]), two_d(var[n]), tr)
        gw[n], delta[n], new_m[n], new_v[n] = [a.reshape(shp) for a in res]
    packs = [_flat_pad([src[n] for n in SMALL_NAMES], 8 * LANE).reshape(-1, LANE) for src in (wts, gw, mom, var)]
    res = adamw("adamw_small", *packs, packs[0].shape[0])
    shapes = [wts[n].shape for n in SMALL_NAMES]
    for out, flat in zip((delta, new_m, new_v), res):
        out.update(dict(zip(SMALL_NAMES, _unflat(flat.reshape(-1), shapes))))

    return (loss, grad_x, *[gw[n] for n in WEIGHT_NAMES], *[delta[n] for n in WEIGHT_NAMES],
            *[new_m[n] for n in WEIGHT_NAMES], *[new_v[n] for n in WEIGHT_NAMES])
```

```python
import functools

import jax
import jax.numpy as jnp
import numpy as np
from jax import lax
from jax.experimental import pallas as pl
from jax.experimental.pallas import tpu as pltpu

F32 = jnp.float32
BF16 = jnp.bfloat16
MESH = pl.DeviceIdType.MESH

D_MODEL = 1024
DEPTH = 2
CHUNK = 64
N_HEADS = 8
HEAD_DIM = 64
D_POOL = 256
D_ATTN = 512
D_CONV = 256
CONV_WIDTH = 31
D_FF = 4096
D_IN = 5376
N_PREV = 8
BAND = (N_PREV + 1) * CHUNK
REL_CLIP = 128
ALPHA = (2.0 * DEPTH) ** 0.25
LN_EPS = 1e-5
NEG_INF = -1e30
N_DEV = 8

ADAM_LR, ADAM_B1, ADAM_B2, ADAM_EPS, ADAM_WD, ADAM_STEP = 0.001, 0.9, 0.999, 1e-08, 0.01, 10

VMEM_LIMIT = 56 * 1024 * 1024

Z_GATE, Z_CONV, Z_POOL, Z_Q, Z_K, Z_V = 0, 3072, 3584, 3840, 4352, 4864
ATT_TILE = 512
LANE = 128


def _dg(a, b, ca, cb):
    return lax.dot_general(a.astype(BF16), b.astype(BF16), (((ca,), (cb,)), ((), ())),
                           preferred_element_type=F32)


@jax.custom_vjp
def mm_nn(a, b):
    return _dg(a, b, 1, 0)


def _mm_nn_fwd(a, b):
    return _dg(a, b, 1, 0), (a, b)


def _mm_nn_bwd(res, g):
    a, b = res
    return _dg(g, b, 1, 1).astype(a.dtype), _dg(a, g, 0, 0).astype(b.dtype)


mm_nn.defvjp(_mm_nn_fwd, _mm_nn_bwd)


@jax.custom_vjp
def mm_nt(a, b):
    return _dg(a, b, 1, 1)


def _mm_nt_fwd(a, b):
    return _dg(a, b, 1, 1), (a, b)


def _mm_nt_bwd(res, g):
    a, b = res
    return _dg(g, b, 1, 0).astype(a.dtype), _dg(g, a, 0, 0).astype(b.dtype)


mm_nt.defvjp(_mm_nt_fwd, _mm_nt_bwd)


def _ln(x):
    mu = jnp.mean(x, axis=-1, keepdims=True)
    xc = x - mu
    var = jnp.mean(xc * xc, axis=-1, keepdims=True)
    return xc * lax.rsqrt(var + LN_EPS)


def _norm_rows(rows):
    return [r if isinstance(r, tuple) else (r, r.shape[1], 0) for r in rows]


def _row_spec(tm, r):
    _, width, cb = r
    return pl.BlockSpec((tm, width), lambda i, cb=cb: (i, cb))


def _full_spec(a):
    nd = a.ndim
    return pl.BlockSpec(a.shape, lambda i, nd=nd: (0,) * nd)


def row_fwd(name, f, rows, params, outs, tm):
    rows = _norm_rows(rows)
    s = rows[0][0].shape[0]
    nr, npar = len(rows), len(params)

    def body(*refs):
        r = [x[...].astype(F32) for x in refs[:nr]]
        p = [x[...] for x in refs[nr:nr + npar]]
        res = f(*r, *p)
        for o_ref, o in zip(refs[nr + npar:], res):
            o_ref[...] = o.astype(o_ref.dtype)

    return pl.pallas_call(
        body, name=name, grid=(s // tm,),
        in_specs=[_row_spec(tm, r) for r in rows] + [_full_spec(p) for p in params],
        out_specs=[pl.BlockSpec((tm, w), lambda i: (i, 0)) for w, _ in outs],
        out_shape=[jax.ShapeDtypeStruct((s, w), dt) for w, dt in outs],
        compiler_params=pltpu.CompilerParams(dimension_semantics=("parallel",), vmem_limit_bytes=VMEM_LIMIT),
    )(*[r[0] for r in rows], *params)


def row_bwd(name, f, rows, params, douts, tm, want_rows, want_params, add_to=None):
    rows = _norm_rows(rows)
    s = rows[0][0].shape[0]
    nr, npar, nd = len(rows), len(params), len(douts)
    nadd = 0 if add_to is None else 1
    n_in = nr + npar + nd + nadd

    def body(*refs):
        i = pl.program_id(0)
        r = [x[...].astype(F32) for x in refs[:nr]]
        p = [x[...].astype(F32) for x in refs[nr:nr + npar]]
        d = [x[...].astype(F32) for x in refs[nr + npar:nr + npar + nd]]
        _, vjp = jax.vjp(f, *r, *p)
        g = vjp(tuple(d))
        out_refs = refs[n_in:]
        for k, (idx, _) in enumerate(want_rows):
            val = g[idx]
            if nadd and k == 0:
                val = val + refs[n_in - 1][...].astype(F32)
            out_refs[k][...] = val.astype(out_refs[k].dtype)
        for k, idx in enumerate(want_params):
            gp = g[nr + idx]
            o_ref = out_refs[len(want_rows) + k]

            @pl.when(i == 0)
            def _():
                o_ref[...] = gp

            @pl.when(i > 0)
            def _():
                o_ref[...] += gp

    in_specs = ([_row_spec(tm, r) for r in rows] + [_full_spec(p) for p in params]
                + [pl.BlockSpec((tm, d.shape[1]), lambda i: (i, 0)) for d in douts])
    args = [r[0] for r in rows] + list(params) + list(douts)
    if nadd:
        in_specs.append(pl.BlockSpec((tm, add_to.shape[1]), lambda i: (i, 0)))
        args.append(add_to)
    out_specs = ([pl.BlockSpec((tm, rows[idx][1]), lambda i: (i, 0)) for idx, _ in want_rows]
                 + [_full_spec(params[idx]) for idx in want_params])
    out_shape = ([jax.ShapeDtypeStruct((s, rows[idx][1]), dt) for idx, dt in want_rows]
                 + [jax.ShapeDtypeStruct(params[idx].shape, F32) for idx in want_params])
    res = pl.pallas_call(
        body, name=name, grid=(s // tm,), in_specs=in_specs, out_specs=out_specs, out_shape=out_shape,
        compiler_params=pltpu.CompilerParams(dimension_semantics=("arbitrary",), vmem_limit_bytes=VMEM_LIMIT),
    )(*args)
    return res[:len(want_rows)], res[len(want_rows):]


def f_lnmod(x, sc, sh):
    return (_ln(x) * (1.0 + sc) + sh,)


def f_merge(p, ao, cv, zg, x, wbd, ps, wbp, wba, wbc, cb, clg, clb, bg, wo, gm, lg, lb):
    pm = mm_nn(p, wbd) * ps
    co = jax.nn.silu(_ln(cv + cb) * clg + clb)
    y_pool = mm_nn(pm, wbp)
    y_attn = mm_nn(ao, wba)
    y_conv = mm_nn(co, wbc)
    gates = jax.nn.sigmoid(zg + bg)
    merged = (gates[:, :D_MODEL] * y_pool + gates[:, D_MODEL:2 * D_MODEL] * y_attn
              + gates[:, 2 * D_MODEL:] * y_conv)
    mix = mm_nn(merged, wo)
    return (_ln(ALPHA * x + gm * mix) * lg + lb,)


def f_relu2(hpre, b1):
    a = jax.nn.relu(hpre + b1)
    return (a * a,)


def f_ffout(x1, ff, b2, gf, lg, lb):
    return (_ln(ALPHA * x1 + gf * (ff + b2)) * lg + lb,)


def mm_big(name, a, b, kind, tiles, out_dtype, j_outer=False):
    if kind == "nn":
        o0, o1, red = a.shape[0], b.shape[1], a.shape[1]
    elif kind == "nt":
        o0, o1, red = a.shape[0], b.shape[0], a.shape[1]
    else:
        o0, o1, red = a.shape[1], b.shape[1], a.shape[0]
    t0, t1, tr = min(tiles[0], o0), min(tiles[1], o1), min(tiles[2], red)
    if kind == "nn":
        a_spec = pl.BlockSpec((t0, tr), lambda i, j, r: (i, r))
        b_spec = pl.BlockSpec((tr, t1), lambda i, j, r: (r, j))
        dims = (1, 0)
    elif kind == "nt":
        a_spec = pl.BlockSpec((t0, tr), lambda i, j, r: (i, r))
        b_spec = pl.BlockSpec((t1, tr), lambda i, j, r: (j, r))
        dims = (1, 1)
    else:
        a_spec = pl.BlockSpec((tr, t0), lambda i, j, r: (r, i))
        b_spec = pl.BlockSpec((tr, t1), lambda i, j, r: (r, j))
        dims = (0, 0)
    assert o0 % t0 == 0 and o1 % t1 == 0 and red % tr == 0, (name, a.shape, b.shape, tiles)
    n0, n1, nred = o0 // t0, o1 // t1, red // tr
    o_spec = pl.BlockSpec((t0, t1), lambda i, j, r: (i, j))
    if j_outer:
        swap = lambda spec: pl.BlockSpec(spec.block_shape, lambda j, i, r, f=spec.index_map: f(i, j, r))
        a_spec, b_spec, o_spec = swap(a_spec), swap(b_spec), swap(o_spec)
        grid = (n1, n0, nred)
    else:
        grid = (n0, n1, nred)

    if nred == 1:
        def body(a_ref, b_ref, o_ref):
            o_ref[...] = _dg(a_ref[...], b_ref[...], *dims).astype(o_ref.dtype)
        scratch = []
    else:
        def body(a_ref, b_ref, o_ref, acc_ref):
            r = pl.program_id(2)
            part = _dg(a_ref[...], b_ref[...], *dims)

            @pl.when(r == 0)
            def _():
                acc_ref[...] = part

            @pl.when(jnp.logical_and(r > 0, r < nred - 1))
            def _():
                acc_ref[...] += part

            @pl.when(r == nred - 1)
            def _():
                o_ref[...] = (acc_ref[...] + part).astype(o_ref.dtype)
        scratch = [pltpu.VMEM((t0, t1), F32)]

    return pl.pallas_call(
        body, name=name, grid=grid,
        in_specs=[a_spec, b_spec],
        out_specs=o_spec,
        out_shape=jax.ShapeDtypeStruct((o0, o1), out_dtype),
        scratch_shapes=scratch,
        compiler_params=pltpu.CompilerParams(dimension_semantics=("parallel", "parallel", "arbitrary"),
                                             vmem_limit_bytes=VMEM_LIMIT),
    )(a, b)


POOL_PAD = 16
POOL_ROWS = 256


def pool_lin(name, x, wvec, transpose, out_dtype):
    arr, width, cb = x
    s = arr.shape[0]
    n_steps = s // POOL_ROWS

    def body(x_ref, w_ref, o_ref, xp_ref):
        wv = w_ref[...]
        zeros = jnp.zeros((POOL_PAD, width), F32)
        xp_ref[0:POOL_PAD, :] = zeros
        xp_ref[s + POOL_PAD:s + 2 * POOL_PAD, :] = zeros

        def count(t0):
            t = lax.broadcasted_iota(jnp.int32, (POOL_ROWS, width), 0) + (t0 + 1)
            return jnp.minimum(t.astype(F32), wv)

        def fill(i, carry):
            t0 = pl.multiple_of(i * POOL_ROWS, POOL_ROWS)
            v = x_ref[pl.ds(t0, POOL_ROWS), :].astype(F32)
            if transpose:
                v = v / count(t0)
            xp_ref[pl.ds(t0 + POOL_PAD, POOL_ROWS), :] = v
            return carry

        lax.fori_loop(0, n_steps, fill, 0)

        def step(i, carry):
            t0 = pl.multiple_of(i * POOL_ROWS, POOL_ROWS)
            win = xp_ref[pl.ds(t0, POOL_ROWS + 2 * POOL_PAD), :]
            acc = jnp.zeros((POOL_ROWS, width), F32)
            for j in range(POOL_PAD):
                off = POOL_PAD + j if transpose else POOL_PAD - j
                acc = acc + jnp.where(wv > j, win[off:off + POOL_ROWS, :], 0.0)
            cur = x_ref[pl.ds(t0, POOL_ROWS), :].astype(F32)
            res = acc - cur if transpose else acc / count(t0) - cur
            o_ref[pl.ds(t0, POOL_ROWS), :] = res.astype(o_ref.dtype)
            return carry

        lax.fori_loop(0, n_steps, step, 0)

    return pl.pallas_call(
        body, name=name, grid=(1,),
        in_specs=[pl.BlockSpec((s, width), lambda i, cb=cb: (0, cb)), pl.BlockSpec((1, width), lambda i: (0, 0))],
        out_specs=pl.BlockSpec((s, width), lambda i: (0, 0)),
        out_shape=jax.ShapeDtypeStruct((s, width), out_dtype),
        scratch_shapes=[pltpu.VMEM((s + 2 * POOL_PAD, width), F32)],
        compiler_params=pltpu.CompilerParams(dimension_semantics=("arbitrary",), vmem_limit_bytes=VMEM_LIMIT),
    )(arr, wvec)


CONV_PAD = 32
CONV_ROWS = 128


def _glu(a, g):
    return a * jax.nn.sigmoid(g)


def conv_fwd(name, zc, w):
    arr, width, cb = zc
    s = arr.shape[0]
    n_steps = s // CONV_ROWS
    lead = CONV_PAD - (CONV_WIDTH - 1)

    def body(z_ref, w_ref, o_ref, hp_ref):
        hp_ref[0:CONV_PAD, :] = jnp.zeros((CONV_PAD, D_CONV), F32)

        def fill(i, carry):
            t0 = pl.multiple_of(i * CONV_ROWS, CONV_ROWS)
            z = z_ref[pl.ds(t0, CONV_ROWS), :]
            hp_ref[pl.ds(t0 + CONV_PAD, CONV_ROWS), :] = _glu(z[:, :D_CONV], z[:, D_CONV:])
            return carry

        lax.fori_loop(0, n_steps, fill, 0)
        wv = w_ref[...]

        def step(i, carry):
            t0 = pl.multiple_of(i * CONV_ROWS, CONV_ROWS)
            win = hp_ref[pl.ds(t0, CONV_ROWS + CONV_PAD), :]
            acc = jnp.zeros((CONV_ROWS, D_CONV), F32)
            for k in range(CONV_WIDTH):
                acc = acc + wv[k:k + 1, :] * win[lead + k:lead + k + CONV_ROWS, :]
            o_ref[pl.ds(t0, CONV_ROWS), :] = acc
            return carry

        lax.fori_loop(0, n_steps, step, 0)

    return pl.pallas_call(
        body, name=name, grid=(1,),
        in_specs=[pl.BlockSpec((s, width), lambda i, cb=cb: (0, cb)), pl.BlockSpec(w.shape, lambda i: (0, 0))],
        out_specs=pl.BlockSpec((s, D_CONV), lambda i: (0, 0)),
        out_shape=jax.ShapeDtypeStruct((s, D_CONV), F32),
        scratch_shapes=[pltpu.VMEM((s + CONV_PAD, D_CONV), F32)],
        compiler_params=pltpu.CompilerParams(dimension_semantics=("arbitrary",), vmem_limit_bytes=VMEM_LIMIT),
    )(arr, w)


def conv_bwd(name, zc, w, dout):
    arr, width, cb = zc
    s = arr.shape[0]
    n_steps = s // CONV_ROWS
    lead = CONV_PAD - (CONV_WIDTH - 1)

    def body(z_ref, w_ref, d_ref, dz_ref, dw_ref, hp_ref, dp_ref):
        hp_ref[0:CONV_PAD, :] = jnp.zeros((CONV_PAD, D_CONV), F32)
        dp_ref[s:s + CONV_PAD, :] = jnp.zeros((CONV_PAD, D_CONV), F32)
        dw_ref[...] = jnp.zeros(dw_ref.shape, F32)

        def fill(i, carry):
            t0 = pl.multiple_of(i * CONV_ROWS, CONV_ROWS)
            z = z_ref[pl.ds(t0, CONV_ROWS), :]
            hp_ref[pl.ds(t0 + CONV_PAD, CONV_ROWS), :] = _glu(z[:, :D_CONV], z[:, D_CONV:])
            dp_ref[pl.ds(t0, CONV_ROWS), :] = d_ref[pl.ds(t0, CONV_ROWS), :]
            return carry

        lax.fori_loop(0, n_steps, fill, 0)
        wv = w_ref[...]

        def step(i, carry):
            t0 = pl.multiple_of(i * CONV_ROWS, CONV_ROWS)
            hwin = hp_ref[pl.ds(t0, CONV_ROWS + CONV_PAD), :]
            dwin = dp_ref[pl.ds(t0, CONV_ROWS + CONV_PAD), :]
            dcur = dwin[0:CONV_ROWS, :]
            dh = jnp.zeros((CONV_ROWS, D_CONV), F32)
            rows = []
            for k in range(CONV_WIDTH):
                rows.append(jnp.sum(dcur * hwin[lead + k:lead + k + CONV_ROWS, :], axis=0, keepdims=True))
                back = CONV_WIDTH - 1 - k
                dh = dh + wv[k:k + 1, :] * dwin[back:back + CONV_ROWS, :]
            rows.append(jnp.zeros((1, D_CONV), F32))
            dw_ref[...] += jnp.concatenate(rows, axis=0)
            z = z_ref[pl.ds(t0, CONV_ROWS), :]
            _, vjp = jax.vjp(_glu, z[:, :D_CONV], z[:, D_CONV:])
            da, dg = vjp(dh)
            dz_ref[pl.ds(t0, CONV_ROWS), :] = jnp.concatenate([da, dg], axis=1).astype(dz_ref.dtype)
            return carry

        lax.fori_loop(0, n_steps, step, 0)

    return pl.pallas_call(
        body, name=name, grid=(1,),
        in_specs=[pl.BlockSpec((s, width), lambda i, cb=cb: (0, cb)), pl.BlockSpec(w.shape, lambda i: (0, 0)),
                  pl.BlockSpec((s, D_CONV), lambda i: (0, 0))],
        out_specs=[pl.BlockSpec((s, width), lambda i: (0, 0)), pl.BlockSpec(w.shape, lambda i: (0, 0))],
        out_shape=[jax.ShapeDtypeStruct((s, width), BF16), jax.ShapeDtypeStruct(w.shape, F32)],
        scratch_shapes=[pltpu.VMEM((s + CONV_PAD, D_CONV), F32), pltpu.VMEM((s + CONV_PAD, D_CONV), F32)],
        compiler_params=pltpu.CompilerParams(dimension_semantics=("arbitrary",), vmem_limit_bytes=VMEM_LIMIT),
    )(arr, w, dout)


HEADS_PER_STEP = LANE // HEAD_DIM
CHUNKS_PER_TILE = ATT_TILE // CHUNK


def _attn_tile(q, kp, kc, vp, vc, bias, first):
    kcat = jnp.concatenate([kp, kc], axis=0)
    vcat = jnp.concatenate([vp, vc], axis=0)
    lane = lax.broadcasted_iota(jnp.int32, (1, LANE), 1)
    col = lax.broadcasted_iota(jnp.int32, (1, 2 * ATT_TILE), 1)
    missing = jnp.logical_and(first, col < ATT_TILE)
    qs = q * (HEAD_DIM ** -0.5)
    o = jnp.zeros((ATT_TILE, LANE), F32)
    for h in range(HEADS_PER_STEP):
        in_head = jnp.logical_and(lane >= h * HEAD_DIM, lane < (h + 1) * HEAD_DIM)
        sc = mm_nt(jnp.where(in_head, qs, 0.0), kcat) + bias[h]
        sc = jnp.where(missing, NEG_INF, sc)
        m = jnp.max(sc, axis=-1, keepdims=True)
        e = jnp.exp(sc - lax.stop_gradient(m))
        p = e / jnp.sum(e, axis=-1, keepdims=True)
        o = o + jnp.where(in_head, mm_nn(p, vcat), 0.0)
    return o


def _z_spec(col0, fn):
    return pl.BlockSpec((ATT_TILE, LANE), lambda hp, n, col0=col0, fn=fn: (fn(n), col0 // LANE + hp))


def attn_fwd(name, z, bias):
    s = z.shape[0]
    nt = s // ATT_TILE

    def body(q_ref, kp_ref, kc_ref, vp_ref, vc_ref, b_ref, o_ref):
        first = pl.program_id(1) == 0
        o = _attn_tile(q_ref[...], kp_ref[...], kc_ref[...], vp_ref[...], vc_ref[...], b_ref[...], first)
        o_ref[...] = o.astype(o_ref.dtype)

    cur = lambda n: n
    prev = lambda n: jnp.maximum(n - 1, 0)
    return pl.pallas_call(
        body, name=name, grid=(N_HEADS // HEADS_PER_STEP, nt),
        in_specs=[_z_spec(Z_Q, cur), _z_spec(Z_K, prev), _z_spec(Z_K, cur), _z_spec(Z_V, prev), _z_spec(Z_V, cur),
                  pl.BlockSpec((HEADS_PER_STEP, ATT_TILE, 2 * ATT_TILE), lambda hp, n: (hp, 0, 0))],
        out_specs=pl.BlockSpec((ATT_TILE, LANE), lambda hp, n: (n, hp)),
        out_shape=jax.ShapeDtypeStruct((s, D_ATTN), BF16),
        compiler_params=pltpu.CompilerParams(dimension_semantics=("parallel", "parallel"), vmem_limit_bytes=VMEM_LIMIT),
    )(z, z, z, z, z, bias)


def attn_bwd(name, z, bias, do):
    s = z.shape[0]
    nt = s // ATT_TILE

    def body(q_ref, kp_ref, kc_ref, vp_ref, vc_ref, b_ref, do_ref, dq_ref, dk_ref, dv_ref, db_ref, kacc, vacc):
        n = pl.program_id(1)

        @pl.when(n == 0)
        def _():
            db_ref[...] = jnp.zeros(db_ref.shape, F32)

        @pl.when(n < nt)
        def _():
            first = n == 0
            fn = functools.partial(_attn_tile, first=first)
            _, vjp = jax.vjp(fn, q_ref[...], kp_ref[...], kc_ref[...], vp_ref[...], vc_ref[...], b_ref[...])
            dq, dkp, dkc, dvp, dvc, db = vjp(do_ref[...].astype(F32))
            dq_ref[...] = dq.astype(dq_ref.dtype)
            db_ref[...] += db

            @pl.when(n > 0)
            def _():
                dk_ref[...] = (kacc[...] + dkp).astype(dk_ref.dtype)
                dv_ref[...] = (vacc[...] + dvp).astype(dv_ref.dtype)

            kacc[...] = dkc
            vacc[...] = dvc

        @pl.when(n == nt)
        def _():
            dk_ref[...] = kacc[...].astype(dk_ref.dtype)
            dv_ref[...] = vacc[...].astype(dv_ref.dtype)

    cur = lambda n: jnp.minimum(n, nt - 1)
    prev = lambda n: jnp.clip(n - 1, 0, nt - 1)
    o_cur = pl.BlockSpec((ATT_TILE, LANE), lambda hp, n: (jnp.minimum(n, nt - 1), hp))
    o_prev = pl.BlockSpec((ATT_TILE, LANE), lambda hp, n: (jnp.maximum(n - 1, 0), hp))
    return pl.pallas_call(
        body, name=name, grid=(N_HEADS // HEADS_PER_STEP, nt + 1),
        in_specs=[_z_spec(Z_Q, cur), _z_spec(Z_K, prev), _z_spec(Z_K, cur), _z_spec(Z_V, prev), _z_spec(Z_V, cur),
                  pl.BlockSpec((HEADS_PER_STEP, ATT_TILE, 2 * ATT_TILE), lambda hp, n: (hp, 0, 0)), o_cur],
        out_specs=[o_cur, o_prev, o_prev, pl.BlockSpec((HEADS_PER_STEP, ATT_TILE, 2 * ATT_TILE), lambda hp, n: (hp, 0, 0))],
        out_shape=[jax.ShapeDtypeStruct((s, D_ATTN), BF16)] * 3 + [jax.ShapeDtypeStruct((N_HEADS, ATT_TILE, 2 * ATT_TILE), F32)],
        scratch_shapes=[pltpu.VMEM((ATT_TILE, LANE), F32), pltpu.VMEM((ATT_TILE, LANE), F32)],
        compiler_params=pltpu.CompilerParams(dimension_semantics=("parallel", "arbitrary"), vmem_limit_bytes=VMEM_LIMIT),
    )(z, z, z, z, z, bias, do)


def loss_head(name, y, tgt, tm):
    s, d = y.shape

    def body(y_ref, t_ref, l_ref, dy_ref):
        i = pl.program_id(0)
        diff = y_ref[...] - t_ref[...]
        dy_ref[...] = diff * (1.0 / d)
        part = 0.5 * jnp.sum(jnp.mean(diff * diff, axis=-1, keepdims=True), axis=0, keepdims=True)

        @pl.when(i == 0)
        def _():
            l_ref[...] = jnp.zeros(l_ref.shape, F32)

        l_ref[...] += jnp.broadcast_to(part, l_ref.shape)

    row = pl.BlockSpec((tm, d), lambda i: (i, 0))
    return pl.pallas_call(
        body, name=name, grid=(s // tm,), in_specs=[row, row],
        out_specs=[pl.BlockSpec((8, LANE), lambda i: (0, 0)), row],
        out_shape=[jax.ShapeDtypeStruct((8, LANE), F32), jax.ShapeDtypeStruct((s, d), F32)],
        compiler_params=pltpu.CompilerParams(dimension_semantics=("arbitrary",), vmem_limit_bytes=VMEM_LIMIT),
    )(y, tgt)


def adamw(name, w, g, m, v, tr):
    r, c = w.shape
    assert r % tr == 0, (name, w.shape, tr)

    def body(w_ref, g_ref, m_ref, v_ref, d_ref, nm_ref, nv_ref):
        gg = g_ref[...]
        m2 = ADAM_B1 * m_ref[...] + (1.0 - ADAM_B1) * gg
        v2 = ADAM_B2 * v_ref[...] + (1.0 - ADAM_B2) * (gg * gg)
        m_hat = m2 / (1.0 - ADAM_B1 ** ADAM_STEP)
        v_hat = v2 / (1.0 - ADAM_B2 ** ADAM_STEP)
        d_ref[...] = -ADAM_LR * (m_hat / (jnp.sqrt(v_hat) + ADAM_EPS) + ADAM_WD * w_ref[...])
        nm_ref[...] = m2
        nv_ref[...] = v2

    blk = pl.BlockSpec((tr, c), lambda i: (i, 0))
    return pl.pallas_call(
        body, name=name, grid=(r // tr,), in_specs=[blk] * 4, out_specs=[blk] * 3,
        out_shape=[jax.ShapeDtypeStruct((r, c), F32)] * 3,
        compiler_params=pltpu.CompilerParams(dimension_semantics=("parallel",), vmem_limit_bytes=VMEM_LIMIT),
    )(w, g, m, v)


def adamw_sum(name, w, layer_blocks, m, v, tr):
    rows, c = w.shape
    nl = len(layer_blocks)
    nb, r, _ = layer_blocks[0].shape
    assert rows == nl * r and r % tr == 0, (name, w.shape, layer_blocks[0].shape, tr)
    per = r // tr

    def body(*refs):
        w_ref, b_refs, (m_ref, v_ref, g_ref, d_ref, nm_ref, nv_ref) = refs[0], refs[1:1 + nl], refs[1 + nl:]
        i = pl.program_id(0)

        def update(b_ref):
            gg = b_ref[0].astype(F32)
            for j in range(1, nb):
                gg = gg + b_ref[j].astype(F32)
            g_ref[...] = gg
            m2 = ADAM_B1 * m_ref[...] + (1.0 - ADAM_B1) * gg
            v2 = ADAM_B2 * v_ref[...] + (1.0 - ADAM_B2) * (gg * gg)
            m_hat = m2 / (1.0 - ADAM_B1 ** ADAM_STEP)
            v_hat = v2 / (1.0 - ADAM_B2 ** ADAM_STEP)
            d_ref[...] = -ADAM_LR * (m_hat / (jnp.sqrt(v_hat) + ADAM_EPS) + ADAM_WD * w_ref[...])
            nm_ref[...] = m2
            nv_ref[...] = v2

        for l in range(nl):
            pl.when(jnp.logical_and(i >= l * per, i < (l + 1) * per))(functools.partial(update, b_refs[l]))

    blk = pl.BlockSpec((tr, c), lambda i: (i, 0))
    b_specs = [pl.BlockSpec((nb, tr, c), lambda i, l=l: (0, jnp.clip(i - l * per, 0, per - 1), 0)) for l in range(nl)]
    return pl.pallas_call(
        body, name=name, grid=(rows // tr,),
        in_specs=[blk] + b_specs + [blk, blk], out_specs=[blk] * 4,
        out_shape=[jax.ShapeDtypeStruct((rows, c), F32)] * 4,
        compiler_params=pltpu.CompilerParams(dimension_semantics=("arbitrary",), vmem_limit_bytes=VMEM_LIMIT),
    )(w, *layer_blocks, m, v)


def sum_blocks(name, blocks, tr):
    nb, r, c = blocks.shape
    assert r % tr == 0, (name, blocks.shape, tr)

    def body(b_ref, o_ref):
        acc = b_ref[0].astype(F32)
        for j in range(1, nb):
            acc = acc + b_ref[j].astype(F32)
        o_ref[...] = acc

    return pl.pallas_call(
        body, name=name, grid=(r // tr,),
        in_specs=[pl.BlockSpec((nb, tr, c), lambda i: (0, i, 0))],
        out_specs=pl.BlockSpec((tr, c), lambda i: (i, 0)),
        out_shape=jax.ShapeDtypeStruct((r, c), F32),
        compiler_params=pltpu.CompilerParams(dimension_semantics=("parallel",), vmem_limit_bytes=VMEM_LIMIT),
    )(blocks)


FLIPS = [(0, 0, 1), (1, 0, 0), (0, 1, 0), (1, 1, 0), (1, 0, 1), (0, 1, 1), (1, 1, 1)]
ANY = pl.BlockSpec(memory_space=pl.ANY)


def _me():
    return lax.axis_index("x"), lax.axis_index("y"), lax.axis_index("c")


def _flip(pos, f):
    return tuple((1 - p) if fi else p for p, fi in zip(pos, f))


def _idx(pos):
    return 4 * pos[0] + 2 * pos[1] + pos[2]


def all_gather_multi(name, shards):
    n = len(shards)

    def body(*refs):
        x_refs, out_refs = refs[:n], refs[n:2 * n]
        send_sems, recv_sems, local_sems = refs[2 * n:]
        x, y, cc = _me()
        me, sibling = (x, y, cc), (x, y, 1 - cc)
        chips = [(1 - x, y), (x, 1 - y), (1 - x, 1 - y)]

        def copy(a, k, block, to, src=None):
            dst = out_refs[a].at[_idx(block)]
            return pltpu.make_async_remote_copy(
                src_ref=dst if src is None else src, dst_ref=dst, send_sem=send_sems.at[7 * a + k],
                recv_sem=recv_sems.at[7 * a + k], device_id=to, device_id_type=MESH)

        mine = [pltpu.make_async_copy(x_refs[a], out_refs[a].at[_idx(me)], local_sems.at[a]) for a in range(n)]
        for cp in mine:
            cp.start()
        first = []
        for a in range(n):
            first.append(copy(a, 0, me, sibling, src=x_refs[a]))
            first += [copy(a, 1 + j, me, (*chip, cc), src=x_refs[a]) for j, chip in enumerate(chips)]
        for cp in first:
            cp.start()
        passed = []
        for j, chip in enumerate(chips):
            for a in range(n):
                copy(a, 1 + j, (*chip, cc), me).wait_recv()
                fwd = copy(a, 4 + j, (*chip, cc), sibling)
                fwd.start()
                passed.append(fwd)
        for a in range(n):
            copy(a, 0, sibling, me).wait_recv()
            for j, chip in enumerate(chips):
                copy(a, 4 + j, (*chip, 1 - cc), me).wait_recv()
        for cp in first + passed:
            cp.wait_send()
        for cp in mine:
            cp.wait()

    return pl.pallas_call(
        body, name=name, in_specs=[ANY] * n, out_specs=[ANY] * n,
        out_shape=[jax.ShapeDtypeStruct((N_DEV,) + a.shape, a.dtype) for a in shards],
        scratch_shapes=[pltpu.SemaphoreType.DMA((7 * n,)), pltpu.SemaphoreType.DMA((7 * n,)),
                        pltpu.SemaphoreType.DMA((n,))],
    )(*shards)


def all_to_all_multi(name, blocks):
    n = len(blocks)

    def body(*refs):
        in_refs, out_refs = refs[:n], refs[n:2 * n]
        send_sems, recv_sems, local_sems = refs[2 * n:]
        me = _me()
        mi = _idx(me)
        mine = [pltpu.make_async_copy(in_refs[a].at[mi], out_refs[a].at[mi], local_sems.at[a]) for a in range(n)]
        for cp in mine:
            cp.start()
        sends, recvs = [], []
        for k, f in enumerate(FLIPS):
            peer = _flip(me, f)
            pi = _idx(peer)
            for a in range(n):
                sems = dict(send_sem=send_sems.at[7 * a + k], recv_sem=recv_sems.at[7 * a + k],
                            device_id=peer, device_id_type=MESH)
                sends.append(pltpu.make_async_remote_copy(src_ref=in_refs[a].at[pi], dst_ref=out_refs[a].at[mi], **sems))
                recvs.append(pltpu.make_async_remote_copy(src_ref=in_refs[a].at[mi], dst_ref=out_refs[a].at[pi], **sems))
        for cp in sends:
            cp.start()
        for cp in recvs:
            cp.wait_recv()
        for cp in sends:
            cp.wait_send()
        for cp in mine:
            cp.wait()

    return pl.pallas_call(
        body, name=name, in_specs=[ANY] * n, out_specs=[ANY] * n,
        out_shape=[jax.ShapeDtypeStruct(a.shape, a.dtype) for a in blocks],
        scratch_shapes=[pltpu.SemaphoreType.DMA((7 * n,)), pltpu.SemaphoreType.DMA((7 * n,)),
                        pltpu.SemaphoreType.DMA((n,))],
    )(*blocks)


HBM = pl.BlockSpec(memory_space=pltpu.HBM)
SEM = pl.BlockSpec(memory_space=pltpu.SEMAPHORE)
DATAFLOW = pltpu.SideEffectType.DATAFLOW_SIDE_EFFECTING


def _exchange_copies(a_refs, l_refs, send_sems, recv_sems, gather):
    me = _me()
    mi = _idx(me)
    out = []
    for k, f in enumerate(FLIPS):
        peer = _flip(me, f)
        for a in range(len(a_refs)):
            src = a_refs[a] if gather else a_refs[a].at[_idx(peer)]
            out.append(pltpu.make_async_remote_copy(
                src_ref=src, dst_ref=l_refs[a].at[mi], send_sem=send_sems.at[7 * a + k],
                recv_sem=recv_sems.at[7 * a + k], device_id=peer, device_id_type=MESH))
    return out


def exchange_start(name, arrays, gather):
    n = len(arrays)
    lands = [lax.empty(((N_DEV,) + a.shape) if gather else a.shape, a.dtype) for a in arrays]

    def body(*refs):
        a_refs, l_refs = refs[:n], refs[n:2 * n]
        send_sems, recv_sems = refs[2 * n], refs[2 * n + 1]
        token = refs[4 * n + 2]
        for cp in _exchange_copies(a_refs, l_refs, send_sems, recv_sems, gather):
            cp.start()
        token[...] = jnp.zeros_like(token)

    hbm = lambda a: pltpu.HBM(a.shape, a.dtype)
    res = pl.pallas_call(
        body, name=name,
        out_shape=(pltpu.SemaphoreType.DMA((7 * n,)), pltpu.SemaphoreType.DMA((7 * n,)),
                   *[hbm(a) for a in arrays], *[hbm(a) for a in lands], jax.ShapeDtypeStruct((8, LANE), F32)),
        in_specs=[HBM] * (2 * n),
        out_specs=(SEM, SEM, *[HBM] * (2 * n), pl.BlockSpec(memory_space=pltpu.VMEM)),
        input_output_aliases={i: i + 2 for i in range(2 * n)},
        compiler_params=pltpu.CompilerParams(has_side_effects=DATAFLOW),
    )(*[pltpu.with_memory_space_constraint(a, pltpu.HBM) for a in arrays],
      *[pltpu.with_memory_space_constraint(a, pltpu.HBM) for a in lands])
    return res[0], res[1], list(res[2:2 + n]), list(res[2 + n:2 + 2 * n]), res[-1]


def exchange_wait(name, send_sems, recv_sems, arrays, lands, after, gather):
    n = len(arrays)

    def body(*refs):
        a_refs, l_refs = refs[:n], refs[n:2 * n]
        ssem, rsem = refs[2 * n], refs[2 * n + 1]
        for cp in _exchange_copies(a_refs, l_refs, ssem, rsem, gather):
            cp.wait_send()
            cp.wait_recv()

    hbm = lambda a: pltpu.HBM(a.shape, a.dtype)
    res = pl.pallas_call(
        body, name=name,
        out_shape=(*[hbm(a) for a in arrays], *[hbm(a) for a in lands]),
        in_specs=[HBM] * (2 * n) + [SEM, SEM, pl.BlockSpec(memory_space=pl.ANY)],
        out_specs=tuple([HBM] * (2 * n)),
        input_output_aliases={i: i for i in range(2 * n)},
        compiler_params=pltpu.CompilerParams(has_side_effects=DATAFLOW),
    )(*arrays, *lands, send_sems, recv_sems, after)
    return list(res[n:])


def ada_fwd(name, c_row, w_cat, b_lay):
    d = c_row.shape[1]
    ncol = w_cat.shape[1]
    vmem = pl.BlockSpec(memory_space=pltpu.VMEM)

    def body(c_ref, w_ref, b_ref, mod_ref, cact_ref, call, send, land, s1, r1, s2, r2):
        me = _me()
        mi = _idx(me)
        call[mi] = c_ref[...]

        def exchange(src_of, dst_buf, ssem, rsem):
            sends, recvs = [], []
            for k, f in enumerate(FLIPS):
                peer = _flip(me, f)
                sends.append(pltpu.make_async_remote_copy(
                    src_ref=src_of(peer), dst_ref=dst_buf.at[mi], send_sem=ssem.at[k], recv_sem=rsem.at[k],
                    device_id=peer, device_id_type=MESH))
                recvs.append(pltpu.make_async_remote_copy(
                    src_ref=src_of(peer), dst_ref=dst_buf.at[_idx(peer)], send_sem=ssem.at[k], recv_sem=rsem.at[k],
                    device_id=peer, device_id_type=MESH))
            for cp in sends:
                cp.start()
            for cp in recvs:
                cp.wait_recv()
            for cp in sends:
                cp.wait_send()

        exchange(lambda peer: c_ref, call, s1, r1)
        for p in range(N_DEV):
            cact_ref[pl.ds(p, 1), :] = jax.nn.silu(call[p])
        res = _dg(cact_ref[...], w_ref[...], 1, 0)
        for p in range(N_DEV):
            send[p] = res[p:p + 1, :]
        land[mi] = send[mi]
        exchange(lambda peer: send.at[_idx(peer)], land, s2, r2)
        mod_ref[...] = land[...] + b_ref[...]

    return pl.pallas_call(
        body, name=name, in_specs=[vmem, vmem, vmem], out_specs=[vmem, vmem],
        out_shape=[jax.ShapeDtypeStruct((N_DEV, 1, ncol), F32), jax.ShapeDtypeStruct((N_DEV, d), F32)],
        scratch_shapes=[pltpu.VMEM((N_DEV, 1, d), F32), pltpu.VMEM((N_DEV, 1, ncol), F32),
                        pltpu.VMEM((N_DEV, 1, ncol), F32),
                        pltpu.SemaphoreType.DMA((7,)), pltpu.SemaphoreType.DMA((7,)),
                        pltpu.SemaphoreType.DMA((7,)), pltpu.SemaphoreType.DMA((7,))],
        compiler_params=pltpu.CompilerParams(vmem_limit_bytes=VMEM_LIMIT),
    )(c_row, w_cat, b_lay)


POOL_WINDOWS = (2, 4, 8, 16)
POOL_GROUP = 64
N_REL = 2 * REL_CLIP + 1
PACK_COLS = 1024
SMALL_NAMES = ["b_ada", "b_gate", "w_pool", "pool_scale", "rel_bias", "conv_w", "conv_b", "conv_ln_g",
               "conv_ln_b", "ln_mix_g", "ln_mix_b", "b_ff1", "b_ff2", "ln_ff_g", "ln_ff_b"]
BIG_NAMES = ["w_in", "w_br_pool", "w_br_attn", "w_br_conv", "w_o", "w_ff1", "w_ff2"]
ROW_SHARDED = ("w_o", "w_ff2")
WEIGHT_NAMES = ["w_ada", "b_ada", "w_in", "b_gate", "w_pool", "pool_scale", "rel_bias", "conv_w", "conv_b",
                "conv_ln_g", "conv_ln_b", "w_br_pool", "w_br_attn", "w_br_conv", "w_o", "ln_mix_g", "ln_mix_b",
                "w_ff1", "b_ff1", "w_ff2", "b_ff2", "ln_ff_g", "ln_ff_b"]


def _perm_cols(w):
    return jnp.concatenate([w[:, 2304:], w[:, 1792:2304], w[:, :256], w[:, 256:768], w[:, 768:1280],
                            w[:, 1280:1792]], axis=1)


def _unperm_cols(wp):
    return jnp.concatenate([wp[:, Z_POOL:Z_Q], wp[:, Z_Q:Z_K], wp[:, Z_K:Z_V], wp[:, Z_V:],
                            wp[:, Z_CONV:Z_POOL], wp[:, :Z_CONV]], axis=1)


def _bias_table(rel_bias):
    far = jnp.broadcast_to(rel_bias[:, 2 * REL_CLIP:], (N_HEADS, BAND - REL_CLIP))
    near = rel_bias[:, REL_CLIP - CHUNK + 1:2 * REL_CLIP][:, ::-1]
    ext = jnp.concatenate([far, near], axis=1)
    return jnp.stack([ext[:, CHUNK - 1 - qi:CHUNK - 1 - qi + BAND] for qi in range(CHUNK)], axis=1)


def _bias_full(rel_bias):
    tab = _bias_table(rel_bias)
    return jnp.concatenate(
        [jnp.pad(tab, ((0, 0), (0, 0), (i * CHUNK, 2 * ATT_TILE - BAND - i * CHUNK)), constant_values=NEG_INF)
         for i in range(CHUNKS_PER_TILE)], axis=1)


def _block_diag(w_pool):
    out = jnp.zeros((D_POOL, D_POOL), F32)
    for g in range(len(POOL_WINDOWS)):
        out = lax.dynamic_update_slice(out, w_pool[g], (g * POOL_GROUP, g * POOL_GROUP))
    return out


def _flat_pad(arrs, mult):
    flat = jnp.concatenate([a.reshape(-1) for a in arrs])
    pad = (-flat.shape[0]) % mult
    return jnp.pad(flat, (0, pad)) if pad else flat


def _unflat(flat, shapes):
    out, off = [], 0
    for shp in shapes:
        n = int(np.prod(shp))
        out.append(flat[off:off + n].reshape(shp))
        off += n
    return out


def _to_blocks(name, full):
    k, n = full.shape
    if name in ROW_SHARDED:
        return full.reshape(N_DEV, k // N_DEV, n)
    return full.reshape(k, N_DEV, n // N_DEV).transpose(1, 0, 2)


def _from_blocks(name, blocks):
    nb, r, c = blocks.shape
    if name in ROW_SHARDED:
        return blocks.reshape(nb * r, c)
    return blocks.transpose(1, 0, 2).reshape(r, nb * c)


class _Layer:
    pass


def _row(v):
    return v.reshape(1, -1)


def _layer_fwd(x, modr, w, wvec):
    sh_m, sc_m, g_m, sh_f, sc_f, g_f = modr
    (u,) = row_fwd("lnmod_mix", f_lnmod, [x], [sc_m, sh_m], [(D_MODEL, BF16)], 512)
    z = mm_big("mm_in", u, w.w_in, "nn", (512, 896, 1024), F32, j_outer=True)
    p = pool_lin("pool_fwd", (z, D_POOL, Z_POOL // D_POOL), wvec, False, F32)
    ao = attn_fwd("attn_fwd", z, w.bias)
    cv = conv_fwd("conv_fwd", (z, 2 * D_CONV, Z_CONV // (2 * D_CONV)), w.conv_w)
    mparams = [w.wbd, w.ps, w.wbp, w.wba, w.wbc, w.cb, w.clg, w.clb, w.bg, w.wo, g_m, w.lmg, w.lmb]
    (x1,) = row_fwd("merge", f_merge, [p, ao, cv, (z, 3 * D_MODEL, 0), x], mparams, [(D_MODEL, F32)], 256)
    (u2,) = row_fwd("lnmod_ff", f_lnmod, [x1], [sc_f, sh_f], [(D_MODEL, BF16)], 512)
    hpre = mm_big("mm_ff1", u2, w.w_ff1, "nn", (512, 1024, 1024), F32, j_outer=True)
    (h,) = row_fwd("relu2", f_relu2, [hpre], [w.b1], [(D_FF, BF16)], 512)
    ff = mm_big("mm_ff2", h, w.w_ff2, "nn", (512, 1024, 4096), F32)
    (x2,) = row_fwd("ffout", f_ffout, [x1, ff], [w.b2, g_f, w.lfg, w.lfb], [(D_MODEL, F32)], 512)
    return x2, (x, u, z, p, ao, cv, x1, u2, hpre, h, ff, mparams)


def _layer_bwd(dx2, saved, modr, w, wvec):
    x, u, z, p, ao, cv, x1, u2, hpre, h, ff, mparams = saved
    sh_m, sc_m, g_m, sh_f, sc_f, g_f = modr
    g = {}
    (dx1a, dff), (g["b_ff2"], dgf, g["ln_ff_g"], g["ln_ff_b"]) = row_bwd(
        "ffout_bwd", f_ffout, [x1, ff], [w.b2, g_f, w.lfg, w.lfb], [dx2], 512, [(0, F32), (1, BF16)], [0, 1, 2, 3])
    dh = mm_big("mm_dh", dff, w.w_ff2, "nt", (512, 1024, 1024), F32, j_outer=True)
    g["w_ff2"] = mm_big("mm_dw_ff2", h, dff, "tn", (1024, 1024, 2048), F32)
    (dhpre,), (g["b_ff1"],) = row_bwd("relu2_bwd", f_relu2, [hpre], [w.b1], [dh], 512, [(0, BF16)], [0])
    du2 = mm_big("mm_du2", dhpre, w.w_ff1, "nt", (512, 1024, 4096), F32)
    g["w_ff1"] = mm_big("mm_dw_ff1", u2, dhpre, "tn", (1024, 1024, 2048), F32)
    (dx1,), (dscf, dshf) = row_bwd("lnmod_ff_bwd", f_lnmod, [x1], [sc_f, sh_f], [du2], 512, [(0, F32)], [0, 1],
                                   add_to=dx1a)
    (dp, dao, dcv, dzg, dxa), dm = row_bwd(
        "merge_bwd", f_merge, [p, ao, cv, (z, 3 * D_MODEL, 0), x], mparams, [dx1], 256,
        [(0, F32), (1, BF16), (2, F32), (3, BF16), (4, F32)], list(range(13)))
    (dwbd, g["pool_scale"], g["w_br_pool"], g["w_br_attn"], g["w_br_conv"], g["conv_b"], g["conv_ln_g"],
     g["conv_ln_b"], g["b_gate"], g["w_o"], dgm, g["ln_mix_g"], g["ln_mix_b"]) = dm
    g["w_pool"] = jnp.stack([dwbd[i * POOL_GROUP:(i + 1) * POOL_GROUP, i * POOL_GROUP:(i + 1) * POOL_GROUP]
                             for i in range(len(POOL_WINDOWS))])
    dzp = pool_lin("pool_bwd", (dp, D_POOL, 0), wvec, True, BF16)
    dq, dk, dv, dbias = attn_bwd("attn_bwd", z, w.bias, dao)
    (g["rel_bias"],) = w.bias_vjp(dbias)
    dzc, dcw = conv_bwd("conv_bwd", (z, 2 * D_CONV, Z_CONV // (2 * D_CONV)), w.conv_w, dcv)
    g["conv_w"] = dcw[:CONV_WIDTH]
    dz = jnp.concatenate([dzg, dzc, dzp, dq, dk, dv], axis=1)
    du = mm_big("mm_du", dz, w.w_in, "nt", (512, 1024, D_IN), F32)
    g["w_in"] = _unperm_cols(mm_big("mm_dw_in", u, dz, "tn", (1024, 896, 2048), F32))
    (dx,), (dscm, dshm) = row_bwd("lnmod_mix_bwd", f_lnmod, [x], [sc_m, sh_m], [du], 512, [(0, F32)], [0, 1],
                                  add_to=dxa)
    dmod = jnp.concatenate([dshm, dscm, dgm, dshf, dscf, dgf], axis=1)
    return dx, dmod, g


def kernel(x, c, w_ada, b_ada, w_in, b_gate, w_pool, pool_scale, rel_bias, conv_w, conv_b, conv_ln_g, conv_ln_b, w_br_pool, w_br_attn, w_br_conv, w_o, ln_mix_g, ln_mix_b, w_ff1, b_ff1, w_ff2, b_ff2, ln_ff_g, ln_ff_b, loss_target, m_w_ada, m_b_ada, m_w_in, m_b_gate, m_w_pool, m_pool_scale, m_rel_bias, m_conv_w, m_conv_b, m_conv_ln_g, m_conv_ln_b, m_w_br_pool, m_w_br_attn, m_w_br_conv, m_w_o, m_ln_mix_g, m_ln_mix_b, m_w_ff1, m_b_ff1, m_w_ff2, m_b_ff2, m_ln_ff_g, m_ln_ff_b, v_w_ada, v_b_ada, v_w_in, v_b_gate, v_w_pool, v_pool_scale, v_rel_bias, v_conv_w, v_conv_b, v_conv_ln_g, v_conv_ln_b, v_w_br_pool, v_w_br_attn, v_w_br_conv, v_w_o, v_ln_mix_g, v_ln_mix_b, v_w_ff1, v_b_ff1, v_w_ff2, v_b_ff2, v_ln_ff_g, v_ln_ff_b):
    args = dict(locals())
    wts = {n: args[n] for n in WEIGHT_NAMES}
    mom = {n: args["m_" + n] for n in WEIGHT_NAMES}
    var = {n: args["v_" + n] for n in WEIGHT_NAMES}
    me = 4 * lax.axis_index("x") + 2 * lax.axis_index("y") + lax.axis_index("c")
    xs, tgt = x[0], loss_target[0]
    nc_ada = w_ada.shape[2]
    wvec = jnp.asarray(np.repeat(np.array(POOL_WINDOWS, np.float32), POOL_GROUP)[None, :])

    w_cat = jnp.concatenate([w_ada[l] for l in range(DEPTH)], axis=1)
    b_lay = b_ada.reshape(DEPTH, N_DEV, nc_ada).transpose(1, 0, 2).reshape(N_DEV, 1, DEPTH * nc_ada)
    land, cact = ada_fwd("ada_fwd", c, w_cat, b_lay)
    mod = land.reshape(N_DEV, DEPTH, nc_ada).transpose(1, 0, 2).reshape(DEPTH, 6 * D_MODEL)
    modr = [[mod[l:l + 1, i * D_MODEL:(i + 1) * D_MODEL] for i in range(6)] for l in range(DEPTH)]

    cw_pack = _flat_pad([conv_w], 8 * LANE).reshape(-1, LANE)
    shards = [[wts[n][l].astype(BF16) for n in BIG_NAMES] for l in range(DEPTH)]
    gathered0 = all_gather_multi("gather_weights_l0", shards[0] + [cw_pack])
    cw_all = gathered0[-1].reshape(N_DEV, -1)[:, :conv_w.size]
    conv_full = cw_all.reshape((N_DEV,) + conv_w.shape).transpose(1, 2, 0, 3).reshape(DEPTH, CONV_WIDTH, D_CONV)
    ssem, rsem, thru, lands, token = exchange_start("gather_weights_l1_start", shards[1], True)
    modr[0] = [r + token[0:1, 0:1] for r in modr[0]]

    def layer_weights(l, blocks):
        full = {n: _from_blocks(n, g) for n, g in zip(BIG_NAMES, blocks)}
        w = _Layer()
        w.w_in = _perm_cols(full["w_in"])
        w.wbp, w.wba, w.wbc = full["w_br_pool"], full["w_br_attn"], full["w_br_conv"]
        w.wo, w.w_ff1, w.w_ff2 = full["w_o"], full["w_ff1"], full["w_ff2"]
        w.wbd = _block_diag(w_pool[l])
        w.ps, w.cb, w.clg, w.clb = _row(pool_scale[l]), _row(conv_b[l]), _row(conv_ln_g[l]), _row(conv_ln_b[l])
        w.bg, w.lmg, w.lmb = _row(b_gate[l]), _row(ln_mix_g[l]), _row(ln_mix_b[l])
        w.b1, w.b2, w.lfg, w.lfb = _row(b_ff1[l]), _row(b_ff2[l]), _row(ln_ff_g[l]), _row(ln_ff_b[l])
        w.conv_w = jnp.pad(conv_full[l], ((0, CONV_PAD - CONV_WIDTH), (0, 0)))
        w.bias, w.bias_vjp = jax.vjp(_bias_full, rel_bias[l])
        return w

    def with_own(lands_, own):
        return [lax.dynamic_update_index_in_dim(ld, o, me, axis=0) for ld, o in zip(lands_, own)]

    layers, saved = [layer_weights(0, gathered0[:-1])], []
    h, sv = _layer_fwd(xs, modr[0], layers[0], wvec)
    saved.append(sv)
    lands = exchange_wait("gather_weights_l1_wait", ssem, rsem, thru, lands, h, True)
    layers.append(layer_weights(1, with_own(lands, shards[1])))
    h, sv = _layer_fwd(h, modr[1], layers[1], wvec)
    saved.append(sv)
    lpart, dy = loss_head("loss_head", h, tgt, 512)
    loss = lax.psum(lpart[0, 0], ("x", "y", "c"))
    grads, dmods = [None] * DEPTH, [None] * DEPTH
    dy, dmods[1], grads[1] = _layer_bwd(dy, saved[1], modr[1], layers[1], wvec)

    gblocks = lambda l: [_to_blocks(n, grads[l][n]).astype(BF16) for n in BIG_NAMES]
    blocks1 = gblocks(1)
    ssem, rsem, thru, lands, token = exchange_start("scatter_grads_l1_start", blocks1, False)
    modr0_b = [r + token[0:1, 0:1] for r in modr[0]]
    dy, dmods[0], grads[0] = _layer_bwd(dy, saved[0], modr0_b, layers[0], wvec)
    lands = exchange_wait("scatter_grads_l1_wait", ssem, rsem, thru, lands, dy, False)
    recv1 = with_own(lands, [lax.dynamic_index_in_dim(b, me, axis=0, keepdims=False) for b in blocks1])
    recv0 = all_to_all_multi("scatter_grads_l0", gblocks(0))
    grad_x = dy[None]
    dmod = jnp.concatenate(dmods, axis=0)

    small_shapes = [wts[n].shape if n != "conv_w" else (DEPTH, CONV_WIDTH, D_CONV) for n in SMALL_NAMES]
    small_local = [dmod] + [jnp.stack([grads[l][n].reshape(shp[1:]) for l in range(DEPTH)])
                            for n, shp in zip(SMALL_NAMES[1:], small_shapes[1:])]
    small_pack = _flat_pad(small_local, 8 * LANE).reshape(-1, LANE)
    (small_all,) = all_gather_multi("gather_small_grads", [small_pack])
    small_sum = sum_blocks("sum_small_grads", small_all, small_pack.shape[0]).reshape(-1)
    gsmall = dict(zip(SMALL_NAMES, _unflat(small_sum, small_shapes)))
    gw = dict(gsmall)
    gw["conv_w"] = lax.dynamic_slice_in_dim(gsmall["conv_w"], me * conv_w.shape[2], conv_w.shape[2], axis=2)

    dmod_all = small_all.reshape(N_DEV, -1)[:, :dmod.size].reshape(N_DEV, DEPTH, N_DEV, nc_ada)
    dm_mine = lax.dynamic_index_in_dim(dmod_all, me, axis=2, keepdims=False).reshape(N_DEV, DEPTH * nc_ada)
    cact_t = jnp.pad(cact.T, ((0, 0), (0, LANE - N_DEV)))
    dm_pad = jnp.pad(dm_mine, ((0, LANE - N_DEV), (0, 0)))
    dw_cat = mm_big("mm_dw_ada", cact_t, dm_pad, "nn", (D_MODEL, DEPTH * nc_ada, LANE), F32)
    gw["w_ada"] = jnp.stack([dw_cat[:, l * nc_ada:(l + 1) * nc_ada] for l in range(DEPTH)])

    delta, new_m, new_v = {}, {}, {}
    for i, n in enumerate(["w_ada"] + BIG_NAMES):
        shp = wts[n].shape
        two_d = lambda a, shp=shp: a.reshape(shp[0] * shp[1], shp[2])
        tr = min(256, shp[1])
        if n == "w_ada":
            res = (gw[n],) + tuple(adamw("adamw_" + n, two_d(wts[n]), two_d(gw[n]), two_d(mom[n]), two_d(var[n]), tr))
        else:
            res = adamw_sum("adamw_" + n, two_d(wts[n]), [recv0[i - 1], recv1[i - 1]], two_d(mom[n]), two_d(var[n]), tr)
        gw[n], delta[n], new_m[n], new_v[n] = [a.reshape(shp) for a in res]
    packs = [_flat_pad([src[n] for n in SMALL_NAMES], 8 * LANE).reshape(-1, LANE) for src in (wts, gw, mom, var)]
    res = adamw("adamw_small", *packs, packs[0].shape[0])
    shapes = [wts[n].shape for n in SMALL_NAMES]
    for out, flat in zip((delta, new_m, new_v), res):
        out.update(dict(zip(SMALL_NAMES, _unflat(flat.reshape(-1), shapes))))

    return (loss, grad_x, *[gw[n] for n in WEIGHT_NAMES], *[delta[n] for n in WEIGHT_NAMES],
            *[new_m[n] for n in WEIGHT_NAMES], *[new_v[n] for n in WEIGHT_NAMES])
```

```python
import functools

import jax
import jax.numpy as jnp
import numpy as np
from jax import lax
from jax.experimental import pallas as pl
from jax.experimental.pallas import tpu as pltpu

F32 = jnp.float32
BF16 = jnp.bfloat16
MESH = pl.DeviceIdType.MESH

D_MODEL = 1024
DEPTH = 2
CHUNK = 64
N_HEADS = 8
HEAD_DIM = 64
D_POOL = 256
D_ATTN = 512
D_CONV = 256
CONV_WIDTH = 31
D_FF = 4096
D_IN = 5376
N_PREV = 8
BAND = (N_PREV + 1) * CHUNK
REL_CLIP = 128
ALPHA = (2.0 * DEPTH) ** 0.25
LN_EPS = 1e-5
NEG_INF = -1e30
N_DEV = 8

ADAM_LR, ADAM_B1, ADAM_B2, ADAM_EPS, ADAM_WD, ADAM_STEP = 0.001, 0.9, 0.999, 1e-08, 0.01, 10

VMEM_LIMIT = 56 * 1024 * 1024

Z_GATE, Z_CONV, Z_POOL, Z_Q, Z_K, Z_V = 0, 3072, 3584, 3840, 4352, 4864
ATT_TILE = 512
LANE = 128


def _dg(a, b, ca, cb):
    return lax.dot_general(a.astype(BF16), b.astype(BF16), (((ca,), (cb,)), ((), ())),
                           preferred_element_type=F32)


@jax.custom_vjp
def mm_nn(a, b):
    return _dg(a, b, 1, 0)


def _mm_nn_fwd(a, b):
    return _dg(a, b, 1, 0), (a, b)


def _mm_nn_bwd(res, g):
    a, b = res
    return _dg(g, b, 1, 1).astype(a.dtype), _dg(a, g, 0, 0).astype(b.dtype)


mm_nn.defvjp(_mm_nn_fwd, _mm_nn_bwd)


@jax.custom_vjp
def mm_nt(a, b):
    return _dg(a, b, 1, 1)


def _mm_nt_fwd(a, b):
    return _dg(a, b, 1, 1), (a, b)


def _mm_nt_bwd(res, g):
    a, b = res
    return _dg(g, b, 1, 0).astype(a.dtype), _dg(g, a, 0, 0).astype(b.dtype)


mm_nt.defvjp(_mm_nt_fwd, _mm_nt_bwd)


def _ln(x):
    mu = jnp.mean(x, axis=-1, keepdims=True)
    xc = x - mu
    var = jnp.mean(xc * xc, axis=-1, keepdims=True)
    return xc * lax.rsqrt(var + LN_EPS)


def _norm_rows(rows):
    return [r if isinstance(r, tuple) else (r, r.shape[1], 0) for r in rows]


def _row_spec(tm, r):
    _, width, cb = r
    return pl.BlockSpec((tm, width), lambda i, cb=cb: (i, cb))


def _full_spec(a):
    nd = a.ndim
    return pl.BlockSpec(a.shape, lambda i, nd=nd: (0,) * nd)


def row_fwd(name, f, rows, params, outs, tm):
    rows = _norm_rows(rows)
    s = rows[0][0].shape[0]
    nr, npar = len(rows), len(params)

    def body(*refs):
        r = [x[...].astype(F32) for x in refs[:nr]]
        p = [x[...] for x in refs[nr:nr + npar]]
        res = f(*r, *p)
        for o_ref, o in zip(refs[nr + npar:], res):
            o_ref[...] = o.astype(o_ref.dtype)

    return pl.pallas_call(
        body, name=name, grid=(s // tm,),
        in_specs=[_row_spec(tm, r) for r in rows] + [_full_spec(p) for p in params],
        out_specs=[pl.BlockSpec((tm, w), lambda i: (i, 0)) for w, _ in outs],
        out_shape=[jax.ShapeDtypeStruct((s, w), dt) for w, dt in outs],
        compiler_params=pltpu.CompilerParams(dimension_semantics=("parallel",), vmem_limit_bytes=VMEM_LIMIT),
    )(*[r[0] for r in rows], *params)


def row_bwd(name, f, rows, params, douts, tm, want_rows, want_params, add_to=None):
    rows = _norm_rows(rows)
    s = rows[0][0].shape[0]
    nr, npar, nd = len(rows), len(params), len(douts)
    nadd = 0 if add_to is None else 1
    n_in = nr + npar + nd + nadd

    def body(*refs):
        i = pl.program_id(0)
        r = [x[...].astype(F32) for x in refs[:nr]]
        p = [x[...].astype(F32) for x in refs[nr:nr + npar]]
        d = [x[...].astype(F32) for x in refs[nr + npar:nr + npar + nd]]
        _, vjp = jax.vjp(f, *r, *p)
        g = vjp(tuple(d))
        out_refs = refs[n_in:]
        for k, (idx, _) in enumerate(want_rows):
            val = g[idx]
            if nadd and k == 0:
                val = val + refs[n_in - 1][...].astype(F32)
            out_refs[k][...] = val.astype(out_refs[k].dtype)
        for k, idx in enumerate(want_params):
            gp = g[nr + idx]
            o_ref = out_refs[len(want_rows) + k]

            @pl.when(i == 0)
            def _():
                o_ref[...] = gp

            @pl.when(i > 0)
            def _():
                o_ref[...] += gp

    in_specs = ([_row_spec(tm, r) for r in rows] + [_full_spec(p) for p in params]
                + [pl.BlockSpec((tm, d.shape[1]), lambda i: (i, 0)) for d in douts])
    args = [r[0] for r in rows] + list(params) + list(douts)
    if nadd:
        in_specs.append(pl.BlockSpec((tm, add_to.shape[1]), lambda i: (i, 0)))
        args.append(add_to)
    out_specs = ([pl.BlockSpec((tm, rows[idx][1]), lambda i: (i, 0)) for idx, _ in want_rows]
                 + [_full_spec(params[idx]) for idx in want_params])
    out_shape = ([jax.ShapeDtypeStruct((s, rows[idx][1]), dt) for idx, dt in want_rows]
                 + [jax.ShapeDtypeStruct(params[idx].shape, F32) for idx in want_params])
    res = pl.pallas_call(
        body, name=name, grid=(s // tm,), in_specs=in_specs, out_specs=out_specs, out_shape=out_shape,
        compiler_params=pltpu.CompilerParams(dimension_semantics=("arbitrary",), vmem_limit_bytes=VMEM_LIMIT),
    )(*args)
    return res[:len(want_rows)], res[len(want_rows):]


def f_lnmod(x, sc, sh):
    return (_ln(x) * (1.0 + sc) + sh,)


def f_merge(p, ao, cv, zg, x, wbd, ps, wbp, wba, wbc, cb, clg, clb, bg, wo, gm, lg, lb):
    pm = mm_nn(p, wbd) * ps
    co = jax.nn.silu(_ln(cv + cb) * clg + clb)
    y_pool = mm_nn(pm, wbp)
    y_attn = mm_nn(ao, wba)
    y_conv = mm_nn(co, wbc)
    gates = jax.nn.sigmoid(zg + bg)
    merged = (gates[:, :D_MODEL] * y_pool + gates[:, D_MODEL:2 * D_MODEL] * y_attn
              + gates[:, 2 * D_MODEL:] * y_conv)
    mix = mm_nn(merged, wo)
    return (_ln(ALPHA * x + gm * mix) * lg + lb,)


def f_relu2(hpre, b1):
    a = jax.nn.relu(hpre + b1)
    return (a * a,)


def f_ffout(x1, ff, b2, gf, lg, lb):
    return (_ln(ALPHA * x1 + gf * (ff + b2)) * lg + lb,)


def mm_big(name, a, b, kind, tiles, out_dtype, j_outer=False):
    if kind == "nn":
        o0, o1, red = a.shape[0], b.shape[1], a.shape[1]
    elif kind == "nt":
        o0, o1, red = a.shape[0], b.shape[0], a.shape[1]
    else:
        o0, o1, red = a.shape[1], b.shape[1], a.shape[0]
    t0, t1, tr = min(tiles[0], o0), min(tiles[1], o1), min(tiles[2], red)
    if kind == "nn":
        a_spec = pl.BlockSpec((t0, tr), lambda i, j, r: (i, r))
        b_spec = pl.BlockSpec((tr, t1), lambda i, j, r: (r, j))
        dims = (1, 0)
    elif kind == "nt":
        a_spec = pl.BlockSpec((t0, tr), lambda i, j, r: (i, r))
        b_spec = pl.BlockSpec((t1, tr), lambda i, j, r: (j, r))
        dims = (1, 1)
    else:
        a_spec = pl.BlockSpec((tr, t0), lambda i, j, r: (r, i))
        b_spec = pl.BlockSpec((tr, t1), lambda i, j, r: (r, j))
        dims = (0, 0)
    assert o0 % t0 == 0 and o1 % t1 == 0 and red % tr == 0, (name, a.shape, b.shape, tiles)
    n0, n1, nred = o0 // t0, o1 // t1, red // tr
    o_spec = pl.BlockSpec((t0, t1), lambda i, j, r: (i, j))
    if j_outer:
        swap = lambda spec: pl.BlockSpec(spec.block_shape, lambda j, i, r, f=spec.index_map: f(i, j, r))
        a_spec, b_spec, o_spec = swap(a_spec), swap(b_spec), swap(o_spec)
        grid = (n1, n0, nred)
    else:
        grid = (n0, n1, nred)

    if nred == 1:
        def body(a_ref, b_ref, o_ref):
            o_ref[...] = _dg(a_ref[...], b_ref[...], *dims).astype(o_ref.dtype)
        scratch = []
    else:
        def body(a_ref, b_ref, o_ref, acc_ref):
            r = pl.program_id(2)
            part = _dg(a_ref[...], b_ref[...], *dims)

            @pl.when(r == 0)
            def _():
                acc_ref[...] = part

            @pl.when(jnp.logical_and(r > 0, r < nred - 1))
            def _():
                acc_ref[...] += part

            @pl.when(r == nred - 1)
            def _():
                o_ref[...] = (acc_ref[...] + part).astype(o_ref.dtype)
        scratch = [pltpu.VMEM((t0, t1), F32)]

    return pl.pallas_call(
        body, name=name, grid=grid,
        in_specs=[a_spec, b_spec],
        out_specs=o_spec,
        out_shape=jax.ShapeDtypeStruct((o0, o1), out_dtype),
        scratch_shapes=scratch,
        compiler_params=pltpu.CompilerParams(dimension_semantics=("parallel", "parallel", "arbitrary"),
                                             vmem_limit_bytes=VMEM_LIMIT),
    )(a, b)


POOL_PAD = 16
POOL_ROWS = 256


def pool_lin(name, x, wvec, transpose, out_dtype):
    arr, width, cb = x
    s = arr.shape[0]
    n_steps = s // POOL_ROWS

    def body(x_ref, w_ref, o_ref, xp_ref):
        wv = w_ref[...]
        zeros = jnp.zeros((POOL_PAD, width), F32)
        xp_ref[0:POOL_PAD, :] = zeros
        xp_ref[s + POOL_PAD:s + 2 * POOL_PAD, :] = zeros

        def count(t0):
            t = lax.broadcasted_iota(jnp.int32, (POOL_ROWS, width), 0) + (t0 + 1)
            return jnp.minimum(t.astype(F32), wv)

        def fill(i, carry):
            t0 = pl.multiple_of(i * POOL_ROWS, POOL_ROWS)
            v = x_ref[pl.ds(t0, POOL_ROWS), :].astype(F32)
            if transpose:
                v = v / count(t0)
            xp_ref[pl.ds(t0 + POOL_PAD, POOL_ROWS), :] = v
            return carry

        lax.fori_loop(0, n_steps, fill, 0)

        def step(i, carry):
            t0 = pl.multiple_of(i * POOL_ROWS, POOL_ROWS)
            win = xp_ref[pl.ds(t0, POOL_ROWS + 2 * POOL_PAD), :]
            acc = jnp.zeros((POOL_ROWS, width), F32)
            for j in range(POOL_PAD):
                off = POOL_PAD + j if transpose else POOL_PAD - j
                acc = acc + jnp.where(wv > j, win[off:off + POOL_ROWS, :], 0.0)
            cur = x_ref[pl.ds(t0, POOL_ROWS), :].astype(F32)
            res = acc - cur if transpose else acc / count(t0) - cur
            o_ref[pl.ds(t0, POOL_ROWS), :] = res.astype(o_ref.dtype)
            return carry

        lax.fori_loop(0, n_steps, step, 0)

    return pl.pallas_call(
        body, name=name, grid=(1,),
        in_specs=[pl.BlockSpec((s, width), lambda i, cb=cb: (0, cb)), pl.BlockSpec((1, width), lambda i: (0, 0))],
        out_specs=pl.BlockSpec((s, width), lambda i: (0, 0)),
        out_shape=jax.ShapeDtypeStruct((s, width), out_dtype),
        scratch_shapes=[pltpu.VMEM((s + 2 * POOL_PAD, width), F32)],
        compiler_params=pltpu.CompilerParams(dimension_semantics=("arbitrary",), vmem_limit_bytes=VMEM_LIMIT),
    )(arr, wvec)


CONV_PAD = 32
CONV_ROWS = 128


def _glu(a, g):
    return a * jax.nn.sigmoid(g)


def conv_fwd(name, zc, w):
    arr, width, cb = zc
    s = arr.shape[0]
    n_steps = s // CONV_ROWS
    lead = CONV_PAD - (CONV_WIDTH - 1)

    def body(z_ref, w_ref, o_ref, hp_ref):
        hp_ref[0:CONV_PAD, :] = jnp.zeros((CONV_PAD, D_CONV), F32)

        def fill(i, carry):
            t0 = pl.multiple_of(i * CONV_ROWS, CONV_ROWS)
            z = z_ref[pl.ds(t0, CONV_ROWS), :]
            hp_ref[pl.ds(t0 + CONV_PAD, CONV_ROWS), :] = _glu(z[:, :D_CONV], z[:, D_CONV:])
            return carry

        lax.fori_loop(0, n_steps, fill, 0)
        wv = w_ref[...]

        def step(i, carry):
            t0 = pl.multiple_of(i * CONV_ROWS, CONV_ROWS)
            win = hp_ref[pl.ds(t0, CONV_ROWS + CONV_PAD), :]
            acc = jnp.zeros((CONV_ROWS, D_CONV), F32)
            for k in range(CONV_WIDTH):
                acc = acc + wv[k:k + 1, :] * win[lead + k:lead + k + CONV_ROWS, :]
            o_ref[pl.ds(t0, CONV_ROWS), :] = acc
            return carry

        lax.fori_loop(0, n_steps, step, 0)

    return pl.pallas_call(
        body, name=name, grid=(1,),
        in_specs=[pl.BlockSpec((s, width), lambda i, cb=cb: (0, cb)), pl.BlockSpec(w.shape, lambda i: (0, 0))],
        out_specs=pl.BlockSpec((s, D_CONV), lambda i: (0, 0)),
        out_shape=jax.ShapeDtypeStruct((s, D_CONV), F32),
        scratch_shapes=[pltpu.VMEM((s + CONV_PAD, D_CONV), F32)],
        compiler_params=pltpu.CompilerParams(dimension_semantics=("arbitrary",), vmem_limit_bytes=VMEM_LIMIT),
    )(arr, w)


def conv_bwd(name, zc, w, dout):
    arr, width, cb = zc
    s = arr.shape[0]
    n_steps = s // CONV_ROWS
    lead = CONV_PAD - (CONV_WIDTH - 1)

    def body(z_ref, w_ref, d_ref, dz_ref, dw_ref, hp_ref, dp_ref):
        hp_ref[0:CONV_PAD, :] = jnp.zeros((CONV_PAD, D_CONV), F32)
        dp_ref[s:s + CONV_PAD, :] = jnp.zeros((CONV_PAD, D_CONV), F32)
        dw_ref[...] = jnp.zeros(dw_ref.shape, F32)

        def fill(i, carry):
            t0 = pl.multiple_of(i * CONV_ROWS, CONV_ROWS)
            z = z_ref[pl.ds(t0, CONV_ROWS), :]
            hp_ref[pl.ds(t0 + CONV_PAD, CONV_ROWS), :] = _glu(z[:, :D_CONV], z[:, D_CONV:])
            dp_ref[pl.ds(t0, CONV_ROWS), :] = d_ref[pl.ds(t0, CONV_ROWS), :]
            return carry

        lax.fori_loop(0, n_steps, fill, 0)
        wv = w_ref[...]

        def step(i, carry):
            t0 = pl.multiple_of(i * CONV_ROWS, CONV_ROWS)
            hwin = hp_ref[pl.ds(t0, CONV_ROWS + CONV_PAD), :]
            dwin = dp_ref[pl.ds(t0, CONV_ROWS + CONV_PAD), :]
            dcur = dwin[0:CONV_ROWS, :]
            dh = jnp.zeros((CONV_ROWS, D_CONV), F32)
            rows = []
            for k in range(CONV_WIDTH):
                rows.append(jnp.sum(dcur * hwin[lead + k:lead + k + CONV_ROWS, :], axis=0, keepdims=True))
                back = CONV_WIDTH - 1 - k
                dh = dh + wv[k:k + 1, :] * dwin[back:back + CONV_ROWS, :]
            rows.append(jnp.zeros((1, D_CONV), F32))
            dw_ref[...] += jnp.concatenate(rows, axis=0)
            z = z_ref[pl.ds(t0, CONV_ROWS), :]
            _, vjp = jax.vjp(_glu, z[:, :D_CONV], z[:, D_CONV:])
            da, dg = vjp(dh)
            dz_ref[pl.ds(t0, CONV_ROWS), :] = jnp.concatenate([da, dg], axis=1).astype(dz_ref.dtype)
            return carry

        lax.fori_loop(0, n_steps, step, 0)

    return pl.pallas_call(
        body, name=name, grid=(1,),
        in_specs=[pl.BlockSpec((s, width), lambda i, cb=cb: (0, cb)), pl.BlockSpec(w.shape, lambda i: (0, 0)),
                  pl.BlockSpec((s, D_CONV), lambda i: (0, 0))],
        out_specs=[pl.BlockSpec((s, width), lambda i: (0, 0)), pl.BlockSpec(w.shape, lambda i: (0, 0))],
        out_shape=[jax.ShapeDtypeStruct((s, width), BF16), jax.ShapeDtypeStruct(w.shape, F32)],
        scratch_shapes=[pltpu.VMEM((s + CONV_PAD, D_CONV), F32), pltpu.VMEM((s + CONV_PAD, D_CONV), F32)],
        compiler_params=pltpu.CompilerParams(dimension_semantics=("arbitrary",), vmem_limit_bytes=VMEM_LIMIT),
    )(arr, w, dout)


HEADS_PER_STEP = LANE // HEAD_DIM
CHUNKS_PER_TILE = ATT_TILE // CHUNK


def _attn_tile(q, kp, kc, vp, vc, bias, first):
    kcat = jnp.concatenate([kp, kc], axis=0)
    vcat = jnp.concatenate([vp, vc], axis=0)
    lane = lax.broadcasted_iota(jnp.int32, (1, LANE), 1)
    col = lax.broadcasted_iota(jnp.int32, (1, 2 * ATT_TILE), 1)
    missing = jnp.logical_and(first, col < ATT_TILE)
    qs = q * (HEAD_DIM ** -0.5)
    o = jnp.zeros((ATT_TILE, LANE), F32)
    for h in range(HEADS_PER_STEP):
        in_head = jnp.logical_and(lane >= h * HEAD_DIM, lane < (h + 1) * HEAD_DIM)
        sc = mm_nt(jnp.where(in_head, qs, 0.0), kcat) + bias[h]
        sc = jnp.where(missing, NEG_INF, sc)
        m = jnp.max(sc, axis=-1, keepdims=True)
        e = jnp.exp(sc - lax.stop_gradient(m))
        p = e / jnp.sum(e, axis=-1, keepdims=True)
        o = o + jnp.where(in_head, mm_nn(p, vcat), 0.0)
    return o


def _z_spec(col0, fn):
    return pl.BlockSpec((ATT_TILE, LANE), lambda hp, n, col0=col0, fn=fn: (fn(n), col0 // LANE + hp))


def attn_fwd(name, z, bias):
    s = z.shape[0]
    nt = s // ATT_TILE

    def body(q_ref, kp_ref, kc_ref, vp_ref, vc_ref, b_ref, o_ref):
        first = pl.program_id(1) == 0
        o = _attn_tile(q_ref[...], kp_ref[...], kc_ref[...], vp_ref[...], vc_ref[...], b_ref[...], first)
        o_ref[...] = o.astype(o_ref.dtype)

    cur = lambda n: n
    prev = lambda n: jnp.maximum(n - 1, 0)
    return pl.pallas_call(
        body, name=name, grid=(N_HEADS // HEADS_PER_STEP, nt),
        in_specs=[_z_spec(Z_Q, cur), _z_spec(Z_K, prev), _z_spec(Z_K, cur), _z_spec(Z_V, prev), _z_spec(Z_V, cur),
                  pl.BlockSpec((HEADS_PER_STEP, ATT_TILE, 2 * ATT_TILE), lambda hp, n: (hp, 0, 0))],
        out_specs=pl.BlockSpec((ATT_TILE, LANE), lambda hp, n: (n, hp)),
        out_shape=jax.ShapeDtypeStruct((s, D_ATTN), BF16),
        compiler_params=pltpu.CompilerParams(dimension_semantics=("parallel", "parallel"), vmem_limit_bytes=VMEM_LIMIT),
    )(z, z, z, z, z, bias)


def attn_bwd(name, z, bias, do):
    s = z.shape[0]
    nt = s // ATT_TILE

    def body(q_ref, kp_ref, kc_ref, vp_ref, vc_ref, b_ref, do_ref, dq_ref, dk_ref, dv_ref, db_ref, kacc, vacc):
        n = pl.program_id(1)

        @pl.when(n == 0)
        def _():
            db_ref[...] = jnp.zeros(db_ref.shape, F32)

        @pl.when(n < nt)
        def _():
            first = n == 0
            fn = functools.partial(_attn_tile, first=first)
            _, vjp = jax.vjp(fn, q_ref[...], kp_ref[...], kc_ref[...], vp_ref[...], vc_ref[...], b_ref[...])
            dq, dkp, dkc, dvp, dvc, db = vjp(do_ref[...].astype(F32))
            dq_ref[...] = dq.astype(dq_ref.dtype)
            db_ref[...] += db

            @pl.when(n > 0)
            def _():
                dk_ref[...] = (kacc[...] + dkp).astype(dk_ref.dtype)
                dv_ref[...] = (vacc[...] + dvp).astype(dv_ref.dtype)

            kacc[...] = dkc
            vacc[...] = dvc

        @pl.when(n == nt)
        def _():
            dk_ref[...] = kacc[...].astype(dk_ref.dtype)
            dv_ref[...] = vacc[...].astype(dv_ref.dtype)

    cur = lambda n: jnp.minimum(n, nt - 1)
    prev = lambda n: jnp.clip(n - 1, 0, nt - 1)
    o_cur = pl.BlockSpec((ATT_TILE, LANE), lambda hp, n: (jnp.minimum(n, nt - 1), hp))
    o_prev = pl.BlockSpec((ATT_TILE, LANE), lambda hp, n: (jnp.maximum(n - 1, 0), hp))
    return pl.pallas_call(
        body, name=name, grid=(N_HEADS // HEADS_PER_STEP, nt + 1),
        in_specs=[_z_spec(Z_Q, cur), _z_spec(Z_K, prev), _z_spec(Z_K, cur), _z_spec(Z_V, prev), _z_spec(Z_V, cur),
                  pl.BlockSpec((HEADS_PER_STEP, ATT_TILE, 2 * ATT_TILE), lambda hp, n: (hp, 0, 0)), o_cur],
        out_specs=[o_cur, o_prev, o_prev, pl.BlockSpec((HEADS_PER_STEP, ATT_TILE, 2 * ATT_TILE), lambda hp, n: (hp, 0, 0))],
        out_shape=[jax.ShapeDtypeStruct((s, D_ATTN), BF16)] * 3 + [jax.ShapeDtypeStruct((N_HEADS, ATT_TILE, 2 * ATT_TILE), F32)],
        scratch_shapes=[pltpu.VMEM((ATT_TILE, LANE), F32), pltpu.VMEM((ATT_TILE, LANE), F32)],
        compiler_params=pltpu.CompilerParams(dimension_semantics=("parallel", "arbitrary"), vmem_limit_bytes=VMEM_LIMIT),
    )(z, z, z, z, z, bias, do)


def loss_head(name, y, tgt, tm):
    s, d = y.shape

    def body(y_ref, t_ref, l_ref, dy_ref):
        i = pl.program_id(0)
        diff = y_ref[...] - t_ref[...]
        dy_ref[...] = diff * (1.0 / d)
        part = 0.5 * jnp.sum(jnp.mean(diff * diff, axis=-1, keepdims=True), axis=0, keepdims=True)

        @pl.when(i == 0)
        def _():
            l_ref[...] = jnp.zeros(l_ref.shape, F32)

        l_ref[...] += jnp.broadcast_to(part, l_ref.shape)

    row = pl.BlockSpec((tm, d), lambda i: (i, 0))
    return pl.pallas_call(
        body, name=name, grid=(s // tm,), in_specs=[row, row],
        out_specs=[pl.BlockSpec((8, LANE), lambda i: (0, 0)), row],
        out_shape=[jax.ShapeDtypeStruct((8, LANE), F32), jax.ShapeDtypeStruct((s, d), F32)],
        compiler_params=pltpu.CompilerParams(dimension_semantics=("arbitrary",), vmem_limit_bytes=VMEM_LIMIT),
    )(y, tgt)


def adamw(name, w, g, m, v, tr):
    r, c = w.shape
    assert r % tr == 0, (name, w.shape, tr)

    def body(w_ref, g_ref, m_ref, v_ref, d_ref, nm_ref, nv_ref):
        gg = g_ref[...]
        m2 = ADAM_B1 * m_ref[...] + (1.0 - ADAM_B1) * gg
        v2 = ADAM_B2 * v_ref[...] + (1.0 - ADAM_B2) * (gg * gg)
        m_hat = m2 / (1.0 - ADAM_B1 ** ADAM_STEP)
        v_hat = v2 / (1.0 - ADAM_B2 ** ADAM_STEP)
        d_ref[...] = -ADAM_LR * (m_hat / (jnp.sqrt(v_hat) + ADAM_EPS) + ADAM_WD * w_ref[...])
        nm_ref[...] = m2
        nv_ref[...] = v2

    blk = pl.BlockSpec((tr, c), lambda i: (i, 0))
    return pl.pallas_call(
        body, name=name, grid=(r // tr,), in_specs=[blk] * 4, out_specs=[blk] * 3,
        out_shape=[jax.ShapeDtypeStruct((r, c), F32)] * 3,
        compiler_params=pltpu.CompilerParams(dimension_semantics=("parallel",), vmem_limit_bytes=VMEM_LIMIT),
    )(w, g, m, v)


def adamw_sum(name, w, layer_blocks, m, v, tr):
    rows, c = w.shape
    nl = len(layer_blocks)
    nb, r, _ = layer_blocks[0].shape
    assert rows == nl * r and r % tr == 0, (name, w.shape, layer_blocks[0].shape, tr)
    per = r // tr

    def body(*refs):
        w_ref, b_refs, (m_ref, v_ref, g_ref, d_ref, nm_ref, nv_ref) = refs[0], refs[1:1 + nl], refs[1 + nl:]
        i = pl.program_id(0)

        def update(b_ref):
            gg = b_ref[0].astype(F32)
            for j in range(1, nb):
                gg = gg + b_ref[j].astype(F32)
            g_ref[...] = gg
            m2 = ADAM_B1 * m_ref[...] + (1.0 - ADAM_B1) * gg
            v2 = ADAM_B2 * v_ref[...] + (1.0 - ADAM_B2) * (gg * gg)
            m_hat = m2 / (1.0 - ADAM_B1 ** ADAM_STEP)
            v_hat = v2 / (1.0 - ADAM_B2 ** ADAM_STEP)
            d_ref[...] = -ADAM_LR * (m_hat / (jnp.sqrt(v_hat) + ADAM_EPS) + ADAM_WD * w_ref[...])
            nm_ref[...] = m2
            nv_ref[...] = v2

        for l in range(nl):
            pl.when(jnp.logical_and(i >= l * per, i < (l + 1) * per))(functools.partial(update, b_refs[l]))

    blk = pl.BlockSpec((tr, c), lambda i: (i, 0))
    b_specs = [pl.BlockSpec((nb, tr, c), lambda i, l=l: (0, jnp.clip(i - l * per, 0, per - 1), 0)) for l in range(nl)]
    return pl.pallas_call(
        body, name=name, grid=(rows // tr,),
        in_specs=[blk] + b_specs + [blk, blk], out_specs=[blk] * 4,
        out_shape=[jax.ShapeDtypeStruct((rows, c), F32)] * 4,
        compiler_params=pltpu.CompilerParams(dimension_semantics=("arbitrary",), vmem_limit_bytes=VMEM_LIMIT),
    )(w, *layer_blocks, m, v)


def sum_blocks(name, blocks, tr):
    nb, r, c = blocks.shape
    assert r % tr == 0, (name, blocks.shape, tr)

    def body(b_ref, o_ref):
        acc = b_ref[0].astype(F32)
        for j in range(1, nb):
            acc = acc + b_ref[j].astype(F32)
        o_ref[...] = acc

    return pl.pallas_call(
        body, name=name, grid=(r // tr,),
        in_specs=[pl.BlockSpec((nb, tr, c), lambda i: (0, i, 0))],
        out_specs=pl.BlockSpec((tr, c), lambda i: (i, 0)),
        out_shape=jax.ShapeDtypeStruct((r, c), F32),
        compiler_params=pltpu.CompilerParams(dimension_semantics=("parallel",), vmem_limit_bytes=VMEM_LIMIT),
    )(blocks)


FLIPS = [(0, 0, 1), (1, 0, 0), (0, 1, 0), (1, 1, 0), (1, 0, 1), (0, 1, 1), (1, 1, 1)]
ANY = pl.BlockSpec(memory_space=pl.ANY)


def _me():
    return lax.axis_index("x"), lax.axis_index("y"), lax.axis_index("c")


def _flip(pos, f):
    return tuple((1 - p) if fi else p for p, fi in zip(pos, f))


def _idx(pos):
    return 4 * pos[0] + 2 * pos[1] + pos[2]


def all_gather_multi(name, shards):
    n = len(shards)

    def body(*refs):
        x_refs, out_refs = refs[:n], refs[n:2 * n]
        send_sems, recv_sems, local_sems = refs[2 * n:]
        x, y, cc = _me()
        me, sibling = (x, y, cc), (x, y, 1 - cc)
        chips = [(1 - x, y), (x, 1 - y), (1 - x, 1 - y)]

        def copy(a, k, block, to, src=None):
            dst = out_refs[a].at[_idx(block)]
            return pltpu.make_async_remote_copy(
                src_ref=dst if src is None else src, dst_ref=dst, send_sem=send_sems.at[7 * a + k],
                recv_sem=recv_sems.at[7 * a + k], device_id=to, device_id_type=MESH)

        mine = [pltpu.make_async_copy(x_refs[a], out_refs[a].at[_idx(me)], local_sems.at[a]) for a in range(n)]
        for cp in mine:
            cp.start()
        first = []
        for a in range(n):
            first.append(copy(a, 0, me, sibling, src=x_refs[a]))
            first += [copy(a, 1 + j, me, (*chip, cc), src=x_refs[a]) for j, chip in enumerate(chips)]
        for cp in first:
            cp.start()
        passed = []
        for j, chip in enumerate(chips):
            for a in range(n):
                copy(a, 1 + j, (*chip, cc), me).wait_recv()
                fwd = copy(a, 4 + j, (*chip, cc), sibling)
                fwd.start()
                passed.append(fwd)
        for a in range(n):
            copy(a, 0, sibling, me).wait_recv()
            for j, chip in enumerate(chips):
                copy(a, 4 + j, (*chip, 1 - cc), me).wait_recv()
        for cp in first + passed:
            cp.wait_send()
        for cp in mine:
            cp.wait()

    return pl.pallas_call(
        body, name=name, in_specs=[ANY] * n, out_specs=[ANY] * n,
        out_shape=[jax.ShapeDtypeStruct((N_DEV,) + a.shape, a.dtype) for a in shards],
        scratch_shapes=[pltpu.SemaphoreType.DMA((7 * n,)), pltpu.SemaphoreType.DMA((7 * n,)),
                        pltpu.SemaphoreType.DMA((n,))],
    )(*shards)


def all_to_all_multi(name, blocks):
    n = len(blocks)

    def body(*refs):
        in_refs, out_refs = refs[:n], refs[n:2 * n]
        send_sems, recv_sems, local_sems = refs[2 * n:]
        me = _me()
        mi = _idx(me)
        mine = [pltpu.make_async_copy(in_refs[a].at[mi], out_refs[a].at[mi], local_sems.at[a]) for a in range(n)]
        for cp in mine:
            cp.start()
        sends, recvs = [], []
        for k, f in enumerate(FLIPS):
            peer = _flip(me, f)
            pi = _idx(peer)
            for a in range(n):
                sems = dict(send_sem=send_sems.at[7 * a + k], recv_sem=recv_sems.at[7 * a + k],
                            device_id=peer, device_id_type=MESH)
                sends.append(pltpu.make_async_remote_copy(src_ref=in_refs[a].at[pi], dst_ref=out_refs[a].at[mi], **sems))
                recvs.append(pltpu.make_async_remote_copy(src_ref=in_refs[a].at[mi], dst_ref=out_refs[a].at[pi], **sems))
        for cp in sends:
            cp.start()
        for cp in recvs:
            cp.wait_recv()
        for cp in sends:
            cp.wait_send()
        for cp in mine:
            cp.wait()

    return pl.pallas_call(
        body, name=name, in_specs=[ANY] * n, out_specs=[ANY] * n,
        out_shape=[jax.ShapeDtypeStruct(a.shape, a.dtype) for a in blocks],
        scratch_shapes=[pltpu.SemaphoreType.DMA((7 * n,)), pltpu.SemaphoreType.DMA((7 * n,)),
                        pltpu.SemaphoreType.DMA((n,))],
    )(*blocks)


HBM = pl.BlockSpec(memory_space=pltpu.HBM)
SEM = pl.BlockSpec(memory_space=pltpu.SEMAPHORE)
DATAFLOW = pltpu.SideEffectType.DATAFLOW_SIDE_EFFECTING


def _exchange_copies(a_refs, l_refs, send_sems, recv_sems, gather):
    me = _me()
    mi = _idx(me)
    out = []
    for k, f in enumerate(FLIPS):
        peer = _flip(me, f)
        for a in range(len(a_refs)):
            src = a_refs[a] if gather else a_refs[a].at[_idx(peer)]
            out.append(pltpu.make_async_remote_copy(
                src_ref=src, dst_ref=l_refs[a].at[mi], send_sem=send_sems.at[7 * a + k],
                recv_sem=recv_sems.at[7 * a + k], device_id=peer, device_id_type=MESH))
    return out


def exchange_start(name, arrays, gather):
    n = len(arrays)
    lands = [lax.empty(((N_DEV,) + a.shape) if gather else a.shape, a.dtype) for a in arrays]

    def body(*refs):
        a_refs, l_refs = refs[:n], refs[n:2 * n]
        send_sems, recv_sems = refs[2 * n], refs[2 * n + 1]
        token = refs[4 * n + 2]
        for cp in _exchange_copies(a_refs, l_refs, send_sems, recv_sems, gather):
            cp.start()
        token[...] = jnp.zeros_like(token)

    hbm = lambda a: pltpu.HBM(a.shape, a.dtype)
    res = pl.pallas_call(
        body, name=name,
        out_shape=(pltpu.SemaphoreType.DMA((7 * n,)), pltpu.SemaphoreType.DMA((7 * n,)),
                   *[hbm(a) for a in arrays], *[hbm(a) for a in lands], jax.ShapeDtypeStruct((8, LANE), F32)),
        in_specs=[HBM] * (2 * n),
        out_specs=(SEM, SEM, *[HBM] * (2 * n), pl.BlockSpec(memory_space=pltpu.VMEM)),
        input_output_aliases={i: i + 2 for i in range(2 * n)},
        compiler_params=pltpu.CompilerParams(has_side_effects=DATAFLOW),
    )(*[pltpu.with_memory_space_constraint(a, pltpu.HBM) for a in arrays],
      *[pltpu.with_memory_space_constraint(a, pltpu.HBM) for a in lands])
    return res[0], res[1], list(res[2:2 + n]), list(res[2 + n:2 + 2 * n]), res[-1]


def exchange_wait(name, send_sems, recv_sems, arrays, lands, after, gather):
    n = len(arrays)

    def body(*refs):
        a_refs, l_refs = refs[:n], refs[n:2 * n]
        ssem, rsem = refs[2 * n], refs[2 * n + 1]
        for cp in _exchange_copies(a_refs, l_refs, ssem, rsem, gather):
            cp.wait_send()
            cp.wait_recv()

    hbm = lambda a: pltpu.HBM(a.shape, a.dtype)
    res = pl.pallas_call(
        body, name=name,
        out_shape=(*[hbm(a) for a in arrays], *[hbm(a) for a in lands]),
        in_specs=[HBM] * (2 * n) + [SEM, SEM, pl.BlockSpec(memory_space=pl.ANY)],
        out_specs=tuple([HBM] * (2 * n)),
        input_output_aliases={i: i for i in range(2 * n)},
        compiler_params=pltpu.CompilerParams(has_side_effects=DATAFLOW),
    )(*arrays, *lands, send_sems, recv_sems, after)
    return list(res[n:])


def ada_fwd(name, c_row, w_cat, b_lay):
    d = c_row.shape[1]
    ncol = w_cat.shape[1]
    vmem = pl.BlockSpec(memory_space=pltpu.VMEM)

    def body(c_ref, w_ref, b_ref, mod_ref, cact_ref, call, send, land, s1, r1, s2, r2):
        me = _me()
        mi = _idx(me)
        call[mi] = c_ref[...]

        def exchange(src_of, dst_buf, ssem, rsem):
            sends, recvs = [], []
            for k, f in enumerate(FLIPS):
                peer = _flip(me, f)
                sends.append(pltpu.make_async_remote_copy(
                    src_ref=src_of(peer), dst_ref=dst_buf.at[mi], send_sem=ssem.at[k], recv_sem=rsem.at[k],
                    device_id=peer, device_id_type=MESH))
                recvs.append(pltpu.make_async_remote_copy(
                    src_ref=src_of(peer), dst_ref=dst_buf.at[_idx(peer)], send_sem=ssem.at[k], recv_sem=rsem.at[k],
                    device_id=peer, device_id_type=MESH))
            for cp in sends:
                cp.start()
            for cp in recvs:
                cp.wait_recv()
            for cp in sends:
                cp.wait_send()

        exchange(lambda peer: c_ref, call, s1, r1)
        for p in range(N_DEV):
            cact_ref[pl.ds(p, 1), :] = jax.nn.silu(call[p])
        res = _dg(cact_ref[...], w_ref[...], 1, 0)
        for p in range(N_DEV):
            send[p] = res[p:p + 1, :]
        land[mi] = send[mi]
        exchange(lambda peer: send.at[_idx(peer)], land, s2, r2)
        mod_ref[...] = land[...] + b_ref[...]

    return pl.pallas_call(
        body, name=name, in_specs=[vmem, vmem, vmem], out_specs=[vmem, vmem],
        out_shape=[jax.ShapeDtypeStruct((N_DEV, 1, ncol), F32), jax.ShapeDtypeStruct((N_DEV, d), F32)],
        scratch_shapes=[pltpu.VMEM((N_DEV, 1, d), F32), pltpu.VMEM((N_DEV, 1, ncol), F32),
                        pltpu.VMEM((N_DEV, 1, ncol), F32),
                        pltpu.SemaphoreType.DMA((7,)), pltpu.SemaphoreType.DMA((7,)),
                        pltpu.SemaphoreType.DMA((7,)), pltpu.SemaphoreType.DMA((7,))],
        compiler_params=pltpu.CompilerParams(vmem_limit_bytes=VMEM_LIMIT),
    )(c_row, w_cat, b_lay)


POOL_WINDOWS = (2, 4, 8, 16)
POOL_GROUP = 64
N_REL = 2 * REL_CLIP + 1
PACK_COLS = 1024
SMALL_NAMES = ["b_ada", "b_gate", "w_pool", "pool_scale", "rel_bias", "conv_w", "conv_b", "conv_ln_g",
               "conv_ln_b", "ln_mix_g", "ln_mix_b", "b_ff1", "b_ff2", "ln_ff_g", "ln_ff_b"]
BIG_NAMES = ["w_in", "w_br_pool", "w_br_attn", "w_br_conv", "w_o", "w_ff1", "w_ff2"]
ROW_SHARDED = ("w_o", "w_ff2")
WEIGHT_NAMES = ["w_ada", "b_ada", "w_in", "b_gate", "w_pool", "pool_scale", "rel_bias", "conv_w", "conv_b",
                "conv_ln_g", "conv_ln_b", "w_br_pool", "w_br_attn", "w_br_conv", "w_o", "ln_mix_g", "ln_mix_b",
                "w_ff1", "b_ff1", "w_ff2", "b_ff2", "ln_ff_g", "ln_ff_b"]


def _perm_cols(w):
    return jnp.concatenate([w[:, 2304:], w[:, 1792:2304], w[:, :256], w[:, 256:768], w[:, 768:1280],
                            w[:, 1280:1792]], axis=1)


def _unperm_cols(wp):
    return jnp.concatenate([wp[:, Z_POOL:Z_Q], wp[:, Z_Q:Z_K], wp[:, Z_K:Z_V], wp[:, Z_V:],
                            wp[:, Z_CONV:Z_POOL], wp[:, :Z_CONV]], axis=1)


def _bias_table(rel_bias):
    far = jnp.broadcast_to(rel_bias[:, 2 * REL_CLIP:], (N_HEADS, BAND - REL_CLIP))
    near = rel_bias[:, REL_CLIP - CHUNK + 1:2 * REL_CLIP][:, ::-1]
    ext = jnp.concatenate([far, near], axis=1)
    return jnp.stack([ext[:, CHUNK - 1 - qi:CHUNK - 1 - qi + BAND] for qi in range(CHUNK)], axis=1)


def _bias_full(rel_bias):
    tab = _bias_table(rel_bias)
    return jnp.concatenate(
        [jnp.pad(tab, ((0, 0), (0, 0), (i * CHUNK, 2 * ATT_TILE - BAND - i * CHUNK)), constant_values=NEG_INF)
         for i in range(CHUNKS_PER_TILE)], axis=1)


def _block_diag(w_pool):
    out = jnp.zeros((D_POOL, D_POOL), F32)
    for g in range(len(POOL_WINDOWS)):
        out = lax.dynamic_update_slice(out, w_pool[g], (g * POOL_GROUP, g * POOL_GROUP))
    return out


def _flat_pad(arrs, mult):
    flat = jnp.concatenate([a.reshape(-1) for a in arrs])
    pad = (-flat.shape[0]) % mult
    return jnp.pad(flat, (0, pad)) if pad else flat


def _unflat(flat, shapes):
    out, off = [], 0
    for shp in shapes:
        n = int(np.prod(shp))
        out.append(flat[off:off + n].reshape(shp))
        off += n
    return out


def _to_blocks(name, full):
    k, n = full.shape
    if name in ROW_SHARDED:
        return full.reshape(N_DEV, k // N_DEV, n)
    return full.reshape(k, N_DEV, n // N_DEV).transpose(1, 0, 2)


def _from_blocks(name, blocks):
    nb, r, c = blocks.shape
    if name in ROW_SHARDED:
        return blocks.reshape(nb * r, c)
    return blocks.transpose(1, 0, 2).reshape(r, nb * c)


class _Layer:
    pass


def _row(v):
    return v.reshape(1, -1)


def _layer_fwd(x, modr, w, wvec):
    sh_m, sc_m, g_m, sh_f, sc_f, g_f = modr
    (u,) = row_fwd("lnmod_mix", f_lnmod, [x], [sc_m, sh_m], [(D_MODEL, BF16)], 512)
    z = mm_big("mm_in", u, w.w_in, "nn", (512, 896, 1024), F32, j_outer=True)
    p = pool_lin("pool_fwd", (z, D_POOL, Z_POOL // D_POOL), wvec, False, F32)
    ao = attn_fwd("attn_fwd", z, w.bias)
    cv = conv_fwd("conv_fwd", (z, 2 * D_CONV, Z_CONV // (2 * D_CONV)), w.conv_w)
    mparams = [w.wbd, w.ps, w.wbp, w.wba, w.wbc, w.cb, w.clg, w.clb, w.bg, w.wo, g_m, w.lmg, w.lmb]
    (x1,) = row_fwd("merge", f_merge, [p, ao, cv, (z, 3 * D_MODEL, 0), x], mparams, [(D_MODEL, F32)], 256)
    (u2,) = row_fwd("lnmod_ff", f_lnmod, [x1], [sc_f, sh_f], [(D_MODEL, BF16)], 512)
    hpre = mm_big("mm_ff1", u2, w.w_ff1, "nn", (512, 1024, 1024), F32, j_outer=True)
    (h,) = row_fwd("relu2", f_relu2, [hpre], [w.b1], [(D_FF, BF16)], 512)
    ff = mm_big("mm_ff2", h, w.w_ff2, "nn", (512, 1024, 4096), F32)
    (x2,) = row_fwd("ffout", f_ffout, [x1, ff], [w.b2, g_f, w.lfg, w.lfb], [(D_MODEL, F32)], 512)
    return x2, (x, u, z, p, ao, cv, x1, u2, hpre, h, ff, mparams)


GRAD_GROUPS = [("ff", ["w_ff2", "w_ff1"]), ("mix", ["w_o", "w_br_pool", "w_br_attn", "w_br_conv"]), ("in", ["w_in"])]


def _layer_bwd(dx2, saved, modr, w, wvec, ready):
    x, u, z, p, ao, cv, x1, u2, hpre, h, ff, mparams = saved
    sh_m, sc_m, g_m, sh_f, sc_f, g_f = modr
    g = {}
    (dx1a, dff), (g["b_ff2"], dgf, g["ln_ff_g"], g["ln_ff_b"]) = row_bwd(
        "ffout_bwd", f_ffout, [x1, ff], [w.b2, g_f, w.lfg, w.lfb], [dx2], 512, [(0, F32), (1, BF16)], [0, 1, 2, 3])
    dh = mm_big("mm_dh", dff, w.w_ff2, "nt", (512, 1024, 1024), F32, j_outer=True)
    g["w_ff2"] = mm_big("mm_dw_ff2", h, dff, "tn", (1024, 1024, 2048), F32)
    (dhpre,), (g["b_ff1"],) = row_bwd("relu2_bwd", f_relu2, [hpre], [w.b1], [dh], 512, [(0, BF16)], [0])
    du2 = mm_big("mm_du2", dhpre, w.w_ff1, "nt", (512, 1024, 4096), F32)
    g["w_ff1"] = mm_big("mm_dw_ff1", u2, dhpre, "tn", (1024, 1024, 2048), F32)
    sc_f = sc_f + ready("ff", g)
    (dx1,), (dscf, dshf) = row_bwd("lnmod_ff_bwd", f_lnmod, [x1], [sc_f, sh_f], [du2], 512, [(0, F32)], [0, 1],
                                   add_to=dx1a)
    (dp, dao, dcv, dzg, dxa), dm = row_bwd(
        "merge_bwd", f_merge, [p, ao, cv, (z, 3 * D_MODEL, 0), x], mparams, [dx1], 256,
        [(0, F32), (1, BF16), (2, F32), (3, BF16), (4, F32)], list(range(13)))
    (dwbd, g["pool_scale"], g["w_br_pool"], g["w_br_attn"], g["w_br_conv"], g["conv_b"], g["conv_ln_g"],
     g["conv_ln_b"], g["b_gate"], g["w_o"], dgm, g["ln_mix_g"], g["ln_mix_b"]) = dm
    g["w_pool"] = jnp.stack([dwbd[i * POOL_GROUP:(i + 1) * POOL_GROUP, i * POOL_GROUP:(i + 1) * POOL_GROUP]
                             for i in range(len(POOL_WINDOWS))])
    dzp = pool_lin("pool_bwd", (dp, D_POOL, 0), wvec + ready("mix", g), True, BF16)
    dq, dk, dv, dbias = attn_bwd("attn_bwd", z, w.bias, dao)
    (g["rel_bias"],) = w.bias_vjp(dbias)
    dzc, dcw = conv_bwd("conv_bwd", (z, 2 * D_CONV, Z_CONV // (2 * D_CONV)), w.conv_w, dcv)
    g["conv_w"] = dcw[:CONV_WIDTH]
    dz = jnp.concatenate([dzg, dzc, dzp, dq, dk, dv], axis=1)
    du = mm_big("mm_du", dz, w.w_in, "nt", (512, 1024, D_IN), F32)
    g["w_in"] = _unperm_cols(mm_big("mm_dw_in", u, dz, "tn", (1024, 896, 2048), F32))
    sc_m = sc_m + ready("in", g)
    (dx,), (dscm, dshm) = row_bwd("lnmod_mix_bwd", f_lnmod, [x], [sc_m, sh_m], [du], 512, [(0, F32)], [0, 1],
                                  add_to=dxa)
    dmod = jnp.concatenate([dshm, dscm, dgm, dshf, dscf, dgf], axis=1)
    return dx, dmod, g


def kernel(x, c, w_ada, b_ada, w_in, b_gate, w_pool, pool_scale, rel_bias, conv_w, conv_b, conv_ln_g, conv_ln_b, w_br_pool, w_br_attn, w_br_conv, w_o, ln_mix_g, ln_mix_b, w_ff1, b_ff1, w_ff2, b_ff2, ln_ff_g, ln_ff_b, loss_target, m_w_ada, m_b_ada, m_w_in, m_b_gate, m_w_pool, m_pool_scale, m_rel_bias, m_conv_w, m_conv_b, m_conv_ln_g, m_conv_ln_b, m_w_br_pool, m_w_br_attn, m_w_br_conv, m_w_o, m_ln_mix_g, m_ln_mix_b, m_w_ff1, m_b_ff1, m_w_ff2, m_b_ff2, m_ln_ff_g, m_ln_ff_b, v_w_ada, v_b_ada, v_w_in, v_b_gate, v_w_pool, v_pool_scale, v_rel_bias, v_conv_w, v_conv_b, v_conv_ln_g, v_conv_ln_b, v_w_br_pool, v_w_br_attn, v_w_br_conv, v_w_o, v_ln_mix_g, v_ln_mix_b, v_w_ff1, v_b_ff1, v_w_ff2, v_b_ff2, v_ln_ff_g, v_ln_ff_b):
    args = dict(locals())
    wts = {n: args[n] for n in WEIGHT_NAMES}
    mom = {n: args["m_" + n] for n in WEIGHT_NAMES}
    var = {n: args["v_" + n] for n in WEIGHT_NAMES}
    me = 4 * lax.axis_index("x") + 2 * lax.axis_index("y") + lax.axis_index("c")
    xs, tgt = x[0], loss_target[0]
    nc_ada = w_ada.shape[2]
    wvec = jnp.asarray(np.repeat(np.array(POOL_WINDOWS, np.float32), POOL_GROUP)[None, :])

    w_cat = jnp.concatenate([w_ada[l] for l in range(DEPTH)], axis=1)
    b_lay = b_ada.reshape(DEPTH, N_DEV, nc_ada).transpose(1, 0, 2).reshape(N_DEV, 1, DEPTH * nc_ada)
    land, cact = ada_fwd("ada_fwd", c, w_cat, b_lay)
    mod = land.reshape(N_DEV, DEPTH, nc_ada).transpose(1, 0, 2).reshape(DEPTH, 6 * D_MODEL)
    modr = [[mod[l:l + 1, i * D_MODEL:(i + 1) * D_MODEL] for i in range(6)] for l in range(DEPTH)]

    cw_pack = _flat_pad([conv_w], 8 * LANE).reshape(-1, LANE)
    shards = [[wts[n][l].astype(BF16) for n in BIG_NAMES] for l in range(DEPTH)]
    gathered0 = all_gather_multi("gather_weights_l0", shards[0] + [cw_pack])
    cw_all = gathered0[-1].reshape(N_DEV, -1)[:, :conv_w.size]
    conv_full = cw_all.reshape((N_DEV,) + conv_w.shape).transpose(1, 2, 0, 3).reshape(DEPTH, CONV_WIDTH, D_CONV)
    ssem, rsem, thru, lands, token = exchange_start("gather_weights_l1_start", shards[1], True)
    modr[0] = [r + token[0:1, 0:1] for r in modr[0]]

    def layer_weights(l, blocks):
        full = {n: _from_blocks(n, g) for n, g in zip(BIG_NAMES, blocks)}
        w = _Layer()
        w.w_in = _perm_cols(full["w_in"])
        w.wbp, w.wba, w.wbc = full["w_br_pool"], full["w_br_attn"], full["w_br_conv"]
        w.wo, w.w_ff1, w.w_ff2 = full["w_o"], full["w_ff1"], full["w_ff2"]
        w.wbd = _block_diag(w_pool[l])
        w.ps, w.cb, w.clg, w.clb = _row(pool_scale[l]), _row(conv_b[l]), _row(conv_ln_g[l]), _row(conv_ln_b[l])
        w.bg, w.lmg, w.lmb = _row(b_gate[l]), _row(ln_mix_g[l]), _row(ln_mix_b[l])
        w.b1, w.b2, w.lfg, w.lfb = _row(b_ff1[l]), _row(b_ff2[l]), _row(ln_ff_g[l]), _row(ln_ff_b[l])
        w.conv_w = jnp.pad(conv_full[l], ((0, CONV_PAD - CONV_WIDTH), (0, 0)))
        w.bias, w.bias_vjp = jax.vjp(_bias_full, rel_bias[l])
        return w

    def with_own(lands_, own):
        return [lax.dynamic_update_index_in_dim(ld, o, me, axis=0) for ld, o in zip(lands_, own)]

    layers, saved = [layer_weights(0, gathered0[:-1])], []
    h, sv = _layer_fwd(xs, modr[0], layers[0], wvec)
    saved.append(sv)
    lands = exchange_wait("gather_weights_l1_wait", ssem, rsem, thru, lands, h, True)
    layers.append(layer_weights(1, with_own(lands, shards[1])))
    h, sv = _layer_fwd(h, modr[1], layers[1], wvec)
    saved.append(sv)
    lpart, dy = loss_head("loss_head", h, tgt, 512)
    loss = lax.psum(lpart[0, 0], ("x", "y", "c"))
    grads, dmods = [None] * DEPTH, [None] * DEPTH
    pending = {}

    def ready_for(l):
        def ready(group, g):
            names = dict(GRAD_GROUPS)[group]
            blocks = [_to_blocks(n, g[n]).astype(BF16) for n in names]
            ssem, rsem, thru, lands, token = exchange_start(f"scatter_l{l}_{group}_start", blocks, False)
            pending[(l, group)] = (names, blocks, ssem, rsem, thru, lands)
            return token[0:1, 0:1]
        return ready

    def received(l, group, after):
        names, blocks, ssem, rsem, thru, lands = pending[(l, group)]
        lands = exchange_wait(f"scatter_l{l}_{group}_wait", ssem, rsem, thru, lands, after, False)
        own = [lax.dynamic_index_in_dim(b, me, axis=0, keepdims=False) for b in blocks]
        return dict(zip(names, with_own(lands, own)))

    for l in reversed(range(DEPTH)):
        dy, dmods[l], grads[l] = _layer_bwd(dy, saved[l], modr[l], layers[l], wvec, ready_for(l))
    recv = [{} for _ in range(DEPTH)]
    for l, group in [(1, "ff"), (1, "mix"), (1, "in"), (0, "ff"), (0, "mix")]:
        recv[l].update(received(l, group, dy))
    grad_x = dy[None]
    dmod = jnp.concatenate(dmods, axis=0)

    small_shapes = [wts[n].shape if n != "conv_w" else (DEPTH, CONV_WIDTH, D_CONV) for n in SMALL_NAMES]
    small_local = [dmod] + [jnp.stack([grads[l][n].reshape(shp[1:]) for l in range(DEPTH)])
                            for n, shp in zip(SMALL_NAMES[1:], small_shapes[1:])]
    small_pack = _flat_pad(small_local, 8 * LANE).reshape(-1, LANE)
    (small_all,) = all_gather_multi("gather_small_grads", [small_pack])
    small_sum = sum_blocks("sum_small_grads", small_all, small_pack.shape[0]).reshape(-1)
    gsmall = dict(zip(SMALL_NAMES, _unflat(small_sum, small_shapes)))
    gw = dict(gsmall)
    gw["conv_w"] = lax.dynamic_slice_in_dim(gsmall["conv_w"], me * conv_w.shape[2], conv_w.shape[2], axis=2)

    dmod_all = small_all.reshape(N_DEV, -1)[:, :dmod.size].reshape(N_DEV, DEPTH, N_DEV, nc_ada)
    dm_mine = lax.dynamic_index_in_dim(dmod_all, me, axis=2, keepdims=False).reshape(N_DEV, DEPTH * nc_ada)
    cact_t = jnp.pad(cact.T, ((0, 0), (0, LANE - N_DEV)))
    dm_pad = jnp.pad(dm_mine, ((0, LANE - N_DEV), (0, 0)))
    dw_cat = mm_big("mm_dw_ada", cact_t, dm_pad, "nn", (D_MODEL, DEPTH * nc_ada, LANE), F32)
    gw["w_ada"] = jnp.stack([dw_cat[:, l * nc_ada:(l + 1) * nc_ada] for l in range(DEPTH)])

    delta, new_m, new_v = {}, {}, {}
    for n in ["w_ada", "w_ff2", "w_ff1", "w_o", "w_br_pool", "w_br_attn", "w_br_conv", "w_in"]:
        shp = wts[n].shape
        two_d = lambda a, shp=shp: a.reshape(shp[0] * shp[1], shp[2])
        tr = min(256, shp[1])
        if n == "w_ada":
            res = (gw[n],) + tuple(adamw("adamw_" + n, two_d(wts[n]), two_d(gw[n]), two_d(mom[n]), two_d(var[n]), tr))
        else:
            if n == "w_in":
                recv[0].update(received(0, "in", new_v["w_br_conv"]))
            res = adamw_sum("adamw_" + n, two_d(wts[n]), [recv[l][n] for l in range(DEPTH)], two_d(mom[n]),
                            two_d(var[n]), tr)
        gw[n], delta[n], new_m[n], new_v[n] = [a.reshape(shp) for a in res]
    packs = [_flat_pad([src[n] for n in SMALL_NAMES], 8 * LANE).reshape(-1, LANE) for src in (wts, gw, mom, var)]
    res = adamw("adamw_small", *packs, packs[0].shape[0])
    shapes = [wts[n].shape for n in SMALL_NAMES]
    for out, flat in zip((delta, new_m, new_v), res):
        out.update(dict(zip(SMALL_NAMES, _unflat(flat.reshape(-1), shapes))))

    return (loss, grad_x, *[gw[n] for n in WEIGHT_NAMES], *[delta[n] for n in WEIGHT_NAMES],
            *[new_m[n] for n in WEIGHT_NAMES], *[new_v[n] for n in WEIGHT_NAMES])
```

```python
import functools

import jax
import jax.numpy as jnp
import numpy as np
from jax import lax
from jax.experimental import pallas as pl
from jax.experimental.pallas import tpu as pltpu

F32 = jnp.float32
BF16 = jnp.bfloat16
MESH = pl.DeviceIdType.MESH

D_MODEL = 1024
DEPTH = 2
CHUNK = 64
N_HEADS = 8
HEAD_DIM = 64
D_POOL = 256
D_ATTN = 512
D_CONV = 256
CONV_WIDTH = 31
D_FF = 4096
D_IN = 5376
N_PREV = 8
BAND = (N_PREV + 1) * CHUNK
REL_CLIP = 128
ALPHA = (2.0 * DEPTH) ** 0.25
LN_EPS = 1e-5
NEG_INF = -1e30
N_DEV = 8

ADAM_LR, ADAM_B1, ADAM_B2, ADAM_EPS, ADAM_WD, ADAM_STEP = 0.001, 0.9, 0.999, 1e-08, 0.01, 10

VMEM_LIMIT = 56 * 1024 * 1024

Z_GATE, Z_CONV, Z_POOL, Z_Q, Z_K, Z_V = 0, 3072, 3584, 3840, 4352, 4864
ATT_TILE = 512
LANE = 128


def _dg(a, b, ca, cb):
    return lax.dot_general(a.astype(BF16), b.astype(BF16), (((ca,), (cb,)), ((), ())),
                           preferred_element_type=F32)


@jax.custom_vjp
def mm_nn(a, b):
    return _dg(a, b, 1, 0)


def _mm_nn_fwd(a, b):
    return _dg(a, b, 1, 0), (a, b)


def _mm_nn_bwd(res, g):
    a, b = res
    return _dg(g, b, 1, 1).astype(a.dtype), _dg(a, g, 0, 0).astype(b.dtype)


mm_nn.defvjp(_mm_nn_fwd, _mm_nn_bwd)


@jax.custom_vjp
def mm_nt(a, b):
    return _dg(a, b, 1, 1)


def _mm_nt_fwd(a, b):
    return _dg(a, b, 1, 1), (a, b)


def _mm_nt_bwd(res, g):
    a, b = res
    return _dg(g, b, 1, 0).astype(a.dtype), _dg(g, a, 0, 0).astype(b.dtype)


mm_nt.defvjp(_mm_nt_fwd, _mm_nt_bwd)


def _ln(x):
    mu = jnp.mean(x, axis=-1, keepdims=True)
    xc = x - mu
    var = jnp.mean(xc * xc, axis=-1, keepdims=True)
    return xc * lax.rsqrt(var + LN_EPS)


def _norm_rows(rows):
    return [r if isinstance(r, tuple) else (r, r.shape[1], 0) for r in rows]


def _row_spec(tm, r):
    _, width, cb = r
    return pl.BlockSpec((tm, width), lambda i, cb=cb: (i, cb))


def _full_spec(a):
    nd = a.ndim
    return pl.BlockSpec(a.shape, lambda i, nd=nd: (0,) * nd)


def row_fwd(name, f, rows, params, outs, tm):
    rows = _norm_rows(rows)
    s = rows[0][0].shape[0]
    nr, npar = len(rows), len(params)

    def body(*refs):
        r = [x[...].astype(F32) for x in refs[:nr]]
        p = [x[...] for x in refs[nr:nr + npar]]
        res = f(*r, *p)
        for o_ref, o in zip(refs[nr + npar:], res):
            o_ref[...] = o.astype(o_ref.dtype)

    return pl.pallas_call(
        body, name=name, grid=(s // tm,),
        in_specs=[_row_spec(tm, r) for r in rows] + [_full_spec(p) for p in params],
        out_specs=[pl.BlockSpec((tm, w), lambda i: (i, 0)) for w, _ in outs],
        out_shape=[jax.ShapeDtypeStruct((s, w), dt) for w, dt in outs],
        compiler_params=pltpu.CompilerParams(dimension_semantics=("parallel",), vmem_limit_bytes=VMEM_LIMIT),
    )(*[r[0] for r in rows], *params)


def row_bwd(name, f, rows, params, douts, tm, want_rows, want_params, add_to=None):
    rows = _norm_rows(rows)
    s = rows[0][0].shape[0]
    nr, npar, nd = len(rows), len(params), len(douts)
    nadd = 0 if add_to is None else 1
    n_in = nr + npar + nd + nadd

    def body(*refs):
        i = pl.program_id(0)
        r = [x[...].astype(F32) for x in refs[:nr]]
        p = [x[...].astype(F32) for x in refs[nr:nr + npar]]
        d = [x[...].astype(F32) for x in refs[nr + npar:nr + npar + nd]]
        _, vjp = jax.vjp(f, *r, *p)
        g = vjp(tuple(d))
        out_refs = refs[n_in:]
        for k, (idx, _) in enumerate(want_rows):
            val = g[idx]
            if nadd and k == 0:
                val = val + refs[n_in - 1][...].astype(F32)
            out_refs[k][...] = val.astype(out_refs[k].dtype)
        for k, idx in enumerate(want_params):
            gp = g[nr + idx]
            o_ref = out_refs[len(want_rows) + k]

            @pl.when(i == 0)
            def _():
                o_ref[...] = gp

            @pl.when(i > 0)
            def _():
                o_ref[...] += gp

    in_specs = ([_row_spec(tm, r) for r in rows] + [_full_spec(p) for p in params]
                + [pl.BlockSpec((tm, d.shape[1]), lambda i: (i, 0)) for d in douts])
    args = [r[0] for r in rows] + list(params) + list(douts)
    if nadd:
        in_specs.append(pl.BlockSpec((tm, add_to.shape[1]), lambda i: (i, 0)))
        args.append(add_to)
    out_specs = ([pl.BlockSpec((tm, rows[idx][1]), lambda i: (i, 0)) for idx, _ in want_rows]
                 + [_full_spec(params[idx]) for idx in want_params])
    out_shape = ([jax.ShapeDtypeStruct((s, rows[idx][1]), dt) for idx, dt in want_rows]
                 + [jax.ShapeDtypeStruct(params[idx].shape, F32) for idx in want_params])
    res = pl.pallas_call(
        body, name=name, grid=(s // tm,), in_specs=in_specs, out_specs=out_specs, out_shape=out_shape,
        compiler_params=pltpu.CompilerParams(dimension_semantics=("arbitrary",), vmem_limit_bytes=VMEM_LIMIT),
    )(*args)
    return res[:len(want_rows)], res[len(want_rows):]


def f_lnmod(x, sc, sh):
    return (_ln(x) * (1.0 + sc) + sh,)


def f_merge(p, ao, cv, zg, x, wbd, ps, wbp, wba, wbc, cb, clg, clb, bg, wo, gm, lg, lb):
    pm = mm_nn(p, wbd) * ps
    co = jax.nn.silu(_ln(cv + cb) * clg + clb)
    y_pool = mm_nn(pm, wbp)
    y_attn = mm_nn(ao, wba)
    y_conv = mm_nn(co, wbc)
    gates = jax.nn.sigmoid(zg + bg)
    merged = (gates[:, :D_MODEL] * y_pool + gates[:, D_MODEL:2 * D_MODEL] * y_attn
              + gates[:, 2 * D_MODEL:] * y_conv)
    mix = mm_nn(merged, wo)
    return (_ln(ALPHA * x + gm * mix) * lg + lb,)


def f_relu2(hpre, b1):
    a = jax.nn.relu(hpre + b1)
    return (a * a,)


def f_ffout(x1, ff, b2, gf, lg, lb):
    return (_ln(ALPHA * x1 + gf * (ff + b2)) * lg + lb,)


def mm_big(name, a, b, kind, tiles, out_dtype, j_outer=False):
    if kind == "nn":
        o0, o1, red = a.shape[0], b.shape[1], a.shape[1]
    elif kind == "nt":
        o0, o1, red = a.shape[0], b.shape[0], a.shape[1]
    else:
        o0, o1, red = a.shape[1], b.shape[1], a.shape[0]
    t0, t1, tr = min(tiles[0], o0), min(tiles[1], o1), min(tiles[2], red)
    if kind == "nn":
        a_spec = pl.BlockSpec((t0, tr), lambda i, j, r: (i, r))
        b_spec = pl.BlockSpec((tr, t1), lambda i, j, r: (r, j))
        dims = (1, 0)
    elif kind == "nt":
        a_spec = pl.BlockSpec((t0, tr), lambda i, j, r: (i, r))
        b_spec = pl.BlockSpec((t1, tr), lambda i, j, r: (j, r))
        dims = (1, 1)
    else:
        a_spec = pl.BlockSpec((tr, t0), lambda i, j, r: (r, i))
        b_spec = pl.BlockSpec((tr, t1), lambda i, j, r: (r, j))
        dims = (0, 0)
    assert o0 % t0 == 0 and o1 % t1 == 0 and red % tr == 0, (name, a.shape, b.shape, tiles)
    n0, n1, nred = o0 // t0, o1 // t1, red // tr
    o_spec = pl.BlockSpec((t0, t1), lambda i, j, r: (i, j))
    if j_outer:
        swap = lambda spec: pl.BlockSpec(spec.block_shape, lambda j, i, r, f=spec.index_map: f(i, j, r))
        a_spec, b_spec, o_spec = swap(a_spec), swap(b_spec), swap(o_spec)
        grid = (n1, n0, nred)
    else:
        grid = (n0, n1, nred)

    if nred == 1:
        def body(a_ref, b_ref, o_ref):
            o_ref[...] = _dg(a_ref[...], b_ref[...], *dims).astype(o_ref.dtype)
        scratch = []
    else:
        def body(a_ref, b_ref, o_ref, acc_ref):
            r = pl.program_id(2)
            part = _dg(a_ref[...], b_ref[...], *dims)

            @pl.when(r == 0)
            def _():
                acc_ref[...] = part

            @pl.when(jnp.logical_and(r > 0, r < nred - 1))
            def _():
                acc_ref[...] += part

            @pl.when(r == nred - 1)
            def _():
                o_ref[...] = (acc_ref[...] + part).astype(o_ref.dtype)
        scratch = [pltpu.VMEM((t0, t1), F32)]

    return pl.pallas_call(
        body, name=name, grid=grid,
        in_specs=[a_spec, b_spec],
        out_specs=o_spec,
        out_shape=jax.ShapeDtypeStruct((o0, o1), out_dtype),
        scratch_shapes=scratch,
        compiler_params=pltpu.CompilerParams(dimension_semantics=("parallel", "parallel", "arbitrary"),
                                             vmem_limit_bytes=VMEM_LIMIT),
    )(a, b)


def mm_ff1_relu2(name, u2, w1, b1, tm, tn):
    m, k = u2.shape
    n = w1.shape[1]
    tm, tn = min(tm, m), min(tn, n)

    def body(a_ref, b_ref, bias_ref, hpre_ref, h_ref):
        acc = _dg(a_ref[...], b_ref[...], 1, 0)
        hpre_ref[...] = acc
        h_ref[...] = f_relu2(acc, bias_ref[...])[0].astype(h_ref.dtype)

    out = pl.BlockSpec((tm, tn), lambda j, i: (i, j))
    return pl.pallas_call(
        body, name=name, grid=(n // tn, m // tm),
        in_specs=[pl.BlockSpec((tm, k), lambda j, i: (i, 0)), pl.BlockSpec((k, tn), lambda j, i: (0, j)),
                  pl.BlockSpec((1, tn), lambda j, i: (0, j))],
        out_specs=[out, out],
        out_shape=[jax.ShapeDtypeStruct((m, n), F32), jax.ShapeDtypeStruct((m, n), BF16)],
        compiler_params=pltpu.CompilerParams(dimension_semantics=("parallel", "parallel"), vmem_limit_bytes=VMEM_LIMIT),
    )(u2, w1, b1)


def mm_dh_relu2(name, dff, w2, hpre, b1, tm, tn):
    m, k = dff.shape
    n = w2.shape[0]
    tm, tn = min(tm, m), min(tn, n)

    def body(a_ref, b_ref, hpre_ref, bias_ref, d_ref, db_ref):
        i = pl.program_id(1)
        dh = _dg(a_ref[...], b_ref[...], 1, 1)
        _, vjp = jax.vjp(f_relu2, hpre_ref[...], bias_ref[...])
        dhpre, db = vjp((dh,))
        d_ref[...] = dhpre.astype(d_ref.dtype)

        @pl.when(i == 0)
        def _():
            db_ref[...] = db

        @pl.when(i > 0)
        def _():
            db_ref[...] += db

    tile = pl.BlockSpec((tm, tn), lambda j, i: (i, j))
    col = pl.BlockSpec((1, tn), lambda j, i: (0, j))
    return pl.pallas_call(
        body, name=name, grid=(n // tn, m // tm),
        in_specs=[pl.BlockSpec((tm, k), lambda j, i: (i, 0)), pl.BlockSpec((tn, k), lambda j, i: (j, 0)), tile, col],
        out_specs=[tile, col],
        out_shape=[jax.ShapeDtypeStruct((m, n), BF16), jax.ShapeDtypeStruct((1, n), F32)],
        compiler_params=pltpu.CompilerParams(dimension_semantics=("parallel", "arbitrary"), vmem_limit_bytes=VMEM_LIMIT),
    )(dff, w2, hpre, b1)


POOL_PAD = 16
POOL_ROWS = 256


def pool_lin(name, x, wvec, transpose, out_dtype):
    arr, width, cb = x
    s = arr.shape[0]
    n_steps = s // POOL_ROWS

    def body(x_ref, w_ref, o_ref, xp_ref):
        wv = w_ref[...]
        zeros = jnp.zeros((POOL_PAD, width), F32)
        xp_ref[0:POOL_PAD, :] = zeros
        xp_ref[s + POOL_PAD:s + 2 * POOL_PAD, :] = zeros

        def count(t0):
            t = lax.broadcasted_iota(jnp.int32, (POOL_ROWS, width), 0) + (t0 + 1)
            return jnp.minimum(t.astype(F32), wv)

        def fill(i, carry):
            t0 = pl.multiple_of(i * POOL_ROWS, POOL_ROWS)
            v = x_ref[pl.ds(t0, POOL_ROWS), :].astype(F32)
            if transpose:
                v = v / count(t0)
            xp_ref[pl.ds(t0 + POOL_PAD, POOL_ROWS), :] = v
            return carry

        lax.fori_loop(0, n_steps, fill, 0)

        def step(i, carry):
            t0 = pl.multiple_of(i * POOL_ROWS, POOL_ROWS)
            win = xp_ref[pl.ds(t0, POOL_ROWS + 2 * POOL_PAD), :]
            acc = jnp.zeros((POOL_ROWS, width), F32)
            for j in range(POOL_PAD):
                off = POOL_PAD + j if transpose else POOL_PAD - j
                acc = acc + jnp.where(wv > j, win[off:off + POOL_ROWS, :], 0.0)
            cur = x_ref[pl.ds(t0, POOL_ROWS), :].astype(F32)
            res = acc - cur if transpose else acc / count(t0) - cur
            o_ref[pl.ds(t0, POOL_ROWS), :] = res.astype(o_ref.dtype)
            return carry

        lax.fori_loop(0, n_steps, step, 0)

    return pl.pallas_call(
        body, name=name, grid=(1,),
        in_specs=[pl.BlockSpec((s, width), lambda i, cb=cb: (0, cb)), pl.BlockSpec((1, width), lambda i: (0, 0))],
        out_specs=pl.BlockSpec((s, width), lambda i: (0, 0)),
        out_shape=jax.ShapeDtypeStruct((s, width), out_dtype),
        scratch_shapes=[pltpu.VMEM((s + 2 * POOL_PAD, width), F32)],
        compiler_params=pltpu.CompilerParams(dimension_semantics=("arbitrary",), vmem_limit_bytes=VMEM_LIMIT),
    )(arr, wvec)


CONV_PAD = 32
CONV_ROWS = 128


def _glu(a, g):
    return a * jax.nn.sigmoid(g)


def conv_fwd(name, zc, w):
    arr, width, cb = zc
    s = arr.shape[0]
    n_steps = s // CONV_ROWS
    lead = CONV_PAD - (CONV_WIDTH - 1)

    def body(z_ref, w_ref, o_ref, hp_ref):
        hp_ref[0:CONV_PAD, :] = jnp.zeros((CONV_PAD, D_CONV), F32)

        def fill(i, carry):
            t0 = pl.multiple_of(i * CONV_ROWS, CONV_ROWS)
            z = z_ref[pl.ds(t0, CONV_ROWS), :]
            hp_ref[pl.ds(t0 + CONV_PAD, CONV_ROWS), :] = _glu(z[:, :D_CONV], z[:, D_CONV:])
            return carry

        lax.fori_loop(0, n_steps, fill, 0)
        wv = w_ref[...]

        def step(i, carry):
            t0 = pl.multiple_of(i * CONV_ROWS, CONV_ROWS)
            win = hp_ref[pl.ds(t0, CONV_ROWS + CONV_PAD), :]
            acc = jnp.zeros((CONV_ROWS, D_CONV), F32)
            for k in range(CONV_WIDTH):
                acc = acc + wv[k:k + 1, :] * win[lead + k:lead + k + CONV_ROWS, :]
            o_ref[pl.ds(t0, CONV_ROWS), :] = acc
            return carry

        lax.fori_loop(0, n_steps, step, 0)

    return pl.pallas_call(
        body, name=name, grid=(1,),
        in_specs=[pl.BlockSpec((s, width), lambda i, cb=cb: (0, cb)), pl.BlockSpec(w.shape, lambda i: (0, 0))],
        out_specs=pl.BlockSpec((s, D_CONV), lambda i: (0, 0)),
        out_shape=jax.ShapeDtypeStruct((s, D_CONV), F32),
        scratch_shapes=[pltpu.VMEM((s + CONV_PAD, D_CONV), F32)],
        compiler_params=pltpu.CompilerParams(dimension_semantics=("arbitrary",), vmem_limit_bytes=VMEM_LIMIT),
    )(arr, w)


def conv_bwd(name, zc, w, dout):
    arr, width, cb = zc
    s = arr.shape[0]
    n_steps = s // CONV_ROWS
    lead = CONV_PAD - (CONV_WIDTH - 1)

    def body(z_ref, w_ref, d_ref, dz_ref, dw_ref, hp_ref, dp_ref):
        hp_ref[0:CONV_PAD, :] = jnp.zeros((CONV_PAD, D_CONV), F32)
        dp_ref[s:s + CONV_PAD, :] = jnp.zeros((CONV_PAD, D_CONV), F32)
        dw_ref[...] = jnp.zeros(dw_ref.shape, F32)

        def fill(i, carry):
            t0 = pl.multiple_of(i * CONV_ROWS, CONV_ROWS)
            z = z_ref[pl.ds(t0, CONV_ROWS), :]
            hp_ref[pl.ds(t0 + CONV_PAD, CONV_ROWS), :] = _glu(z[:, :D_CONV], z[:, D_CONV:])
            dp_ref[pl.ds(t0, CONV_ROWS), :] = d_ref[pl.ds(t0, CONV_ROWS), :]
            return carry

        lax.fori_loop(0, n_steps, fill, 0)
        wv = w_ref[...]

        def step(i, carry):
            t0 = pl.multiple_of(i * CONV_ROWS, CONV_ROWS)
            hwin = hp_ref[pl.ds(t0, CONV_ROWS + CONV_PAD), :]
            dwin = dp_ref[pl.ds(t0, CONV_ROWS + CONV_PAD), :]
            dcur = dwin[0:CONV_ROWS, :]
            dh = jnp.zeros((CONV_ROWS, D_CONV), F32)
            rows = []
            for k in range(CONV_WIDTH):
                rows.append(jnp.sum(dcur * hwin[lead + k:lead + k + CONV_ROWS, :], axis=0, keepdims=True))
                back = CONV_WIDTH - 1 - k
                dh = dh + wv[k:k + 1, :] * dwin[back:back + CONV_ROWS, :]
            rows.append(jnp.zeros((1, D_CONV), F32))
            dw_ref[...] += jnp.concatenate(rows, axis=0)
            z = z_ref[pl.ds(t0, CONV_ROWS), :]
            _, vjp = jax.vjp(_glu, z[:, :D_CONV], z[:, D_CONV:])
            da, dg = vjp(dh)
            dz_ref[pl.ds(t0, CONV_ROWS), :] = jnp.concatenate([da, dg], axis=1).astype(dz_ref.dtype)
            return carry

        lax.fori_loop(0, n_steps, step, 0)

    return pl.pallas_call(
        body, name=name, grid=(1,),
        in_specs=[pl.BlockSpec((s, width), lambda i, cb=cb: (0, cb)), pl.BlockSpec(w.shape, lambda i: (0, 0)),
                  pl.BlockSpec((s, D_CONV), lambda i: (0, 0))],
        out_specs=[pl.BlockSpec((s, width), lambda i: (0, 0)), pl.BlockSpec(w.shape, lambda i: (0, 0))],
        out_shape=[jax.ShapeDtypeStruct((s, width), BF16), jax.ShapeDtypeStruct(w.shape, F32)],
        scratch_shapes=[pltpu.VMEM((s + CONV_PAD, D_CONV), F32), pltpu.VMEM((s + CONV_PAD, D_CONV), F32)],
        compiler_params=pltpu.CompilerParams(dimension_semantics=("arbitrary",), vmem_limit_bytes=VMEM_LIMIT),
    )(arr, w, dout)


HEADS_PER_STEP = LANE // HEAD_DIM
CHUNKS_PER_TILE = ATT_TILE // CHUNK


def _attn_tile(q, kp, kc, vp, vc, bias, first):
    kcat = jnp.concatenate([kp, kc], axis=0)
    vcat = jnp.concatenate([vp, vc], axis=0)
    lane = lax.broadcasted_iota(jnp.int32, (1, LANE), 1)
    col = lax.broadcasted_iota(jnp.int32, (1, 2 * ATT_TILE), 1)
    missing = jnp.logical_and(first, col < ATT_TILE)
    qs = q * (HEAD_DIM ** -0.5)
    o = jnp.zeros((ATT_TILE, LANE), F32)
    for h in range(HEADS_PER_STEP):
        in_head = jnp.logical_and(lane >= h * HEAD_DIM, lane < (h + 1) * HEAD_DIM)
        sc = mm_nt(jnp.where(in_head, qs, 0.0), kcat) + bias[h]
        sc = jnp.where(missing, NEG_INF, sc)
        m = jnp.max(sc, axis=-1, keepdims=True)
        e = jnp.exp(sc - lax.stop_gradient(m))
        p = e / jnp.sum(e, axis=-1, keepdims=True)
        o = o + jnp.where(in_head, mm_nn(p, vcat), 0.0)
    return o


def _z_spec(col0, fn):
    return pl.BlockSpec((ATT_TILE, LANE), lambda hp, n, col0=col0, fn=fn: (fn(n), col0 // LANE + hp))


def attn_fwd(name, z, bias):
    s = z.shape[0]
    nt = s // ATT_TILE

    def body(q_ref, kp_ref, kc_ref, vp_ref, vc_ref, b_ref, o_ref):
        first = pl.program_id(1) == 0
        o = _attn_tile(q_ref[...], kp_ref[...], kc_ref[...], vp_ref[...], vc_ref[...], b_ref[...], first)
        o_ref[...] = o.astype(o_ref.dtype)

    cur = lambda n: n
    prev = lambda n: jnp.maximum(n - 1, 0)
    return pl.pallas_call(
        body, name=name, grid=(N_HEADS // HEADS_PER_STEP, nt),
        in_specs=[_z_spec(Z_Q, cur), _z_spec(Z_K, prev), _z_spec(Z_K, cur), _z_spec(Z_V, prev), _z_spec(Z_V, cur),
                  pl.BlockSpec((HEADS_PER_STEP, ATT_TILE, 2 * ATT_TILE), lambda hp, n: (hp, 0, 0))],
        out_specs=pl.BlockSpec((ATT_TILE, LANE), lambda hp, n: (n, hp)),
        out_shape=jax.ShapeDtypeStruct((s, D_ATTN), BF16),
        compiler_params=pltpu.CompilerParams(dimension_semantics=("parallel", "parallel"), vmem_limit_bytes=VMEM_LIMIT),
    )(z, z, z, z, z, bias)


def attn_bwd(name, z, bias, do):
    s = z.shape[0]
    nt = s // ATT_TILE

    def body(q_ref, kp_ref, kc_ref, vp_ref, vc_ref, b_ref, do_ref, dq_ref, dk_ref, dv_ref, db_ref, kacc, vacc):
        n = pl.program_id(1)

        @pl.when(n == 0)
        def _():
            db_ref[...] = jnp.zeros(db_ref.shape, F32)

        @pl.when(n < nt)
        def _():
            first = n == 0
            fn = functools.partial(_attn_tile, first=first)
            _, vjp = jax.vjp(fn, q_ref[...], kp_ref[...], kc_ref[...], vp_ref[...], vc_ref[...], b_ref[...])
            dq, dkp, dkc, dvp, dvc, db = vjp(do_ref[...].astype(F32))
            dq_ref[...] = dq.astype(dq_ref.dtype)
            db_ref[...] += db

            @pl.when(n > 0)
            def _():
                dk_ref[...] = (kacc[...] + dkp).astype(dk_ref.dtype)
                dv_ref[...] = (vacc[...] + dvp).astype(dv_ref.dtype)

            kacc[...] = dkc
            vacc[...] = dvc

        @pl.when(n == nt)
        def _():
            dk_ref[...] = kacc[...].astype(dk_ref.dtype)
            dv_ref[...] = vacc[...].astype(dv_ref.dtype)

    cur = lambda n: jnp.minimum(n, nt - 1)
    prev = lambda n: jnp.clip(n - 1, 0, nt - 1)
    o_cur = pl.BlockSpec((ATT_TILE, LANE), lambda hp, n: (jnp.minimum(n, nt - 1), hp))
    o_prev = pl.BlockSpec((ATT_TILE, LANE), lambda hp, n: (jnp.maximum(n - 1, 0), hp))
    return pl.pallas_call(
        body, name=name, grid=(N_HEADS // HEADS_PER_STEP, nt + 1),
        in_specs=[_z_spec(Z_Q, cur), _z_spec(Z_K, prev), _z_spec(Z_K, cur), _z_spec(Z_V, prev), _z_spec(Z_V, cur),
                  pl.BlockSpec((HEADS_PER_STEP, ATT_TILE, 2 * ATT_TILE), lambda hp, n: (hp, 0, 0)), o_cur],
        out_specs=[o_cur, o_prev, o_prev, pl.BlockSpec((HEADS_PER_STEP, ATT_TILE, 2 * ATT_TILE), lambda hp, n: (hp, 0, 0))],
        out_shape=[jax.ShapeDtypeStruct((s, D_ATTN), BF16)] * 3 + [jax.ShapeDtypeStruct((N_HEADS, ATT_TILE, 2 * ATT_TILE), F32)],
        scratch_shapes=[pltpu.VMEM((ATT_TILE, LANE), F32), pltpu.VMEM((ATT_TILE, LANE), F32)],
        compiler_params=pltpu.CompilerParams(dimension_semantics=("parallel", "arbitrary"), vmem_limit_bytes=VMEM_LIMIT),
    )(z, z, z, z, z, bias, do)


def loss_head(name, y, tgt, tm):
    s, d = y.shape

    def body(y_ref, t_ref, l_ref, dy_ref):
        i = pl.program_id(0)
        diff = y_ref[...] - t_ref[...]
        dy_ref[...] = diff * (1.0 / d)
        part = 0.5 * jnp.sum(jnp.mean(diff * diff, axis=-1, keepdims=True), axis=0, keepdims=True)

        @pl.when(i == 0)
        def _():
            l_ref[...] = jnp.zeros(l_ref.shape, F32)

        l_ref[...] += jnp.broadcast_to(part, l_ref.shape)

    row = pl.BlockSpec((tm, d), lambda i: (i, 0))
    return pl.pallas_call(
        body, name=name, grid=(s // tm,), in_specs=[row, row],
        out_specs=[pl.BlockSpec((8, LANE), lambda i: (0, 0)), row],
        out_shape=[jax.ShapeDtypeStruct((8, LANE), F32), jax.ShapeDtypeStruct((s, d), F32)],
        compiler_params=pltpu.CompilerParams(dimension_semantics=("arbitrary",), vmem_limit_bytes=VMEM_LIMIT),
    )(y, tgt)


def adamw(name, w, g, m, v, tr):
    r, c = w.shape
    assert r % tr == 0, (name, w.shape, tr)

    def body(w_ref, g_ref, m_ref, v_ref, d_ref, nm_ref, nv_ref):
        gg = g_ref[...]
        m2 = ADAM_B1 * m_ref[...] + (1.0 - ADAM_B1) * gg
        v2 = ADAM_B2 * v_ref[...] + (1.0 - ADAM_B2) * (gg * gg)
        m_hat = m2 / (1.0 - ADAM_B1 ** ADAM_STEP)
        v_hat = v2 / (1.0 - ADAM_B2 ** ADAM_STEP)
        d_ref[...] = -ADAM_LR * (m_hat / (jnp.sqrt(v_hat) + ADAM_EPS) + ADAM_WD * w_ref[...])
        nm_ref[...] = m2
        nv_ref[...] = v2

    blk = pl.BlockSpec((tr, c), lambda i: (i, 0))
    return pl.pallas_call(
        body, name=name, grid=(r // tr,), in_specs=[blk] * 4, out_specs=[blk] * 3,
        out_shape=[jax.ShapeDtypeStruct((r, c), F32)] * 3,
        compiler_params=pltpu.CompilerParams(dimension_semantics=("parallel",), vmem_limit_bytes=VMEM_LIMIT),
    )(w, g, m, v)


def adamw_sum(name, w, layer_blocks, m, v, tr):
    rows, c = w.shape
    nl = len(layer_blocks)
    nb, r, _ = layer_blocks[0].shape
    assert rows == nl * r and r % tr == 0, (name, w.shape, layer_blocks[0].shape, tr)
    per = r // tr

    def body(*refs):
        w_ref, b_refs, (m_ref, v_ref, g_ref, d_ref, nm_ref, nv_ref) = refs[0], refs[1:1 + nl], refs[1 + nl:]
        i = pl.program_id(0)

        def update(b_ref):
            gg = b_ref[0].astype(F32)
            for j in range(1, nb):
                gg = gg + b_ref[j].astype(F32)
            g_ref[...] = gg
            m2 = ADAM_B1 * m_ref[...] + (1.0 - ADAM_B1) * gg
            v2 = ADAM_B2 * v_ref[...] + (1.0 - ADAM_B2) * (gg * gg)
            m_hat = m2 / (1.0 - ADAM_B1 ** ADAM_STEP)
            v_hat = v2 / (1.0 - ADAM_B2 ** ADAM_STEP)
            d_ref[...] = -ADAM_LR * (m_hat / (jnp.sqrt(v_hat) + ADAM_EPS) + ADAM_WD * w_ref[...])
            nm_ref[...] = m2
            nv_ref[...] = v2

        for l in range(nl):
            pl.when(jnp.logical_and(i >= l * per, i < (l + 1) * per))(functools.partial(update, b_refs[l]))

    blk = pl.BlockSpec((tr, c), lambda i: (i, 0))
    b_specs = [pl.BlockSpec((nb, tr, c), lambda i, l=l: (0, jnp.clip(i - l * per, 0, per - 1), 0)) for l in range(nl)]
    return pl.pallas_call(
        body, name=name, grid=(rows // tr,),
        in_specs=[blk] + b_specs + [blk, blk], out_specs=[blk] * 4,
        out_shape=[jax.ShapeDtypeStruct((rows, c), F32)] * 4,
        compiler_params=pltpu.CompilerParams(dimension_semantics=("arbitrary",), vmem_limit_bytes=VMEM_LIMIT),
    )(w, *layer_blocks, m, v)


def sum_blocks(name, blocks, tr):
    nb, r, c = blocks.shape
    assert r % tr == 0, (name, blocks.shape, tr)

    def body(b_ref, o_ref):
        acc = b_ref[0].astype(F32)
        for j in range(1, nb):
            acc = acc + b_ref[j].astype(F32)
        o_ref[...] = acc

    return pl.pallas_call(
        body, name=name, grid=(r // tr,),
        in_specs=[pl.BlockSpec((nb, tr, c), lambda i: (0, i, 0))],
        out_specs=pl.BlockSpec((tr, c), lambda i: (i, 0)),
        out_shape=jax.ShapeDtypeStruct((r, c), F32),
        compiler_params=pltpu.CompilerParams(dimension_semantics=("parallel",), vmem_limit_bytes=VMEM_LIMIT),
    )(blocks)


FLIPS = [(0, 0, 1), (1, 0, 0), (0, 1, 0), (1, 1, 0), (1, 0, 1), (0, 1, 1), (1, 1, 1)]
ANY = pl.BlockSpec(memory_space=pl.ANY)


def _me():
    return lax.axis_index("x"), lax.axis_index("y"), lax.axis_index("c")


def _flip(pos, f):
    return tuple((1 - p) if fi else p for p, fi in zip(pos, f))


def _idx(pos):
    return 4 * pos[0] + 2 * pos[1] + pos[2]


def all_gather_multi(name, shards):
    n = len(shards)

    def body(*refs):
        x_refs, out_refs = refs[:n], refs[n:2 * n]
        send_sems, recv_sems, local_sems = refs[2 * n:]
        x, y, cc = _me()
        me, sibling = (x, y, cc), (x, y, 1 - cc)
        chips = [(1 - x, y), (x, 1 - y), (1 - x, 1 - y)]

        def copy(a, k, block, to, src=None):
            dst = out_refs[a].at[_idx(block)]
            return pltpu.make_async_remote_copy(
                src_ref=dst if src is None else src, dst_ref=dst, send_sem=send_sems.at[7 * a + k],
                recv_sem=recv_sems.at[7 * a + k], device_id=to, device_id_type=MESH)

        mine = [pltpu.make_async_copy(x_refs[a], out_refs[a].at[_idx(me)], local_sems.at[a]) for a in range(n)]
        for cp in mine:
            cp.start()
        first = []
        for a in range(n):
            first.append(copy(a, 0, me, sibling, src=x_refs[a]))
            first += [copy(a, 1 + j, me, (*chip, cc), src=x_refs[a]) for j, chip in enumerate(chips)]
        for cp in first:
            cp.start()
        passed = []
        for j, chip in enumerate(chips):
            for a in range(n):
                copy(a, 1 + j, (*chip, cc), me).wait_recv()
                fwd = copy(a, 4 + j, (*chip, cc), sibling)
                fwd.start()
                passed.append(fwd)
        for a in range(n):
            copy(a, 0, sibling, me).wait_recv()
            for j, chip in enumerate(chips):
                copy(a, 4 + j, (*chip, 1 - cc), me).wait_recv()
        for cp in first + passed:
            cp.wait_send()
        for cp in mine:
            cp.wait()

    return pl.pallas_call(
        body, name=name, in_specs=[ANY] * n, out_specs=[ANY] * n,
        out_shape=[jax.ShapeDtypeStruct((N_DEV,) + a.shape, a.dtype) for a in shards],
        scratch_shapes=[pltpu.SemaphoreType.DMA((7 * n,)), pltpu.SemaphoreType.DMA((7 * n,)),
                        pltpu.SemaphoreType.DMA((n,))],
    )(*shards)


def all_to_all_multi(name, blocks):
    n = len(blocks)

    def body(*refs):
        in_refs, out_refs = refs[:n], refs[n:2 * n]
        send_sems, recv_sems, local_sems = refs[2 * n:]
        me = _me()
        mi = _idx(me)
        mine = [pltpu.make_async_copy(in_refs[a].at[mi], out_refs[a].at[mi], local_sems.at[a]) for a in range(n)]
        for cp in mine:
            cp.start()
        sends, recvs = [], []
        for k, f in enumerate(FLIPS):
            peer = _flip(me, f)
            pi = _idx(peer)
            for a in range(n):
                sems = dict(send_sem=send_sems.at[7 * a + k], recv_sem=recv_sems.at[7 * a + k],
                            device_id=peer, device_id_type=MESH)
                sends.append(pltpu.make_async_remote_copy(src_ref=in_refs[a].at[pi], dst_ref=out_refs[a].at[mi], **sems))
                recvs.append(pltpu.make_async_remote_copy(src_ref=in_refs[a].at[mi], dst_ref=out_refs[a].at[pi], **sems))
        for cp in sends:
            cp.start()
        for cp in recvs:
            cp.wait_recv()
        for cp in sends:
            cp.wait_send()
        for cp in mine:
            cp.wait()

    return pl.pallas_call(
        body, name=name, in_specs=[ANY] * n, out_specs=[ANY] * n,
        out_shape=[jax.ShapeDtypeStruct(a.shape, a.dtype) for a in blocks],
        scratch_shapes=[pltpu.SemaphoreType.DMA((7 * n,)), pltpu.SemaphoreType.DMA((7 * n,)),
                        pltpu.SemaphoreType.DMA((n,))],
    )(*blocks)


HBM = pl.BlockSpec(memory_space=pltpu.HBM)
SEM = pl.BlockSpec(memory_space=pltpu.SEMAPHORE)
DATAFLOW = pltpu.SideEffectType.DATAFLOW_SIDE_EFFECTING


def _exchange_copies(a_refs, l_refs, send_sems, recv_sems, gather):
    me = _me()
    mi = _idx(me)
    out = []
    for k, f in enumerate(FLIPS):
        peer = _flip(me, f)
        for a in range(len(a_refs)):
            src = a_refs[a] if gather else a_refs[a].at[_idx(peer)]
            out.append(pltpu.make_async_remote_copy(
                src_ref=src, dst_ref=l_refs[a].at[mi], send_sem=send_sems.at[7 * a + k],
                recv_sem=recv_sems.at[7 * a + k], device_id=peer, device_id_type=MESH))
    return out


def exchange_start(name, arrays, gather):
    n = len(arrays)
    lands = [lax.empty(((N_DEV,) + a.shape) if gather else a.shape, a.dtype) for a in arrays]

    def body(*refs):
        a_refs, l_refs = refs[:n], refs[n:2 * n]
        send_sems, recv_sems = refs[2 * n], refs[2 * n + 1]
        token = refs[4 * n + 2]
        for cp in _exchange_copies(a_refs, l_refs, send_sems, recv_sems, gather):
            cp.start()
        token[...] = jnp.zeros_like(token)

    hbm = lambda a: pltpu.HBM(a.shape, a.dtype)
    res = pl.pallas_call(
        body, name=name,
        out_shape=(pltpu.SemaphoreType.DMA((7 * n,)), pltpu.SemaphoreType.DMA((7 * n,)),
                   *[hbm(a) for a in arrays], *[hbm(a) for a in lands], jax.ShapeDtypeStruct((8, LANE), F32)),
        in_specs=[HBM] * (2 * n),
        out_specs=(SEM, SEM, *[HBM] * (2 * n), pl.BlockSpec(memory_space=pltpu.VMEM)),
        input_output_aliases={i: i + 2 for i in range(2 * n)},
        compiler_params=pltpu.CompilerParams(has_side_effects=DATAFLOW),
    )(*[pltpu.with_memory_space_constraint(a, pltpu.HBM) for a in arrays],
      *[pltpu.with_memory_space_constraint(a, pltpu.HBM) for a in lands])
    return res[0], res[1], list(res[2:2 + n]), list(res[2 + n:2 + 2 * n]), res[-1]


def exchange_wait(name, send_sems, recv_sems, arrays, lands, after, gather):
    n = len(arrays)

    def body(*refs):
        a_refs, l_refs = refs[:n], refs[n:2 * n]
        ssem, rsem = refs[2 * n], refs[2 * n + 1]
        for cp in _exchange_copies(a_refs, l_refs, ssem, rsem, gather):
            cp.wait_send()
            cp.wait_recv()

    hbm = lambda a: pltpu.HBM(a.shape, a.dtype)
    res = pl.pallas_call(
        body, name=name,
        out_shape=(*[hbm(a) for a in arrays], *[hbm(a) for a in lands]),
        in_specs=[HBM] * (2 * n) + [SEM, SEM, pl.BlockSpec(memory_space=pl.ANY)],
        out_specs=tuple([HBM] * (2 * n)),
        input_output_aliases={i: i for i in range(2 * n)},
        compiler_params=pltpu.CompilerParams(has_side_effects=DATAFLOW),
    )(*arrays, *lands, send_sems, recv_sems, after)
    return list(res[n:])


def ada_fwd(name, c_row, w_cat, b_lay):
    d = c_row.shape[1]
    ncol = w_cat.shape[1]
    vmem = pl.BlockSpec(memory_space=pltpu.VMEM)

    def body(c_ref, w_ref, b_ref, mod_ref, cact_ref, call, send, land, s1, r1, s2, r2):
        me = _me()
        mi = _idx(me)
        call[mi] = c_ref[...]

        def exchange(src_of, dst_buf, ssem, rsem):
            sends, recvs = [], []
            for k, f in enumerate(FLIPS):
                peer = _flip(me, f)
                sends.append(pltpu.make_async_remote_copy(
                    src_ref=src_of(peer), dst_ref=dst_buf.at[mi], send_sem=ssem.at[k], recv_sem=rsem.at[k],
                    device_id=peer, device_id_type=MESH))
                recvs.append(pltpu.make_async_remote_copy(
                    src_ref=src_of(peer), dst_ref=dst_buf.at[_idx(peer)], send_sem=ssem.at[k], recv_sem=rsem.at[k],
                    device_id=peer, device_id_type=MESH))
            for cp in sends:
                cp.start()
            for cp in recvs:
                cp.wait_recv()
            for cp in sends:
                cp.wait_send()

        exchange(lambda peer: c_ref, call, s1, r1)
        for p in range(N_DEV):
            cact_ref[pl.ds(p, 1), :] = jax.nn.silu(call[p])
        res = _dg(cact_ref[...], w_ref[...], 1, 0)
        for p in range(N_DEV):
            send[p] = res[p:p + 1, :]
        land[mi] = send[mi]
        exchange(lambda peer: send.at[_idx(peer)], land, s2, r2)
        mod_ref[...] = land[...] + b_ref[...]

    return pl.pallas_call(
        body, name=name, in_specs=[vmem, vmem, vmem], out_specs=[vmem, vmem],
        out_shape=[jax.ShapeDtypeStruct((N_DEV, 1, ncol), F32), jax.ShapeDtypeStruct((N_DEV, d), F32)],
        scratch_shapes=[pltpu.VMEM((N_DEV, 1, d), F32), pltpu.VMEM((N_DEV, 1, ncol), F32),
                        pltpu.VMEM((N_DEV, 1, ncol), F32),
                        pltpu.SemaphoreType.DMA((7,)), pltpu.SemaphoreType.DMA((7,)),
                        pltpu.SemaphoreType.DMA((7,)), pltpu.SemaphoreType.DMA((7,))],
        compiler_params=pltpu.CompilerParams(vmem_limit_bytes=VMEM_LIMIT),
    )(c_row, w_cat, b_lay)


POOL_WINDOWS = (2, 4, 8, 16)
POOL_GROUP = 64
N_REL = 2 * REL_CLIP + 1
PACK_COLS = 1024
SMALL_NAMES = ["b_ada", "b_gate", "w_pool", "pool_scale", "rel_bias", "conv_w", "conv_b", "conv_ln_g",
               "conv_ln_b", "ln_mix_g", "ln_mix_b", "b_ff1", "b_ff2", "ln_ff_g", "ln_ff_b"]
BIG_NAMES = ["w_in", "w_br_pool", "w_br_attn", "w_br_conv", "w_o", "w_ff1", "w_ff2"]
ROW_SHARDED = ("w_o", "w_ff2")
WEIGHT_NAMES = ["w_ada", "b_ada", "w_in", "b_gate", "w_pool", "pool_scale", "rel_bias", "conv_w", "conv_b",
                "conv_ln_g", "conv_ln_b", "w_br_pool", "w_br_attn", "w_br_conv", "w_o", "ln_mix_g", "ln_mix_b",
                "w_ff1", "b_ff1", "w_ff2", "b_ff2", "ln_ff_g", "ln_ff_b"]


def _perm_cols(w):
    return jnp.concatenate([w[:, 2304:], w[:, 1792:2304], w[:, :256], w[:, 256:768], w[:, 768:1280],
                            w[:, 1280:1792]], axis=1)


def _unperm_cols(wp):
    return jnp.concatenate([wp[:, Z_POOL:Z_Q], wp[:, Z_Q:Z_K], wp[:, Z_K:Z_V], wp[:, Z_V:],
                            wp[:, Z_CONV:Z_POOL], wp[:, :Z_CONV]], axis=1)


def _bias_table(rel_bias):
    far = jnp.broadcast_to(rel_bias[:, 2 * REL_CLIP:], (N_HEADS, BAND - REL_CLIP))
    near = rel_bias[:, REL_CLIP - CHUNK + 1:2 * REL_CLIP][:, ::-1]
    ext = jnp.concatenate([far, near], axis=1)
    return jnp.stack([ext[:, CHUNK - 1 - qi:CHUNK - 1 - qi + BAND] for qi in range(CHUNK)], axis=1)


def _bias_full(rel_bias):
    tab = _bias_table(rel_bias)
    return jnp.concatenate(
        [jnp.pad(tab, ((0, 0), (0, 0), (i * CHUNK, 2 * ATT_TILE - BAND - i * CHUNK)), constant_values=NEG_INF)
         for i in range(CHUNKS_PER_TILE)], axis=1)


def _block_diag(w_pool):
    out = jnp.zeros((D_POOL, D_POOL), F32)
    for g in range(len(POOL_WINDOWS)):
        out = lax.dynamic_update_slice(out, w_pool[g], (g * POOL_GROUP, g * POOL_GROUP))
    return out


def _flat_pad(arrs, mult):
    flat = jnp.concatenate([a.reshape(-1) for a in arrs])
    pad = (-flat.shape[0]) % mult
    return jnp.pad(flat, (0, pad)) if pad else flat


def _unflat(flat, shapes):
    out, off = [], 0
    for shp in shapes:
        n = int(np.prod(shp))
        out.append(flat[off:off + n].reshape(shp))
        off += n
    return out


def _to_blocks(name, full):
    k, n = full.shape
    if name in ROW_SHARDED:
        return full.reshape(N_DEV, k // N_DEV, n)
    return full.reshape(k, N_DEV, n // N_DEV).transpose(1, 0, 2)


def _from_blocks(name, blocks):
    nb, r, c = blocks.shape
    if name in ROW_SHARDED:
        return blocks.reshape(nb * r, c)
    return blocks.transpose(1, 0, 2).reshape(r, nb * c)


class _Layer:
    pass


def _row(v):
    return v.reshape(1, -1)


def _layer_fwd(x, modr, w, wvec):
    sh_m, sc_m, g_m, sh_f, sc_f, g_f = modr
    (u,) = row_fwd("lnmod_mix", f_lnmod, [x], [sc_m, sh_m], [(D_MODEL, BF16)], 512)
    z = mm_big("mm_in", u, w.w_in, "nn", (512, 896, 1024), F32, j_outer=True)
    p = pool_lin("pool_fwd", (z, D_POOL, Z_POOL // D_POOL), wvec, False, F32)
    ao = attn_fwd("attn_fwd", z, w.bias)
    cv = conv_fwd("conv_fwd", (z, 2 * D_CONV, Z_CONV // (2 * D_CONV)), w.conv_w)
    mparams = [w.wbd, w.ps, w.wbp, w.wba, w.wbc, w.cb, w.clg, w.clb, w.bg, w.wo, g_m, w.lmg, w.lmb]
    (x1,) = row_fwd("merge", f_merge, [p, ao, cv, (z, 3 * D_MODEL, 0), x], mparams, [(D_MODEL, F32)], 256)
    (u2,) = row_fwd("lnmod_ff", f_lnmod, [x1], [sc_f, sh_f], [(D_MODEL, BF16)], 512)
    hpre, h = mm_ff1_relu2("mm_ff1", u2, w.w_ff1, w.b1, 512, 1024)
    ff = mm_big("mm_ff2", h, w.w_ff2, "nn", (512, 1024, 4096), F32)
    (x2,) = row_fwd("ffout", f_ffout, [x1, ff], [w.b2, g_f, w.lfg, w.lfb], [(D_MODEL, F32)], 512)
    return x2, (x, u, z, p, ao, cv, x1, u2, hpre, h, ff, mparams)


GRAD_GROUPS = [("ff", ["w_ff2", "w_ff1"]), ("mix", ["w_o", "w_br_pool", "w_br_attn", "w_br_conv"]), ("in", ["w_in"])]


def _layer_bwd(dx2, saved, modr, w, wvec, ready):
    x, u, z, p, ao, cv, x1, u2, hpre, h, ff, mparams = saved
    sh_m, sc_m, g_m, sh_f, sc_f, g_f = modr
    g = {}
    (dx1a, dff), (g["b_ff2"], dgf, g["ln_ff_g"], g["ln_ff_b"]) = row_bwd(
        "ffout_bwd", f_ffout, [x1, ff], [w.b2, g_f, w.lfg, w.lfb], [dx2], 512, [(0, F32), (1, BF16)], [0, 1, 2, 3])
    dhpre, g["b_ff1"] = mm_dh_relu2("mm_dh", dff, w.w_ff2, hpre, w.b1, 512, 1024)
    g["w_ff2"] = mm_big("mm_dw_ff2", h, dff, "tn", (1024, 1024, 2048), BF16)
    du2 = mm_big("mm_du2", dhpre, w.w_ff1, "nt", (512, 1024, 4096), F32)
    g["w_ff1"] = mm_big("mm_dw_ff1", u2, dhpre, "tn", (1024, 1024, 2048), BF16)
    sc_f = sc_f + ready("ff", g)
    (dx1,), (dscf, dshf) = row_bwd("lnmod_ff_bwd", f_lnmod, [x1], [sc_f, sh_f], [du2], 512, [(0, F32)], [0, 1],
                                   add_to=dx1a)
    (dp, dao, dcv, dzg, dxa), dm = row_bwd(
        "merge_bwd", f_merge, [p, ao, cv, (z, 3 * D_MODEL, 0), x], mparams, [dx1], 256,
        [(0, F32), (1, BF16), (2, F32), (3, BF16), (4, F32)], list(range(13)))
    (dwbd, g["pool_scale"], g["w_br_pool"], g["w_br_attn"], g["w_br_conv"], g["conv_b"], g["conv_ln_g"],
     g["conv_ln_b"], g["b_gate"], g["w_o"], dgm, g["ln_mix_g"], g["ln_mix_b"]) = dm
    g["w_pool"] = jnp.stack([dwbd[i * POOL_GROUP:(i + 1) * POOL_GROUP, i * POOL_GROUP:(i + 1) * POOL_GROUP]
                             for i in range(len(POOL_WINDOWS))])
    dzp = pool_lin("pool_bwd", (dp, D_POOL, 0), wvec + ready("mix", g), True, BF16)
    dq, dk, dv, dbias = attn_bwd("attn_bwd", z, w.bias, dao)
    (g["rel_bias"],) = w.bias_vjp(dbias)
    dzc, dcw = conv_bwd("conv_bwd", (z, 2 * D_CONV, Z_CONV // (2 * D_CONV)), w.conv_w, dcv)
    g["conv_w"] = dcw[:CONV_WIDTH]
    dz = jnp.concatenate([dzg, dzc, dzp, dq, dk, dv], axis=1)
    du = mm_big("mm_du", dz, w.w_in, "nt", (512, 1024, D_IN), F32)
    g["w_in"] = _unperm_cols(mm_big("mm_dw_in", u, dz, "tn", (1024, 896, 2048), BF16))
    sc_m = sc_m + ready("in", g)
    (dx,), (dscm, dshm) = row_bwd("lnmod_mix_bwd", f_lnmod, [x], [sc_m, sh_m], [du], 512, [(0, F32)], [0, 1],
                                  add_to=dxa)
    dmod = jnp.concatenate([dshm, dscm, dgm, dshf, dscf, dgf], axis=1)
    return dx, dmod, g


def kernel(x, c, w_ada, b_ada, w_in, b_gate, w_pool, pool_scale, rel_bias, conv_w, conv_b, conv_ln_g, conv_ln_b, w_br_pool, w_br_attn, w_br_conv, w_o, ln_mix_g, ln_mix_b, w_ff1, b_ff1, w_ff2, b_ff2, ln_ff_g, ln_ff_b, loss_target, m_w_ada, m_b_ada, m_w_in, m_b_gate, m_w_pool, m_pool_scale, m_rel_bias, m_conv_w, m_conv_b, m_conv_ln_g, m_conv_ln_b, m_w_br_pool, m_w_br_attn, m_w_br_conv, m_w_o, m_ln_mix_g, m_ln_mix_b, m_w_ff1, m_b_ff1, m_w_ff2, m_b_ff2, m_ln_ff_g, m_ln_ff_b, v_w_ada, v_b_ada, v_w_in, v_b_gate, v_w_pool, v_pool_scale, v_rel_bias, v_conv_w, v_conv_b, v_conv_ln_g, v_conv_ln_b, v_w_br_pool, v_w_br_attn, v_w_br_conv, v_w_o, v_ln_mix_g, v_ln_mix_b, v_w_ff1, v_b_ff1, v_w_ff2, v_b_ff2, v_ln_ff_g, v_ln_ff_b):
    args = dict(locals())
    wts = {n: args[n] for n in WEIGHT_NAMES}
    mom = {n: args["m_" + n] for n in WEIGHT_NAMES}
    var = {n: args["v_" + n] for n in WEIGHT_NAMES}
    me = 4 * lax.axis_index("x") + 2 * lax.axis_index("y") + lax.axis_index("c")
    xs, tgt = x[0], loss_target[0]
    nc_ada = w_ada.shape[2]
    wvec = jnp.asarray(np.repeat(np.array(POOL_WINDOWS, np.float32), POOL_GROUP)[None, :])

    w_cat = jnp.concatenate([w_ada[l] for l in range(DEPTH)], axis=1)
    b_lay = b_ada.reshape(DEPTH, N_DEV, nc_ada).transpose(1, 0, 2).reshape(N_DEV, 1, DEPTH * nc_ada)
    land, cact = ada_fwd("ada_fwd", c, w_cat, b_lay)
    mod = land.reshape(N_DEV, DEPTH, nc_ada).transpose(1, 0, 2).reshape(DEPTH, 6 * D_MODEL)
    modr = [[mod[l:l + 1, i * D_MODEL:(i + 1) * D_MODEL] for i in range(6)] for l in range(DEPTH)]

    cw_pack = _flat_pad([conv_w], 8 * LANE).reshape(-1, LANE)
    shards = [[wts[n][l].astype(BF16) for n in BIG_NAMES] for l in range(DEPTH)]
    gathered0 = all_gather_multi("gather_weights_l0", shards[0] + [cw_pack])
    cw_all = gathered0[-1].reshape(N_DEV, -1)[:, :conv_w.size]
    conv_full = cw_all.reshape((N_DEV,) + conv_w.shape).transpose(1, 2, 0, 3).reshape(DEPTH, CONV_WIDTH, D_CONV)
    ssem, rsem, thru, lands, token = exchange_start("gather_weights_l1_start", shards[1], True)
    modr[0] = [r + token[0:1, 0:1] for r in modr[0]]

    def layer_weights(l, blocks):
        full = {n: _from_blocks(n, g) for n, g in zip(BIG_NAMES, blocks)}
        w = _Layer()
        w.w_in = _perm_cols(full["w_in"])
        w.wbp, w.wba, w.wbc = full["w_br_pool"], full["w_br_attn"], full["w_br_conv"]
        w.wo, w.w_ff1, w.w_ff2 = full["w_o"], full["w_ff1"], full["w_ff2"]
        w.wbd = _block_diag(w_pool[l])
        w.ps, w.cb, w.clg, w.clb = _row(pool_scale[l]), _row(conv_b[l]), _row(conv_ln_g[l]), _row(conv_ln_b[l])
        w.bg, w.lmg, w.lmb = _row(b_gate[l]), _row(ln_mix_g[l]), _row(ln_mix_b[l])
        w.b1, w.b2, w.lfg, w.lfb = _row(b_ff1[l]), _row(b_ff2[l]), _row(ln_ff_g[l]), _row(ln_ff_b[l])
        w.conv_w = jnp.pad(conv_full[l], ((0, CONV_PAD - CONV_WIDTH), (0, 0)))
        w.bias, w.bias_vjp = jax.vjp(_bias_full, rel_bias[l])
        return w

    def with_own(lands_, own):
        return [lax.dynamic_update_index_in_dim(ld, o, me, axis=0) for ld, o in zip(lands_, own)]

    layers, saved = [layer_weights(0, gathered0[:-1])], []
    h, sv = _layer_fwd(xs, modr[0], layers[0], wvec)
    saved.append(sv)
    lands = exchange_wait("gather_weights_l1_wait", ssem, rsem, thru, lands, h, True)
    layers.append(layer_weights(1, with_own(lands, shards[1])))
    h, sv = _layer_fwd(h, modr[1], layers[1], wvec)
    saved.append(sv)
    lpart, dy = loss_head("loss_head", h, tgt, 512)
    loss = lax.psum(lpart[0, 0], ("x", "y", "c"))
    grads, dmods = [None] * DEPTH, [None] * DEPTH
    pending = {}

    def ready_for(l):
        def ready(group, g):
            names = dict(GRAD_GROUPS)[group]
            blocks = [_to_blocks(n, g[n]).astype(BF16) for n in names]
            ssem, rsem, thru, lands, token = exchange_start(f"scatter_l{l}_{group}_start", blocks, False)
            pending[(l, group)] = (names, blocks, ssem, rsem, thru, lands)
            return token[0:1, 0:1]
        return ready

    def received(l, group, after):
        names, blocks, ssem, rsem, thru, lands = pending[(l, group)]
        lands = exchange_wait(f"scatter_l{l}_{group}_wait", ssem, rsem, thru, lands, after, False)
        own = [lax.dynamic_index_in_dim(b, me, axis=0, keepdims=False) for b in blocks]
        return dict(zip(names, with_own(lands, own)))

    for l in reversed(range(DEPTH)):
        dy, dmods[l], grads[l] = _layer_bwd(dy, saved[l], modr[l], layers[l], wvec, ready_for(l))
    recv = [{} for _ in range(DEPTH)]
    for l, group in [(1, "ff"), (1, "mix"), (1, "in"), (0, "ff"), (0, "mix")]:
        recv[l].update(received(l, group, dy))
    grad_x = dy[None]
    dmod = jnp.concatenate(dmods, axis=0)

    small_shapes = [wts[n].shape if n != "conv_w" else (DEPTH, CONV_WIDTH, D_CONV) for n in SMALL_NAMES]
    small_local = [dmod] + [jnp.stack([grads[l][n].reshape(shp[1:]) for l in range(DEPTH)])
                            for n, shp in zip(SMALL_NAMES[1:], small_shapes[1:])]
    small_pack = _flat_pad(small_local, 8 * LANE).reshape(-1, LANE)
    (small_all,) = all_gather_multi("gather_small_grads", [small_pack])
    small_sum = sum_blocks("sum_small_grads", small_all, small_pack.shape[0]).reshape(-1)
    gsmall = dict(zip(SMALL_NAMES, _unflat(small_sum, small_shapes)))
    gw = dict(gsmall)
    gw["conv_w"] = lax.dynamic_slice_in_dim(gsmall["conv_w"], me * conv_w.shape[2], conv_w.shape[2], axis=2)

    dmod_all = small_all.reshape(N_DEV, -1)[:, :dmod.size].reshape(N_DEV, DEPTH, N_DEV, nc_ada)
    dm_mine = lax.dynamic_index_in_dim(dmod_all, me, axis=2, keepdims=False).reshape(N_DEV, DEPTH * nc_ada)
    cact_t = jnp.pad(cact.T, ((0, 0), (0, LANE - N_DEV)))
    dm_pad = jnp.pad(dm_mine, ((0, LANE - N_DEV), (0, 0)))
    dw_cat = mm_big("mm_dw_ada", cact_t, dm_pad, "nn", (D_MODEL, DEPTH * nc_ada, LANE), F32)
    gw["w_ada"] = jnp.stack([dw_cat[:, l * nc_ada:(l + 1) * nc_ada] for l in range(DEPTH)])

    delta, new_m, new_v = {}, {}, {}
    for n in ["w_ada", "w_ff2", "w_ff1", "w_o", "w_br_pool", "w_br_attn", "w_br_conv", "w_in"]:
        shp = wts[n].shape
        two_d = lambda a, shp=shp: a.reshape(shp[0] * shp[1], shp[2])
        tr = min(256, shp[1])
        if n == "w_ada":
            res = (gw[n],) + tuple(adamw("adamw_" + n, two_d(wts[n]), two_d(gw[n]), two_d(mom[n]), two_d(var[n]), tr))
        else:
            if n == "w_in":
                recv[0].update(received(0, "in", new_v["w_br_conv"]))
            res = adamw_sum("adamw_" + n, two_d(wts[n]), [recv[l][n] for l in range(DEPTH)], two_d(mom[n]),
                            two_d(var[n]), tr)
        gw[n], delta[n], new_m[n], new_v[n] = [a.reshape(shp) for a in res]
    packs = [_flat_pad([src[n] for n in SMALL_NAMES], 8 * LANE).reshape(-1, LANE) for src in (wts, gw, mom, var)]
    res = adamw("adamw_small", *packs, packs[0].shape[0])
    shapes = [wts[n].shape for n in SMALL_NAMES]
    for out, flat in zip((delta, new_m, new_v), res):
        out.update(dict(zip(SMALL_NAMES, _unflat(flat.reshape(-1), shapes))))

    return (loss, grad_x, *[gw[n] for n in WEIGHT_NAMES], *[delta[n] for n in WEIGHT_NAMES],
            *[new_m[n] for n in WEIGHT_NAMES], *[new_v[n] for n in WEIGHT_NAMES])
```

```python
import functools

import jax
import jax.numpy as jnp
import numpy as np
from jax import lax
from jax.experimental import pallas as pl
from jax.experimental.pallas import tpu as pltpu

F32 = jnp.float32
BF16 = jnp.bfloat16
MESH = pl.DeviceIdType.MESH

D_MODEL = 1024
DEPTH = 2
CHUNK = 64
N_HEADS = 8
HEAD_DIM = 64
D_POOL = 256
D_ATTN = 512
D_CONV = 256
CONV_WIDTH = 31
D_FF = 4096
D_IN = 5376
N_PREV = 8
BAND = (N_PREV + 1) * CHUNK
REL_CLIP = 128
ALPHA = (2.0 * DEPTH) ** 0.25
LN_EPS = 1e-5
NEG_INF = -1e30
N_DEV = 8

ADAM_LR, ADAM_B1, ADAM_B2, ADAM_EPS, ADAM_WD, ADAM_STEP = 0.001, 0.9, 0.999, 1e-08, 0.01, 10

VMEM_LIMIT = 56 * 1024 * 1024

Z_GATE, Z_CONV, Z_POOL, Z_Q, Z_K, Z_V = 0, 3072, 3584, 3840, 4352, 4864
ATT_TILE = 256
LANE = 128


def _dg(a, b, ca, cb):
    return lax.dot_general(a.astype(BF16), b.astype(BF16), (((ca,), (cb,)), ((), ())),
                           preferred_element_type=F32)


@jax.custom_vjp
def mm_nn(a, b):
    return _dg(a, b, 1, 0)


def _mm_nn_fwd(a, b):
    return _dg(a, b, 1, 0), (a, b)


def _mm_nn_bwd(res, g):
    a, b = res
    return _dg(g, b, 1, 1).astype(a.dtype), _dg(a, g, 0, 0).astype(b.dtype)


mm_nn.defvjp(_mm_nn_fwd, _mm_nn_bwd)


@jax.custom_vjp
def mm_nt(a, b):
    return _dg(a, b, 1, 1)


def _mm_nt_fwd(a, b):
    return _dg(a, b, 1, 1), (a, b)


def _mm_nt_bwd(res, g):
    a, b = res
    return _dg(g, b, 1, 0).astype(a.dtype), _dg(g, a, 0, 0).astype(b.dtype)


mm_nt.defvjp(_mm_nt_fwd, _mm_nt_bwd)


def _ln(x):
    mu = jnp.mean(x, axis=-1, keepdims=True)
    xc = x - mu
    var = jnp.mean(xc * xc, axis=-1, keepdims=True)
    return xc * lax.rsqrt(var + LN_EPS)


def _norm_rows(rows):
    return [r if isinstance(r, tuple) else (r, r.shape[1], 0) for r in rows]


def _row_spec(tm, r):
    _, width, cb = r
    return pl.BlockSpec((tm, width), lambda i, cb=cb: (i, cb))


def _full_spec(a):
    nd = a.ndim
    return pl.BlockSpec(a.shape, lambda i, nd=nd: (0,) * nd)


def row_fwd(name, f, rows, params, outs, tm):
    rows = _norm_rows(rows)
    s = rows[0][0].shape[0]
    nr, npar = len(rows), len(params)

    def body(*refs):
        r = [x[...].astype(F32) for x in refs[:nr]]
        p = [x[...] for x in refs[nr:nr + npar]]
        res = f(*r, *p)
        for o_ref, o in zip(refs[nr + npar:], res):
            o_ref[...] = o.astype(o_ref.dtype)

    return pl.pallas_call(
        body, name=name, grid=(s // tm,),
        in_specs=[_row_spec(tm, r) for r in rows] + [_full_spec(p) for p in params],
        out_specs=[pl.BlockSpec((tm, w), lambda i: (i, 0)) for w, _ in outs],
        out_shape=[jax.ShapeDtypeStruct((s, w), dt) for w, dt in outs],
        compiler_params=pltpu.CompilerParams(dimension_semantics=("parallel",), vmem_limit_bytes=VMEM_LIMIT),
    )(*[r[0] for r in rows], *params)


def row_bwd(name, f, rows, params, douts, tm, want_rows, want_params, add_to=None):
    rows = _norm_rows(rows)
    s = rows[0][0].shape[0]
    nr, npar, nd = len(rows), len(params), len(douts)
    nadd = 0 if add_to is None else 1
    n_in = nr + npar + nd + nadd

    def body(*refs):
        i = pl.program_id(0)
        r = [x[...].astype(F32) for x in refs[:nr]]
        p = [x[...].astype(F32) for x in refs[nr:nr + npar]]
        d = [x[...].astype(F32) for x in refs[nr + npar:nr + npar + nd]]
        _, vjp = jax.vjp(f, *r, *p)
        g = vjp(tuple(d))
        out_refs = refs[n_in:]
        for k, (idx, _) in enumerate(want_rows):
            val = g[idx]
            if nadd and k == 0:
                val = val + refs[n_in - 1][...].astype(F32)
            out_refs[k][...] = val.astype(out_refs[k].dtype)
        for k, idx in enumerate(want_params):
            gp = g[nr + idx]
            o_ref = out_refs[len(want_rows) + k]

            @pl.when(i == 0)
            def _():
                o_ref[...] = gp

            @pl.when(i > 0)
            def _():
                o_ref[...] += gp

    in_specs = ([_row_spec(tm, r) for r in rows] + [_full_spec(p) for p in params]
                + [pl.BlockSpec((tm, d.shape[1]), lambda i: (i, 0)) for d in douts])
    args = [r[0] for r in rows] + list(params) + list(douts)
    if nadd:
        in_specs.append(pl.BlockSpec((tm, add_to.shape[1]), lambda i: (i, 0)))
        args.append(add_to)
    out_specs = ([pl.BlockSpec((tm, rows[idx][1]), lambda i: (i, 0)) for idx, _ in want_rows]
                 + [_full_spec(params[idx]) for idx in want_params])
    out_shape = ([jax.ShapeDtypeStruct((s, rows[idx][1]), dt) for idx, dt in want_rows]
                 + [jax.ShapeDtypeStruct(params[idx].shape, F32) for idx in want_params])
    res = pl.pallas_call(
        body, name=name, grid=(s // tm,), in_specs=in_specs, out_specs=out_specs, out_shape=out_shape,
        compiler_params=pltpu.CompilerParams(dimension_semantics=("arbitrary",), vmem_limit_bytes=VMEM_LIMIT),
    )(*args)
    return res[:len(want_rows)], res[len(want_rows):]


def f_lnmod(x, sc, sh):
    return (_ln(x) * (1.0 + sc) + sh,)


def f_merge(p, ao, cv, zg, x, wbd, ps, wbp, wba, wbc, cb, clg, clb, bg, wo, gm, lg, lb):
    pm = mm_nn(p, wbd) * ps
    co = jax.nn.silu(_ln(cv + cb) * clg + clb)
    y_pool = mm_nn(pm, wbp)
    y_attn = mm_nn(ao, wba)
    y_conv = mm_nn(co, wbc)
    gates = jax.nn.sigmoid(zg + bg)
    merged = (gates[:, :D_MODEL] * y_pool + gates[:, D_MODEL:2 * D_MODEL] * y_attn
              + gates[:, 2 * D_MODEL:] * y_conv)
    mix = mm_nn(merged, wo)
    return (_ln(ALPHA * x + gm * mix) * lg + lb,)


def f_relu2(hpre, b1):
    a = jax.nn.relu(hpre + b1)
    return (a * a,)


def f_ffout(x1, ff, b2, gf, lg, lb):
    return (_ln(ALPHA * x1 + gf * (ff + b2)) * lg + lb,)


def mm_big(name, a, b, kind, tiles, out_dtype, j_outer=False, after=None):
    if kind == "nn":
        o0, o1, red = a.shape[0], b.shape[1], a.shape[1]
    elif kind == "nt":
        o0, o1, red = a.shape[0], b.shape[0], a.shape[1]
    else:
        o0, o1, red = a.shape[1], b.shape[1], a.shape[0]
    t0, t1, tr = min(tiles[0], o0), min(tiles[1], o1), min(tiles[2], red)
    if kind == "nn":
        a_spec = pl.BlockSpec((t0, tr), lambda i, j, r: (i, r))
        b_spec = pl.BlockSpec((tr, t1), lambda i, j, r: (r, j))
        dims = (1, 0)
    elif kind == "nt":
        a_spec = pl.BlockSpec((t0, tr), lambda i, j, r: (i, r))
        b_spec = pl.BlockSpec((t1, tr), lambda i, j, r: (j, r))
        dims = (1, 1)
    else:
        a_spec = pl.BlockSpec((tr, t0), lambda i, j, r: (r, i))
        b_spec = pl.BlockSpec((tr, t1), lambda i, j, r: (r, j))
        dims = (0, 0)
    assert o0 % t0 == 0 and o1 % t1 == 0 and red % tr == 0, (name, a.shape, b.shape, tiles)
    n0, n1, nred = o0 // t0, o1 // t1, red // tr
    o_spec = pl.BlockSpec((t0, t1), lambda i, j, r: (i, j))
    if j_outer:
        swap = lambda spec: pl.BlockSpec(spec.block_shape, lambda j, i, r, f=spec.index_map: f(i, j, r))
        a_spec, b_spec, o_spec = swap(a_spec), swap(b_spec), swap(o_spec)
        grid = (n1, n0, nred)
    else:
        grid = (n0, n1, nred)

    deps = [] if after is None else [after]
    dep_specs = [pl.BlockSpec(d.shape, lambda i, j, r, nd=d.ndim: (0,) * nd) for d in deps]
    if nred == 1:
        def body(a_ref, b_ref, *rest):
            o_ref = rest[len(deps)]
            o_ref[...] = _dg(a_ref[...], b_ref[...], *dims).astype(o_ref.dtype)
        scratch = []
    else:
        def body(a_ref, b_ref, *rest):
            o_ref, acc_ref = rest[len(deps):]
            r = pl.program_id(2)
            part = _dg(a_ref[...], b_ref[...], *dims)

            @pl.when(r == 0)
            def _():
                acc_ref[...] = part

            @pl.when(jnp.logical_and(r > 0, r < nred - 1))
            def _():
                acc_ref[...] += part

            @pl.when(r == nred - 1)
            def _():
                o_ref[...] = (acc_ref[...] + part).astype(o_ref.dtype)
        scratch = [pltpu.VMEM((t0, t1), F32)]

    return pl.pallas_call(
        body, name=name, grid=grid,
        in_specs=[a_spec, b_spec] + dep_specs,
        out_specs=o_spec,
        out_shape=jax.ShapeDtypeStruct((o0, o1), out_dtype),
        scratch_shapes=scratch,
        compiler_params=pltpu.CompilerParams(dimension_semantics=("parallel", "parallel", "arbitrary"),
                                             vmem_limit_bytes=VMEM_LIMIT),
    )(a, b, *deps)


def mm_ff1_relu2(name, u2, w1, b1, tm, tn):
    m, k = u2.shape
    n = w1.shape[1]
    tm, tn = min(tm, m), min(tn, n)

    def body(a_ref, b_ref, bias_ref, hpre_ref, h_ref):
        acc = _dg(a_ref[...], b_ref[...], 1, 0)
        hpre_ref[...] = acc
        h_ref[...] = f_relu2(acc, bias_ref[...])[0].astype(h_ref.dtype)

    out = pl.BlockSpec((tm, tn), lambda j, i: (i, j))
    return pl.pallas_call(
        body, name=name, grid=(n // tn, m // tm),
        in_specs=[pl.BlockSpec((tm, k), lambda j, i: (i, 0)), pl.BlockSpec((k, tn), lambda j, i: (0, j)),
                  pl.BlockSpec((1, tn), lambda j, i: (0, j))],
        out_specs=[out, out],
        out_shape=[jax.ShapeDtypeStruct((m, n), F32), jax.ShapeDtypeStruct((m, n), BF16)],
        compiler_params=pltpu.CompilerParams(dimension_semantics=("parallel", "parallel"), vmem_limit_bytes=VMEM_LIMIT),
    )(u2, w1, b1)


def mm_dh_relu2(name, dff, w2, hpre, b1, tm, tn):
    m, k = dff.shape
    n = w2.shape[0]
    tm, tn = min(tm, m), min(tn, n)

    def body(a_ref, b_ref, hpre_ref, bias_ref, d_ref, db_ref):
        i = pl.program_id(1)
        dh = _dg(a_ref[...], b_ref[...], 1, 1)
        _, vjp = jax.vjp(f_relu2, hpre_ref[...], bias_ref[...])
        dhpre, db = vjp((dh,))
        d_ref[...] = dhpre.astype(d_ref.dtype)

        @pl.when(i == 0)
        def _():
            db_ref[...] = db

        @pl.when(i > 0)
        def _():
            db_ref[...] += db

    tile = pl.BlockSpec((tm, tn), lambda j, i: (i, j))
    col = pl.BlockSpec((1, tn), lambda j, i: (0, j))
    return pl.pallas_call(
        body, name=name, grid=(n // tn, m // tm),
        in_specs=[pl.BlockSpec((tm, k), lambda j, i: (i, 0)), pl.BlockSpec((tn, k), lambda j, i: (j, 0)), tile, col],
        out_specs=[tile, col],
        out_shape=[jax.ShapeDtypeStruct((m, n), BF16), jax.ShapeDtypeStruct((1, n), F32)],
        compiler_params=pltpu.CompilerParams(dimension_semantics=("parallel", "arbitrary"), vmem_limit_bytes=VMEM_LIMIT),
    )(dff, w2, hpre, b1)


POOL_PAD = 16
POOL_ROWS = 256


def pool_lin(name, x, wvec, transpose, out_dtype):
    arr, width, cb = x
    s = arr.shape[0]
    n_steps = s // POOL_ROWS

    def body(x_ref, w_ref, o_ref, xp_ref):
        wv = w_ref[...]
        zeros = jnp.zeros((POOL_PAD, width), F32)
        xp_ref[0:POOL_PAD, :] = zeros
        xp_ref[s + POOL_PAD:s + 2 * POOL_PAD, :] = zeros

        def count(t0):
            t = lax.broadcasted_iota(jnp.int32, (POOL_ROWS, width), 0) + (t0 + 1)
            return jnp.minimum(t.astype(F32), wv)

        def fill(i, carry):
            t0 = pl.multiple_of(i * POOL_ROWS, POOL_ROWS)
            v = x_ref[pl.ds(t0, POOL_ROWS), :].astype(F32)
            if transpose:
                v = v / count(t0)
            xp_ref[pl.ds(t0 + POOL_PAD, POOL_ROWS), :] = v
            return carry

        lax.fori_loop(0, n_steps, fill, 0)

        def step(i, carry):
            t0 = pl.multiple_of(i * POOL_ROWS, POOL_ROWS)
            win = xp_ref[pl.ds(t0, POOL_ROWS + 2 * POOL_PAD), :]
            acc = jnp.zeros((POOL_ROWS, width), F32)
            for j in range(POOL_PAD):
                off = POOL_PAD + j if transpose else POOL_PAD - j
                acc = acc + jnp.where(wv > j, win[off:off + POOL_ROWS, :], 0.0)
            cur = x_ref[pl.ds(t0, POOL_ROWS), :].astype(F32)
            res = acc - cur if transpose else acc / count(t0) - cur
            o_ref[pl.ds(t0, POOL_ROWS), :] = res.astype(o_ref.dtype)
            return carry

        lax.fori_loop(0, n_steps, step, 0)

    return pl.pallas_call(
        body, name=name, grid=(1,),
        in_specs=[pl.BlockSpec((s, width), lambda i, cb=cb: (0, cb)), pl.BlockSpec((1, width), lambda i: (0, 0))],
        out_specs=pl.BlockSpec((s, width), lambda i: (0, 0)),
        out_shape=jax.ShapeDtypeStruct((s, width), out_dtype),
        scratch_shapes=[pltpu.VMEM((s + 2 * POOL_PAD, width), F32)],
        compiler_params=pltpu.CompilerParams(dimension_semantics=("arbitrary",), vmem_limit_bytes=VMEM_LIMIT),
    )(arr, wvec)


CONV_PAD = 32
CONV_ROWS = 128


def _glu(a, g):
    return a * jax.nn.sigmoid(g)


def conv_fwd(name, zc, w):
    arr, width, cb = zc
    s = arr.shape[0]
    n_steps = s // CONV_ROWS
    lead = CONV_PAD - (CONV_WIDTH - 1)

    def body(z_ref, w_ref, o_ref, hp_ref):
        hp_ref[0:CONV_PAD, :] = jnp.zeros((CONV_PAD, D_CONV), F32)

        def fill(i, carry):
            t0 = pl.multiple_of(i * CONV_ROWS, CONV_ROWS)
            z = z_ref[pl.ds(t0, CONV_ROWS), :]
            hp_ref[pl.ds(t0 + CONV_PAD, CONV_ROWS), :] = _glu(z[:, :D_CONV], z[:, D_CONV:])
            return carry

        lax.fori_loop(0, n_steps, fill, 0)
        wv = w_ref[...]

        def step(i, carry):
            t0 = pl.multiple_of(i * CONV_ROWS, CONV_ROWS)
            win = hp_ref[pl.ds(t0, CONV_ROWS + CONV_PAD), :]
            acc = jnp.zeros((CONV_ROWS, D_CONV), F32)
            for k in range(CONV_WIDTH):
                acc = acc + wv[k:k + 1, :] * win[lead + k:lead + k + CONV_ROWS, :]
            o_ref[pl.ds(t0, CONV_ROWS), :] = acc
            return carry

        lax.fori_loop(0, n_steps, step, 0)

    return pl.pallas_call(
        body, name=name, grid=(1,),
        in_specs=[pl.BlockSpec((s, width), lambda i, cb=cb: (0, cb)), pl.BlockSpec(w.shape, lambda i: (0, 0))],
        out_specs=pl.BlockSpec((s, D_CONV), lambda i: (0, 0)),
        out_shape=jax.ShapeDtypeStruct((s, D_CONV), F32),
        scratch_shapes=[pltpu.VMEM((s + CONV_PAD, D_CONV), F32)],
        compiler_params=pltpu.CompilerParams(dimension_semantics=("arbitrary",), vmem_limit_bytes=VMEM_LIMIT),
    )(arr, w)


def conv_bwd(name, zc, w, dout):
    arr, width, cb = zc
    s = arr.shape[0]
    n_steps = s // CONV_ROWS
    lead = CONV_PAD - (CONV_WIDTH - 1)

    def body(z_ref, w_ref, d_ref, dz_ref, dw_ref, hp_ref, dp_ref):
        hp_ref[0:CONV_PAD, :] = jnp.zeros((CONV_PAD, D_CONV), F32)
        dp_ref[s:s + CONV_PAD, :] = jnp.zeros((CONV_PAD, D_CONV), F32)
        dw_ref[...] = jnp.zeros(dw_ref.shape, F32)

        def fill(i, carry):
            t0 = pl.multiple_of(i * CONV_ROWS, CONV_ROWS)
            z = z_ref[pl.ds(t0, CONV_ROWS), :]
            hp_ref[pl.ds(t0 + CONV_PAD, CONV_ROWS), :] = _glu(z[:, :D_CONV], z[:, D_CONV:])
            dp_ref[pl.ds(t0, CONV_ROWS), :] = d_ref[pl.ds(t0, CONV_ROWS), :]
            return carry

        lax.fori_loop(0, n_steps, fill, 0)
        wv = w_ref[...]

        def step(i, carry):
            t0 = pl.multiple_of(i * CONV_ROWS, CONV_ROWS)
            hwin = hp_ref[pl.ds(t0, CONV_ROWS + CONV_PAD), :]
            dwin = dp_ref[pl.ds(t0, CONV_ROWS + CONV_PAD), :]
            dcur = dwin[0:CONV_ROWS, :]
            dh = jnp.zeros((CONV_ROWS, D_CONV), F32)
            rows = []
            for k in range(CONV_WIDTH):
                rows.append(jnp.sum(dcur * hwin[lead + k:lead + k + CONV_ROWS, :], axis=0, keepdims=True))
                back = CONV_WIDTH - 1 - k
                dh = dh + wv[k:k + 1, :] * dwin[back:back + CONV_ROWS, :]
            rows.append(jnp.zeros((1, D_CONV), F32))
            dw_ref[...] += jnp.concatenate(rows, axis=0)
            z = z_ref[pl.ds(t0, CONV_ROWS), :]
            _, vjp = jax.vjp(_glu, z[:, :D_CONV], z[:, D_CONV:])
            da, dg = vjp(dh)
            dz_ref[pl.ds(t0, CONV_ROWS), :] = jnp.concatenate([da, dg], axis=1).astype(dz_ref.dtype)
            return carry

        lax.fori_loop(0, n_steps, step, 0)

    return pl.pallas_call(
        body, name=name, grid=(1,),
        in_specs=[pl.BlockSpec((s, width), lambda i, cb=cb: (0, cb)), pl.BlockSpec(w.shape, lambda i: (0, 0)),
                  pl.BlockSpec((s, D_CONV), lambda i: (0, 0))],
        out_specs=[pl.BlockSpec((s, width), lambda i: (0, 0)), pl.BlockSpec(w.shape, lambda i: (0, 0))],
        out_shape=[jax.ShapeDtypeStruct((s, width), BF16), jax.ShapeDtypeStruct(w.shape, F32)],
        scratch_shapes=[pltpu.VMEM((s + CONV_PAD, D_CONV), F32), pltpu.VMEM((s + CONV_PAD, D_CONV), F32)],
        compiler_params=pltpu.CompilerParams(dimension_semantics=("arbitrary",), vmem_limit_bytes=VMEM_LIMIT),
    )(arr, w, dout)


HEADS_PER_STEP = LANE // HEAD_DIM
CHUNKS_PER_TILE = ATT_TILE // CHUNK
KEY_BLOCKS = N_PREV * CHUNK // ATT_TILE + 1
KEY_SPAN = KEY_BLOCKS * ATT_TILE


def _attn_tile(q, *rest, missing_cols):
    kcat = jnp.concatenate(rest[:KEY_BLOCKS], axis=0)
    vcat = jnp.concatenate(rest[KEY_BLOCKS:2 * KEY_BLOCKS], axis=0)
    bias = rest[2 * KEY_BLOCKS]
    lane = lax.broadcasted_iota(jnp.int32, (1, LANE), 1)
    col = lax.broadcasted_iota(jnp.int32, (1, KEY_SPAN), 1)
    missing = col < missing_cols
    qs = q * (HEAD_DIM ** -0.5)
    o = jnp.zeros((ATT_TILE, LANE), F32)
    for h in range(HEADS_PER_STEP):
        in_head = jnp.logical_and(lane >= h * HEAD_DIM, lane < (h + 1) * HEAD_DIM)
        sc = mm_nt(jnp.where(in_head, qs, 0.0), kcat) + bias[h]
        sc = jnp.where(missing, NEG_INF, sc)
        m = jnp.max(sc, axis=-1, keepdims=True)
        e = jnp.exp(sc - lax.stop_gradient(m))
        p = e / jnp.sum(e, axis=-1, keepdims=True)
        o = o + jnp.where(in_head, mm_nn(p, vcat), 0.0)
    return o


def _missing_cols(n):
    return jnp.maximum((KEY_BLOCKS - 1 - n) * ATT_TILE, 0)


def _attn_in_specs(nt):
    def spec(col0, back):
        return pl.BlockSpec((ATT_TILE, LANE),
                            lambda hp, n, col0=col0, back=back: (jnp.clip(n - back, 0, nt - 1), col0 // LANE + hp))
    backs = list(range(KEY_BLOCKS - 1, -1, -1))
    return ([spec(Z_Q, 0)] + [spec(Z_K, b) for b in backs] + [spec(Z_V, b) for b in backs]
            + [pl.BlockSpec((HEADS_PER_STEP, ATT_TILE, KEY_SPAN), lambda hp, n: (hp, 0, 0))])


def attn_fwd(name, z, bias):
    s = z.shape[0]
    nt = s // ATT_TILE
    n_in = 2 + 2 * KEY_BLOCKS

    def body(*refs):
        o_ref = refs[n_in]
        vals = [r[...] for r in refs[:n_in]]
        o = _attn_tile(*vals, missing_cols=_missing_cols(pl.program_id(1)))
        o_ref[...] = o.astype(o_ref.dtype)

    return pl.pallas_call(
        body, name=name, grid=(N_HEADS // HEADS_PER_STEP, nt),
        in_specs=_attn_in_specs(nt),
        out_specs=pl.BlockSpec((ATT_TILE, LANE), lambda hp, n: (n, hp)),
        out_shape=jax.ShapeDtypeStruct((s, D_ATTN), BF16),
        compiler_params=pltpu.CompilerParams(dimension_semantics=("parallel", "parallel"), vmem_limit_bytes=VMEM_LIMIT),
    )(*([z] * (n_in - 1)), bias)


def attn_bwd(name, z, bias, do):
    s = z.shape[0]
    nt = s // ATT_TILE
    n_in = 2 + 2 * KEY_BLOCKS
    nc = KEY_BLOCKS - 1

    def body(*refs):
        do_ref = refs[n_in]
        dq_ref, dk_ref, dv_ref, db_ref = refs[n_in + 1:n_in + 5]
        kacc, vacc = refs[n_in + 5:n_in + 5 + nc], refs[n_in + 5 + nc:]
        n = pl.program_id(1)

        @pl.when(n == 0)
        def _():
            db_ref[...] = jnp.zeros(db_ref.shape, F32)
            for acc in (*kacc, *vacc):
                acc[...] = jnp.zeros(acc.shape, F32)

        def shift(out_ref, accs, contrib):
            @pl.when(n >= nc)
            def _():
                first = accs[0][...] if contrib is None else accs[0][...] + contrib[0]
                out_ref[...] = first.astype(out_ref.dtype)
            for j in range(nc - 1):
                accs[j][...] = accs[j + 1][...] if contrib is None else accs[j + 1][...] + contrib[j + 1]
            if contrib is not None:
                accs[nc - 1][...] = contrib[nc]

        @pl.when(n < nt)
        def _():
            fn = functools.partial(_attn_tile, missing_cols=_missing_cols(n))
            _, vjp = jax.vjp(fn, *[r[...] for r in refs[:n_in]])
            grads = vjp(do_ref[...].astype(F32))
            dq_ref[...] = grads[0].astype(dq_ref.dtype)
            db_ref[...] += grads[n_in - 1]
            shift(dk_ref, kacc, grads[1:1 + KEY_BLOCKS])
            shift(dv_ref, vacc, grads[1 + KEY_BLOCKS:1 + 2 * KEY_BLOCKS])

        @pl.when(n >= nt)
        def _():
            shift(dk_ref, kacc, None)
            shift(dv_ref, vacc, None)

    o_cur = pl.BlockSpec((ATT_TILE, LANE), lambda hp, n: (jnp.minimum(n, nt - 1), hp))
    o_old = pl.BlockSpec((ATT_TILE, LANE), lambda hp, n: (jnp.maximum(n - nc, 0), hp))
    b_spec = pl.BlockSpec((HEADS_PER_STEP, ATT_TILE, KEY_SPAN), lambda hp, n: (hp, 0, 0))
    return pl.pallas_call(
        body, name=name, grid=(N_HEADS // HEADS_PER_STEP, nt + nc),
        in_specs=_attn_in_specs(nt) + [o_cur],
        out_specs=[o_cur, o_old, o_old, b_spec],
        out_shape=[jax.ShapeDtypeStruct((s, D_ATTN), BF16)] * 3 + [jax.ShapeDtypeStruct((N_HEADS, ATT_TILE, KEY_SPAN), F32)],
        scratch_shapes=[pltpu.VMEM((ATT_TILE, LANE), F32)] * (2 * nc),
        compiler_params=pltpu.CompilerParams(dimension_semantics=("parallel", "arbitrary"), vmem_limit_bytes=VMEM_LIMIT),
    )(*([z] * (n_in - 1)), bias, do)


def loss_head(name, y, tgt, tm):
    s, d = y.shape

    def body(y_ref, t_ref, l_ref, dy_ref):
        i = pl.program_id(0)
        diff = y_ref[...] - t_ref[...]
        dy_ref[...] = diff * (1.0 / d)
        part = 0.5 * jnp.sum(jnp.mean(diff * diff, axis=-1, keepdims=True), axis=0, keepdims=True)

        @pl.when(i == 0)
        def _():
            l_ref[...] = jnp.zeros(l_ref.shape, F32)

        l_ref[...] += jnp.broadcast_to(part, l_ref.shape)

    row = pl.BlockSpec((tm, d), lambda i: (i, 0))
    return pl.pallas_call(
        body, name=name, grid=(s // tm,), in_specs=[row, row],
        out_specs=[pl.BlockSpec((8, LANE), lambda i: (0, 0)), row],
        out_shape=[jax.ShapeDtypeStruct((8, LANE), F32), jax.ShapeDtypeStruct((s, d), F32)],
        compiler_params=pltpu.CompilerParams(dimension_semantics=("arbitrary",), vmem_limit_bytes=VMEM_LIMIT),
    )(y, tgt)


def adamw(name, w, g, m, v, tr):
    r, c = w.shape
    assert r % tr == 0, (name, w.shape, tr)

    def body(w_ref, g_ref, m_ref, v_ref, d_ref, nm_ref, nv_ref):
        gg = g_ref[...]
        m2 = ADAM_B1 * m_ref[...] + (1.0 - ADAM_B1) * gg
        v2 = ADAM_B2 * v_ref[...] + (1.0 - ADAM_B2) * (gg * gg)
        m_hat = m2 / (1.0 - ADAM_B1 ** ADAM_STEP)
        v_hat = v2 / (1.0 - ADAM_B2 ** ADAM_STEP)
        d_ref[...] = -ADAM_LR * (m_hat / (jnp.sqrt(v_hat) + ADAM_EPS) + ADAM_WD * w_ref[...])
        nm_ref[...] = m2
        nv_ref[...] = v2

    blk = pl.BlockSpec((tr, c), lambda i: (i, 0))
    return pl.pallas_call(
        body, name=name, grid=(r // tr,), in_specs=[blk] * 4, out_specs=[blk] * 3,
        out_shape=[jax.ShapeDtypeStruct((r, c), F32)] * 3,
        compiler_params=pltpu.CompilerParams(dimension_semantics=("parallel",), vmem_limit_bytes=VMEM_LIMIT),
    )(w, g, m, v)


def adamw_sum(name, w, layer_blocks, m, v, tr):
    rows, c = w.shape
    nl = len(layer_blocks)
    nb, r, _ = layer_blocks[0].shape
    assert rows == nl * r and r % tr == 0, (name, w.shape, layer_blocks[0].shape, tr)
    per = r // tr

    def body(*refs):
        w_ref, b_refs, (m_ref, v_ref, g_ref, d_ref, nm_ref, nv_ref) = refs[0], refs[1:1 + nl], refs[1 + nl:]
        i = pl.program_id(0)

        def update(b_ref):
            gg = b_ref[0].astype(F32)
            for j in range(1, nb):
                gg = gg + b_ref[j].astype(F32)
            g_ref[...] = gg
            m2 = ADAM_B1 * m_ref[...] + (1.0 - ADAM_B1) * gg
            v2 = ADAM_B2 * v_ref[...] + (1.0 - ADAM_B2) * (gg * gg)
            m_hat = m2 / (1.0 - ADAM_B1 ** ADAM_STEP)
            v_hat = v2 / (1.0 - ADAM_B2 ** ADAM_STEP)
            d_ref[...] = -ADAM_LR * (m_hat / (jnp.sqrt(v_hat) + ADAM_EPS) + ADAM_WD * w_ref[...])
            nm_ref[...] = m2
            nv_ref[...] = v2

        for l in range(nl):
            pl.when(jnp.logical_and(i >= l * per, i < (l + 1) * per))(functools.partial(update, b_refs[l]))

    blk = pl.BlockSpec((tr, c), lambda i: (i, 0))
    b_specs = [pl.BlockSpec((nb, tr, c), lambda i, l=l: (0, jnp.clip(i - l * per, 0, per - 1), 0)) for l in range(nl)]
    return pl.pallas_call(
        body, name=name, grid=(rows // tr,),
        in_specs=[blk] + b_specs + [blk, blk], out_specs=[blk] * 4,
        out_shape=[jax.ShapeDtypeStruct((rows, c), F32)] * 4,
        compiler_params=pltpu.CompilerParams(dimension_semantics=("arbitrary",), vmem_limit_bytes=VMEM_LIMIT),
    )(w, *layer_blocks, m, v)


def sum_blocks(name, blocks, tr):
    nb, r, c = blocks.shape
    assert r % tr == 0, (name, blocks.shape, tr)

    def body(b_ref, o_ref):
        acc = b_ref[0].astype(F32)
        for j in range(1, nb):
            acc = acc + b_ref[j].astype(F32)
        o_ref[...] = acc

    return pl.pallas_call(
        body, name=name, grid=(r // tr,),
        in_specs=[pl.BlockSpec((nb, tr, c), lambda i: (0, i, 0))],
        out_specs=pl.BlockSpec((tr, c), lambda i: (i, 0)),
        out_shape=jax.ShapeDtypeStruct((r, c), F32),
        compiler_params=pltpu.CompilerParams(dimension_semantics=("parallel",), vmem_limit_bytes=VMEM_LIMIT),
    )(blocks)


FLIPS = [(0, 0, 1), (1, 0, 0), (0, 1, 0), (1, 1, 0), (1, 0, 1), (0, 1, 1), (1, 1, 1)]
ANY = pl.BlockSpec(memory_space=pl.ANY)


def _me():
    return lax.axis_index("x"), lax.axis_index("y"), lax.axis_index("c")


def _flip(pos, f):
    return tuple((1 - p) if fi else p for p, fi in zip(pos, f))


def _idx(pos):
    return 4 * pos[0] + 2 * pos[1] + pos[2]


def all_gather_multi(name, shards):
    n = len(shards)

    def body(*refs):
        x_refs, out_refs = refs[:n], refs[n:2 * n]
        send_sems, recv_sems, local_sems = refs[2 * n:]
        x, y, cc = _me()
        me, sibling = (x, y, cc), (x, y, 1 - cc)
        chips = [(1 - x, y), (x, 1 - y), (1 - x, 1 - y)]

        def copy(a, k, block, to, src=None):
            dst = out_refs[a].at[_idx(block)]
            return pltpu.make_async_remote_copy(
                src_ref=dst if src is None else src, dst_ref=dst, send_sem=send_sems.at[7 * a + k],
                recv_sem=recv_sems.at[7 * a + k], device_id=to, device_id_type=MESH)

        mine = [pltpu.make_async_copy(x_refs[a], out_refs[a].at[_idx(me)], local_sems.at[a]) for a in range(n)]
        for cp in mine:
            cp.start()
        first = []
        for a in range(n):
            first.append(copy(a, 0, me, sibling, src=x_refs[a]))
            first += [copy(a, 1 + j, me, (*chip, cc), src=x_refs[a]) for j, chip in enumerate(chips)]
        for cp in first:
            cp.start()
        passed = []
        for j, chip in enumerate(chips):
            for a in range(n):
                copy(a, 1 + j, (*chip, cc), me).wait_recv()
                fwd = copy(a, 4 + j, (*chip, cc), sibling)
                fwd.start()
                passed.append(fwd)
        for a in range(n):
            copy(a, 0, sibling, me).wait_recv()
            for j, chip in enumerate(chips):
                copy(a, 4 + j, (*chip, 1 - cc), me).wait_recv()
        for cp in first + passed:
            cp.wait_send()
        for cp in mine:
            cp.wait()

    return pl.pallas_call(
        body, name=name, in_specs=[ANY] * n, out_specs=[ANY] * n,
        out_shape=[jax.ShapeDtypeStruct((N_DEV,) + a.shape, a.dtype) for a in shards],
        scratch_shapes=[pltpu.SemaphoreType.DMA((7 * n,)), pltpu.SemaphoreType.DMA((7 * n,)),
                        pltpu.SemaphoreType.DMA((n,))],
    )(*shards)


def all_to_all_multi(name, blocks):
    n = len(blocks)

    def body(*refs):
        in_refs, out_refs = refs[:n], refs[n:2 * n]
        send_sems, recv_sems, local_sems = refs[2 * n:]
        me = _me()
        mi = _idx(me)
        mine = [pltpu.make_async_copy(in_refs[a].at[mi], out_refs[a].at[mi], local_sems.at[a]) for a in range(n)]
        for cp in mine:
            cp.start()
        sends, recvs = [], []
        for k, f in enumerate(FLIPS):
            peer = _flip(me, f)
            pi = _idx(peer)
            for a in range(n):
                sems = dict(send_sem=send_sems.at[7 * a + k], recv_sem=recv_sems.at[7 * a + k],
                            device_id=peer, device_id_type=MESH)
                sends.append(pltpu.make_async_remote_copy(src_ref=in_refs[a].at[pi], dst_ref=out_refs[a].at[mi], **sems))
                recvs.append(pltpu.make_async_remote_copy(src_ref=in_refs[a].at[mi], dst_ref=out_refs[a].at[pi], **sems))
        for cp in sends:
            cp.start()
        for cp in recvs:
            cp.wait_recv()
        for cp in sends:
            cp.wait_send()
        for cp in mine:
            cp.wait()

    return pl.pallas_call(
        body, name=name, in_specs=[ANY] * n, out_specs=[ANY] * n,
        out_shape=[jax.ShapeDtypeStruct(a.shape, a.dtype) for a in blocks],
        scratch_shapes=[pltpu.SemaphoreType.DMA((7 * n,)), pltpu.SemaphoreType.DMA((7 * n,)),
                        pltpu.SemaphoreType.DMA((n,))],
    )(*blocks)


HBM = pl.BlockSpec(memory_space=pltpu.HBM)
SEM = pl.BlockSpec(memory_space=pltpu.SEMAPHORE)
DATAFLOW = pltpu.SideEffectType.DATAFLOW_SIDE_EFFECTING


def _exchange_copies(a_refs, l_refs, send_sems, recv_sems, gather):
    me = _me()
    mi = _idx(me)
    out = []
    for k, f in enumerate(FLIPS):
        peer = _flip(me, f)
        for a in range(len(a_refs)):
            src = a_refs[a] if gather else a_refs[a].at[_idx(peer)]
            out.append(pltpu.make_async_remote_copy(
                src_ref=src, dst_ref=l_refs[a].at[mi], send_sem=send_sems.at[7 * a + k],
                recv_sem=recv_sems.at[7 * a + k], device_id=peer, device_id_type=MESH))
    return out


def exchange_start(name, arrays, gather):
    n = len(arrays)
    lands = [lax.empty(((N_DEV,) + a.shape) if gather else a.shape, a.dtype) for a in arrays]

    def body(*refs):
        a_refs, l_refs = refs[:n], refs[n:2 * n]
        send_sems, recv_sems = refs[2 * n], refs[2 * n + 1]
        token = refs[4 * n + 2]
        for cp in _exchange_copies(a_refs, l_refs, send_sems, recv_sems, gather):
            cp.start()
        token[...] = jnp.zeros_like(token)

    hbm = lambda a: pltpu.HBM(a.shape, a.dtype)
    res = pl.pallas_call(
        body, name=name,
        out_shape=(pltpu.SemaphoreType.DMA((7 * n,)), pltpu.SemaphoreType.DMA((7 * n,)),
                   *[hbm(a) for a in arrays], *[hbm(a) for a in lands], jax.ShapeDtypeStruct((8, LANE), F32)),
        in_specs=[HBM] * (2 * n),
        out_specs=(SEM, SEM, *[HBM] * (2 * n), pl.BlockSpec(memory_space=pltpu.VMEM)),
        input_output_aliases={i: i + 2 for i in range(2 * n)},
        compiler_params=pltpu.CompilerParams(has_side_effects=DATAFLOW),
    )(*[pltpu.with_memory_space_constraint(a, pltpu.HBM) for a in arrays],
      *[pltpu.with_memory_space_constraint(a, pltpu.HBM) for a in lands])
    return res[0], res[1], list(res[2:2 + n]), list(res[2 + n:2 + 2 * n]), res[-1]


def exchange_wait(name, send_sems, recv_sems, arrays, lands, after, gather):
    n = len(arrays)

    def body(*refs):
        a_refs, l_refs = refs[:n], refs[n:2 * n]
        ssem, rsem = refs[2 * n], refs[2 * n + 1]
        for cp in _exchange_copies(a_refs, l_refs, ssem, rsem, gather):
            cp.wait_send()
            cp.wait_recv()

    hbm = lambda a: pltpu.HBM(a.shape, a.dtype)
    res = pl.pallas_call(
        body, name=name,
        out_shape=(*[hbm(a) for a in arrays], *[hbm(a) for a in lands]),
        in_specs=[HBM] * (2 * n) + [SEM, SEM, pl.BlockSpec(memory_space=pl.ANY)],
        out_specs=tuple([HBM] * (2 * n)),
        input_output_aliases={i: i for i in range(2 * n)},
        compiler_params=pltpu.CompilerParams(has_side_effects=DATAFLOW),
    )(*arrays, *lands, send_sems, recv_sems, after)
    return list(res[n:])


def ada_fwd(name, c_row, w_cat, b_lay):
    d = c_row.shape[1]
    ncol = w_cat.shape[1]
    vmem = pl.BlockSpec(memory_space=pltpu.VMEM)

    def body(c_ref, w_ref, b_ref, mod_ref, cact_ref, call, send, land, s1, r1, s2, r2):
        me = _me()
        mi = _idx(me)
        call[mi] = c_ref[...]

        def exchange(src_of, dst_buf, ssem, rsem):
            sends, recvs = [], []
            for k, f in enumerate(FLIPS):
                peer = _flip(me, f)
                sends.append(pltpu.make_async_remote_copy(
                    src_ref=src_of(peer), dst_ref=dst_buf.at[mi], send_sem=ssem.at[k], recv_sem=rsem.at[k],
                    device_id=peer, device_id_type=MESH))
                recvs.append(pltpu.make_async_remote_copy(
                    src_ref=src_of(peer), dst_ref=dst_buf.at[_idx(peer)], send_sem=ssem.at[k], recv_sem=rsem.at[k],
                    device_id=peer, device_id_type=MESH))
            for cp in sends:
                cp.start()
            for cp in recvs:
                cp.wait_recv()
            for cp in sends:
                cp.wait_send()

        exchange(lambda peer: c_ref, call, s1, r1)
        for p in range(N_DEV):
            cact_ref[pl.ds(p, 1), :] = jax.nn.silu(call[p])
        res = _dg(cact_ref[...], w_ref[...], 1, 0)
        for p in range(N_DEV):
            send[p] = res[p:p + 1, :]
        land[mi] = send[mi]
        exchange(lambda peer: send.at[_idx(peer)], land, s2, r2)
        mod_ref[...] = land[...] + b_ref[...]

    return pl.pallas_call(
        body, name=name, in_specs=[vmem, vmem, vmem], out_specs=[vmem, vmem],
        out_shape=[jax.ShapeDtypeStruct((N_DEV, 1, ncol), F32), jax.ShapeDtypeStruct((N_DEV, d), F32)],
        scratch_shapes=[pltpu.VMEM((N_DEV, 1, d), F32), pltpu.VMEM((N_DEV, 1, ncol), F32),
                        pltpu.VMEM((N_DEV, 1, ncol), F32),
                        pltpu.SemaphoreType.DMA((7,)), pltpu.SemaphoreType.DMA((7,)),
                        pltpu.SemaphoreType.DMA((7,)), pltpu.SemaphoreType.DMA((7,))],
        compiler_params=pltpu.CompilerParams(vmem_limit_bytes=VMEM_LIMIT),
    )(c_row, w_cat, b_lay)


POOL_WINDOWS = (2, 4, 8, 16)
POOL_GROUP = 64
N_REL = 2 * REL_CLIP + 1
PACK_COLS = 1024
SMALL_NAMES = ["b_ada", "b_gate", "w_pool", "pool_scale", "rel_bias", "conv_w", "conv_b", "conv_ln_g",
               "conv_ln_b", "ln_mix_g", "ln_mix_b", "b_ff1", "b_ff2", "ln_ff_g", "ln_ff_b"]
BIG_NAMES = ["w_in", "w_br_pool", "w_br_attn", "w_br_conv", "w_o", "w_ff1", "w_ff2"]
ROW_SHARDED = ("w_o", "w_ff2")
WEIGHT_NAMES = ["w_ada", "b_ada", "w_in", "b_gate", "w_pool", "pool_scale", "rel_bias", "conv_w", "conv_b",
                "conv_ln_g", "conv_ln_b", "w_br_pool", "w_br_attn", "w_br_conv", "w_o", "ln_mix_g", "ln_mix_b",
                "w_ff1", "b_ff1", "w_ff2", "b_ff2", "ln_ff_g", "ln_ff_b"]


def _perm_cols(w):
    return jnp.concatenate([w[:, 2304:], w[:, 1792:2304], w[:, :256], w[:, 256:768], w[:, 768:1280],
                            w[:, 1280:1792]], axis=1)


def _unperm_cols(wp):
    return jnp.concatenate([wp[:, Z_POOL:Z_Q], wp[:, Z_Q:Z_K], wp[:, Z_K:Z_V], wp[:, Z_V:],
                            wp[:, Z_CONV:Z_POOL], wp[:, :Z_CONV]], axis=1)


def _bias_table(rel_bias):
    far = jnp.broadcast_to(rel_bias[:, 2 * REL_CLIP:], (N_HEADS, BAND - REL_CLIP))
    near = rel_bias[:, REL_CLIP - CHUNK + 1:2 * REL_CLIP][:, ::-1]
    ext = jnp.concatenate([far, near], axis=1)
    return jnp.stack([ext[:, CHUNK - 1 - qi:CHUNK - 1 - qi + BAND] for qi in range(CHUNK)], axis=1)


def _bias_full(rel_bias):
    tab = _bias_table(rel_bias)
    return jnp.concatenate(
        [jnp.pad(tab, ((0, 0), (0, 0), (i * CHUNK, KEY_SPAN - BAND - i * CHUNK)), constant_values=NEG_INF)
         for i in range(CHUNKS_PER_TILE)], axis=1)


def _block_diag(w_pool):
    out = jnp.zeros((D_POOL, D_POOL), F32)
    for g in range(len(POOL_WINDOWS)):
        out = lax.dynamic_update_slice(out, w_pool[g], (g * POOL_GROUP, g * POOL_GROUP))
    return out


def _flat_pad(arrs, mult):
    flat = jnp.concatenate([a.reshape(-1) for a in arrs])
    pad = (-flat.shape[0]) % mult
    return jnp.pad(flat, (0, pad)) if pad else flat


def _unflat(flat, shapes):
    out, off = [], 0
    for shp in shapes:
        n = int(np.prod(shp))
        out.append(flat[off:off + n].reshape(shp))
        off += n
    return out


def _to_blocks(name, full):
    k, n = full.shape
    if name in ROW_SHARDED:
        return full.reshape(N_DEV, k // N_DEV, n)
    return full.reshape(k, N_DEV, n // N_DEV).transpose(1, 0, 2)


def _from_blocks(name, blocks):
    nb, r, c = blocks.shape
    if name in ROW_SHARDED:
        return blocks.reshape(nb * r, c)
    return blocks.transpose(1, 0, 2).reshape(r, nb * c)


class _Layer:
    pass


def _row(v):
    return v.reshape(1, -1)


def _layer_fwd(x, modr, w, wvec):
    sh_m, sc_m, g_m, sh_f, sc_f, g_f = modr
    (u,) = row_fwd("lnmod_mix", f_lnmod, [x], [sc_m, sh_m], [(D_MODEL, BF16)], 512)
    z = mm_big("mm_in", u, w.w_in, "nn", (512, 896, 1024), F32, j_outer=True)
    p = pool_lin("pool_fwd", (z, D_POOL, Z_POOL // D_POOL), wvec, False, F32)
    ao = attn_fwd("attn_fwd", z, w.bias)
    cv = conv_fwd("conv_fwd", (z, 2 * D_CONV, Z_CONV // (2 * D_CONV)), w.conv_w)
    mparams = [w.wbd, w.ps, w.wbp, w.wba, w.wbc, w.cb, w.clg, w.clb, w.bg, w.wo, g_m, w.lmg, w.lmb]
    (x1,) = row_fwd("merge", f_merge, [p, ao, cv, (z, 3 * D_MODEL, 0), x], mparams, [(D_MODEL, F32)], 256)
    (u2,) = row_fwd("lnmod_ff", f_lnmod, [x1], [sc_f, sh_f], [(D_MODEL, BF16)], 512)
    hpre, h = mm_ff1_relu2("mm_ff1", u2, w.w_ff1, w.b1, 512, 1024)
    ff = mm_big("mm_ff2", h, w.w_ff2, "nn", (512, 1024, 4096), F32)
    (x2,) = row_fwd("ffout", f_ffout, [x1, ff], [w.b2, g_f, w.lfg, w.lfb], [(D_MODEL, F32)], 512)
    return x2, (x, u, z, p, ao, cv, x1, u2, hpre, h, ff, mparams)


GRAD_GROUPS = [("ff", ["w_ff2", "w_ff1"]), ("mix", ["w_o", "w_br_pool", "w_br_attn", "w_br_conv"]), ("in", ["w_in"])]


def _layer_bwd(dx2, saved, modr, w, wvec, ready):
    x, u, z, p, ao, cv, x1, u2, hpre, h, ff, mparams = saved
    sh_m, sc_m, g_m, sh_f, sc_f, g_f = modr
    g = {}
    (dx1a, dff), (g["b_ff2"], dgf, g["ln_ff_g"], g["ln_ff_b"]) = row_bwd(
        "ffout_bwd", f_ffout, [x1, ff], [w.b2, g_f, w.lfg, w.lfb], [dx2], 512, [(0, F32), (1, BF16)], [0, 1, 2, 3])
    dhpre, g["b_ff1"] = mm_dh_relu2("mm_dh", dff, w.w_ff2, hpre, w.b1, 512, 1024)
    g["w_ff2"] = mm_big("mm_dw_ff2", h, dff, "tn", (1024, 1024, 2048), BF16)
    du2 = mm_big("mm_du2", dhpre, w.w_ff1, "nt", (512, 1024, 4096), F32)
    g["w_ff1"] = mm_big("mm_dw_ff1", u2, dhpre, "tn", (1024, 1024, 2048), BF16)
    sc_f = sc_f + ready("ff", g)
    (dx1,), (dscf, dshf) = row_bwd("lnmod_ff_bwd", f_lnmod, [x1], [sc_f, sh_f], [du2], 512, [(0, F32)], [0, 1],
                                   add_to=dx1a)
    (dp, dao, dcv, dzg, dxa), dm = row_bwd(
        "merge_bwd", f_merge, [p, ao, cv, (z, 3 * D_MODEL, 0), x], mparams, [dx1], 256,
        [(0, F32), (1, BF16), (2, F32), (3, BF16), (4, F32)], list(range(13)))
    (dwbd, g["pool_scale"], g["w_br_pool"], g["w_br_attn"], g["w_br_conv"], g["conv_b"], g["conv_ln_g"],
     g["conv_ln_b"], g["b_gate"], g["w_o"], dgm, g["ln_mix_g"], g["ln_mix_b"]) = dm
    g["w_pool"] = jnp.stack([dwbd[i * POOL_GROUP:(i + 1) * POOL_GROUP, i * POOL_GROUP:(i + 1) * POOL_GROUP]
                             for i in range(len(POOL_WINDOWS))])
    dzp = pool_lin("pool_bwd", (dp, D_POOL, 0), wvec + ready("mix", g), True, BF16)
    dq, dk, dv, dbias = attn_bwd("attn_bwd", z, w.bias, dao)
    (g["rel_bias"],) = w.bias_vjp(dbias)
    dzc, dcw = conv_bwd("conv_bwd", (z, 2 * D_CONV, Z_CONV // (2 * D_CONV)), w.conv_w, dcv)
    g["conv_w"] = dcw[:CONV_WIDTH]
    dz = jnp.concatenate([dzg, dzc, dzp, dq, dk, dv], axis=1)
    du = mm_big("mm_du", dz, w.w_in, "nt", (512, 1024, D_IN), F32)
    (dx,), (dscm, dshm) = row_bwd("lnmod_mix_bwd", f_lnmod, [x], [sc_m, sh_m], [du], 512, [(0, F32)], [0, 1],
                                  add_to=dxa)
    g["dmod"] = jnp.concatenate([dshm, dscm, dgm, dshf, dscf, dgf], axis=1)
    tok = ready("small", g)
    g["w_in"] = _unperm_cols(mm_big("mm_dw_in", u, dz, "tn", (1024, 896, 2048), BF16, after=tok))
    return dx, g, ready("in", g)


def kernel(x, c, w_ada, b_ada, w_in, b_gate, w_pool, pool_scale, rel_bias, conv_w, conv_b, conv_ln_g, conv_ln_b, w_br_pool, w_br_attn, w_br_conv, w_o, ln_mix_g, ln_mix_b, w_ff1, b_ff1, w_ff2, b_ff2, ln_ff_g, ln_ff_b, loss_target, m_w_ada, m_b_ada, m_w_in, m_b_gate, m_w_pool, m_pool_scale, m_rel_bias, m_conv_w, m_conv_b, m_conv_ln_g, m_conv_ln_b, m_w_br_pool, m_w_br_attn, m_w_br_conv, m_w_o, m_ln_mix_g, m_ln_mix_b, m_w_ff1, m_b_ff1, m_w_ff2, m_b_ff2, m_ln_ff_g, m_ln_ff_b, v_w_ada, v_b_ada, v_w_in, v_b_gate, v_w_pool, v_pool_scale, v_rel_bias, v_conv_w, v_conv_b, v_conv_ln_g, v_conv_ln_b, v_w_br_pool, v_w_br_attn, v_w_br_conv, v_w_o, v_ln_mix_g, v_ln_mix_b, v_w_ff1, v_b_ff1, v_w_ff2, v_b_ff2, v_ln_ff_g, v_ln_ff_b):
    args = dict(locals())
    wts = {n: args[n] for n in WEIGHT_NAMES}
    mom = {n: args["m_" + n] for n in WEIGHT_NAMES}
    var = {n: args["v_" + n] for n in WEIGHT_NAMES}
    me = 4 * lax.axis_index("x") + 2 * lax.axis_index("y") + lax.axis_index("c")
    xs, tgt = x[0], loss_target[0]
    nc_ada = w_ada.shape[2]
    wvec = jnp.asarray(np.repeat(np.array(POOL_WINDOWS, np.float32), POOL_GROUP)[None, :])

    w_cat = jnp.concatenate([w_ada[l] for l in range(DEPTH)], axis=1)
    b_lay = b_ada.reshape(DEPTH, N_DEV, nc_ada).transpose(1, 0, 2).reshape(N_DEV, 1, DEPTH * nc_ada)
    land, cact = ada_fwd("ada_fwd", c, w_cat, b_lay)
    mod = land.reshape(N_DEV, DEPTH, nc_ada).transpose(1, 0, 2).reshape(DEPTH, 6 * D_MODEL)
    modr = [[mod[l:l + 1, i * D_MODEL:(i + 1) * D_MODEL] for i in range(6)] for l in range(DEPTH)]

    cw_pack = _flat_pad([conv_w], 8 * LANE).reshape(-1, LANE)
    shards = [[wts[n][l].astype(BF16) for n in BIG_NAMES] for l in range(DEPTH)]
    gathered0 = all_gather_multi("gather_weights_l0", shards[0] + [cw_pack])
    cw_all = gathered0[-1].reshape(N_DEV, -1)[:, :conv_w.size]
    conv_full = cw_all.reshape((N_DEV,) + conv_w.shape).transpose(1, 2, 0, 3).reshape(DEPTH, CONV_WIDTH, D_CONV)
    ssem, rsem, thru, lands, token = exchange_start("gather_weights_l1_start", shards[1], True)
    modr[0] = [r + token[0:1, 0:1] for r in modr[0]]

    def layer_weights(l, blocks):
        full = {n: _from_blocks(n, g) for n, g in zip(BIG_NAMES, blocks)}
        w = _Layer()
        w.w_in = _perm_cols(full["w_in"])
        w.wbp, w.wba, w.wbc = full["w_br_pool"], full["w_br_attn"], full["w_br_conv"]
        w.wo, w.w_ff1, w.w_ff2 = full["w_o"], full["w_ff1"], full["w_ff2"]
        w.wbd = _block_diag(w_pool[l])
        w.ps, w.cb, w.clg, w.clb = _row(pool_scale[l]), _row(conv_b[l]), _row(conv_ln_g[l]), _row(conv_ln_b[l])
        w.bg, w.lmg, w.lmb = _row(b_gate[l]), _row(ln_mix_g[l]), _row(ln_mix_b[l])
        w.b1, w.b2, w.lfg, w.lfb = _row(b_ff1[l]), _row(b_ff2[l]), _row(ln_ff_g[l]), _row(ln_ff_b[l])
        w.conv_w = jnp.pad(conv_full[l], ((0, CONV_PAD - CONV_WIDTH), (0, 0)))
        w.bias, w.bias_vjp = jax.vjp(_bias_full, rel_bias[l])
        return w

    def with_own(lands_, own):
        return [lax.dynamic_update_index_in_dim(ld, o, me, axis=0) for ld, o in zip(lands_, own)]

    layers, saved = [layer_weights(0, gathered0[:-1])], []
    h, sv = _layer_fwd(xs, modr[0], layers[0], wvec)
    saved.append(sv)
    lands = exchange_wait("gather_weights_l1_wait", ssem, rsem, thru, lands, h, True)
    layers.append(layer_weights(1, with_own(lands, shards[1])))
    h, sv = _layer_fwd(h, modr[1], layers[1], wvec)
    saved.append(sv)
    lpart, dy = loss_head("loss_head", h, tgt, 512)
    loss = lax.psum(lpart[0, 0], ("x", "y", "c"))
    grads, dmods = [None] * DEPTH, [None] * DEPTH
    pending = {}
    small_shapes = [wts[n].shape if n != "conv_w" else (DEPTH, CONV_WIDTH, D_CONV) for n in SMALL_NAMES]

    def ready_for(l):
        def ready(group, g):
            if group == "small":
                if l > 0:
                    return None
                both = [g, grads[1]]
                local = [jnp.concatenate([both[k]["dmod"] for k in range(DEPTH)], axis=0)]
                local += [jnp.stack([both[k][n].reshape(shp[1:]) for k in range(DEPTH)])
                          for n, shp in zip(SMALL_NAMES[1:], small_shapes[1:])]
                pack = _flat_pad(local, 8 * LANE).reshape(-1, LANE)
                ssem, rsem, thru, lands, token = exchange_start("gather_small_grads_start", [pack], True)
                pending["small"] = (pack, ssem, rsem, thru, lands)
                return token
            names = dict(GRAD_GROUPS)[group]
            blocks = [_to_blocks(n, g[n]).astype(BF16) for n in names]
            ssem, rsem, thru, lands, token = exchange_start(f"scatter_l{l}_{group}_start", blocks, False)
            pending[(l, group)] = (names, blocks, ssem, rsem, thru, lands)
            return token[0:1, 0:1]
        return ready

    def received(l, group, after):
        names, blocks, ssem, rsem, thru, lands = pending[(l, group)]
        lands = exchange_wait(f"scatter_l{l}_{group}_wait", ssem, rsem, thru, lands, after, False)
        own = [lax.dynamic_index_in_dim(b, me, axis=0, keepdims=False) for b in blocks]
        return dict(zip(names, with_own(lands, own)))

    dy, grads[1], token = _layer_bwd(dy, saved[1], modr[1], layers[1], wvec, ready_for(1))
    modr0 = [r + token for r in modr[0]]
    dy, grads[0], _ = _layer_bwd(dy, saved[0], modr0, layers[0], wvec, ready_for(0))
    recv = [{} for _ in range(DEPTH)]
    for l, group in [(1, "ff"), (1, "mix"), (1, "in"), (0, "ff"), (0, "mix")]:
        recv[l].update(received(l, group, dy))
    grad_x = dy[None]
    dmod_size = DEPTH * 6 * D_MODEL

    small_pack, ssem, rsem, thru, lands = pending["small"]
    (small_all,) = with_own(exchange_wait("gather_small_grads_wait", ssem, rsem, thru, lands, dy, True), [small_pack])
    small_sum = sum_blocks("sum_small_grads", small_all, small_pack.shape[0]).reshape(-1)
    gsmall = dict(zip(SMALL_NAMES, _unflat(small_sum, small_shapes)))
    gw = dict(gsmall)
    gw["conv_w"] = lax.dynamic_slice_in_dim(gsmall["conv_w"], me * conv_w.shape[2], conv_w.shape[2], axis=2)

    dmod_all = small_all.reshape(N_DEV, -1)[:, :dmod_size].reshape(N_DEV, DEPTH, N_DEV, nc_ada)
    dm_mine = lax.dynamic_index_in_dim(dmod_all, me, axis=2, keepdims=False).reshape(N_DEV, DEPTH * nc_ada)
    cact_t = jnp.pad(cact.T, ((0, 0), (0, LANE - N_DEV)))
    dm_pad = jnp.pad(dm_mine, ((0, LANE - N_DEV), (0, 0)))
    dw_cat = mm_big("mm_dw_ada", cact_t, dm_pad, "nn", (D_MODEL, DEPTH * nc_ada, LANE), F32)
    gw["w_ada"] = jnp.stack([dw_cat[:, l * nc_ada:(l + 1) * nc_ada] for l in range(DEPTH)])

    delta, new_m, new_v = {}, {}, {}
    for n in ["w_ada", "w_ff2", "w_ff1", "w_o", "w_br_pool", "w_br_attn", "w_br_conv", "w_in"]:
        shp = wts[n].shape
        two_d = lambda a, shp=shp: a.reshape(shp[0] * shp[1], shp[2])
        tr = min(256, shp[1])
        if n == "w_ada":
            res = (gw[n],) + tuple(adamw("adamw_" + n, two_d(wts[n]), two_d(gw[n]), two_d(mom[n]), two_d(var[n]), tr))
        else:
            if n == "w_in":
                recv[0].update(received(0, "in", new_v["w_br_conv"]))
            res = adamw_sum("adamw_" + n, two_d(wts[n]), [recv[l][n] for l in range(DEPTH)], two_d(mom[n]),
                            two_d(var[n]), tr)
        gw[n], delta[n], new_m[n], new_v[n] = [a.reshape(shp) for a in res]
    packs = [_flat_pad([src[n] for n in SMALL_NAMES], 8 * LANE).reshape(-1, LANE) for src in (wts, gw, mom, var)]
    res = adamw("adamw_small", *packs, packs[0].shape[0])
    shapes = [wts[n].shape for n in SMALL_NAMES]
    for out, flat in zip((delta, new_m, new_v), res):
        out.update(dict(zip(SMALL_NAMES, _unflat(flat.reshape(-1), shapes))))

    return (loss, grad_x, *[gw[n] for n in WEIGHT_NAMES], *[delta[n] for n in WEIGHT_NAMES],
            *[new_m[n] for n in WEIGHT_NAMES], *[new_v[n] for n in WEIGHT_NAMES])
```

```python
import functools

import jax
import jax.numpy as jnp
import numpy as np
from jax import lax
from jax.experimental import pallas as pl
from jax.experimental.pallas import tpu as pltpu

F32 = jnp.float32
BF16 = jnp.bfloat16
MESH = pl.DeviceIdType.MESH

D_MODEL = 1024
DEPTH = 2
CHUNK = 64
N_HEADS = 8
HEAD_DIM = 64
D_POOL = 256
D_ATTN = 512
D_CONV = 256
CONV_WIDTH = 31
D_FF = 4096
D_IN = 5376
N_PREV = 8
BAND = (N_PREV + 1) * CHUNK
REL_CLIP = 128
ALPHA = (2.0 * DEPTH) ** 0.25
LN_EPS = 1e-5
NEG_INF = -1e30
N_DEV = 8

ADAM_LR, ADAM_B1, ADAM_B2, ADAM_EPS, ADAM_WD, ADAM_STEP = 0.001, 0.9, 0.999, 1e-08, 0.01, 10

VMEM_LIMIT = 56 * 1024 * 1024

Z_GATE, Z_CONV, Z_POOL, Z_Q, Z_K, Z_V = 0, 3072, 3584, 3840, 4352, 4864
ATT_TILE = 512
LANE = 128


def _dg(a, b, ca, cb):
    return lax.dot_general(a.astype(BF16), b.astype(BF16), (((ca,), (cb,)), ((), ())),
                           preferred_element_type=F32)


@jax.custom_vjp
def mm_nn(a, b):
    return _dg(a, b, 1, 0)


def _mm_nn_fwd(a, b):
    return _dg(a, b, 1, 0), (a, b)


def _mm_nn_bwd(res, g):
    a, b = res
    return _dg(g, b, 1, 1).astype(a.dtype), _dg(a, g, 0, 0).astype(b.dtype)


mm_nn.defvjp(_mm_nn_fwd, _mm_nn_bwd)


@jax.custom_vjp
def mm_nt(a, b):
    return _dg(a, b, 1, 1)


def _mm_nt_fwd(a, b):
    return _dg(a, b, 1, 1), (a, b)


def _mm_nt_bwd(res, g):
    a, b = res
    return _dg(g, b, 1, 0).astype(a.dtype), _dg(g, a, 0, 0).astype(b.dtype)


mm_nt.defvjp(_mm_nt_fwd, _mm_nt_bwd)


def _ln(x):
    mu = jnp.mean(x, axis=-1, keepdims=True)
    xc = x - mu
    var = jnp.mean(xc * xc, axis=-1, keepdims=True)
    return xc * lax.rsqrt(var + LN_EPS)


def _norm_rows(rows):
    return [r if isinstance(r, tuple) else (r, r.shape[1], 0) for r in rows]


def _row_spec(tm, r):
    _, width, cb = r
    return pl.BlockSpec((tm, width), lambda i, cb=cb: (i, cb))


def _full_spec(a):
    nd = a.ndim
    return pl.BlockSpec(a.shape, lambda i, nd=nd: (0,) * nd)


def row_fwd(name, f, rows, params, outs, tm):
    rows = _norm_rows(rows)
    s = rows[0][0].shape[0]
    nr, npar = len(rows), len(params)

    def body(*refs):
        r = [x[...].astype(F32) for x in refs[:nr]]
        p = [x[...] for x in refs[nr:nr + npar]]
        res = f(*r, *p)
        for o_ref, o in zip(refs[nr + npar:], res):
            o_ref[...] = o.astype(o_ref.dtype)

    return pl.pallas_call(
        body, name=name, grid=(s // tm,),
        in_specs=[_row_spec(tm, r) for r in rows] + [_full_spec(p) for p in params],
        out_specs=[pl.BlockSpec((tm, w), lambda i: (i, 0)) for w, _ in outs],
        out_shape=[jax.ShapeDtypeStruct((s, w), dt) for w, dt in outs],
        compiler_params=pltpu.CompilerParams(dimension_semantics=("parallel",), vmem_limit_bytes=VMEM_LIMIT),
    )(*[r[0] for r in rows], *params)


def row_bwd(name, f, rows, params, douts, tm, want_rows, want_params, add_to=None):
    rows = _norm_rows(rows)
    s = rows[0][0].shape[0]
    nr, npar, nd = len(rows), len(params), len(douts)
    nadd = 0 if add_to is None else 1
    n_in = nr + npar + nd + nadd

    def body(*refs):
        i = pl.program_id(0)
        r = [x[...].astype(F32) for x in refs[:nr]]
        p = [x[...].astype(F32) for x in refs[nr:nr + npar]]
        d = [x[...].astype(F32) for x in refs[nr + npar:nr + npar + nd]]
        _, vjp = jax.vjp(f, *r, *p)
        g = vjp(tuple(d))
        out_refs = refs[n_in:]
        for k, (idx, _) in enumerate(want_rows):
            val = g[idx]
            if nadd and k == 0:
                val = val + refs[n_in - 1][...].astype(F32)
            out_refs[k][...] = val.astype(out_refs[k].dtype)
        for k, idx in enumerate(want_params):
            gp = g[nr + idx]
            o_ref = out_refs[len(want_rows) + k]

            @pl.when(i == 0)
            def _():
                o_ref[...] = gp

            @pl.when(i > 0)
            def _():
                o_ref[...] += gp

    in_specs = ([_row_spec(tm, r) for r in rows] + [_full_spec(p) for p in params]
                + [pl.BlockSpec((tm, d.shape[1]), lambda i: (i, 0)) for d in douts])
    args = [r[0] for r in rows] + list(params) + list(douts)
    if nadd:
        in_specs.append(pl.BlockSpec((tm, add_to.shape[1]), lambda i: (i, 0)))
        args.append(add_to)
    out_specs = ([pl.BlockSpec((tm, rows[idx][1]), lambda i: (i, 0)) for idx, _ in want_rows]
                 + [_full_spec(params[idx]) for idx in want_params])
    out_shape = ([jax.ShapeDtypeStruct((s, rows[idx][1]), dt) for idx, dt in want_rows]
                 + [jax.ShapeDtypeStruct(params[idx].shape, F32) for idx in want_params])
    res = pl.pallas_call(
        body, name=name, grid=(s // tm,), in_specs=in_specs, out_specs=out_specs, out_shape=out_shape,
        compiler_params=pltpu.CompilerParams(dimension_semantics=("arbitrary",), vmem_limit_bytes=VMEM_LIMIT),
    )(*args)
    return res[:len(want_rows)], res[len(want_rows):]


def f_lnmod(x, sc, sh):
    return (_ln(x) * (1.0 + sc) + sh,)


def f_merge(p, ao, cv, zg, x, wbd, ps, wbp, wba, wbc, cb, clg, clb, bg, wo, gm, lg, lb):
    pm = mm_nn(p, wbd) * ps
    co = jax.nn.silu(_ln(cv + cb) * clg + clb)
    y_pool = mm_nn(pm, wbp)
    y_attn = mm_nn(ao, wba)
    y_conv = mm_nn(co, wbc)
    gates = jax.nn.sigmoid(zg + bg)
    merged = (gates[:, :D_MODEL] * y_pool + gates[:, D_MODEL:2 * D_MODEL] * y_attn
              + gates[:, 2 * D_MODEL:] * y_conv)
    mix = mm_nn(merged, wo)
    return (_ln(ALPHA * x + gm * mix) * lg + lb,)


def f_relu2(hpre, b1):
    a = jax.nn.relu(hpre + b1)
    return (a * a,)


def f_ffout(x1, ff, b2, gf, lg, lb):
    return (_ln(ALPHA * x1 + gf * (ff + b2)) * lg + lb,)


def mm_big(name, a, b, kind, tiles, out_dtype, j_outer=False, after=None):
    if kind == "nn":
        o0, o1, red = a.shape[0], b.shape[1], a.shape[1]
    elif kind == "nt":
        o0, o1, red = a.shape[0], b.shape[0], a.shape[1]
    else:
        o0, o1, red = a.shape[1], b.shape[1], a.shape[0]
    t0, t1, tr = min(tiles[0], o0), min(tiles[1], o1), min(tiles[2], red)
    if kind == "nn":
        a_spec = pl.BlockSpec((t0, tr), lambda i, j, r: (i, r))
        b_spec = pl.BlockSpec((tr, t1), lambda i, j, r: (r, j))
        dims = (1, 0)
    elif kind == "nt":
        a_spec = pl.BlockSpec((t0, tr), lambda i, j, r: (i, r))
        b_spec = pl.BlockSpec((t1, tr), lambda i, j, r: (j, r))
        dims = (1, 1)
    else:
        a_spec = pl.BlockSpec((tr, t0), lambda i, j, r: (r, i))
        b_spec = pl.BlockSpec((tr, t1), lambda i, j, r: (r, j))
        dims = (0, 0)
    assert o0 % t0 == 0 and o1 % t1 == 0 and red % tr == 0, (name, a.shape, b.shape, tiles)
    n0, n1, nred = o0 // t0, o1 // t1, red // tr
    o_spec = pl.BlockSpec((t0, t1), lambda i, j, r: (i, j))
    if j_outer:
        swap = lambda spec: pl.BlockSpec(spec.block_shape, lambda j, i, r, f=spec.index_map: f(i, j, r))
        a_spec, b_spec, o_spec = swap(a_spec), swap(b_spec), swap(o_spec)
        grid = (n1, n0, nred)
    else:
        grid = (n0, n1, nred)

    deps = [] if after is None else [after]
    dep_specs = [pl.BlockSpec(d.shape, lambda i, j, r, nd=d.ndim: (0,) * nd) for d in deps]
    if nred == 1:
        def body(a_ref, b_ref, *rest):
            o_ref = rest[len(deps)]
            o_ref[...] = _dg(a_ref[...], b_ref[...], *dims).astype(o_ref.dtype)
        scratch = []
    else:
        def body(a_ref, b_ref, *rest):
            o_ref, acc_ref = rest[len(deps):]
            r = pl.program_id(2)
            part = _dg(a_ref[...], b_ref[...], *dims)

            @pl.when(r == 0)
            def _():
                acc_ref[...] = part

            @pl.when(jnp.logical_and(r > 0, r < nred - 1))
            def _():
                acc_ref[...] += part

            @pl.when(r == nred - 1)
            def _():
                o_ref[...] = (acc_ref[...] + part).astype(o_ref.dtype)
        scratch = [pltpu.VMEM((t0, t1), F32)]

    return pl.pallas_call(
        body, name=name, grid=grid,
        in_specs=[a_spec, b_spec] + dep_specs,
        out_specs=o_spec,
        out_shape=jax.ShapeDtypeStruct((o0, o1), out_dtype),
        scratch_shapes=scratch,
        compiler_params=pltpu.CompilerParams(dimension_semantics=("parallel", "parallel", "arbitrary"),
                                             vmem_limit_bytes=VMEM_LIMIT),
    )(a, b, *deps)


def mm_ff1_relu2(name, u2, w1, b1, tm, tn):
    m, k = u2.shape
    n = w1.shape[1]
    tm, tn = min(tm, m), min(tn, n)

    def body(a_ref, b_ref, bias_ref, hpre_ref, h_ref):
        acc = _dg(a_ref[...], b_ref[...], 1, 0)
        hpre_ref[...] = acc
        h_ref[...] = f_relu2(acc, bias_ref[...])[0].astype(h_ref.dtype)

    out = pl.BlockSpec((tm, tn), lambda j, i: (i, j))
    return pl.pallas_call(
        body, name=name, grid=(n // tn, m // tm),
        in_specs=[pl.BlockSpec((tm, k), lambda j, i: (i, 0)), pl.BlockSpec((k, tn), lambda j, i: (0, j)),
                  pl.BlockSpec((1, tn), lambda j, i: (0, j))],
        out_specs=[out, out],
        out_shape=[jax.ShapeDtypeStruct((m, n), F32), jax.ShapeDtypeStruct((m, n), BF16)],
        compiler_params=pltpu.CompilerParams(dimension_semantics=("parallel", "parallel"), vmem_limit_bytes=VMEM_LIMIT),
    )(u2, w1, b1)


def mm_dh_relu2(name, dff, w2, hpre, b1, tm, tn):
    m, k = dff.shape
    n = w2.shape[0]
    tm, tn = min(tm, m), min(tn, n)

    def body(a_ref, b_ref, hpre_ref, bias_ref, d_ref, db_ref):
        i = pl.program_id(1)
        dh = _dg(a_ref[...], b_ref[...], 1, 1)
        _, vjp = jax.vjp(f_relu2, hpre_ref[...], bias_ref[...])
        dhpre, db = vjp((dh,))
        d_ref[...] = dhpre.astype(d_ref.dtype)

        @pl.when(i == 0)
        def _():
            db_ref[...] = db

        @pl.when(i > 0)
        def _():
            db_ref[...] += db

    tile = pl.BlockSpec((tm, tn), lambda j, i: (i, j))
    col = pl.BlockSpec((1, tn), lambda j, i: (0, j))
    return pl.pallas_call(
        body, name=name, grid=(n // tn, m // tm),
        in_specs=[pl.BlockSpec((tm, k), lambda j, i: (i, 0)), pl.BlockSpec((tn, k), lambda j, i: (j, 0)), tile, col],
        out_specs=[tile, col],
        out_shape=[jax.ShapeDtypeStruct((m, n), BF16), jax.ShapeDtypeStruct((1, n), F32)],
        compiler_params=pltpu.CompilerParams(dimension_semantics=("parallel", "arbitrary"), vmem_limit_bytes=VMEM_LIMIT),
    )(dff, w2, hpre, b1)


POOL_PAD = 16
POOL_ROWS = 256


def pool_lin(name, x, wvec, transpose, out_dtype):
    arr, width, cb = x
    s = arr.shape[0]
    n_steps = s // POOL_ROWS

    def body(x_ref, w_ref, o_ref, xp_ref):
        wv = w_ref[...]
        zeros = jnp.zeros((POOL_PAD, width), F32)
        xp_ref[0:POOL_PAD, :] = zeros
        xp_ref[s + POOL_PAD:s + 2 * POOL_PAD, :] = zeros

        def count(t0):
            t = lax.broadcasted_iota(jnp.int32, (POOL_ROWS, width), 0) + (t0 + 1)
            return jnp.minimum(t.astype(F32), wv)

        def fill(i, carry):
            t0 = pl.multiple_of(i * POOL_ROWS, POOL_ROWS)
            v = x_ref[pl.ds(t0, POOL_ROWS), :].astype(F32)
            if transpose:
                v = v / count(t0)
            xp_ref[pl.ds(t0 + POOL_PAD, POOL_ROWS), :] = v
            return carry

        lax.fori_loop(0, n_steps, fill, 0)

        def step(i, carry):
            t0 = pl.multiple_of(i * POOL_ROWS, POOL_ROWS)
            win = xp_ref[pl.ds(t0, POOL_ROWS + 2 * POOL_PAD), :]
            acc = jnp.zeros((POOL_ROWS, width), F32)
            for j in range(POOL_PAD):
                off = POOL_PAD + j if transpose else POOL_PAD - j
                acc = acc + jnp.where(wv > j, win[off:off + POOL_ROWS, :], 0.0)
            cur = x_ref[pl.ds(t0, POOL_ROWS), :].astype(F32)
            res = acc - cur if transpose else acc / count(t0) - cur
            o_ref[pl.ds(t0, POOL_ROWS), :] = res.astype(o_ref.dtype)
            return carry

        lax.fori_loop(0, n_steps, step, 0)

    return pl.pallas_call(
        body, name=name, grid=(1,),
        in_specs=[pl.BlockSpec((s, width), lambda i, cb=cb: (0, cb)), pl.BlockSpec((1, width), lambda i: (0, 0))],
        out_specs=pl.BlockSpec((s, width), lambda i: (0, 0)),
        out_shape=jax.ShapeDtypeStruct((s, width), out_dtype),
        scratch_shapes=[pltpu.VMEM((s + 2 * POOL_PAD, width), F32)],
        compiler_params=pltpu.CompilerParams(dimension_semantics=("arbitrary",), vmem_limit_bytes=VMEM_LIMIT),
    )(arr, wvec)


CONV_PAD = 32
CONV_ROWS = 128


def _glu(a, g):
    return a * jax.nn.sigmoid(g)


def conv_fwd(name, zc, w):
    arr, width, cb = zc
    s = arr.shape[0]
    n_steps = s // CONV_ROWS
    lead = CONV_PAD - (CONV_WIDTH - 1)

    def body(z_ref, w_ref, o_ref, hp_ref):
        hp_ref[0:CONV_PAD, :] = jnp.zeros((CONV_PAD, D_CONV), F32)

        def fill(i, carry):
            t0 = pl.multiple_of(i * CONV_ROWS, CONV_ROWS)
            z = z_ref[pl.ds(t0, CONV_ROWS), :]
            hp_ref[pl.ds(t0 + CONV_PAD, CONV_ROWS), :] = _glu(z[:, :D_CONV], z[:, D_CONV:])
            return carry

        lax.fori_loop(0, n_steps, fill, 0)
        wv = w_ref[...]

        def step(i, carry):
            t0 = pl.multiple_of(i * CONV_ROWS, CONV_ROWS)
            win = hp_ref[pl.ds(t0, CONV_ROWS + CONV_PAD), :]
            acc = jnp.zeros((CONV_ROWS, D_CONV), F32)
            for k in range(CONV_WIDTH):
                acc = acc + wv[k:k + 1, :] * win[lead + k:lead + k + CONV_ROWS, :]
            o_ref[pl.ds(t0, CONV_ROWS), :] = acc
            return carry

        lax.fori_loop(0, n_steps, step, 0)

    return pl.pallas_call(
        body, name=name, grid=(1,),
        in_specs=[pl.BlockSpec((s, width), lambda i, cb=cb: (0, cb)), pl.BlockSpec(w.shape, lambda i: (0, 0))],
        out_specs=pl.BlockSpec((s, D_CONV), lambda i: (0, 0)),
        out_shape=jax.ShapeDtypeStruct((s, D_CONV), F32),
        scratch_shapes=[pltpu.VMEM((s + CONV_PAD, D_CONV), F32)],
        compiler_params=pltpu.CompilerParams(dimension_semantics=("arbitrary",), vmem_limit_bytes=VMEM_LIMIT),
    )(arr, w)


def conv_bwd(name, zc, w, dout):
    arr, width, cb = zc
    s = arr.shape[0]
    n_steps = s // CONV_ROWS
    lead = CONV_PAD - (CONV_WIDTH - 1)

    def body(z_ref, w_ref, d_ref, dz_ref, dw_ref, hp_ref, dp_ref):
        hp_ref[0:CONV_PAD, :] = jnp.zeros((CONV_PAD, D_CONV), F32)
        dp_ref[s:s + CONV_PAD, :] = jnp.zeros((CONV_PAD, D_CONV), F32)
        dw_ref[...] = jnp.zeros(dw_ref.shape, F32)

        def fill(i, carry):
            t0 = pl.multiple_of(i * CONV_ROWS, CONV_ROWS)
            z = z_ref[pl.ds(t0, CONV_ROWS), :]
            hp_ref[pl.ds(t0 + CONV_PAD, CONV_ROWS), :] = _glu(z[:, :D_CONV], z[:, D_CONV:])
            dp_ref[pl.ds(t0, CONV_ROWS), :] = d_ref[pl.ds(t0, CONV_ROWS), :]
            return carry

        lax.fori_loop(0, n_steps, fill, 0)
        wv = w_ref[...]

        def step(i, carry):
            t0 = pl.multiple_of(i * CONV_ROWS, CONV_ROWS)
            hwin = hp_ref[pl.ds(t0, CONV_ROWS + CONV_PAD), :]
            dwin = dp_ref[pl.ds(t0, CONV_ROWS + CONV_PAD), :]
            dcur = dwin[0:CONV_ROWS, :]
            dh = jnp.zeros((CONV_ROWS, D_CONV), F32)
            rows = []
            for k in range(CONV_WIDTH):
                rows.append(jnp.sum(dcur * hwin[lead + k:lead + k + CONV_ROWS, :], axis=0, keepdims=True))
                back = CONV_WIDTH - 1 - k
                dh = dh + wv[k:k + 1, :] * dwin[back:back + CONV_ROWS, :]
            rows.append(jnp.zeros((1, D_CONV), F32))
            dw_ref[...] += jnp.concatenate(rows, axis=0)
            z = z_ref[pl.ds(t0, CONV_ROWS), :]
            _, vjp = jax.vjp(_glu, z[:, :D_CONV], z[:, D_CONV:])
            da, dg = vjp(dh)
            dz_ref[pl.ds(t0, CONV_ROWS), :] = jnp.concatenate([da, dg], axis=1).astype(dz_ref.dtype)
            return carry

        lax.fori_loop(0, n_steps, step, 0)

    return pl.pallas_call(
        body, name=name, grid=(1,),
        in_specs=[pl.BlockSpec((s, width), lambda i, cb=cb: (0, cb)), pl.BlockSpec(w.shape, lambda i: (0, 0)),
                  pl.BlockSpec((s, D_CONV), lambda i: (0, 0))],
        out_specs=[pl.BlockSpec((s, width), lambda i: (0, 0)), pl.BlockSpec(w.shape, lambda i: (0, 0))],
        out_shape=[jax.ShapeDtypeStruct((s, width), BF16), jax.ShapeDtypeStruct(w.shape, F32)],
        scratch_shapes=[pltpu.VMEM((s + CONV_PAD, D_CONV), F32), pltpu.VMEM((s + CONV_PAD, D_CONV), F32)],
        compiler_params=pltpu.CompilerParams(dimension_semantics=("arbitrary",), vmem_limit_bytes=VMEM_LIMIT),
    )(arr, w, dout)


HEADS_PER_STEP = LANE // HEAD_DIM
CHUNKS_PER_TILE = ATT_TILE // CHUNK
KEY_BLOCKS = N_PREV * CHUNK // ATT_TILE + 1
KEY_SPAN = KEY_BLOCKS * ATT_TILE


def _attn_tile(q, *rest, missing_cols):
    kcat = jnp.concatenate(rest[:KEY_BLOCKS], axis=0)
    vcat = jnp.concatenate(rest[KEY_BLOCKS:2 * KEY_BLOCKS], axis=0)
    bias = rest[2 * KEY_BLOCKS]
    lane = lax.broadcasted_iota(jnp.int32, (1, LANE), 1)
    col = lax.broadcasted_iota(jnp.int32, (1, KEY_SPAN), 1)
    missing = col < missing_cols
    qs = q * (HEAD_DIM ** -0.5)
    o = jnp.zeros((ATT_TILE, LANE), F32)
    for h in range(HEADS_PER_STEP):
        in_head = jnp.logical_and(lane >= h * HEAD_DIM, lane < (h + 1) * HEAD_DIM)
        sc = mm_nt(jnp.where(in_head, qs, 0.0), kcat) + bias[h]
        sc = jnp.where(missing, NEG_INF, sc)
        m = jnp.max(sc, axis=-1, keepdims=True)
        e = jnp.exp(sc - lax.stop_gradient(m))
        p = e / jnp.sum(e, axis=-1, keepdims=True)
        o = o + jnp.where(in_head, mm_nn(p, vcat), 0.0)
    return o


def _missing_cols(n):
    return jnp.maximum((KEY_BLOCKS - 1 - n) * ATT_TILE, 0)


def _attn_in_specs(nt):
    def spec(col0, back):
        return pl.BlockSpec((ATT_TILE, LANE),
                            lambda hp, n, col0=col0, back=back: (jnp.clip(n - back, 0, nt - 1), col0 // LANE + hp))
    backs = list(range(KEY_BLOCKS - 1, -1, -1))
    return ([spec(Z_Q, 0)] + [spec(Z_K, b) for b in backs] + [spec(Z_V, b) for b in backs]
            + [pl.BlockSpec((HEADS_PER_STEP, ATT_TILE, KEY_SPAN), lambda hp, n: (hp, 0, 0))])


def attn_fwd(name, z, bias):
    s = z.shape[0]
    nt = s // ATT_TILE
    n_in = 2 + 2 * KEY_BLOCKS

    def body(*refs):
        o_ref = refs[n_in]
        vals = [r[...] for r in refs[:n_in]]
        o = _attn_tile(*vals, missing_cols=_missing_cols(pl.program_id(1)))
        o_ref[...] = o.astype(o_ref.dtype)

    return pl.pallas_call(
        body, name=name, grid=(N_HEADS // HEADS_PER_STEP, nt),
        in_specs=_attn_in_specs(nt),
        out_specs=pl.BlockSpec((ATT_TILE, LANE), lambda hp, n: (n, hp)),
        out_shape=jax.ShapeDtypeStruct((s, D_ATTN), BF16),
        compiler_params=pltpu.CompilerParams(dimension_semantics=("parallel", "parallel"), vmem_limit_bytes=VMEM_LIMIT),
    )(*([z] * (n_in - 1)), bias)


def attn_bwd(name, z, bias, do, after=None):
    s = z.shape[0]
    nt = s // ATT_TILE
    n_in = 2 + 2 * KEY_BLOCKS
    nc = KEY_BLOCKS - 1

    deps = [] if after is None else [after]
    n_out = n_in + 1 + len(deps)

    def body(*refs):
        do_ref = refs[n_in]
        dq_ref, dk_ref, dv_ref, db_ref = refs[n_out:n_out + 4]
        kacc, vacc = refs[n_out + 4:n_out + 4 + nc], refs[n_out + 4 + nc:]
        n = pl.program_id(1)

        @pl.when(n == 0)
        def _():
            db_ref[...] = jnp.zeros(db_ref.shape, F32)
            for acc in (*kacc, *vacc):
                acc[...] = jnp.zeros(acc.shape, F32)

        def shift(out_ref, accs, contrib):
            @pl.when(n >= nc)
            def _():
                first = accs[0][...] if contrib is None else accs[0][...] + contrib[0]
                out_ref[...] = first.astype(out_ref.dtype)
            for j in range(nc - 1):
                accs[j][...] = accs[j + 1][...] if contrib is None else accs[j + 1][...] + contrib[j + 1]
            if contrib is not None:
                accs[nc - 1][...] = contrib[nc]

        @pl.when(n < nt)
        def _():
            fn = functools.partial(_attn_tile, missing_cols=_missing_cols(n))
            _, vjp = jax.vjp(fn, *[r[...] for r in refs[:n_in]])
            grads = vjp(do_ref[...].astype(F32))
            dq_ref[...] = grads[0].astype(dq_ref.dtype)
            db_ref[...] += grads[n_in - 1]
            shift(dk_ref, kacc, grads[1:1 + KEY_BLOCKS])
            shift(dv_ref, vacc, grads[1 + KEY_BLOCKS:1 + 2 * KEY_BLOCKS])

        @pl.when(n >= nt)
        def _():
            shift(dk_ref, kacc, None)
            shift(dv_ref, vacc, None)

    o_cur = pl.BlockSpec((ATT_TILE, LANE), lambda hp, n: (jnp.minimum(n, nt - 1), hp))
    o_old = pl.BlockSpec((ATT_TILE, LANE), lambda hp, n: (jnp.maximum(n - nc, 0), hp))
    b_spec = pl.BlockSpec((HEADS_PER_STEP, ATT_TILE, KEY_SPAN), lambda hp, n: (hp, 0, 0))
    return pl.pallas_call(
        body, name=name, grid=(N_HEADS // HEADS_PER_STEP, nt + nc),
        in_specs=_attn_in_specs(nt) + [o_cur] + [pl.BlockSpec(d.shape, lambda hp, n: (0, 0)) for d in deps],
        out_specs=[o_cur, o_old, o_old, b_spec],
        out_shape=[jax.ShapeDtypeStruct((s, D_ATTN), BF16)] * 3 + [jax.ShapeDtypeStruct((N_HEADS, ATT_TILE, KEY_SPAN), F32)],
        scratch_shapes=[pltpu.VMEM((ATT_TILE, LANE), F32)] * (2 * nc),
        compiler_params=pltpu.CompilerParams(dimension_semantics=("parallel", "arbitrary"), vmem_limit_bytes=VMEM_LIMIT),
    )(*([z] * (n_in - 1)), bias, do, *deps)


def loss_head(name, y, tgt, tm):
    s, d = y.shape

    def body(y_ref, t_ref, l_ref, dy_ref):
        i = pl.program_id(0)
        diff = y_ref[...] - t_ref[...]
        dy_ref[...] = diff * (1.0 / d)
        part = 0.5 * jnp.sum(jnp.mean(diff * diff, axis=-1, keepdims=True), axis=0, keepdims=True)

        @pl.when(i == 0)
        def _():
            l_ref[...] = jnp.zeros(l_ref.shape, F32)

        l_ref[...] += jnp.broadcast_to(part, l_ref.shape)

    row = pl.BlockSpec((tm, d), lambda i: (i, 0))
    return pl.pallas_call(
        body, name=name, grid=(s // tm,), in_specs=[row, row],
        out_specs=[pl.BlockSpec((8, LANE), lambda i: (0, 0)), row],
        out_shape=[jax.ShapeDtypeStruct((8, LANE), F32), jax.ShapeDtypeStruct((s, d), F32)],
        compiler_params=pltpu.CompilerParams(dimension_semantics=("arbitrary",), vmem_limit_bytes=VMEM_LIMIT),
    )(y, tgt)


def adamw(name, w, g, m, v, tr):
    r, c = w.shape
    assert r % tr == 0, (name, w.shape, tr)

    def body(w_ref, g_ref, m_ref, v_ref, d_ref, nm_ref, nv_ref):
        gg = g_ref[...]
        m2 = ADAM_B1 * m_ref[...] + (1.0 - ADAM_B1) * gg
        v2 = ADAM_B2 * v_ref[...] + (1.0 - ADAM_B2) * (gg * gg)
        m_hat = m2 / (1.0 - ADAM_B1 ** ADAM_STEP)
        v_hat = v2 / (1.0 - ADAM_B2 ** ADAM_STEP)
        d_ref[...] = -ADAM_LR * (m_hat / (jnp.sqrt(v_hat) + ADAM_EPS) + ADAM_WD * w_ref[...])
        nm_ref[...] = m2
        nv_ref[...] = v2

    blk = pl.BlockSpec((tr, c), lambda i: (i, 0))
    return pl.pallas_call(
        body, name=name, grid=(r // tr,), in_specs=[blk] * 4, out_specs=[blk] * 3,
        out_shape=[jax.ShapeDtypeStruct((r, c), F32)] * 3,
        compiler_params=pltpu.CompilerParams(dimension_semantics=("parallel",), vmem_limit_bytes=VMEM_LIMIT),
    )(w, g, m, v)


def adamw_sum(name, w, layer_blocks, m, v, tr):
    rows, c = w.shape
    nl = len(layer_blocks)
    nb, r, _ = layer_blocks[0].shape
    assert rows == nl * r and r % tr == 0, (name, w.shape, layer_blocks[0].shape, tr)
    per = r // tr

    def body(*refs):
        w_ref, b_refs, (m_ref, v_ref, g_ref, d_ref, nm_ref, nv_ref) = refs[0], refs[1:1 + nl], refs[1 + nl:]
        i = pl.program_id(0)

        def update(b_ref):
            gg = b_ref[0].astype(F32)
            for j in range(1, nb):
                gg = gg + b_ref[j].astype(F32)
            g_ref[...] = gg
            m2 = ADAM_B1 * m_ref[...] + (1.0 - ADAM_B1) * gg
            v2 = ADAM_B2 * v_ref[...] + (1.0 - ADAM_B2) * (gg * gg)
            m_hat = m2 / (1.0 - ADAM_B1 ** ADAM_STEP)
            v_hat = v2 / (1.0 - ADAM_B2 ** ADAM_STEP)
            d_ref[...] = -ADAM_LR * (m_hat / (jnp.sqrt(v_hat) + ADAM_EPS) + ADAM_WD * w_ref[...])
            nm_ref[...] = m2
            nv_ref[...] = v2

        for l in range(nl):
            pl.when(jnp.logical_and(i >= l * per, i < (l + 1) * per))(functools.partial(update, b_refs[l]))

    blk = pl.BlockSpec((tr, c), lambda i: (i, 0))
    b_specs = [pl.BlockSpec((nb, tr, c), lambda i, l=l: (0, jnp.clip(i - l * per, 0, per - 1), 0)) for l in range(nl)]
    return pl.pallas_call(
        body, name=name, grid=(rows // tr,),
        in_specs=[blk] + b_specs + [blk, blk], out_specs=[blk] * 4,
        out_shape=[jax.ShapeDtypeStruct((rows, c), F32)] * 4,
        compiler_params=pltpu.CompilerParams(dimension_semantics=("arbitrary",), vmem_limit_bytes=VMEM_LIMIT),
    )(w, *layer_blocks, m, v)


def sum_blocks(name, blocks, tr):
    nb, r, c = blocks.shape
    assert r % tr == 0, (name, blocks.shape, tr)

    def body(b_ref, o_ref):
        acc = b_ref[0].astype(F32)
        for j in range(1, nb):
            acc = acc + b_ref[j].astype(F32)
        o_ref[...] = acc

    return pl.pallas_call(
        body, name=name, grid=(r // tr,),
        in_specs=[pl.BlockSpec((nb, tr, c), lambda i: (0, i, 0))],
        out_specs=pl.BlockSpec((tr, c), lambda i: (i, 0)),
        out_shape=jax.ShapeDtypeStruct((r, c), F32),
        compiler_params=pltpu.CompilerParams(dimension_semantics=("parallel",), vmem_limit_bytes=VMEM_LIMIT),
    )(blocks)


FLIPS = [(0, 0, 1), (1, 0, 0), (0, 1, 0), (1, 1, 0), (1, 0, 1), (0, 1, 1), (1, 1, 1)]
ANY = pl.BlockSpec(memory_space=pl.ANY)


def _me():
    return lax.axis_index("x"), lax.axis_index("y"), lax.axis_index("c")


def _flip(pos, f):
    return tuple((1 - p) if fi else p for p, fi in zip(pos, f))


def _idx(pos):
    return 4 * pos[0] + 2 * pos[1] + pos[2]


def all_gather_multi(name, shards):
    n = len(shards)

    def body(*refs):
        x_refs, out_refs, token = refs[:n], refs[n:2 * n], refs[2 * n]
        send_sems, recv_sems, local_sems = refs[2 * n + 1:]
        token[...] = jnp.zeros_like(token)
        x, y, cc = _me()
        me, sibling = (x, y, cc), (x, y, 1 - cc)
        chips = [(1 - x, y), (x, 1 - y), (1 - x, 1 - y)]

        def copy(a, k, block, to, src=None):
            dst = out_refs[a].at[_idx(block)]
            return pltpu.make_async_remote_copy(
                src_ref=dst if src is None else src, dst_ref=dst, send_sem=send_sems.at[7 * a + k],
                recv_sem=recv_sems.at[7 * a + k], device_id=to, device_id_type=MESH)

        mine = [pltpu.make_async_copy(x_refs[a], out_refs[a].at[_idx(me)], local_sems.at[a]) for a in range(n)]
        for cp in mine:
            cp.start()
        first = []
        for a in range(n):
            first.append(copy(a, 0, me, sibling, src=x_refs[a]))
            first += [copy(a, 1 + j, me, (*chip, cc), src=x_refs[a]) for j, chip in enumerate(chips)]
        for cp in first:
            cp.start()
        passed = []
        for j, chip in enumerate(chips):
            for a in range(n):
                copy(a, 1 + j, (*chip, cc), me).wait_recv()
                fwd = copy(a, 4 + j, (*chip, cc), sibling)
                fwd.start()
                passed.append(fwd)
        for a in range(n):
            copy(a, 0, sibling, me).wait_recv()
            for j, chip in enumerate(chips):
                copy(a, 4 + j, (*chip, 1 - cc), me).wait_recv()
        for cp in first + passed:
            cp.wait_send()
        for cp in mine:
            cp.wait()

    return pl.pallas_call(
        body, name=name, in_specs=[ANY] * n, out_specs=[ANY] * n + [pl.BlockSpec(memory_space=pltpu.VMEM)],
        out_shape=[jax.ShapeDtypeStruct((N_DEV,) + a.shape, a.dtype) for a in shards]
        + [jax.ShapeDtypeStruct((8, LANE), F32)],
        scratch_shapes=[pltpu.SemaphoreType.DMA((7 * n,)), pltpu.SemaphoreType.DMA((7 * n,)),
                        pltpu.SemaphoreType.DMA((n,))],
    )(*shards)


def all_to_all_multi(name, blocks):
    n = len(blocks)

    def body(*refs):
        in_refs, out_refs = refs[:n], refs[n:2 * n]
        send_sems, recv_sems, local_sems = refs[2 * n:]
        me = _me()
        mi = _idx(me)
        mine = [pltpu.make_async_copy(in_refs[a].at[mi], out_refs[a].at[mi], local_sems.at[a]) for a in range(n)]
        for cp in mine:
            cp.start()
        sends, recvs = [], []
        for k, f in enumerate(FLIPS):
            peer = _flip(me, f)
            pi = _idx(peer)
            for a in range(n):
                sems = dict(send_sem=send_sems.at[7 * a + k], recv_sem=recv_sems.at[7 * a + k],
                            device_id=peer, device_id_type=MESH)
                sends.append(pltpu.make_async_remote_copy(src_ref=in_refs[a].at[pi], dst_ref=out_refs[a].at[mi], **sems))
                recvs.append(pltpu.make_async_remote_copy(src_ref=in_refs[a].at[mi], dst_ref=out_refs[a].at[pi], **sems))
        for cp in sends:
            cp.start()
        for cp in recvs:
            cp.wait_recv()
        for cp in sends:
            cp.wait_send()
        for cp in mine:
            cp.wait()

    return pl.pallas_call(
        body, name=name, in_specs=[ANY] * n, out_specs=[ANY] * n,
        out_shape=[jax.ShapeDtypeStruct(a.shape, a.dtype) for a in blocks],
        scratch_shapes=[pltpu.SemaphoreType.DMA((7 * n,)), pltpu.SemaphoreType.DMA((7 * n,)),
                        pltpu.SemaphoreType.DMA((n,))],
    )(*blocks)


HBM = pl.BlockSpec(memory_space=pltpu.HBM)
SEM = pl.BlockSpec(memory_space=pltpu.SEMAPHORE)
DATAFLOW = pltpu.SideEffectType.DATAFLOW_SIDE_EFFECTING


def _exchange_copies(a_refs, l_refs, send_sems, recv_sems, gather):
    me = _me()
    mi = _idx(me)
    out = []
    for k, f in enumerate(FLIPS):
        peer = _flip(me, f)
        for a in range(len(a_refs)):
            src = a_refs[a] if gather else a_refs[a].at[_idx(peer)]
            out.append(pltpu.make_async_remote_copy(
                src_ref=src, dst_ref=l_refs[a].at[mi], send_sem=send_sems.at[7 * a + k],
                recv_sem=recv_sems.at[7 * a + k], device_id=peer, device_id_type=MESH))
    return out


def exchange_start(name, arrays, gather):
    n = len(arrays)
    lands = [lax.empty(((N_DEV,) + a.shape) if gather else a.shape, a.dtype) for a in arrays]

    def body(*refs):
        a_refs, l_refs = refs[:n], refs[n:2 * n]
        send_sems, recv_sems = refs[2 * n], refs[2 * n + 1]
        token = refs[4 * n + 2]
        for cp in _exchange_copies(a_refs, l_refs, send_sems, recv_sems, gather):
            cp.start()
        token[...] = jnp.zeros_like(token)

    hbm = lambda a: pltpu.HBM(a.shape, a.dtype)
    res = pl.pallas_call(
        body, name=name,
        out_shape=(pltpu.SemaphoreType.DMA((7 * n,)), pltpu.SemaphoreType.DMA((7 * n,)),
                   *[hbm(a) for a in arrays], *[hbm(a) for a in lands], jax.ShapeDtypeStruct((8, LANE), F32)),
        in_specs=[HBM] * (2 * n),
        out_specs=(SEM, SEM, *[HBM] * (2 * n), pl.BlockSpec(memory_space=pltpu.VMEM)),
        input_output_aliases={i: i + 2 for i in range(2 * n)},
        compiler_params=pltpu.CompilerParams(has_side_effects=DATAFLOW),
    )(*[pltpu.with_memory_space_constraint(a, pltpu.HBM) for a in arrays],
      *[pltpu.with_memory_space_constraint(a, pltpu.HBM) for a in lands])
    return res[0], res[1], list(res[2:2 + n]), list(res[2 + n:2 + 2 * n]), res[-1]


def exchange_wait(name, send_sems, recv_sems, arrays, lands, after, gather):
    n = len(arrays)

    def body(*refs):
        a_refs, l_refs = refs[:n], refs[n:2 * n]
        ssem, rsem = refs[2 * n], refs[2 * n + 1]
        for cp in _exchange_copies(a_refs, l_refs, ssem, rsem, gather):
            cp.wait_send()
            cp.wait_recv()

    hbm = lambda a: pltpu.HBM(a.shape, a.dtype)
    res = pl.pallas_call(
        body, name=name,
        out_shape=(*[hbm(a) for a in arrays], *[hbm(a) for a in lands]),
        in_specs=[HBM] * (2 * n) + [SEM, SEM, pl.BlockSpec(memory_space=pl.ANY)],
        out_specs=tuple([HBM] * (2 * n)),
        input_output_aliases={i: i for i in range(2 * n)},
        compiler_params=pltpu.CompilerParams(has_side_effects=DATAFLOW),
    )(*arrays, *lands, send_sems, recv_sems, after)
    return list(res[n:])


def ada_fwd(name, c_row, w_cat, b_lay):
    d = c_row.shape[1]
    ncol = w_cat.shape[1]
    vmem = pl.BlockSpec(memory_space=pltpu.VMEM)

    def body(c_ref, w_ref, b_ref, mod_ref, cact_ref, call, send, land, s1, r1, s2, r2):
        me = _me()
        mi = _idx(me)
        call[mi] = c_ref[...]

        def exchange(src_of, dst_buf, ssem, rsem):
            sends, recvs = [], []
            for k, f in enumerate(FLIPS):
                peer = _flip(me, f)
                sends.append(pltpu.make_async_remote_copy(
                    src_ref=src_of(peer), dst_ref=dst_buf.at[mi], send_sem=ssem.at[k], recv_sem=rsem.at[k],
                    device_id=peer, device_id_type=MESH))
                recvs.append(pltpu.make_async_remote_copy(
                    src_ref=src_of(peer), dst_ref=dst_buf.at[_idx(peer)], send_sem=ssem.at[k], recv_sem=rsem.at[k],
                    device_id=peer, device_id_type=MESH))
            for cp in sends:
                cp.start()
            for cp in recvs:
                cp.wait_recv()
            for cp in sends:
                cp.wait_send()

        exchange(lambda peer: c_ref, call, s1, r1)
        for p in range(N_DEV):
            cact_ref[pl.ds(p, 1), :] = jax.nn.silu(call[p])
        res = _dg(cact_ref[...], w_ref[...], 1, 0)
        for p in range(N_DEV):
            send[p] = res[p:p + 1, :]
        land[mi] = send[mi]
        exchange(lambda peer: send.at[_idx(peer)], land, s2, r2)
        mod_ref[...] = land[...] + b_ref[...]

    return pl.pallas_call(
        body, name=name, in_specs=[vmem, vmem, vmem], out_specs=[vmem, vmem],
        out_shape=[jax.ShapeDtypeStruct((N_DEV, 1, ncol), F32), jax.ShapeDtypeStruct((N_DEV, d), F32)],
        scratch_shapes=[pltpu.VMEM((N_DEV, 1, d), F32), pltpu.VMEM((N_DEV, 1, ncol), F32),
                        pltpu.VMEM((N_DEV, 1, ncol), F32),
                        pltpu.SemaphoreType.DMA((7,)), pltpu.SemaphoreType.DMA((7,)),
                        pltpu.SemaphoreType.DMA((7,)), pltpu.SemaphoreType.DMA((7,))],
        compiler_params=pltpu.CompilerParams(vmem_limit_bytes=VMEM_LIMIT),
    )(c_row, w_cat, b_lay)


POOL_WINDOWS = (2, 4, 8, 16)
POOL_GROUP = 64
N_REL = 2 * REL_CLIP + 1
PACK_COLS = 1024
SMALL_NAMES = ["b_ada", "b_gate", "w_pool", "pool_scale", "rel_bias", "conv_w", "conv_b", "conv_ln_g",
               "conv_ln_b", "ln_mix_g", "ln_mix_b", "b_ff1", "b_ff2", "ln_ff_g", "ln_ff_b"]
BIG_NAMES = ["w_in", "w_br_pool", "w_br_attn", "w_br_conv", "w_o", "w_ff1", "w_ff2"]
ROW_SHARDED = ("w_o", "w_ff2")
WEIGHT_NAMES = ["w_ada", "b_ada", "w_in", "b_gate", "w_pool", "pool_scale", "rel_bias", "conv_w", "conv_b",
                "conv_ln_g", "conv_ln_b", "w_br_pool", "w_br_attn", "w_br_conv", "w_o", "ln_mix_g", "ln_mix_b",
                "w_ff1", "b_ff1", "w_ff2", "b_ff2", "ln_ff_g", "ln_ff_b"]


def _perm_cols(w):
    return jnp.concatenate([w[:, 2304:], w[:, 1792:2304], w[:, :256], w[:, 256:768], w[:, 768:1280],
                            w[:, 1280:1792]], axis=1)


def _unperm_cols(wp):
    return jnp.concatenate([wp[:, Z_POOL:Z_Q], wp[:, Z_Q:Z_K], wp[:, Z_K:Z_V], wp[:, Z_V:],
                            wp[:, Z_CONV:Z_POOL], wp[:, :Z_CONV]], axis=1)


def _bias_table(rel_bias):
    far = jnp.broadcast_to(rel_bias[:, 2 * REL_CLIP:], (N_HEADS, BAND - REL_CLIP))
    near = rel_bias[:, REL_CLIP - CHUNK + 1:2 * REL_CLIP][:, ::-1]
    ext = jnp.concatenate([far, near], axis=1)
    return jnp.stack([ext[:, CHUNK - 1 - qi:CHUNK - 1 - qi + BAND] for qi in range(CHUNK)], axis=1)


def _bias_full(rel_bias):
    tab = _bias_table(rel_bias)
    return jnp.concatenate(
        [jnp.pad(tab, ((0, 0), (0, 0), (i * CHUNK, KEY_SPAN - BAND - i * CHUNK)), constant_values=NEG_INF)
         for i in range(CHUNKS_PER_TILE)], axis=1)


def _block_diag(w_pool):
    out = jnp.zeros((D_POOL, D_POOL), F32)
    for g in range(len(POOL_WINDOWS)):
        out = lax.dynamic_update_slice(out, w_pool[g], (g * POOL_GROUP, g * POOL_GROUP))
    return out


def _flat_pad(arrs, mult):
    flat = jnp.concatenate([a.reshape(-1) for a in arrs])
    pad = (-flat.shape[0]) % mult
    return jnp.pad(flat, (0, pad)) if pad else flat


def _unflat(flat, shapes):
    out, off = [], 0
    for shp in shapes:
        n = int(np.prod(shp))
        out.append(flat[off:off + n].reshape(shp))
        off += n
    return out


def _to_blocks(name, full):
    k, n = full.shape
    if name in ROW_SHARDED:
        return full.reshape(N_DEV, k // N_DEV, n)
    return full.reshape(k, N_DEV, n // N_DEV).transpose(1, 0, 2)


def _from_blocks(name, blocks):
    nb, r, c = blocks.shape
    if name in ROW_SHARDED:
        return blocks.reshape(nb * r, c)
    return blocks.transpose(1, 0, 2).reshape(r, nb * c)


class _Layer:
    pass


def _row(v):
    return v.reshape(1, -1)


def _layer_fwd(x, modr, w, wvec, fetch_rest):
    sh_m, sc_m, g_m, sh_f, sc_f, g_f = modr
    (u,) = row_fwd("lnmod_mix", f_lnmod, [x], [sc_m, sh_m], [(D_MODEL, BF16)], 512)
    z = mm_big("mm_in", u, w.w_in, "nn", (512, 896, 1024), F32, j_outer=True)
    p = pool_lin("pool_fwd", (z, D_POOL, Z_POOL // D_POOL), wvec, False, F32)
    ao = attn_fwd("attn_fwd", z, w.bias)
    cv = conv_fwd("conv_fwd", (z, 2 * D_CONV, Z_CONV // (2 * D_CONV)), w.conv_w)
    fetch_rest(w, cv)
    mparams = [w.wbd, w.ps, w.wbp, w.wba, w.wbc, w.cb, w.clg, w.clb, w.bg, w.wo, g_m, w.lmg, w.lmb]
    (x1,) = row_fwd("merge", f_merge, [p, ao, cv, (z, 3 * D_MODEL, 0), x], mparams, [(D_MODEL, F32)], 256)
    (u2,) = row_fwd("lnmod_ff", f_lnmod, [x1], [sc_f, sh_f], [(D_MODEL, BF16)], 512)
    hpre, h = mm_ff1_relu2("mm_ff1", u2, w.w_ff1, w.b1, 512, 1024)
    ff = mm_big("mm_ff2", h, w.w_ff2, "nn", (512, 1024, 4096), F32)
    (x2,) = row_fwd("ffout", f_ffout, [x1, ff], [w.b2, g_f, w.lfg, w.lfb], [(D_MODEL, F32)], 512)
    return x2, (x, u, z, p, ao, cv, x1, u2, hpre, h, ff, mparams)


GRAD_GROUPS = [("ff", ["w_ff2", "w_ff1"]), ("mix", ["w_o", "w_br_pool", "w_br_attn", "w_br_conv"]), ("in", ["w_in"])]


def _layer_bwd(dx2, saved, modr, w, wvec, ready):
    x, u, z, p, ao, cv, x1, u2, hpre, h, ff, mparams = saved
    sh_m, sc_m, g_m, sh_f, sc_f, g_f = modr
    g = {}
    (dx1a, dff), (g["b_ff2"], dgf, g["ln_ff_g"], g["ln_ff_b"]) = row_bwd(
        "ffout_bwd", f_ffout, [x1, ff], [w.b2, g_f, w.lfg, w.lfb], [dx2], 512, [(0, F32), (1, BF16)], [0, 1, 2, 3])
    dhpre, g["b_ff1"] = mm_dh_relu2("mm_dh", dff, w.w_ff2, hpre, w.b1, 512, 1024)
    g["w_ff2"] = mm_big("mm_dw_ff2", h, dff, "tn", (1024, 1024, 2048), BF16)
    du2 = mm_big("mm_du2", dhpre, w.w_ff1, "nt", (512, 1024, 4096), F32)
    g["w_ff1"] = mm_big("mm_dw_ff1", u2, dhpre, "tn", (1024, 1024, 2048), BF16)
    sc_f = sc_f + ready("ff", g)[0:1, 0:1]
    (dx1,), (dscf, dshf) = row_bwd("lnmod_ff_bwd", f_lnmod, [x1], [sc_f, sh_f], [du2], 512, [(0, F32)], [0, 1],
                                   add_to=dx1a)
    (dp, dao, dcv, dzg, dxa), dm = row_bwd(
        "merge_bwd", f_merge, [p, ao, cv, (z, 3 * D_MODEL, 0), x], mparams, [dx1], 256,
        [(0, F32), (1, BF16), (2, F32), (3, BF16), (4, F32)], list(range(13)))
    (dwbd, g["pool_scale"], g["w_br_pool"], g["w_br_attn"], g["w_br_conv"], g["conv_b"], g["conv_ln_g"],
     g["conv_ln_b"], g["b_gate"], g["w_o"], dgm, g["ln_mix_g"], g["ln_mix_b"]) = dm
    g["w_pool"] = jnp.stack([dwbd[i * POOL_GROUP:(i + 1) * POOL_GROUP, i * POOL_GROUP:(i + 1) * POOL_GROUP]
                             for i in range(len(POOL_WINDOWS))])
    tok = ready("mix", g)
    dzp = pool_lin("pool_bwd", (dp, D_POOL, 0), wvec + tok[0:1, 0:1], True, BF16)
    dq, dk, dv, dbias = attn_bwd("attn_bwd", z, w.bias, dao, after=tok)
    (g["rel_bias"],) = w.bias_vjp(dbias)
    dzc, dcw = conv_bwd("conv_bwd", (z, 2 * D_CONV, Z_CONV // (2 * D_CONV)), w.conv_w + tok[0:1, 0:1], dcv)
    g["conv_w"] = dcw[:CONV_WIDTH]
    dz = jnp.concatenate([dzg, dzc, dzp, dq, dk, dv], axis=1)
    du = mm_big("mm_du", dz, w.w_in, "nt", (512, 1024, D_IN), F32)
    (dx,), (dscm, dshm) = row_bwd("lnmod_mix_bwd", f_lnmod, [x], [sc_m, sh_m], [du], 512, [(0, F32)], [0, 1],
                                  add_to=dxa)
    g["dmod"] = jnp.concatenate([dshm, dscm, dgm, dshf, dscf, dgf], axis=1)
    tok = ready("small", g)
    g["w_in"] = _unperm_cols(mm_big("mm_dw_in", u, dz, "tn", (1024, 896, 2048), BF16, after=tok))
    return dx, g, ready("in", g)


def kernel(x, c, w_ada, b_ada, w_in, b_gate, w_pool, pool_scale, rel_bias, conv_w, conv_b, conv_ln_g, conv_ln_b, w_br_pool, w_br_attn, w_br_conv, w_o, ln_mix_g, ln_mix_b, w_ff1, b_ff1, w_ff2, b_ff2, ln_ff_g, ln_ff_b, loss_target, m_w_ada, m_b_ada, m_w_in, m_b_gate, m_w_pool, m_pool_scale, m_rel_bias, m_conv_w, m_conv_b, m_conv_ln_g, m_conv_ln_b, m_w_br_pool, m_w_br_attn, m_w_br_conv, m_w_o, m_ln_mix_g, m_ln_mix_b, m_w_ff1, m_b_ff1, m_w_ff2, m_b_ff2, m_ln_ff_g, m_ln_ff_b, v_w_ada, v_b_ada, v_w_in, v_b_gate, v_w_pool, v_pool_scale, v_rel_bias, v_conv_w, v_conv_b, v_conv_ln_g, v_conv_ln_b, v_w_br_pool, v_w_br_attn, v_w_br_conv, v_w_o, v_ln_mix_g, v_ln_mix_b, v_w_ff1, v_b_ff1, v_w_ff2, v_b_ff2, v_ln_ff_g, v_ln_ff_b):
    args = dict(locals())
    wts = {n: args[n] for n in WEIGHT_NAMES}
    mom = {n: args["m_" + n] for n in WEIGHT_NAMES}
    var = {n: args["v_" + n] for n in WEIGHT_NAMES}
    me = 4 * lax.axis_index("x") + 2 * lax.axis_index("y") + lax.axis_index("c")
    xs, tgt = x[0], loss_target[0]
    nc_ada = w_ada.shape[2]
    wvec = jnp.asarray(np.repeat(np.array(POOL_WINDOWS, np.float32), POOL_GROUP)[None, :])

    w_cat = jnp.concatenate([w_ada[l] for l in range(DEPTH)], axis=1)
    b_lay = b_ada.reshape(DEPTH, N_DEV, nc_ada).transpose(1, 0, 2).reshape(N_DEV, 1, DEPTH * nc_ada)
    land, cact = ada_fwd("ada_fwd", c, w_cat, b_lay)
    mod = land.reshape(N_DEV, DEPTH, nc_ada).transpose(1, 0, 2).reshape(DEPTH, 6 * D_MODEL)
    modr = [[mod[l:l + 1, i * D_MODEL:(i + 1) * D_MODEL] for i in range(6)] for l in range(DEPTH)]

    cw_pack = _flat_pad([conv_w], 8 * LANE).reshape(-1, LANE)
    shards = [[wts[n][l].astype(BF16) for n in BIG_NAMES] for l in range(DEPTH)]
    g_in0, cw_all, token = all_gather_multi("gather_w_in_l0", [shards[0][0], cw_pack])
    cw_all = cw_all.reshape(N_DEV, -1)[:, :conv_w.size]
    conv_full = cw_all.reshape((N_DEV,) + conv_w.shape).transpose(1, 2, 0, 3).reshape(DEPTH, CONV_WIDTH, D_CONV)
    gathers = {}
    for key, arrs in [("rest_l0", shards[0][1:]), ("w_in_l1", shards[1][:1]), ("rest_l1", shards[1][1:])]:
        arrs = [a + token[0:1, 0:1].astype(a.dtype) for a in arrs]
        ssem, rsem, thru, lands, token = exchange_start(f"gather_{key}_start", arrs, True)
        gathers[key] = (arrs, ssem, rsem, thru, lands)
    modr[0] = [r + token[0:1, 0:1] for r in modr[0]]

    def with_own(lands_, own):
        return [lax.dynamic_update_index_in_dim(ld, o, me, axis=0) for ld, o in zip(lands_, own)]

    def gathered(key, after):
        arrs, ssem, rsem, thru, lands = gathers[key]
        return with_own(exchange_wait(f"gather_{key}_wait", ssem, rsem, thru, lands, after, True), arrs)

    def fetch_rest_for(l):
        def fetch_rest(w, after):
            full = {n: _from_blocks(n, g) for n, g in zip(BIG_NAMES[1:], gathered(f"rest_l{l}", after))}
            w.wbp, w.wba, w.wbc = full["w_br_pool"], full["w_br_attn"], full["w_br_conv"]
            w.wo, w.w_ff1, w.w_ff2 = full["w_o"], full["w_ff1"], full["w_ff2"]
        return fetch_rest

    def layer_weights(l, in_blocks):
        w = _Layer()
        w.w_in = _perm_cols(_from_blocks("w_in", in_blocks))
        w.wbd = _block_diag(w_pool[l])
        w.ps, w.cb, w.clg, w.clb = _row(pool_scale[l]), _row(conv_b[l]), _row(conv_ln_g[l]), _row(conv_ln_b[l])
        w.bg, w.lmg, w.lmb = _row(b_gate[l]), _row(ln_mix_g[l]), _row(ln_mix_b[l])
        w.b1, w.b2, w.lfg, w.lfb = _row(b_ff1[l]), _row(b_ff2[l]), _row(ln_ff_g[l]), _row(ln_ff_b[l])
        w.conv_w = jnp.pad(conv_full[l], ((0, CONV_PAD - CONV_WIDTH), (0, 0)))
        w.bias, w.bias_vjp = jax.vjp(_bias_full, rel_bias[l])
        return w

    layers, saved = [layer_weights(0, g_in0)], []
    h, sv = _layer_fwd(xs, modr[0], layers[0], wvec, fetch_rest_for(0))
    saved.append(sv)
    layers.append(layer_weights(1, gathered("w_in_l1", h)[0]))
    h, sv = _layer_fwd(h, modr[1], layers[1], wvec, fetch_rest_for(1))
    saved.append(sv)
    lpart, dy = loss_head("loss_head", h, tgt, 512)
    loss = lax.psum(lpart[0, 0], ("x", "y", "c"))
    grads, dmods = [None] * DEPTH, [None] * DEPTH
    pending = {}
    small_shapes = [wts[n].shape if n != "conv_w" else (DEPTH, CONV_WIDTH, D_CONV) for n in SMALL_NAMES]

    def ready_for(l):
        def ready(group, g):
            if group == "small":
                if l > 0:
                    return None
                both = [g, grads[1]]
                local = [jnp.concatenate([both[k]["dmod"] for k in range(DEPTH)], axis=0)]
                local += [jnp.stack([both[k][n].reshape(shp[1:]) for k in range(DEPTH)])
                          for n, shp in zip(SMALL_NAMES[1:], small_shapes[1:])]
                pack = _flat_pad(local, 8 * LANE).reshape(-1, LANE)
                ssem, rsem, thru, lands, token = exchange_start("gather_small_grads_start", [pack], True)
                pending["small"] = (pack, ssem, rsem, thru, lands)
                return token
            names = dict(GRAD_GROUPS)[group]
            blocks = [_to_blocks(n, g[n]).astype(BF16) for n in names]
            ssem, rsem, thru, lands, token = exchange_start(f"scatter_l{l}_{group}_start", blocks, False)
            pending[(l, group)] = (names, blocks, ssem, rsem, thru, lands)
            return token
        return ready

    def received(l, group, after):
        names, blocks, ssem, rsem, thru, lands = pending[(l, group)]
        lands = exchange_wait(f"scatter_l{l}_{group}_wait", ssem, rsem, thru, lands, after, False)
        own = [lax.dynamic_index_in_dim(b, me, axis=0, keepdims=False) for b in blocks]
        return dict(zip(names, with_own(lands, own)))

    dy, grads[1], token = _layer_bwd(dy, saved[1], modr[1], layers[1], wvec, ready_for(1))
    modr0 = [r + token[0:1, 0:1] for r in modr[0]]
    dy, grads[0], token = _layer_bwd(dy, saved[0], modr0, layers[0], wvec, ready_for(0))
    recv = [{} for _ in range(DEPTH)]
    for l, group in [(1, "ff"), (1, "mix"), (1, "in"), (0, "ff"), (0, "mix")]:
        recv[l].update(received(l, group, token))
    grad_x = dy[None]
    dmod_size = DEPTH * 6 * D_MODEL

    small_pack, ssem, rsem, thru, lands = pending["small"]
    (small_all,) = with_own(exchange_wait("gather_small_grads_wait", ssem, rsem, thru, lands, token, True), [small_pack])
    small_sum = sum_blocks("sum_small_grads", small_all, small_pack.shape[0]).reshape(-1)
    gsmall = dict(zip(SMALL_NAMES, _unflat(small_sum, small_shapes)))
    gw = dict(gsmall)
    gw["conv_w"] = lax.dynamic_slice_in_dim(gsmall["conv_w"], me * conv_w.shape[2], conv_w.shape[2], axis=2)

    dmod_all = small_all.reshape(N_DEV, -1)[:, :dmod_size].reshape(N_DEV, DEPTH, N_DEV, nc_ada)
    dm_mine = lax.dynamic_index_in_dim(dmod_all, me, axis=2, keepdims=False).reshape(N_DEV, DEPTH * nc_ada)
    cact_t = jnp.pad(cact.T, ((0, 0), (0, LANE - N_DEV)))
    dm_pad = jnp.pad(dm_mine, ((0, LANE - N_DEV), (0, 0)))
    dw_cat = mm_big("mm_dw_ada", cact_t, dm_pad, "nn", (D_MODEL, DEPTH * nc_ada, LANE), F32)
    gw["w_ada"] = jnp.stack([dw_cat[:, l * nc_ada:(l + 1) * nc_ada] for l in range(DEPTH)])

    delta, new_m, new_v = {}, {}, {}
    for n in ["w_ada", "w_ff2", "w_ff1", "w_o", "w_br_pool", "w_br_attn", "w_br_conv", "w_in"]:
        shp = wts[n].shape
        two_d = lambda a, shp=shp: a.reshape(shp[0] * shp[1], shp[2])
        tr = min(256, shp[1])
        if n == "w_ada":
            res = (gw[n],) + tuple(adamw("adamw_" + n, two_d(wts[n]), two_d(gw[n]), two_d(mom[n]), two_d(var[n]), tr))
        else:
            if n == "w_in":
                recv[0].update(received(0, "in", new_v["w_br_conv"]))
            res = adamw_sum("adamw_" + n, two_d(wts[n]), [recv[l][n] for l in range(DEPTH)], two_d(mom[n]),
                            two_d(var[n]), tr)
        gw[n], delta[n], new_m[n], new_v[n] = [a.reshape(shp) for a in res]
    packs = [_flat_pad([src[n] for n in SMALL_NAMES], 8 * LANE).reshape(-1, LANE) for src in (wts, gw, mom, var)]
    res = adamw("adamw_small", *packs, packs[0].shape[0])
    shapes = [wts[n].shape for n in SMALL_NAMES]
    for out, flat in zip((delta, new_m, new_v), res):
        out.update(dict(zip(SMALL_NAMES, _unflat(flat.reshape(-1), shapes))))

    return (loss, grad_x, *[gw[n] for n in WEIGHT_NAMES], *[delta[n] for n in WEIGHT_NAMES],
            *[new_m[n] for n in WEIGHT_NAMES], *[new_v[n] for n in WEIGHT_NAMES])
```

```python
import functools

import jax
import jax.numpy as jnp
import numpy as np
from jax import lax
from jax.experimental import pallas as pl
from jax.experimental.pallas import tpu as pltpu

F32 = jnp.float32
BF16 = jnp.bfloat16
MESH = pl.DeviceIdType.MESH

D_MODEL = 1024
DEPTH = 2
CHUNK = 64
N_HEADS = 8
HEAD_DIM = 64
D_POOL = 256
D_ATTN = 512
D_CONV = 256
CONV_WIDTH = 31
D_FF = 4096
D_IN = 5376
N_PREV = 8
BAND = (N_PREV + 1) * CHUNK
REL_CLIP = 128
ALPHA = (2.0 * DEPTH) ** 0.25
LN_EPS = 1e-5
NEG_INF = -1e30
N_DEV = 8

ADAM_LR, ADAM_B1, ADAM_B2, ADAM_EPS, ADAM_WD, ADAM_STEP = 0.001, 0.9, 0.999, 1e-08, 0.01, 10

VMEM_LIMIT = 56 * 1024 * 1024

Z_GATE, Z_CONV, Z_POOL, Z_Q, Z_K, Z_V = 0, 3072, 3584, 3840, 4352, 4864
ATT_TILE = 512
LANE = 128


def _dg(a, b, ca, cb):
    return lax.dot_general(a.astype(BF16), b.astype(BF16), (((ca,), (cb,)), ((), ())),
                           preferred_element_type=F32)


@jax.custom_vjp
def mm_nn(a, b):
    return _dg(a, b, 1, 0)


def _mm_nn_fwd(a, b):
    return _dg(a, b, 1, 0), (a, b)


def _mm_nn_bwd(res, g):
    a, b = res
    return _dg(g, b, 1, 1).astype(a.dtype), _dg(a, g, 0, 0).astype(b.dtype)


mm_nn.defvjp(_mm_nn_fwd, _mm_nn_bwd)


@jax.custom_vjp
def mm_nt(a, b):
    return _dg(a, b, 1, 1)


def _mm_nt_fwd(a, b):
    return _dg(a, b, 1, 1), (a, b)


def _mm_nt_bwd(res, g):
    a, b = res
    return _dg(g, b, 1, 0).astype(a.dtype), _dg(g, a, 0, 0).astype(b.dtype)


mm_nt.defvjp(_mm_nt_fwd, _mm_nt_bwd)


def _ln(x):
    mu = jnp.mean(x, axis=-1, keepdims=True)
    xc = x - mu
    var = jnp.mean(xc * xc, axis=-1, keepdims=True)
    return xc * lax.rsqrt(var + LN_EPS)


def _norm_rows(rows):
    return [r if isinstance(r, tuple) else (r, r.shape[1], 0) for r in rows]


def _row_spec(tm, r):
    _, width, cb = r
    return pl.BlockSpec((tm, width), lambda i, cb=cb: (i, cb))


def _full_spec(a):
    nd = a.ndim
    return pl.BlockSpec(a.shape, lambda i, nd=nd: (0,) * nd)


def row_fwd(name, f, rows, params, outs, tm):
    rows = _norm_rows(rows)
    s = rows[0][0].shape[0]
    nr, npar = len(rows), len(params)

    def body(*refs):
        r = [x[...].astype(F32) for x in refs[:nr]]
        p = [x[...] for x in refs[nr:nr + npar]]
        res = f(*r, *p)
        for o_ref, o in zip(refs[nr + npar:], res):
            o_ref[...] = o.astype(o_ref.dtype)

    return pl.pallas_call(
        body, name=name, grid=(s // tm,),
        in_specs=[_row_spec(tm, r) for r in rows] + [_full_spec(p) for p in params],
        out_specs=[pl.BlockSpec((tm, w), lambda i: (i, 0)) for w, _ in outs],
        out_shape=[jax.ShapeDtypeStruct((s, w), dt) for w, dt in outs],
        compiler_params=pltpu.CompilerParams(dimension_semantics=("parallel",), vmem_limit_bytes=VMEM_LIMIT),
    )(*[r[0] for r in rows], *params)


def row_bwd(name, f, rows, params, douts, tm, want_rows, want_params, add_to=None):
    rows = _norm_rows(rows)
    s = rows[0][0].shape[0]
    nr, npar, nd = len(rows), len(params), len(douts)
    nadd = 0 if add_to is None else 1
    n_in = nr + npar + nd + nadd

    def body(*refs):
        i = pl.program_id(0)
        r = [x[...].astype(F32) for x in refs[:nr]]
        p = [x[...].astype(F32) for x in refs[nr:nr + npar]]
        d = [x[...].astype(F32) for x in refs[nr + npar:nr + npar + nd]]
        _, vjp = jax.vjp(f, *r, *p)
        g = vjp(tuple(d))
        out_refs = refs[n_in:]
        for k, (idx, _) in enumerate(want_rows):
            val = g[idx]
            if nadd and k == 0:
                val = val + refs[n_in - 1][...].astype(F32)
            out_refs[k][...] = val.astype(out_refs[k].dtype)
        for k, idx in enumerate(want_params):
            gp = g[nr + idx]
            o_ref = out_refs[len(want_rows) + k]

            @pl.when(i == 0)
            def _():
                o_ref[...] = gp

            @pl.when(i > 0)
            def _():
                o_ref[...] += gp

    in_specs = ([_row_spec(tm, r) for r in rows] + [_full_spec(p) for p in params]
                + [pl.BlockSpec((tm, d.shape[1]), lambda i: (i, 0)) for d in douts])
    args = [r[0] for r in rows] + list(params) + list(douts)
    if nadd:
        in_specs.append(pl.BlockSpec((tm, add_to.shape[1]), lambda i: (i, 0)))
        args.append(add_to)
    out_specs = ([pl.BlockSpec((tm, rows[idx][1]), lambda i: (i, 0)) for idx, _ in want_rows]
                 + [_full_spec(params[idx]) for idx in want_params])
    out_shape = ([jax.ShapeDtypeStruct((s, rows[idx][1]), dt) for idx, dt in want_rows]
                 + [jax.ShapeDtypeStruct(params[idx].shape, F32) for idx in want_params])
    res = pl.pallas_call(
        body, name=name, grid=(s // tm,), in_specs=in_specs, out_specs=out_specs, out_shape=out_shape,
        compiler_params=pltpu.CompilerParams(dimension_semantics=("arbitrary",), vmem_limit_bytes=VMEM_LIMIT),
    )(*args)
    return res[:len(want_rows)], res[len(want_rows):]


def f_lnmod(x, sc, sh):
    return (_ln(x) * (1.0 + sc) + sh,)


def f_merge(p, ao, cv, zg, x, wbd, ps, wbp, wba, wbc, cb, clg, clb, bg, wo, gm, lg, lb):
    pm = mm_nn(p, wbd) * ps
    co = jax.nn.silu(_ln(cv + cb) * clg + clb)
    y_pool = mm_nn(pm, wbp)
    y_attn = mm_nn(ao, wba)
    y_conv = mm_nn(co, wbc)
    gates = jax.nn.sigmoid(zg + bg)
    merged = (gates[:, :D_MODEL] * y_pool + gates[:, D_MODEL:2 * D_MODEL] * y_attn
              + gates[:, 2 * D_MODEL:] * y_conv)
    mix = mm_nn(merged, wo)
    return (_ln(ALPHA * x + gm * mix) * lg + lb,)


def f_relu2(hpre, b1):
    a = jax.nn.relu(hpre + b1)
    return (a * a,)


def f_ffout(x1, ff, b2, gf, lg, lb):
    return (_ln(ALPHA * x1 + gf * (ff + b2)) * lg + lb,)


def mm_big(name, a, b, kind, tiles, out_dtype, j_outer=False, after=None):
    if kind == "nn":
        o0, o1, red = a.shape[0], b.shape[1], a.shape[1]
    elif kind == "nt":
        o0, o1, red = a.shape[0], b.shape[0], a.shape[1]
    else:
        o0, o1, red = a.shape[1], b.shape[1], a.shape[0]
    t0, t1, tr = min(tiles[0], o0), min(tiles[1], o1), min(tiles[2], red)
    if kind == "nn":
        a_spec = pl.BlockSpec((t0, tr), lambda i, j, r: (i, r))
        b_spec = pl.BlockSpec((tr, t1), lambda i, j, r: (r, j))
        dims = (1, 0)
    elif kind == "nt":
        a_spec = pl.BlockSpec((t0, tr), lambda i, j, r: (i, r))
        b_spec = pl.BlockSpec((t1, tr), lambda i, j, r: (j, r))
        dims = (1, 1)
    else:
        a_spec = pl.BlockSpec((tr, t0), lambda i, j, r: (r, i))
        b_spec = pl.BlockSpec((tr, t1), lambda i, j, r: (r, j))
        dims = (0, 0)
    assert o0 % t0 == 0 and o1 % t1 == 0 and red % tr == 0, (name, a.shape, b.shape, tiles)
    n0, n1, nred = o0 // t0, o1 // t1, red // tr
    o_spec = pl.BlockSpec((t0, t1), lambda i, j, r: (i, j))
    if j_outer:
        swap = lambda spec: pl.BlockSpec(spec.block_shape, lambda j, i, r, f=spec.index_map: f(i, j, r))
        a_spec, b_spec, o_spec = swap(a_spec), swap(b_spec), swap(o_spec)
        grid = (n1, n0, nred)
    else:
        grid = (n0, n1, nred)

    deps = [] if after is None else [after]
    dep_specs = [pl.BlockSpec(d.shape, lambda i, j, r, nd=d.ndim: (0,) * nd) for d in deps]
    if nred == 1:
        def body(a_ref, b_ref, *rest):
            o_ref = rest[len(deps)]
            o_ref[...] = _dg(a_ref[...], b_ref[...], *dims).astype(o_ref.dtype)
        scratch = []
    else:
        def body(a_ref, b_ref, *rest):
            o_ref, acc_ref = rest[len(deps):]
            r = pl.program_id(2)
            part = _dg(a_ref[...], b_ref[...], *dims)

            @pl.when(r == 0)
            def _():
                acc_ref[...] = part

            @pl.when(jnp.logical_and(r > 0, r < nred - 1))
            def _():
                acc_ref[...] += part

            @pl.when(r == nred - 1)
            def _():
                o_ref[...] = (acc_ref[...] + part).astype(o_ref.dtype)
        scratch = [pltpu.VMEM((t0, t1), F32)]

    return pl.pallas_call(
        body, name=name, grid=grid,
        in_specs=[a_spec, b_spec] + dep_specs,
        out_specs=o_spec,
        out_shape=jax.ShapeDtypeStruct((o0, o1), out_dtype),
        scratch_shapes=scratch,
        compiler_params=pltpu.CompilerParams(dimension_semantics=("parallel", "parallel", "arbitrary"),
                                             vmem_limit_bytes=VMEM_LIMIT),
    )(a, b, *deps)


def mm_ff1_relu2(name, u2, w1, b1, tm, tn):
    m, k = u2.shape
    n = w1.shape[1]
    tm, tn = min(tm, m), min(tn, n)

    def body(a_ref, b_ref, bias_ref, hpre_ref, h_ref):
        acc = _dg(a_ref[...], b_ref[...], 1, 0)
        hpre_ref[...] = acc
        h_ref[...] = f_relu2(acc, bias_ref[...])[0].astype(h_ref.dtype)

    out = pl.BlockSpec((tm, tn), lambda j, i: (i, j))
    return pl.pallas_call(
        body, name=name, grid=(n // tn, m // tm),
        in_specs=[pl.BlockSpec((tm, k), lambda j, i: (i, 0)), pl.BlockSpec((k, tn), lambda j, i: (0, j)),
                  pl.BlockSpec((1, tn), lambda j, i: (0, j))],
        out_specs=[out, out],
        out_shape=[jax.ShapeDtypeStruct((m, n), F32), jax.ShapeDtypeStruct((m, n), BF16)],
        compiler_params=pltpu.CompilerParams(dimension_semantics=("parallel", "parallel"), vmem_limit_bytes=VMEM_LIMIT),
    )(u2, w1, b1)


def mm_dh_relu2(name, dff, w2, hpre, b1, tm, tn):
    m, k = dff.shape
    n = w2.shape[0]
    tm, tn = min(tm, m), min(tn, n)

    def body(a_ref, b_ref, hpre_ref, bias_ref, d_ref, db_ref):
        i = pl.program_id(1)
        dh = _dg(a_ref[...], b_ref[...], 1, 1)
        _, vjp = jax.vjp(f_relu2, hpre_ref[...], bias_ref[...])
        dhpre, db = vjp((dh,))
        d_ref[...] = dhpre.astype(d_ref.dtype)

        @pl.when(i == 0)
        def _():
            db_ref[...] = db

        @pl.when(i > 0)
        def _():
            db_ref[...] += db

    tile = pl.BlockSpec((tm, tn), lambda j, i: (i, j))
    col = pl.BlockSpec((1, tn), lambda j, i: (0, j))
    return pl.pallas_call(
        body, name=name, grid=(n // tn, m // tm),
        in_specs=[pl.BlockSpec((tm, k), lambda j, i: (i, 0)), pl.BlockSpec((tn, k), lambda j, i: (j, 0)), tile, col],
        out_specs=[tile, col],
        out_shape=[jax.ShapeDtypeStruct((m, n), BF16), jax.ShapeDtypeStruct((1, n), F32)],
        compiler_params=pltpu.CompilerParams(dimension_semantics=("parallel", "arbitrary"), vmem_limit_bytes=VMEM_LIMIT),
    )(dff, w2, hpre, b1)


POOL_PAD = 16
POOL_ROWS = 256


def pool_lin(name, x, wvec, transpose, out_dtype):
    arr, width, cb = x
    s = arr.shape[0]
    n_steps = s // POOL_ROWS

    def body(x_ref, w_ref, o_ref, xp_ref):
        wv = w_ref[...]
        zeros = jnp.zeros((POOL_PAD, width), F32)
        xp_ref[0:POOL_PAD, :] = zeros
        xp_ref[s + POOL_PAD:s + 2 * POOL_PAD, :] = zeros

        def count(t0):
            t = lax.broadcasted_iota(jnp.int32, (POOL_ROWS, width), 0) + (t0 + 1)
            return jnp.minimum(t.astype(F32), wv)

        def fill(i, carry):
            t0 = pl.multiple_of(i * POOL_ROWS, POOL_ROWS)
            v = x_ref[pl.ds(t0, POOL_ROWS), :].astype(F32)
            if transpose:
                v = v / count(t0)
            xp_ref[pl.ds(t0 + POOL_PAD, POOL_ROWS), :] = v
            return carry

        lax.fori_loop(0, n_steps, fill, 0)

        def step(i, carry):
            t0 = pl.multiple_of(i * POOL_ROWS, POOL_ROWS)
            win = xp_ref[pl.ds(t0, POOL_ROWS + 2 * POOL_PAD), :]
            acc = jnp.zeros((POOL_ROWS, width), F32)
            for j in range(POOL_PAD):
                off = POOL_PAD + j if transpose else POOL_PAD - j
                acc = acc + jnp.where(wv > j, win[off:off + POOL_ROWS, :], 0.0)
            cur = x_ref[pl.ds(t0, POOL_ROWS), :].astype(F32)
            res = acc - cur if transpose else acc / count(t0) - cur
            o_ref[pl.ds(t0, POOL_ROWS), :] = res.astype(o_ref.dtype)
            return carry

        lax.fori_loop(0, n_steps, step, 0)

    return pl.pallas_call(
        body, name=name, grid=(1,),
        in_specs=[pl.BlockSpec((s, width), lambda i, cb=cb: (0, cb)), pl.BlockSpec((1, width), lambda i: (0, 0))],
        out_specs=pl.BlockSpec((s, width), lambda i: (0, 0)),
        out_shape=jax.ShapeDtypeStruct((s, width), out_dtype),
        scratch_shapes=[pltpu.VMEM((s + 2 * POOL_PAD, width), F32)],
        compiler_params=pltpu.CompilerParams(dimension_semantics=("arbitrary",), vmem_limit_bytes=VMEM_LIMIT),
    )(arr, wvec)


CONV_PAD = 32
CONV_ROWS = 128


def _glu(a, g):
    return a * jax.nn.sigmoid(g)


def conv_fwd(name, zc, w):
    arr, width, cb = zc
    s = arr.shape[0]
    n_steps = s // CONV_ROWS
    lead = CONV_PAD - (CONV_WIDTH - 1)

    def body(z_ref, w_ref, o_ref, hp_ref):
        hp_ref[0:CONV_PAD, :] = jnp.zeros((CONV_PAD, D_CONV), F32)

        def fill(i, carry):
            t0 = pl.multiple_of(i * CONV_ROWS, CONV_ROWS)
            z = z_ref[pl.ds(t0, CONV_ROWS), :]
            hp_ref[pl.ds(t0 + CONV_PAD, CONV_ROWS), :] = _glu(z[:, :D_CONV], z[:, D_CONV:])
            return carry

        lax.fori_loop(0, n_steps, fill, 0)
        wv = w_ref[...]

        def step(i, carry):
            t0 = pl.multiple_of(i * CONV_ROWS, CONV_ROWS)
            win = hp_ref[pl.ds(t0, CONV_ROWS + CONV_PAD), :]
            acc = jnp.zeros((CONV_ROWS, D_CONV), F32)
            for k in range(CONV_WIDTH):
                acc = acc + wv[k:k + 1, :] * win[lead + k:lead + k + CONV_ROWS, :]
            o_ref[pl.ds(t0, CONV_ROWS), :] = acc
            return carry

        lax.fori_loop(0, n_steps, step, 0)

    return pl.pallas_call(
        body, name=name, grid=(1,),
        in_specs=[pl.BlockSpec((s, width), lambda i, cb=cb: (0, cb)), pl.BlockSpec(w.shape, lambda i: (0, 0))],
        out_specs=pl.BlockSpec((s, D_CONV), lambda i: (0, 0)),
        out_shape=jax.ShapeDtypeStruct((s, D_CONV), F32),
        scratch_shapes=[pltpu.VMEM((s + CONV_PAD, D_CONV), F32)],
        compiler_params=pltpu.CompilerParams(dimension_semantics=("arbitrary",), vmem_limit_bytes=VMEM_LIMIT),
    )(arr, w)


def conv_bwd(name, zc, w, dout):
    arr, width, cb = zc
    s = arr.shape[0]
    n_steps = s // CONV_ROWS
    lead = CONV_PAD - (CONV_WIDTH - 1)

    def body(z_ref, w_ref, d_ref, dz_ref, dw_ref, hp_ref, dp_ref):
        hp_ref[0:CONV_PAD, :] = jnp.zeros((CONV_PAD, D_CONV), F32)
        dp_ref[s:s + CONV_PAD, :] = jnp.zeros((CONV_PAD, D_CONV), F32)
        dw_ref[...] = jnp.zeros(dw_ref.shape, F32)

        def fill(i, carry):
            t0 = pl.multiple_of(i * CONV_ROWS, CONV_ROWS)
            z = z_ref[pl.ds(t0, CONV_ROWS), :]
            hp_ref[pl.ds(t0 + CONV_PAD, CONV_ROWS), :] = _glu(z[:, :D_CONV], z[:, D_CONV:])
            dp_ref[pl.ds(t0, CONV_ROWS), :] = d_ref[pl.ds(t0, CONV_ROWS), :]
            return carry

        lax.fori_loop(0, n_steps, fill, 0)
        wv = w_ref[...]

        def step(i, carry):
            t0 = pl.multiple_of(i * CONV_ROWS, CONV_ROWS)
            hwin = hp_ref[pl.ds(t0, CONV_ROWS + CONV_PAD), :]
            dwin = dp_ref[pl.ds(t0, CONV_ROWS + CONV_PAD), :]
            dcur = dwin[0:CONV_ROWS, :]
            dh = jnp.zeros((CONV_ROWS, D_CONV), F32)
            rows = []
            for k in range(CONV_WIDTH):
                rows.append(jnp.sum(dcur * hwin[lead + k:lead + k + CONV_ROWS, :], axis=0, keepdims=True))
                back = CONV_WIDTH - 1 - k
                dh = dh + wv[k:k + 1, :] * dwin[back:back + CONV_ROWS, :]
            rows.append(jnp.zeros((1, D_CONV), F32))
            dw_ref[...] += jnp.concatenate(rows, axis=0)
            z = z_ref[pl.ds(t0, CONV_ROWS), :]
            _, vjp = jax.vjp(_glu, z[:, :D_CONV], z[:, D_CONV:])
            da, dg = vjp(dh)
            dz_ref[pl.ds(t0, CONV_ROWS), :] = jnp.concatenate([da, dg], axis=1).astype(dz_ref.dtype)
            return carry

        lax.fori_loop(0, n_steps, step, 0)

    return pl.pallas_call(
        body, name=name, grid=(1,),
        in_specs=[pl.BlockSpec((s, width), lambda i, cb=cb: (0, cb)), pl.BlockSpec(w.shape, lambda i: (0, 0)),
                  pl.BlockSpec((s, D_CONV), lambda i: (0, 0))],
        out_specs=[pl.BlockSpec((s, width), lambda i: (0, 0)), pl.BlockSpec(w.shape, lambda i: (0, 0))],
        out_shape=[jax.ShapeDtypeStruct((s, width), BF16), jax.ShapeDtypeStruct(w.shape, F32)],
        scratch_shapes=[pltpu.VMEM((s + CONV_PAD, D_CONV), F32), pltpu.VMEM((s + CONV_PAD, D_CONV), F32)],
        compiler_params=pltpu.CompilerParams(dimension_semantics=("arbitrary",), vmem_limit_bytes=VMEM_LIMIT),
    )(arr, w, dout)


HEADS_PER_STEP = LANE // HEAD_DIM
CHUNKS_PER_TILE = ATT_TILE // CHUNK
KEY_BLOCKS = N_PREV * CHUNK // ATT_TILE + 1
KEY_SPAN = KEY_BLOCKS * ATT_TILE


def _attn_tile(q, *rest, missing_cols):
    kcat = jnp.concatenate(rest[:KEY_BLOCKS], axis=0)
    vcat = jnp.concatenate(rest[KEY_BLOCKS:2 * KEY_BLOCKS], axis=0)
    bias = rest[2 * KEY_BLOCKS]
    lane = lax.broadcasted_iota(jnp.int32, (1, LANE), 1)
    col = lax.broadcasted_iota(jnp.int32, (1, KEY_SPAN), 1)
    missing = col < missing_cols
    qs = q * (HEAD_DIM ** -0.5)
    o = jnp.zeros((ATT_TILE, LANE), F32)
    for h in range(HEADS_PER_STEP):
        in_head = jnp.logical_and(lane >= h * HEAD_DIM, lane < (h + 1) * HEAD_DIM)
        sc = mm_nt(jnp.where(in_head, qs, 0.0), kcat) + bias[h]
        sc = jnp.where(missing, NEG_INF, sc)
        m = jnp.max(sc, axis=-1, keepdims=True)
        e = jnp.exp(sc - lax.stop_gradient(m))
        p = e / jnp.sum(e, axis=-1, keepdims=True)
        o = o + jnp.where(in_head, mm_nn(p, vcat), 0.0)
    return o


def _missing_cols(n):
    return jnp.maximum((KEY_BLOCKS - 1 - n) * ATT_TILE, 0)


def _attn_in_specs(nt):
    def spec(col0, back):
        return pl.BlockSpec((ATT_TILE, LANE),
                            lambda hp, n, col0=col0, back=back: (jnp.clip(n - back, 0, nt - 1), col0 // LANE + hp))
    backs = list(range(KEY_BLOCKS - 1, -1, -1))
    return ([spec(Z_Q, 0)] + [spec(Z_K, b) for b in backs] + [spec(Z_V, b) for b in backs]
            + [pl.BlockSpec((HEADS_PER_STEP, ATT_TILE, KEY_SPAN), lambda hp, n: (hp, 0, 0))])


def attn_fwd(name, z, bias):
    s = z.shape[0]
    nt = s // ATT_TILE
    n_in = 2 + 2 * KEY_BLOCKS

    def body(*refs):
        o_ref = refs[n_in]
        vals = [r[...] for r in refs[:n_in]]
        o = _attn_tile(*vals, missing_cols=_missing_cols(pl.program_id(1)))
        o_ref[...] = o.astype(o_ref.dtype)

    return pl.pallas_call(
        body, name=name, grid=(N_HEADS // HEADS_PER_STEP, nt),
        in_specs=_attn_in_specs(nt),
        out_specs=pl.BlockSpec((ATT_TILE, LANE), lambda hp, n: (n, hp)),
        out_shape=jax.ShapeDtypeStruct((s, D_ATTN), BF16),
        compiler_params=pltpu.CompilerParams(dimension_semantics=("parallel", "parallel"), vmem_limit_bytes=VMEM_LIMIT),
    )(*([z] * (n_in - 1)), bias)


def attn_bwd(name, z, bias, do, after=None):
    s = z.shape[0]
    nt = s // ATT_TILE
    n_in = 2 + 2 * KEY_BLOCKS
    nc = KEY_BLOCKS - 1

    deps = [] if after is None else [after]
    n_out = n_in + 1 + len(deps)

    def body(*refs):
        do_ref = refs[n_in]
        dq_ref, dk_ref, dv_ref, db_ref = refs[n_out:n_out + 4]
        kacc, vacc = refs[n_out + 4:n_out + 4 + nc], refs[n_out + 4 + nc:]
        n = pl.program_id(1)

        @pl.when(n == 0)
        def _():
            db_ref[...] = jnp.zeros(db_ref.shape, F32)
            for acc in (*kacc, *vacc):
                acc[...] = jnp.zeros(acc.shape, F32)

        def shift(out_ref, accs, contrib):
            @pl.when(n >= nc)
            def _():
                first = accs[0][...] if contrib is None else accs[0][...] + contrib[0]
                out_ref[...] = first.astype(out_ref.dtype)
            for j in range(nc - 1):
                accs[j][...] = accs[j + 1][...] if contrib is None else accs[j + 1][...] + contrib[j + 1]
            if contrib is not None:
                accs[nc - 1][...] = contrib[nc]

        @pl.when(n < nt)
        def _():
            fn = functools.partial(_attn_tile, missing_cols=_missing_cols(n))
            _, vjp = jax.vjp(fn, *[r[...] for r in refs[:n_in]])
            grads = vjp(do_ref[...].astype(F32))
            dq_ref[...] = grads[0].astype(dq_ref.dtype)
            db_ref[...] += grads[n_in - 1]
            shift(dk_ref, kacc, grads[1:1 + KEY_BLOCKS])
            shift(dv_ref, vacc, grads[1 + KEY_BLOCKS:1 + 2 * KEY_BLOCKS])

        @pl.when(n >= nt)
        def _():
            shift(dk_ref, kacc, None)
            shift(dv_ref, vacc, None)

    o_cur = pl.BlockSpec((ATT_TILE, LANE), lambda hp, n: (jnp.minimum(n, nt - 1), hp))
    o_old = pl.BlockSpec((ATT_TILE, LANE), lambda hp, n: (jnp.maximum(n - nc, 0), hp))
    b_spec = pl.BlockSpec((HEADS_PER_STEP, ATT_TILE, KEY_SPAN), lambda hp, n: (hp, 0, 0))
    return pl.pallas_call(
        body, name=name, grid=(N_HEADS // HEADS_PER_STEP, nt + nc),
        in_specs=_attn_in_specs(nt) + [o_cur] + [pl.BlockSpec(d.shape, lambda hp, n: (0, 0)) for d in deps],
        out_specs=[o_cur, o_old, o_old, b_spec],
        out_shape=[jax.ShapeDtypeStruct((s, D_ATTN), BF16)] * 3 + [jax.ShapeDtypeStruct((N_HEADS, ATT_TILE, KEY_SPAN), F32)],
        scratch_shapes=[pltpu.VMEM((ATT_TILE, LANE), F32)] * (2 * nc),
        compiler_params=pltpu.CompilerParams(dimension_semantics=("parallel", "arbitrary"), vmem_limit_bytes=VMEM_LIMIT),
    )(*([z] * (n_in - 1)), bias, do, *deps)


def loss_head(name, y, tgt, tm):
    s, d = y.shape

    def body(y_ref, t_ref, l_ref, dy_ref):
        i = pl.program_id(0)
        diff = y_ref[...] - t_ref[...]
        dy_ref[...] = diff * (1.0 / d)
        part = 0.5 * jnp.sum(jnp.mean(diff * diff, axis=-1, keepdims=True), axis=0, keepdims=True)

        @pl.when(i == 0)
        def _():
            l_ref[...] = jnp.zeros(l_ref.shape, F32)

        l_ref[...] += jnp.broadcast_to(part, l_ref.shape)

    row = pl.BlockSpec((tm, d), lambda i: (i, 0))
    return pl.pallas_call(
        body, name=name, grid=(s // tm,), in_specs=[row, row],
        out_specs=[pl.BlockSpec((8, LANE), lambda i: (0, 0)), row],
        out_shape=[jax.ShapeDtypeStruct((8, LANE), F32), jax.ShapeDtypeStruct((s, d), F32)],
        compiler_params=pltpu.CompilerParams(dimension_semantics=("arbitrary",), vmem_limit_bytes=VMEM_LIMIT),
    )(y, tgt)


def adamw(name, w, g, m, v, tr):
    r, c = w.shape
    assert r % tr == 0, (name, w.shape, tr)

    def body(w_ref, g_ref, m_ref, v_ref, d_ref, nm_ref, nv_ref):
        gg = g_ref[...]
        m2 = ADAM_B1 * m_ref[...] + (1.0 - ADAM_B1) * gg
        v2 = ADAM_B2 * v_ref[...] + (1.0 - ADAM_B2) * (gg * gg)
        m_hat = m2 / (1.0 - ADAM_B1 ** ADAM_STEP)
        v_hat = v2 / (1.0 - ADAM_B2 ** ADAM_STEP)
        d_ref[...] = -ADAM_LR * (m_hat / (jnp.sqrt(v_hat) + ADAM_EPS) + ADAM_WD * w_ref[...])
        nm_ref[...] = m2
        nv_ref[...] = v2

    blk = pl.BlockSpec((tr, c), lambda i: (i, 0))
    return pl.pallas_call(
        body, name=name, grid=(r // tr,), in_specs=[blk] * 4, out_specs=[blk] * 3,
        out_shape=[jax.ShapeDtypeStruct((r, c), F32)] * 3,
        compiler_params=pltpu.CompilerParams(dimension_semantics=("parallel",), vmem_limit_bytes=VMEM_LIMIT),
    )(w, g, m, v)


def adamw_sum(name, w, layer_blocks, m, v, tr):
    rows, c = w.shape
    nl = len(layer_blocks)
    nb, r, _ = layer_blocks[0].shape
    assert rows == nl * r and r % tr == 0, (name, w.shape, layer_blocks[0].shape, tr)
    per = r // tr

    def body(*refs):
        w_ref, b_refs, (m_ref, v_ref, g_ref, d_ref, nm_ref, nv_ref) = refs[0], refs[1:1 + nl], refs[1 + nl:]
        i = pl.program_id(0)

        def update(b_ref):
            gg = b_ref[0].astype(F32)
            for j in range(1, nb):
                gg = gg + b_ref[j].astype(F32)
            g_ref[...] = gg
            m2 = ADAM_B1 * m_ref[...] + (1.0 - ADAM_B1) * gg
            v2 = ADAM_B2 * v_ref[...] + (1.0 - ADAM_B2) * (gg * gg)
            m_hat = m2 / (1.0 - ADAM_B1 ** ADAM_STEP)
            v_hat = v2 / (1.0 - ADAM_B2 ** ADAM_STEP)
            d_ref[...] = -ADAM_LR * (m_hat / (jnp.sqrt(v_hat) + ADAM_EPS) + ADAM_WD * w_ref[...])
            nm_ref[...] = m2
            nv_ref[...] = v2

        for l in range(nl):
            pl.when(jnp.logical_and(i >= l * per, i < (l + 1) * per))(functools.partial(update, b_refs[l]))

    blk = pl.BlockSpec((tr, c), lambda i: (i, 0))
    b_specs = [pl.BlockSpec((nb, tr, c), lambda i, l=l: (0, jnp.clip(i - l * per, 0, per - 1), 0)) for l in range(nl)]
    return pl.pallas_call(
        body, name=name, grid=(rows // tr,),
        in_specs=[blk] + b_specs + [blk, blk], out_specs=[blk] * 4,
        out_shape=[jax.ShapeDtypeStruct((rows, c), F32)] * 4,
        compiler_params=pltpu.CompilerParams(dimension_semantics=("arbitrary",), vmem_limit_bytes=VMEM_LIMIT),
    )(w, *layer_blocks, m, v)


def sum_blocks(name, blocks, tr):
    nb, r, c = blocks.shape
    assert r % tr == 0, (name, blocks.shape, tr)

    def body(b_ref, o_ref):
        acc = b_ref[0].astype(F32)
        for j in range(1, nb):
            acc = acc + b_ref[j].astype(F32)
        o_ref[...] = acc

    return pl.pallas_call(
        body, name=name, grid=(r // tr,),
        in_specs=[pl.BlockSpec((nb, tr, c), lambda i: (0, i, 0))],
        out_specs=pl.BlockSpec((tr, c), lambda i: (i, 0)),
        out_shape=jax.ShapeDtypeStruct((r, c), F32),
        compiler_params=pltpu.CompilerParams(dimension_semantics=("parallel",), vmem_limit_bytes=VMEM_LIMIT),
    )(blocks)


FLIPS = [(0, 0, 1), (1, 0, 0), (0, 1, 0), (1, 1, 0), (1, 0, 1), (0, 1, 1), (1, 1, 1)]
ANY = pl.BlockSpec(memory_space=pl.ANY)


def _me():
    return lax.axis_index("x"), lax.axis_index("y"), lax.axis_index("c")


def _flip(pos, f):
    return tuple((1 - p) if fi else p for p, fi in zip(pos, f))


def _idx(pos):
    return 4 * pos[0] + 2 * pos[1] + pos[2]


def all_gather_multi(name, shards):
    n = len(shards)

    def body(*refs):
        x_refs, out_refs, token = refs[:n], refs[n:2 * n], refs[2 * n]
        send_sems, recv_sems, local_sems = refs[2 * n + 1:]
        token[...] = jnp.zeros_like(token)
        x, y, cc = _me()
        me, sibling = (x, y, cc), (x, y, 1 - cc)
        chips = [(1 - x, y), (x, 1 - y), (1 - x, 1 - y)]

        def copy(a, k, block, to, src=None):
            dst = out_refs[a].at[_idx(block)]
            return pltpu.make_async_remote_copy(
                src_ref=dst if src is None else src, dst_ref=dst, send_sem=send_sems.at[7 * a + k],
                recv_sem=recv_sems.at[7 * a + k], device_id=to, device_id_type=MESH)

        mine = [pltpu.make_async_copy(x_refs[a], out_refs[a].at[_idx(me)], local_sems.at[a]) for a in range(n)]
        for cp in mine:
            cp.start()
        first = []
        for a in range(n):
            first.append(copy(a, 0, me, sibling, src=x_refs[a]))
            first += [copy(a, 1 + j, me, (*chip, cc), src=x_refs[a]) for j, chip in enumerate(chips)]
        for cp in first:
            cp.start()
        passed = []
        for j, chip in enumerate(chips):
            for a in range(n):
                copy(a, 1 + j, (*chip, cc), me).wait_recv()
                fwd = copy(a, 4 + j, (*chip, cc), sibling)
                fwd.start()
                passed.append(fwd)
        for a in range(n):
            copy(a, 0, sibling, me).wait_recv()
            for j, chip in enumerate(chips):
                copy(a, 4 + j, (*chip, 1 - cc), me).wait_recv()
        for cp in first + passed:
            cp.wait_send()
        for cp in mine:
            cp.wait()

    return pl.pallas_call(
        body, name=name, in_specs=[ANY] * n, out_specs=[ANY] * n + [pl.BlockSpec(memory_space=pltpu.VMEM)],
        out_shape=[jax.ShapeDtypeStruct((N_DEV,) + a.shape, a.dtype) for a in shards]
        + [jax.ShapeDtypeStruct((8, LANE), F32)],
        scratch_shapes=[pltpu.SemaphoreType.DMA((7 * n,)), pltpu.SemaphoreType.DMA((7 * n,)),
                        pltpu.SemaphoreType.DMA((n,))],
    )(*shards)


def all_to_all_multi(name, blocks):
    n = len(blocks)

    def body(*refs):
        in_refs, out_refs = refs[:n], refs[n:2 * n]
        send_sems, recv_sems, local_sems = refs[2 * n:]
        me = _me()
        mi = _idx(me)
        mine = [pltpu.make_async_copy(in_refs[a].at[mi], out_refs[a].at[mi], local_sems.at[a]) for a in range(n)]
        for cp in mine:
            cp.start()
        sends, recvs = [], []
        for k, f in enumerate(FLIPS):
            peer = _flip(me, f)
            pi = _idx(peer)
            for a in range(n):
                sems = dict(send_sem=send_sems.at[7 * a + k], recv_sem=recv_sems.at[7 * a + k],
                            device_id=peer, device_id_type=MESH)
                sends.append(pltpu.make_async_remote_copy(src_ref=in_refs[a].at[pi], dst_ref=out_refs[a].at[mi], **sems))
                recvs.append(pltpu.make_async_remote_copy(src_ref=in_refs[a].at[mi], dst_ref=out_refs[a].at[pi], **sems))
        for cp in sends:
            cp.start()
        for cp in recvs:
            cp.wait_recv()
        for cp in sends:
            cp.wait_send()
        for cp in mine:
            cp.wait()

    return pl.pallas_call(
        body, name=name, in_specs=[ANY] * n, out_specs=[ANY] * n,
        out_shape=[jax.ShapeDtypeStruct(a.shape, a.dtype) for a in blocks],
        scratch_shapes=[pltpu.SemaphoreType.DMA((7 * n,)), pltpu.SemaphoreType.DMA((7 * n,)),
                        pltpu.SemaphoreType.DMA((n,))],
    )(*blocks)


HBM = pl.BlockSpec(memory_space=pltpu.HBM)
SEM = pl.BlockSpec(memory_space=pltpu.SEMAPHORE)
DATAFLOW = pltpu.SideEffectType.DATAFLOW_SIDE_EFFECTING


def _exchange_copies(a_refs, l_refs, send_sems, recv_sems, gather):
    me = _me()
    mi = _idx(me)
    out = []
    for k, f in enumerate(FLIPS):
        peer = _flip(me, f)
        for a in range(len(a_refs)):
            src = a_refs[a] if gather else a_refs[a].at[_idx(peer)]
            out.append(pltpu.make_async_remote_copy(
                src_ref=src, dst_ref=l_refs[a].at[mi], send_sem=send_sems.at[7 * a + k],
                recv_sem=recv_sems.at[7 * a + k], device_id=peer, device_id_type=MESH))
    return out


def exchange_start(name, arrays, gather):
    n = len(arrays)
    lands = [lax.empty(((N_DEV,) + a.shape) if gather else a.shape, a.dtype) for a in arrays]

    def body(*refs):
        a_refs, l_refs = refs[:n], refs[n:2 * n]
        send_sems, recv_sems = refs[2 * n], refs[2 * n + 1]
        token = refs[4 * n + 2]
        for cp in _exchange_copies(a_refs, l_refs, send_sems, recv_sems, gather):
            cp.start()
        token[...] = jnp.zeros_like(token)

    hbm = lambda a: pltpu.HBM(a.shape, a.dtype)
    res = pl.pallas_call(
        body, name=name,
        out_shape=(pltpu.SemaphoreType.DMA((7 * n,)), pltpu.SemaphoreType.DMA((7 * n,)),
                   *[hbm(a) for a in arrays], *[hbm(a) for a in lands], jax.ShapeDtypeStruct((8, LANE), F32)),
        in_specs=[HBM] * (2 * n),
        out_specs=(SEM, SEM, *[HBM] * (2 * n), pl.BlockSpec(memory_space=pltpu.VMEM)),
        input_output_aliases={i: i + 2 for i in range(2 * n)},
        compiler_params=pltpu.CompilerParams(has_side_effects=DATAFLOW),
    )(*[pltpu.with_memory_space_constraint(a, pltpu.HBM) for a in arrays],
      *[pltpu.with_memory_space_constraint(a, pltpu.HBM) for a in lands])
    return res[0], res[1], list(res[2:2 + n]), list(res[2 + n:2 + 2 * n]), res[-1]


def exchange_wait(name, send_sems, recv_sems, arrays, lands, after, gather):
    n = len(arrays)

    def body(*refs):
        a_refs, l_refs = refs[:n], refs[n:2 * n]
        ssem, rsem = refs[2 * n], refs[2 * n + 1]
        for cp in _exchange_copies(a_refs, l_refs, ssem, rsem, gather):
            cp.wait_send()
            cp.wait_recv()

    hbm = lambda a: pltpu.HBM(a.shape, a.dtype)
    res = pl.pallas_call(
        body, name=name,
        out_shape=(*[hbm(a) for a in arrays], *[hbm(a) for a in lands]),
        in_specs=[HBM] * (2 * n) + [SEM, SEM, pl.BlockSpec(memory_space=pl.ANY)],
        out_specs=tuple([HBM] * (2 * n)),
        input_output_aliases={i: i for i in range(2 * n)},
        compiler_params=pltpu.CompilerParams(has_side_effects=DATAFLOW),
    )(*arrays, *lands, send_sems, recv_sems, after)
    return list(res[n:])


def ada_fwd(name, c_row, w_cat, b_lay):
    d = c_row.shape[1]
    ncol = w_cat.shape[1]
    vmem = pl.BlockSpec(memory_space=pltpu.VMEM)

    def body(c_ref, w_ref, b_ref, mod_ref, cact_ref, token, call, send, land, s1, r1, s2, r2):
        token[...] = jnp.zeros_like(token)
        me = _me()
        mi = _idx(me)
        call[mi] = c_ref[...]

        def exchange(src_of, dst_buf, ssem, rsem):
            sends, recvs = [], []
            for k, f in enumerate(FLIPS):
                peer = _flip(me, f)
                sends.append(pltpu.make_async_remote_copy(
                    src_ref=src_of(peer), dst_ref=dst_buf.at[mi], send_sem=ssem.at[k], recv_sem=rsem.at[k],
                    device_id=peer, device_id_type=MESH))
                recvs.append(pltpu.make_async_remote_copy(
                    src_ref=src_of(peer), dst_ref=dst_buf.at[_idx(peer)], send_sem=ssem.at[k], recv_sem=rsem.at[k],
                    device_id=peer, device_id_type=MESH))
            for cp in sends:
                cp.start()
            for cp in recvs:
                cp.wait_recv()
            for cp in sends:
                cp.wait_send()

        exchange(lambda peer: c_ref, call, s1, r1)
        for p in range(N_DEV):
            cact_ref[pl.ds(p, 1), :] = jax.nn.silu(call[p])
        res = _dg(cact_ref[...], w_ref[...], 1, 0)
        for p in range(N_DEV):
            send[p] = res[p:p + 1, :]
        land[mi] = send[mi]
        exchange(lambda peer: send.at[_idx(peer)], land, s2, r2)
        mod_ref[...] = land[...] + b_ref[...]

    return pl.pallas_call(
        body, name=name, in_specs=[vmem, vmem, vmem], out_specs=[vmem, vmem, vmem],
        out_shape=[jax.ShapeDtypeStruct((N_DEV, 1, ncol), F32), jax.ShapeDtypeStruct((N_DEV, d), F32),
                   jax.ShapeDtypeStruct((8, LANE), F32)],
        scratch_shapes=[pltpu.VMEM((N_DEV, 1, d), F32), pltpu.VMEM((N_DEV, 1, ncol), F32),
                        pltpu.VMEM((N_DEV, 1, ncol), F32),
                        pltpu.SemaphoreType.DMA((7,)), pltpu.SemaphoreType.DMA((7,)),
                        pltpu.SemaphoreType.DMA((7,)), pltpu.SemaphoreType.DMA((7,))],
        compiler_params=pltpu.CompilerParams(vmem_limit_bytes=VMEM_LIMIT),
    )(c_row, w_cat, b_lay)


POOL_WINDOWS = (2, 4, 8, 16)
POOL_GROUP = 64
N_REL = 2 * REL_CLIP + 1
PACK_COLS = 1024
SMALL_NAMES = ["b_ada", "b_gate", "w_pool", "pool_scale", "rel_bias", "conv_w", "conv_b", "conv_ln_g",
               "conv_ln_b", "ln_mix_g", "ln_mix_b", "b_ff1", "b_ff2", "ln_ff_g", "ln_ff_b"]
BIG_NAMES = ["w_in", "w_br_pool", "w_br_attn", "w_br_conv", "w_o", "w_ff1", "w_ff2"]
ROW_SHARDED = ("w_o", "w_ff2")
WEIGHT_NAMES = ["w_ada", "b_ada", "w_in", "b_gate", "w_pool", "pool_scale", "rel_bias", "conv_w", "conv_b",
                "conv_ln_g", "conv_ln_b", "w_br_pool", "w_br_attn", "w_br_conv", "w_o", "ln_mix_g", "ln_mix_b",
                "w_ff1", "b_ff1", "w_ff2", "b_ff2", "ln_ff_g", "ln_ff_b"]


def _perm_cols(w):
    return jnp.concatenate([w[:, 2304:], w[:, 1792:2304], w[:, :256], w[:, 256:768], w[:, 768:1280],
                            w[:, 1280:1792]], axis=1)


def _unperm_cols(wp):
    return jnp.concatenate([wp[:, Z_POOL:Z_Q], wp[:, Z_Q:Z_K], wp[:, Z_K:Z_V], wp[:, Z_V:],
                            wp[:, Z_CONV:Z_POOL], wp[:, :Z_CONV]], axis=1)


def _bias_table(rel_bias):
    far = jnp.broadcast_to(rel_bias[:, 2 * REL_CLIP:], (N_HEADS, BAND - REL_CLIP))
    near = rel_bias[:, REL_CLIP - CHUNK + 1:2 * REL_CLIP][:, ::-1]
    ext = jnp.concatenate([far, near, jnp.zeros((N_HEADS, 1), F32)], axis=1)
    length = BAND + CHUNK
    flat = jnp.tile(ext, (1, CHUNK + 1))
    skew = flat[:, CHUNK - 1:CHUNK - 1 + CHUNK * (length - 1)].reshape(N_HEADS, CHUNK, length - 1)
    return skew[:, :, :BAND]


def _bias_full(rel_bias):
    tab = _bias_table(rel_bias)
    return jnp.concatenate(
        [jnp.pad(tab, ((0, 0), (0, 0), (i * CHUNK, KEY_SPAN - BAND - i * CHUNK)), constant_values=NEG_INF)
         for i in range(CHUNKS_PER_TILE)], axis=1)


def _block_diag(w_pool):
    out = jnp.zeros((D_POOL, D_POOL), F32)
    for g in range(len(POOL_WINDOWS)):
        out = lax.dynamic_update_slice(out, w_pool[g], (g * POOL_GROUP, g * POOL_GROUP))
    return out


def _flat_pad(arrs, mult):
    flat = jnp.concatenate([a.reshape(-1) for a in arrs])
    pad = (-flat.shape[0]) % mult
    return jnp.pad(flat, (0, pad)) if pad else flat


def _unflat(flat, shapes):
    out, off = [], 0
    for shp in shapes:
        n = int(np.prod(shp))
        out.append(flat[off:off + n].reshape(shp))
        off += n
    return out


def _to_blocks(name, full):
    k, n = full.shape
    if name in ROW_SHARDED:
        return full.reshape(N_DEV, k // N_DEV, n)
    return full.reshape(k, N_DEV, n // N_DEV).transpose(1, 0, 2)


def _from_blocks(name, blocks):
    nb, r, c = blocks.shape
    if name in ROW_SHARDED:
        return blocks.reshape(nb * r, c)
    return blocks.transpose(1, 0, 2).reshape(r, nb * c)


class _Layer:
    pass


def _row(v):
    return v.reshape(1, -1)


def _layer_fwd(x, modr, w, wvec, fetch_rest):
    sh_m, sc_m, g_m, sh_f, sc_f, g_f = modr
    (u,) = row_fwd("lnmod_mix", f_lnmod, [x], [sc_m, sh_m], [(D_MODEL, BF16)], 512)
    z = mm_big("mm_in", u, w.w_in, "nn", (512, 896, 1024), F32, j_outer=True)
    p = pool_lin("pool_fwd", (z, D_POOL, Z_POOL // D_POOL), wvec, False, F32)
    ao = attn_fwd("attn_fwd", z, w.bias)
    cv = conv_fwd("conv_fwd", (z, 2 * D_CONV, Z_CONV // (2 * D_CONV)), w.conv_w)
    fetch_rest(w, cv)
    mparams = [w.wbd, w.ps, w.wbp, w.wba, w.wbc, w.cb, w.clg, w.clb, w.bg, w.wo, g_m, w.lmg, w.lmb]
    (x1,) = row_fwd("merge", f_merge, [p, ao, cv, (z, 3 * D_MODEL, 0), x], mparams, [(D_MODEL, F32)], 256)
    (u2,) = row_fwd("lnmod_ff", f_lnmod, [x1], [sc_f, sh_f], [(D_MODEL, BF16)], 512)
    hpre, h = mm_ff1_relu2("mm_ff1", u2, w.w_ff1, w.b1, 512, 1024)
    ff = mm_big("mm_ff2", h, w.w_ff2, "nn", (512, 1024, 4096), F32)
    (x2,) = row_fwd("ffout", f_ffout, [x1, ff], [w.b2, g_f, w.lfg, w.lfb], [(D_MODEL, F32)], 512)
    return x2, (x, u, z, p, ao, cv, x1, u2, hpre, h, ff, mparams)


GRAD_GROUPS = [("ff", ["w_ff2", "w_ff1"]), ("mix", ["w_o", "w_br_pool", "w_br_attn", "w_br_conv"]), ("in", ["w_in"])]


def _layer_bwd(dx2, saved, modr, w, wvec, ready):
    x, u, z, p, ao, cv, x1, u2, hpre, h, ff, mparams = saved
    sh_m, sc_m, g_m, sh_f, sc_f, g_f = modr
    g = {}
    (dx1a, dff), (g["b_ff2"], dgf, g["ln_ff_g"], g["ln_ff_b"]) = row_bwd(
        "ffout_bwd", f_ffout, [x1, ff], [w.b2, g_f, w.lfg, w.lfb], [dx2], 512, [(0, F32), (1, BF16)], [0, 1, 2, 3])
    dhpre, g["b_ff1"] = mm_dh_relu2("mm_dh", dff, w.w_ff2, hpre, w.b1, 512, 1024)
    g["w_ff2"] = mm_big("mm_dw_ff2", h, dff, "tn", (1024, 1024, 2048), BF16)
    du2 = mm_big("mm_du2", dhpre, w.w_ff1, "nt", (512, 1024, 4096), F32)
    g["w_ff1"] = mm_big("mm_dw_ff1", u2, dhpre, "tn", (1024, 1024, 2048), BF16)
    sc_f = sc_f + ready("ff", g)[0:1, 0:1]
    (dx1,), (dscf, dshf) = row_bwd("lnmod_ff_bwd", f_lnmod, [x1], [sc_f, sh_f], [du2], 512, [(0, F32)], [0, 1],
                                   add_to=dx1a)
    (dp, dao, dcv, dzg, dxa), dm = row_bwd(
        "merge_bwd", f_merge, [p, ao, cv, (z, 3 * D_MODEL, 0), x], mparams, [dx1], 256,
        [(0, F32), (1, BF16), (2, F32), (3, BF16), (4, F32)], list(range(13)))
    (dwbd, g["pool_scale"], g["w_br_pool"], g["w_br_attn"], g["w_br_conv"], g["conv_b"], g["conv_ln_g"],
     g["conv_ln_b"], g["b_gate"], g["w_o"], dgm, g["ln_mix_g"], g["ln_mix_b"]) = dm
    g["w_pool"] = jnp.stack([dwbd[i * POOL_GROUP:(i + 1) * POOL_GROUP, i * POOL_GROUP:(i + 1) * POOL_GROUP]
                             for i in range(len(POOL_WINDOWS))])
    tok = ready("mix", g)
    dzp = pool_lin("pool_bwd", (dp, D_POOL, 0), wvec + tok[0:1, 0:1], True, BF16)
    dq, dk, dv, dbias = attn_bwd("attn_bwd", z, w.bias, dao, after=tok)
    (g["rel_bias"],) = w.bias_vjp(dbias)
    dzc, dcw = conv_bwd("conv_bwd", (z, 2 * D_CONV, Z_CONV // (2 * D_CONV)), w.conv_w + tok[0:1, 0:1], dcv)
    g["conv_w"] = dcw[:CONV_WIDTH]
    dz = jnp.concatenate([dzg, dzc, dzp, dq, dk, dv], axis=1)
    du = mm_big("mm_du", dz, w.w_in, "nt", (512, 1024, D_IN), F32)
    (dx,), (dscm, dshm) = row_bwd("lnmod_mix_bwd", f_lnmod, [x], [sc_m, sh_m], [du], 512, [(0, F32)], [0, 1],
                                  add_to=dxa)
    g["dmod"] = jnp.concatenate([dshm, dscm, dgm, dshf, dscf, dgf], axis=1)
    tok = ready("small", g)
    g["w_in"] = _unperm_cols(mm_big("mm_dw_in", u, dz, "tn", (1024, 896, 2048), BF16, after=tok))
    return dx, g, ready("in", g)


def kernel(x, c, w_ada, b_ada, w_in, b_gate, w_pool, pool_scale, rel_bias, conv_w, conv_b, conv_ln_g, conv_ln_b, w_br_pool, w_br_attn, w_br_conv, w_o, ln_mix_g, ln_mix_b, w_ff1, b_ff1, w_ff2, b_ff2, ln_ff_g, ln_ff_b, loss_target, m_w_ada, m_b_ada, m_w_in, m_b_gate, m_w_pool, m_pool_scale, m_rel_bias, m_conv_w, m_conv_b, m_conv_ln_g, m_conv_ln_b, m_w_br_pool, m_w_br_attn, m_w_br_conv, m_w_o, m_ln_mix_g, m_ln_mix_b, m_w_ff1, m_b_ff1, m_w_ff2, m_b_ff2, m_ln_ff_g, m_ln_ff_b, v_w_ada, v_b_ada, v_w_in, v_b_gate, v_w_pool, v_pool_scale, v_rel_bias, v_conv_w, v_conv_b, v_conv_ln_g, v_conv_ln_b, v_w_br_pool, v_w_br_attn, v_w_br_conv, v_w_o, v_ln_mix_g, v_ln_mix_b, v_w_ff1, v_b_ff1, v_w_ff2, v_b_ff2, v_ln_ff_g, v_ln_ff_b):
    args = dict(locals())
    wts = {n: args[n] for n in WEIGHT_NAMES}
    mom = {n: args["m_" + n] for n in WEIGHT_NAMES}
    var = {n: args["v_" + n] for n in WEIGHT_NAMES}
    me = 4 * lax.axis_index("x") + 2 * lax.axis_index("y") + lax.axis_index("c")
    xs, tgt = x[0], loss_target[0]
    nc_ada = w_ada.shape[2]
    wvec = jnp.asarray(np.repeat(np.array(POOL_WINDOWS, np.float32), POOL_GROUP)[None, :])

    w_cat = jnp.concatenate([w_ada[l] for l in range(DEPTH)], axis=1)
    b_lay = b_ada.reshape(DEPTH, N_DEV, nc_ada).transpose(1, 0, 2).reshape(N_DEV, 1, DEPTH * nc_ada)
    land, cact, ada_token = ada_fwd("ada_fwd", c, w_cat, b_lay)
    mod = land.reshape(N_DEV, DEPTH, nc_ada).transpose(1, 0, 2).reshape(DEPTH, 6 * D_MODEL)
    modr = [[mod[l:l + 1, i * D_MODEL:(i + 1) * D_MODEL] for i in range(6)] for l in range(DEPTH)]

    cw_pack = _flat_pad([conv_w], 8 * LANE).reshape(-1, LANE) + ada_token[0:1, 0:1]
    shards =[[wts[n][l].astype(BF16) for n in BIG_NAMES] for l in range(DEPTH)]
    g_in0, cw_all, token = all_gather_multi("gather_w_in_l0", [shards[0][0], cw_pack])
    cw_all = cw_all.reshape(N_DEV, -1)[:, :conv_w.size]
    conv_full = cw_all.reshape((N_DEV,) + conv_w.shape).transpose(1, 2, 0, 3).reshape(DEPTH, CONV_WIDTH, D_CONV)
    gathers = {}
    for key, arrs in [("rest_l0", shards[0][1:]), ("w_in_l1", shards[1][:1]), ("rest_l1", shards[1][1:])]:
        arrs = [a + token[0:1, 0:1].astype(a.dtype) for a in arrs]
        ssem, rsem, thru, lands, token = exchange_start(f"gather_{key}_start", arrs, True)
        gathers[key] = (arrs, ssem, rsem, thru, lands)
    modr[0] = [r + token[0:1, 0:1] for r in modr[0]]

    def with_own(lands_, own):
        return [lax.dynamic_update_index_in_dim(ld, o, me, axis=0) for ld, o in zip(lands_, own)]

    def gathered(key, after):
        arrs, ssem, rsem, thru, lands = gathers[key]
        return with_own(exchange_wait(f"gather_{key}_wait", ssem, rsem, thru, lands, after, True), arrs)

    def fetch_rest_for(l):
        def fetch_rest(w, after):
            full = {n: _from_blocks(n, g) for n, g in zip(BIG_NAMES[1:], gathered(f"rest_l{l}", after))}
            w.wbp, w.wba, w.wbc = full["w_br_pool"], full["w_br_attn"], full["w_br_conv"]
            w.wo, w.w_ff1, w.w_ff2 = full["w_o"], full["w_ff1"], full["w_ff2"]
        return fetch_rest

    def layer_weights(l, in_blocks):
        w = _Layer()
        w.w_in = _perm_cols(_from_blocks("w_in", in_blocks))
        w.wbd = _block_diag(w_pool[l])
        w.ps, w.cb, w.clg, w.clb = _row(pool_scale[l]), _row(conv_b[l]), _row(conv_ln_g[l]), _row(conv_ln_b[l])
        w.bg, w.lmg, w.lmb = _row(b_gate[l]), _row(ln_mix_g[l]), _row(ln_mix_b[l])
        w.b1, w.b2, w.lfg, w.lfb = _row(b_ff1[l]), _row(b_ff2[l]), _row(ln_ff_g[l]), _row(ln_ff_b[l])
        w.conv_w = jnp.pad(conv_full[l], ((0, CONV_PAD - CONV_WIDTH), (0, 0)))
        w.bias, w.bias_vjp = jax.vjp(_bias_full, rel_bias[l])
        return w

    layers, saved = [layer_weights(0, g_in0)], []
    h, sv = _layer_fwd(xs, modr[0], layers[0], wvec, fetch_rest_for(0))
    saved.append(sv)
    layers.append(layer_weights(1, gathered("w_in_l1", h)[0]))
    h, sv = _layer_fwd(h, modr[1], layers[1], wvec, fetch_rest_for(1))
    saved.append(sv)
    lpart, dy = loss_head("loss_head", h, tgt, 512)
    loss = lax.psum(lpart[0, 0], ("x", "y", "c"))
    grads, dmods = [None] * DEPTH, [None] * DEPTH
    pending = {}
    small_shapes = [wts[n].shape if n != "conv_w" else (DEPTH, CONV_WIDTH, D_CONV) for n in SMALL_NAMES]

    def ready_for(l):
        def ready(group, g):
            if group == "small":
                if l > 0:
                    return None
                both = [g, grads[1]]
                local = [jnp.concatenate([both[k]["dmod"] for k in range(DEPTH)], axis=0)]
                local += [jnp.stack([both[k][n].reshape(shp[1:]) for k in range(DEPTH)])
                          for n, shp in zip(SMALL_NAMES[1:], small_shapes[1:])]
                pack = _flat_pad(local, 8 * LANE).reshape(-1, LANE)
                ssem, rsem, thru, lands, token = exchange_start("gather_small_grads_start", [pack], True)
                pending["small"] = (pack, ssem, rsem, thru, lands)
                return token
            names = dict(GRAD_GROUPS)[group]
            blocks = [_to_blocks(n, g[n]).astype(BF16) for n in names]
            ssem, rsem, thru, lands, token = exchange_start(f"scatter_l{l}_{group}_start", blocks, False)
            pending[(l, group)] = (names, blocks, ssem, rsem, thru, lands)
            return token
        return ready

    def received(l, group, after):
        names, blocks, ssem, rsem, thru, lands = pending[(l, group)]
        lands = exchange_wait(f"scatter_l{l}_{group}_wait", ssem, rsem, thru, lands, after, False)
        own = [lax.dynamic_index_in_dim(b, me, axis=0, keepdims=False) for b in blocks]
        return dict(zip(names, with_own(lands, own)))

    dy, grads[1], token = _layer_bwd(dy, saved[1], modr[1], layers[1], wvec, ready_for(1))
    modr0 = [r + token[0:1, 0:1] for r in modr[0]]
    dy, grads[0], token = _layer_bwd(dy, saved[0], modr0, layers[0], wvec, ready_for(0))
    recv = [{} for _ in range(DEPTH)]
    for l, group in [(1, "ff"), (1, "mix"), (1, "in"), (0, "ff"), (0, "mix")]:
        recv[l].update(received(l, group, token))
    grad_x = dy[None]
    dmod_size = DEPTH * 6 * D_MODEL

    small_pack, ssem, rsem, thru, lands = pending["small"]
    (small_all,) = with_own(exchange_wait("gather_small_grads_wait", ssem, rsem, thru, lands, token, True), [small_pack])
    small_sum = sum_blocks("sum_small_grads", small_all, small_pack.shape[0]).reshape(-1)
    gsmall = dict(zip(SMALL_NAMES, _unflat(small_sum, small_shapes)))
    gw = dict(gsmall)
    gw["conv_w"] = lax.dynamic_slice_in_dim(gsmall["conv_w"], me * conv_w.shape[2], conv_w.shape[2], axis=2)

    dmod_all = small_all.reshape(N_DEV, -1)[:, :dmod_size].reshape(N_DEV, DEPTH, N_DEV, nc_ada)
    dm_mine = lax.dynamic_index_in_dim(dmod_all, me, axis=2, keepdims=False).reshape(N_DEV, DEPTH * nc_ada)
    cact_t = jnp.pad(cact.T, ((0, 0), (0, LANE - N_DEV)))
    dm_pad = jnp.pad(dm_mine, ((0, LANE - N_DEV), (0, 0)))
    dw_cat = mm_big("mm_dw_ada", cact_t, dm_pad, "nn", (D_MODEL, DEPTH * nc_ada, LANE), F32)
    gw["w_ada"] = jnp.stack([dw_cat[:, l * nc_ada:(l + 1) * nc_ada] for l in range(DEPTH)])

    delta, new_m, new_v = {}, {}, {}
    for n in ["w_ada", "w_ff2", "w_ff1", "w_o", "w_br_pool", "w_br_attn", "w_br_conv", "w_in"]:
        shp = wts[n].shape
        two_d = lambda a, shp=shp: a.reshape(shp[0] * shp[1], shp[2])
        tr = min(256, shp[1])
        if n == "w_ada":
            res = (gw[n],) + tuple(adamw("adamw_" + n, two_d(wts[n]), two_d(gw[n]), two_d(mom[n]), two_d(var[n]), tr))
        else:
            if n == "w_in":
                recv[0].update(received(0, "in", new_v["w_br_conv"]))
            res = adamw_sum("adamw_" + n, two_d(wts[n]), [recv[l][n] for l in range(DEPTH)], two_d(mom[n]),
                            two_d(var[n]), tr)
        gw[n], delta[n], new_m[n], new_v[n] = [a.reshape(shp) for a in res]
    packs = [_flat_pad([src[n] for n in SMALL_NAMES], 8 * LANE).reshape(-1, LANE) for src in (wts, gw, mom, var)]
    res = adamw("adamw_small", *packs, packs[0].shape[0])
    shapes = [wts[n].shape for n in SMALL_NAMES]
    for out, flat in zip((delta, new_m, new_v), res):
        out.update(dict(zip(SMALL_NAMES, _unflat(flat.reshape(-1), shapes))))

    return (loss, grad_x, *[gw[n] for n in WEIGHT_NAMES], *[delta[n] for n in WEIGHT_NAMES],
            *[new_m[n] for n in WEIGHT_NAMES], *[new_v[n] for n in WEIGHT_NAMES])
```

```python
import functools

import jax
import jax.numpy as jnp
import numpy as np
from jax import lax
from jax.experimental import pallas as pl
from jax.experimental.pallas import tpu as pltpu

F32 = jnp.float32
BF16 = jnp.bfloat16
MESH = pl.DeviceIdType.MESH

D_MODEL = 1024
DEPTH = 2
CHUNK = 64
N_HEADS = 8
HEAD_DIM = 64
D_POOL = 256
D_ATTN = 512
D_CONV = 256
CONV_WIDTH = 31
D_FF = 4096
D_IN = 5376
N_PREV = 8
BAND = (N_PREV + 1) * CHUNK
REL_CLIP = 128
ALPHA = (2.0 * DEPTH) ** 0.25
LN_EPS = 1e-5
NEG_INF = -1e30
N_DEV = 8

ADAM_LR, ADAM_B1, ADAM_B2, ADAM_EPS, ADAM_WD, ADAM_STEP = 0.001, 0.9, 0.999, 1e-08, 0.01, 10

VMEM_LIMIT = 56 * 1024 * 1024

Z_GATE, Z_CONV, Z_POOL, Z_Q, Z_K, Z_V = 0, 3072, 3584, 3840, 4352, 4864
ATT_TILE = 512
LANE = 128


def _dg(a, b, ca, cb):
    return lax.dot_general(a.astype(BF16), b.astype(BF16), (((ca,), (cb,)), ((), ())),
                           preferred_element_type=F32)


@jax.custom_vjp
def mm_nn(a, b):
    return _dg(a, b, 1, 0)


def _mm_nn_fwd(a, b):
    return _dg(a, b, 1, 0), (a, b)


def _mm_nn_bwd(res, g):
    a, b = res
    return _dg(g, b, 1, 1).astype(a.dtype), _dg(a, g, 0, 0).astype(b.dtype)


mm_nn.defvjp(_mm_nn_fwd, _mm_nn_bwd)


@jax.custom_vjp
def mm_nt(a, b):
    return _dg(a, b, 1, 1)


def _mm_nt_fwd(a, b):
    return _dg(a, b, 1, 1), (a, b)


def _mm_nt_bwd(res, g):
    a, b = res
    return _dg(g, b, 1, 0).astype(a.dtype), _dg(g, a, 0, 0).astype(b.dtype)


mm_nt.defvjp(_mm_nt_fwd, _mm_nt_bwd)


@jax.custom_vjp
def mm_nn_shadow(a, w, shadow):
    return _dg(a, w, 1, 0)


def _mm_nn_shadow_fwd(a, w, shadow):
    return _dg(a, w, 1, 0), (a, w)


def _mm_nn_shadow_bwd(res, g):
    a, w = res
    return _dg(g, w, 1, 1).astype(a.dtype), jnp.zeros_like(w), _dg(a, g, 0, 0)


mm_nn_shadow.defvjp(_mm_nn_shadow_fwd, _mm_nn_shadow_bwd)


def mm_w(a, w):
    return mm_nn_shadow(a, w[0], w[1]) if isinstance(w, tuple) else mm_nn(a, w)


def _ln(x):
    mu = jnp.mean(x, axis=-1, keepdims=True)
    xc = x - mu
    var = jnp.mean(xc * xc, axis=-1, keepdims=True)
    return xc * lax.rsqrt(var + LN_EPS)


def _norm_rows(rows):
    return [r if isinstance(r, tuple) else (r, r.shape[1], 0) for r in rows]


def _row_spec(tm, r):
    _, width, cb = r
    return pl.BlockSpec((tm, width), lambda i, cb=cb: (i, cb))


def _full_spec(a):
    nd = a.ndim
    return pl.BlockSpec(a.shape, lambda i, nd=nd: (0,) * nd)


def row_fwd(name, f, rows, params, outs, tm):
    rows = _norm_rows(rows)
    s = rows[0][0].shape[0]
    nr, npar = len(rows), len(params)

    def body(*refs):
        r = [x[...].astype(F32) for x in refs[:nr]]
        p = [x[...] for x in refs[nr:nr + npar]]
        res = f(*r, *p)
        for o_ref, o in zip(refs[nr + npar:], res):
            o_ref[...] = o.astype(o_ref.dtype)

    return pl.pallas_call(
        body, name=name, grid=(s // tm,),
        in_specs=[_row_spec(tm, r) for r in rows] + [_full_spec(p) for p in params],
        out_specs=[pl.BlockSpec((tm, w), lambda i: (i, 0)) for w, _ in outs],
        out_shape=[jax.ShapeDtypeStruct((s, w), dt) for w, dt in outs],
        compiler_params=pltpu.CompilerParams(dimension_semantics=("parallel",), vmem_limit_bytes=VMEM_LIMIT),
    )(*[r[0] for r in rows], *params)


def row_bwd(name, f, rows, params, douts, tm, want_rows, want_params, add_to=None):
    rows = _norm_rows(rows)
    s = rows[0][0].shape[0]
    nr, npar, nd = len(rows), len(params), len(douts)
    nadd = 0 if add_to is None else 1
    n_in = nr + npar + nd + nadd

    def body(*refs):
        i = pl.program_id(0)
        r = [x[...].astype(F32) for x in refs[:nr]]
        p = [(x[...], jnp.zeros(x.shape, F32)) if x.dtype == BF16 else x[...] for x in refs[nr:nr + npar]]
        d = [x[...].astype(F32) for x in refs[nr + npar:nr + npar + nd]]
        _, vjp = jax.vjp(f, *r, *p)
        g = vjp(tuple(d))
        out_refs = refs[n_in:]
        for k, (idx, _) in enumerate(want_rows):
            val = g[idx]
            if nadd and k == 0:
                val = val + refs[n_in - 1][...].astype(F32)
            out_refs[k][...] = val.astype(out_refs[k].dtype)
        for k, idx in enumerate(want_params):
            gp = g[nr + idx]
            gp = gp[1] if isinstance(gp, tuple) else gp
            o_ref = out_refs[len(want_rows) + k]

            @pl.when(i == 0)
            def _():
                o_ref[...] = gp

            @pl.when(i > 0)
            def _():
                o_ref[...] += gp

    in_specs = ([_row_spec(tm, r) for r in rows] + [_full_spec(p) for p in params]
                + [pl.BlockSpec((tm, d.shape[1]), lambda i: (i, 0)) for d in douts])
    args = [r[0] for r in rows] + list(params) + list(douts)
    if nadd:
        in_specs.append(pl.BlockSpec((tm, add_to.shape[1]), lambda i: (i, 0)))
        args.append(add_to)
    out_specs = ([pl.BlockSpec((tm, rows[idx][1]), lambda i: (i, 0)) for idx, _ in want_rows]
                 + [_full_spec(params[idx]) for idx in want_params])
    out_shape = ([jax.ShapeDtypeStruct((s, rows[idx][1]), dt) for idx, dt in want_rows]
                 + [jax.ShapeDtypeStruct(params[idx].shape, F32) for idx in want_params])
    res = pl.pallas_call(
        body, name=name, grid=(s // tm,), in_specs=in_specs, out_specs=out_specs, out_shape=out_shape,
        compiler_params=pltpu.CompilerParams(dimension_semantics=("arbitrary",), vmem_limit_bytes=VMEM_LIMIT),
    )(*args)
    return res[:len(want_rows)], res[len(want_rows):]


def f_lnmod(x, sc, sh):
    return (_ln(x) * (1.0 + sc) + sh,)


def f_merge(p, ao, cv, zg, x, wbd, ps, wbp, wba, wbc, cb, clg, clb, bg, wo, gm, lg, lb):
    pm = mm_w(p, wbd) * ps
    co = jax.nn.silu(_ln(cv + cb) * clg + clb)
    y_pool = mm_w(pm, wbp)
    y_attn = mm_w(ao, wba)
    y_conv = mm_w(co, wbc)
    gates = jax.nn.sigmoid(zg + bg)
    merged = (gates[:, :D_MODEL] * y_pool + gates[:, D_MODEL:2 * D_MODEL] * y_attn
              + gates[:, 2 * D_MODEL:] * y_conv)
    mix = mm_w(merged, wo)
    return (_ln(ALPHA * x + gm * mix) * lg + lb,)


def f_relu2(hpre, b1):
    a = jax.nn.relu(hpre + b1)
    return (a * a,)


def f_ffout(x1, ff, b2, gf, lg, lb):
    return (_ln(ALPHA * x1 + gf * (ff + b2)) * lg + lb,)


def mm_big(name, a, b, kind, tiles, out_dtype, j_outer=False, after=None):
    if kind == "nn":
        o0, o1, red = a.shape[0], b.shape[1], a.shape[1]
    elif kind == "nt":
        o0, o1, red = a.shape[0], b.shape[0], a.shape[1]
    else:
        o0, o1, red = a.shape[1], b.shape[1], a.shape[0]
    t0, t1, tr = min(tiles[0], o0), min(tiles[1], o1), min(tiles[2], red)
    if kind == "nn":
        a_spec = pl.BlockSpec((t0, tr), lambda i, j, r: (i, r))
        b_spec = pl.BlockSpec((tr, t1), lambda i, j, r: (r, j))
        dims = (1, 0)
    elif kind == "nt":
        a_spec = pl.BlockSpec((t0, tr), lambda i, j, r: (i, r))
        b_spec = pl.BlockSpec((t1, tr), lambda i, j, r: (j, r))
        dims = (1, 1)
    else:
        a_spec = pl.BlockSpec((tr, t0), lambda i, j, r: (r, i))
        b_spec = pl.BlockSpec((tr, t1), lambda i, j, r: (r, j))
        dims = (0, 0)
    assert o0 % t0 == 0 and o1 % t1 == 0 and red % tr == 0, (name, a.shape, b.shape, tiles)
    n0, n1, nred = o0 // t0, o1 // t1, red // tr
    o_spec = pl.BlockSpec((t0, t1), lambda i, j, r: (i, j))
    if j_outer:
        swap = lambda spec: pl.BlockSpec(spec.block_shape, lambda j, i, r, f=spec.index_map: f(i, j, r))
        a_spec, b_spec, o_spec = swap(a_spec), swap(b_spec), swap(o_spec)
        grid = (n1, n0, nred)
    else:
        grid = (n0, n1, nred)

    deps = [] if after is None else [after]
    dep_specs = [pl.BlockSpec(d.shape, lambda i, j, r, nd=d.ndim: (0,) * nd) for d in deps]
    if nred == 1:
        def body(a_ref, b_ref, *rest):
            o_ref = rest[len(deps)]
            o_ref[...] = _dg(a_ref[...], b_ref[...], *dims).astype(o_ref.dtype)
        scratch = []
    else:
        def body(a_ref, b_ref, *rest):
            o_ref, acc_ref = rest[len(deps):]
            r = pl.program_id(2)
            part = _dg(a_ref[...], b_ref[...], *dims)

            @pl.when(r == 0)
            def _():
                acc_ref[...] = part

            @pl.when(jnp.logical_and(r > 0, r < nred - 1))
            def _():
                acc_ref[...] += part

            @pl.when(r == nred - 1)
            def _():
                o_ref[...] = (acc_ref[...] + part).astype(o_ref.dtype)
        scratch = [pltpu.VMEM((t0, t1), F32)]

    return pl.pallas_call(
        body, name=name, grid=grid,
        in_specs=[a_spec, b_spec] + dep_specs,
        out_specs=o_spec,
        out_shape=jax.ShapeDtypeStruct((o0, o1), out_dtype),
        scratch_shapes=scratch,
        compiler_params=pltpu.CompilerParams(dimension_semantics=("parallel", "parallel", "arbitrary"),
                                             vmem_limit_bytes=VMEM_LIMIT),
    )(a, b, *deps)


def mm_ff1_relu2(name, u2, w1, b1, tm, tn):
    m, k = u2.shape
    n = w1.shape[1]
    tm, tn = min(tm, m), min(tn, n)

    def body(a_ref, b_ref, bias_ref, hpre_ref, h_ref):
        acc = _dg(a_ref[...], b_ref[...], 1, 0)
        hpre_ref[...] = acc
        h_ref[...] = f_relu2(acc, bias_ref[...])[0].astype(h_ref.dtype)

    out = pl.BlockSpec((tm, tn), lambda j, i: (i, j))
    return pl.pallas_call(
        body, name=name, grid=(n // tn, m // tm),
        in_specs=[pl.BlockSpec((tm, k), lambda j, i: (i, 0)), pl.BlockSpec((k, tn), lambda j, i: (0, j)),
                  pl.BlockSpec((1, tn), lambda j, i: (0, j))],
        out_specs=[out, out],
        out_shape=[jax.ShapeDtypeStruct((m, n), F32), jax.ShapeDtypeStruct((m, n), BF16)],
        compiler_params=pltpu.CompilerParams(dimension_semantics=("parallel", "parallel"), vmem_limit_bytes=VMEM_LIMIT),
    )(u2, w1, b1)


def mm_dh_relu2(name, dff, w2, hpre, b1, tm, tn):
    m, k = dff.shape
    n = w2.shape[0]
    tm, tn = min(tm, m), min(tn, n)

    def body(a_ref, b_ref, hpre_ref, bias_ref, d_ref, db_ref):
        i = pl.program_id(1)
        dh = _dg(a_ref[...], b_ref[...], 1, 1)
        _, vjp = jax.vjp(f_relu2, hpre_ref[...], bias_ref[...])
        dhpre, db = vjp((dh,))
        d_ref[...] = dhpre.astype(d_ref.dtype)

        @pl.when(i == 0)
        def _():
            db_ref[...] = db

        @pl.when(i > 0)
        def _():
            db_ref[...] += db

    tile = pl.BlockSpec((tm, tn), lambda j, i: (i, j))
    col = pl.BlockSpec((1, tn), lambda j, i: (0, j))
    return pl.pallas_call(
        body, name=name, grid=(n // tn, m // tm),
        in_specs=[pl.BlockSpec((tm, k), lambda j, i: (i, 0)), pl.BlockSpec((tn, k), lambda j, i: (j, 0)), tile, col],
        out_specs=[tile, col],
        out_shape=[jax.ShapeDtypeStruct((m, n), BF16), jax.ShapeDtypeStruct((1, n), F32)],
        compiler_params=pltpu.CompilerParams(dimension_semantics=("parallel", "arbitrary"), vmem_limit_bytes=VMEM_LIMIT),
    )(dff, w2, hpre, b1)


POOL_PAD = 16
POOL_ROWS = 256


def pool_lin(name, x, wvec, transpose, out_dtype):
    arr, width, cb = x
    s = arr.shape[0]
    n_steps = s // POOL_ROWS

    def body(x_ref, w_ref, o_ref, xp_ref):
        wv = w_ref[...]
        zeros = jnp.zeros((POOL_PAD, width), F32)
        xp_ref[0:POOL_PAD, :] = zeros
        xp_ref[s + POOL_PAD:s + 2 * POOL_PAD, :] = zeros

        def count(t0):
            t = lax.broadcasted_iota(jnp.int32, (POOL_ROWS, width), 0) + (t0 + 1)
            return jnp.minimum(t.astype(F32), wv)

        def fill(i, carry):
            t0 = pl.multiple_of(i * POOL_ROWS, POOL_ROWS)
            v = x_ref[pl.ds(t0, POOL_ROWS), :].astype(F32)
            if transpose:
                v = v / count(t0)
            xp_ref[pl.ds(t0 + POOL_PAD, POOL_ROWS), :] = v
            return carry

        lax.fori_loop(0, n_steps, fill, 0)

        def step(i, carry):
            t0 = pl.multiple_of(i * POOL_ROWS, POOL_ROWS)
            win = xp_ref[pl.ds(t0, POOL_ROWS + 2 * POOL_PAD), :]
            acc = jnp.zeros((POOL_ROWS, width), F32)
            for j in range(POOL_PAD):
                off = POOL_PAD + j if transpose else POOL_PAD - j
                acc = acc + jnp.where(wv > j, win[off:off + POOL_ROWS, :], 0.0)
            cur = x_ref[pl.ds(t0, POOL_ROWS), :].astype(F32)
            res = acc - cur if transpose else acc / count(t0) - cur
            o_ref[pl.ds(t0, POOL_ROWS), :] = res.astype(o_ref.dtype)
            return carry

        lax.fori_loop(0, n_steps, step, 0)

    return pl.pallas_call(
        body, name=name, grid=(1,),
        in_specs=[pl.BlockSpec((s, width), lambda i, cb=cb: (0, cb)), pl.BlockSpec((1, width), lambda i: (0, 0))],
        out_specs=pl.BlockSpec((s, width), lambda i: (0, 0)),
        out_shape=jax.ShapeDtypeStruct((s, width), out_dtype),
        scratch_shapes=[pltpu.VMEM((s + 2 * POOL_PAD, width), F32)],
        compiler_params=pltpu.CompilerParams(dimension_semantics=("arbitrary",), vmem_limit_bytes=VMEM_LIMIT),
    )(arr, wvec)


CONV_PAD = 32
CONV_ROWS = 128


def _glu(a, g):
    return a * jax.nn.sigmoid(g)


def conv_fwd(name, zc, w):
    arr, width, cb = zc
    s = arr.shape[0]
    n_steps = s // CONV_ROWS
    lead = CONV_PAD - (CONV_WIDTH - 1)

    def body(z_ref, w_ref, o_ref, hp_ref):
        hp_ref[0:CONV_PAD, :] = jnp.zeros((CONV_PAD, D_CONV), F32)

        def fill(i, carry):
            t0 = pl.multiple_of(i * CONV_ROWS, CONV_ROWS)
            z = z_ref[pl.ds(t0, CONV_ROWS), :]
            hp_ref[pl.ds(t0 + CONV_PAD, CONV_ROWS), :] = _glu(z[:, :D_CONV], z[:, D_CONV:])
            return carry

        lax.fori_loop(0, n_steps, fill, 0)
        wv = w_ref[...]

        def step(i, carry):
            t0 = pl.multiple_of(i * CONV_ROWS, CONV_ROWS)
            win = hp_ref[pl.ds(t0, CONV_ROWS + CONV_PAD), :]
            acc = jnp.zeros((CONV_ROWS, D_CONV), F32)
            for k in range(CONV_WIDTH):
                acc = acc + wv[k:k + 1, :] * win[lead + k:lead + k + CONV_ROWS, :]
            o_ref[pl.ds(t0, CONV_ROWS), :] = acc
            return carry

        lax.fori_loop(0, n_steps, step, 0)

    return pl.pallas_call(
        body, name=name, grid=(1,),
        in_specs=[pl.BlockSpec((s, width), lambda i, cb=cb: (0, cb)), pl.BlockSpec(w.shape, lambda i: (0, 0))],
        out_specs=pl.BlockSpec((s, D_CONV), lambda i: (0, 0)),
        out_shape=jax.ShapeDtypeStruct((s, D_CONV), F32),
        scratch_shapes=[pltpu.VMEM((s + CONV_PAD, D_CONV), F32)],
        compiler_params=pltpu.CompilerParams(dimension_semantics=("arbitrary",), vmem_limit_bytes=VMEM_LIMIT),
    )(arr, w)


def conv_bwd(name, zc, w, dout):
    arr, width, cb = zc
    s = arr.shape[0]
    n_steps = s // CONV_ROWS
    lead = CONV_PAD - (CONV_WIDTH - 1)

    def body(z_ref, w_ref, d_ref, dz_ref, dw_ref, hp_ref, dp_ref):
        hp_ref[0:CONV_PAD, :] = jnp.zeros((CONV_PAD, D_CONV), F32)
        dp_ref[s:s + CONV_PAD, :] = jnp.zeros((CONV_PAD, D_CONV), F32)
        dw_ref[...] = jnp.zeros(dw_ref.shape, F32)

        def fill(i, carry):
            t0 = pl.multiple_of(i * CONV_ROWS, CONV_ROWS)
            z = z_ref[pl.ds(t0, CONV_ROWS), :]
            hp_ref[pl.ds(t0 + CONV_PAD, CONV_ROWS), :] = _glu(z[:, :D_CONV], z[:, D_CONV:])
            dp_ref[pl.ds(t0, CONV_ROWS), :] = d_ref[pl.ds(t0, CONV_ROWS), :]
            return carry

        lax.fori_loop(0, n_steps, fill, 0)
        wv = w_ref[...]

        def step(i, carry):
            t0 = pl.multiple_of(i * CONV_ROWS, CONV_ROWS)
            hwin = hp_ref[pl.ds(t0, CONV_ROWS + CONV_PAD), :]
            dwin = dp_ref[pl.ds(t0, CONV_ROWS + CONV_PAD), :]
            dcur = dwin[0:CONV_ROWS, :]
            dh = jnp.zeros((CONV_ROWS, D_CONV), F32)
            rows = []
            for k in range(CONV_WIDTH):
                rows.append(jnp.sum(dcur * hwin[lead + k:lead + k + CONV_ROWS, :], axis=0, keepdims=True))
                back = CONV_WIDTH - 1 - k
                dh = dh + wv[k:k + 1, :] * dwin[back:back + CONV_ROWS, :]
            rows.append(jnp.zeros((1, D_CONV), F32))
            dw_ref[...] += jnp.concatenate(rows, axis=0)
            z = z_ref[pl.ds(t0, CONV_ROWS), :]
            _, vjp = jax.vjp(_glu, z[:, :D_CONV], z[:, D_CONV:])
            da, dg = vjp(dh)
            dz_ref[pl.ds(t0, CONV_ROWS), :] = jnp.concatenate([da, dg], axis=1).astype(dz_ref.dtype)
            return carry

        lax.fori_loop(0, n_steps, step, 0)

    return pl.pallas_call(
        body, name=name, grid=(1,),
        in_specs=[pl.BlockSpec((s, width), lambda i, cb=cb: (0, cb)), pl.BlockSpec(w.shape, lambda i: (0, 0)),
                  pl.BlockSpec((s, D_CONV), lambda i: (0, 0))],
        out_specs=[pl.BlockSpec((s, width), lambda i: (0, 0)), pl.BlockSpec(w.shape, lambda i: (0, 0))],
        out_shape=[jax.ShapeDtypeStruct((s, width), BF16), jax.ShapeDtypeStruct(w.shape, F32)],
        scratch_shapes=[pltpu.VMEM((s + CONV_PAD, D_CONV), F32), pltpu.VMEM((s + CONV_PAD, D_CONV), F32)],
        compiler_params=pltpu.CompilerParams(dimension_semantics=("arbitrary",), vmem_limit_bytes=VMEM_LIMIT),
    )(arr, w, dout)


HEADS_PER_STEP = LANE // HEAD_DIM
CHUNKS_PER_TILE = ATT_TILE // CHUNK
KEY_BLOCKS = N_PREV * CHUNK // ATT_TILE + 1
KEY_SPAN = KEY_BLOCKS * ATT_TILE


def _attn_tile(q, *rest, missing_cols):
    kcat = jnp.concatenate(rest[:KEY_BLOCKS], axis=0)
    vcat = jnp.concatenate(rest[KEY_BLOCKS:2 * KEY_BLOCKS], axis=0)
    bias = rest[2 * KEY_BLOCKS]
    lane = lax.broadcasted_iota(jnp.int32, (1, LANE), 1)
    col = lax.broadcasted_iota(jnp.int32, (1, KEY_SPAN), 1)
    missing = col < missing_cols
    qs = q * (HEAD_DIM ** -0.5)
    o = jnp.zeros((ATT_TILE, LANE), F32)
    for h in range(HEADS_PER_STEP):
        in_head = jnp.logical_and(lane >= h * HEAD_DIM, lane < (h + 1) * HEAD_DIM)
        sc = mm_nt(jnp.where(in_head, qs, 0.0), kcat) + bias[h]
        sc = jnp.where(missing, NEG_INF, sc)
        m = jnp.max(sc, axis=-1, keepdims=True)
        e = jnp.exp(sc - lax.stop_gradient(m))
        p = e / jnp.sum(e, axis=-1, keepdims=True)
        o = o + jnp.where(in_head, mm_nn(p, vcat), 0.0)
    return o


def _missing_cols(n):
    return jnp.maximum((KEY_BLOCKS - 1 - n) * ATT_TILE, 0)


def _attn_in_specs(nt):
    def spec(col0, back):
        return pl.BlockSpec((ATT_TILE, LANE),
                            lambda hp, n, col0=col0, back=back: (jnp.clip(n - back, 0, nt - 1), col0 // LANE + hp))
    backs = list(range(KEY_BLOCKS - 1, -1, -1))
    return ([spec(Z_Q, 0)] + [spec(Z_K, b) for b in backs] + [spec(Z_V, b) for b in backs]
            + [pl.BlockSpec((HEADS_PER_STEP, ATT_TILE, KEY_SPAN), lambda hp, n: (hp, 0, 0))])


def attn_fwd(name, z, bias):
    s = z.shape[0]
    nt = s // ATT_TILE
    n_in = 2 + 2 * KEY_BLOCKS

    def body(*refs):
        o_ref = refs[n_in]
        vals = [r[...] for r in refs[:n_in]]
        o = _attn_tile(*vals, missing_cols=_missing_cols(pl.program_id(1)))
        o_ref[...] = o.astype(o_ref.dtype)

    return pl.pallas_call(
        body, name=name, grid=(N_HEADS // HEADS_PER_STEP, nt),
        in_specs=_attn_in_specs(nt),
        out_specs=pl.BlockSpec((ATT_TILE, LANE), lambda hp, n: (n, hp)),
        out_shape=jax.ShapeDtypeStruct((s, D_ATTN), BF16),
        compiler_params=pltpu.CompilerParams(dimension_semantics=("parallel", "parallel"), vmem_limit_bytes=VMEM_LIMIT),
    )(*([z] * (n_in - 1)), bias)


def attn_bwd(name, z, bias, do, after=None):
    s = z.shape[0]
    nt = s // ATT_TILE
    n_in = 2 + 2 * KEY_BLOCKS
    nc = KEY_BLOCKS - 1

    deps = [] if after is None else [after]
    n_out = n_in + 1 + len(deps)

    def body(*refs):
        do_ref = refs[n_in]
        dq_ref, dk_ref, dv_ref, db_ref = refs[n_out:n_out + 4]
        kacc, vacc = refs[n_out + 4:n_out + 4 + nc], refs[n_out + 4 + nc:]
        n = pl.program_id(1)

        @pl.when(n == 0)
        def _():
            db_ref[...] = jnp.zeros(db_ref.shape, F32)
            for acc in (*kacc, *vacc):
                acc[...] = jnp.zeros(acc.shape, F32)

        def shift(out_ref, accs, contrib):
            @pl.when(n >= nc)
            def _():
                first = accs[0][...] if contrib is None else accs[0][...] + contrib[0]
                out_ref[...] = first.astype(out_ref.dtype)
            for j in range(nc - 1):
                accs[j][...] = accs[j + 1][...] if contrib is None else accs[j + 1][...] + contrib[j + 1]
            if contrib is not None:
                accs[nc - 1][...] = contrib[nc]

        @pl.when(n < nt)
        def _():
            fn = functools.partial(_attn_tile, missing_cols=_missing_cols(n))
            _, vjp = jax.vjp(fn, *[r[...] for r in refs[:n_in]])
            grads = vjp(do_ref[...].astype(F32))
            dq_ref[...] = grads[0].astype(dq_ref.dtype)
            db_ref[...] += grads[n_in - 1]
            shift(dk_ref, kacc, grads[1:1 + KEY_BLOCKS])
            shift(dv_ref, vacc, grads[1 + KEY_BLOCKS:1 + 2 * KEY_BLOCKS])

        @pl.when(n >= nt)
        def _():
            shift(dk_ref, kacc, None)
            shift(dv_ref, vacc, None)

    o_cur = pl.BlockSpec((ATT_TILE, LANE), lambda hp, n: (jnp.minimum(n, nt - 1), hp))
    o_old = pl.BlockSpec((ATT_TILE, LANE), lambda hp, n: (jnp.maximum(n - nc, 0), hp))
    b_spec = pl.BlockSpec((HEADS_PER_STEP, ATT_TILE, KEY_SPAN), lambda hp, n: (hp, 0, 0))
    return pl.pallas_call(
        body, name=name, grid=(N_HEADS // HEADS_PER_STEP, nt + nc),
        in_specs=_attn_in_specs(nt) + [o_cur] + [pl.BlockSpec(d.shape, lambda hp, n: (0, 0)) for d in deps],
        out_specs=[o_cur, o_old, o_old, b_spec],
        out_shape=[jax.ShapeDtypeStruct((s, D_ATTN), BF16)] * 3 + [jax.ShapeDtypeStruct((N_HEADS, ATT_TILE, KEY_SPAN), F32)],
        scratch_shapes=[pltpu.VMEM((ATT_TILE, LANE), F32)] * (2 * nc),
        compiler_params=pltpu.CompilerParams(dimension_semantics=("parallel", "arbitrary"), vmem_limit_bytes=VMEM_LIMIT),
    )(*([z] * (n_in - 1)), bias, do, *deps)


def loss_head(name, y, tgt, tm):
    s, d = y.shape

    def body(y_ref, t_ref, l_ref, dy_ref):
        i = pl.program_id(0)
        diff = y_ref[...] - t_ref[...]
        dy_ref[...] = diff * (1.0 / d)
        part = 0.5 * jnp.sum(jnp.mean(diff * diff, axis=-1, keepdims=True), axis=0, keepdims=True)

        @pl.when(i == 0)
        def _():
            l_ref[...] = jnp.zeros(l_ref.shape, F32)

        l_ref[...] += jnp.broadcast_to(part, l_ref.shape)

    row = pl.BlockSpec((tm, d), lambda i: (i, 0))
    return pl.pallas_call(
        body, name=name, grid=(s // tm,), in_specs=[row, row],
        out_specs=[pl.BlockSpec((8, LANE), lambda i: (0, 0)), row],
        out_shape=[jax.ShapeDtypeStruct((8, LANE), F32), jax.ShapeDtypeStruct((s, d), F32)],
        compiler_params=pltpu.CompilerParams(dimension_semantics=("arbitrary",), vmem_limit_bytes=VMEM_LIMIT),
    )(y, tgt)


def adamw(name, w, g, m, v, tr):
    r, c = w.shape
    assert r % tr == 0, (name, w.shape, tr)

    def body(w_ref, g_ref, m_ref, v_ref, d_ref, nm_ref, nv_ref):
        gg = g_ref[...]
        m2 = ADAM_B1 * m_ref[...] + (1.0 - ADAM_B1) * gg
        v2 = ADAM_B2 * v_ref[...] + (1.0 - ADAM_B2) * (gg * gg)
        m_hat = m2 / (1.0 - ADAM_B1 ** ADAM_STEP)
        v_hat = v2 / (1.0 - ADAM_B2 ** ADAM_STEP)
        d_ref[...] = -ADAM_LR * (m_hat / (jnp.sqrt(v_hat) + ADAM_EPS) + ADAM_WD * w_ref[...])
        nm_ref[...] = m2
        nv_ref[...] = v2

    blk = pl.BlockSpec((tr, c), lambda i: (i, 0))
    return pl.pallas_call(
        body, name=name, grid=(r // tr,), in_specs=[blk] * 4, out_specs=[blk] * 3,
        out_shape=[jax.ShapeDtypeStruct((r, c), F32)] * 3,
        compiler_params=pltpu.CompilerParams(dimension_semantics=("parallel",), vmem_limit_bytes=VMEM_LIMIT),
    )(w, g, m, v)


def adamw_sum(name, w, layer_blocks, m, v, tr):
    rows, c = w.shape
    nl = len(layer_blocks)
    nb, r, _ = layer_blocks[0].shape
    assert rows == nl * r and r % tr == 0, (name, w.shape, layer_blocks[0].shape, tr)
    per = r // tr

    def body(*refs):
        w_ref, b_refs, (m_ref, v_ref, g_ref, d_ref, nm_ref, nv_ref) = refs[0], refs[1:1 + nl], refs[1 + nl:]
        i = pl.program_id(0)

        def update(b_ref):
            gg = b_ref[0].astype(F32)
            for j in range(1, nb):
                gg = gg + b_ref[j].astype(F32)
            g_ref[...] = gg
            m2 = ADAM_B1 * m_ref[...] + (1.0 - ADAM_B1) * gg
            v2 = ADAM_B2 * v_ref[...] + (1.0 - ADAM_B2) * (gg * gg)
            m_hat = m2 / (1.0 - ADAM_B1 ** ADAM_STEP)
            v_hat = v2 / (1.0 - ADAM_B2 ** ADAM_STEP)
            d_ref[...] = -ADAM_LR * (m_hat / (jnp.sqrt(v_hat) + ADAM_EPS) + ADAM_WD * w_ref[...])
            nm_ref[...] = m2
            nv_ref[...] = v2

        for l in range(nl):
            pl.when(jnp.logical_and(i >= l * per, i < (l + 1) * per))(functools.partial(update, b_refs[l]))

    blk = pl.BlockSpec((tr, c), lambda i: (i, 0))
    b_specs = [pl.BlockSpec((nb, tr, c), lambda i, l=l: (0, jnp.clip(i - l * per, 0, per - 1), 0)) for l in range(nl)]
    return pl.pallas_call(
        body, name=name, grid=(rows // tr,),
        in_specs=[blk] + b_specs + [blk, blk], out_specs=[blk] * 4,
        out_shape=[jax.ShapeDtypeStruct((rows, c), F32)] * 4,
        compiler_params=pltpu.CompilerParams(dimension_semantics=("arbitrary",), vmem_limit_bytes=VMEM_LIMIT),
    )(w, *layer_blocks, m, v)


def sum_blocks(name, blocks, tr):
    nb, r, c = blocks.shape
    assert r % tr == 0, (name, blocks.shape, tr)

    def body(b_ref, o_ref):
        acc = b_ref[0].astype(F32)
        for j in range(1, nb):
            acc = acc + b_ref[j].astype(F32)
        o_ref[...] = acc

    return pl.pallas_call(
        body, name=name, grid=(r // tr,),
        in_specs=[pl.BlockSpec((nb, tr, c), lambda i: (0, i, 0))],
        out_specs=pl.BlockSpec((tr, c), lambda i: (i, 0)),
        out_shape=jax.ShapeDtypeStruct((r, c), F32),
        compiler_params=pltpu.CompilerParams(dimension_semantics=("parallel",), vmem_limit_bytes=VMEM_LIMIT),
    )(blocks)


FLIPS = [(0, 0, 1), (1, 0, 0), (0, 1, 0), (1, 1, 0), (1, 0, 1), (0, 1, 1), (1, 1, 1)]
ANY = pl.BlockSpec(memory_space=pl.ANY)


def _me():
    return lax.axis_index("x"), lax.axis_index("y"), lax.axis_index("c")


def _flip(pos, f):
    return tuple((1 - p) if fi else p for p, fi in zip(pos, f))


def _idx(pos):
    return 4 * pos[0] + 2 * pos[1] + pos[2]


def all_gather_multi(name, shards):
    n = len(shards)

    def body(*refs):
        x_refs, out_refs, token = refs[:n], refs[n:2 * n], refs[2 * n]
        send_sems, recv_sems, local_sems = refs[2 * n + 1:]
        token[...] = jnp.zeros_like(token)
        x, y, cc = _me()
        me, sibling = (x, y, cc), (x, y, 1 - cc)
        chips = [(1 - x, y), (x, 1 - y), (1 - x, 1 - y)]

        def copy(a, k, block, to, src=None):
            dst = out_refs[a].at[_idx(block)]
            return pltpu.make_async_remote_copy(
                src_ref=dst if src is None else src, dst_ref=dst, send_sem=send_sems.at[7 * a + k],
                recv_sem=recv_sems.at[7 * a + k], device_id=to, device_id_type=MESH)

        mine = [pltpu.make_async_copy(x_refs[a], out_refs[a].at[_idx(me)], local_sems.at[a]) for a in range(n)]
        for cp in mine:
            cp.start()
        first = []
        for a in range(n):
            first.append(copy(a, 0, me, sibling, src=x_refs[a]))
            first += [copy(a, 1 + j, me, (*chip, cc), src=x_refs[a]) for j, chip in enumerate(chips)]
        for cp in first:
            cp.start()
        passed = []
        for j, chip in enumerate(chips):
            for a in range(n):
                copy(a, 1 + j, (*chip, cc), me).wait_recv()
                fwd = copy(a, 4 + j, (*chip, cc), sibling)
                fwd.start()
                passed.append(fwd)
        for a in range(n):
            copy(a, 0, sibling, me).wait_recv()
            for j, chip in enumerate(chips):
                copy(a, 4 + j, (*chip, 1 - cc), me).wait_recv()
        for cp in first + passed:
            cp.wait_send()
        for cp in mine:
            cp.wait()

    return pl.pallas_call(
        body, name=name, in_specs=[ANY] * n, out_specs=[ANY] * n + [pl.BlockSpec(memory_space=pltpu.VMEM)],
        out_shape=[jax.ShapeDtypeStruct((N_DEV,) + a.shape, a.dtype) for a in shards]
        + [jax.ShapeDtypeStruct((8, LANE), F32)],
        scratch_shapes=[pltpu.SemaphoreType.DMA((7 * n,)), pltpu.SemaphoreType.DMA((7 * n,)),
                        pltpu.SemaphoreType.DMA((n,))],
    )(*shards)


def all_to_all_multi(name, blocks):
    n = len(blocks)

    def body(*refs):
        in_refs, out_refs = refs[:n], refs[n:2 * n]
        send_sems, recv_sems, local_sems = refs[2 * n:]
        me = _me()
        mi = _idx(me)
        mine = [pltpu.make_async_copy(in_refs[a].at[mi], out_refs[a].at[mi], local_sems.at[a]) for a in range(n)]
        for cp in mine:
            cp.start()
        sends, recvs = [], []
        for k, f in enumerate(FLIPS):
            peer = _flip(me, f)
            pi = _idx(peer)
            for a in range(n):
                sems = dict(send_sem=send_sems.at[7 * a + k], recv_sem=recv_sems.at[7 * a + k],
                            device_id=peer, device_id_type=MESH)
                sends.append(pltpu.make_async_remote_copy(src_ref=in_refs[a].at[pi], dst_ref=out_refs[a].at[mi], **sems))
                recvs.append(pltpu.make_async_remote_copy(src_ref=in_refs[a].at[mi], dst_ref=out_refs[a].at[pi], **sems))
        for cp in sends:
            cp.start()
        for cp in recvs:
            cp.wait_recv()
        for cp in sends:
            cp.wait_send()
        for cp in mine:
            cp.wait()

    return pl.pallas_call(
        body, name=name, in_specs=[ANY] * n, out_specs=[ANY] * n,
        out_shape=[jax.ShapeDtypeStruct(a.shape, a.dtype) for a in blocks],
        scratch_shapes=[pltpu.SemaphoreType.DMA((7 * n,)), pltpu.SemaphoreType.DMA((7 * n,)),
                        pltpu.SemaphoreType.DMA((n,))],
    )(*blocks)


HBM = pl.BlockSpec(memory_space=pltpu.HBM)
SEM = pl.BlockSpec(memory_space=pltpu.SEMAPHORE)
DATAFLOW = pltpu.SideEffectType.DATAFLOW_SIDE_EFFECTING


def _exchange_copies(a_refs, l_refs, send_sems, recv_sems, gather):
    me = _me()
    mi = _idx(me)
    out = []
    for k, f in enumerate(FLIPS):
        peer = _flip(me, f)
        for a in range(len(a_refs)):
            src = a_refs[a] if gather else a_refs[a].at[_idx(peer)]
            out.append(pltpu.make_async_remote_copy(
                src_ref=src, dst_ref=l_refs[a].at[mi], send_sem=send_sems.at[7 * a + k],
                recv_sem=recv_sems.at[7 * a + k], device_id=peer, device_id_type=MESH))
    return out


def exchange_start(name, arrays, gather):
    n = len(arrays)
    lands = [lax.empty(((N_DEV,) + a.shape) if gather else a.shape, a.dtype) for a in arrays]

    def body(*refs):
        a_refs, l_refs = refs[:n], refs[n:2 * n]
        send_sems, recv_sems = refs[2 * n], refs[2 * n + 1]
        token = refs[4 * n + 2]
        for cp in _exchange_copies(a_refs, l_refs, send_sems, recv_sems, gather):
            cp.start()
        token[...] = jnp.zeros_like(token)

    hbm = lambda a: pltpu.HBM(a.shape, a.dtype)
    res = pl.pallas_call(
        body, name=name,
        out_shape=(pltpu.SemaphoreType.DMA((7 * n,)), pltpu.SemaphoreType.DMA((7 * n,)),
                   *[hbm(a) for a in arrays], *[hbm(a) for a in lands], jax.ShapeDtypeStruct((8, LANE), F32)),
        in_specs=[HBM] * (2 * n),
        out_specs=(SEM, SEM, *[HBM] * (2 * n), pl.BlockSpec(memory_space=pltpu.VMEM)),
        input_output_aliases={i: i + 2 for i in range(2 * n)},
        compiler_params=pltpu.CompilerParams(has_side_effects=DATAFLOW),
    )(*[pltpu.with_memory_space_constraint(a, pltpu.HBM) for a in arrays],
      *[pltpu.with_memory_space_constraint(a, pltpu.HBM) for a in lands])
    return res[0], res[1], list(res[2:2 + n]), list(res[2 + n:2 + 2 * n]), res[-1]


def exchange_wait(name, send_sems, recv_sems, arrays, lands, after, gather):
    n = len(arrays)
    after = list(after) if isinstance(after, (list, tuple)) else [after]

    def body(*refs):
        a_refs, l_refs = refs[:n], refs[n:2 * n]
        ssem, rsem = refs[2 * n], refs[2 * n + 1]
        for cp in _exchange_copies(a_refs, l_refs, ssem, rsem, gather):
            cp.wait_send()
            cp.wait_recv()

    hbm = lambda a: pltpu.HBM(a.shape, a.dtype)
    res = pl.pallas_call(
        body, name=name,
        out_shape=(*[hbm(a) for a in arrays], *[hbm(a) for a in lands]),
        in_specs=[HBM] * (2 * n) + [SEM, SEM] + [pl.BlockSpec(memory_space=pl.ANY)] * len(after),
        out_specs=tuple([HBM] * (2 * n)),
        input_output_aliases={i: i for i in range(2 * n)},
        compiler_params=pltpu.CompilerParams(has_side_effects=DATAFLOW),
    )(*arrays, *lands, send_sems, recv_sems, *after)
    return list(res[n:])


def ada_fwd(name, c_row, w_cat, b_lay):
    d = c_row.shape[1]
    ncol = w_cat.shape[1]
    vmem = pl.BlockSpec(memory_space=pltpu.VMEM)

    def body(c_ref, w_ref, b_ref, mod_ref, cact_ref, token, call, send, land, s1, r1, s2, r2):
        token[...] = jnp.zeros_like(token)
        me = _me()
        mi = _idx(me)
        call[mi] = c_ref[...]

        def exchange(src_of, dst_buf, ssem, rsem):
            sends, recvs = [], []
            for k, f in enumerate(FLIPS):
                peer = _flip(me, f)
                sends.append(pltpu.make_async_remote_copy(
                    src_ref=src_of(peer), dst_ref=dst_buf.at[mi], send_sem=ssem.at[k], recv_sem=rsem.at[k],
                    device_id=peer, device_id_type=MESH))
                recvs.append(pltpu.make_async_remote_copy(
                    src_ref=src_of(peer), dst_ref=dst_buf.at[_idx(peer)], send_sem=ssem.at[k], recv_sem=rsem.at[k],
                    device_id=peer, device_id_type=MESH))
            for cp in sends:
                cp.start()
            for cp in recvs:
                cp.wait_recv()
            for cp in sends:
                cp.wait_send()

        exchange(lambda peer: c_ref, call, s1, r1)
        for p in range(N_DEV):
            cact_ref[pl.ds(p, 1), :] = jax.nn.silu(call[p])
        res = _dg(cact_ref[...], w_ref[...], 1, 0)
        for p in range(N_DEV):
            send[p] = res[p:p + 1, :]
        land[mi] = send[mi]
        exchange(lambda peer: send.at[_idx(peer)], land, s2, r2)
        mod_ref[...] = land[...] + b_ref[...]

    return pl.pallas_call(
        body, name=name, in_specs=[vmem, vmem, vmem], out_specs=[vmem, vmem, vmem],
        out_shape=[jax.ShapeDtypeStruct((N_DEV, 1, ncol), F32), jax.ShapeDtypeStruct((N_DEV, d), F32),
                   jax.ShapeDtypeStruct((8, LANE), F32)],
        scratch_shapes=[pltpu.VMEM((N_DEV, 1, d), F32), pltpu.VMEM((N_DEV, 1, ncol), F32),
                        pltpu.VMEM((N_DEV, 1, ncol), F32),
                        pltpu.SemaphoreType.DMA((7,)), pltpu.SemaphoreType.DMA((7,)),
                        pltpu.SemaphoreType.DMA((7,)), pltpu.SemaphoreType.DMA((7,))],
        compiler_params=pltpu.CompilerParams(vmem_limit_bytes=VMEM_LIMIT),
    )(c_row, w_cat, b_lay)


POOL_WINDOWS = (2, 4, 8, 16)
POOL_GROUP = 64
N_REL = 2 * REL_CLIP + 1
PACK_COLS = 1024
SMALL_NAMES = ["b_ada", "b_gate", "w_pool", "pool_scale", "rel_bias", "conv_w", "conv_b", "conv_ln_g",
               "conv_ln_b", "ln_mix_g", "ln_mix_b", "b_ff1", "b_ff2", "ln_ff_g", "ln_ff_b"]
BIG_NAMES = ["w_in", "w_br_pool", "w_br_attn", "w_br_conv", "w_o", "w_ff1", "w_ff2"]
ROW_SHARDED = ("w_o", "w_ff2")
WEIGHT_NAMES = ["w_ada", "b_ada", "w_in", "b_gate", "w_pool", "pool_scale", "rel_bias", "conv_w", "conv_b",
                "conv_ln_g", "conv_ln_b", "w_br_pool", "w_br_attn", "w_br_conv", "w_o", "ln_mix_g", "ln_mix_b",
                "w_ff1", "b_ff1", "w_ff2", "b_ff2", "ln_ff_g", "ln_ff_b"]


def _perm_cols(w):
    return jnp.concatenate([w[:, 2304:], w[:, 1792:2304], w[:, :256], w[:, 256:768], w[:, 768:1280],
                            w[:, 1280:1792]], axis=1)


def _unperm_cols(wp):
    return jnp.concatenate([wp[:, Z_POOL:Z_Q], wp[:, Z_Q:Z_K], wp[:, Z_K:Z_V], wp[:, Z_V:],
                            wp[:, Z_CONV:Z_POOL], wp[:, :Z_CONV]], axis=1)


def _bias_table(rel_bias):
    far = jnp.broadcast_to(rel_bias[:, 2 * REL_CLIP:], (N_HEADS, BAND - REL_CLIP))
    near = rel_bias[:, REL_CLIP - CHUNK + 1:2 * REL_CLIP][:, ::-1]
    ext = jnp.concatenate([far, near, jnp.zeros((N_HEADS, 1), F32)], axis=1)
    length = BAND + CHUNK
    flat = jnp.tile(ext, (1, CHUNK + 1))
    skew = flat[:, CHUNK - 1:CHUNK - 1 + CHUNK * (length - 1)].reshape(N_HEADS, CHUNK, length - 1)
    return skew[:, :, :BAND]


def _bias_full(rel_bias):
    tab = _bias_table(rel_bias)
    return jnp.concatenate(
        [jnp.pad(tab, ((0, 0), (0, 0), (i * CHUNK, KEY_SPAN - BAND - i * CHUNK)), constant_values=NEG_INF)
         for i in range(CHUNKS_PER_TILE)], axis=1)


def _block_diag(w_pool):
    out = jnp.zeros((D_POOL, D_POOL), F32)
    for g in range(len(POOL_WINDOWS)):
        out = lax.dynamic_update_slice(out, w_pool[g], (g * POOL_GROUP, g * POOL_GROUP))
    return out


def _flat_pad(arrs, mult):
    flat = jnp.concatenate([a.reshape(-1) for a in arrs])
    pad = (-flat.shape[0]) % mult
    return jnp.pad(flat, (0, pad)) if pad else flat


def _unflat(flat, shapes):
    out, off = [], 0
    for shp in shapes:
        n = int(np.prod(shp))
        out.append(flat[off:off + n].reshape(shp))
        off += n
    return out


def _to_blocks(name, full):
    k, n = full.shape
    if name in ROW_SHARDED:
        return full.reshape(N_DEV, k // N_DEV, n)
    return full.reshape(k, N_DEV, n // N_DEV).transpose(1, 0, 2)


def _from_blocks(name, blocks):
    nb, r, c = blocks.shape
    if name in ROW_SHARDED:
        return blocks.reshape(nb * r, c)
    return blocks.transpose(1, 0, 2).reshape(r, nb * c)


class _Layer:
    pass


def _row(v):
    return v.reshape(1, -1)


def _layer_fwd(x, modr, w, wvec, fetch_rest):
    sh_m, sc_m, g_m, sh_f, sc_f, g_f = modr
    (u,) = row_fwd("lnmod_mix", f_lnmod, [x], [sc_m, sh_m], [(D_MODEL, BF16)], 512)
    z = mm_big("mm_in", u, w.w_in, "nn", (1024, 896, 1024), F32, j_outer=True)
    p = pool_lin("pool_fwd", (z, D_POOL, Z_POOL // D_POOL), wvec, False, F32)
    ao = attn_fwd("attn_fwd", z, w.bias)
    cv = conv_fwd("conv_fwd", (z, 2 * D_CONV, Z_CONV // (2 * D_CONV)), w.conv_w)
    fetch_rest(w, cv)
    mparams = [w.wbd, w.ps, w.wbp, w.wba, w.wbc, w.cb, w.clg, w.clb, w.bg, w.wo, g_m, w.lmg, w.lmb]
    (x1,) = row_fwd("merge", f_merge, [p, ao, cv, (z, 3 * D_MODEL, 0), x], mparams, [(D_MODEL, F32)], 256)
    (u2,) = row_fwd("lnmod_ff", f_lnmod, [x1], [sc_f, sh_f], [(D_MODEL, BF16)], 512)
    hpre, h = mm_ff1_relu2("mm_ff1", u2, w.w_ff1, w.b1, 1024, 1024)
    ff = mm_big("mm_ff2", h, w.w_ff2, "nn", (512, 1024, 4096), F32)
    (x2,) = row_fwd("ffout", f_ffout, [x1, ff], [w.b2, g_f, w.lfg, w.lfb], [(D_MODEL, F32)], 512)
    return x2, (x, u, z, p, ao, cv, x1, u2, hpre, h, ff, mparams)


GRAD_GROUPS = [("ff", ["w_ff2", "w_ff1"]), ("mix", ["w_o", "w_br_pool", "w_br_attn", "w_br_conv"]), ("in", ["w_in"])]


def _layer_bwd(dx2, saved, modr, w, wvec, ready):
    x, u, z, p, ao, cv, x1, u2, hpre, h, ff, mparams = saved
    sh_m, sc_m, g_m, sh_f, sc_f, g_f = modr
    g = {}
    (dx1a, dff), (g["b_ff2"], dgf, g["ln_ff_g"], g["ln_ff_b"]) = row_bwd(
        "ffout_bwd", f_ffout, [x1, ff], [w.b2, g_f, w.lfg, w.lfb], [dx2], 512, [(0, F32), (1, BF16)], [0, 1, 2, 3])
    dhpre, g["b_ff1"] = mm_dh_relu2("mm_dh", dff, w.w_ff2, hpre, w.b1, 1024, 1024)
    g["w_ff2"] = mm_big("mm_dw_ff2", h, dff, "tn", (1024, 1024, 2048), BF16)
    du2 = mm_big("mm_du2", dhpre, w.w_ff1, "nt", (512, 1024, 4096), F32)
    g["w_ff1"] = mm_big("mm_dw_ff1", u2, dhpre, "tn", (1024, 1024, 2048), BF16)
    sc_f = sc_f + ready("ff", g)[0:1, 0:1]
    (dx1,), (dscf, dshf) = row_bwd("lnmod_ff_bwd", f_lnmod, [x1], [sc_f, sh_f], [du2], 512, [(0, F32)], [0, 1],
                                   add_to=dx1a)
    (dp, dao, dcv, dzg, dxa), dm = row_bwd(
        "merge_bwd", f_merge, [p, ao, cv, (z, 3 * D_MODEL, 0), x], mparams, [dx1], 256,
        [(0, F32), (1, BF16), (2, F32), (3, BF16), (4, F32)], list(range(13)))
    (dwbd, g["pool_scale"], g["w_br_pool"], g["w_br_attn"], g["w_br_conv"], g["conv_b"], g["conv_ln_g"],
     g["conv_ln_b"], g["b_gate"], g["w_o"], dgm, g["ln_mix_g"], g["ln_mix_b"]) = dm
    g["w_pool"] = jnp.stack([dwbd[i * POOL_GROUP:(i + 1) * POOL_GROUP, i * POOL_GROUP:(i + 1) * POOL_GROUP]
                             for i in range(len(POOL_WINDOWS))])
    tok = ready("mix", g)
    dzp = pool_lin("pool_bwd", (dp, D_POOL, 0), wvec + tok[0:1, 0:1], True, BF16)
    dq, dk, dv, dbias = attn_bwd("attn_bwd", z, w.bias, dao, after=tok)
    (g["rel_bias"],) = w.bias_vjp(dbias)
    dzc, dcw = conv_bwd("conv_bwd", (z, 2 * D_CONV, Z_CONV // (2 * D_CONV)), w.conv_w + tok[0:1, 0:1], dcv)
    g["conv_w"] = dcw[:CONV_WIDTH]
    dz = jnp.concatenate([dzg, dzc, dzp, dq, dk, dv], axis=1)
    du = mm_big("mm_du", dz, w.w_in, "nt", (512, 1024, D_IN), F32)
    (dx,), (dscm, dshm) = row_bwd("lnmod_mix_bwd", f_lnmod, [x], [sc_m, sh_m], [du], 512, [(0, F32)], [0, 1],
                                  add_to=dxa)
    g["dmod"] = jnp.concatenate([dshm, dscm, dgm, dshf, dscf, dgf], axis=1)
    tok = ready("small", g)
    g["w_in"] = _unperm_cols(mm_big("mm_dw_in", u, dz, "tn", (1024, 896, 2048), BF16, after=tok))
    return dx, g, ready("in", g)


def kernel(x, c, w_ada, b_ada, w_in, b_gate, w_pool, pool_scale, rel_bias, conv_w, conv_b, conv_ln_g, conv_ln_b, w_br_pool, w_br_attn, w_br_conv, w_o, ln_mix_g, ln_mix_b, w_ff1, b_ff1, w_ff2, b_ff2, ln_ff_g, ln_ff_b, loss_target, m_w_ada, m_b_ada, m_w_in, m_b_gate, m_w_pool, m_pool_scale, m_rel_bias, m_conv_w, m_conv_b, m_conv_ln_g, m_conv_ln_b, m_w_br_pool, m_w_br_attn, m_w_br_conv, m_w_o, m_ln_mix_g, m_ln_mix_b, m_w_ff1, m_b_ff1, m_w_ff2, m_b_ff2, m_ln_ff_g, m_ln_ff_b, v_w_ada, v_b_ada, v_w_in, v_b_gate, v_w_pool, v_pool_scale, v_rel_bias, v_conv_w, v_conv_b, v_conv_ln_g, v_conv_ln_b, v_w_br_pool, v_w_br_attn, v_w_br_conv, v_w_o, v_ln_mix_g, v_ln_mix_b, v_w_ff1, v_b_ff1, v_w_ff2, v_b_ff2, v_ln_ff_g, v_ln_ff_b):
    args = dict(locals())
    wts = {n: args[n] for n in WEIGHT_NAMES}
    mom = {n: args["m_" + n] for n in WEIGHT_NAMES}
    var = {n: args["v_" + n] for n in WEIGHT_NAMES}
    me = 4 * lax.axis_index("x") + 2 * lax.axis_index("y") + lax.axis_index("c")
    xs, tgt = x[0], loss_target[0]
    nc_ada = w_ada.shape[2]
    wvec = jnp.asarray(np.repeat(np.array(POOL_WINDOWS, np.float32), POOL_GROUP)[None, :])

    w_cat = jnp.concatenate([w_ada[l] for l in range(DEPTH)], axis=1)
    b_lay = b_ada.reshape(DEPTH, N_DEV, nc_ada).transpose(1, 0, 2).reshape(N_DEV, 1, DEPTH * nc_ada)
    land, cact, ada_token = ada_fwd("ada_fwd", c, w_cat, b_lay)
    mod = land.reshape(N_DEV, DEPTH, nc_ada).transpose(1, 0, 2).reshape(DEPTH, 6 * D_MODEL)
    modr = [[mod[l:l + 1, i * D_MODEL:(i + 1) * D_MODEL] for i in range(6)] for l in range(DEPTH)]

    cw_pack = _flat_pad([conv_w], 8 * LANE).reshape(-1, LANE) + ada_token[0:1, 0:1]
    shards =[[wts[n][l].astype(BF16) for n in BIG_NAMES] for l in range(DEPTH)]
    g_in0, cw_all, token = all_gather_multi("gather_w_in_l0", [shards[0][0], cw_pack])
    cw_all = cw_all.reshape(N_DEV, -1)[:, :conv_w.size]
    conv_full = cw_all.reshape((N_DEV,) + conv_w.shape).transpose(1, 2, 0, 3).reshape(DEPTH, CONV_WIDTH, D_CONV)
    gathers = {}
    for key, arrs in [("rest_l0", shards[0][1:]), ("w_in_l1", shards[1][:1]), ("rest_l1", shards[1][1:])]:
        arrs = [a + token[0:1, 0:1].astype(a.dtype) for a in arrs]
        ssem, rsem, thru, lands, token = exchange_start(f"gather_{key}_start", arrs, True)
        gathers[key] = (arrs, ssem, rsem, thru, lands)
    modr[0] = [r + token[0:1, 0:1] for r in modr[0]]

    def with_own(lands_, own):
        return [lax.dynamic_update_index_in_dim(ld, o, me, axis=0) for ld, o in zip(lands_, own)]

    def gathered(key, after):
        arrs, ssem, rsem, thru, lands = gathers[key]
        return with_own(exchange_wait(f"gather_{key}_wait", ssem, rsem, thru, lands, after, True), arrs)

    def fetch_rest_for(l):
        def fetch_rest(w, after):
            full = {n: _from_blocks(n, g) for n, g in zip(BIG_NAMES[1:], gathered(f"rest_l{l}", after))}
            w.wbp, w.wba, w.wbc = full["w_br_pool"], full["w_br_attn"], full["w_br_conv"]
            w.wo, w.w_ff1, w.w_ff2 = full["w_o"], full["w_ff1"], full["w_ff2"]
        return fetch_rest

    def layer_weights(l, in_blocks):
        w = _Layer()
        w.w_in = _perm_cols(_from_blocks("w_in", in_blocks))
        w.wbd = _block_diag(w_pool[l])
        w.ps, w.cb, w.clg, w.clb = _row(pool_scale[l]), _row(conv_b[l]), _row(conv_ln_g[l]), _row(conv_ln_b[l])
        w.bg, w.lmg, w.lmb = _row(b_gate[l]), _row(ln_mix_g[l]), _row(ln_mix_b[l])
        w.b1, w.b2, w.lfg, w.lfb = _row(b_ff1[l]), _row(b_ff2[l]), _row(ln_ff_g[l]), _row(ln_ff_b[l])
        w.conv_w = jnp.pad(conv_full[l], ((0, CONV_PAD - CONV_WIDTH), (0, 0)))
        w.bias, w.bias_vjp = jax.vjp(_bias_full, rel_bias[l])
        return w

    layers, saved = [layer_weights(0, g_in0)], []
    h, sv = _layer_fwd(xs, modr[0], layers[0], wvec, fetch_rest_for(0))
    saved.append(sv)
    layers.append(layer_weights(1, gathered("w_in_l1", h)[0]))
    h, sv = _layer_fwd(h, modr[1], layers[1], wvec, fetch_rest_for(1))
    saved.append(sv)
    lpart, dy = loss_head("loss_head", h, tgt, 512)
    loss = lax.psum(lpart[0, 0], ("x", "y", "c"))
    grads, dmods = [None] * DEPTH, [None] * DEPTH
    pending = {}
    small_shapes = [wts[n].shape if n != "conv_w" else (DEPTH, CONV_WIDTH, D_CONV) for n in SMALL_NAMES]

    def ready_for(l):
        def ready(group, g):
            if group == "small":
                if l > 0:
                    return None
                both = [g, grads[1]]
                local = [jnp.concatenate([both[k]["dmod"] for k in range(DEPTH)], axis=0)]
                local += [jnp.stack([both[k][n].reshape(shp[1:]) for k in range(DEPTH)])
                          for n, shp in zip(SMALL_NAMES[1:], small_shapes[1:])]
                pack = _flat_pad(local, 8 * LANE).reshape(-1, LANE)
                ssem, rsem, thru, lands, token = exchange_start("gather_small_grads_start", [pack], True)
                pending["small"] = (pack, ssem, rsem, thru, lands)
                return token
            names = dict(GRAD_GROUPS)[group]
            blocks = [_to_blocks(n, g[n]).astype(BF16) for n in names]
            ssem, rsem, thru, lands, token = exchange_start(f"scatter_l{l}_{group}_start", blocks, False)
            pending[(l, group)] = (names, blocks, ssem, rsem, thru, lands)
            return token
        return ready

    def received(l, group, after):
        names, blocks, ssem, rsem, thru, lands = pending[(l, group)]
        lands = exchange_wait(f"scatter_l{l}_{group}_wait", ssem, rsem, thru, lands, after, False)
        own = [lax.dynamic_index_in_dim(b, me, axis=0, keepdims=False) for b in blocks]
        return dict(zip(names, with_own(lands, own)))

    dy, grads[1], token = _layer_bwd(dy, saved[1], modr[1], layers[1], wvec, ready_for(1))
    modr0 = [r + token[0:1, 0:1] for r in modr[0]]
    dy, grads[0], token = _layer_bwd(dy, saved[0], modr0, layers[0], wvec, ready_for(0))
    recv = [{} for _ in range(DEPTH)]
    for l, group in [(1, "ff"), (1, "mix"), (1, "in"), (0, "ff"), (0, "mix")]:
        recv[l].update(received(l, group, token))
    grad_x = dy[None]
    dmod_size = DEPTH * 6 * D_MODEL

    small_pack, ssem, rsem, thru, lands = pending["small"]
    (small_all,) = with_own(exchange_wait("gather_small_grads_wait", ssem, rsem, thru, lands, token, True), [small_pack])
    small_sum = sum_blocks("sum_small_grads", small_all, small_pack.shape[0]).reshape(-1)
    gsmall = dict(zip(SMALL_NAMES, _unflat(small_sum, small_shapes)))
    gw = dict(gsmall)
    gw["conv_w"] = lax.dynamic_slice_in_dim(gsmall["conv_w"], me * conv_w.shape[2], conv_w.shape[2], axis=2)

    dmod_all = small_all.reshape(N_DEV, -1)[:, :dmod_size].reshape(N_DEV, DEPTH, N_DEV, nc_ada)
    dm_mine = lax.dynamic_index_in_dim(dmod_all, me, axis=2, keepdims=False).reshape(N_DEV, DEPTH * nc_ada)
    cact_t = jnp.pad(cact.T, ((0, 0), (0, LANE - N_DEV)))
    dm_pad = jnp.pad(dm_mine, ((0, LANE - N_DEV), (0, 0)))
    dw_cat = mm_big("mm_dw_ada", cact_t, dm_pad, "nn", (D_MODEL, DEPTH * nc_ada, LANE), F32)
    gw["w_ada"] = jnp.stack([dw_cat[:, l * nc_ada:(l + 1) * nc_ada] for l in range(DEPTH)])

    delta, new_m, new_v = {}, {}, {}
    packs = [_flat_pad([src[n] for n in SMALL_NAMES], 8 * LANE).reshape(-1, LANE) for src in (wts, gw, mom, var)]
    small_res = adamw("adamw_small", *packs, packs[0].shape[0])
    shapes = [wts[n].shape for n in SMALL_NAMES]
    for out, flat in zip((delta, new_m, new_v), small_res):
        out.update(dict(zip(SMALL_NAMES, _unflat(flat.reshape(-1), shapes))))
    for n in ["w_ada", "w_ff2", "w_ff1", "w_o", "w_br_pool", "w_br_attn", "w_br_conv", "w_in"]:
        shp = wts[n].shape
        two_d = lambda a, shp=shp: a.reshape(shp[0] * shp[1], shp[2])
        tr = min(256, shp[1])
        if n == "w_ada":
            res = (gw[n],) + tuple(adamw("adamw_" + n, two_d(wts[n]), two_d(gw[n]), two_d(mom[n]), two_d(var[n]), tr))
        else:
            if n == "w_in":
                done = [small_res[2]] + [new_v[k] for k in ["w_ada"] + BIG_NAMES[1:]]
                recv[0].update(received(0, "in", done))
            res = adamw_sum("adamw_" + n, two_d(wts[n]), [recv[l][n] for l in range(DEPTH)], two_d(mom[n]),
                            two_d(var[n]), tr)
        gw[n], delta[n], new_m[n], new_v[n] = [a.reshape(shp) for a in res]

    return (loss, grad_x, *[gw[n] for n in WEIGHT_NAMES], *[delta[n] for n in WEIGHT_NAMES],
            *[new_m[n] for n in WEIGHT_NAMES], *[new_v[n] for n in WEIGHT_NAMES])
```

```python
import functools

import jax
import jax.numpy as jnp
import numpy as np
from jax import lax
from jax.experimental import pallas as pl
from jax.experimental.pallas import tpu as pltpu

F32 = jnp.float32
BF16 = jnp.bfloat16
MESH = pl.DeviceIdType.MESH

D_MODEL = 1024
DEPTH = 2
CHUNK = 64
N_HEADS = 8
HEAD_DIM = 64
D_POOL = 256
D_ATTN = 512
D_CONV = 256
CONV_WIDTH = 31
D_FF = 4096
D_IN = 5376
N_PREV = 8
BAND = (N_PREV + 1) * CHUNK
REL_CLIP = 128
ALPHA = (2.0 * DEPTH) ** 0.25
LN_EPS = 1e-5
NEG_INF = -1e30
N_DEV = 8

ADAM_LR, ADAM_B1, ADAM_B2, ADAM_EPS, ADAM_WD, ADAM_STEP = 0.001, 0.9, 0.999, 1e-08, 0.01, 10

VMEM_LIMIT = 56 * 1024 * 1024

Z_GATE, Z_CONV, Z_POOL, Z_Q, Z_K, Z_V = 0, 3072, 3584, 3840, 4352, 4864
ATT_TILE = 512
LANE = 128


def _dg(a, b, ca, cb):
    return lax.dot_general(a.astype(BF16), b.astype(BF16), (((ca,), (cb,)), ((), ())),
                           preferred_element_type=F32)


@jax.custom_vjp
def mm_nn(a, b):
    return _dg(a, b, 1, 0)


def _mm_nn_fwd(a, b):
    return _dg(a, b, 1, 0), (a, b)


def _mm_nn_bwd(res, g):
    a, b = res
    return _dg(g, b, 1, 1).astype(a.dtype), _dg(a, g, 0, 0).astype(b.dtype)


mm_nn.defvjp(_mm_nn_fwd, _mm_nn_bwd)


@jax.custom_vjp
def mm_nt(a, b):
    return _dg(a, b, 1, 1)


def _mm_nt_fwd(a, b):
    return _dg(a, b, 1, 1), (a, b)


def _mm_nt_bwd(res, g):
    a, b = res
    return _dg(g, b, 1, 0).astype(a.dtype), _dg(g, a, 0, 0).astype(b.dtype)


mm_nt.defvjp(_mm_nt_fwd, _mm_nt_bwd)


@jax.custom_vjp
def mm_nn_shadow(a, w, shadow):
    return _dg(a, w, 1, 0)


def _mm_nn_shadow_fwd(a, w, shadow):
    return _dg(a, w, 1, 0), (a, w)


def _mm_nn_shadow_bwd(res, g):
    a, w = res
    return _dg(g, w, 1, 1).astype(a.dtype), jnp.zeros_like(w), _dg(a, g, 0, 0)


mm_nn_shadow.defvjp(_mm_nn_shadow_fwd, _mm_nn_shadow_bwd)


def mm_w(a, w):
    return mm_nn_shadow(a, w[0], w[1]) if isinstance(w, tuple) else mm_nn(a, w)


def _ln(x):
    mu = jnp.mean(x, axis=-1, keepdims=True)
    xc = x - mu
    var = jnp.mean(xc * xc, axis=-1, keepdims=True)
    return xc * lax.rsqrt(var + LN_EPS)


def _norm_rows(rows):
    return [r if isinstance(r, tuple) else (r, r.shape[1], 0) for r in rows]


def _row_spec(tm, r):
    _, width, cb = r
    return pl.BlockSpec((tm, width), lambda i, cb=cb: (i, cb))


def _full_spec(a):
    nd = a.ndim
    return pl.BlockSpec(a.shape, lambda i, nd=nd: (0,) * nd)


def row_fwd(name, f, rows, params, outs, tm):
    rows = _norm_rows(rows)
    s = rows[0][0].shape[0]
    nr, npar = len(rows), len(params)

    def body(*refs):
        r = [x[...].astype(F32) for x in refs[:nr]]
        p = [x[...] for x in refs[nr:nr + npar]]
        res = f(*r, *p)
        for o_ref, o in zip(refs[nr + npar:], res):
            o_ref[...] = o.astype(o_ref.dtype)

    return pl.pallas_call(
        body, name=name, grid=(s // tm,),
        in_specs=[_row_spec(tm, r) for r in rows] + [_full_spec(p) for p in params],
        out_specs=[pl.BlockSpec((tm, w), lambda i: (i, 0)) for w, _ in outs],
        out_shape=[jax.ShapeDtypeStruct((s, w), dt) for w, dt in outs],
        compiler_params=pltpu.CompilerParams(dimension_semantics=("parallel",), vmem_limit_bytes=VMEM_LIMIT),
    )(*[r[0] for r in rows], *params)


def row_bwd(name, f, rows, params, douts, tm, want_rows, want_params, add_to=None):
    rows = _norm_rows(rows)
    s = rows[0][0].shape[0]
    nr, npar, nd = len(rows), len(params), len(douts)
    nadd = 0 if add_to is None else 1
    n_in = nr + npar + nd + nadd

    def body(*refs):
        i = pl.program_id(0)
        r = [x[...].astype(F32) for x in refs[:nr]]
        p = [(x[...], jnp.zeros(x.shape, F32)) if x.dtype == BF16 else x[...] for x in refs[nr:nr + npar]]
        d = [x[...].astype(F32) for x in refs[nr + npar:nr + npar + nd]]
        _, vjp = jax.vjp(f, *r, *p)
        g = vjp(tuple(d))
        out_refs = refs[n_in:]
        for k, (idx, _) in enumerate(want_rows):
            val = g[idx]
            if nadd and k == 0:
                val = val + refs[n_in - 1][...].astype(F32)
            out_refs[k][...] = val.astype(out_refs[k].dtype)
        for k, idx in enumerate(want_params):
            gp = g[nr + idx]
            gp = gp[1] if isinstance(gp, tuple) else gp
            o_ref = out_refs[len(want_rows) + k]

            @pl.when(i == 0)
            def _():
                o_ref[...] = gp

            @pl.when(i > 0)
            def _():
                o_ref[...] += gp

    in_specs = ([_row_spec(tm, r) for r in rows] + [_full_spec(p) for p in params]
                + [pl.BlockSpec((tm, d.shape[1]), lambda i: (i, 0)) for d in douts])
    args = [r[0] for r in rows] + list(params) + list(douts)
    if nadd:
        in_specs.append(pl.BlockSpec((tm, add_to.shape[1]), lambda i: (i, 0)))
        args.append(add_to)
    out_specs = ([pl.BlockSpec((tm, rows[idx][1]), lambda i: (i, 0)) for idx, _ in want_rows]
                 + [_full_spec(params[idx]) for idx in want_params])
    out_shape = ([jax.ShapeDtypeStruct((s, rows[idx][1]), dt) for idx, dt in want_rows]
                 + [jax.ShapeDtypeStruct(params[idx].shape, F32) for idx in want_params])
    res = pl.pallas_call(
        body, name=name, grid=(s // tm,), in_specs=in_specs, out_specs=out_specs, out_shape=out_shape,
        compiler_params=pltpu.CompilerParams(dimension_semantics=("arbitrary",), vmem_limit_bytes=VMEM_LIMIT),
    )(*args)
    return res[:len(want_rows)], res[len(want_rows):]


def f_lnmod(x, sc, sh):
    return (_ln(x) * (1.0 + sc) + sh,)


def f_merge(p, ao, cv, zg, x, wbd, ps, wbp, wba, wbc, cb, clg, clb, bg, wo, gm, lg, lb):
    pm = mm_w(p, wbd) * ps
    co = jax.nn.silu(_ln(cv + cb) * clg + clb)
    y_pool = mm_w(pm, wbp)
    y_attn = mm_w(ao, wba)
    y_conv = mm_w(co, wbc)
    gates = jax.nn.sigmoid(zg + bg)
    merged = (gates[:, :D_MODEL] * y_pool + gates[:, D_MODEL:2 * D_MODEL] * y_attn
              + gates[:, 2 * D_MODEL:] * y_conv)
    mix = mm_w(merged, wo)
    return (_ln(ALPHA * x + gm * mix) * lg + lb,)


def f_relu2(hpre, b1):
    a = jax.nn.relu(hpre + b1)
    return (a * a,)


def f_ffout(x1, ff, b2, gf, lg, lb):
    return (_ln(ALPHA * x1 + gf * (ff + b2)) * lg + lb,)


def mm_big(name, a, b, kind, tiles, out_dtype, j_outer=False, after=None):
    if kind == "nn":
        o0, o1, red = a.shape[0], b.shape[1], a.shape[1]
    elif kind == "nt":
        o0, o1, red = a.shape[0], b.shape[0], a.shape[1]
    else:
        o0, o1, red = a.shape[1], b.shape[1], a.shape[0]
    t0, t1, tr = min(tiles[0], o0), min(tiles[1], o1), min(tiles[2], red)
    if kind == "nn":
        a_spec = pl.BlockSpec((t0, tr), lambda i, j, r: (i, r))
        b_spec = pl.BlockSpec((tr, t1), lambda i, j, r: (r, j))
        dims = (1, 0)
    elif kind == "nt":
        a_spec = pl.BlockSpec((t0, tr), lambda i, j, r: (i, r))
        b_spec = pl.BlockSpec((t1, tr), lambda i, j, r: (j, r))
        dims = (1, 1)
    else:
        a_spec = pl.BlockSpec((tr, t0), lambda i, j, r: (r, i))
        b_spec = pl.BlockSpec((tr, t1), lambda i, j, r: (r, j))
        dims = (0, 0)
    assert o0 % t0 == 0 and o1 % t1 == 0 and red % tr == 0, (name, a.shape, b.shape, tiles)
    n0, n1, nred = o0 // t0, o1 // t1, red // tr
    o_spec = pl.BlockSpec((t0, t1), lambda i, j, r: (i, j))
    if j_outer:
        swap = lambda spec: pl.BlockSpec(spec.block_shape, lambda j, i, r, f=spec.index_map: f(i, j, r))
        a_spec, b_spec, o_spec = swap(a_spec), swap(b_spec), swap(o_spec)
        grid = (n1, n0, nred)
    else:
        grid = (n0, n1, nred)

    deps = [] if after is None else [after]
    dep_specs = [pl.BlockSpec(d.shape, lambda i, j, r, nd=d.ndim: (0,) * nd) for d in deps]
    if nred == 1:
        def body(a_ref, b_ref, *rest):
            o_ref = rest[len(deps)]
            o_ref[...] = _dg(a_ref[...], b_ref[...], *dims).astype(o_ref.dtype)
        scratch = []
    else:
        def body(a_ref, b_ref, *rest):
            o_ref, acc_ref = rest[len(deps):]
            r = pl.program_id(2)
            part = _dg(a_ref[...], b_ref[...], *dims)

            @pl.when(r == 0)
            def _():
                acc_ref[...] = part

            @pl.when(jnp.logical_and(r > 0, r < nred - 1))
            def _():
                acc_ref[...] += part

            @pl.when(r == nred - 1)
            def _():
                o_ref[...] = (acc_ref[...] + part).astype(o_ref.dtype)
        scratch = [pltpu.VMEM((t0, t1), F32)]

    return pl.pallas_call(
        body, name=name, grid=grid,
        in_specs=[a_spec, b_spec] + dep_specs,
        out_specs=o_spec,
        out_shape=jax.ShapeDtypeStruct((o0, o1), out_dtype),
        scratch_shapes=scratch,
        compiler_params=pltpu.CompilerParams(dimension_semantics=("parallel", "parallel", "arbitrary"),
                                             vmem_limit_bytes=VMEM_LIMIT),
    )(a, b, *deps)


def mm_ff1_relu2(name, u2, w1, b1, tm, tn):
    m, k = u2.shape
    n = w1.shape[1]
    tm, tn = min(tm, m), min(tn, n)

    def body(a_ref, b_ref, bias_ref, hpre_ref, h_ref):
        acc = _dg(a_ref[...], b_ref[...], 1, 0)
        hpre_ref[...] = acc
        h_ref[...] = f_relu2(acc, bias_ref[...])[0].astype(h_ref.dtype)

    out = pl.BlockSpec((tm, tn), lambda j, i: (i, j))
    return pl.pallas_call(
        body, name=name, grid=(n // tn, m // tm),
        in_specs=[pl.BlockSpec((tm, k), lambda j, i: (i, 0)), pl.BlockSpec((k, tn), lambda j, i: (0, j)),
                  pl.BlockSpec((1, tn), lambda j, i: (0, j))],
        out_specs=[out, out],
        out_shape=[jax.ShapeDtypeStruct((m, n), F32), jax.ShapeDtypeStruct((m, n), BF16)],
        compiler_params=pltpu.CompilerParams(dimension_semantics=("parallel", "parallel"), vmem_limit_bytes=VMEM_LIMIT),
    )(u2, w1, b1)


def mm_dh_relu2(name, dff, w2, hpre, b1, tm, tn):
    m, k = dff.shape
    n = w2.shape[0]
    tm, tn = min(tm, m), min(tn, n)

    def body(a_ref, b_ref, hpre_ref, bias_ref, d_ref, db_ref):
        i = pl.program_id(1)
        dh = _dg(a_ref[...], b_ref[...], 1, 1)
        _, vjp = jax.vjp(f_relu2, hpre_ref[...], bias_ref[...])
        dhpre, db = vjp((dh,))
        d_ref[...] = dhpre.astype(d_ref.dtype)

        @pl.when(i == 0)
        def _():
            db_ref[...] = db

        @pl.when(i > 0)
        def _():
            db_ref[...] += db

    tile = pl.BlockSpec((tm, tn), lambda j, i: (i, j))
    col = pl.BlockSpec((1, tn), lambda j, i: (0, j))
    return pl.pallas_call(
        body, name=name, grid=(n // tn, m // tm),
        in_specs=[pl.BlockSpec((tm, k), lambda j, i: (i, 0)), pl.BlockSpec((tn, k), lambda j, i: (j, 0)), tile, col],
        out_specs=[tile, col],
        out_shape=[jax.ShapeDtypeStruct((m, n), BF16), jax.ShapeDtypeStruct((1, n), F32)],
        compiler_params=pltpu.CompilerParams(dimension_semantics=("parallel", "arbitrary"), vmem_limit_bytes=VMEM_LIMIT),
    )(dff, w2, hpre, b1)


POOL_PAD = 16
POOL_ROWS = 256


def pool_lin(name, x, wvec, transpose, out_dtype):
    arr, width, cb = x
    s = arr.shape[0]
    n_steps = s // POOL_ROWS

    def body(x_ref, w_ref, o_ref, xp_ref):
        wv = w_ref[...]
        zeros = jnp.zeros((POOL_PAD, width), F32)
        xp_ref[0:POOL_PAD, :] = zeros
        xp_ref[s + POOL_PAD:s + 2 * POOL_PAD, :] = zeros

        def count(t0):
            t = lax.broadcasted_iota(jnp.int32, (POOL_ROWS, width), 0) + (t0 + 1)
            return jnp.minimum(t.astype(F32), wv)

        def fill(i, carry):
            t0 = pl.multiple_of(i * POOL_ROWS, POOL_ROWS)
            v = x_ref[pl.ds(t0, POOL_ROWS), :].astype(F32)
            if transpose:
                v = v / count(t0)
            xp_ref[pl.ds(t0 + POOL_PAD, POOL_ROWS), :] = v
            return carry

        lax.fori_loop(0, n_steps, fill, 0)

        def step(i, carry):
            t0 = pl.multiple_of(i * POOL_ROWS, POOL_ROWS)
            win = xp_ref[pl.ds(t0, POOL_ROWS + 2 * POOL_PAD), :]
            acc = jnp.zeros((POOL_ROWS, width), F32)
            for j in range(POOL_PAD):
                off = POOL_PAD + j if transpose else POOL_PAD - j
                acc = acc + jnp.where(wv > j, win[off:off + POOL_ROWS, :], 0.0)
            cur = x_ref[pl.ds(t0, POOL_ROWS), :].astype(F32)
            res = acc - cur if transpose else acc / count(t0) - cur
            o_ref[pl.ds(t0, POOL_ROWS), :] = res.astype(o_ref.dtype)
            return carry

        lax.fori_loop(0, n_steps, step, 0)

    return pl.pallas_call(
        body, name=name, grid=(1,),
        in_specs=[pl.BlockSpec((s, width), lambda i, cb=cb: (0, cb)), pl.BlockSpec((1, width), lambda i: (0, 0))],
        out_specs=pl.BlockSpec((s, width), lambda i: (0, 0)),
        out_shape=jax.ShapeDtypeStruct((s, width), out_dtype),
        scratch_shapes=[pltpu.VMEM((s + 2 * POOL_PAD, width), F32)],
        compiler_params=pltpu.CompilerParams(dimension_semantics=("arbitrary",), vmem_limit_bytes=VMEM_LIMIT),
    )(arr, wvec)


CONV_PAD = 32
CONV_ROWS = 128


def _glu(a, g):
    return a * jax.nn.sigmoid(g)


def conv_fwd(name, zc, w):
    arr, width, cb = zc
    s = arr.shape[0]
    n_steps = s // CONV_ROWS
    lead = CONV_PAD - (CONV_WIDTH - 1)

    def body(z_ref, w_ref, o_ref, hp_ref):
        hp_ref[0:CONV_PAD, :] = jnp.zeros((CONV_PAD, D_CONV), F32)

        def fill(i, carry):
            t0 = pl.multiple_of(i * CONV_ROWS, CONV_ROWS)
            z = z_ref[pl.ds(t0, CONV_ROWS), :]
            hp_ref[pl.ds(t0 + CONV_PAD, CONV_ROWS), :] = _glu(z[:, :D_CONV], z[:, D_CONV:])
            return carry

        lax.fori_loop(0, n_steps, fill, 0)
        wv = w_ref[...]

        def step(i, carry):
            t0 = pl.multiple_of(i * CONV_ROWS, CONV_ROWS)
            win = hp_ref[pl.ds(t0, CONV_ROWS + CONV_PAD), :]
            acc = jnp.zeros((CONV_ROWS, D_CONV), F32)
            for k in range(CONV_WIDTH):
                acc = acc + wv[k:k + 1, :] * win[lead + k:lead + k + CONV_ROWS, :]
            o_ref[pl.ds(t0, CONV_ROWS), :] = acc
            return carry

        lax.fori_loop(0, n_steps, step, 0)

    return pl.pallas_call(
        body, name=name, grid=(1,),
        in_specs=[pl.BlockSpec((s, width), lambda i, cb=cb: (0, cb)), pl.BlockSpec(w.shape, lambda i: (0, 0))],
        out_specs=pl.BlockSpec((s, D_CONV), lambda i: (0, 0)),
        out_shape=jax.ShapeDtypeStruct((s, D_CONV), F32),
        scratch_shapes=[pltpu.VMEM((s + CONV_PAD, D_CONV), F32)],
        compiler_params=pltpu.CompilerParams(dimension_semantics=("arbitrary",), vmem_limit_bytes=VMEM_LIMIT),
    )(arr, w)


def conv_bwd(name, zc, w, dout):
    arr, width, cb = zc
    s = arr.shape[0]
    n_steps = s // CONV_ROWS
    lead = CONV_PAD - (CONV_WIDTH - 1)

    def body(z_ref, w_ref, d_ref, dz_ref, dw_ref, hp_ref, dp_ref):
        hp_ref[0:CONV_PAD, :] = jnp.zeros((CONV_PAD, D_CONV), F32)
        dp_ref[s:s + CONV_PAD, :] = jnp.zeros((CONV_PAD, D_CONV), F32)
        dw_ref[...] = jnp.zeros(dw_ref.shape, F32)

        def fill(i, carry):
            t0 = pl.multiple_of(i * CONV_ROWS, CONV_ROWS)
            z = z_ref[pl.ds(t0, CONV_ROWS), :]
            hp_ref[pl.ds(t0 + CONV_PAD, CONV_ROWS), :] = _glu(z[:, :D_CONV], z[:, D_CONV:])
            dp_ref[pl.ds(t0, CONV_ROWS), :] = d_ref[pl.ds(t0, CONV_ROWS), :]
            return carry

        lax.fori_loop(0, n_steps, fill, 0)
        wv = w_ref[...]

        def step(i, carry):
            t0 = pl.multiple_of(i * CONV_ROWS, CONV_ROWS)
            hwin = hp_ref[pl.ds(t0, CONV_ROWS + CONV_PAD), :]
            dwin = dp_ref[pl.ds(t0, CONV_ROWS + CONV_PAD), :]
            dcur = dwin[0:CONV_ROWS, :]
            dh = jnp.zeros((CONV_ROWS, D_CONV), F32)
            rows = []
            for k in range(CONV_WIDTH):
                rows.append(jnp.sum(dcur * hwin[lead + k:lead + k + CONV_ROWS, :], axis=0, keepdims=True))
                back = CONV_WIDTH - 1 - k
                dh = dh + wv[k:k + 1, :] * dwin[back:back + CONV_ROWS, :]
            rows.append(jnp.zeros((1, D_CONV), F32))
            dw_ref[...] += jnp.concatenate(rows, axis=0)
            z = z_ref[pl.ds(t0, CONV_ROWS), :]
            _, vjp = jax.vjp(_glu, z[:, :D_CONV], z[:, D_CONV:])
            da, dg = vjp(dh)
            dz_ref[pl.ds(t0, CONV_ROWS), :] = jnp.concatenate([da, dg], axis=1).astype(dz_ref.dtype)
            return carry

        lax.fori_loop(0, n_steps, step, 0)

    return pl.pallas_call(
        body, name=name, grid=(1,),
        in_specs=[pl.BlockSpec((s, width), lambda i, cb=cb: (0, cb)), pl.BlockSpec(w.shape, lambda i: (0, 0)),
                  pl.BlockSpec((s, D_CONV), lambda i: (0, 0))],
        out_specs=[pl.BlockSpec((s, width), lambda i: (0, 0)), pl.BlockSpec(w.shape, lambda i: (0, 0))],
        out_shape=[jax.ShapeDtypeStruct((s, width), BF16), jax.ShapeDtypeStruct(w.shape, F32)],
        scratch_shapes=[pltpu.VMEM((s + CONV_PAD, D_CONV), F32), pltpu.VMEM((s + CONV_PAD, D_CONV), F32)],
        compiler_params=pltpu.CompilerParams(dimension_semantics=("arbitrary",), vmem_limit_bytes=VMEM_LIMIT),
    )(arr, w, dout)


HEADS_PER_STEP = LANE // HEAD_DIM
CHUNKS_PER_TILE = ATT_TILE // CHUNK
KEY_BLOCKS = N_PREV * CHUNK // ATT_TILE + 1
KEY_SPAN = KEY_BLOCKS * ATT_TILE


def _attn_tile(q, *rest, missing_cols):
    kcat = jnp.concatenate(rest[:KEY_BLOCKS], axis=0)
    vcat = jnp.concatenate(rest[KEY_BLOCKS:2 * KEY_BLOCKS], axis=0)
    bias = rest[2 * KEY_BLOCKS]
    lane = lax.broadcasted_iota(jnp.int32, (1, LANE), 1)
    col = lax.broadcasted_iota(jnp.int32, (1, KEY_SPAN), 1)
    missing = col < missing_cols
    qs = q * (HEAD_DIM ** -0.5)
    o = jnp.zeros((ATT_TILE, LANE), F32)
    for h in range(HEADS_PER_STEP):
        in_head = jnp.logical_and(lane >= h * HEAD_DIM, lane < (h + 1) * HEAD_DIM)
        sc = mm_nt(jnp.where(in_head, qs, 0.0), kcat) + bias[h]
        sc = jnp.where(missing, NEG_INF, sc)
        m = jnp.max(sc, axis=-1, keepdims=True)
        e = jnp.exp(sc - lax.stop_gradient(m))
        p = e / jnp.sum(e, axis=-1, keepdims=True)
        o = o + jnp.where(in_head, mm_nn(p, vcat), 0.0)
    return o


def _missing_cols(n):
    return jnp.maximum((KEY_BLOCKS - 1 - n) * ATT_TILE, 0)


def _attn_in_specs(nt):
    def spec(col0, back):
        return pl.BlockSpec((ATT_TILE, LANE),
                            lambda hp, n, col0=col0, back=back: (jnp.clip(n - back, 0, nt - 1), col0 // LANE + hp))
    backs = list(range(KEY_BLOCKS - 1, -1, -1))
    return ([spec(Z_Q, 0)] + [spec(Z_K, b) for b in backs] + [spec(Z_V, b) for b in backs]
            + [pl.BlockSpec((HEADS_PER_STEP, ATT_TILE, KEY_SPAN), lambda hp, n: (hp, 0, 0))])


def attn_fwd(name, z, bias):
    s = z.shape[0]
    nt = s // ATT_TILE
    n_in = 2 + 2 * KEY_BLOCKS

    def body(*refs):
        o_ref = refs[n_in]
        vals = [r[...] for r in refs[:n_in]]
        o = _attn_tile(*vals, missing_cols=_missing_cols(pl.program_id(1)))
        o_ref[...] = o.astype(o_ref.dtype)

    return pl.pallas_call(
        body, name=name, grid=(N_HEADS // HEADS_PER_STEP, nt),
        in_specs=_attn_in_specs(nt),
        out_specs=pl.BlockSpec((ATT_TILE, LANE), lambda hp, n: (n, hp)),
        out_shape=jax.ShapeDtypeStruct((s, D_ATTN), BF16),
        compiler_params=pltpu.CompilerParams(dimension_semantics=("parallel", "parallel"), vmem_limit_bytes=VMEM_LIMIT),
    )(*([z] * (n_in - 1)), bias)


def attn_bwd(name, z, bias, do, after=None):
    s = z.shape[0]
    nt = s // ATT_TILE
    n_in = 2 + 2 * KEY_BLOCKS
    nc = KEY_BLOCKS - 1

    deps = [] if after is None else [after]
    n_out = n_in + 1 + len(deps)

    def body(*refs):
        do_ref = refs[n_in]
        dq_ref, dk_ref, dv_ref, db_ref = refs[n_out:n_out + 4]
        kacc, vacc = refs[n_out + 4:n_out + 4 + nc], refs[n_out + 4 + nc:]
        n = pl.program_id(1)

        @pl.when(n == 0)
        def _():
            db_ref[...] = jnp.zeros(db_ref.shape, F32)
            for acc in (*kacc, *vacc):
                acc[...] = jnp.zeros(acc.shape, F32)

        def shift(out_ref, accs, contrib):
            @pl.when(n >= nc)
            def _():
                first = accs[0][...] if contrib is None else accs[0][...] + contrib[0]
                out_ref[...] = first.astype(out_ref.dtype)
            for j in range(nc - 1):
                accs[j][...] = accs[j + 1][...] if contrib is None else accs[j + 1][...] + contrib[j + 1]
            if contrib is not None:
                accs[nc - 1][...] = contrib[nc]

        @pl.when(n < nt)
        def _():
            fn = functools.partial(_attn_tile, missing_cols=_missing_cols(n))
            _, vjp = jax.vjp(fn, *[r[...] for r in refs[:n_in]])
            grads = vjp(do_ref[...].astype(F32))
            dq_ref[...] = grads[0].astype(dq_ref.dtype)
            db_ref[...] += grads[n_in - 1]
            shift(dk_ref, kacc, grads[1:1 + KEY_BLOCKS])
            shift(dv_ref, vacc, grads[1 + KEY_BLOCKS:1 + 2 * KEY_BLOCKS])

        @pl.when(n >= nt)
        def _():
            shift(dk_ref, kacc, None)
            shift(dv_ref, vacc, None)

    o_cur = pl.BlockSpec((ATT_TILE, LANE), lambda hp, n: (jnp.minimum(n, nt - 1), hp))
    o_old = pl.BlockSpec((ATT_TILE, LANE), lambda hp, n: (jnp.maximum(n - nc, 0), hp))
    b_spec = pl.BlockSpec((HEADS_PER_STEP, ATT_TILE, KEY_SPAN), lambda hp, n: (hp, 0, 0))
    return pl.pallas_call(
        body, name=name, grid=(N_HEADS // HEADS_PER_STEP, nt + nc),
        in_specs=_attn_in_specs(nt) + [o_cur] + [pl.BlockSpec(d.shape, lambda hp, n: (0, 0)) for d in deps],
        out_specs=[o_cur, o_old, o_old, b_spec],
        out_shape=[jax.ShapeDtypeStruct((s, D_ATTN), BF16)] * 3 + [jax.ShapeDtypeStruct((N_HEADS, ATT_TILE, KEY_SPAN), F32)],
        scratch_shapes=[pltpu.VMEM((ATT_TILE, LANE), F32)] * (2 * nc),
        compiler_params=pltpu.CompilerParams(dimension_semantics=("parallel", "arbitrary"), vmem_limit_bytes=VMEM_LIMIT),
    )(*([z] * (n_in - 1)), bias, do, *deps)


def loss_head(name, y, tgt, tm):
    s, d = y.shape

    def body(y_ref, t_ref, l_ref, dy_ref):
        i = pl.program_id(0)
        diff = y_ref[...] - t_ref[...]
        dy_ref[...] = diff * (1.0 / d)
        part = 0.5 * jnp.sum(jnp.mean(diff * diff, axis=-1, keepdims=True), axis=0, keepdims=True)

        @pl.when(i == 0)
        def _():
            l_ref[...] = jnp.zeros(l_ref.shape, F32)

        l_ref[...] += jnp.broadcast_to(part, l_ref.shape)

    row = pl.BlockSpec((tm, d), lambda i: (i, 0))
    return pl.pallas_call(
        body, name=name, grid=(s // tm,), in_specs=[row, row],
        out_specs=[pl.BlockSpec((8, LANE), lambda i: (0, 0)), row],
        out_shape=[jax.ShapeDtypeStruct((8, LANE), F32), jax.ShapeDtypeStruct((s, d), F32)],
        compiler_params=pltpu.CompilerParams(dimension_semantics=("arbitrary",), vmem_limit_bytes=VMEM_LIMIT),
    )(y, tgt)


def adamw(name, w, g, m, v, tr):
    r, c = w.shape
    assert r % tr == 0, (name, w.shape, tr)

    def body(w_ref, g_ref, m_ref, v_ref, d_ref, nm_ref, nv_ref):
        gg = g_ref[...]
        m2 = ADAM_B1 * m_ref[...] + (1.0 - ADAM_B1) * gg
        v2 = ADAM_B2 * v_ref[...] + (1.0 - ADAM_B2) * (gg * gg)
        m_hat = m2 / (1.0 - ADAM_B1 ** ADAM_STEP)
        v_hat = v2 / (1.0 - ADAM_B2 ** ADAM_STEP)
        d_ref[...] = -ADAM_LR * (m_hat / (jnp.sqrt(v_hat) + ADAM_EPS) + ADAM_WD * w_ref[...])
        nm_ref[...] = m2
        nv_ref[...] = v2

    blk = pl.BlockSpec((tr, c), lambda i: (i, 0))
    return pl.pallas_call(
        body, name=name, grid=(r // tr,), in_specs=[blk] * 4, out_specs=[blk] * 3,
        out_shape=[jax.ShapeDtypeStruct((r, c), F32)] * 3,
        compiler_params=pltpu.CompilerParams(dimension_semantics=("parallel",), vmem_limit_bytes=VMEM_LIMIT),
    )(w, g, m, v)


def adamw_sum(name, w, layer_blocks, m, v, tr):
    rows, c = w.shape
    nl = len(layer_blocks)
    nb, r, _ = layer_blocks[0].shape
    assert rows == nl * r and r % tr == 0, (name, w.shape, layer_blocks[0].shape, tr)
    per = r // tr

    def body(*refs):
        w_ref, b_refs, (m_ref, v_ref, g_ref, d_ref, nm_ref, nv_ref) = refs[0], refs[1:1 + nl], refs[1 + nl:]
        i = pl.program_id(0)

        def update(b_ref):
            gg = b_ref[0].astype(F32)
            for j in range(1, nb):
                gg = gg + b_ref[j].astype(F32)
            g_ref[...] = gg
            m2 = ADAM_B1 * m_ref[...] + (1.0 - ADAM_B1) * gg
            v2 = ADAM_B2 * v_ref[...] + (1.0 - ADAM_B2) * (gg * gg)
            m_hat = m2 / (1.0 - ADAM_B1 ** ADAM_STEP)
            v_hat = v2 / (1.0 - ADAM_B2 ** ADAM_STEP)
            d_ref[...] = -ADAM_LR * (m_hat / (jnp.sqrt(v_hat) + ADAM_EPS) + ADAM_WD * w_ref[...])
            nm_ref[...] = m2
            nv_ref[...] = v2

        for l in range(nl):
            pl.when(jnp.logical_and(i >= l * per, i < (l + 1) * per))(functools.partial(update, b_refs[l]))

    blk = pl.BlockSpec((tr, c), lambda i: (i, 0))
    b_specs = [pl.BlockSpec((nb, tr, c), lambda i, l=l: (0, jnp.clip(i - l * per, 0, per - 1), 0)) for l in range(nl)]
    return pl.pallas_call(
        body, name=name, grid=(rows // tr,),
        in_specs=[blk] + b_specs + [blk, blk], out_specs=[blk] * 4,
        out_shape=[jax.ShapeDtypeStruct((rows, c), F32)] * 4,
        compiler_params=pltpu.CompilerParams(dimension_semantics=("arbitrary",), vmem_limit_bytes=VMEM_LIMIT),
    )(w, *layer_blocks, m, v)


def sum_blocks(name, blocks, tr):
    nb, r, c = blocks.shape
    assert r % tr == 0, (name, blocks.shape, tr)

    def body(b_ref, o_ref):
        acc = b_ref[0].astype(F32)
        for j in range(1, nb):
            acc = acc + b_ref[j].astype(F32)
        o_ref[...] = acc

    return pl.pallas_call(
        body, name=name, grid=(r // tr,),
        in_specs=[pl.BlockSpec((nb, tr, c), lambda i: (0, i, 0))],
        out_specs=pl.BlockSpec((tr, c), lambda i: (i, 0)),
        out_shape=jax.ShapeDtypeStruct((r, c), F32),
        compiler_params=pltpu.CompilerParams(dimension_semantics=("parallel",), vmem_limit_bytes=VMEM_LIMIT),
    )(blocks)


FLIPS = [(0, 0, 1), (1, 0, 0), (0, 1, 0), (1, 1, 0), (1, 0, 1), (0, 1, 1), (1, 1, 1)]
ANY = pl.BlockSpec(memory_space=pl.ANY)


def _me():
    return lax.axis_index("x"), lax.axis_index("y"), lax.axis_index("c")


def _flip(pos, f):
    return tuple((1 - p) if fi else p for p, fi in zip(pos, f))


def _idx(pos):
    return 4 * pos[0] + 2 * pos[1] + pos[2]


def all_gather_multi(name, shards):
    n = len(shards)

    def body(*refs):
        x_refs, out_refs, token = refs[:n], refs[n:2 * n], refs[2 * n]
        send_sems, recv_sems, local_sems = refs[2 * n + 1:]
        token[...] = jnp.zeros_like(token)
        x, y, cc = _me()
        me, sibling = (x, y, cc), (x, y, 1 - cc)
        chips = [(1 - x, y), (x, 1 - y), (1 - x, 1 - y)]

        def copy(a, k, block, to, src=None):
            dst = out_refs[a].at[_idx(block)]
            return pltpu.make_async_remote_copy(
                src_ref=dst if src is None else src, dst_ref=dst, send_sem=send_sems.at[7 * a + k],
                recv_sem=recv_sems.at[7 * a + k], device_id=to, device_id_type=MESH)

        mine = [pltpu.make_async_copy(x_refs[a], out_refs[a].at[_idx(me)], local_sems.at[a]) for a in range(n)]
        for cp in mine:
            cp.start()
        first = []
        for a in range(n):
            first.append(copy(a, 0, me, sibling, src=x_refs[a]))
            first += [copy(a, 1 + j, me, (*chip, cc), src=x_refs[a]) for j, chip in enumerate(chips)]
        for cp in first:
            cp.start()
        passed = []
        for j, chip in enumerate(chips):
            for a in range(n):
                copy(a, 1 + j, (*chip, cc), me).wait_recv()
                fwd = copy(a, 4 + j, (*chip, cc), sibling)
                fwd.start()
                passed.append(fwd)
        for a in range(n):
            copy(a, 0, sibling, me).wait_recv()
            for j, chip in enumerate(chips):
                copy(a, 4 + j, (*chip, 1 - cc), me).wait_recv()
        for cp in first + passed:
            cp.wait_send()
        for cp in mine:
            cp.wait()

    return pl.pallas_call(
        body, name=name, in_specs=[ANY] * n, out_specs=[ANY] * n + [pl.BlockSpec(memory_space=pltpu.VMEM)],
        out_shape=[jax.ShapeDtypeStruct((N_DEV,) + a.shape, a.dtype) for a in shards]
        + [jax.ShapeDtypeStruct((8, LANE), F32)],
        scratch_shapes=[pltpu.SemaphoreType.DMA((7 * n,)), pltpu.SemaphoreType.DMA((7 * n,)),
                        pltpu.SemaphoreType.DMA((n,))],
    )(*shards)


def all_to_all_multi(name, blocks):
    n = len(blocks)

    def body(*refs):
        in_refs, out_refs = refs[:n], refs[n:2 * n]
        send_sems, recv_sems, local_sems = refs[2 * n:]
        me = _me()
        mi = _idx(me)
        mine = [pltpu.make_async_copy(in_refs[a].at[mi], out_refs[a].at[mi], local_sems.at[a]) for a in range(n)]
        for cp in mine:
            cp.start()
        sends, recvs = [], []
        for k, f in enumerate(FLIPS):
            peer = _flip(me, f)
            pi = _idx(peer)
            for a in range(n):
                sems = dict(send_sem=send_sems.at[7 * a + k], recv_sem=recv_sems.at[7 * a + k],
                            device_id=peer, device_id_type=MESH)
                sends.append(pltpu.make_async_remote_copy(src_ref=in_refs[a].at[pi], dst_ref=out_refs[a].at[mi], **sems))
                recvs.append(pltpu.make_async_remote_copy(src_ref=in_refs[a].at[mi], dst_ref=out_refs[a].at[pi], **sems))
        for cp in sends:
            cp.start()
        for cp in recvs:
            cp.wait_recv()
        for cp in sends:
            cp.wait_send()
        for cp in mine:
            cp.wait()

    return pl.pallas_call(
        body, name=name, in_specs=[ANY] * n, out_specs=[ANY] * n,
        out_shape=[jax.ShapeDtypeStruct(a.shape, a.dtype) for a in blocks],
        scratch_shapes=[pltpu.SemaphoreType.DMA((7 * n,)), pltpu.SemaphoreType.DMA((7 * n,)),
                        pltpu.SemaphoreType.DMA((n,))],
    )(*blocks)


HBM = pl.BlockSpec(memory_space=pltpu.HBM)
SEM = pl.BlockSpec(memory_space=pltpu.SEMAPHORE)
DATAFLOW = pltpu.SideEffectType.DATAFLOW_SIDE_EFFECTING


def _exchange_copies(a_refs, l_refs, send_sems, recv_sems, gather):
    me = _me()
    mi = _idx(me)
    out = []
    for k, f in enumerate(FLIPS):
        peer = _flip(me, f)
        for a in range(len(a_refs)):
            src = a_refs[a] if gather else a_refs[a].at[_idx(peer)]
            out.append(pltpu.make_async_remote_copy(
                src_ref=src, dst_ref=l_refs[a].at[mi], send_sem=send_sems.at[7 * a + k],
                recv_sem=recv_sems.at[7 * a + k], device_id=peer, device_id_type=MESH))
    return out


def exchange_start(name, arrays, gather):
    n = len(arrays)
    lands = [lax.empty(((N_DEV,) + a.shape) if gather else a.shape, a.dtype) for a in arrays]

    def body(*refs):
        a_refs, l_refs = refs[:n], refs[n:2 * n]
        send_sems, recv_sems = refs[2 * n], refs[2 * n + 1]
        token = refs[4 * n + 2]
        for cp in _exchange_copies(a_refs, l_refs, send_sems, recv_sems, gather):
            cp.start()
        token[...] = jnp.zeros_like(token)

    hbm = lambda a: pltpu.HBM(a.shape, a.dtype)
    res = pl.pallas_call(
        body, name=name,
        out_shape=(pltpu.SemaphoreType.DMA((7 * n,)), pltpu.SemaphoreType.DMA((7 * n,)),
                   *[hbm(a) for a in arrays], *[hbm(a) for a in lands], jax.ShapeDtypeStruct((8, LANE), F32)),
        in_specs=[HBM] * (2 * n),
        out_specs=(SEM, SEM, *[HBM] * (2 * n), pl.BlockSpec(memory_space=pltpu.VMEM)),
        input_output_aliases={i: i + 2 for i in range(2 * n)},
        compiler_params=pltpu.CompilerParams(has_side_effects=DATAFLOW),
    )(*[pltpu.with_memory_space_constraint(a, pltpu.HBM) for a in arrays],
      *[pltpu.with_memory_space_constraint(a, pltpu.HBM) for a in lands])
    return res[0], res[1], list(res[2:2 + n]), list(res[2 + n:2 + 2 * n]), res[-1]


def exchange_wait(name, send_sems, recv_sems, arrays, lands, after, gather):
    n = len(arrays)
    after = list(after) if isinstance(after, (list, tuple)) else [after]

    def body(*refs):
        a_refs, l_refs = refs[:n], refs[n:2 * n]
        ssem, rsem = refs[2 * n], refs[2 * n + 1]
        for cp in _exchange_copies(a_refs, l_refs, ssem, rsem, gather):
            cp.wait_send()
            cp.wait_recv()

    hbm = lambda a: pltpu.HBM(a.shape, a.dtype)
    res = pl.pallas_call(
        body, name=name,
        out_shape=(*[hbm(a) for a in arrays], *[hbm(a) for a in lands]),
        in_specs=[HBM] * (2 * n) + [SEM, SEM] + [pl.BlockSpec(memory_space=pl.ANY)] * len(after),
        out_specs=tuple([HBM] * (2 * n)),
        input_output_aliases={i: i for i in range(2 * n)},
        compiler_params=pltpu.CompilerParams(has_side_effects=DATAFLOW),
    )(*arrays, *lands, send_sems, recv_sems, *after)
    return list(res[n:])


def ada_fwd(name, c_row, w_cat, b_lay):
    d = c_row.shape[1]
    ncol = w_cat.shape[1]
    vmem = pl.BlockSpec(memory_space=pltpu.VMEM)

    def body(c_ref, w_ref, b_ref, mod_ref, cact_ref, token, call, send, land, s1, r1, s2, r2):
        token[...] = jnp.zeros_like(token)
        me = _me()
        mi = _idx(me)
        call[mi] = c_ref[...]

        def exchange(src_of, dst_buf, ssem, rsem):
            sends, recvs = [], []
            for k, f in enumerate(FLIPS):
                peer = _flip(me, f)
                sends.append(pltpu.make_async_remote_copy(
                    src_ref=src_of(peer), dst_ref=dst_buf.at[mi], send_sem=ssem.at[k], recv_sem=rsem.at[k],
                    device_id=peer, device_id_type=MESH))
                recvs.append(pltpu.make_async_remote_copy(
                    src_ref=src_of(peer), dst_ref=dst_buf.at[_idx(peer)], send_sem=ssem.at[k], recv_sem=rsem.at[k],
                    device_id=peer, device_id_type=MESH))
            for cp in sends:
                cp.start()
            for cp in recvs:
                cp.wait_recv()
            for cp in sends:
                cp.wait_send()

        exchange(lambda peer: c_ref, call, s1, r1)
        for p in range(N_DEV):
            cact_ref[pl.ds(p, 1), :] = jax.nn.silu(call[p])
        res = _dg(cact_ref[...], w_ref[...], 1, 0)
        for p in range(N_DEV):
            send[p] = res[p:p + 1, :]
        land[mi] = send[mi]
        exchange(lambda peer: send.at[_idx(peer)], land, s2, r2)
        mod_ref[...] = land[...] + b_ref[...]

    return pl.pallas_call(
        body, name=name, in_specs=[vmem, vmem, vmem], out_specs=[vmem, vmem, vmem],
        out_shape=[jax.ShapeDtypeStruct((N_DEV, 1, ncol), F32), jax.ShapeDtypeStruct((N_DEV, d), F32),
                   jax.ShapeDtypeStruct((8, LANE), F32)],
        scratch_shapes=[pltpu.VMEM((N_DEV, 1, d), F32), pltpu.VMEM((N_DEV, 1, ncol), F32),
                        pltpu.VMEM((N_DEV, 1, ncol), F32),
                        pltpu.SemaphoreType.DMA((7,)), pltpu.SemaphoreType.DMA((7,)),
                        pltpu.SemaphoreType.DMA((7,)), pltpu.SemaphoreType.DMA((7,))],
        compiler_params=pltpu.CompilerParams(vmem_limit_bytes=VMEM_LIMIT),
    )(c_row, w_cat, b_lay)


POOL_WINDOWS = (2, 4, 8, 16)
POOL_GROUP = 64
N_REL = 2 * REL_CLIP + 1
PACK_COLS = 1024
SMALL_NAMES = ["b_ada", "b_gate", "w_pool", "pool_scale", "rel_bias", "conv_w", "conv_b", "conv_ln_g",
               "conv_ln_b", "ln_mix_g", "ln_mix_b", "b_ff1", "b_ff2", "ln_ff_g", "ln_ff_b"]
BIG_NAMES = ["w_in", "w_br_pool", "w_br_attn", "w_br_conv", "w_o", "w_ff1", "w_ff2"]
ROW_SHARDED = ("w_in", "w_o", "w_ff2")
WEIGHT_NAMES = ["w_ada", "b_ada", "w_in", "b_gate", "w_pool", "pool_scale", "rel_bias", "conv_w", "conv_b",
                "conv_ln_g", "conv_ln_b", "w_br_pool", "w_br_attn", "w_br_conv", "w_o", "ln_mix_g", "ln_mix_b",
                "w_ff1", "b_ff1", "w_ff2", "b_ff2", "ln_ff_g", "ln_ff_b"]


def _perm_rows(wt):
    return jnp.concatenate([wt[2304:], wt[1792:2304], wt[:256], wt[256:768], wt[768:1280], wt[1280:1792]], axis=0)


def _unperm_rows(wp):
    return jnp.concatenate([wp[Z_POOL:Z_Q], wp[Z_Q:Z_K], wp[Z_K:Z_V], wp[Z_V:], wp[Z_CONV:Z_POOL], wp[:Z_CONV]],
                           axis=0)


def _bias_table(rel_bias):
    far = jnp.broadcast_to(rel_bias[:, 2 * REL_CLIP:], (N_HEADS, BAND - REL_CLIP))
    near = rel_bias[:, REL_CLIP - CHUNK + 1:2 * REL_CLIP][:, ::-1]
    ext = jnp.concatenate([far, near, jnp.zeros((N_HEADS, 1), F32)], axis=1)
    length = BAND + CHUNK
    flat = jnp.tile(ext, (1, CHUNK + 1))
    skew = flat[:, CHUNK - 1:CHUNK - 1 + CHUNK * (length - 1)].reshape(N_HEADS, CHUNK, length - 1)
    return skew[:, :, :BAND]


def _bias_full(rel_bias):
    tab = _bias_table(rel_bias)
    return jnp.concatenate(
        [jnp.pad(tab, ((0, 0), (0, 0), (i * CHUNK, KEY_SPAN - BAND - i * CHUNK)), constant_values=NEG_INF)
         for i in range(CHUNKS_PER_TILE)], axis=1)


def _block_diag(w_pool):
    out = jnp.zeros((D_POOL, D_POOL), F32)
    for g in range(len(POOL_WINDOWS)):
        out = lax.dynamic_update_slice(out, w_pool[g], (g * POOL_GROUP, g * POOL_GROUP))
    return out


def _flat_pad(arrs, mult):
    flat = jnp.concatenate([a.reshape(-1) for a in arrs])
    pad = (-flat.shape[0]) % mult
    return jnp.pad(flat, (0, pad)) if pad else flat


def _unflat(flat, shapes):
    out, off = [], 0
    for shp in shapes:
        n = int(np.prod(shp))
        out.append(flat[off:off + n].reshape(shp))
        off += n
    return out


def _to_blocks(name, full):
    k, n = full.shape
    if name in ROW_SHARDED:
        return full.reshape(N_DEV, k // N_DEV, n)
    return full.reshape(k, N_DEV, n // N_DEV).transpose(1, 0, 2)


def _from_blocks(name, blocks):
    nb, r, c = blocks.shape
    if name in ROW_SHARDED:
        return blocks.reshape(nb * r, c)
    return blocks.transpose(1, 0, 2).reshape(r, nb * c)


class _Layer:
    pass


def _row(v):
    return v.reshape(1, -1)


def _layer_fwd(x, modr, w, wvec, fetch_rest):
    sh_m, sc_m, g_m, sh_f, sc_f, g_f = modr
    (u,) = row_fwd("lnmod_mix", f_lnmod, [x], [sc_m, sh_m], [(D_MODEL, BF16)], 512)
    z = mm_big("mm_in", u, w.w_in_t, "nt", (1024, 896, 1024), F32, j_outer=True)
    p = pool_lin("pool_fwd", (z, D_POOL, Z_POOL // D_POOL), wvec, False, F32)
    ao = attn_fwd("attn_fwd", z, w.bias)
    cv = conv_fwd("conv_fwd", (z, 2 * D_CONV, Z_CONV // (2 * D_CONV)), w.conv_w)
    fetch_rest(w, cv)
    mparams = [w.wbd, w.ps, w.wbp, w.wba, w.wbc, w.cb, w.clg, w.clb, w.bg, w.wo, g_m, w.lmg, w.lmb]
    (x1,) = row_fwd("merge", f_merge, [p, ao, cv, (z, 3 * D_MODEL, 0), x], mparams, [(D_MODEL, F32)], 256)
    (u2,) = row_fwd("lnmod_ff", f_lnmod, [x1], [sc_f, sh_f], [(D_MODEL, BF16)], 512)
    hpre, h = mm_ff1_relu2("mm_ff1", u2, w.w_ff1, w.b1, 1024, 1024)
    ff = mm_big("mm_ff2", h, w.w_ff2, "nn", (512, 1024, 4096), F32)
    (x2,) = row_fwd("ffout", f_ffout, [x1, ff], [w.b2, g_f, w.lfg, w.lfb], [(D_MODEL, F32)], 512)
    return x2, (x, u, z, p, ao, cv, x1, u2, hpre, h, ff, mparams)


GRAD_GROUPS = [("ff", ["w_ff2", "w_ff1"]), ("mix", ["w_o", "w_br_pool", "w_br_attn", "w_br_conv"]), ("in", ["w_in"])]


def _layer_bwd(dx2, saved, modr, w, wvec, ready):
    x, u, z, p, ao, cv, x1, u2, hpre, h, ff, mparams = saved
    sh_m, sc_m, g_m, sh_f, sc_f, g_f = modr
    g = {}
    (dx1a, dff), (g["b_ff2"], dgf, g["ln_ff_g"], g["ln_ff_b"]) = row_bwd(
        "ffout_bwd", f_ffout, [x1, ff], [w.b2, g_f, w.lfg, w.lfb], [dx2], 512, [(0, F32), (1, BF16)], [0, 1, 2, 3])
    dhpre, g["b_ff1"] = mm_dh_relu2("mm_dh", dff, w.w_ff2, hpre, w.b1, 1024, 1024)
    g["w_ff2"] = mm_big("mm_dw_ff2", h, dff, "tn", (1024, 1024, 2048), BF16)
    du2 = mm_big("mm_du2", dhpre, w.w_ff1, "nt", (512, 1024, 4096), F32)
    g["w_ff1"] = mm_big("mm_dw_ff1", u2, dhpre, "tn", (1024, 1024, 2048), BF16)
    sc_f = sc_f + ready("ff", g)[0:1, 0:1]
    (dx1,), (dscf, dshf) = row_bwd("lnmod_ff_bwd", f_lnmod, [x1], [sc_f, sh_f], [du2], 512, [(0, F32)], [0, 1],
                                   add_to=dx1a)
    (dp, dao, dcv, dzg, dxa), dm = row_bwd(
        "merge_bwd", f_merge, [p, ao, cv, (z, 3 * D_MODEL, 0), x], mparams, [dx1], 256,
        [(0, F32), (1, BF16), (2, F32), (3, BF16), (4, F32)], list(range(13)))
    (dwbd, g["pool_scale"], g["w_br_pool"], g["w_br_attn"], g["w_br_conv"], g["conv_b"], g["conv_ln_g"],
     g["conv_ln_b"], g["b_gate"], g["w_o"], dgm, g["ln_mix_g"], g["ln_mix_b"]) = dm
    g["w_pool"] = jnp.stack([dwbd[i * POOL_GROUP:(i + 1) * POOL_GROUP, i * POOL_GROUP:(i + 1) * POOL_GROUP]
                             for i in range(len(POOL_WINDOWS))])
    tok = ready("mix", g)
    dzp = pool_lin("pool_bwd", (dp, D_POOL, 0), wvec + tok[0:1, 0:1], True, BF16)
    dq, dk, dv, dbias = attn_bwd("attn_bwd", z, w.bias, dao, after=tok)
    (g["rel_bias"],) = w.bias_vjp(dbias)
    dzc, dcw = conv_bwd("conv_bwd", (z, 2 * D_CONV, Z_CONV // (2 * D_CONV)), w.conv_w + tok[0:1, 0:1], dcv)
    g["conv_w"] = dcw[:CONV_WIDTH]
    dz = jnp.concatenate([dzg, dzc, dzp, dq, dk, dv], axis=1)
    du = mm_big("mm_du", dz, w.w_in_t, "nn", (512, 1024, D_IN), F32)
    (dx,), (dscm, dshm) = row_bwd("lnmod_mix_bwd", f_lnmod, [x], [sc_m, sh_m], [du], 512, [(0, F32)], [0, 1],
                                  add_to=dxa)
    g["dmod"] = jnp.concatenate([dshm, dscm, dgm, dshf, dscf, dgf], axis=1)
    tok = ready("small", g)
    g["w_in"] = _unperm_rows(mm_big("mm_dw_in", dz, u, "tn", (896, 1024, 2048), BF16, after=tok))
    return dx, g, ready("in", g)


def kernel(x, c, w_ada, b_ada, w_in, b_gate, w_pool, pool_scale, rel_bias, conv_w, conv_b, conv_ln_g, conv_ln_b, w_br_pool, w_br_attn, w_br_conv, w_o, ln_mix_g, ln_mix_b, w_ff1, b_ff1, w_ff2, b_ff2, ln_ff_g, ln_ff_b, loss_target, m_w_ada, m_b_ada, m_w_in, m_b_gate, m_w_pool, m_pool_scale, m_rel_bias, m_conv_w, m_conv_b, m_conv_ln_g, m_conv_ln_b, m_w_br_pool, m_w_br_attn, m_w_br_conv, m_w_o, m_ln_mix_g, m_ln_mix_b, m_w_ff1, m_b_ff1, m_w_ff2, m_b_ff2, m_ln_ff_g, m_ln_ff_b, v_w_ada, v_b_ada, v_w_in, v_b_gate, v_w_pool, v_pool_scale, v_rel_bias, v_conv_w, v_conv_b, v_conv_ln_g, v_conv_ln_b, v_w_br_pool, v_w_br_attn, v_w_br_conv, v_w_o, v_ln_mix_g, v_ln_mix_b, v_w_ff1, v_b_ff1, v_w_ff2, v_b_ff2, v_ln_ff_g, v_ln_ff_b):
    args = dict(locals())
    wts = {n: args[n] for n in WEIGHT_NAMES}
    mom = {n: args["m_" + n] for n in WEIGHT_NAMES}
    var = {n: args["v_" + n] for n in WEIGHT_NAMES}
    me = 4 * lax.axis_index("x") + 2 * lax.axis_index("y") + lax.axis_index("c")
    xs, tgt = x[0], loss_target[0]
    nc_ada = w_ada.shape[2]
    wvec = jnp.asarray(np.repeat(np.array(POOL_WINDOWS, np.float32), POOL_GROUP)[None, :])

    w_cat = jnp.concatenate([w_ada[l] for l in range(DEPTH)], axis=1)
    b_lay = b_ada.reshape(DEPTH, N_DEV, nc_ada).transpose(1, 0, 2).reshape(N_DEV, 1, DEPTH * nc_ada)
    land, cact, ada_token = ada_fwd("ada_fwd", c, w_cat, b_lay)
    mod = land.reshape(N_DEV, DEPTH, nc_ada).transpose(1, 0, 2).reshape(DEPTH, 6 * D_MODEL)
    modr = [[mod[l:l + 1, i * D_MODEL:(i + 1) * D_MODEL] for i in range(6)] for l in range(DEPTH)]

    cw_pack = _flat_pad([conv_w], 8 * LANE).reshape(-1, LANE) + ada_token[0:1, 0:1]
    xlayout = lambda n, a: jnp.swapaxes(a, -1, -2) if n == "w_in" else a
    shards = [[xlayout(n, wts[n][l]).astype(BF16) for n in BIG_NAMES] for l in range(DEPTH)]
    g_in0, cw_all, token = all_gather_multi("gather_w_in_l0", [shards[0][0], cw_pack])
    cw_all = cw_all.reshape(N_DEV, -1)[:, :conv_w.size]
    conv_full = cw_all.reshape((N_DEV,) + conv_w.shape).transpose(1, 2, 0, 3).reshape(DEPTH, CONV_WIDTH, D_CONV)
    gathers = {}
    for key, arrs in [("rest_l0", shards[0][1:]), ("w_in_l1", shards[1][:1]), ("rest_l1", shards[1][1:])]:
        arrs = [a + token[0:1, 0:1].astype(a.dtype) for a in arrs]
        ssem, rsem, thru, lands, token = exchange_start(f"gather_{key}_start", arrs, True)
        gathers[key] = (arrs, ssem, rsem, thru, lands)
    modr[0] = [r + token[0:1, 0:1] for r in modr[0]]

    def with_own(lands_, own):
        return [lax.dynamic_update_index_in_dim(ld, o, me, axis=0) for ld, o in zip(lands_, own)]

    def gathered(key, after):
        arrs, ssem, rsem, thru, lands = gathers[key]
        return with_own(exchange_wait(f"gather_{key}_wait", ssem, rsem, thru, lands, after, True), arrs)

    def fetch_rest_for(l):
        def fetch_rest(w, after):
            full = {n: _from_blocks(n, g) for n, g in zip(BIG_NAMES[1:], gathered(f"rest_l{l}", after))}
            w.wbp, w.wba, w.wbc = full["w_br_pool"], full["w_br_attn"], full["w_br_conv"]
            w.wo, w.w_ff1, w.w_ff2 = full["w_o"], full["w_ff1"], full["w_ff2"]
        return fetch_rest

    def layer_weights(l, in_blocks):
        w = _Layer()
        w.w_in_t = _perm_rows(_from_blocks("w_in", in_blocks))
        w.wbd = _block_diag(w_pool[l])
        w.ps, w.cb, w.clg, w.clb = _row(pool_scale[l]), _row(conv_b[l]), _row(conv_ln_g[l]), _row(conv_ln_b[l])
        w.bg, w.lmg, w.lmb = _row(b_gate[l]), _row(ln_mix_g[l]), _row(ln_mix_b[l])
        w.b1, w.b2, w.lfg, w.lfb = _row(b_ff1[l]), _row(b_ff2[l]), _row(ln_ff_g[l]), _row(ln_ff_b[l])
        w.conv_w = jnp.pad(conv_full[l], ((0, CONV_PAD - CONV_WIDTH), (0, 0)))
        w.bias, w.bias_vjp = jax.vjp(_bias_full, rel_bias[l])
        return w

    layers, saved = [layer_weights(0, g_in0)], []
    h, sv = _layer_fwd(xs, modr[0], layers[0], wvec, fetch_rest_for(0))
    saved.append(sv)
    layers.append(layer_weights(1, gathered("w_in_l1", h)[0]))
    h, sv = _layer_fwd(h, modr[1], layers[1], wvec, fetch_rest_for(1))
    saved.append(sv)
    lpart, dy = loss_head("loss_head", h, tgt, 512)
    loss = lax.psum(lpart[0, 0], ("x", "y", "c"))
    grads, dmods = [None] * DEPTH, [None] * DEPTH
    pending = {}
    small_shapes = [wts[n].shape if n != "conv_w" else (DEPTH, CONV_WIDTH, D_CONV) for n in SMALL_NAMES]

    def ready_for(l):
        def ready(group, g):
            if group == "small":
                if l > 0:
                    return None
                both = [g, grads[1]]
                local = [jnp.concatenate([both[k]["dmod"] for k in range(DEPTH)], axis=0)]
                local += [jnp.stack([both[k][n].reshape(shp[1:]) for k in range(DEPTH)])
                          for n, shp in zip(SMALL_NAMES[1:], small_shapes[1:])]
                pack = _flat_pad(local, 8 * LANE).reshape(-1, LANE)
                ssem, rsem, thru, lands, token = exchange_start("gather_small_grads_start", [pack], True)
                pending["small"] = (pack, ssem, rsem, thru, lands)
                return token
            names = dict(GRAD_GROUPS)[group]
            blocks = [_to_blocks(n, g[n]).astype(BF16) for n in names]
            ssem, rsem, thru, lands, token = exchange_start(f"scatter_l{l}_{group}_start", blocks, False)
            pending[(l, group)] = (names, blocks, ssem, rsem, thru, lands)
            return token
        return ready

    def received(l, group, after):
        names, blocks, ssem, rsem, thru, lands = pending[(l, group)]
        lands = exchange_wait(f"scatter_l{l}_{group}_wait", ssem, rsem, thru, lands, after, False)
        own = [lax.dynamic_index_in_dim(b, me, axis=0, keepdims=False) for b in blocks]
        return dict(zip(names, with_own(lands, own)))

    dy, grads[1], token = _layer_bwd(dy, saved[1], modr[1], layers[1], wvec, ready_for(1))
    modr0 = [r + token[0:1, 0:1] for r in modr[0]]
    dy, grads[0], token = _layer_bwd(dy, saved[0], modr0, layers[0], wvec, ready_for(0))
    recv = [{} for _ in range(DEPTH)]
    for l, group in [(1, "ff"), (1, "mix"), (1, "in"), (0, "ff"), (0, "mix")]:
        recv[l].update(received(l, group, token))
    grad_x = dy[None]
    dmod_size = DEPTH * 6 * D_MODEL

    small_pack, ssem, rsem, thru, lands = pending["small"]
    (small_all,) = with_own(exchange_wait("gather_small_grads_wait", ssem, rsem, thru, lands, token, True), [small_pack])
    small_sum = sum_blocks("sum_small_grads", small_all, small_pack.shape[0]).reshape(-1)
    gsmall = dict(zip(SMALL_NAMES, _unflat(small_sum, small_shapes)))
    gw = dict(gsmall)
    gw["conv_w"] = lax.dynamic_slice_in_dim(gsmall["conv_w"], me * conv_w.shape[2], conv_w.shape[2], axis=2)

    dmod_all = small_all.reshape(N_DEV, -1)[:, :dmod_size].reshape(N_DEV, DEPTH, N_DEV, nc_ada)
    dm_mine = lax.dynamic_index_in_dim(dmod_all, me, axis=2, keepdims=False).reshape(N_DEV, DEPTH * nc_ada)
    cact_t = jnp.pad(cact.T, ((0, 0), (0, LANE - N_DEV)))
    dm_pad = jnp.pad(dm_mine, ((0, LANE - N_DEV), (0, 0)))
    dw_cat = mm_big("mm_dw_ada", cact_t, dm_pad, "nn", (D_MODEL, DEPTH * nc_ada, LANE), F32)
    gw["w_ada"] = jnp.stack([dw_cat[:, l * nc_ada:(l + 1) * nc_ada] for l in range(DEPTH)])

    delta, new_m, new_v = {}, {}, {}
    packs = [_flat_pad([src[n] for n in SMALL_NAMES], 8 * LANE).reshape(-1, LANE) for src in (wts, gw, mom, var)]
    small_res = adamw("adamw_small", *packs, packs[0].shape[0])
    shapes = [wts[n].shape for n in SMALL_NAMES]
    for out, flat in zip((delta, new_m, new_v), small_res):
        out.update(dict(zip(SMALL_NAMES, _unflat(flat.reshape(-1), shapes))))
    for n in ["w_ada", "w_ff2", "w_ff1", "w_o", "w_br_pool", "w_br_attn", "w_br_conv", "w_in"]:
        shp = xlayout(n, wts[n]).shape
        two_d = lambda a, shp=shp, n=n: xlayout(n, a).reshape(shp[0] * shp[1], shp[2])
        tr = 224 if n == "w_in" else min(256, shp[1])
        if n == "w_ada":
            res = (gw[n],) + tuple(adamw("adamw_" + n, two_d(wts[n]), two_d(gw[n]), two_d(mom[n]), two_d(var[n]), tr))
        else:
            if n == "w_in":
                done = [small_res[2]] + [new_v[k] for k in ["w_ada"] + BIG_NAMES[1:]]
                recv[0].update(received(0, "in", done))
            res = adamw_sum("adamw_" + n, two_d(wts[n]), [recv[l][n] for l in range(DEPTH)], two_d(mom[n]),
                            two_d(var[n]), tr)
        gw[n], delta[n], new_m[n], new_v[n] = [xlayout(n, a.reshape(shp)) for a in res]

    return (loss, grad_x, *[gw[n] for n in WEIGHT_NAMES], *[delta[n] for n in WEIGHT_NAMES],
            *[new_m[n] for n in WEIGHT_NAMES], *[new_v[n] for n in WEIGHT_NAMES])
```

```python
import functools

import jax
import jax.numpy as jnp
import numpy as np
from jax import lax
from jax.experimental import pallas as pl
from jax.experimental.pallas import tpu as pltpu

F32 = jnp.float32
BF16 = jnp.bfloat16
MESH = pl.DeviceIdType.MESH

D_MODEL = 1024
DEPTH = 2
CHUNK = 64
N_HEADS = 8
HEAD_DIM = 64
D_POOL = 256
D_ATTN = 512
D_CONV = 256
CONV_WIDTH = 31
D_FF = 4096
D_IN = 5376
N_PREV = 8
BAND = (N_PREV + 1) * CHUNK
REL_CLIP = 128
ALPHA = (2.0 * DEPTH) ** 0.25
LN_EPS = 1e-5
NEG_INF = -1e30
N_DEV = 8

ADAM_LR, ADAM_B1, ADAM_B2, ADAM_EPS, ADAM_WD, ADAM_STEP = 0.001, 0.9, 0.999, 1e-08, 0.01, 10

VMEM_LIMIT = 56 * 1024 * 1024

Z_POOL, Z_Q, Z_K, Z_V, Z_CONV, Z_GATE = 0, 256, 768, 1280, 1792, 2304
GATE_BLOCK = 768
ATT_TILE = 512
LANE = 128


def _dg(a, b, ca, cb):
    return lax.dot_general(a.astype(BF16), b.astype(BF16), (((ca,), (cb,)), ((), ())),
                           preferred_element_type=F32)


@jax.custom_vjp
def mm_nn(a, b):
    return _dg(a, b, 1, 0)


def _mm_nn_fwd(a, b):
    return _dg(a, b, 1, 0), (a, b)


def _mm_nn_bwd(res, g):
    a, b = res
    return _dg(g, b, 1, 1).astype(a.dtype), _dg(a, g, 0, 0).astype(b.dtype)


mm_nn.defvjp(_mm_nn_fwd, _mm_nn_bwd)


@jax.custom_vjp
def mm_nt(a, b):
    return _dg(a, b, 1, 1)


def _mm_nt_fwd(a, b):
    return _dg(a, b, 1, 1), (a, b)


def _mm_nt_bwd(res, g):
    a, b = res
    return _dg(g, b, 1, 0).astype(a.dtype), _dg(g, a, 0, 0).astype(b.dtype)


mm_nt.defvjp(_mm_nt_fwd, _mm_nt_bwd)


@jax.custom_vjp
def mm_nn_shadow(a, w, shadow):
    return _dg(a, w, 1, 0)


def _mm_nn_shadow_fwd(a, w, shadow):
    return _dg(a, w, 1, 0), (a, w)


def _mm_nn_shadow_bwd(res, g):
    a, w = res
    return _dg(g, w, 1, 1).astype(a.dtype), jnp.zeros_like(w), _dg(a, g, 0, 0)


mm_nn_shadow.defvjp(_mm_nn_shadow_fwd, _mm_nn_shadow_bwd)


def mm_w(a, w):
    return mm_nn_shadow(a, w[0], w[1]) if isinstance(w, tuple) else mm_nn(a, w)


def _ln(x):
    mu = jnp.mean(x, axis=-1, keepdims=True)
    xc = x - mu
    var = jnp.mean(xc * xc, axis=-1, keepdims=True)
    return xc * lax.rsqrt(var + LN_EPS)


def _norm_rows(rows):
    return [r if isinstance(r, tuple) else (r, r.shape[1], 0) for r in rows]


def _row_spec(tm, r):
    _, width, cb = r
    return pl.BlockSpec((tm, width), lambda i, cb=cb: (i, cb))


def _full_spec(a):
    nd = a.ndim
    return pl.BlockSpec(a.shape, lambda i, nd=nd: (0,) * nd)


def row_fwd(name, f, rows, params, outs, tm):
    rows = _norm_rows(rows)
    s = rows[0][0].shape[0]
    nr, npar = len(rows), len(params)

    def body(*refs):
        r = [x[...].astype(F32) for x in refs[:nr]]
        p = [x[...] for x in refs[nr:nr + npar]]
        res = f(*r, *p)
        for o_ref, o in zip(refs[nr + npar:], res):
            o_ref[...] = o.astype(o_ref.dtype)

    return pl.pallas_call(
        body, name=name, grid=(s // tm,),
        in_specs=[_row_spec(tm, r) for r in rows] + [_full_spec(p) for p in params],
        out_specs=[pl.BlockSpec((tm, w), lambda i: (i, 0)) for w, _ in outs],
        out_shape=[jax.ShapeDtypeStruct((s, w), dt) for w, dt in outs],
        compiler_params=pltpu.CompilerParams(dimension_semantics=("parallel",), vmem_limit_bytes=VMEM_LIMIT),
    )(*[r[0] for r in rows], *params)


def row_bwd(name, f, rows, params, douts, tm, want_rows, want_params, add_to=None):
    rows = _norm_rows(rows)
    s = rows[0][0].shape[0]
    nr, npar, nd = len(rows), len(params), len(douts)
    nadd = 0 if add_to is None else 1
    n_in = nr + npar + nd + nadd

    def body(*refs):
        i = pl.program_id(0)
        r = [x[...].astype(F32) for x in refs[:nr]]
        p = [(x[...], jnp.zeros(x.shape, F32)) if x.dtype == BF16 else x[...] for x in refs[nr:nr + npar]]
        d = [x[...].astype(F32) for x in refs[nr + npar:nr + npar + nd]]
        _, vjp = jax.vjp(f, *r, *p)
        g = vjp(tuple(d))
        out_refs = refs[n_in:]
        for k, (idx, _) in enumerate(want_rows):
            val = g[idx]
            if nadd and k == 0:
                val = val + refs[n_in - 1][...].astype(F32)
            out_refs[k][...] = val.astype(out_refs[k].dtype)
        for k, idx in enumerate(want_params):
            gp = g[nr + idx]
            gp = gp[1] if isinstance(gp, tuple) else gp
            o_ref = out_refs[len(want_rows) + k]

            @pl.when(i == 0)
            def _():
                o_ref[...] = gp

            @pl.when(i > 0)
            def _():
                o_ref[...] += gp

    in_specs = ([_row_spec(tm, r) for r in rows] + [_full_spec(p) for p in params]
                + [pl.BlockSpec((tm, d.shape[1]), lambda i: (i, 0)) for d in douts])
    args = [r[0] for r in rows] + list(params) + list(douts)
    if nadd:
        in_specs.append(pl.BlockSpec((tm, add_to.shape[1]), lambda i: (i, 0)))
        args.append(add_to)
    out_specs = ([pl.BlockSpec((tm, rows[idx][1]), lambda i: (i, 0)) for idx, _ in want_rows]
                 + [_full_spec(params[idx]) for idx in want_params])
    out_shape = ([jax.ShapeDtypeStruct((s, rows[idx][1]), dt) for idx, dt in want_rows]
                 + [jax.ShapeDtypeStruct(params[idx].shape, F32) for idx in want_params])
    res = pl.pallas_call(
        body, name=name, grid=(s // tm,), in_specs=in_specs, out_specs=out_specs, out_shape=out_shape,
        compiler_params=pltpu.CompilerParams(dimension_semantics=("arbitrary",), vmem_limit_bytes=VMEM_LIMIT),
    )(*args)
    return res[:len(want_rows)], res[len(want_rows):]


def f_lnmod(x, sc, sh):
    return (_ln(x) * (1.0 + sc) + sh,)


def f_merge(p, ao, cv, zg0, zg1, zg2, zg3, x, wbd, ps, wbp, wba, wbc, cb, clg, clb, bg, wo, gm, lg, lb):
    zg = jnp.concatenate([zg0, zg1, zg2, zg3], axis=1)
    pm = mm_w(p, wbd) * ps
    co = jax.nn.silu(_ln(cv + cb) * clg + clb)
    y_pool = mm_w(pm, wbp)
    y_attn = mm_w(ao, wba)
    y_conv = mm_w(co, wbc)
    gates = jax.nn.sigmoid(zg + bg)
    merged = (gates[:, :D_MODEL] * y_pool + gates[:, D_MODEL:2 * D_MODEL] * y_attn
              + gates[:, 2 * D_MODEL:] * y_conv)
    mix = mm_w(merged, wo)
    return (_ln(ALPHA * x + gm * mix) * lg + lb,)


def f_relu2(hpre, b1):
    a = jax.nn.relu(hpre + b1)
    return (a * a,)


def f_ffout(x1, ff, b2, gf, lg, lb):
    return (_ln(ALPHA * x1 + gf * (ff + b2)) * lg + lb,)


def mm_big(name, a, b, kind, tiles, out_dtype, j_outer=False, after=None):
    if kind == "nn":
        o0, o1, red = a.shape[0], b.shape[1], a.shape[1]
    elif kind == "nt":
        o0, o1, red = a.shape[0], b.shape[0], a.shape[1]
    else:
        o0, o1, red = a.shape[1], b.shape[1], a.shape[0]
    t0, t1, tr = min(tiles[0], o0), min(tiles[1], o1), min(tiles[2], red)
    if kind == "nn":
        a_spec = pl.BlockSpec((t0, tr), lambda i, j, r: (i, r))
        b_spec = pl.BlockSpec((tr, t1), lambda i, j, r: (r, j))
        dims = (1, 0)
    elif kind == "nt":
        a_spec = pl.BlockSpec((t0, tr), lambda i, j, r: (i, r))
        b_spec = pl.BlockSpec((t1, tr), lambda i, j, r: (j, r))
        dims = (1, 1)
    else:
        a_spec = pl.BlockSpec((tr, t0), lambda i, j, r: (r, i))
        b_spec = pl.BlockSpec((tr, t1), lambda i, j, r: (r, j))
        dims = (0, 0)
    assert o0 % t0 == 0 and o1 % t1 == 0 and red % tr == 0, (name, a.shape, b.shape, tiles)
    n0, n1, nred = o0 // t0, o1 // t1, red // tr
    o_spec = pl.BlockSpec((t0, t1), lambda i, j, r: (i, j))
    if j_outer:
        swap = lambda spec: pl.BlockSpec(spec.block_shape, lambda j, i, r, f=spec.index_map: f(i, j, r))
        a_spec, b_spec, o_spec = swap(a_spec), swap(b_spec), swap(o_spec)
        grid = (n1, n0, nred)
    else:
        grid = (n0, n1, nred)

    deps = [] if after is None else [after]
    dep_specs = [pl.BlockSpec(d.shape, lambda i, j, r, nd=d.ndim: (0,) * nd) for d in deps]
    if nred == 1:
        def body(a_ref, b_ref, *rest):
            o_ref = rest[len(deps)]
            o_ref[...] = _dg(a_ref[...], b_ref[...], *dims).astype(o_ref.dtype)
        scratch = []
    else:
        def body(a_ref, b_ref, *rest):
            o_ref, acc_ref = rest[len(deps):]
            r = pl.program_id(2)
            part = _dg(a_ref[...], b_ref[...], *dims)

            @pl.when(r == 0)
            def _():
                acc_ref[...] = part

            @pl.when(jnp.logical_and(r > 0, r < nred - 1))
            def _():
                acc_ref[...] += part

            @pl.when(r == nred - 1)
            def _():
                o_ref[...] = (acc_ref[...] + part).astype(o_ref.dtype)
        scratch = [pltpu.VMEM((t0, t1), F32)]

    return pl.pallas_call(
        body, name=name, grid=grid,
        in_specs=[a_spec, b_spec] + dep_specs,
        out_specs=o_spec,
        out_shape=jax.ShapeDtypeStruct((o0, o1), out_dtype),
        scratch_shapes=scratch,
        compiler_params=pltpu.CompilerParams(dimension_semantics=("parallel", "parallel", "arbitrary"),
                                             vmem_limit_bytes=VMEM_LIMIT),
    )(a, b, *deps)


def mm_ff1_relu2(name, u2, w1, b1, tm, tn):
    m, k = u2.shape
    n = w1.shape[1]
    tm, tn = min(tm, m), min(tn, n)

    def body(a_ref, b_ref, bias_ref, hpre_ref, h_ref):
        acc = _dg(a_ref[...], b_ref[...], 1, 0)
        hpre_ref[...] = acc
        h_ref[...] = f_relu2(acc, bias_ref[...])[0].astype(h_ref.dtype)

    out = pl.BlockSpec((tm, tn), lambda j, i: (i, j))
    return pl.pallas_call(
        body, name=name, grid=(n // tn, m // tm),
        in_specs=[pl.BlockSpec((tm, k), lambda j, i: (i, 0)), pl.BlockSpec((k, tn), lambda j, i: (0, j)),
                  pl.BlockSpec((1, tn), lambda j, i: (0, j))],
        out_specs=[out, out],
        out_shape=[jax.ShapeDtypeStruct((m, n), F32), jax.ShapeDtypeStruct((m, n), BF16)],
        compiler_params=pltpu.CompilerParams(dimension_semantics=("parallel", "parallel"), vmem_limit_bytes=VMEM_LIMIT),
    )(u2, w1, b1)


def mm_dh_relu2(name, dff, w2, hpre, b1, tm, tn):
    m, k = dff.shape
    n = w2.shape[0]
    tm, tn = min(tm, m), min(tn, n)

    def body(a_ref, b_ref, hpre_ref, bias_ref, d_ref, db_ref):
        i = pl.program_id(1)
        dh = _dg(a_ref[...], b_ref[...], 1, 1)
        _, vjp = jax.vjp(f_relu2, hpre_ref[...], bias_ref[...])
        dhpre, db = vjp((dh,))
        d_ref[...] = dhpre.astype(d_ref.dtype)

        @pl.when(i == 0)
        def _():
            db_ref[...] = db

        @pl.when(i > 0)
        def _():
            db_ref[...] += db

    tile = pl.BlockSpec((tm, tn), lambda j, i: (i, j))
    col = pl.BlockSpec((1, tn), lambda j, i: (0, j))
    return pl.pallas_call(
        body, name=name, grid=(n // tn, m // tm),
        in_specs=[pl.BlockSpec((tm, k), lambda j, i: (i, 0)), pl.BlockSpec((tn, k), lambda j, i: (j, 0)), tile, col],
        out_specs=[tile, col],
        out_shape=[jax.ShapeDtypeStruct((m, n), BF16), jax.ShapeDtypeStruct((1, n), F32)],
        compiler_params=pltpu.CompilerParams(dimension_semantics=("parallel", "arbitrary"), vmem_limit_bytes=VMEM_LIMIT),
    )(dff, w2, hpre, b1)


POOL_PAD = 16
POOL_ROWS = 256


def pool_lin(name, x, wvec, transpose, out_dtype):
    arr, width, cb = x
    s = arr.shape[0]
    n_steps = s // POOL_ROWS

    def body(x_ref, w_ref, o_ref, xp_ref):
        wv = w_ref[...]
        zeros = jnp.zeros((POOL_PAD, width), F32)
        xp_ref[0:POOL_PAD, :] = zeros
        xp_ref[s + POOL_PAD:s + 2 * POOL_PAD, :] = zeros

        def count(t0):
            t = lax.broadcasted_iota(jnp.int32, (POOL_ROWS, width), 0) + (t0 + 1)
            return jnp.minimum(t.astype(F32), wv)

        def fill(i, carry):
            t0 = pl.multiple_of(i * POOL_ROWS, POOL_ROWS)
            v = x_ref[pl.ds(t0, POOL_ROWS), :].astype(F32)
            if transpose:
                v = v / count(t0)
            xp_ref[pl.ds(t0 + POOL_PAD, POOL_ROWS), :] = v
            return carry

        lax.fori_loop(0, n_steps, fill, 0)

        def step(i, carry):
            t0 = pl.multiple_of(i * POOL_ROWS, POOL_ROWS)
            win = xp_ref[pl.ds(t0, POOL_ROWS + 2 * POOL_PAD), :]
            acc = jnp.zeros((POOL_ROWS, width), F32)
            for j in range(POOL_PAD):
                off = POOL_PAD + j if transpose else POOL_PAD - j
                acc = acc + jnp.where(wv > j, win[off:off + POOL_ROWS, :], 0.0)
            cur = x_ref[pl.ds(t0, POOL_ROWS), :].astype(F32)
            res = acc - cur if transpose else acc / count(t0) - cur
            o_ref[pl.ds(t0, POOL_ROWS), :] = res.astype(o_ref.dtype)
            return carry

        lax.fori_loop(0, n_steps, step, 0)

    return pl.pallas_call(
        body, name=name, grid=(1,),
        in_specs=[pl.BlockSpec((s, width), lambda i, cb=cb: (0, cb)), pl.BlockSpec((1, width), lambda i: (0, 0))],
        out_specs=pl.BlockSpec((s, width), lambda i: (0, 0)),
        out_shape=jax.ShapeDtypeStruct((s, width), out_dtype),
        scratch_shapes=[pltpu.VMEM((s + 2 * POOL_PAD, width), F32)],
        compiler_params=pltpu.CompilerParams(dimension_semantics=("arbitrary",), vmem_limit_bytes=VMEM_LIMIT),
    )(arr, wvec)


CONV_PAD = 32
CONV_ROWS = 128


def _glu(a, g):
    return a * jax.nn.sigmoid(g)


def conv_fwd(name, zc, w):
    arr, width, cb = zc
    s = arr.shape[0]
    n_steps = s // CONV_ROWS
    lead = CONV_PAD - (CONV_WIDTH - 1)

    def body(a_ref, g_ref, w_ref, o_ref, hp_ref):
        hp_ref[0:CONV_PAD, :] = jnp.zeros((CONV_PAD, D_CONV), F32)

        def fill(i, carry):
            t0 = pl.multiple_of(i * CONV_ROWS, CONV_ROWS)
            rows = pl.ds(t0, CONV_ROWS)
            hp_ref[pl.ds(t0 + CONV_PAD, CONV_ROWS), :] = _glu(a_ref[rows, :], g_ref[rows, :])
            return carry

        lax.fori_loop(0, n_steps, fill, 0)
        wv = w_ref[...]

        def step(i, carry):
            t0 = pl.multiple_of(i * CONV_ROWS, CONV_ROWS)
            win = hp_ref[pl.ds(t0, CONV_ROWS + CONV_PAD), :]
            acc = jnp.zeros((CONV_ROWS, D_CONV), F32)
            for k in range(CONV_WIDTH):
                acc = acc + wv[k:k + 1, :] * win[lead + k:lead + k + CONV_ROWS, :]
            o_ref[pl.ds(t0, CONV_ROWS), :] = acc
            return carry

        lax.fori_loop(0, n_steps, step, 0)

    return pl.pallas_call(
        body, name=name, grid=(1,),
        in_specs=[pl.BlockSpec((s, width), lambda i, cb=cb: (0, cb)),
                  pl.BlockSpec((s, width), lambda i, cb=cb: (0, cb + 1)), pl.BlockSpec(w.shape, lambda i: (0, 0))],
        out_specs=pl.BlockSpec((s, D_CONV), lambda i: (0, 0)),
        out_shape=jax.ShapeDtypeStruct((s, D_CONV), F32),
        scratch_shapes=[pltpu.VMEM((s + CONV_PAD, D_CONV), F32)],
        compiler_params=pltpu.CompilerParams(dimension_semantics=("arbitrary",), vmem_limit_bytes=VMEM_LIMIT),
    )(arr, arr, w)


def conv_bwd(name, zc, w, dout):
    arr, width, cb = zc
    s = arr.shape[0]
    n_steps = s // CONV_ROWS
    lead = CONV_PAD - (CONV_WIDTH - 1)

    def body(a_ref, g_ref, w_ref, d_ref, dz_ref, dw_ref, hp_ref, dp_ref):
        hp_ref[0:CONV_PAD, :] = jnp.zeros((CONV_PAD, D_CONV), F32)
        dp_ref[s:s + CONV_PAD, :] = jnp.zeros((CONV_PAD, D_CONV), F32)
        dw_ref[...] = jnp.zeros(dw_ref.shape, F32)

        def fill(i, carry):
            t0 = pl.multiple_of(i * CONV_ROWS, CONV_ROWS)
            rows = pl.ds(t0, CONV_ROWS)
            hp_ref[pl.ds(t0 + CONV_PAD, CONV_ROWS), :] = _glu(a_ref[rows, :], g_ref[rows, :])
            dp_ref[rows, :] = d_ref[rows, :]
            return carry

        lax.fori_loop(0, n_steps, fill, 0)
        wv = w_ref[...]

        def step(i, carry):
            t0 = pl.multiple_of(i * CONV_ROWS, CONV_ROWS)
            hwin = hp_ref[pl.ds(t0, CONV_ROWS + CONV_PAD), :]
            dwin = dp_ref[pl.ds(t0, CONV_ROWS + CONV_PAD), :]
            dcur = dwin[0:CONV_ROWS, :]
            dh = jnp.zeros((CONV_ROWS, D_CONV), F32)
            rows = []
            for k in range(CONV_WIDTH):
                rows.append(jnp.sum(dcur * hwin[lead + k:lead + k + CONV_ROWS, :], axis=0, keepdims=True))
                back = CONV_WIDTH - 1 - k
                dh = dh + wv[k:k + 1, :] * dwin[back:back + CONV_ROWS, :]
            rows.append(jnp.zeros((1, D_CONV), F32))
            dw_ref[...] += jnp.concatenate(rows, axis=0)
            rows_now = pl.ds(t0, CONV_ROWS)
            _, vjp = jax.vjp(_glu, a_ref[rows_now, :], g_ref[rows_now, :])
            da, dg = vjp(dh)
            dz_ref[pl.ds(t0, CONV_ROWS), :] = jnp.concatenate([da, dg], axis=1).astype(dz_ref.dtype)
            return carry

        lax.fori_loop(0, n_steps, step, 0)

    return pl.pallas_call(
        body, name=name, grid=(1,),
        in_specs=[pl.BlockSpec((s, width), lambda i, cb=cb: (0, cb)),
                  pl.BlockSpec((s, width), lambda i, cb=cb: (0, cb + 1)),
                  pl.BlockSpec(w.shape, lambda i: (0, 0)), pl.BlockSpec((s, D_CONV), lambda i: (0, 0))],
        out_specs=[pl.BlockSpec((s, 2 * width), lambda i: (0, 0)), pl.BlockSpec(w.shape, lambda i: (0, 0))],
        out_shape=[jax.ShapeDtypeStruct((s, 2 * width), BF16), jax.ShapeDtypeStruct(w.shape, F32)],
        scratch_shapes=[pltpu.VMEM((s + CONV_PAD, D_CONV), F32), pltpu.VMEM((s + CONV_PAD, D_CONV), F32)],
        compiler_params=pltpu.CompilerParams(dimension_semantics=("arbitrary",), vmem_limit_bytes=VMEM_LIMIT),
    )(arr, arr, w, dout)


HEADS_PER_STEP = LANE // HEAD_DIM
CHUNKS_PER_TILE = ATT_TILE // CHUNK
KEY_BLOCKS = N_PREV * CHUNK // ATT_TILE + 1
KEY_SPAN = KEY_BLOCKS * ATT_TILE


def _attn_tile(q, *rest, missing_cols):
    kcat = jnp.concatenate(rest[:KEY_BLOCKS], axis=0)
    vcat = jnp.concatenate(rest[KEY_BLOCKS:2 * KEY_BLOCKS], axis=0)
    bias = rest[2 * KEY_BLOCKS]
    lane = lax.broadcasted_iota(jnp.int32, (1, LANE), 1)
    col = lax.broadcasted_iota(jnp.int32, (1, KEY_SPAN), 1)
    missing = col < missing_cols
    qs = q * (HEAD_DIM ** -0.5)
    o = jnp.zeros((ATT_TILE, LANE), F32)
    for h in range(HEADS_PER_STEP):
        in_head = jnp.logical_and(lane >= h * HEAD_DIM, lane < (h + 1) * HEAD_DIM)
        sc = mm_nt(jnp.where(in_head, qs, 0.0), kcat) + bias[h]
        sc = jnp.where(missing, NEG_INF, sc)
        m = jnp.max(sc, axis=-1, keepdims=True)
        e = jnp.exp(sc - lax.stop_gradient(m))
        p = e / jnp.sum(e, axis=-1, keepdims=True)
        o = o + jnp.where(in_head, mm_nn(p, vcat), 0.0)
    return o


def _missing_cols(n):
    return jnp.maximum((KEY_BLOCKS - 1 - n) * ATT_TILE, 0)


def _attn_in_specs(nt):
    def spec(col0, back):
        return pl.BlockSpec((ATT_TILE, LANE),
                            lambda hp, n, col0=col0, back=back: (jnp.clip(n - back, 0, nt - 1), col0 // LANE + hp))
    backs = list(range(KEY_BLOCKS - 1, -1, -1))
    return ([spec(Z_Q, 0)] + [spec(Z_K, b) for b in backs] + [spec(Z_V, b) for b in backs]
            + [pl.BlockSpec((HEADS_PER_STEP, ATT_TILE, KEY_SPAN), lambda hp, n: (hp, 0, 0))])


def attn_fwd(name, z, bias):
    s = z.shape[0]
    nt = s // ATT_TILE
    n_in = 2 + 2 * KEY_BLOCKS

    def body(*refs):
        o_ref = refs[n_in]
        vals = [r[...] for r in refs[:n_in]]
        o = _attn_tile(*vals, missing_cols=_missing_cols(pl.program_id(1)))
        o_ref[...] = o.astype(o_ref.dtype)

    return pl.pallas_call(
        body, name=name, grid=(N_HEADS // HEADS_PER_STEP, nt),
        in_specs=_attn_in_specs(nt),
        out_specs=pl.BlockSpec((ATT_TILE, LANE), lambda hp, n: (n, hp)),
        out_shape=jax.ShapeDtypeStruct((s, D_ATTN), BF16),
        compiler_params=pltpu.CompilerParams(dimension_semantics=("parallel", "parallel"), vmem_limit_bytes=VMEM_LIMIT),
    )(*([z] * (n_in - 1)), bias)


def attn_bwd(name, z, bias, do, after=None):
    s = z.shape[0]
    nt = s // ATT_TILE
    n_in = 2 + 2 * KEY_BLOCKS
    nc = KEY_BLOCKS - 1

    deps = [] if after is None else [after]
    n_out = n_in + 1 + len(deps)

    def body(*refs):
        do_ref = refs[n_in]
        dq_ref, dk_ref, dv_ref, db_ref = refs[n_out:n_out + 4]
        kacc, vacc = refs[n_out + 4:n_out + 4 + nc], refs[n_out + 4 + nc:]
        n = pl.program_id(1)

        @pl.when(n == 0)
        def _():
            db_ref[...] = jnp.zeros(db_ref.shape, F32)
            for acc in (*kacc, *vacc):
                acc[...] = jnp.zeros(acc.shape, F32)

        def shift(out_ref, accs, contrib):
            @pl.when(n >= nc)
            def _():
                first = accs[0][...] if contrib is None else accs[0][...] + contrib[0]
                out_ref[...] = first.astype(out_ref.dtype)
            for j in range(nc - 1):
                accs[j][...] = accs[j + 1][...] if contrib is None else accs[j + 1][...] + contrib[j + 1]
            if contrib is not None:
                accs[nc - 1][...] = contrib[nc]

        @pl.when(n < nt)
        def _():
            fn = functools.partial(_attn_tile, missing_cols=_missing_cols(n))
            _, vjp = jax.vjp(fn, *[r[...] for r in refs[:n_in]])
            grads = vjp(do_ref[...].astype(F32))
            dq_ref[...] = grads[0].astype(dq_ref.dtype)
            db_ref[...] += grads[n_in - 1]
            shift(dk_ref, kacc, grads[1:1 + KEY_BLOCKS])
            shift(dv_ref, vacc, grads[1 + KEY_BLOCKS:1 + 2 * KEY_BLOCKS])

        @pl.when(n >= nt)
        def _():
            shift(dk_ref, kacc, None)
            shift(dv_ref, vacc, None)

    o_cur = pl.BlockSpec((ATT_TILE, LANE), lambda hp, n: (jnp.minimum(n, nt - 1), hp))
    o_old = pl.BlockSpec((ATT_TILE, LANE), lambda hp, n: (jnp.maximum(n - nc, 0), hp))
    b_spec = pl.BlockSpec((HEADS_PER_STEP, ATT_TILE, KEY_SPAN), lambda hp, n: (hp, 0, 0))
    return pl.pallas_call(
        body, name=name, grid=(N_HEADS // HEADS_PER_STEP, nt + nc),
        in_specs=_attn_in_specs(nt) + [o_cur] + [pl.BlockSpec(d.shape, lambda hp, n: (0, 0)) for d in deps],
        out_specs=[o_cur, o_old, o_old, b_spec],
        out_shape=[jax.ShapeDtypeStruct((s, D_ATTN), BF16)] * 3 + [jax.ShapeDtypeStruct((N_HEADS, ATT_TILE, KEY_SPAN), F32)],
        scratch_shapes=[pltpu.VMEM((ATT_TILE, LANE), F32)] * (2 * nc),
        compiler_params=pltpu.CompilerParams(dimension_semantics=("parallel", "arbitrary"), vmem_limit_bytes=VMEM_LIMIT),
    )(*([z] * (n_in - 1)), bias, do, *deps)


def loss_head(name, y, tgt, tm):
    s, d = y.shape

    def body(y_ref, t_ref, l_ref, dy_ref):
        i = pl.program_id(0)
        diff = y_ref[...] - t_ref[...]
        dy_ref[...] = diff * (1.0 / d)
        part = 0.5 * jnp.sum(jnp.mean(diff * diff, axis=-1, keepdims=True), axis=0, keepdims=True)

        @pl.when(i == 0)
        def _():
            l_ref[...] = jnp.zeros(l_ref.shape, F32)

        l_ref[...] += jnp.broadcast_to(part, l_ref.shape)

    row = pl.BlockSpec((tm, d), lambda i: (i, 0))
    return pl.pallas_call(
        body, name=name, grid=(s // tm,), in_specs=[row, row],
        out_specs=[pl.BlockSpec((8, LANE), lambda i: (0, 0)), row],
        out_shape=[jax.ShapeDtypeStruct((8, LANE), F32), jax.ShapeDtypeStruct((s, d), F32)],
        compiler_params=pltpu.CompilerParams(dimension_semantics=("arbitrary",), vmem_limit_bytes=VMEM_LIMIT),
    )(y, tgt)


def adamw(name, w, g, m, v, tr):
    r, c = w.shape
    assert r % tr == 0, (name, w.shape, tr)

    def body(w_ref, g_ref, m_ref, v_ref, d_ref, nm_ref, nv_ref):
        gg = g_ref[...]
        m2 = ADAM_B1 * m_ref[...] + (1.0 - ADAM_B1) * gg
        v2 = ADAM_B2 * v_ref[...] + (1.0 - ADAM_B2) * (gg * gg)
        m_hat = m2 / (1.0 - ADAM_B1 ** ADAM_STEP)
        v_hat = v2 / (1.0 - ADAM_B2 ** ADAM_STEP)
        d_ref[...] = -ADAM_LR * (m_hat / (jnp.sqrt(v_hat) + ADAM_EPS) + ADAM_WD * w_ref[...])
        nm_ref[...] = m2
        nv_ref[...] = v2

    blk = pl.BlockSpec((tr, c), lambda i: (i, 0))
    return pl.pallas_call(
        body, name=name, grid=(r // tr,), in_specs=[blk] * 4, out_specs=[blk] * 3,
        out_shape=[jax.ShapeDtypeStruct((r, c), F32)] * 3,
        compiler_params=pltpu.CompilerParams(dimension_semantics=("parallel",), vmem_limit_bytes=VMEM_LIMIT),
    )(w, g, m, v)


def adamw_sum(name, w, layer_blocks, m, v, tr):
    rows, c = w.shape
    nl = len(layer_blocks)
    nb, r, _ = layer_blocks[0].shape
    assert rows == nl * r and r % tr == 0, (name, w.shape, layer_blocks[0].shape, tr)
    per = r // tr

    def body(*refs):
        w_ref, b_refs, (m_ref, v_ref, g_ref, d_ref, nm_ref, nv_ref) = refs[0], refs[1:1 + nl], refs[1 + nl:]
        i = pl.program_id(0)

        def update(b_ref):
            gg = b_ref[0].astype(F32)
            for j in range(1, nb):
                gg = gg + b_ref[j].astype(F32)
            g_ref[...] = gg
            m2 = ADAM_B1 * m_ref[...] + (1.0 - ADAM_B1) * gg
            v2 = ADAM_B2 * v_ref[...] + (1.0 - ADAM_B2) * (gg * gg)
            m_hat = m2 / (1.0 - ADAM_B1 ** ADAM_STEP)
            v_hat = v2 / (1.0 - ADAM_B2 ** ADAM_STEP)
            d_ref[...] = -ADAM_LR * (m_hat / (jnp.sqrt(v_hat) + ADAM_EPS) + ADAM_WD * w_ref[...])
            nm_ref[...] = m2
            nv_ref[...] = v2

        for l in range(nl):
            pl.when(jnp.logical_and(i >= l * per, i < (l + 1) * per))(functools.partial(update, b_refs[l]))

    blk = pl.BlockSpec((tr, c), lambda i: (i, 0))
    b_specs = [pl.BlockSpec((nb, tr, c), lambda i, l=l: (0, jnp.clip(i - l * per, 0, per - 1), 0)) for l in range(nl)]
    return pl.pallas_call(
        body, name=name, grid=(rows // tr,),
        in_specs=[blk] + b_specs + [blk, blk], out_specs=[blk] * 4,
        out_shape=[jax.ShapeDtypeStruct((rows, c), F32)] * 4,
        compiler_params=pltpu.CompilerParams(dimension_semantics=("arbitrary",), vmem_limit_bytes=VMEM_LIMIT),
    )(w, *layer_blocks, m, v)


def sum_blocks(name, blocks, tr):
    nb, r, c = blocks.shape
    assert r % tr == 0, (name, blocks.shape, tr)

    def body(b_ref, o_ref):
        acc = b_ref[0].astype(F32)
        for j in range(1, nb):
            acc = acc + b_ref[j].astype(F32)
        o_ref[...] = acc

    return pl.pallas_call(
        body, name=name, grid=(r // tr,),
        in_specs=[pl.BlockSpec((nb, tr, c), lambda i: (0, i, 0))],
        out_specs=pl.BlockSpec((tr, c), lambda i: (i, 0)),
        out_shape=jax.ShapeDtypeStruct((r, c), F32),
        compiler_params=pltpu.CompilerParams(dimension_semantics=("parallel",), vmem_limit_bytes=VMEM_LIMIT),
    )(blocks)


FLIPS = [(0, 0, 1), (1, 0, 0), (0, 1, 0), (1, 1, 0), (1, 0, 1), (0, 1, 1), (1, 1, 1)]
ANY = pl.BlockSpec(memory_space=pl.ANY)


def _me():
    return lax.axis_index("x"), lax.axis_index("y"), lax.axis_index("c")


def _flip(pos, f):
    return tuple((1 - p) if fi else p for p, fi in zip(pos, f))


def _idx(pos):
    return 4 * pos[0] + 2 * pos[1] + pos[2]


def all_gather_multi(name, shards):
    n = len(shards)

    def body(*refs):
        x_refs, out_refs, token = refs[:n], refs[n:2 * n], refs[2 * n]
        send_sems, recv_sems, local_sems = refs[2 * n + 1:]
        token[...] = jnp.zeros_like(token)
        x, y, cc = _me()
        me, sibling = (x, y, cc), (x, y, 1 - cc)
        chips = [(1 - x, y), (x, 1 - y), (1 - x, 1 - y)]

        def copy(a, k, block, to, src=None):
            dst = out_refs[a].at[_idx(block)]
            return pltpu.make_async_remote_copy(
                src_ref=dst if src is None else src, dst_ref=dst, send_sem=send_sems.at[7 * a + k],
                recv_sem=recv_sems.at[7 * a + k], device_id=to, device_id_type=MESH)

        mine = [pltpu.make_async_copy(x_refs[a], out_refs[a].at[_idx(me)], local_sems.at[a]) for a in range(n)]
        for cp in mine:
            cp.start()
        first = []
        for a in range(n):
            first.append(copy(a, 0, me, sibling, src=x_refs[a]))
            first += [copy(a, 1 + j, me, (*chip, cc), src=x_refs[a]) for j, chip in enumerate(chips)]
        for cp in first:
            cp.start()
        passed = []
        for j, chip in enumerate(chips):
            for a in range(n):
                copy(a, 1 + j, (*chip, cc), me).wait_recv()
                fwd = copy(a, 4 + j, (*chip, cc), sibling)
                fwd.start()
                passed.append(fwd)
        for a in range(n):
            copy(a, 0, sibling, me).wait_recv()
            for j, chip in enumerate(chips):
                copy(a, 4 + j, (*chip, 1 - cc), me).wait_recv()
        for cp in first + passed:
            cp.wait_send()
        for cp in mine:
            cp.wait()

    return pl.pallas_call(
        body, name=name, in_specs=[ANY] * n, out_specs=[ANY] * n + [pl.BlockSpec(memory_space=pltpu.VMEM)],
        out_shape=[jax.ShapeDtypeStruct((N_DEV,) + a.shape, a.dtype) for a in shards]
        + [jax.ShapeDtypeStruct((8, LANE), F32)],
        scratch_shapes=[pltpu.SemaphoreType.DMA((7 * n,)), pltpu.SemaphoreType.DMA((7 * n,)),
                        pltpu.SemaphoreType.DMA((n,))],
    )(*shards)


def all_to_all_multi(name, blocks):
    n = len(blocks)

    def body(*refs):
        in_refs, out_refs = refs[:n], refs[n:2 * n]
        send_sems, recv_sems, local_sems = refs[2 * n:]
        me = _me()
        mi = _idx(me)
        mine = [pltpu.make_async_copy(in_refs[a].at[mi], out_refs[a].at[mi], local_sems.at[a]) for a in range(n)]
        for cp in mine:
            cp.start()
        sends, recvs = [], []
        for k, f in enumerate(FLIPS):
            peer = _flip(me, f)
            pi = _idx(peer)
            for a in range(n):
                sems = dict(send_sem=send_sems.at[7 * a + k], recv_sem=recv_sems.at[7 * a + k],
                            device_id=peer, device_id_type=MESH)
                sends.append(pltpu.make_async_remote_copy(src_ref=in_refs[a].at[pi], dst_ref=out_refs[a].at[mi], **sems))
                recvs.append(pltpu.make_async_remote_copy(src_ref=in_refs[a].at[mi], dst_ref=out_refs[a].at[pi], **sems))
        for cp in sends:
            cp.start()
        for cp in recvs:
            cp.wait_recv()
        for cp in sends:
            cp.wait_send()
        for cp in mine:
            cp.wait()

    return pl.pallas_call(
        body, name=name, in_specs=[ANY] * n, out_specs=[ANY] * n,
        out_shape=[jax.ShapeDtypeStruct(a.shape, a.dtype) for a in blocks],
        scratch_shapes=[pltpu.SemaphoreType.DMA((7 * n,)), pltpu.SemaphoreType.DMA((7 * n,)),
                        pltpu.SemaphoreType.DMA((n,))],
    )(*blocks)


HBM = pl.BlockSpec(memory_space=pltpu.HBM)
SEM = pl.BlockSpec(memory_space=pltpu.SEMAPHORE)
DATAFLOW = pltpu.SideEffectType.DATAFLOW_SIDE_EFFECTING


def _exchange_copies(a_refs, l_refs, send_sems, recv_sems, gather):
    me = _me()
    mi = _idx(me)
    out = []
    for k, f in enumerate(FLIPS):
        peer = _flip(me, f)
        for a in range(len(a_refs)):
            src = a_refs[a] if gather else a_refs[a].at[_idx(peer)]
            out.append(pltpu.make_async_remote_copy(
                src_ref=src, dst_ref=l_refs[a].at[mi], send_sem=send_sems.at[7 * a + k],
                recv_sem=recv_sems.at[7 * a + k], device_id=peer, device_id_type=MESH))
    return out


def exchange_start(name, arrays, gather):
    n = len(arrays)
    lands = [lax.empty(((N_DEV,) + a.shape) if gather else a.shape, a.dtype) for a in arrays]

    def body(*refs):
        a_refs, l_refs = refs[:n], refs[n:2 * n]
        send_sems, recv_sems = refs[2 * n], refs[2 * n + 1]
        token = refs[4 * n + 2]
        for cp in _exchange_copies(a_refs, l_refs, send_sems, recv_sems, gather):
            cp.start()
        token[...] = jnp.zeros_like(token)

    hbm = lambda a: pltpu.HBM(a.shape, a.dtype)
    res = pl.pallas_call(
        body, name=name,
        out_shape=(pltpu.SemaphoreType.DMA((7 * n,)), pltpu.SemaphoreType.DMA((7 * n,)),
                   *[hbm(a) for a in arrays], *[hbm(a) for a in lands], jax.ShapeDtypeStruct((8, LANE), F32)),
        in_specs=[HBM] * (2 * n),
        out_specs=(SEM, SEM, *[HBM] * (2 * n), pl.BlockSpec(memory_space=pltpu.VMEM)),
        input_output_aliases={i: i + 2 for i in range(2 * n)},
        compiler_params=pltpu.CompilerParams(has_side_effects=DATAFLOW),
    )(*[pltpu.with_memory_space_constraint(a, pltpu.HBM) for a in arrays],
      *[pltpu.with_memory_space_constraint(a, pltpu.HBM) for a in lands])
    return res[0], res[1], list(res[2:2 + n]), list(res[2 + n:2 + 2 * n]), res[-1]


def exchange_wait(name, send_sems, recv_sems, arrays, lands, after, gather):
    n = len(arrays)
    after = list(after) if isinstance(after, (list, tuple)) else [after]

    def body(*refs):
        a_refs, l_refs = refs[:n], refs[n:2 * n]
        ssem, rsem = refs[2 * n], refs[2 * n + 1]
        for cp in _exchange_copies(a_refs, l_refs, ssem, rsem, gather):
            cp.wait_send()
            cp.wait_recv()

    hbm = lambda a: pltpu.HBM(a.shape, a.dtype)
    res = pl.pallas_call(
        body, name=name,
        out_shape=(*[hbm(a) for a in arrays], *[hbm(a) for a in lands]),
        in_specs=[HBM] * (2 * n) + [SEM, SEM] + [pl.BlockSpec(memory_space=pl.ANY)] * len(after),
        out_specs=tuple([HBM] * (2 * n)),
        input_output_aliases={i: i for i in range(2 * n)},
        compiler_params=pltpu.CompilerParams(has_side_effects=DATAFLOW),
    )(*arrays, *lands, send_sems, recv_sems, *after)
    return list(res[n:])


def ada_fwd(name, c_row, w_cat, b_lay):
    d = c_row.shape[1]
    ncol = w_cat.shape[1]
    vmem = pl.BlockSpec(memory_space=pltpu.VMEM)

    def body(c_ref, w_ref, b_ref, mod_ref, cact_ref, token, call, send, land, s1, r1, s2, r2):
        token[...] = jnp.zeros_like(token)
        me = _me()
        mi = _idx(me)
        call[mi] = c_ref[...]

        def exchange(src_of, dst_buf, ssem, rsem):
            sends, recvs = [], []
            for k, f in enumerate(FLIPS):
                peer = _flip(me, f)
                sends.append(pltpu.make_async_remote_copy(
                    src_ref=src_of(peer), dst_ref=dst_buf.at[mi], send_sem=ssem.at[k], recv_sem=rsem.at[k],
                    device_id=peer, device_id_type=MESH))
                recvs.append(pltpu.make_async_remote_copy(
                    src_ref=src_of(peer), dst_ref=dst_buf.at[_idx(peer)], send_sem=ssem.at[k], recv_sem=rsem.at[k],
                    device_id=peer, device_id_type=MESH))
            for cp in sends:
                cp.start()
            for cp in recvs:
                cp.wait_recv()
            for cp in sends:
                cp.wait_send()

        exchange(lambda peer: c_ref, call, s1, r1)
        for p in range(N_DEV):
            cact_ref[pl.ds(p, 1), :] = jax.nn.silu(call[p])
        res = _dg(cact_ref[...], w_ref[...], 1, 0)
        for p in range(N_DEV):
            send[p] = res[p:p + 1, :]
        land[mi] = send[mi]
        exchange(lambda peer: send.at[_idx(peer)], land, s2, r2)
        mod_ref[...] = land[...] + b_ref[...]

    return pl.pallas_call(
        body, name=name, in_specs=[vmem, vmem, vmem], out_specs=[vmem, vmem, vmem],
        out_shape=[jax.ShapeDtypeStruct((N_DEV, 1, ncol), F32), jax.ShapeDtypeStruct((N_DEV, d), F32),
                   jax.ShapeDtypeStruct((8, LANE), F32)],
        scratch_shapes=[pltpu.VMEM((N_DEV, 1, d), F32), pltpu.VMEM((N_DEV, 1, ncol), F32),
                        pltpu.VMEM((N_DEV, 1, ncol), F32),
                        pltpu.SemaphoreType.DMA((7,)), pltpu.SemaphoreType.DMA((7,)),
                        pltpu.SemaphoreType.DMA((7,)), pltpu.SemaphoreType.DMA((7,))],
        compiler_params=pltpu.CompilerParams(vmem_limit_bytes=VMEM_LIMIT),
    )(c_row, w_cat, b_lay)


POOL_WINDOWS = (2, 4, 8, 16)
POOL_GROUP = 64
N_REL = 2 * REL_CLIP + 1
PACK_COLS = 1024
SMALL_NAMES = ["b_ada", "b_gate", "w_pool", "pool_scale", "rel_bias", "conv_w", "conv_b", "conv_ln_g",
               "conv_ln_b", "ln_mix_g", "ln_mix_b", "b_ff1", "b_ff2", "ln_ff_g", "ln_ff_b"]
BIG_NAMES = ["w_in", "w_br_pool", "w_br_attn", "w_br_conv", "w_o", "w_ff1", "w_ff2"]
ROW_SHARDED = ("w_in", "w_o", "w_ff2")
WEIGHT_NAMES = ["w_ada", "b_ada", "w_in", "b_gate", "w_pool", "pool_scale", "rel_bias", "conv_w", "conv_b",
                "conv_ln_g", "conv_ln_b", "w_br_pool", "w_br_attn", "w_br_conv", "w_o", "ln_mix_g", "ln_mix_b",
                "w_ff1", "b_ff1", "w_ff2", "b_ff2", "ln_ff_g", "ln_ff_b"]


def _bias_table(rel_bias):
    far = jnp.broadcast_to(rel_bias[:, 2 * REL_CLIP:], (N_HEADS, BAND - REL_CLIP))
    near = rel_bias[:, REL_CLIP - CHUNK + 1:2 * REL_CLIP][:, ::-1]
    ext = jnp.concatenate([far, near, jnp.zeros((N_HEADS, 1), F32)], axis=1)
    length = BAND + CHUNK
    flat = jnp.tile(ext, (1, CHUNK + 1))
    skew = flat[:, CHUNK - 1:CHUNK - 1 + CHUNK * (length - 1)].reshape(N_HEADS, CHUNK, length - 1)
    return skew[:, :, :BAND]


def _bias_full(rel_bias):
    tab = _bias_table(rel_bias)
    return jnp.concatenate(
        [jnp.pad(tab, ((0, 0), (0, 0), (i * CHUNK, KEY_SPAN - BAND - i * CHUNK)), constant_values=NEG_INF)
         for i in range(CHUNKS_PER_TILE)], axis=1)


def _block_diag(w_pool):
    out = jnp.zeros((D_POOL, D_POOL), F32)
    for g in range(len(POOL_WINDOWS)):
        out = lax.dynamic_update_slice(out, w_pool[g], (g * POOL_GROUP, g * POOL_GROUP))
    return out


def _flat_pad(arrs, mult):
    flat = jnp.concatenate([a.reshape(-1) for a in arrs])
    pad = (-flat.shape[0]) % mult
    return jnp.pad(flat, (0, pad)) if pad else flat


def _unflat(flat, shapes):
    out, off = [], 0
    for shp in shapes:
        n = int(np.prod(shp))
        out.append(flat[off:off + n].reshape(shp))
        off += n
    return out


def _to_blocks(name, full):
    k, n = full.shape
    if name in ROW_SHARDED:
        return full.reshape(N_DEV, k // N_DEV, n)
    return full.reshape(k, N_DEV, n // N_DEV).transpose(1, 0, 2)


def _from_blocks(name, blocks):
    nb, r, c = blocks.shape
    if name in ROW_SHARDED:
        return blocks.reshape(nb * r, c)
    return blocks.transpose(1, 0, 2).reshape(r, nb * c)


class _Layer:
    pass


def _row(v):
    return v.reshape(1, -1)


def _layer_fwd(x, modr, w, wvec, fetch_rest):
    sh_m, sc_m, g_m, sh_f, sc_f, g_f = modr
    (u,) = row_fwd("lnmod_mix", f_lnmod, [x], [sc_m, sh_m], [(D_MODEL, BF16)], 512)
    z = mm_big("mm_in", u, w.w_in_t, "nt", (1024, 896, 1024), F32, j_outer=True)
    p = pool_lin("pool_fwd", (z, D_POOL, Z_POOL // D_POOL), wvec, False, F32)
    ao = attn_fwd("attn_fwd", z, w.bias)
    cv = conv_fwd("conv_fwd", (z, D_CONV, Z_CONV // D_CONV), w.conv_w)
    fetch_rest(w, cv)
    mparams = [w.wbd, w.ps, w.wbp, w.wba, w.wbc, w.cb, w.clg, w.clb, w.bg, w.wo, g_m, w.lmg, w.lmb]
    gate_blocks = [(z, GATE_BLOCK, Z_GATE // GATE_BLOCK + k) for k in range(3 * D_MODEL // GATE_BLOCK)]
    (x1,) = row_fwd("merge", f_merge, [p, ao, cv, *gate_blocks, x], mparams, [(D_MODEL, F32)], 256)
    (u2,) = row_fwd("lnmod_ff", f_lnmod, [x1], [sc_f, sh_f], [(D_MODEL, BF16)], 512)
    hpre, h = mm_ff1_relu2("mm_ff1", u2, w.w_ff1, w.b1, 1024, 1024)
    ff = mm_big("mm_ff2", h, w.w_ff2, "nn", (512, 1024, 4096), F32)
    (x2,) = row_fwd("ffout", f_ffout, [x1, ff], [w.b2, g_f, w.lfg, w.lfb], [(D_MODEL, F32)], 512)
    return x2, (x, u, z, p, ao, cv, x1, u2, hpre, h, ff, mparams)


GRAD_GROUPS = [("ff", ["w_ff2", "w_ff1"]), ("mix", ["w_o", "w_br_pool", "w_br_attn", "w_br_conv"]), ("in", ["w_in"])]


def _layer_bwd(dx2, saved, modr, w, wvec, ready):
    x, u, z, p, ao, cv, x1, u2, hpre, h, ff, mparams = saved
    sh_m, sc_m, g_m, sh_f, sc_f, g_f = modr
    g = {}
    (dx1a, dff), (g["b_ff2"], dgf, g["ln_ff_g"], g["ln_ff_b"]) = row_bwd(
        "ffout_bwd", f_ffout, [x1, ff], [w.b2, g_f, w.lfg, w.lfb], [dx2], 512, [(0, F32), (1, BF16)], [0, 1, 2, 3])
    dhpre, g["b_ff1"] = mm_dh_relu2("mm_dh", dff, w.w_ff2, hpre, w.b1, 1024, 1024)
    g["w_ff2"] = mm_big("mm_dw_ff2", h, dff, "tn", (1024, 1024, 2048), BF16)
    du2 = mm_big("mm_du2", dhpre, w.w_ff1, "nt", (512, 1024, 4096), F32)
    g["w_ff1"] = mm_big("mm_dw_ff1", u2, dhpre, "tn", (1024, 1024, 2048), BF16)
    sc_f = sc_f + ready("ff", g)[0:1, 0:1]
    (dx1,), (dscf, dshf) = row_bwd("lnmod_ff_bwd", f_lnmod, [x1], [sc_f, sh_f], [du2], 512, [(0, F32)], [0, 1],
                                   add_to=dx1a)
    gate_blocks = [(z, GATE_BLOCK, Z_GATE // GATE_BLOCK + k) for k in range(3 * D_MODEL // GATE_BLOCK)]
    (dp, dao, dcv, *dzg, dxa), dm = row_bwd(
        "merge_bwd", f_merge, [p, ao, cv, *gate_blocks, x], mparams, [dx1], 256,
        [(0, F32), (1, BF16), (2, F32), (3, BF16), (4, BF16), (5, BF16), (6, BF16), (7, F32)], list(range(13)))
    (dwbd, g["pool_scale"], g["w_br_pool"], g["w_br_attn"], g["w_br_conv"], g["conv_b"], g["conv_ln_g"],
     g["conv_ln_b"], g["b_gate"], g["w_o"], dgm, g["ln_mix_g"], g["ln_mix_b"]) = dm
    g["w_pool"] = jnp.stack([dwbd[i * POOL_GROUP:(i + 1) * POOL_GROUP, i * POOL_GROUP:(i + 1) * POOL_GROUP]
                             for i in range(len(POOL_WINDOWS))])
    tok = ready("mix", g)
    dzp = pool_lin("pool_bwd", (dp, D_POOL, 0), wvec + tok[0:1, 0:1], True, BF16)
    dq, dk, dv, dbias = attn_bwd("attn_bwd", z, w.bias, dao, after=tok)
    (g["rel_bias"],) = w.bias_vjp(dbias)
    dzc, dcw = conv_bwd("conv_bwd", (z, D_CONV, Z_CONV // D_CONV), w.conv_w + tok[0:1, 0:1], dcv)
    g["conv_w"] = dcw[:CONV_WIDTH]
    dz = jnp.concatenate([dzp, dq, dk, dv, dzc, *dzg], axis=1)
    du = mm_big("mm_du", dz, w.w_in_t, "nn", (512, 1024, D_IN), F32)
    (dx,), (dscm, dshm) = row_bwd("lnmod_mix_bwd", f_lnmod, [x], [sc_m, sh_m], [du], 512, [(0, F32)], [0, 1],
                                  add_to=dxa)
    g["dmod"] = jnp.concatenate([dshm, dscm, dgm, dshf, dscf, dgf], axis=1)
    tok = ready("small", g)
    g["w_in"] = mm_big("mm_dw_in", dz, u, "tn", (896, 1024, 2048), BF16, after=tok)
    return dx, g, ready("in", g)


def kernel(x, c, w_ada, b_ada, w_in, b_gate, w_pool, pool_scale, rel_bias, conv_w, conv_b, conv_ln_g, conv_ln_b, w_br_pool, w_br_attn, w_br_conv, w_o, ln_mix_g, ln_mix_b, w_ff1, b_ff1, w_ff2, b_ff2, ln_ff_g, ln_ff_b, loss_target, m_w_ada, m_b_ada, m_w_in, m_b_gate, m_w_pool, m_pool_scale, m_rel_bias, m_conv_w, m_conv_b, m_conv_ln_g, m_conv_ln_b, m_w_br_pool, m_w_br_attn, m_w_br_conv, m_w_o, m_ln_mix_g, m_ln_mix_b, m_w_ff1, m_b_ff1, m_w_ff2, m_b_ff2, m_ln_ff_g, m_ln_ff_b, v_w_ada, v_b_ada, v_w_in, v_b_gate, v_w_pool, v_pool_scale, v_rel_bias, v_conv_w, v_conv_b, v_conv_ln_g, v_conv_ln_b, v_w_br_pool, v_w_br_attn, v_w_br_conv, v_w_o, v_ln_mix_g, v_ln_mix_b, v_w_ff1, v_b_ff1, v_w_ff2, v_b_ff2, v_ln_ff_g, v_ln_ff_b):
    args = dict(locals())
    wts = {n: args[n] for n in WEIGHT_NAMES}
    mom = {n: args["m_" + n] for n in WEIGHT_NAMES}
    var = {n: args["v_" + n] for n in WEIGHT_NAMES}
    me = 4 * lax.axis_index("x") + 2 * lax.axis_index("y") + lax.axis_index("c")
    xs, tgt = x[0], loss_target[0]
    nc_ada = w_ada.shape[2]
    wvec = jnp.asarray(np.repeat(np.array(POOL_WINDOWS, np.float32), POOL_GROUP)[None, :])

    w_cat = jnp.concatenate([w_ada[l] for l in range(DEPTH)], axis=1)
    b_lay = b_ada.reshape(DEPTH, N_DEV, nc_ada).transpose(1, 0, 2).reshape(N_DEV, 1, DEPTH * nc_ada)
    land, cact, ada_token = ada_fwd("ada_fwd", c, w_cat, b_lay)
    mod = land.reshape(N_DEV, DEPTH, nc_ada).transpose(1, 0, 2).reshape(DEPTH, 6 * D_MODEL)
    modr = [[mod[l:l + 1, i * D_MODEL:(i + 1) * D_MODEL] for i in range(6)] for l in range(DEPTH)]

    cw_pack = _flat_pad([conv_w], 8 * LANE).reshape(-1, LANE) + ada_token[0:1, 0:1]
    xlayout = lambda n, a: jnp.swapaxes(a, -1, -2) if n == "w_in" else a
    shards = [[xlayout(n, wts[n][l]).astype(BF16) for n in BIG_NAMES] for l in range(DEPTH)]
    n_first = [BIG_NAMES.index("w_ff1"), 1]
    *first0, cw_all, token = all_gather_multi("gather_first_l0", shards[0][:n_first[0]] + [cw_pack])
    cw_all = cw_all.reshape(N_DEV, -1)[:, :conv_w.size]
    conv_full = cw_all.reshape((N_DEV,) + conv_w.shape).transpose(1, 2, 0, 3).reshape(DEPTH, CONV_WIDTH, D_CONV)
    gathers = {}
    for key, arrs in [("rest_l0", shards[0][n_first[0]:]), ("first_l1", shards[1][:n_first[1]]),
                      ("rest_l1", shards[1][n_first[1]:])]:
        arrs = [a + token[0:1, 0:1].astype(a.dtype) for a in arrs]
        ssem, rsem, thru, lands, token = exchange_start(f"gather_{key}_start", arrs, True)
        gathers[key] = (arrs, ssem, rsem, thru, lands)
    modr[0] = [r + token[0:1, 0:1] for r in modr[0]]

    def with_own(lands_, own):
        return [lax.dynamic_update_index_in_dim(ld, o, me, axis=0) for ld, o in zip(lands_, own)]

    def gathered(key, after):
        arrs, ssem, rsem, thru, lands = gathers[key]
        return with_own(exchange_wait(f"gather_{key}_wait", ssem, rsem, thru, lands, after, True), arrs)

    attr = dict(w_in="w_in_t", w_br_pool="wbp", w_br_attn="wba", w_br_conv="wbc", w_o="wo", w_ff1="w_ff1", w_ff2="w_ff2")

    def assign(w, names, blocks):
        for n, g in zip(names, blocks):
            setattr(w, attr[n], _from_blocks(n, g))

    def fetch_rest_for(l):
        return lambda w, after: assign(w, BIG_NAMES[n_first[l]:], gathered(f"rest_l{l}", after))

    def layer_weights(l, first_blocks):
        w = _Layer()
        assign(w, BIG_NAMES[:n_first[l]], first_blocks)
        w.wbd = _block_diag(w_pool[l])
        w.ps, w.cb, w.clg, w.clb = _row(pool_scale[l]), _row(conv_b[l]), _row(conv_ln_g[l]), _row(conv_ln_b[l])
        w.bg, w.lmg, w.lmb = _row(b_gate[l]), _row(ln_mix_g[l]), _row(ln_mix_b[l])
        w.b1, w.b2, w.lfg, w.lfb = _row(b_ff1[l]), _row(b_ff2[l]), _row(ln_ff_g[l]), _row(ln_ff_b[l])
        w.conv_w = jnp.pad(conv_full[l], ((0, CONV_PAD - CONV_WIDTH), (0, 0)))
        w.bias, w.bias_vjp = jax.vjp(_bias_full, rel_bias[l])
        return w

    layers, saved = [layer_weights(0, first0)], []
    h, sv = _layer_fwd(xs, modr[0], layers[0], wvec, fetch_rest_for(0))
    saved.append(sv)
    layers.append(layer_weights(1, gathered("first_l1", h)))
    h, sv = _layer_fwd(h, modr[1], layers[1], wvec, fetch_rest_for(1))
    saved.append(sv)
    lpart, dy = loss_head("loss_head", h, tgt, 512)
    loss = lax.psum(lpart[0, 0], ("x", "y", "c"))
    grads, dmods = [None] * DEPTH, [None] * DEPTH
    pending = {}
    small_shapes = [wts[n].shape if n != "conv_w" else (DEPTH, CONV_WIDTH, D_CONV) for n in SMALL_NAMES]

    def ready_for(l):
        def ready(group, g):
            if group == "small":
                if l > 0:
                    return None
                both = [g, grads[1]]
                local = [jnp.concatenate([both[k]["dmod"] for k in range(DEPTH)], axis=0)]
                local += [jnp.stack([both[k][n].reshape(shp[1:]) for k in range(DEPTH)])
                          for n, shp in zip(SMALL_NAMES[1:], small_shapes[1:])]
                pack = _flat_pad(local, 8 * LANE).reshape(-1, LANE)
                ssem, rsem, thru, lands, token = exchange_start("gather_small_grads_start", [pack], True)
                pending["small"] = (pack, ssem, rsem, thru, lands)
                return token
            names = dict(GRAD_GROUPS)[group]
            blocks = [_to_blocks(n, g[n]).astype(BF16) for n in names]
            ssem, rsem, thru, lands, token = exchange_start(f"scatter_l{l}_{group}_start", blocks, False)
            pending[(l, group)] = (names, blocks, ssem, rsem, thru, lands)
            return token
        return ready

    def received(l, group, after):
        names, blocks, ssem, rsem, thru, lands = pending[(l, group)]
        lands = exchange_wait(f"scatter_l{l}_{group}_wait", ssem, rsem, thru, lands, after, False)
        own = [lax.dynamic_index_in_dim(b, me, axis=0, keepdims=False) for b in blocks]
        return dict(zip(names, with_own(lands, own)))

    dy, grads[1], token = _layer_bwd(dy, saved[1], modr[1], layers[1], wvec, ready_for(1))
    modr0 = [r + token[0:1, 0:1] for r in modr[0]]
    dy, grads[0], token = _layer_bwd(dy, saved[0], modr0, layers[0], wvec, ready_for(0))
    recv = [{} for _ in range(DEPTH)]
    for l, group in [(1, "ff"), (1, "mix"), (1, "in"), (0, "ff"), (0, "mix")]:
        recv[l].update(received(l, group, token))
    grad_x = dy[None]
    dmod_size = DEPTH * 6 * D_MODEL

    small_pack, ssem, rsem, thru, lands = pending["small"]
    (small_all,) = with_own(exchange_wait("gather_small_grads_wait", ssem, rsem, thru, lands, token, True), [small_pack])
    small_sum = sum_blocks("sum_small_grads", small_all, small_pack.shape[0]).reshape(-1)
    gsmall = dict(zip(SMALL_NAMES, _unflat(small_sum, small_shapes)))
    gw = dict(gsmall)
    gw["conv_w"] = lax.dynamic_slice_in_dim(gsmall["conv_w"], me * conv_w.shape[2], conv_w.shape[2], axis=2)

    dmod_all = small_all.reshape(N_DEV, -1)[:, :dmod_size].reshape(N_DEV, DEPTH, N_DEV, nc_ada)
    dm_mine = lax.dynamic_index_in_dim(dmod_all, me, axis=2, keepdims=False).reshape(N_DEV, DEPTH * nc_ada)
    cact_t = jnp.pad(cact.T, ((0, 0), (0, LANE - N_DEV)))
    dm_pad = jnp.pad(dm_mine, ((0, LANE - N_DEV), (0, 0)))
    dw_cat = mm_big("mm_dw_ada", cact_t, dm_pad, "nn", (D_MODEL, DEPTH * nc_ada, LANE), F32)
    gw["w_ada"] = jnp.stack([dw_cat[:, l * nc_ada:(l + 1) * nc_ada] for l in range(DEPTH)])

    delta, new_m, new_v = {}, {}, {}
    packs = [_flat_pad([src[n] for n in SMALL_NAMES], 8 * LANE).reshape(-1, LANE) for src in (wts, gw, mom, var)]
    small_res = adamw("adamw_small", *packs, packs[0].shape[0])
    shapes = [wts[n].shape for n in SMALL_NAMES]
    for out, flat in zip((delta, new_m, new_v), small_res):
        out.update(dict(zip(SMALL_NAMES, _unflat(flat.reshape(-1), shapes))))
    for n in ["w_ada", "w_ff2", "w_ff1", "w_o", "w_br_pool", "w_br_attn", "w_br_conv", "w_in"]:
        shp = xlayout(n, wts[n]).shape
        two_d = lambda a, shp=shp, n=n: xlayout(n, a).reshape(shp[0] * shp[1], shp[2])
        tr = 224 if n == "w_in" else min(256, shp[1])
        if n == "w_ada":
            res = (gw[n],) + tuple(adamw("adamw_" + n, two_d(wts[n]), two_d(gw[n]), two_d(mom[n]), two_d(var[n]), tr))
        else:
            if n == "w_in":
                done = [small_res[2]] + [new_v[k] for k in ["w_ada"] + BIG_NAMES[1:]]
                recv[0].update(received(0, "in", done))
            res = adamw_sum("adamw_" + n, two_d(wts[n]), [recv[l][n] for l in range(DEPTH)], two_d(mom[n]),
                            two_d(var[n]), tr)
        gw[n], delta[n], new_m[n], new_v[n] = [xlayout(n, a.reshape(shp)) for a in res]

    return (loss, grad_x, *[gw[n] for n in WEIGHT_NAMES], *[delta[n] for n in WEIGHT_NAMES],
            *[new_m[n] for n in WEIGHT_NAMES], *[new_v[n] for n in WEIGHT_NAMES])
```

```python
import functools

import jax
import jax.numpy as jnp
import numpy as np
from jax import lax
from jax.experimental import pallas as pl
from jax.experimental.pallas import tpu as pltpu

F32 = jnp.float32
BF16 = jnp.bfloat16
MESH = pl.DeviceIdType.MESH

D_MODEL = 1024
DEPTH = 2
CHUNK = 64
N_HEADS = 8
HEAD_DIM = 64
D_POOL = 256
D_ATTN = 512
D_CONV = 256
CONV_WIDTH = 31
D_FF = 4096
D_IN = 5376
N_PREV = 8
BAND = (N_PREV + 1) * CHUNK
REL_CLIP = 128
ALPHA = (2.0 * DEPTH) ** 0.25
LN_EPS = 1e-5
NEG_INF = -1e30
N_DEV = 8

ADAM_LR, ADAM_B1, ADAM_B2, ADAM_EPS, ADAM_WD, ADAM_STEP = 0.001, 0.9, 0.999, 1e-08, 0.01, 10

VMEM_LIMIT = 56 * 1024 * 1024

Z_POOL, Z_Q, Z_K, Z_V, Z_CONV, Z_GATE = 0, 256, 768, 1280, 1792, 2304
GATE_BLOCK = 768
ATT_TILE = 512
LANE = 128


def _dg(a, b, ca, cb):
    return lax.dot_general(a.astype(BF16), b.astype(BF16), (((ca,), (cb,)), ((), ())),
                           preferred_element_type=F32)


@jax.custom_vjp
def mm_nn(a, b):
    return _dg(a, b, 1, 0)


def _mm_nn_fwd(a, b):
    return _dg(a, b, 1, 0), (a, b)


def _mm_nn_bwd(res, g):
    a, b = res
    return _dg(g, b, 1, 1).astype(a.dtype), _dg(a, g, 0, 0).astype(b.dtype)


mm_nn.defvjp(_mm_nn_fwd, _mm_nn_bwd)


@jax.custom_vjp
def mm_nt(a, b):
    return _dg(a, b, 1, 1)


def _mm_nt_fwd(a, b):
    return _dg(a, b, 1, 1), (a, b)


def _mm_nt_bwd(res, g):
    a, b = res
    return _dg(g, b, 1, 0).astype(a.dtype), _dg(g, a, 0, 0).astype(b.dtype)


mm_nt.defvjp(_mm_nt_fwd, _mm_nt_bwd)


@jax.custom_vjp
def mm_nn_shadow(a, w, shadow):
    return _dg(a, w, 1, 0)


def _mm_nn_shadow_fwd(a, w, shadow):
    return _dg(a, w, 1, 0), (a, w)


def _mm_nn_shadow_bwd(res, g):
    a, w = res
    return _dg(g, w, 1, 1).astype(a.dtype), jnp.zeros_like(w), _dg(a, g, 0, 0)


mm_nn_shadow.defvjp(_mm_nn_shadow_fwd, _mm_nn_shadow_bwd)


def mm_w(a, w):
    return mm_nn_shadow(a, w[0], w[1]) if isinstance(w, tuple) else mm_nn(a, w)


def _ln(x):
    mu = jnp.mean(x, axis=-1, keepdims=True)
    xc = x - mu
    var = jnp.mean(xc * xc, axis=-1, keepdims=True)
    return xc * lax.rsqrt(var + LN_EPS)


def _norm_rows(rows):
    return [r if isinstance(r, tuple) else (r, r.shape[1], 0) for r in rows]


def _row_spec(tm, r):
    _, width, cb = r
    return pl.BlockSpec((tm, width), lambda i, cb=cb: (i, cb))


def _full_spec(a):
    nd = a.ndim
    return pl.BlockSpec(a.shape, lambda i, nd=nd: (0,) * nd)


def row_fwd(name, f, rows, params, outs, tm):
    rows = _norm_rows(rows)
    s = rows[0][0].shape[0]
    nr, npar = len(rows), len(params)

    def body(*refs):
        r = [x[...].astype(F32) for x in refs[:nr]]
        p = [x[...] for x in refs[nr:nr + npar]]
        res = f(*r, *p)
        for o_ref, o in zip(refs[nr + npar:], res):
            o_ref[...] = o.astype(o_ref.dtype)

    return pl.pallas_call(
        body, name=name, grid=(s // tm,),
        in_specs=[_row_spec(tm, r) for r in rows] + [_full_spec(p) for p in params],
        out_specs=[pl.BlockSpec((tm, w), lambda i: (i, 0)) for w, _ in outs],
        out_shape=[jax.ShapeDtypeStruct((s, w), dt) for w, dt in outs],
        compiler_params=pltpu.CompilerParams(dimension_semantics=("parallel",), vmem_limit_bytes=VMEM_LIMIT),
    )(*[r[0] for r in rows], *params)


def row_bwd(name, f, rows, params, douts, tm, want_rows, want_params, add_to=None):
    rows = _norm_rows(rows)
    s = rows[0][0].shape[0]
    nr, npar, nd = len(rows), len(params), len(douts)
    nadd = 0 if add_to is None else 1
    n_in = nr + npar + nd + nadd

    def body(*refs):
        i = pl.program_id(0)
        r = [x[...].astype(F32) for x in refs[:nr]]
        p = [(x[...], jnp.zeros(x.shape, F32)) if x.dtype == BF16 else x[...] for x in refs[nr:nr + npar]]
        d = [x[...].astype(F32) for x in refs[nr + npar:nr + npar + nd]]
        _, vjp = jax.vjp(f, *r, *p)
        g = vjp(tuple(d))
        out_refs = refs[n_in:]
        for k, (idx, _) in enumerate(want_rows):
            val = g[idx]
            if nadd and k == 0:
                val = val + refs[n_in - 1][...].astype(F32)
            out_refs[k][...] = val.astype(out_refs[k].dtype)
        for k, idx in enumerate(want_params):
            gp = g[nr + idx]
            gp = gp[1] if isinstance(gp, tuple) else gp
            o_ref = out_refs[len(want_rows) + k]

            @pl.when(i == 0)
            def _():
                o_ref[...] = gp

            @pl.when(i > 0)
            def _():
                o_ref[...] += gp

    in_specs = ([_row_spec(tm, r) for r in rows] + [_full_spec(p) for p in params]
                + [pl.BlockSpec((tm, d.shape[1]), lambda i: (i, 0)) for d in douts])
    args = [r[0] for r in rows] + list(params) + list(douts)
    if nadd:
        in_specs.append(pl.BlockSpec((tm, add_to.shape[1]), lambda i: (i, 0)))
        args.append(add_to)
    out_specs = ([pl.BlockSpec((tm, rows[idx][1]), lambda i: (i, 0)) for idx, _ in want_rows]
                 + [_full_spec(params[idx]) for idx in want_params])
    out_shape = ([jax.ShapeDtypeStruct((s, rows[idx][1]), dt) for idx, dt in want_rows]
                 + [jax.ShapeDtypeStruct(params[idx].shape, F32) for idx in want_params])
    res = pl.pallas_call(
        body, name=name, grid=(s // tm,), in_specs=in_specs, out_specs=out_specs, out_shape=out_shape,
        compiler_params=pltpu.CompilerParams(dimension_semantics=("arbitrary",), vmem_limit_bytes=VMEM_LIMIT),
    )(*args)
    return res[:len(want_rows)], res[len(want_rows):]


def f_lnmod(x, sc, sh):
    return (_ln(x) * (1.0 + sc) + sh,)


def f_merge(p, ao, cv, zg0, zg1, zg2, zg3, x, wbd, ps, wbp, wba, wbc, cb, clg, clb, bg, wo, gm, lg, lb):
    zg = jnp.concatenate([zg0, zg1, zg2, zg3], axis=1)
    pm = mm_w(p, wbd) * ps
    co = jax.nn.silu(_ln(cv + cb) * clg + clb)
    y_pool = mm_w(pm, wbp)
    y_attn = mm_w(ao, wba)
    y_conv = mm_w(co, wbc)
    gates = jax.nn.sigmoid(zg + bg)
    merged = (gates[:, :D_MODEL] * y_pool + gates[:, D_MODEL:2 * D_MODEL] * y_attn
              + gates[:, 2 * D_MODEL:] * y_conv)
    mix = mm_w(merged, wo)
    return (_ln(ALPHA * x + gm * mix) * lg + lb,)


def f_relu2(hpre, b1):
    a = jax.nn.relu(hpre + b1)
    return (a * a,)


def f_ffout(x1, ff, b2, gf, lg, lb):
    return (_ln(ALPHA * x1 + gf * (ff + b2)) * lg + lb,)


def mm_big(name, a, b, kind, tiles, out_dtype, j_outer=False, after=None):
    if kind == "nn":
        o0, o1, red = a.shape[0], b.shape[1], a.shape[1]
    elif kind == "nt":
        o0, o1, red = a.shape[0], b.shape[0], a.shape[1]
    else:
        o0, o1, red = a.shape[1], b.shape[1], a.shape[0]
    t0, t1, tr = min(tiles[0], o0), min(tiles[1], o1), min(tiles[2], red)
    if kind == "nn":
        a_spec = pl.BlockSpec((t0, tr), lambda i, j, r: (i, r))
        b_spec = pl.BlockSpec((tr, t1), lambda i, j, r: (r, j))
        dims = (1, 0)
    elif kind == "nt":
        a_spec = pl.BlockSpec((t0, tr), lambda i, j, r: (i, r))
        b_spec = pl.BlockSpec((t1, tr), lambda i, j, r: (j, r))
        dims = (1, 1)
    else:
        a_spec = pl.BlockSpec((tr, t0), lambda i, j, r: (r, i))
        b_spec = pl.BlockSpec((tr, t1), lambda i, j, r: (r, j))
        dims = (0, 0)
    assert o0 % t0 == 0 and o1 % t1 == 0 and red % tr == 0, (name, a.shape, b.shape, tiles)
    n0, n1, nred = o0 // t0, o1 // t1, red // tr
    o_spec = pl.BlockSpec((t0, t1), lambda i, j, r: (i, j))
    if j_outer:
        swap = lambda spec: pl.BlockSpec(spec.block_shape, lambda j, i, r, f=spec.index_map: f(i, j, r))
        a_spec, b_spec, o_spec = swap(a_spec), swap(b_spec), swap(o_spec)
        grid = (n1, n0, nred)
    else:
        grid = (n0, n1, nred)

    deps = [] if after is None else [after]
    dep_specs = [pl.BlockSpec(d.shape, lambda i, j, r, nd=d.ndim: (0,) * nd) for d in deps]
    if nred == 1:
        def body(a_ref, b_ref, *rest):
            o_ref = rest[len(deps)]
            o_ref[...] = _dg(a_ref[...], b_ref[...], *dims).astype(o_ref.dtype)
        scratch = []
    else:
        def body(a_ref, b_ref, *rest):
            o_ref, acc_ref = rest[len(deps):]
            r = pl.program_id(2)
            part = _dg(a_ref[...], b_ref[...], *dims)

            @pl.when(r == 0)
            def _():
                acc_ref[...] = part

            @pl.when(jnp.logical_and(r > 0, r < nred - 1))
            def _():
                acc_ref[...] += part

            @pl.when(r == nred - 1)
            def _():
                o_ref[...] = (acc_ref[...] + part).astype(o_ref.dtype)
        scratch = [pltpu.VMEM((t0, t1), F32)]

    return pl.pallas_call(
        body, name=name, grid=grid,
        in_specs=[a_spec, b_spec] + dep_specs,
        out_specs=o_spec,
        out_shape=jax.ShapeDtypeStruct((o0, o1), out_dtype),
        scratch_shapes=scratch,
        compiler_params=pltpu.CompilerParams(dimension_semantics=("parallel", "parallel", "arbitrary"),
                                             vmem_limit_bytes=VMEM_LIMIT),
    )(a, b, *deps)


def mm_ff1_relu2(name, u2, w1, b1, tm, tn):
    m, k = u2.shape
    n = w1.shape[1]
    tm, tn = min(tm, m), min(tn, n)

    def body(a_ref, b_ref, bias_ref, hpre_ref, h_ref):
        acc = _dg(a_ref[...], b_ref[...], 1, 0)
        hpre_ref[...] = acc
        h_ref[...] = f_relu2(acc, bias_ref[...])[0].astype(h_ref.dtype)

    out = pl.BlockSpec((tm, tn), lambda j, i: (i, j))
    return pl.pallas_call(
        body, name=name, grid=(n // tn, m // tm),
        in_specs=[pl.BlockSpec((tm, k), lambda j, i: (i, 0)), pl.BlockSpec((k, tn), lambda j, i: (0, j)),
                  pl.BlockSpec((1, tn), lambda j, i: (0, j))],
        out_specs=[out, out],
        out_shape=[jax.ShapeDtypeStruct((m, n), F32), jax.ShapeDtypeStruct((m, n), BF16)],
        compiler_params=pltpu.CompilerParams(dimension_semantics=("parallel", "parallel"), vmem_limit_bytes=VMEM_LIMIT),
    )(u2, w1, b1)


def mm_dh_relu2(name, dff, w2, hpre, b1, tm, tn):
    m, k = dff.shape
    n = w2.shape[0]
    tm, tn = min(tm, m), min(tn, n)

    def body(a_ref, b_ref, hpre_ref, bias_ref, d_ref, db_ref):
        i = pl.program_id(1)
        dh = _dg(a_ref[...], b_ref[...], 1, 1)
        _, vjp = jax.vjp(f_relu2, hpre_ref[...], bias_ref[...])
        dhpre, db = vjp((dh,))
        d_ref[...] = dhpre.astype(d_ref.dtype)

        @pl.when(i == 0)
        def _():
            db_ref[...] = db

        @pl.when(i > 0)
        def _():
            db_ref[...] += db

    tile = pl.BlockSpec((tm, tn), lambda j, i: (i, j))
    col = pl.BlockSpec((1, tn), lambda j, i: (0, j))
    return pl.pallas_call(
        body, name=name, grid=(n // tn, m // tm),
        in_specs=[pl.BlockSpec((tm, k), lambda j, i: (i, 0)), pl.BlockSpec((tn, k), lambda j, i: (j, 0)), tile, col],
        out_specs=[tile, col],
        out_shape=[jax.ShapeDtypeStruct((m, n), BF16), jax.ShapeDtypeStruct((1, n), F32)],
        compiler_params=pltpu.CompilerParams(dimension_semantics=("parallel", "arbitrary"), vmem_limit_bytes=VMEM_LIMIT),
    )(dff, w2, hpre, b1)


POOL_PAD = 16
POOL_ROWS = 256


def pool_lin(name, x, wvec, transpose, out_dtype):
    arr, width, cb = x
    s = arr.shape[0]
    n_steps = s // POOL_ROWS

    def body(x_ref, w_ref, o_ref, xp_ref):
        wv = w_ref[...]
        zeros = jnp.zeros((POOL_PAD, width), F32)
        xp_ref[0:POOL_PAD, :] = zeros
        xp_ref[s + POOL_PAD:s + 2 * POOL_PAD, :] = zeros

        def count(t0):
            t = lax.broadcasted_iota(jnp.int32, (POOL_ROWS, width), 0) + (t0 + 1)
            return jnp.minimum(t.astype(F32), wv)

        def fill(i, carry):
            t0 = pl.multiple_of(i * POOL_ROWS, POOL_ROWS)
            v = x_ref[pl.ds(t0, POOL_ROWS), :].astype(F32)
            if transpose:
                v = v / count(t0)
            xp_ref[pl.ds(t0 + POOL_PAD, POOL_ROWS), :] = v
            return carry

        lax.fori_loop(0, n_steps, fill, 0)

        def step(i, carry):
            t0 = pl.multiple_of(i * POOL_ROWS, POOL_ROWS)
            win = xp_ref[pl.ds(t0, POOL_ROWS + 2 * POOL_PAD), :]
            acc = jnp.zeros((POOL_ROWS, width), F32)
            for j in range(POOL_PAD):
                off = POOL_PAD + j if transpose else POOL_PAD - j
                acc = acc + jnp.where(wv > j, win[off:off + POOL_ROWS, :], 0.0)
            cur = x_ref[pl.ds(t0, POOL_ROWS), :].astype(F32)
            res = acc - cur if transpose else acc / count(t0) - cur
            o_ref[pl.ds(t0, POOL_ROWS), :] = res.astype(o_ref.dtype)
            return carry

        lax.fori_loop(0, n_steps, step, 0)

    return pl.pallas_call(
        body, name=name, grid=(1,),
        in_specs=[pl.BlockSpec((s, width), lambda i, cb=cb: (0, cb)), pl.BlockSpec((1, width), lambda i: (0, 0))],
        out_specs=pl.BlockSpec((s, width), lambda i: (0, 0)),
        out_shape=jax.ShapeDtypeStruct((s, width), out_dtype),
        scratch_shapes=[pltpu.VMEM((s + 2 * POOL_PAD, width), F32)],
        compiler_params=pltpu.CompilerParams(dimension_semantics=("arbitrary",), vmem_limit_bytes=VMEM_LIMIT),
    )(arr, wvec)


CONV_PAD = 32
CONV_ROWS = 128


def _glu(a, g):
    return a * jax.nn.sigmoid(g)


def conv_fwd(name, zc, w):
    arr, width, cb = zc
    s = arr.shape[0]
    n_steps = s // CONV_ROWS
    lead = CONV_PAD - (CONV_WIDTH - 1)

    def body(a_ref, g_ref, w_ref, o_ref, hp_ref):
        hp_ref[0:CONV_PAD, :] = jnp.zeros((CONV_PAD, D_CONV), F32)

        def fill(i, carry):
            t0 = pl.multiple_of(i * CONV_ROWS, CONV_ROWS)
            rows = pl.ds(t0, CONV_ROWS)
            hp_ref[pl.ds(t0 + CONV_PAD, CONV_ROWS), :] = _glu(a_ref[rows, :], g_ref[rows, :])
            return carry

        lax.fori_loop(0, n_steps, fill, 0)
        wv = w_ref[...]

        def step(i, carry):
            t0 = pl.multiple_of(i * CONV_ROWS, CONV_ROWS)
            win = hp_ref[pl.ds(t0, CONV_ROWS + CONV_PAD), :]
            acc = jnp.zeros((CONV_ROWS, D_CONV), F32)
            for k in range(CONV_WIDTH):
                acc = acc + wv[k:k + 1, :] * win[lead + k:lead + k + CONV_ROWS, :]
            o_ref[pl.ds(t0, CONV_ROWS), :] = acc
            return carry

        lax.fori_loop(0, n_steps, step, 0)

    return pl.pallas_call(
        body, name=name, grid=(1,),
        in_specs=[pl.BlockSpec((s, width), lambda i, cb=cb: (0, cb)),
                  pl.BlockSpec((s, width), lambda i, cb=cb: (0, cb + 1)), pl.BlockSpec(w.shape, lambda i: (0, 0))],
        out_specs=pl.BlockSpec((s, D_CONV), lambda i: (0, 0)),
        out_shape=jax.ShapeDtypeStruct((s, D_CONV), F32),
        scratch_shapes=[pltpu.VMEM((s + CONV_PAD, D_CONV), F32)],
        compiler_params=pltpu.CompilerParams(dimension_semantics=("arbitrary",), vmem_limit_bytes=VMEM_LIMIT),
    )(arr, arr, w)


def conv_bwd(name, zc, w, dout):
    arr, width, cb = zc
    s = arr.shape[0]
    n_steps = s // CONV_ROWS
    lead = CONV_PAD - (CONV_WIDTH - 1)

    def body(a_ref, g_ref, w_ref, d_ref, dz_ref, dw_ref, hp_ref, dp_ref):
        hp_ref[0:CONV_PAD, :] = jnp.zeros((CONV_PAD, D_CONV), F32)
        dp_ref[s:s + CONV_PAD, :] = jnp.zeros((CONV_PAD, D_CONV), F32)
        dw_ref[...] = jnp.zeros(dw_ref.shape, F32)

        def fill(i, carry):
            t0 = pl.multiple_of(i * CONV_ROWS, CONV_ROWS)
            rows = pl.ds(t0, CONV_ROWS)
            hp_ref[pl.ds(t0 + CONV_PAD, CONV_ROWS), :] = _glu(a_ref[rows, :], g_ref[rows, :])
            dp_ref[rows, :] = d_ref[rows, :]
            return carry

        lax.fori_loop(0, n_steps, fill, 0)
        wv = w_ref[...]

        def step(i, carry):
            t0 = pl.multiple_of(i * CONV_ROWS, CONV_ROWS)
            hwin = hp_ref[pl.ds(t0, CONV_ROWS + CONV_PAD), :]
            dwin = dp_ref[pl.ds(t0, CONV_ROWS + CONV_PAD), :]
            dcur = dwin[0:CONV_ROWS, :]
            dh = jnp.zeros((CONV_ROWS, D_CONV), F32)
            rows = []
            for k in range(CONV_WIDTH):
                rows.append(jnp.sum(dcur * hwin[lead + k:lead + k + CONV_ROWS, :], axis=0, keepdims=True))
                back = CONV_WIDTH - 1 - k
                dh = dh + wv[k:k + 1, :] * dwin[back:back + CONV_ROWS, :]
            rows.append(jnp.zeros((1, D_CONV), F32))
            dw_ref[...] += jnp.concatenate(rows, axis=0)
            rows_now = pl.ds(t0, CONV_ROWS)
            _, vjp = jax.vjp(_glu, a_ref[rows_now, :], g_ref[rows_now, :])
            da, dg = vjp(dh)
            dz_ref[pl.ds(t0, CONV_ROWS), :] = jnp.concatenate([da, dg], axis=1).astype(dz_ref.dtype)
            return carry

        lax.fori_loop(0, n_steps, step, 0)

    return pl.pallas_call(
        body, name=name, grid=(1,),
        in_specs=[pl.BlockSpec((s, width), lambda i, cb=cb: (0, cb)),
                  pl.BlockSpec((s, width), lambda i, cb=cb: (0, cb + 1)),
                  pl.BlockSpec(w.shape, lambda i: (0, 0)), pl.BlockSpec((s, D_CONV), lambda i: (0, 0))],
        out_specs=[pl.BlockSpec((s, 2 * width), lambda i: (0, 0)), pl.BlockSpec(w.shape, lambda i: (0, 0))],
        out_shape=[jax.ShapeDtypeStruct((s, 2 * width), BF16), jax.ShapeDtypeStruct(w.shape, F32)],
        scratch_shapes=[pltpu.VMEM((s + CONV_PAD, D_CONV), F32), pltpu.VMEM((s + CONV_PAD, D_CONV), F32)],
        compiler_params=pltpu.CompilerParams(dimension_semantics=("arbitrary",), vmem_limit_bytes=VMEM_LIMIT),
    )(arr, arr, w, dout)


HEADS_PER_STEP = LANE // HEAD_DIM
CHUNKS_PER_TILE = ATT_TILE // CHUNK
KEY_BLOCKS = N_PREV * CHUNK // ATT_TILE + 1
KEY_SPAN = KEY_BLOCKS * ATT_TILE


def _attn_tile(q, *rest, missing_cols):
    kcat = jnp.concatenate(rest[:KEY_BLOCKS], axis=0)
    vcat = jnp.concatenate(rest[KEY_BLOCKS:2 * KEY_BLOCKS], axis=0)
    bias = rest[2 * KEY_BLOCKS]
    lane = lax.broadcasted_iota(jnp.int32, (1, LANE), 1)
    col = lax.broadcasted_iota(jnp.int32, (1, KEY_SPAN), 1)
    missing = col < missing_cols
    qs = q * (HEAD_DIM ** -0.5)
    o = jnp.zeros((ATT_TILE, LANE), F32)
    for h in range(HEADS_PER_STEP):
        in_head = jnp.logical_and(lane >= h * HEAD_DIM, lane < (h + 1) * HEAD_DIM)
        sc = mm_nt(jnp.where(in_head, qs, 0.0), kcat) + bias[h]
        sc = jnp.where(missing, NEG_INF, sc)
        m = jnp.max(sc, axis=-1, keepdims=True)
        e = jnp.exp(sc - lax.stop_gradient(m))
        p = e / jnp.sum(e, axis=-1, keepdims=True)
        o = o + jnp.where(in_head, mm_nn(p, vcat), 0.0)
    return o


def _missing_cols(n):
    return jnp.maximum((KEY_BLOCKS - 1 - n) * ATT_TILE, 0)


def _attn_in_specs(nt):
    def spec(col0, back):
        return pl.BlockSpec((ATT_TILE, LANE),
                            lambda hp, n, col0=col0, back=back: (jnp.clip(n - back, 0, nt - 1), col0 // LANE + hp))
    backs = list(range(KEY_BLOCKS - 1, -1, -1))
    return ([spec(Z_Q, 0)] + [spec(Z_K, b) for b in backs] + [spec(Z_V, b) for b in backs]
            + [pl.BlockSpec((HEADS_PER_STEP, ATT_TILE, KEY_SPAN), lambda hp, n: (hp, 0, 0))])


def attn_fwd(name, z, bias):
    s = z.shape[0]
    nt = s // ATT_TILE
    n_in = 2 + 2 * KEY_BLOCKS

    def body(*refs):
        o_ref = refs[n_in]
        vals = [r[...] for r in refs[:n_in]]
        o = _attn_tile(*vals, missing_cols=_missing_cols(pl.program_id(1)))
        o_ref[...] = o.astype(o_ref.dtype)

    return pl.pallas_call(
        body, name=name, grid=(N_HEADS // HEADS_PER_STEP, nt),
        in_specs=_attn_in_specs(nt),
        out_specs=pl.BlockSpec((ATT_TILE, LANE), lambda hp, n: (n, hp)),
        out_shape=jax.ShapeDtypeStruct((s, D_ATTN), BF16),
        compiler_params=pltpu.CompilerParams(dimension_semantics=("parallel", "parallel"), vmem_limit_bytes=VMEM_LIMIT),
    )(*([z] * (n_in - 1)), bias)


def attn_bwd(name, z, bias, do, after=None):
    s = z.shape[0]
    nt = s // ATT_TILE
    n_in = 2 + 2 * KEY_BLOCKS
    nc = KEY_BLOCKS - 1

    deps = [] if after is None else [after]
    n_out = n_in + 1 + len(deps)

    def body(*refs):
        do_ref = refs[n_in]
        dq_ref, dk_ref, dv_ref, db_ref = refs[n_out:n_out + 4]
        kacc, vacc = refs[n_out + 4:n_out + 4 + nc], refs[n_out + 4 + nc:]
        n = pl.program_id(1)

        @pl.when(n == 0)
        def _():
            db_ref[...] = jnp.zeros(db_ref.shape, F32)
            for acc in (*kacc, *vacc):
                acc[...] = jnp.zeros(acc.shape, F32)

        def shift(out_ref, accs, contrib):
            @pl.when(n >= nc)
            def _():
                first = accs[0][...] if contrib is None else accs[0][...] + contrib[0]
                out_ref[...] = first.astype(out_ref.dtype)
            for j in range(nc - 1):
                accs[j][...] = accs[j + 1][...] if contrib is None else accs[j + 1][...] + contrib[j + 1]
            if contrib is not None:
                accs[nc - 1][...] = contrib[nc]

        @pl.when(n < nt)
        def _():
            fn = functools.partial(_attn_tile, missing_cols=_missing_cols(n))
            _, vjp = jax.vjp(fn, *[r[...] for r in refs[:n_in]])
            grads = vjp(do_ref[...].astype(F32))
            dq_ref[...] = grads[0].astype(dq_ref.dtype)
            db_ref[...] += grads[n_in - 1]
            shift(dk_ref, kacc, grads[1:1 + KEY_BLOCKS])
            shift(dv_ref, vacc, grads[1 + KEY_BLOCKS:1 + 2 * KEY_BLOCKS])

        @pl.when(n >= nt)
        def _():
            shift(dk_ref, kacc, None)
            shift(dv_ref, vacc, None)

    o_cur = pl.BlockSpec((ATT_TILE, LANE), lambda hp, n: (jnp.minimum(n, nt - 1), hp))
    o_old = pl.BlockSpec((ATT_TILE, LANE), lambda hp, n: (jnp.maximum(n - nc, 0), hp))
    b_spec = pl.BlockSpec((HEADS_PER_STEP, ATT_TILE, KEY_SPAN), lambda hp, n: (hp, 0, 0))
    return pl.pallas_call(
        body, name=name, grid=(N_HEADS // HEADS_PER_STEP, nt + nc),
        in_specs=_attn_in_specs(nt) + [o_cur] + [pl.BlockSpec(d.shape, lambda hp, n: (0, 0)) for d in deps],
        out_specs=[o_cur, o_old, o_old, b_spec],
        out_shape=[jax.ShapeDtypeStruct((s, D_ATTN), BF16)] * 3 + [jax.ShapeDtypeStruct((N_HEADS, ATT_TILE, KEY_SPAN), F32)],
        scratch_shapes=[pltpu.VMEM((ATT_TILE, LANE), F32)] * (2 * nc),
        compiler_params=pltpu.CompilerParams(dimension_semantics=("parallel", "arbitrary"), vmem_limit_bytes=VMEM_LIMIT),
    )(*([z] * (n_in - 1)), bias, do, *deps)


def loss_head(name, y, tgt, tm):
    s, d = y.shape

    def body(y_ref, t_ref, l_ref, dy_ref):
        i = pl.program_id(0)
        diff = y_ref[...] - t_ref[...]
        dy_ref[...] = diff * (1.0 / d)
        part = 0.5 * jnp.sum(jnp.mean(diff * diff, axis=-1, keepdims=True), axis=0, keepdims=True)

        @pl.when(i == 0)
        def _():
            l_ref[...] = jnp.zeros(l_ref.shape, F32)

        l_ref[...] += jnp.broadcast_to(part, l_ref.shape)

    row = pl.BlockSpec((tm, d), lambda i: (i, 0))
    return pl.pallas_call(
        body, name=name, grid=(s // tm,), in_specs=[row, row],
        out_specs=[pl.BlockSpec((8, LANE), lambda i: (0, 0)), row],
        out_shape=[jax.ShapeDtypeStruct((8, LANE), F32), jax.ShapeDtypeStruct((s, d), F32)],
        compiler_params=pltpu.CompilerParams(dimension_semantics=("arbitrary",), vmem_limit_bytes=VMEM_LIMIT),
    )(y, tgt)


def adamw(name, w, g, m, v, tr):
    r, c = w.shape
    assert r % tr == 0, (name, w.shape, tr)

    def body(w_ref, g_ref, m_ref, v_ref, d_ref, nm_ref, nv_ref):
        gg = g_ref[...]
        m2 = ADAM_B1 * m_ref[...] + (1.0 - ADAM_B1) * gg
        v2 = ADAM_B2 * v_ref[...] + (1.0 - ADAM_B2) * (gg * gg)
        m_hat = m2 / (1.0 - ADAM_B1 ** ADAM_STEP)
        v_hat = v2 / (1.0 - ADAM_B2 ** ADAM_STEP)
        d_ref[...] = -ADAM_LR * (m_hat / (jnp.sqrt(v_hat) + ADAM_EPS) + ADAM_WD * w_ref[...])
        nm_ref[...] = m2
        nv_ref[...] = v2

    blk = pl.BlockSpec((tr, c), lambda i: (i, 0))
    return pl.pallas_call(
        body, name=name, grid=(r // tr,), in_specs=[blk] * 4, out_specs=[blk] * 3,
        out_shape=[jax.ShapeDtypeStruct((r, c), F32)] * 3,
        compiler_params=pltpu.CompilerParams(dimension_semantics=("parallel",), vmem_limit_bytes=VMEM_LIMIT),
    )(w, g, m, v)


def adamw_sum(name, w, layer_blocks, m, v, tr):
    rows, c = w.shape
    nl = len(layer_blocks)
    nb, r, _ = layer_blocks[0].shape
    assert rows == nl * r and r % tr == 0, (name, w.shape, layer_blocks[0].shape, tr)
    per = r // tr

    def body(*refs):
        w_ref, b_refs, (m_ref, v_ref, g_ref, d_ref, nm_ref, nv_ref) = refs[0], refs[1:1 + nl], refs[1 + nl:]
        i = pl.program_id(0)

        def update(b_ref):
            gg = b_ref[0].astype(F32)
            for j in range(1, nb):
                gg = gg + b_ref[j].astype(F32)
            g_ref[...] = gg
            m2 = ADAM_B1 * m_ref[...] + (1.0 - ADAM_B1) * gg
            v2 = ADAM_B2 * v_ref[...] + (1.0 - ADAM_B2) * (gg * gg)
            m_hat = m2 / (1.0 - ADAM_B1 ** ADAM_STEP)
            v_hat = v2 / (1.0 - ADAM_B2 ** ADAM_STEP)
            d_ref[...] = -ADAM_LR * (m_hat / (jnp.sqrt(v_hat) + ADAM_EPS) + ADAM_WD * w_ref[...])
            nm_ref[...] = m2
            nv_ref[...] = v2

        for l in range(nl):
            pl.when(jnp.logical_and(i >= l * per, i < (l + 1) * per))(functools.partial(update, b_refs[l]))

    blk = pl.BlockSpec((tr, c), lambda i: (i, 0))
    b_specs = [pl.BlockSpec((nb, tr, c), lambda i, l=l: (0, jnp.clip(i - l * per, 0, per - 1), 0)) for l in range(nl)]
    return pl.pallas_call(
        body, name=name, grid=(rows // tr,),
        in_specs=[blk] + b_specs + [blk, blk], out_specs=[blk] * 4,
        out_shape=[jax.ShapeDtypeStruct((rows, c), F32)] * 4,
        compiler_params=pltpu.CompilerParams(dimension_semantics=("arbitrary",), vmem_limit_bytes=VMEM_LIMIT),
    )(w, *layer_blocks, m, v)


def sum_blocks(name, blocks, tr):
    nb, r, c = blocks.shape
    assert r % tr == 0, (name, blocks.shape, tr)

    def body(b_ref, o_ref):
        acc = b_ref[0].astype(F32)
        for j in range(1, nb):
            acc = acc + b_ref[j].astype(F32)
        o_ref[...] = acc

    return pl.pallas_call(
        body, name=name, grid=(r // tr,),
        in_specs=[pl.BlockSpec((nb, tr, c), lambda i: (0, i, 0))],
        out_specs=pl.BlockSpec((tr, c), lambda i: (i, 0)),
        out_shape=jax.ShapeDtypeStruct((r, c), F32),
        compiler_params=pltpu.CompilerParams(dimension_semantics=("parallel",), vmem_limit_bytes=VMEM_LIMIT),
    )(blocks)


FLIPS = [(0, 0, 1), (1, 0, 0), (0, 1, 0), (1, 1, 0), (1, 0, 1), (0, 1, 1), (1, 1, 1)]
ANY = pl.BlockSpec(memory_space=pl.ANY)


def _me():
    return lax.axis_index("x"), lax.axis_index("y"), lax.axis_index("c")


def _flip(pos, f):
    return tuple((1 - p) if fi else p for p, fi in zip(pos, f))


def _idx(pos):
    return 4 * pos[0] + 2 * pos[1] + pos[2]


def all_gather_multi(name, shards):
    n = len(shards)

    def body(*refs):
        x_refs, out_refs, token = refs[:n], refs[n:2 * n], refs[2 * n]
        send_sems, recv_sems, local_sems = refs[2 * n + 1:]
        token[...] = jnp.zeros_like(token)
        x, y, cc = _me()
        me, sibling = (x, y, cc), (x, y, 1 - cc)
        chips = [(1 - x, y), (x, 1 - y), (1 - x, 1 - y)]

        def copy(a, k, block, to, src=None):
            dst = out_refs[a].at[_idx(block)]
            return pltpu.make_async_remote_copy(
                src_ref=dst if src is None else src, dst_ref=dst, send_sem=send_sems.at[7 * a + k],
                recv_sem=recv_sems.at[7 * a + k], device_id=to, device_id_type=MESH)

        mine = [pltpu.make_async_copy(x_refs[a], out_refs[a].at[_idx(me)], local_sems.at[a]) for a in range(n)]
        for cp in mine:
            cp.start()
        first = []
        for a in range(n):
            first.append(copy(a, 0, me, sibling, src=x_refs[a]))
            first += [copy(a, 1 + j, me, (*chip, cc), src=x_refs[a]) for j, chip in enumerate(chips)]
        for cp in first:
            cp.start()
        passed = []
        for j, chip in enumerate(chips):
            for a in range(n):
                copy(a, 1 + j, (*chip, cc), me).wait_recv()
                fwd = copy(a, 4 + j, (*chip, cc), sibling)
                fwd.start()
                passed.append(fwd)
        for a in range(n):
            copy(a, 0, sibling, me).wait_recv()
            for j, chip in enumerate(chips):
                copy(a, 4 + j, (*chip, 1 - cc), me).wait_recv()
        for cp in first + passed:
            cp.wait_send()
        for cp in mine:
            cp.wait()

    return pl.pallas_call(
        body, name=name, in_specs=[ANY] * n, out_specs=[ANY] * n + [pl.BlockSpec(memory_space=pltpu.VMEM)],
        out_shape=[jax.ShapeDtypeStruct((N_DEV,) + a.shape, a.dtype) for a in shards]
        + [jax.ShapeDtypeStruct((8, LANE), F32)],
        scratch_shapes=[pltpu.SemaphoreType.DMA((7 * n,)), pltpu.SemaphoreType.DMA((7 * n,)),
                        pltpu.SemaphoreType.DMA((n,))],
    )(*shards)


def all_to_all_multi(name, blocks):
    n = len(blocks)

    def body(*refs):
        in_refs, out_refs = refs[:n], refs[n:2 * n]
        send_sems, recv_sems, local_sems = refs[2 * n:]
        me = _me()
        mi = _idx(me)
        mine = [pltpu.make_async_copy(in_refs[a].at[mi], out_refs[a].at[mi], local_sems.at[a]) for a in range(n)]
        for cp in mine:
            cp.start()
        sends, recvs = [], []
        for k, f in enumerate(FLIPS):
            peer = _flip(me, f)
            pi = _idx(peer)
            for a in range(n):
                sems = dict(send_sem=send_sems.at[7 * a + k], recv_sem=recv_sems.at[7 * a + k],
                            device_id=peer, device_id_type=MESH)
                sends.append(pltpu.make_async_remote_copy(src_ref=in_refs[a].at[pi], dst_ref=out_refs[a].at[mi], **sems))
                recvs.append(pltpu.make_async_remote_copy(src_ref=in_refs[a].at[mi], dst_ref=out_refs[a].at[pi], **sems))
        for cp in sends:
            cp.start()
        for cp in recvs:
            cp.wait_recv()
        for cp in sends:
            cp.wait_send()
        for cp in mine:
            cp.wait()

    return pl.pallas_call(
        body, name=name, in_specs=[ANY] * n, out_specs=[ANY] * n,
        out_shape=[jax.ShapeDtypeStruct(a.shape, a.dtype) for a in blocks],
        scratch_shapes=[pltpu.SemaphoreType.DMA((7 * n,)), pltpu.SemaphoreType.DMA((7 * n,)),
                        pltpu.SemaphoreType.DMA((n,))],
    )(*blocks)


HBM = pl.BlockSpec(memory_space=pltpu.HBM)
SEM = pl.BlockSpec(memory_space=pltpu.SEMAPHORE)
DATAFLOW = pltpu.SideEffectType.DATAFLOW_SIDE_EFFECTING


def _exchange_copies(a_refs, l_refs, send_sems, recv_sems, gather):
    me = _me()
    mi = _idx(me)
    out = []
    for k, f in enumerate(FLIPS):
        peer = _flip(me, f)
        for a in range(len(a_refs)):
            src = a_refs[a] if gather else a_refs[a].at[_idx(peer)]
            out.append(pltpu.make_async_remote_copy(
                src_ref=src, dst_ref=l_refs[a].at[mi], send_sem=send_sems.at[7 * a + k],
                recv_sem=recv_sems.at[7 * a + k], device_id=peer, device_id_type=MESH))
    return out


def exchange_start(name, arrays, gather):
    n = len(arrays)
    lands = [lax.empty(((N_DEV,) + a.shape) if gather else a.shape, a.dtype) for a in arrays]

    def body(*refs):
        a_refs, l_refs = refs[:n], refs[n:2 * n]
        send_sems, recv_sems = refs[2 * n], refs[2 * n + 1]
        token = refs[4 * n + 2]
        for cp in _exchange_copies(a_refs, l_refs, send_sems, recv_sems, gather):
            cp.start()
        token[...] = jnp.zeros_like(token)

    hbm = lambda a: pltpu.HBM(a.shape, a.dtype)
    res = pl.pallas_call(
        body, name=name,
        out_shape=(pltpu.SemaphoreType.DMA((7 * n,)), pltpu.SemaphoreType.DMA((7 * n,)),
                   *[hbm(a) for a in arrays], *[hbm(a) for a in lands], jax.ShapeDtypeStruct((8, LANE), F32)),
        in_specs=[HBM] * (2 * n),
        out_specs=(SEM, SEM, *[HBM] * (2 * n), pl.BlockSpec(memory_space=pltpu.VMEM)),
        input_output_aliases={i: i + 2 for i in range(2 * n)},
        compiler_params=pltpu.CompilerParams(has_side_effects=DATAFLOW),
    )(*[pltpu.with_memory_space_constraint(a, pltpu.HBM) for a in arrays],
      *[pltpu.with_memory_space_constraint(a, pltpu.HBM) for a in lands])
    return res[0], res[1], list(res[2:2 + n]), list(res[2 + n:2 + 2 * n]), res[-1]


def exchange_wait(name, send_sems, recv_sems, arrays, lands, after, gather):
    n = len(arrays)
    after = list(after) if isinstance(after, (list, tuple)) else [after]

    def body(*refs):
        a_refs, l_refs = refs[:n], refs[n:2 * n]
        ssem, rsem = refs[2 * n], refs[2 * n + 1]
        for cp in _exchange_copies(a_refs, l_refs, ssem, rsem, gather):
            cp.wait_send()
            cp.wait_recv()

    hbm = lambda a: pltpu.HBM(a.shape, a.dtype)
    res = pl.pallas_call(
        body, name=name,
        out_shape=(*[hbm(a) for a in arrays], *[hbm(a) for a in lands]),
        in_specs=[HBM] * (2 * n) + [SEM, SEM] + [pl.BlockSpec(memory_space=pl.ANY)] * len(after),
        out_specs=tuple([HBM] * (2 * n)),
        input_output_aliases={i: i for i in range(2 * n)},
        compiler_params=pltpu.CompilerParams(has_side_effects=DATAFLOW),
    )(*arrays, *lands, send_sems, recv_sems, *after)
    return list(res[n:])


def ada_fwd(name, c_row, w_cat, b_lay):
    d = c_row.shape[1]
    ncol = w_cat.shape[1]
    vmem = pl.BlockSpec(memory_space=pltpu.VMEM)

    def body(c_ref, w_ref, b_ref, mod_ref, cact_ref, token, call, send, land, s1, r1, s2, r2):
        token[...] = jnp.zeros_like(token)
        me = _me()
        mi = _idx(me)
        call[mi] = c_ref[...]

        def exchange(src_of, dst_buf, ssem, rsem):
            sends, recvs = [], []
            for k, f in enumerate(FLIPS):
                peer = _flip(me, f)
                sends.append(pltpu.make_async_remote_copy(
                    src_ref=src_of(peer), dst_ref=dst_buf.at[mi], send_sem=ssem.at[k], recv_sem=rsem.at[k],
                    device_id=peer, device_id_type=MESH))
                recvs.append(pltpu.make_async_remote_copy(
                    src_ref=src_of(peer), dst_ref=dst_buf.at[_idx(peer)], send_sem=ssem.at[k], recv_sem=rsem.at[k],
                    device_id=peer, device_id_type=MESH))
            for cp in sends:
                cp.start()
            for cp in recvs:
                cp.wait_recv()
            for cp in sends:
                cp.wait_send()

        exchange(lambda peer: c_ref, call, s1, r1)
        for p in range(N_DEV):
            cact_ref[pl.ds(p, 1), :] = jax.nn.silu(call[p])
        res = _dg(cact_ref[...], w_ref[...], 1, 0)
        for p in range(N_DEV):
            send[p] = res[p:p + 1, :]
        land[mi] = send[mi]
        exchange(lambda peer: send.at[_idx(peer)], land, s2, r2)
        mod_ref[...] = land[...] + b_ref[...]

    return pl.pallas_call(
        body, name=name, in_specs=[vmem, vmem, vmem], out_specs=[vmem, vmem, vmem],
        out_shape=[jax.ShapeDtypeStruct((N_DEV, 1, ncol), F32), jax.ShapeDtypeStruct((N_DEV, d), F32),
                   jax.ShapeDtypeStruct((8, LANE), F32)],
        scratch_shapes=[pltpu.VMEM((N_DEV, 1, d), F32), pltpu.VMEM((N_DEV, 1, ncol), F32),
                        pltpu.VMEM((N_DEV, 1, ncol), F32),
                        pltpu.SemaphoreType.DMA((7,)), pltpu.SemaphoreType.DMA((7,)),
                        pltpu.SemaphoreType.DMA((7,)), pltpu.SemaphoreType.DMA((7,))],
        compiler_params=pltpu.CompilerParams(vmem_limit_bytes=VMEM_LIMIT),
    )(c_row, w_cat, b_lay)


POOL_WINDOWS = (2, 4, 8, 16)
POOL_GROUP = 64
N_REL = 2 * REL_CLIP + 1
PACK_COLS = 1024
SMALL_NAMES = ["b_ada", "b_gate", "w_pool", "pool_scale", "rel_bias", "conv_w", "conv_b", "conv_ln_g",
               "conv_ln_b", "ln_mix_g", "ln_mix_b", "b_ff1", "b_ff2", "ln_ff_g", "ln_ff_b"]
BIG_NAMES = ["w_in", "w_br_pool", "w_br_attn", "w_br_conv", "w_o", "w_ff1", "w_ff2"]
ROW_SHARDED = ("w_in", "w_o", "w_ff2")
WEIGHT_NAMES = ["w_ada", "b_ada", "w_in", "b_gate", "w_pool", "pool_scale", "rel_bias", "conv_w", "conv_b",
                "conv_ln_g", "conv_ln_b", "w_br_pool", "w_br_attn", "w_br_conv", "w_o", "ln_mix_g", "ln_mix_b",
                "w_ff1", "b_ff1", "w_ff2", "b_ff2", "ln_ff_g", "ln_ff_b"]


def _bias_table(rel_bias):
    far = jnp.broadcast_to(rel_bias[:, 2 * REL_CLIP:], (N_HEADS, BAND - REL_CLIP))
    near = rel_bias[:, REL_CLIP - CHUNK + 1:2 * REL_CLIP][:, ::-1]
    ext = jnp.concatenate([far, near, jnp.zeros((N_HEADS, 1), F32)], axis=1)
    length = BAND + CHUNK
    flat = jnp.tile(ext, (1, CHUNK + 1))
    skew = flat[:, CHUNK - 1:CHUNK - 1 + CHUNK * (length - 1)].reshape(N_HEADS, CHUNK, length - 1)
    return skew[:, :, :BAND]


def _bias_full(rel_bias):
    tab = _bias_table(rel_bias)
    return jnp.concatenate(
        [jnp.pad(tab, ((0, 0), (0, 0), (i * CHUNK, KEY_SPAN - BAND - i * CHUNK)), constant_values=NEG_INF)
         for i in range(CHUNKS_PER_TILE)], axis=1)


def _block_diag(w_pool):
    out = jnp.zeros((D_POOL, D_POOL), F32)
    for g in range(len(POOL_WINDOWS)):
        out = lax.dynamic_update_slice(out, w_pool[g], (g * POOL_GROUP, g * POOL_GROUP))
    return out


def _flat_pad(arrs, mult):
    flat = jnp.concatenate([a.reshape(-1) for a in arrs])
    pad = (-flat.shape[0]) % mult
    return jnp.pad(flat, (0, pad)) if pad else flat


def _unflat(flat, shapes):
    out, off = [], 0
    for shp in shapes:
        n = int(np.prod(shp))
        out.append(flat[off:off + n].reshape(shp))
        off += n
    return out


def _to_blocks(name, full):
    k, n = full.shape
    if name in ROW_SHARDED:
        return full.reshape(N_DEV, k // N_DEV, n)
    return full.reshape(k, N_DEV, n // N_DEV).transpose(1, 0, 2)


def _from_blocks(name, blocks):
    nb, r, c = blocks.shape
    if name in ROW_SHARDED:
        return blocks.reshape(nb * r, c)
    return blocks.transpose(1, 0, 2).reshape(r, nb * c)


class _Layer:
    pass


def _row(v):
    return v.reshape(1, -1)


def _layer_fwd(x, modr, w, wvec, fetch_rest):
    sh_m, sc_m, g_m, sh_f, sc_f, g_f = modr
    (u,) = row_fwd("lnmod_mix", f_lnmod, [x], [sc_m, sh_m], [(D_MODEL, BF16)], 512)
    z = mm_big("mm_in", u, w.w_in_t, "nt", (1024, 896, 1024), F32, j_outer=True)
    p = pool_lin("pool_fwd", (z, D_POOL, Z_POOL // D_POOL), wvec, False, F32)
    ao = attn_fwd("attn_fwd", z, w.bias)
    cv = conv_fwd("conv_fwd", (z, D_CONV, Z_CONV // D_CONV), w.conv_w)
    if not hasattr(w, "wo"):
        fetch_rest(w, cv)
    mparams = [w.wbd, w.ps, w.wbp, w.wba, w.wbc, w.cb, w.clg, w.clb, w.bg, w.wo, g_m, w.lmg, w.lmb]
    gate_blocks = [(z, GATE_BLOCK, Z_GATE // GATE_BLOCK + k) for k in range(3 * D_MODEL // GATE_BLOCK)]
    (x1,) = row_fwd("merge", f_merge, [p, ao, cv, *gate_blocks, x], mparams, [(D_MODEL, F32)], 512)
    (u2,) = row_fwd("lnmod_ff", f_lnmod, [x1], [sc_f, sh_f], [(D_MODEL, BF16)], 512)
    if not hasattr(w, "w_ff1"):
        fetch_rest(w, u2)
    hpre, h = mm_ff1_relu2("mm_ff1", u2, w.w_ff1, w.b1, 1024, 1024)
    ff = mm_big("mm_ff2", h, w.w_ff2, "nn", (512, 1024, 4096), F32)
    (x2,) = row_fwd("ffout", f_ffout, [x1, ff], [w.b2, g_f, w.lfg, w.lfb], [(D_MODEL, F32)], 512)
    return x2, (x, u, z, p, ao, cv, x1, u2, hpre, h, ff, mparams)


GRAD_GROUPS = [("ff", ["w_ff2", "w_ff1"]), ("mix", ["w_o", "w_br_pool", "w_br_attn", "w_br_conv"]), ("in", ["w_in"])]


def _layer_bwd(dx2, saved, modr, w, wvec, ready):
    x, u, z, p, ao, cv, x1, u2, hpre, h, ff, mparams = saved
    sh_m, sc_m, g_m, sh_f, sc_f, g_f = modr
    g = {}
    (dx1a, dff), (g["b_ff2"], dgf, g["ln_ff_g"], g["ln_ff_b"]) = row_bwd(
        "ffout_bwd", f_ffout, [x1, ff], [w.b2, g_f, w.lfg, w.lfb], [dx2], 512, [(0, F32), (1, BF16)], [0, 1, 2, 3])
    dhpre, g["b_ff1"] = mm_dh_relu2("mm_dh", dff, w.w_ff2, hpre, w.b1, 1024, 1024)
    g["w_ff2"] = mm_big("mm_dw_ff2", h, dff, "tn", (1024, 1024, 2048), BF16)
    du2 = mm_big("mm_du2", dhpre, w.w_ff1, "nt", (512, 1024, 4096), F32)
    g["w_ff1"] = mm_big("mm_dw_ff1", u2, dhpre, "tn", (1024, 1024, 2048), BF16)
    sc_f = sc_f + ready("ff", g)[0:1, 0:1]
    (dx1,), (dscf, dshf) = row_bwd("lnmod_ff_bwd", f_lnmod, [x1], [sc_f, sh_f], [du2], 512, [(0, F32)], [0, 1],
                                   add_to=dx1a)
    gate_blocks = [(z, GATE_BLOCK, Z_GATE // GATE_BLOCK + k) for k in range(3 * D_MODEL // GATE_BLOCK)]
    (dp, dao, dcv, *dzg, dxa), dm = row_bwd(
        "merge_bwd", f_merge, [p, ao, cv, *gate_blocks, x], mparams, [dx1], 256,
        [(0, F32), (1, BF16), (2, F32), (3, BF16), (4, BF16), (5, BF16), (6, BF16), (7, F32)], list(range(13)))
    (dwbd, g["pool_scale"], g["w_br_pool"], g["w_br_attn"], g["w_br_conv"], g["conv_b"], g["conv_ln_g"],
     g["conv_ln_b"], g["b_gate"], g["w_o"], dgm, g["ln_mix_g"], g["ln_mix_b"]) = dm
    g["w_pool"] = jnp.stack([dwbd[i * POOL_GROUP:(i + 1) * POOL_GROUP, i * POOL_GROUP:(i + 1) * POOL_GROUP]
                             for i in range(len(POOL_WINDOWS))])
    tok = ready("mix", g)
    dzp = pool_lin("pool_bwd", (dp, D_POOL, 0), wvec + tok[0:1, 0:1], True, BF16)
    dq, dk, dv, dbias = attn_bwd("attn_bwd", z, w.bias, dao, after=tok)
    (g["rel_bias"],) = w.bias_vjp(dbias)
    dzc, dcw = conv_bwd("conv_bwd", (z, D_CONV, Z_CONV // D_CONV), w.conv_w + tok[0:1, 0:1], dcv)
    g["conv_w"] = dcw[:CONV_WIDTH]
    dz = jnp.concatenate([dzp, dq, dk, dv, dzc, *dzg], axis=1)
    du = mm_big("mm_du", dz, w.w_in_t, "nn", (512, 1024, D_IN), F32)
    (dx,), (dscm, dshm) = row_bwd("lnmod_mix_bwd", f_lnmod, [x], [sc_m, sh_m], [du], 512, [(0, F32)], [0, 1],
                                  add_to=dxa)
    g["dmod"] = jnp.concatenate([dshm, dscm, dgm, dshf, dscf, dgf], axis=1)
    tok = ready("small", g)
    g["w_in"] = mm_big("mm_dw_in", dz, u, "tn", (896, 1024, 2048), BF16, after=tok)
    return dx, g, ready("in", g)


def kernel(x, c, w_ada, b_ada, w_in, b_gate, w_pool, pool_scale, rel_bias, conv_w, conv_b, conv_ln_g, conv_ln_b, w_br_pool, w_br_attn, w_br_conv, w_o, ln_mix_g, ln_mix_b, w_ff1, b_ff1, w_ff2, b_ff2, ln_ff_g, ln_ff_b, loss_target, m_w_ada, m_b_ada, m_w_in, m_b_gate, m_w_pool, m_pool_scale, m_rel_bias, m_conv_w, m_conv_b, m_conv_ln_g, m_conv_ln_b, m_w_br_pool, m_w_br_attn, m_w_br_conv, m_w_o, m_ln_mix_g, m_ln_mix_b, m_w_ff1, m_b_ff1, m_w_ff2, m_b_ff2, m_ln_ff_g, m_ln_ff_b, v_w_ada, v_b_ada, v_w_in, v_b_gate, v_w_pool, v_pool_scale, v_rel_bias, v_conv_w, v_conv_b, v_conv_ln_g, v_conv_ln_b, v_w_br_pool, v_w_br_attn, v_w_br_conv, v_w_o, v_ln_mix_g, v_ln_mix_b, v_w_ff1, v_b_ff1, v_w_ff2, v_b_ff2, v_ln_ff_g, v_ln_ff_b):
    args = dict(locals())
    wts = {n: args[n] for n in WEIGHT_NAMES}
    mom = {n: args["m_" + n] for n in WEIGHT_NAMES}
    var = {n: args["v_" + n] for n in WEIGHT_NAMES}
    me = 4 * lax.axis_index("x") + 2 * lax.axis_index("y") + lax.axis_index("c")
    xs, tgt = x[0], loss_target[0]
    nc_ada = w_ada.shape[2]
    wvec = jnp.asarray(np.repeat(np.array(POOL_WINDOWS, np.float32), POOL_GROUP)[None, :])

    w_cat = jnp.concatenate([w_ada[l] for l in range(DEPTH)], axis=1)
    b_lay = b_ada.reshape(DEPTH, N_DEV, nc_ada).transpose(1, 0, 2).reshape(N_DEV, 1, DEPTH * nc_ada)
    land, cact, ada_token = ada_fwd("ada_fwd", c, w_cat, b_lay)
    mod = land.reshape(N_DEV, DEPTH, nc_ada).transpose(1, 0, 2).reshape(DEPTH, 6 * D_MODEL)
    modr = [[mod[l:l + 1, i * D_MODEL:(i + 1) * D_MODEL] for i in range(6)] for l in range(DEPTH)]

    cw_pack = _flat_pad([conv_w], 8 * LANE).reshape(-1, LANE) + ada_token[0:1, 0:1]
    xlayout = lambda n, a: jnp.swapaxes(a, -1, -2) if n == "w_in" else a
    shards = [[xlayout(n, wts[n][l]).astype(BF16) for n in BIG_NAMES] for l in range(DEPTH)]
    n_first = [BIG_NAMES.index("w_ff1"), 1]
    *first0, cw_all, token = all_gather_multi("gather_first_l0", shards[0][:n_first[0]] + [cw_pack])
    cw_all = cw_all.reshape(N_DEV, -1)[:, :conv_w.size]
    conv_full = cw_all.reshape((N_DEV,) + conv_w.shape).transpose(1, 2, 0, 3).reshape(DEPTH, CONV_WIDTH, D_CONV)
    gathers = {}
    for key, arrs in [("rest_l0", shards[0][n_first[0]:]), ("first_l1", shards[1][:n_first[1]]),
                      ("rest_l1", shards[1][n_first[1]:])]:
        arrs = [a + token[0:1, 0:1].astype(a.dtype) for a in arrs]
        ssem, rsem, thru, lands, token = exchange_start(f"gather_{key}_start", arrs, True)
        gathers[key] = (arrs, ssem, rsem, thru, lands)
    modr[0] = [r + token[0:1, 0:1] for r in modr[0]]

    def with_own(lands_, own):
        return [lax.dynamic_update_index_in_dim(ld, o, me, axis=0) for ld, o in zip(lands_, own)]

    def gathered(key, after):
        arrs, ssem, rsem, thru, lands = gathers[key]
        return with_own(exchange_wait(f"gather_{key}_wait", ssem, rsem, thru, lands, after, True), arrs)

    attr = dict(w_in="w_in_t", w_br_pool="wbp", w_br_attn="wba", w_br_conv="wbc", w_o="wo", w_ff1="w_ff1", w_ff2="w_ff2")

    def assign(w, names, blocks):
        for n, g in zip(names, blocks):
            setattr(w, attr[n], _from_blocks(n, g))

    def fetch_rest_for(l):
        return lambda w, after: assign(w, BIG_NAMES[n_first[l]:], gathered(f"rest_l{l}", after))

    def layer_weights(l, first_blocks):
        w = _Layer()
        assign(w, BIG_NAMES[:n_first[l]], first_blocks)
        w.wbd = _block_diag(w_pool[l])
        w.ps, w.cb, w.clg, w.clb = _row(pool_scale[l]), _row(conv_b[l]), _row(conv_ln_g[l]), _row(conv_ln_b[l])
        w.bg, w.lmg, w.lmb = _row(b_gate[l]), _row(ln_mix_g[l]), _row(ln_mix_b[l])
        w.b1, w.b2, w.lfg, w.lfb = _row(b_ff1[l]), _row(b_ff2[l]), _row(ln_ff_g[l]), _row(ln_ff_b[l])
        w.conv_w = jnp.pad(conv_full[l], ((0, CONV_PAD - CONV_WIDTH), (0, 0)))
        w.bias, w.bias_vjp = jax.vjp(_bias_full, rel_bias[l])
        return w

    layers, saved = [layer_weights(0, first0)], []
    h, sv = _layer_fwd(xs, modr[0], layers[0], wvec, fetch_rest_for(0))
    saved.append(sv)
    layers.append(layer_weights(1, gathered("first_l1", h)))
    h, sv = _layer_fwd(h, modr[1], layers[1], wvec, fetch_rest_for(1))
    saved.append(sv)
    lpart, dy = loss_head("loss_head", h, tgt, 512)
    loss = lax.psum(lpart[0, 0], ("x", "y", "c"))
    grads, dmods = [None] * DEPTH, [None] * DEPTH
    pending = {}
    small_shapes = [wts[n].shape if n != "conv_w" else (DEPTH, CONV_WIDTH, D_CONV) for n in SMALL_NAMES]

    def ready_for(l):
        def ready(group, g):
            if group == "small":
                if l > 0:
                    return None
                both = [g, grads[1]]
                local = [jnp.concatenate([both[k]["dmod"] for k in range(DEPTH)], axis=0)]
                local += [jnp.stack([both[k][n].reshape(shp[1:]) for k in range(DEPTH)])
                          for n, shp in zip(SMALL_NAMES[1:], small_shapes[1:])]
                pack = _flat_pad(local, 8 * LANE).reshape(-1, LANE)
                ssem, rsem, thru, lands, token = exchange_start("gather_small_grads_start", [pack], True)
                pending["small"] = (pack, ssem, rsem, thru, lands)
                return token
            names = dict(GRAD_GROUPS)[group]
            blocks = [_to_blocks(n, g[n]).astype(BF16) for n in names]
            ssem, rsem, thru, lands, token = exchange_start(f"scatter_l{l}_{group}_start", blocks, False)
            pending[(l, group)] = (names, blocks, ssem, rsem, thru, lands)
            return token
        return ready

    def received(l, group, after):
        names, blocks, ssem, rsem, thru, lands = pending[(l, group)]
        lands = exchange_wait(f"scatter_l{l}_{group}_wait", ssem, rsem, thru, lands, after, False)
        own = [lax.dynamic_index_in_dim(b, me, axis=0, keepdims=False) for b in blocks]
        return dict(zip(names, with_own(lands, own)))

    dy, grads[1], token = _layer_bwd(dy, saved[1], modr[1], layers[1], wvec, ready_for(1))
    modr0 = [r + token[0:1, 0:1] for r in modr[0]]
    dy, grads[0], token = _layer_bwd(dy, saved[0], modr0, layers[0], wvec, ready_for(0))
    recv = [{} for _ in range(DEPTH)]
    for l, group in [(1, "ff"), (1, "mix"), (1, "in"), (0, "ff"), (0, "mix")]:
        recv[l].update(received(l, group, token))
    grad_x = dy[None]
    dmod_size = DEPTH * 6 * D_MODEL

    small_pack, ssem, rsem, thru, lands = pending["small"]
    (small_all,) = with_own(exchange_wait("gather_small_grads_wait", ssem, rsem, thru, lands, token, True), [small_pack])
    small_sum = sum_blocks("sum_small_grads", small_all, small_pack.shape[0]).reshape(-1)
    gsmall = dict(zip(SMALL_NAMES, _unflat(small_sum, small_shapes)))
    gw = dict(gsmall)
    gw["conv_w"] = lax.dynamic_slice_in_dim(gsmall["conv_w"], me * conv_w.shape[2], conv_w.shape[2], axis=2)

    dmod_all = small_all.reshape(N_DEV, -1)[:, :dmod_size].reshape(N_DEV, DEPTH, N_DEV, nc_ada)
    dm_mine = lax.dynamic_index_in_dim(dmod_all, me, axis=2, keepdims=False).reshape(N_DEV, DEPTH * nc_ada)
    cact_t = jnp.pad(cact.T, ((0, 0), (0, LANE - N_DEV)))
    dm_pad = jnp.pad(dm_mine, ((0, LANE - N_DEV), (0, 0)))
    dw_cat = mm_big("mm_dw_ada", cact_t, dm_pad, "nn", (D_MODEL, DEPTH * nc_ada, LANE), F32)
    gw["w_ada"] = jnp.stack([dw_cat[:, l * nc_ada:(l + 1) * nc_ada] for l in range(DEPTH)])

    delta, new_m, new_v = {}, {}, {}
    packs = [_flat_pad([src[n] for n in SMALL_NAMES], 8 * LANE).reshape(-1, LANE) for src in (wts, gw, mom, var)]
    small_res = adamw("adamw_small", *packs, packs[0].shape[0])
    shapes = [wts[n].shape for n in SMALL_NAMES]
    for out, flat in zip((delta, new_m, new_v), small_res):
        out.update(dict(zip(SMALL_NAMES, _unflat(flat.reshape(-1), shapes))))
    for n in ["w_ada", "w_ff2", "w_ff1", "w_o", "w_br_pool", "w_br_attn", "w_br_conv", "w_in"]:
        shp = xlayout(n, wts[n]).shape
        two_d = lambda a, shp=shp, n=n: xlayout(n, a).reshape(shp[0] * shp[1], shp[2])
        tr = 224 if n == "w_in" else min(256, shp[1])
        if n == "w_ada":
            res = (gw[n],) + tuple(adamw("adamw_" + n, two_d(wts[n]), two_d(gw[n]), two_d(mom[n]), two_d(var[n]), tr))
        else:
            if n == "w_in":
                done = [small_res[2]] + [new_v[k] for k in ["w_ada"] + BIG_NAMES[1:]]
                recv[0].update(received(0, "in", done))
            res = adamw_sum("adamw_" + n, two_d(wts[n]), [recv[l][n] for l in range(DEPTH)], two_d(mom[n]),
                            two_d(var[n]), tr)
        gw[n], delta[n], new_m[n], new_v[n] = [xlayout(n, a.reshape(shp)) for a in res]

    return (loss, grad_x, *[gw[n] for n in WEIGHT_NAMES], *[delta[n] for n in WEIGHT_NAMES],
            *[new_m[n] for n in WEIGHT_NAMES], *[new_v[n] for n in WEIGHT_NAMES])
```

```python
import functools

import jax
import jax.numpy as jnp
import numpy as np
from jax import lax
from jax.experimental import pallas as pl
from jax.experimental.pallas import tpu as pltpu

F32 = jnp.float32
BF16 = jnp.bfloat16
MESH = pl.DeviceIdType.MESH

D_MODEL = 1024
DEPTH = 2
CHUNK = 64
N_HEADS = 8
HEAD_DIM = 64
D_POOL = 256
D_ATTN = 512
D_CONV = 256
CONV_WIDTH = 31
D_FF = 4096
D_IN = 5376
N_PREV = 8
BAND = (N_PREV + 1) * CHUNK
REL_CLIP = 128
ALPHA = (2.0 * DEPTH) ** 0.25
LN_EPS = 1e-5
NEG_INF = -1e30
N_DEV = 8

ADAM_LR, ADAM_B1, ADAM_B2, ADAM_EPS, ADAM_WD, ADAM_STEP = 0.001, 0.9, 0.999, 1e-08, 0.01, 10

VMEM_LIMIT = 56 * 1024 * 1024

Z_POOL, Z_Q, Z_K, Z_V, Z_CONV, Z_GATE = 0, 256, 768, 1280, 1792, 2304
GATE_BLOCK = 768
ATT_TILE = 512
LANE = 128


def _dg(a, b, ca, cb):
    return lax.dot_general(a.astype(BF16), b.astype(BF16), (((ca,), (cb,)), ((), ())),
                           preferred_element_type=F32)


@jax.custom_vjp
def mm_nn(a, b):
    return _dg(a, b, 1, 0)


def _mm_nn_fwd(a, b):
    return _dg(a, b, 1, 0), (a, b)


def _mm_nn_bwd(res, g):
    a, b = res
    return _dg(g, b, 1, 1).astype(a.dtype), _dg(a, g, 0, 0).astype(b.dtype)


mm_nn.defvjp(_mm_nn_fwd, _mm_nn_bwd)


@jax.custom_vjp
def mm_nt(a, b):
    return _dg(a, b, 1, 1)


def _mm_nt_fwd(a, b):
    return _dg(a, b, 1, 1), (a, b)


def _mm_nt_bwd(res, g):
    a, b = res
    return _dg(g, b, 1, 0).astype(a.dtype), _dg(g, a, 0, 0).astype(b.dtype)


mm_nt.defvjp(_mm_nt_fwd, _mm_nt_bwd)


@jax.custom_vjp
def mm_nn_shadow(a, w, shadow):
    return _dg(a, w, 1, 0)


def _mm_nn_shadow_fwd(a, w, shadow):
    return _dg(a, w, 1, 0), (a, w)


def _mm_nn_shadow_bwd(res, g):
    a, w = res
    return _dg(g, w, 1, 1).astype(a.dtype), jnp.zeros_like(w), _dg(a, g, 0, 0)


mm_nn_shadow.defvjp(_mm_nn_shadow_fwd, _mm_nn_shadow_bwd)


def mm_w(a, w):
    return mm_nn_shadow(a, w[0], w[1]) if isinstance(w, tuple) else mm_nn(a, w)


def _ln(x):
    mu = jnp.mean(x, axis=-1, keepdims=True)
    xc = x - mu
    var = jnp.mean(xc * xc, axis=-1, keepdims=True)
    return xc * lax.rsqrt(var + LN_EPS)


def _norm_rows(rows):
    return [r if isinstance(r, tuple) else (r, r.shape[1], 0) for r in rows]


def _row_spec(tm, r):
    _, width, cb = r
    return pl.BlockSpec((tm, width), lambda i, cb=cb: (i, cb))


def _full_spec(a):
    nd = a.ndim
    return pl.BlockSpec(a.shape, lambda i, nd=nd: (0,) * nd)


def row_fwd(name, f, rows, params, outs, tm):
    rows = _norm_rows(rows)
    s = rows[0][0].shape[0]
    nr, npar = len(rows), len(params)

    def body(*refs):
        r = [x[...].astype(F32) for x in refs[:nr]]
        p = [x[...] for x in refs[nr:nr + npar]]
        res = f(*r, *p)
        for o_ref, o in zip(refs[nr + npar:], res):
            o_ref[...] = o.astype(o_ref.dtype)

    return pl.pallas_call(
        body, name=name, grid=(s // tm,),
        in_specs=[_row_spec(tm, r) for r in rows] + [_full_spec(p) for p in params],
        out_specs=[pl.BlockSpec((tm, w), lambda i: (i, 0)) for w, _ in outs],
        out_shape=[jax.ShapeDtypeStruct((s, w), dt) for w, dt in outs],
        compiler_params=pltpu.CompilerParams(dimension_semantics=("parallel",), vmem_limit_bytes=VMEM_LIMIT),
    )(*[r[0] for r in rows], *params)


def row_bwd(name, f, rows, params, douts, tm, want_rows, want_params, add_to=None):
    rows = _norm_rows(rows)
    s = rows[0][0].shape[0]
    nr, npar, nd = len(rows), len(params), len(douts)
    nadd = 0 if add_to is None else 1
    n_in = nr + npar + nd + nadd

    def body(*refs):
        i = pl.program_id(0)
        r = [x[...].astype(F32) for x in refs[:nr]]
        p = [(x[...], jnp.zeros(x.shape, F32)) if x.dtype == BF16 else x[...] for x in refs[nr:nr + npar]]
        d = [x[...].astype(F32) for x in refs[nr + npar:nr + npar + nd]]
        _, vjp = jax.vjp(f, *r, *p)
        g = vjp(tuple(d))
        out_refs = refs[n_in:]
        for k, (idx, _) in enumerate(want_rows):
            val = g[idx]
            if nadd and k == 0:
                val = val + refs[n_in - 1][...].astype(F32)
            out_refs[k][...] = val.astype(out_refs[k].dtype)
        for k, idx in enumerate(want_params):
            gp = g[nr + idx]
            gp = gp[1] if isinstance(gp, tuple) else gp
            o_ref = out_refs[len(want_rows) + k]

            @pl.when(i == 0)
            def _():
                o_ref[...] = gp

            @pl.when(i > 0)
            def _():
                o_ref[...] += gp

    in_specs = ([_row_spec(tm, r) for r in rows] + [_full_spec(p) for p in params]
                + [pl.BlockSpec((tm, d.shape[1]), lambda i: (i, 0)) for d in douts])
    args = [r[0] for r in rows] + list(params) + list(douts)
    if nadd:
        in_specs.append(pl.BlockSpec((tm, add_to.shape[1]), lambda i: (i, 0)))
        args.append(add_to)
    out_specs = ([pl.BlockSpec((tm, rows[idx][1]), lambda i: (i, 0)) for idx, _ in want_rows]
                 + [_full_spec(params[idx]) for idx in want_params])
    out_shape = ([jax.ShapeDtypeStruct((s, rows[idx][1]), dt) for idx, dt in want_rows]
                 + [jax.ShapeDtypeStruct(params[idx].shape, F32) for idx in want_params])
    res = pl.pallas_call(
        body, name=name, grid=(s // tm,), in_specs=in_specs, out_specs=out_specs, out_shape=out_shape,
        compiler_params=pltpu.CompilerParams(dimension_semantics=("arbitrary",), vmem_limit_bytes=VMEM_LIMIT),
    )(*args)
    return res[:len(want_rows)], res[len(want_rows):]


def f_lnmod(x, sc, sh):
    return (_ln(x) * (1.0 + sc) + sh,)


def f_merge(p, ao, cv, zg0, zg1, zg2, zg3, x, wbd, ps, wbp, wba, wbc, cb, clg, clb, bg, wo, gm, lg, lb):
    zg = jnp.concatenate([zg0, zg1, zg2, zg3], axis=1)
    pm = mm_w(p, wbd) * ps
    co = jax.nn.silu(_ln(cv + cb) * clg + clb)
    y_pool = mm_w(pm, wbp)
    y_attn = mm_w(ao, wba)
    y_conv = mm_w(co, wbc)
    gates = jax.nn.sigmoid(zg + bg)
    merged = (gates[:, :D_MODEL] * y_pool + gates[:, D_MODEL:2 * D_MODEL] * y_attn
              + gates[:, 2 * D_MODEL:] * y_conv)
    mix = mm_w(merged, wo)
    return (_ln(ALPHA * x + gm * mix) * lg + lb,)


def f_relu2(hpre, b1):
    a = jax.nn.relu(hpre + b1)
    return (a * a,)


def f_ffout(x1, ff, b2, gf, lg, lb):
    return (_ln(ALPHA * x1 + gf * (ff + b2)) * lg + lb,)


def mm_big(name, a, b, kind, tiles, out_dtype, j_outer=False, after=None):
    if kind == "nn":
        o0, o1, red = a.shape[0], b.shape[1], a.shape[1]
    elif kind == "nt":
        o0, o1, red = a.shape[0], b.shape[0], a.shape[1]
    else:
        o0, o1, red = a.shape[1], b.shape[1], a.shape[0]
    t0, t1, tr = min(tiles[0], o0), min(tiles[1], o1), min(tiles[2], red)
    if kind == "nn":
        a_spec = pl.BlockSpec((t0, tr), lambda i, j, r: (i, r))
        b_spec = pl.BlockSpec((tr, t1), lambda i, j, r: (r, j))
        dims = (1, 0)
    elif kind == "nt":
        a_spec = pl.BlockSpec((t0, tr), lambda i, j, r: (i, r))
        b_spec = pl.BlockSpec((t1, tr), lambda i, j, r: (j, r))
        dims = (1, 1)
    else:
        a_spec = pl.BlockSpec((tr, t0), lambda i, j, r: (r, i))
        b_spec = pl.BlockSpec((tr, t1), lambda i, j, r: (r, j))
        dims = (0, 0)
    assert o0 % t0 == 0 and o1 % t1 == 0 and red % tr == 0, (name, a.shape, b.shape, tiles)
    n0, n1, nred = o0 // t0, o1 // t1, red // tr
    o_spec = pl.BlockSpec((t0, t1), lambda i, j, r: (i, j))
    if j_outer:
        swap = lambda spec: pl.BlockSpec(spec.block_shape, lambda j, i, r, f=spec.index_map: f(i, j, r))
        a_spec, b_spec, o_spec = swap(a_spec), swap(b_spec), swap(o_spec)
        grid = (n1, n0, nred)
    else:
        grid = (n0, n1, nred)

    deps = [] if after is None else [after]
    dep_specs = [pl.BlockSpec(d.shape, lambda i, j, r, nd=d.ndim: (0,) * nd) for d in deps]
    if nred == 1:
        def body(a_ref, b_ref, *rest):
            o_ref = rest[len(deps)]
            o_ref[...] = _dg(a_ref[...], b_ref[...], *dims).astype(o_ref.dtype)
        scratch = []
    else:
        def body(a_ref, b_ref, *rest):
            o_ref, acc_ref = rest[len(deps):]
            r = pl.program_id(2)
            part = _dg(a_ref[...], b_ref[...], *dims)

            @pl.when(r == 0)
            def _():
                acc_ref[...] = part

            @pl.when(jnp.logical_and(r > 0, r < nred - 1))
            def _():
                acc_ref[...] += part

            @pl.when(r == nred - 1)
            def _():
                o_ref[...] = (acc_ref[...] + part).astype(o_ref.dtype)
        scratch = [pltpu.VMEM((t0, t1), F32)]

    return pl.pallas_call(
        body, name=name, grid=grid,
        in_specs=[a_spec, b_spec] + dep_specs,
        out_specs=o_spec,
        out_shape=jax.ShapeDtypeStruct((o0, o1), out_dtype),
        scratch_shapes=scratch,
        compiler_params=pltpu.CompilerParams(dimension_semantics=("parallel", "parallel", "arbitrary"),
                                             vmem_limit_bytes=VMEM_LIMIT),
    )(a, b, *deps)


def mm_ff1_relu2(name, u2, w1, b1, tm, tn):
    m, k = u2.shape
    n = w1.shape[1]
    tm, tn = min(tm, m), min(tn, n)

    def body(a_ref, b_ref, bias_ref, hpre_ref, h_ref):
        acc = _dg(a_ref[...], b_ref[...], 1, 0)
        hpre_ref[...] = acc
        h_ref[...] = f_relu2(acc, bias_ref[...])[0].astype(h_ref.dtype)

    out = pl.BlockSpec((tm, tn), lambda j, i: (i, j))
    return pl.pallas_call(
        body, name=name, grid=(n // tn, m // tm),
        in_specs=[pl.BlockSpec((tm, k), lambda j, i: (i, 0)), pl.BlockSpec((k, tn), lambda j, i: (0, j)),
                  pl.BlockSpec((1, tn), lambda j, i: (0, j))],
        out_specs=[out, out],
        out_shape=[jax.ShapeDtypeStruct((m, n), F32), jax.ShapeDtypeStruct((m, n), BF16)],
        compiler_params=pltpu.CompilerParams(dimension_semantics=("parallel", "parallel"), vmem_limit_bytes=VMEM_LIMIT),
    )(u2, w1, b1)


def mm_dh_relu2(name, dff, w2, hpre, b1, tm, tn):
    m, k = dff.shape
    n = w2.shape[0]
    tm, tn = min(tm, m), min(tn, n)

    def body(a_ref, b_ref, hpre_ref, bias_ref, d_ref, db_ref):
        i = pl.program_id(1)
        dh = _dg(a_ref[...], b_ref[...], 1, 1)
        _, vjp = jax.vjp(f_relu2, hpre_ref[...], bias_ref[...])
        dhpre, db = vjp((dh,))
        d_ref[...] = dhpre.astype(d_ref.dtype)

        @pl.when(i == 0)
        def _():
            db_ref[...] = db

        @pl.when(i > 0)
        def _():
            db_ref[...] += db

    tile = pl.BlockSpec((tm, tn), lambda j, i: (i, j))
    col = pl.BlockSpec((1, tn), lambda j, i: (0, j))
    return pl.pallas_call(
        body, name=name, grid=(n // tn, m // tm),
        in_specs=[pl.BlockSpec((tm, k), lambda j, i: (i, 0)), pl.BlockSpec((tn, k), lambda j, i: (j, 0)), tile, col],
        out_specs=[tile, col],
        out_shape=[jax.ShapeDtypeStruct((m, n), BF16), jax.ShapeDtypeStruct((1, n), F32)],
        compiler_params=pltpu.CompilerParams(dimension_semantics=("parallel", "arbitrary"), vmem_limit_bytes=VMEM_LIMIT),
    )(dff, w2, hpre, b1)


POOL_PAD = 16
POOL_ROWS = 256


def pool_lin(name, x, wvec, transpose, out_dtype):
    arr, width, cb = x
    s = arr.shape[0]
    n_steps = s // POOL_ROWS

    def body(x_ref, w_ref, o_ref, xp_ref):
        wv = w_ref[...]
        zeros = jnp.zeros((POOL_PAD, width), F32)
        xp_ref[0:POOL_PAD, :] = zeros
        xp_ref[s + POOL_PAD:s + 2 * POOL_PAD, :] = zeros

        def count(t0):
            t = lax.broadcasted_iota(jnp.int32, (POOL_ROWS, width), 0) + (t0 + 1)
            return jnp.minimum(t.astype(F32), wv)

        def fill(i, carry):
            t0 = pl.multiple_of(i * POOL_ROWS, POOL_ROWS)
            v = x_ref[pl.ds(t0, POOL_ROWS), :].astype(F32)
            if transpose:
                v = v / count(t0)
            xp_ref[pl.ds(t0 + POOL_PAD, POOL_ROWS), :] = v
            return carry

        lax.fori_loop(0, n_steps, fill, 0)

        def step(i, carry):
            t0 = pl.multiple_of(i * POOL_ROWS, POOL_ROWS)
            win = xp_ref[pl.ds(t0, POOL_ROWS + 2 * POOL_PAD), :]
            acc = jnp.zeros((POOL_ROWS, width), F32)
            for j in range(POOL_PAD):
                off = POOL_PAD + j if transpose else POOL_PAD - j
                acc = acc + jnp.where(wv > j, win[off:off + POOL_ROWS, :], 0.0)
            cur = x_ref[pl.ds(t0, POOL_ROWS), :].astype(F32)
            res = acc - cur if transpose else acc / count(t0) - cur
            o_ref[pl.ds(t0, POOL_ROWS), :] = res.astype(o_ref.dtype)
            return carry

        lax.fori_loop(0, n_steps, step, 0)

    return pl.pallas_call(
        body, name=name, grid=(1,),
        in_specs=[pl.BlockSpec((s, width), lambda i, cb=cb: (0, cb)), pl.BlockSpec((1, width), lambda i: (0, 0))],
        out_specs=pl.BlockSpec((s, width), lambda i: (0, 0)),
        out_shape=jax.ShapeDtypeStruct((s, width), out_dtype),
        scratch_shapes=[pltpu.VMEM((s + 2 * POOL_PAD, width), F32)],
        compiler_params=pltpu.CompilerParams(dimension_semantics=("arbitrary",), vmem_limit_bytes=VMEM_LIMIT),
    )(arr, wvec)


CONV_PAD = 32
CONV_ROWS = 128


def _glu(a, g):
    return a * jax.nn.sigmoid(g)


def conv_fwd(name, zc, w):
    arr, width, cb = zc
    s = arr.shape[0]
    n_steps = s // CONV_ROWS
    lead = CONV_PAD - (CONV_WIDTH - 1)

    def body(a_ref, g_ref, w_ref, o_ref, hp_ref):
        hp_ref[0:CONV_PAD, :] = jnp.zeros((CONV_PAD, D_CONV), F32)

        def fill(i, carry):
            t0 = pl.multiple_of(i * CONV_ROWS, CONV_ROWS)
            rows = pl.ds(t0, CONV_ROWS)
            hp_ref[pl.ds(t0 + CONV_PAD, CONV_ROWS), :] = _glu(a_ref[rows, :], g_ref[rows, :])
            return carry

        lax.fori_loop(0, n_steps, fill, 0)
        wv = w_ref[...]

        def step(i, carry):
            t0 = pl.multiple_of(i * CONV_ROWS, CONV_ROWS)
            win = hp_ref[pl.ds(t0, CONV_ROWS + CONV_PAD), :]
            acc = jnp.zeros((CONV_ROWS, D_CONV), F32)
            for k in range(CONV_WIDTH):
                acc = acc + wv[k:k + 1, :] * win[lead + k:lead + k + CONV_ROWS, :]
            o_ref[pl.ds(t0, CONV_ROWS), :] = acc
            return carry

        lax.fori_loop(0, n_steps, step, 0)

    return pl.pallas_call(
        body, name=name, grid=(1,),
        in_specs=[pl.BlockSpec((s, width), lambda i, cb=cb: (0, cb)),
                  pl.BlockSpec((s, width), lambda i, cb=cb: (0, cb + 1)), pl.BlockSpec(w.shape, lambda i: (0, 0))],
        out_specs=pl.BlockSpec((s, D_CONV), lambda i: (0, 0)),
        out_shape=jax.ShapeDtypeStruct((s, D_CONV), F32),
        scratch_shapes=[pltpu.VMEM((s + CONV_PAD, D_CONV), F32)],
        compiler_params=pltpu.CompilerParams(dimension_semantics=("arbitrary",), vmem_limit_bytes=VMEM_LIMIT),
    )(arr, arr, w)


def conv_bwd(name, zc, w, dout):
    arr, width, cb = zc
    s = arr.shape[0]
    n_steps = s // CONV_ROWS
    lead = CONV_PAD - (CONV_WIDTH - 1)

    def body(a_ref, g_ref, w_ref, d_ref, dz_ref, dw_ref, hp_ref, dp_ref):
        hp_ref[0:CONV_PAD, :] = jnp.zeros((CONV_PAD, D_CONV), F32)
        dp_ref[s:s + CONV_PAD, :] = jnp.zeros((CONV_PAD, D_CONV), F32)
        dw_ref[...] = jnp.zeros(dw_ref.shape, F32)

        def fill(i, carry):
            t0 = pl.multiple_of(i * CONV_ROWS, CONV_ROWS)
            rows = pl.ds(t0, CONV_ROWS)
            hp_ref[pl.ds(t0 + CONV_PAD, CONV_ROWS), :] = _glu(a_ref[rows, :], g_ref[rows, :])
            dp_ref[rows, :] = d_ref[rows, :]
            return carry

        lax.fori_loop(0, n_steps, fill, 0)
        wv = w_ref[...]

        def step(i, carry):
            t0 = pl.multiple_of(i * CONV_ROWS, CONV_ROWS)
            hwin = hp_ref[pl.ds(t0, CONV_ROWS + CONV_PAD), :]
            dwin = dp_ref[pl.ds(t0, CONV_ROWS + CONV_PAD), :]
            dcur = dwin[0:CONV_ROWS, :]
            dh = jnp.zeros((CONV_ROWS, D_CONV), F32)
            rows = []
            for k in range(CONV_WIDTH):
                rows.append(jnp.sum(dcur * hwin[lead + k:lead + k + CONV_ROWS, :], axis=0, keepdims=True))
                back = CONV_WIDTH - 1 - k
                dh = dh + wv[k:k + 1, :] * dwin[back:back + CONV_ROWS, :]
            rows.append(jnp.zeros((1, D_CONV), F32))
            dw_ref[...] += jnp.concatenate(rows, axis=0)
            rows_now = pl.ds(t0, CONV_ROWS)
            _, vjp = jax.vjp(_glu, a_ref[rows_now, :], g_ref[rows_now, :])
            da, dg = vjp(dh)
            dz_ref[pl.ds(t0, CONV_ROWS), :] = jnp.concatenate([da, dg], axis=1).astype(dz_ref.dtype)
            return carry

        lax.fori_loop(0, n_steps, step, 0)

    return pl.pallas_call(
        body, name=name, grid=(1,),
        in_specs=[pl.BlockSpec((s, width), lambda i, cb=cb: (0, cb)),
                  pl.BlockSpec((s, width), lambda i, cb=cb: (0, cb + 1)),
                  pl.BlockSpec(w.shape, lambda i: (0, 0)), pl.BlockSpec((s, D_CONV), lambda i: (0, 0))],
        out_specs=[pl.BlockSpec((s, 2 * width), lambda i: (0, 0)), pl.BlockSpec(w.shape, lambda i: (0, 0))],
        out_shape=[jax.ShapeDtypeStruct((s, 2 * width), BF16), jax.ShapeDtypeStruct(w.shape, F32)],
        scratch_shapes=[pltpu.VMEM((s + CONV_PAD, D_CONV), F32), pltpu.VMEM((s + CONV_PAD, D_CONV), F32)],
        compiler_params=pltpu.CompilerParams(dimension_semantics=("arbitrary",), vmem_limit_bytes=VMEM_LIMIT),
    )(arr, arr, w, dout)


HEADS_PER_STEP = LANE // HEAD_DIM
CHUNKS_PER_TILE = ATT_TILE // CHUNK
KEY_BLOCKS = N_PREV * CHUNK // ATT_TILE + 1
KEY_SPAN = KEY_BLOCKS * ATT_TILE


def _attn_tile(q, *rest, missing_cols):
    kcat = jnp.concatenate(rest[:KEY_BLOCKS], axis=0)
    vcat = jnp.concatenate(rest[KEY_BLOCKS:2 * KEY_BLOCKS], axis=0)
    bias = rest[2 * KEY_BLOCKS]
    lane = lax.broadcasted_iota(jnp.int32, (1, LANE), 1)
    col = lax.broadcasted_iota(jnp.int32, (1, KEY_SPAN), 1)
    missing = col < missing_cols
    qs = q * (HEAD_DIM ** -0.5)
    o = jnp.zeros((ATT_TILE, LANE), F32)
    for h in range(HEADS_PER_STEP):
        in_head = jnp.logical_and(lane >= h * HEAD_DIM, lane < (h + 1) * HEAD_DIM)
        sc = mm_nt(jnp.where(in_head, qs, 0.0), kcat) + bias[h]
        sc = jnp.where(missing, NEG_INF, sc)
        m = jnp.max(sc, axis=-1, keepdims=True)
        e = jnp.exp(sc - lax.stop_gradient(m))
        p = e / jnp.sum(e, axis=-1, keepdims=True)
        o = o + jnp.where(in_head, mm_nn(p, vcat), 0.0)
    return o


def _missing_cols(n):
    return jnp.maximum((KEY_BLOCKS - 1 - n) * ATT_TILE, 0)


def _attn_in_specs(nt):
    def spec(col0, back):
        return pl.BlockSpec((ATT_TILE, LANE),
                            lambda hp, n, col0=col0, back=back: (jnp.clip(n - back, 0, nt - 1), col0 // LANE + hp))
    backs = list(range(KEY_BLOCKS - 1, -1, -1))
    return ([spec(Z_Q, 0)] + [spec(Z_K, b) for b in backs] + [spec(Z_V, b) for b in backs]
            + [pl.BlockSpec((HEADS_PER_STEP, ATT_TILE, KEY_SPAN), lambda hp, n: (hp, 0, 0))])


def attn_fwd(name, z, bias):
    s = z.shape[0]
    nt = s // ATT_TILE
    n_in = 2 + 2 * KEY_BLOCKS

    def body(*refs):
        o_ref = refs[n_in]
        vals = [r[...] for r in refs[:n_in]]
        o = _attn_tile(*vals, missing_cols=_missing_cols(pl.program_id(1)))
        o_ref[...] = o.astype(o_ref.dtype)

    return pl.pallas_call(
        body, name=name, grid=(N_HEADS // HEADS_PER_STEP, nt),
        in_specs=_attn_in_specs(nt),
        out_specs=pl.BlockSpec((ATT_TILE, LANE), lambda hp, n: (n, hp)),
        out_shape=jax.ShapeDtypeStruct((s, D_ATTN), BF16),
        compiler_params=pltpu.CompilerParams(dimension_semantics=("parallel", "parallel"), vmem_limit_bytes=VMEM_LIMIT),
    )(*([z] * (n_in - 1)), bias)


def attn_bwd(name, z, bias, do, after=None):
    s = z.shape[0]
    nt = s // ATT_TILE
    n_in = 2 + 2 * KEY_BLOCKS
    nc = KEY_BLOCKS - 1

    deps = [] if after is None else [after]
    n_out = n_in + 1 + len(deps)

    def body(*refs):
        do_ref = refs[n_in]
        dq_ref, dk_ref, dv_ref, db_ref = refs[n_out:n_out + 4]
        kacc, vacc = refs[n_out + 4:n_out + 4 + nc], refs[n_out + 4 + nc:]
        n = pl.program_id(1)

        @pl.when(n == 0)
        def _():
            db_ref[...] = jnp.zeros(db_ref.shape, F32)
            for acc in (*kacc, *vacc):
                acc[...] = jnp.zeros(acc.shape, F32)

        def shift(out_ref, accs, contrib):
            @pl.when(n >= nc)
            def _():
                first = accs[0][...] if contrib is None else accs[0][...] + contrib[0]
                out_ref[...] = first.astype(out_ref.dtype)
            for j in range(nc - 1):
                accs[j][...] = accs[j + 1][...] if contrib is None else accs[j + 1][...] + contrib[j + 1]
            if contrib is not None:
                accs[nc - 1][...] = contrib[nc]

        @pl.when(n < nt)
        def _():
            fn = functools.partial(_attn_tile, missing_cols=_missing_cols(n))
            _, vjp = jax.vjp(fn, *[r[...] for r in refs[:n_in]])
            grads = vjp(do_ref[...].astype(F32))
            dq_ref[...] = grads[0].astype(dq_ref.dtype)
            db_ref[...] += grads[n_in - 1]
            shift(dk_ref, kacc, grads[1:1 + KEY_BLOCKS])
            shift(dv_ref, vacc, grads[1 + KEY_BLOCKS:1 + 2 * KEY_BLOCKS])

        @pl.when(n >= nt)
        def _():
            shift(dk_ref, kacc, None)
            shift(dv_ref, vacc, None)

    o_cur = pl.BlockSpec((ATT_TILE, LANE), lambda hp, n: (jnp.minimum(n, nt - 1), hp))
    o_old = pl.BlockSpec((ATT_TILE, LANE), lambda hp, n: (jnp.maximum(n - nc, 0), hp))
    b_spec = pl.BlockSpec((HEADS_PER_STEP, ATT_TILE, KEY_SPAN), lambda hp, n: (hp, 0, 0))
    return pl.pallas_call(
        body, name=name, grid=(N_HEADS // HEADS_PER_STEP, nt + nc),
        in_specs=_attn_in_specs(nt) + [o_cur] + [pl.BlockSpec(d.shape, lambda hp, n: (0, 0)) for d in deps],
        out_specs=[o_cur, o_old, o_old, b_spec],
        out_shape=[jax.ShapeDtypeStruct((s, D_ATTN), BF16)] * 3 + [jax.ShapeDtypeStruct((N_HEADS, ATT_TILE, KEY_SPAN), F32)],
        scratch_shapes=[pltpu.VMEM((ATT_TILE, LANE), F32)] * (2 * nc),
        compiler_params=pltpu.CompilerParams(dimension_semantics=("parallel", "arbitrary"), vmem_limit_bytes=VMEM_LIMIT),
    )(*([z] * (n_in - 1)), bias, do, *deps)


def loss_head(name, y, tgt, tm):
    s, d = y.shape

    def body(y_ref, t_ref, l_ref, dy_ref):
        i = pl.program_id(0)
        diff = y_ref[...] - t_ref[...]
        dy_ref[...] = diff * (1.0 / d)
        part = 0.5 * jnp.sum(jnp.mean(diff * diff, axis=-1, keepdims=True), axis=0, keepdims=True)

        @pl.when(i == 0)
        def _():
            l_ref[...] = jnp.zeros(l_ref.shape, F32)

        l_ref[...] += jnp.broadcast_to(part, l_ref.shape)

    row = pl.BlockSpec((tm, d), lambda i: (i, 0))
    return pl.pallas_call(
        body, name=name, grid=(s // tm,), in_specs=[row, row],
        out_specs=[pl.BlockSpec((8, LANE), lambda i: (0, 0)), row],
        out_shape=[jax.ShapeDtypeStruct((8, LANE), F32), jax.ShapeDtypeStruct((s, d), F32)],
        compiler_params=pltpu.CompilerParams(dimension_semantics=("arbitrary",), vmem_limit_bytes=VMEM_LIMIT),
    )(y, tgt)


def adamw(name, w, g, m, v, tr):
    r, c = w.shape
    assert r % tr == 0, (name, w.shape, tr)

    def body(w_ref, g_ref, m_ref, v_ref, d_ref, nm_ref, nv_ref):
        gg = g_ref[...]
        m2 = ADAM_B1 * m_ref[...] + (1.0 - ADAM_B1) * gg
        v2 = ADAM_B2 * v_ref[...] + (1.0 - ADAM_B2) * (gg * gg)
        m_hat = m2 / (1.0 - ADAM_B1 ** ADAM_STEP)
        v_hat = v2 / (1.0 - ADAM_B2 ** ADAM_STEP)
        d_ref[...] = -ADAM_LR * (m_hat / (jnp.sqrt(v_hat) + ADAM_EPS) + ADAM_WD * w_ref[...])
        nm_ref[...] = m2
        nv_ref[...] = v2

    blk = pl.BlockSpec((tr, c), lambda i: (i, 0))
    return pl.pallas_call(
        body, name=name, grid=(r // tr,), in_specs=[blk] * 4, out_specs=[blk] * 3,
        out_shape=[jax.ShapeDtypeStruct((r, c), F32)] * 3,
        compiler_params=pltpu.CompilerParams(dimension_semantics=("parallel",), vmem_limit_bytes=VMEM_LIMIT),
    )(w, g, m, v)


def adamw_sum(name, w, layer_blocks, m, v, tr):
    rows, c = w.shape
    nl = len(layer_blocks)
    nb, r, _ = layer_blocks[0].shape
    assert rows == nl * r and r % tr == 0, (name, w.shape, layer_blocks[0].shape, tr)
    per = r // tr

    def body(*refs):
        w_ref, b_refs, (m_ref, v_ref, g_ref, d_ref, nm_ref, nv_ref) = refs[0], refs[1:1 + nl], refs[1 + nl:]
        i = pl.program_id(0)

        def update(b_ref):
            gg = b_ref[0].astype(F32)
            for j in range(1, nb):
                gg = gg + b_ref[j].astype(F32)
            g_ref[...] = gg
            m2 = ADAM_B1 * m_ref[...] + (1.0 - ADAM_B1) * gg
            v2 = ADAM_B2 * v_ref[...] + (1.0 - ADAM_B2) * (gg * gg)
            m_hat = m2 / (1.0 - ADAM_B1 ** ADAM_STEP)
            v_hat = v2 / (1.0 - ADAM_B2 ** ADAM_STEP)
            d_ref[...] = -ADAM_LR * (m_hat / (jnp.sqrt(v_hat) + ADAM_EPS) + ADAM_WD * w_ref[...])
            nm_ref[...] = m2
            nv_ref[...] = v2

        for l in range(nl):
            pl.when(jnp.logical_and(i >= l * per, i < (l + 1) * per))(functools.partial(update, b_refs[l]))

    blk = pl.BlockSpec((tr, c), lambda i: (i, 0))
    b_specs = [pl.BlockSpec((nb, tr, c), lambda i, l=l: (0, jnp.clip(i - l * per, 0, per - 1), 0)) for l in range(nl)]
    return pl.pallas_call(
        body, name=name, grid=(rows // tr,),
        in_specs=[blk] + b_specs + [blk, blk], out_specs=[blk] * 4,
        out_shape=[jax.ShapeDtypeStruct((rows, c), F32)] * 4,
        compiler_params=pltpu.CompilerParams(dimension_semantics=("arbitrary",), vmem_limit_bytes=VMEM_LIMIT),
    )(w, *layer_blocks, m, v)


def sum_blocks(name, blocks, tr):
    nb, r, c = blocks.shape
    assert r % tr == 0, (name, blocks.shape, tr)

    def body(b_ref, o_ref):
        acc = b_ref[0].astype(F32)
        for j in range(1, nb):
            acc = acc + b_ref[j].astype(F32)
        o_ref[...] = acc

    return pl.pallas_call(
        body, name=name, grid=(r // tr,),
        in_specs=[pl.BlockSpec((nb, tr, c), lambda i: (0, i, 0))],
        out_specs=pl.BlockSpec((tr, c), lambda i: (i, 0)),
        out_shape=jax.ShapeDtypeStruct((r, c), F32),
        compiler_params=pltpu.CompilerParams(dimension_semantics=("parallel",), vmem_limit_bytes=VMEM_LIMIT),
    )(blocks)


FLIPS = [(0, 0, 1), (1, 0, 0), (0, 1, 0), (1, 1, 0), (1, 0, 1), (0, 1, 1), (1, 1, 1)]
ANY = pl.BlockSpec(memory_space=pl.ANY)


def _me():
    return lax.axis_index("x"), lax.axis_index("y"), lax.axis_index("c")


def _flip(pos, f):
    return tuple((1 - p) if fi else p for p, fi in zip(pos, f))


def _idx(pos):
    return 4 * pos[0] + 2 * pos[1] + pos[2]


def all_gather_multi(name, shards):
    n = len(shards)

    def body(*refs):
        x_refs, out_refs, token = refs[:n], refs[n:2 * n], refs[2 * n]
        send_sems, recv_sems, local_sems = refs[2 * n + 1:]
        token[...] = jnp.zeros_like(token)
        x, y, cc = _me()
        me, sibling = (x, y, cc), (x, y, 1 - cc)
        chips = [(1 - x, y), (x, 1 - y), (1 - x, 1 - y)]

        def copy(a, k, block, to, src=None):
            dst = out_refs[a].at[_idx(block)]
            return pltpu.make_async_remote_copy(
                src_ref=dst if src is None else src, dst_ref=dst, send_sem=send_sems.at[7 * a + k],
                recv_sem=recv_sems.at[7 * a + k], device_id=to, device_id_type=MESH)

        mine = [pltpu.make_async_copy(x_refs[a], out_refs[a].at[_idx(me)], local_sems.at[a]) for a in range(n)]
        for cp in mine:
            cp.start()
        first = []
        for a in range(n):
            first.append(copy(a, 0, me, sibling, src=x_refs[a]))
            first += [copy(a, 1 + j, me, (*chip, cc), src=x_refs[a]) for j, chip in enumerate(chips)]
        for cp in first:
            cp.start()
        passed = []
        for j, chip in enumerate(chips):
            for a in range(n):
                copy(a, 1 + j, (*chip, cc), me).wait_recv()
                fwd = copy(a, 4 + j, (*chip, cc), sibling)
                fwd.start()
                passed.append(fwd)
        for a in range(n):
            copy(a, 0, sibling, me).wait_recv()
            for j, chip in enumerate(chips):
                copy(a, 4 + j, (*chip, 1 - cc), me).wait_recv()
        for cp in first + passed:
            cp.wait_send()
        for cp in mine:
            cp.wait()

    return pl.pallas_call(
        body, name=name, in_specs=[ANY] * n, out_specs=[ANY] * n + [pl.BlockSpec(memory_space=pltpu.VMEM)],
        out_shape=[jax.ShapeDtypeStruct((N_DEV,) + a.shape, a.dtype) for a in shards]
        + [jax.ShapeDtypeStruct((8, LANE), F32)],
        scratch_shapes=[pltpu.SemaphoreType.DMA((7 * n,)), pltpu.SemaphoreType.DMA((7 * n,)),
                        pltpu.SemaphoreType.DMA((n,))],
    )(*shards)


HBM = pl.BlockSpec(memory_space=pltpu.HBM)
SEM = pl.BlockSpec(memory_space=pltpu.SEMAPHORE)
DATAFLOW = pltpu.SideEffectType.DATAFLOW_SIDE_EFFECTING


def _exchange_copies(a_refs, l_refs, send_sems, recv_sems, gather):
    me = _me()
    mi = _idx(me)
    out = []
    for k, f in enumerate(FLIPS):
        peer = _flip(me, f)
        for a in range(len(a_refs)):
            src = a_refs[a] if gather else a_refs[a].at[_idx(peer)]
            out.append(pltpu.make_async_remote_copy(
                src_ref=src, dst_ref=l_refs[a].at[mi], send_sem=send_sems.at[7 * a + k],
                recv_sem=recv_sems.at[7 * a + k], device_id=peer, device_id_type=MESH))
    return out


def exchange_start(name, arrays, gather):
    n = len(arrays)
    lands = [lax.empty(((N_DEV,) + a.shape) if gather else a.shape, a.dtype) for a in arrays]

    def body(*refs):
        a_refs, l_refs = refs[:n], refs[n:2 * n]
        send_sems, recv_sems = refs[2 * n], refs[2 * n + 1]
        token = refs[4 * n + 2]
        for cp in _exchange_copies(a_refs, l_refs, send_sems, recv_sems, gather):
            cp.start()
        token[...] = jnp.zeros_like(token)

    hbm = lambda a: pltpu.HBM(a.shape, a.dtype)
    res = pl.pallas_call(
        body, name=name,
        out_shape=(pltpu.SemaphoreType.DMA((7 * n,)), pltpu.SemaphoreType.DMA((7 * n,)),
                   *[hbm(a) for a in arrays], *[hbm(a) for a in lands], jax.ShapeDtypeStruct((8, LANE), F32)),
        in_specs=[HBM] * (2 * n),
        out_specs=(SEM, SEM, *[HBM] * (2 * n), pl.BlockSpec(memory_space=pltpu.VMEM)),
        input_output_aliases={i: i + 2 for i in range(2 * n)},
        compiler_params=pltpu.CompilerParams(has_side_effects=DATAFLOW),
    )(*[pltpu.with_memory_space_constraint(a, pltpu.HBM) for a in arrays],
      *[pltpu.with_memory_space_constraint(a, pltpu.HBM) for a in lands])
    return res[0], res[1], list(res[2:2 + n]), list(res[2 + n:2 + 2 * n]), res[-1]


def exchange_wait(name, send_sems, recv_sems, arrays, lands, after, gather):
    n = len(arrays)
    after = list(after) if isinstance(after, (list, tuple)) else [after]

    def body(*refs):
        a_refs, l_refs = refs[:n], refs[n:2 * n]
        ssem, rsem = refs[2 * n], refs[2 * n + 1]
        for cp in _exchange_copies(a_refs, l_refs, ssem, rsem, gather):
            cp.wait_send()
            cp.wait_recv()

    hbm = lambda a: pltpu.HBM(a.shape, a.dtype)
    res = pl.pallas_call(
        body, name=name,
        out_shape=(*[hbm(a) for a in arrays], *[hbm(a) for a in lands]),
        in_specs=[HBM] * (2 * n) + [SEM, SEM] + [pl.BlockSpec(memory_space=pl.ANY)] * len(after),
        out_specs=tuple([HBM] * (2 * n)),
        input_output_aliases={i: i for i in range(2 * n)},
        compiler_params=pltpu.CompilerParams(has_side_effects=DATAFLOW),
    )(*arrays, *lands, send_sems, recv_sems, *after)
    return list(res[n:])


def ada_fwd(name, c_row, w_cat, b_lay):
    d = c_row.shape[1]
    ncol = w_cat.shape[1]
    vmem = pl.BlockSpec(memory_space=pltpu.VMEM)

    def body(c_ref, w_ref, b_ref, mod_ref, cact_ref, token, call, send, land, s1, r1, s2, r2):
        token[...] = jnp.zeros_like(token)
        me = _me()
        mi = _idx(me)
        call[mi] = c_ref[...]

        def exchange(src_of, dst_buf, ssem, rsem):
            sends, recvs = [], []
            for k, f in enumerate(FLIPS):
                peer = _flip(me, f)
                sends.append(pltpu.make_async_remote_copy(
                    src_ref=src_of(peer), dst_ref=dst_buf.at[mi], send_sem=ssem.at[k], recv_sem=rsem.at[k],
                    device_id=peer, device_id_type=MESH))
                recvs.append(pltpu.make_async_remote_copy(
                    src_ref=src_of(peer), dst_ref=dst_buf.at[_idx(peer)], send_sem=ssem.at[k], recv_sem=rsem.at[k],
                    device_id=peer, device_id_type=MESH))
            for cp in sends:
                cp.start()
            for cp in recvs:
                cp.wait_recv()
            for cp in sends:
                cp.wait_send()

        exchange(lambda peer: c_ref, call, s1, r1)
        for p in range(N_DEV):
            cact_ref[pl.ds(p, 1), :] = jax.nn.silu(call[p])
        res = _dg(cact_ref[...], w_ref[...], 1, 0)
        for p in range(N_DEV):
            send[p] = res[p:p + 1, :]
        land[mi] = send[mi]
        exchange(lambda peer: send.at[_idx(peer)], land, s2, r2)
        mod_ref[...] = land[...] + b_ref[...]

    return pl.pallas_call(
        body, name=name, in_specs=[vmem, vmem, vmem], out_specs=[vmem, vmem, vmem],
        out_shape=[jax.ShapeDtypeStruct((N_DEV, 1, ncol), F32), jax.ShapeDtypeStruct((N_DEV, d), F32),
                   jax.ShapeDtypeStruct((8, LANE), F32)],
        scratch_shapes=[pltpu.VMEM((N_DEV, 1, d), F32), pltpu.VMEM((N_DEV, 1, ncol), F32),
                        pltpu.VMEM((N_DEV, 1, ncol), F32),
                        pltpu.SemaphoreType.DMA((7,)), pltpu.SemaphoreType.DMA((7,)),
                        pltpu.SemaphoreType.DMA((7,)), pltpu.SemaphoreType.DMA((7,))],
        compiler_params=pltpu.CompilerParams(vmem_limit_bytes=VMEM_LIMIT),
    )(c_row, w_cat, b_lay)


POOL_WINDOWS = (2, 4, 8, 16)
POOL_GROUP = 64
N_REL = 2 * REL_CLIP + 1
SMALL_NAMES = ["b_ada", "b_gate", "w_pool", "pool_scale", "rel_bias", "conv_w", "conv_b", "conv_ln_g",
               "conv_ln_b", "ln_mix_g", "ln_mix_b", "b_ff1", "b_ff2", "ln_ff_g", "ln_ff_b"]
BIG_NAMES = ["w_in", "w_br_pool", "w_br_attn", "w_br_conv", "w_o", "w_ff1", "w_ff2"]
ROW_SHARDED = ("w_in", "w_o", "w_ff2")
WEIGHT_NAMES = ["w_ada", "b_ada", "w_in", "b_gate", "w_pool", "pool_scale", "rel_bias", "conv_w", "conv_b",
                "conv_ln_g", "conv_ln_b", "w_br_pool", "w_br_attn", "w_br_conv", "w_o", "ln_mix_g", "ln_mix_b",
                "w_ff1", "b_ff1", "w_ff2", "b_ff2", "ln_ff_g", "ln_ff_b"]


def _bias_table(rel_bias):
    far = jnp.broadcast_to(rel_bias[:, 2 * REL_CLIP:], (N_HEADS, BAND - REL_CLIP))
    near = rel_bias[:, REL_CLIP - CHUNK + 1:2 * REL_CLIP][:, ::-1]
    ext = jnp.concatenate([far, near, jnp.zeros((N_HEADS, 1), F32)], axis=1)
    length = BAND + CHUNK
    flat = jnp.tile(ext, (1, CHUNK + 1))
    skew = flat[:, CHUNK - 1:CHUNK - 1 + CHUNK * (length - 1)].reshape(N_HEADS, CHUNK, length - 1)
    return skew[:, :, :BAND]


def _bias_full(rel_bias):
    tab = _bias_table(rel_bias)
    return jnp.concatenate(
        [jnp.pad(tab, ((0, 0), (0, 0), (i * CHUNK, KEY_SPAN - BAND - i * CHUNK)), constant_values=NEG_INF)
         for i in range(CHUNKS_PER_TILE)], axis=1)


def _block_diag(w_pool):
    out = jnp.zeros((D_POOL, D_POOL), F32)
    for g in range(len(POOL_WINDOWS)):
        out = lax.dynamic_update_slice(out, w_pool[g], (g * POOL_GROUP, g * POOL_GROUP))
    return out


def _flat_pad(arrs, mult):
    flat = jnp.concatenate([a.reshape(-1) for a in arrs])
    pad = (-flat.shape[0]) % mult
    return jnp.pad(flat, (0, pad)) if pad else flat


def _unflat(flat, shapes):
    out, off = [], 0
    for shp in shapes:
        n = int(np.prod(shp))
        out.append(flat[off:off + n].reshape(shp))
        off += n
    return out


def _to_blocks(name, full):
    k, n = full.shape
    if name in ROW_SHARDED:
        return full.reshape(N_DEV, k // N_DEV, n)
    return full.reshape(k, N_DEV, n // N_DEV).transpose(1, 0, 2)


def _from_blocks(name, blocks):
    nb, r, c = blocks.shape
    if name in ROW_SHARDED:
        return blocks.reshape(nb * r, c)
    return blocks.transpose(1, 0, 2).reshape(r, nb * c)


class _Layer:
    pass


def _row(v):
    return v.reshape(1, -1)


def _layer_fwd(x, modr, w, wvec, fetch_rest):
    sh_m, sc_m, g_m, sh_f, sc_f, g_f = modr
    (u,) = row_fwd("lnmod_mix", f_lnmod, [x], [sc_m, sh_m], [(D_MODEL, BF16)], 512)
    z = mm_big("mm_in", u, w.w_in_t, "nt", (1024, 896, 1024), F32, j_outer=True)
    p = pool_lin("pool_fwd", (z, D_POOL, Z_POOL // D_POOL), wvec, False, F32)
    ao = attn_fwd("attn_fwd", z, w.bias)
    cv = conv_fwd("conv_fwd", (z, D_CONV, Z_CONV // D_CONV), w.conv_w)
    if not hasattr(w, "wo"):
        fetch_rest(w, cv)
    mparams = [w.wbd, w.ps, w.wbp, w.wba, w.wbc, w.cb, w.clg, w.clb, w.bg, w.wo, g_m, w.lmg, w.lmb]
    gate_blocks = [(z, GATE_BLOCK, Z_GATE // GATE_BLOCK + k) for k in range(3 * D_MODEL // GATE_BLOCK)]
    (x1,) = row_fwd("merge", f_merge, [p, ao, cv, *gate_blocks, x], mparams, [(D_MODEL, F32)], 512)
    (u2,) = row_fwd("lnmod_ff", f_lnmod, [x1], [sc_f, sh_f], [(D_MODEL, BF16)], 512)
    if not hasattr(w, "w_ff1"):
        fetch_rest(w, u2)
    hpre, h = mm_ff1_relu2("mm_ff1", u2, w.w_ff1, w.b1, 1024, 1024)
    ff = mm_big("mm_ff2", h, w.w_ff2, "nn", (1024, 1024, 4096), F32)
    (x2,) = row_fwd("ffout", f_ffout, [x1, ff], [w.b2, g_f, w.lfg, w.lfb], [(D_MODEL, F32)], 512)
    return x2, (x, u, z, p, ao, cv, x1, u2, hpre, h, ff, mparams)


GRAD_GROUPS = [("ff", ["w_ff2", "w_ff1"]), ("mix", ["w_o", "w_br_pool", "w_br_attn", "w_br_conv"]), ("in", ["w_in"])]


def _layer_bwd(dx2, saved, modr, w, wvec, ready):
    x, u, z, p, ao, cv, x1, u2, hpre, h, ff, mparams = saved
    sh_m, sc_m, g_m, sh_f, sc_f, g_f = modr
    g = {}
    (dx1a, dff), (g["b_ff2"], dgf, g["ln_ff_g"], g["ln_ff_b"]) = row_bwd(
        "ffout_bwd", f_ffout, [x1, ff], [w.b2, g_f, w.lfg, w.lfb], [dx2], 512, [(0, F32), (1, BF16)], [0, 1, 2, 3])
    dhpre, g["b_ff1"] = mm_dh_relu2("mm_dh", dff, w.w_ff2, hpre, w.b1, 1024, 1024)
    g["w_ff2"] = mm_big("mm_dw_ff2", h, dff, "tn", (1024, 1024, 2048), BF16)
    du2 = mm_big("mm_du2", dhpre, w.w_ff1, "nt", (1024, 1024, 4096), F32)
    g["w_ff1"] = mm_big("mm_dw_ff1", u2, dhpre, "tn", (1024, 1024, 2048), BF16)
    sc_f = sc_f + ready("ff", g)[0:1, 0:1]
    (dx1,), (dscf, dshf) = row_bwd("lnmod_ff_bwd", f_lnmod, [x1], [sc_f, sh_f], [du2], 512, [(0, F32)], [0, 1],
                                   add_to=dx1a)
    gate_blocks = [(z, GATE_BLOCK, Z_GATE // GATE_BLOCK + k) for k in range(3 * D_MODEL // GATE_BLOCK)]
    (dp, dao, dcv, *dzg, dxa), dm = row_bwd(
        "merge_bwd", f_merge, [p, ao, cv, *gate_blocks, x], mparams, [dx1], 256,
        [(0, F32), (1, BF16), (2, F32), (3, BF16), (4, BF16), (5, BF16), (6, BF16), (7, F32)], list(range(13)))
    (dwbd, g["pool_scale"], g["w_br_pool"], g["w_br_attn"], g["w_br_conv"], g["conv_b"], g["conv_ln_g"],
     g["conv_ln_b"], g["b_gate"], g["w_o"], dgm, g["ln_mix_g"], g["ln_mix_b"]) = dm
    g["w_pool"] = jnp.stack([dwbd[i * POOL_GROUP:(i + 1) * POOL_GROUP, i * POOL_GROUP:(i + 1) * POOL_GROUP]
                             for i in range(len(POOL_WINDOWS))])
    tok = ready("mix", g)
    dzp = pool_lin("pool_bwd", (dp, D_POOL, 0), wvec + tok[0:1, 0:1], True, BF16)
    dq, dk, dv, dbias = attn_bwd("attn_bwd", z, w.bias, dao, after=tok)
    (g["rel_bias"],) = w.bias_vjp(dbias)
    dzc, dcw = conv_bwd("conv_bwd", (z, D_CONV, Z_CONV // D_CONV), w.conv_w + tok[0:1, 0:1], dcv)
    g["conv_w"] = dcw[:CONV_WIDTH]
    dz = jnp.concatenate([dzp, dq, dk, dv, dzc, *dzg], axis=1)
    du = mm_big("mm_du", dz, w.w_in_t, "nn", (512, 1024, D_IN), F32)
    (dx,), (dscm, dshm) = row_bwd("lnmod_mix_bwd", f_lnmod, [x], [sc_m, sh_m], [du], 512, [(0, F32)], [0, 1],
                                  add_to=dxa)
    g["dmod"] = jnp.concatenate([dshm, dscm, dgm, dshf, dscf, dgf], axis=1)
    tok = ready("small", g)
    g["w_in"] = mm_big("mm_dw_in", dz, u, "tn", (896, 1024, 2048), BF16, after=tok)
    return dx, g, ready("in", g)


def kernel(x, c, w_ada, b_ada, w_in, b_gate, w_pool, pool_scale, rel_bias, conv_w, conv_b, conv_ln_g, conv_ln_b, w_br_pool, w_br_attn, w_br_conv, w_o, ln_mix_g, ln_mix_b, w_ff1, b_ff1, w_ff2, b_ff2, ln_ff_g, ln_ff_b, loss_target, m_w_ada, m_b_ada, m_w_in, m_b_gate, m_w_pool, m_pool_scale, m_rel_bias, m_conv_w, m_conv_b, m_conv_ln_g, m_conv_ln_b, m_w_br_pool, m_w_br_attn, m_w_br_conv, m_w_o, m_ln_mix_g, m_ln_mix_b, m_w_ff1, m_b_ff1, m_w_ff2, m_b_ff2, m_ln_ff_g, m_ln_ff_b, v_w_ada, v_b_ada, v_w_in, v_b_gate, v_w_pool, v_pool_scale, v_rel_bias, v_conv_w, v_conv_b, v_conv_ln_g, v_conv_ln_b, v_w_br_pool, v_w_br_attn, v_w_br_conv, v_w_o, v_ln_mix_g, v_ln_mix_b, v_w_ff1, v_b_ff1, v_w_ff2, v_b_ff2, v_ln_ff_g, v_ln_ff_b):
    args = dict(locals())
    wts = {n: args[n] for n in WEIGHT_NAMES}
    mom = {n: args["m_" + n] for n in WEIGHT_NAMES}
    var = {n: args["v_" + n] for n in WEIGHT_NAMES}
    me = 4 * lax.axis_index("x") + 2 * lax.axis_index("y") + lax.axis_index("c")
    xs, tgt = x[0], loss_target[0]
    nc_ada = w_ada.shape[2]
    wvec = jnp.asarray(np.repeat(np.array(POOL_WINDOWS, np.float32), POOL_GROUP)[None, :])

    w_cat = jnp.concatenate([w_ada[l] for l in range(DEPTH)], axis=1)
    b_lay = b_ada.reshape(DEPTH, N_DEV, nc_ada).transpose(1, 0, 2).reshape(N_DEV, 1, DEPTH * nc_ada)
    land, cact, ada_token = ada_fwd("ada_fwd", c, w_cat, b_lay)
    mod = land.reshape(N_DEV, DEPTH, nc_ada).transpose(1, 0, 2).reshape(DEPTH, 6 * D_MODEL)
    modr = [[mod[l:l + 1, i * D_MODEL:(i + 1) * D_MODEL] for i in range(6)] for l in range(DEPTH)]

    cw_pack = _flat_pad([conv_w], 8 * LANE).reshape(-1, LANE) + ada_token[0:1, 0:1]
    xlayout = lambda n, a: jnp.swapaxes(a, -1, -2) if n == "w_in" else a
    shards = [[xlayout(n, wts[n][l]).astype(BF16) for n in BIG_NAMES] for l in range(DEPTH)]
    n_first = [BIG_NAMES.index("w_ff1"), 1]
    *first0, cw_all, token = all_gather_multi("gather_first_l0", shards[0][:n_first[0]] + [cw_pack])
    cw_all = cw_all.reshape(N_DEV, -1)[:, :conv_w.size]
    conv_full = cw_all.reshape((N_DEV,) + conv_w.shape).transpose(1, 2, 0, 3).reshape(DEPTH, CONV_WIDTH, D_CONV)
    gathers = {}
    for key, arrs in [("rest_l0", shards[0][n_first[0]:]), ("first_l1", shards[1][:n_first[1]]),
                      ("rest_l1", shards[1][n_first[1]:])]:
        arrs = [a + token[0:1, 0:1].astype(a.dtype) for a in arrs]
        ssem, rsem, thru, lands, token = exchange_start(f"gather_{key}_start", arrs, True)
        gathers[key] = (arrs, ssem, rsem, thru, lands)
    modr[0] = [r + token[0:1, 0:1] for r in modr[0]]

    def with_own(lands_, own):
        return [lax.dynamic_update_index_in_dim(ld, o, me, axis=0) for ld, o in zip(lands_, own)]

    def gathered(key, after):
        arrs, ssem, rsem, thru, lands = gathers[key]
        return with_own(exchange_wait(f"gather_{key}_wait", ssem, rsem, thru, lands, after, True), arrs)

    attr = dict(w_in="w_in_t", w_br_pool="wbp", w_br_attn="wba", w_br_conv="wbc", w_o="wo", w_ff1="w_ff1", w_ff2="w_ff2")

    def assign(w, names, blocks):
        for n, g in zip(names, blocks):
            setattr(w, attr[n], _from_blocks(n, g))

    def fetch_rest_for(l):
        return lambda w, after: assign(w, BIG_NAMES[n_first[l]:], gathered(f"rest_l{l}", after))

    def layer_weights(l, first_blocks):
        w = _Layer()
        assign(w, BIG_NAMES[:n_first[l]], first_blocks)
        w.wbd = _block_diag(w_pool[l])
        w.ps, w.cb, w.clg, w.clb = _row(pool_scale[l]), _row(conv_b[l]), _row(conv_ln_g[l]), _row(conv_ln_b[l])
        w.bg, w.lmg, w.lmb = _row(b_gate[l]), _row(ln_mix_g[l]), _row(ln_mix_b[l])
        w.b1, w.b2, w.lfg, w.lfb = _row(b_ff1[l]), _row(b_ff2[l]), _row(ln_ff_g[l]), _row(ln_ff_b[l])
        w.conv_w = jnp.pad(conv_full[l], ((0, CONV_PAD - CONV_WIDTH), (0, 0)))
        w.bias, w.bias_vjp = jax.vjp(_bias_full, rel_bias[l])
        return w

    layers, saved = [layer_weights(0, first0)], []
    h, sv = _layer_fwd(xs, modr[0], layers[0], wvec, fetch_rest_for(0))
    saved.append(sv)
    layers.append(layer_weights(1, gathered("first_l1", h)))
    h, sv = _layer_fwd(h, modr[1], layers[1], wvec, fetch_rest_for(1))
    saved.append(sv)
    lpart, dy = loss_head("loss_head", h, tgt, 512)
    loss = lax.psum(lpart[0, 0], ("x", "y", "c"))
    grads = [None] * DEPTH
    pending = {}
    small_shapes = [wts[n].shape if n != "conv_w" else (DEPTH, CONV_WIDTH, D_CONV) for n in SMALL_NAMES]

    def ready_for(l):
        def ready(group, g):
            if group == "small":
                if l > 0:
                    return None
                both = [g, grads[1]]
                local = [jnp.concatenate([both[k]["dmod"] for k in range(DEPTH)], axis=0)]
                local += [jnp.stack([both[k][n].reshape(shp[1:]) for k in range(DEPTH)])
                          for n, shp in zip(SMALL_NAMES[1:], small_shapes[1:])]
                pack = _flat_pad(local, 8 * LANE).reshape(-1, LANE)
                ssem, rsem, thru, lands, token = exchange_start("gather_small_grads_start", [pack], True)
                pending["small"] = (pack, ssem, rsem, thru, lands)
                return token
            names = dict(GRAD_GROUPS)[group]
            blocks = [_to_blocks(n, g[n]).astype(BF16) for n in names]
            ssem, rsem, thru, lands, token = exchange_start(f"scatter_l{l}_{group}_start", blocks, False)
            pending[(l, group)] = (names, blocks, ssem, rsem, thru, lands)
            return token
        return ready

    def received(l, group, after):
        names, blocks, ssem, rsem, thru, lands = pending[(l, group)]
        lands = exchange_wait(f"scatter_l{l}_{group}_wait", ssem, rsem, thru, lands, after, False)
        own = [lax.dynamic_index_in_dim(b, me, axis=0, keepdims=False) for b in blocks]
        return dict(zip(names, with_own(lands, own)))

    dy, grads[1], token = _layer_bwd(dy, saved[1], modr[1], layers[1], wvec, ready_for(1))
    modr0 = [r + token[0:1, 0:1] for r in modr[0]]
    dy, grads[0], token = _layer_bwd(dy, saved[0], modr0, layers[0], wvec, ready_for(0))
    recv = [{} for _ in range(DEPTH)]
    for l, group in [(1, "ff"), (1, "mix"), (1, "in"), (0, "ff"), (0, "mix")]:
        recv[l].update(received(l, group, token))
    grad_x = dy[None]
    dmod_size = DEPTH * 6 * D_MODEL

    small_pack, ssem, rsem, thru, lands = pending["small"]
    (small_all,) = with_own(exchange_wait("gather_small_grads_wait", ssem, rsem, thru, lands, token, True), [small_pack])
    small_sum = sum_blocks("sum_small_grads", small_all, small_pack.shape[0]).reshape(-1)
    gsmall = dict(zip(SMALL_NAMES, _unflat(small_sum, small_shapes)))
    gw = dict(gsmall)
    gw["conv_w"] = lax.dynamic_slice_in_dim(gsmall["conv_w"], me * conv_w.shape[2], conv_w.shape[2], axis=2)

    dmod_all = small_all.reshape(N_DEV, -1)[:, :dmod_size].reshape(N_DEV, DEPTH, N_DEV, nc_ada)
    dm_mine = lax.dynamic_index_in_dim(dmod_all, me, axis=2, keepdims=False).reshape(N_DEV, DEPTH * nc_ada)
    cact_t = jnp.pad(cact.T, ((0, 0), (0, LANE - N_DEV)))
    dm_pad = jnp.pad(dm_mine, ((0, LANE - N_DEV), (0, 0)))
    dw_cat = mm_big("mm_dw_ada", cact_t, dm_pad, "nn", (D_MODEL, DEPTH * nc_ada, LANE), F32)
    gw["w_ada"] = jnp.stack([dw_cat[:, l * nc_ada:(l + 1) * nc_ada] for l in range(DEPTH)])

    delta, new_m, new_v = {}, {}, {}
    packs = [_flat_pad([src[n] for n in SMALL_NAMES], 8 * LANE).reshape(-1, LANE) for src in (wts, gw, mom, var)]
    small_res = adamw("adamw_small", *packs, packs[0].shape[0])
    shapes = [wts[n].shape for n in SMALL_NAMES]
    for out, flat in zip((delta, new_m, new_v), small_res):
        out.update(dict(zip(SMALL_NAMES, _unflat(flat.reshape(-1), shapes))))
    for n in ["w_ada", "w_ff2", "w_ff1", "w_o", "w_br_pool", "w_br_attn", "w_br_conv", "w_in"]:
        shp = xlayout(n, wts[n]).shape
        two_d = lambda a, shp=shp, n=n: xlayout(n, a).reshape(shp[0] * shp[1], shp[2])
        tr = 224 if n == "w_in" else min(256, shp[1])
        if n == "w_ada":
            res = (gw[n],) + tuple(adamw("adamw_" + n, two_d(wts[n]), two_d(gw[n]), two_d(mom[n]), two_d(var[n]), tr))
        else:
            if n == "w_in":
                done = [small_res[2]] + [new_v[k] for k in ["w_ada"] + BIG_NAMES[1:]]
                recv[0].update(received(0, "in", done))
            res = adamw_sum("adamw_" + n, two_d(wts[n]), [recv[l][n] for l in range(DEPTH)], two_d(mom[n]),
                            two_d(var[n]), tr)
        gw[n], delta[n], new_m[n], new_v[n] = [xlayout(n, a.reshape(shp)) for a in res]

    return (loss, grad_x, *[gw[n] for n in WEIGHT_NAMES], *[delta[n] for n in WEIGHT_NAMES],
            *[new_m[n] for n in WEIGHT_NAMES], *[new_v[n] for n in WEIGHT_NAMES])
```

```python
import functools

import jax
import jax.numpy as jnp
import numpy as np
from jax import lax
from jax.experimental import pallas as pl
from jax.experimental.pallas import tpu as pltpu

F32 = jnp.float32
BF16 = jnp.bfloat16
MESH = pl.DeviceIdType.MESH

D_MODEL = 1024
DEPTH = 2
CHUNK = 64
N_HEADS = 8
HEAD_DIM = 64
D_POOL = 256
D_ATTN = 512
D_CONV = 256
CONV_WIDTH = 31
D_FF = 4096
D_IN = 5376
N_PREV = 8
BAND = (N_PREV + 1) * CHUNK
REL_CLIP = 128
ALPHA = (2.0 * DEPTH) ** 0.25
LN_EPS = 1e-5
NEG_INF = -1e30
N_DEV = 8

ADAM_LR, ADAM_B1, ADAM_B2, ADAM_EPS, ADAM_WD, ADAM_STEP = 0.001, 0.9, 0.999, 1e-08, 0.01, 10

VMEM_LIMIT = 56 * 1024 * 1024

Z_POOL, Z_Q, Z_K, Z_V, Z_CONV, Z_GATE = 0, 256, 768, 1280, 1792, 2304
GATE_BLOCK = 768
ATT_TILE = 512
LANE = 128


def _dg(a, b, ca, cb):
    return lax.dot_general(a.astype(BF16), b.astype(BF16), (((ca,), (cb,)), ((), ())),
                           preferred_element_type=F32)


@jax.custom_vjp
def mm_nn(a, b):
    return _dg(a, b, 1, 0)


def _mm_nn_fwd(a, b):
    return _dg(a, b, 1, 0), (a, b)


def _mm_nn_bwd(res, g):
    a, b = res
    return _dg(g, b, 1, 1).astype(a.dtype), _dg(a, g, 0, 0).astype(b.dtype)


mm_nn.defvjp(_mm_nn_fwd, _mm_nn_bwd)


@jax.custom_vjp
def mm_nt(a, b):
    return _dg(a, b, 1, 1)


def _mm_nt_fwd(a, b):
    return _dg(a, b, 1, 1), (a, b)


def _mm_nt_bwd(res, g):
    a, b = res
    return _dg(g, b, 1, 0).astype(a.dtype), _dg(g, a, 0, 0).astype(b.dtype)


mm_nt.defvjp(_mm_nt_fwd, _mm_nt_bwd)


@jax.custom_vjp
def mm_nn_shadow(a, w, shadow):
    return _dg(a, w, 1, 0)


def _mm_nn_shadow_fwd(a, w, shadow):
    return _dg(a, w, 1, 0), (a, w)


def _mm_nn_shadow_bwd(res, g):
    a, w = res
    return _dg(g, w, 1, 1).astype(a.dtype), jnp.zeros_like(w), _dg(a, g, 0, 0)


mm_nn_shadow.defvjp(_mm_nn_shadow_fwd, _mm_nn_shadow_bwd)


def mm_w(a, w):
    return mm_nn_shadow(a, w[0], w[1]) if isinstance(w, tuple) else mm_nn(a, w)


def _ln(x):
    mu = jnp.mean(x, axis=-1, keepdims=True)
    xc = x - mu
    var = jnp.mean(xc * xc, axis=-1, keepdims=True)
    return xc * lax.rsqrt(var + LN_EPS)


def _norm_rows(rows):
    return [r if isinstance(r, tuple) else (r, r.shape[1], 0) for r in rows]


def _row_spec(tm, r):
    _, width, cb = r
    return pl.BlockSpec((tm, width), lambda i, cb=cb: (i, cb))


def _full_spec(a):
    nd = a.ndim
    return pl.BlockSpec(a.shape, lambda i, nd=nd: (0,) * nd)


def row_fwd(name, f, rows, params, outs, tm):
    rows = _norm_rows(rows)
    s = rows[0][0].shape[0]
    nr, npar = len(rows), len(params)

    def body(*refs):
        r = [x[...].astype(F32) for x in refs[:nr]]
        p = [x[...] for x in refs[nr:nr + npar]]
        res = f(*r, *p)
        for o_ref, o in zip(refs[nr + npar:], res):
            o_ref[...] = o.astype(o_ref.dtype)

    return pl.pallas_call(
        body, name=name, grid=(s // tm,),
        in_specs=[_row_spec(tm, r) for r in rows] + [_full_spec(p) for p in params],
        out_specs=[pl.BlockSpec((tm, w), lambda i: (i, 0)) for w, _ in outs],
        out_shape=[jax.ShapeDtypeStruct((s, w), dt) for w, dt in outs],
        compiler_params=pltpu.CompilerParams(dimension_semantics=("parallel",), vmem_limit_bytes=VMEM_LIMIT),
    )(*[r[0] for r in rows], *params)


def row_bwd(name, f, rows, params, douts, tm, want_rows, want_params, add_to=None):
    rows = _norm_rows(rows)
    s = rows[0][0].shape[0]
    nr, npar, nd = len(rows), len(params), len(douts)
    nadd = 0 if add_to is None else 1
    n_in = nr + npar + nd + nadd

    def body(*refs):
        i = pl.program_id(0)
        r = [x[...].astype(F32) for x in refs[:nr]]
        p = [(x[...], jnp.zeros(x.shape, F32)) if x.dtype == BF16 else x[...] for x in refs[nr:nr + npar]]
        d = [x[...].astype(F32) for x in refs[nr + npar:nr + npar + nd]]
        _, vjp = jax.vjp(f, *r, *p)
        g = vjp(tuple(d))
        out_refs = refs[n_in:]
        for k, (idx, _) in enumerate(want_rows):
            val = g[idx]
            if nadd and k == 0:
                val = val + refs[n_in - 1][...].astype(F32)
            out_refs[k][...] = val.astype(out_refs[k].dtype)
        for k, idx in enumerate(want_params):
            gp = g[nr + idx]
            gp = gp[1] if isinstance(gp, tuple) else gp
            o_ref = out_refs[len(want_rows) + k]

            @pl.when(i == 0)
            def _():
                o_ref[...] = gp

            @pl.when(i > 0)
            def _():
                o_ref[...] += gp

    in_specs = ([_row_spec(tm, r) for r in rows] + [_full_spec(p) for p in params]
                + [pl.BlockSpec((tm, d.shape[1]), lambda i: (i, 0)) for d in douts])
    args = [r[0] for r in rows] + list(params) + list(douts)
    if nadd:
        in_specs.append(pl.BlockSpec((tm, add_to.shape[1]), lambda i: (i, 0)))
        args.append(add_to)
    out_specs = ([pl.BlockSpec((tm, rows[idx][1]), lambda i: (i, 0)) for idx, _ in want_rows]
                 + [_full_spec(params[idx]) for idx in want_params])
    out_shape = ([jax.ShapeDtypeStruct((s, rows[idx][1]), dt) for idx, dt in want_rows]
                 + [jax.ShapeDtypeStruct(params[idx].shape, F32) for idx in want_params])
    res = pl.pallas_call(
        body, name=name, grid=(s // tm,), in_specs=in_specs, out_specs=out_specs, out_shape=out_shape,
        compiler_params=pltpu.CompilerParams(dimension_semantics=("arbitrary",), vmem_limit_bytes=VMEM_LIMIT),
    )(*args)
    return res[:len(want_rows)], res[len(want_rows):]


def f_lnmod(x, sc, sh):
    return (_ln(x) * (1.0 + sc) + sh,)


def f_merge(p, ao, cv, zg0, zg1, zg2, zg3, x, wbd, ps, wbp, wba, wbc, cb, clg, clb, bg, wo, gm, lg, lb):
    zg = jnp.concatenate([zg0, zg1, zg2, zg3], axis=1)
    pm = mm_w(p, wbd) * ps
    co = jax.nn.silu(_ln(cv + cb) * clg + clb)
    y_pool = mm_w(pm, wbp)
    y_attn = mm_w(ao, wba)
    y_conv = mm_w(co, wbc)
    gates = jax.nn.sigmoid(zg + bg)
    merged = (gates[:, :D_MODEL] * y_pool + gates[:, D_MODEL:2 * D_MODEL] * y_attn
              + gates[:, 2 * D_MODEL:] * y_conv)
    mix = mm_w(merged, wo)
    return (_ln(ALPHA * x + gm * mix) * lg + lb,)


def f_relu2(hpre, b1):
    a = jax.nn.relu(hpre + b1)
    return (a * a,)


def f_ffout(x1, ff, b2, gf, lg, lb):
    return (_ln(ALPHA * x1 + gf * (ff + b2)) * lg + lb,)


def mm_big(name, a, b, kind, tiles, out_dtype, j_outer=False, after=None):
    if kind == "nn":
        o0, o1, red = a.shape[0], b.shape[1], a.shape[1]
    elif kind == "nt":
        o0, o1, red = a.shape[0], b.shape[0], a.shape[1]
    else:
        o0, o1, red = a.shape[1], b.shape[1], a.shape[0]
    t0, t1, tr = min(tiles[0], o0), min(tiles[1], o1), min(tiles[2], red)
    if kind == "nn":
        a_spec = pl.BlockSpec((t0, tr), lambda i, j, r: (i, r))
        b_spec = pl.BlockSpec((tr, t1), lambda i, j, r: (r, j))
        dims = (1, 0)
    elif kind == "nt":
        a_spec = pl.BlockSpec((t0, tr), lambda i, j, r: (i, r))
        b_spec = pl.BlockSpec((t1, tr), lambda i, j, r: (j, r))
        dims = (1, 1)
    else:
        a_spec = pl.BlockSpec((tr, t0), lambda i, j, r: (r, i))
        b_spec = pl.BlockSpec((tr, t1), lambda i, j, r: (r, j))
        dims = (0, 0)
    assert o0 % t0 == 0 and o1 % t1 == 0 and red % tr == 0, (name, a.shape, b.shape, tiles)
    n0, n1, nred = o0 // t0, o1 // t1, red // tr
    o_spec = pl.BlockSpec((t0, t1), lambda i, j, r: (i, j))
    if j_outer:
        swap = lambda spec: pl.BlockSpec(spec.block_shape, lambda j, i, r, f=spec.index_map: f(i, j, r))
        a_spec, b_spec, o_spec = swap(a_spec), swap(b_spec), swap(o_spec)
        grid = (n1, n0, nred)
    else:
        grid = (n0, n1, nred)

    deps = [] if after is None else [after]
    dep_specs = [pl.BlockSpec(d.shape, lambda i, j, r, nd=d.ndim: (0,) * nd) for d in deps]
    if nred == 1:
        def body(a_ref, b_ref, *rest):
            o_ref = rest[len(deps)]
            o_ref[...] = _dg(a_ref[...], b_ref[...], *dims).astype(o_ref.dtype)
        scratch = []
    else:
        def body(a_ref, b_ref, *rest):
            o_ref, acc_ref = rest[len(deps):]
            r = pl.program_id(2)
            part = _dg(a_ref[...], b_ref[...], *dims)

            @pl.when(r == 0)
            def _():
                acc_ref[...] = part

            @pl.when(jnp.logical_and(r > 0, r < nred - 1))
            def _():
                acc_ref[...] += part

            @pl.when(r == nred - 1)
            def _():
                o_ref[...] = (acc_ref[...] + part).astype(o_ref.dtype)
        scratch = [pltpu.VMEM((t0, t1), F32)]

    return pl.pallas_call(
        body, name=name, grid=grid,
        in_specs=[a_spec, b_spec] + dep_specs,
        out_specs=o_spec,
        out_shape=jax.ShapeDtypeStruct((o0, o1), out_dtype),
        scratch_shapes=scratch,
        compiler_params=pltpu.CompilerParams(dimension_semantics=("parallel", "parallel", "arbitrary"),
                                             vmem_limit_bytes=VMEM_LIMIT),
    )(a, b, *deps)


def mm_ff1_relu2(name, u2, w1, b1, tm, tn):
    m, k = u2.shape
    n = w1.shape[1]
    tm, tn = min(tm, m), min(tn, n)

    def body(a_ref, b_ref, bias_ref, hpre_ref, h_ref):
        acc = _dg(a_ref[...], b_ref[...], 1, 0)
        hpre_ref[...] = acc
        h_ref[...] = f_relu2(acc, bias_ref[...])[0].astype(h_ref.dtype)

    out = pl.BlockSpec((tm, tn), lambda j, i: (i, j))
    return pl.pallas_call(
        body, name=name, grid=(n // tn, m // tm),
        in_specs=[pl.BlockSpec((tm, k), lambda j, i: (i, 0)), pl.BlockSpec((k, tn), lambda j, i: (0, j)),
                  pl.BlockSpec((1, tn), lambda j, i: (0, j))],
        out_specs=[out, out],
        out_shape=[jax.ShapeDtypeStruct((m, n), F32), jax.ShapeDtypeStruct((m, n), BF16)],
        compiler_params=pltpu.CompilerParams(dimension_semantics=("parallel", "parallel"), vmem_limit_bytes=VMEM_LIMIT),
    )(u2, w1, b1)


def mm_dh_relu2(name, dff, w2, hpre, b1, tm, tn):
    m, k = dff.shape
    n = w2.shape[0]
    tm, tn = min(tm, m), min(tn, n)

    def body(a_ref, b_ref, hpre_ref, bias_ref, d_ref, db_ref):
        i = pl.program_id(1)
        dh = _dg(a_ref[...], b_ref[...], 1, 1)
        _, vjp = jax.vjp(f_relu2, hpre_ref[...], bias_ref[...])
        dhpre, db = vjp((dh,))
        d_ref[...] = dhpre.astype(d_ref.dtype)

        @pl.when(i == 0)
        def _():
            db_ref[...] = db

        @pl.when(i > 0)
        def _():
            db_ref[...] += db

    tile = pl.BlockSpec((tm, tn), lambda j, i: (i, j))
    col = pl.BlockSpec((1, tn), lambda j, i: (0, j))
    return pl.pallas_call(
        body, name=name, grid=(n // tn, m // tm),
        in_specs=[pl.BlockSpec((tm, k), lambda j, i: (i, 0)), pl.BlockSpec((tn, k), lambda j, i: (j, 0)), tile, col],
        out_specs=[tile, col],
        out_shape=[jax.ShapeDtypeStruct((m, n), BF16), jax.ShapeDtypeStruct((1, n), F32)],
        compiler_params=pltpu.CompilerParams(dimension_semantics=("parallel", "arbitrary"), vmem_limit_bytes=VMEM_LIMIT),
    )(dff, w2, hpre, b1)


POOL_PAD = 16
POOL_ROWS = 256


def pool_lin(name, x, wvec, transpose, out_dtype):
    arr, width, cb = x
    s = arr.shape[0]
    n_steps = s // POOL_ROWS

    def body(x_ref, w_ref, o_ref, xp_ref):
        wv = w_ref[...]
        zeros = jnp.zeros((POOL_PAD, width), F32)
        xp_ref[0:POOL_PAD, :] = zeros
        xp_ref[s + POOL_PAD:s + 2 * POOL_PAD, :] = zeros

        def count(t0):
            t = lax.broadcasted_iota(jnp.int32, (POOL_ROWS, width), 0) + (t0 + 1)
            return jnp.minimum(t.astype(F32), wv)

        def fill(i, carry):
            t0 = pl.multiple_of(i * POOL_ROWS, POOL_ROWS)
            v = x_ref[pl.ds(t0, POOL_ROWS), :].astype(F32)
            if transpose:
                v = v / count(t0)
            xp_ref[pl.ds(t0 + POOL_PAD, POOL_ROWS), :] = v
            return carry

        lax.fori_loop(0, n_steps, fill, 0)

        def step(i, carry):
            t0 = pl.multiple_of(i * POOL_ROWS, POOL_ROWS)
            win = xp_ref[pl.ds(t0, POOL_ROWS + 2 * POOL_PAD), :]
            acc = jnp.zeros((POOL_ROWS, width), F32)
            for j in range(POOL_PAD):
                off = POOL_PAD + j if transpose else POOL_PAD - j
                acc = acc + jnp.where(wv > j, win[off:off + POOL_ROWS, :], 0.0)
            cur = x_ref[pl.ds(t0, POOL_ROWS), :].astype(F32)
            res = acc - cur if transpose else acc / count(t0) - cur
            o_ref[pl.ds(t0, POOL_ROWS), :] = res.astype(o_ref.dtype)
            return carry

        lax.fori_loop(0, n_steps, step, 0)

    return pl.pallas_call(
        body, name=name, grid=(1,),
        in_specs=[pl.BlockSpec((s, width), lambda i, cb=cb: (0, cb)), pl.BlockSpec((1, width), lambda i: (0, 0))],
        out_specs=pl.BlockSpec((s, width), lambda i: (0, 0)),
        out_shape=jax.ShapeDtypeStruct((s, width), out_dtype),
        scratch_shapes=[pltpu.VMEM((s + 2 * POOL_PAD, width), F32)],
        compiler_params=pltpu.CompilerParams(dimension_semantics=("arbitrary",), vmem_limit_bytes=VMEM_LIMIT),
    )(arr, wvec)


CONV_PAD = 32
CONV_ROWS = 128


def _glu(a, g):
    return a * jax.nn.sigmoid(g)


def conv_fwd(name, zc, w):
    arr, width, cb = zc
    s = arr.shape[0]
    n_steps = s // CONV_ROWS
    lead = CONV_PAD - (CONV_WIDTH - 1)

    def body(a_ref, g_ref, w_ref, o_ref, hp_ref):
        hp_ref[0:CONV_PAD, :] = jnp.zeros((CONV_PAD, D_CONV), F32)

        def fill(i, carry):
            t0 = pl.multiple_of(i * CONV_ROWS, CONV_ROWS)
            rows = pl.ds(t0, CONV_ROWS)
            hp_ref[pl.ds(t0 + CONV_PAD, CONV_ROWS), :] = _glu(a_ref[rows, :], g_ref[rows, :])
            return carry

        lax.fori_loop(0, n_steps, fill, 0)
        wv = w_ref[...]

        def step(i, carry):
            t0 = pl.multiple_of(i * CONV_ROWS, CONV_ROWS)
            win = hp_ref[pl.ds(t0, CONV_ROWS + CONV_PAD), :]
            acc = jnp.zeros((CONV_ROWS, D_CONV), F32)
            for k in range(CONV_WIDTH):
                acc = acc + wv[k:k + 1, :] * win[lead + k:lead + k + CONV_ROWS, :]
            o_ref[pl.ds(t0, CONV_ROWS), :] = acc
            return carry

        lax.fori_loop(0, n_steps, step, 0)

    return pl.pallas_call(
        body, name=name, grid=(1,),
        in_specs=[pl.BlockSpec((s, width), lambda i, cb=cb: (0, cb)),
                  pl.BlockSpec((s, width), lambda i, cb=cb: (0, cb + 1)), pl.BlockSpec(w.shape, lambda i: (0, 0))],
        out_specs=pl.BlockSpec((s, D_CONV), lambda i: (0, 0)),
        out_shape=jax.ShapeDtypeStruct((s, D_CONV), F32),
        scratch_shapes=[pltpu.VMEM((s + CONV_PAD, D_CONV), F32)],
        compiler_params=pltpu.CompilerParams(dimension_semantics=("arbitrary",), vmem_limit_bytes=VMEM_LIMIT),
    )(arr, arr, w)


def conv_bwd(name, zc, w, dout):
    arr, width, cb = zc
    s = arr.shape[0]
    n_steps = s // CONV_ROWS
    lead = CONV_PAD - (CONV_WIDTH - 1)

    def body(a_ref, g_ref, w_ref, d_ref, dz_ref, dw_ref, hp_ref, dp_ref):
        hp_ref[0:CONV_PAD, :] = jnp.zeros((CONV_PAD, D_CONV), F32)
        dp_ref[s:s + CONV_PAD, :] = jnp.zeros((CONV_PAD, D_CONV), F32)
        dw_ref[...] = jnp.zeros(dw_ref.shape, F32)

        def fill(i, carry):
            t0 = pl.multiple_of(i * CONV_ROWS, CONV_ROWS)
            rows = pl.ds(t0, CONV_ROWS)
            hp_ref[pl.ds(t0 + CONV_PAD, CONV_ROWS), :] = _glu(a_ref[rows, :], g_ref[rows, :])
            dp_ref[rows, :] = d_ref[rows, :]
            return carry

        lax.fori_loop(0, n_steps, fill, 0)
        wv = w_ref[...]

        def step(i, carry):
            t0 = pl.multiple_of(i * CONV_ROWS, CONV_ROWS)
            hwin = hp_ref[pl.ds(t0, CONV_ROWS + CONV_PAD), :]
            dwin = dp_ref[pl.ds(t0, CONV_ROWS + CONV_PAD), :]
            dcur = dwin[0:CONV_ROWS, :]
            dh = jnp.zeros((CONV_ROWS, D_CONV), F32)
            rows = []
            for k in range(CONV_WIDTH):
                rows.append(jnp.sum(dcur * hwin[lead + k:lead + k + CONV_ROWS, :], axis=0, keepdims=True))
                back = CONV_WIDTH - 1 - k
                dh = dh + wv[k:k + 1, :] * dwin[back:back + CONV_ROWS, :]
            rows.append(jnp.zeros((1, D_CONV), F32))
            dw_ref[...] += jnp.concatenate(rows, axis=0)
            rows_now = pl.ds(t0, CONV_ROWS)
            _, vjp = jax.vjp(_glu, a_ref[rows_now, :], g_ref[rows_now, :])
            da, dg = vjp(dh)
            dz_ref[pl.ds(t0, CONV_ROWS), :] = jnp.concatenate([da, dg], axis=1).astype(dz_ref.dtype)
            return carry

        lax.fori_loop(0, n_steps, step, 0)

    return pl.pallas_call(
        body, name=name, grid=(1,),
        in_specs=[pl.BlockSpec((s, width), lambda i, cb=cb: (0, cb)),
                  pl.BlockSpec((s, width), lambda i, cb=cb: (0, cb + 1)),
                  pl.BlockSpec(w.shape, lambda i: (0, 0)), pl.BlockSpec((s, D_CONV), lambda i: (0, 0))],
        out_specs=[pl.BlockSpec((s, 2 * width), lambda i: (0, 0)), pl.BlockSpec(w.shape, lambda i: (0, 0))],
        out_shape=[jax.ShapeDtypeStruct((s, 2 * width), BF16), jax.ShapeDtypeStruct(w.shape, F32)],
        scratch_shapes=[pltpu.VMEM((s + CONV_PAD, D_CONV), F32), pltpu.VMEM((s + CONV_PAD, D_CONV), F32)],
        compiler_params=pltpu.CompilerParams(dimension_semantics=("arbitrary",), vmem_limit_bytes=VMEM_LIMIT),
    )(arr, arr, w, dout)


HEADS_PER_STEP = LANE // HEAD_DIM
CHUNKS_PER_TILE = ATT_TILE // CHUNK
KEY_BLOCKS = N_PREV * CHUNK // ATT_TILE + 1
KEY_SPAN = KEY_BLOCKS * ATT_TILE


def _attn_tile(q, *rest, missing_cols):
    kcat = jnp.concatenate(rest[:KEY_BLOCKS], axis=0)
    vcat = jnp.concatenate(rest[KEY_BLOCKS:2 * KEY_BLOCKS], axis=0)
    bias = rest[2 * KEY_BLOCKS]
    lane = lax.broadcasted_iota(jnp.int32, (1, LANE), 1)
    col = lax.broadcasted_iota(jnp.int32, (1, KEY_SPAN), 1)
    missing = col < missing_cols
    qs = q * (HEAD_DIM ** -0.5)
    o = jnp.zeros((ATT_TILE, LANE), F32)
    for h in range(HEADS_PER_STEP):
        in_head = jnp.logical_and(lane >= h * HEAD_DIM, lane < (h + 1) * HEAD_DIM)
        sc = mm_nt(jnp.where(in_head, qs, 0.0), kcat) + bias[h]
        sc = jnp.where(missing, NEG_INF, sc)
        m = jnp.max(sc, axis=-1, keepdims=True)
        e = jnp.exp(sc - lax.stop_gradient(m))
        p = e / jnp.sum(e, axis=-1, keepdims=True)
        o = o + jnp.where(in_head, mm_nn(p, vcat), 0.0)
    return o


def _missing_cols(n):
    return jnp.maximum((KEY_BLOCKS - 1 - n) * ATT_TILE, 0)


def _attn_in_specs(nt):
    def spec(col0, back):
        return pl.BlockSpec((ATT_TILE, LANE),
                            lambda hp, n, col0=col0, back=back: (jnp.clip(n - back, 0, nt - 1), col0 // LANE + hp))
    backs = list(range(KEY_BLOCKS - 1, -1, -1))
    return ([spec(Z_Q, 0)] + [spec(Z_K, b) for b in backs] + [spec(Z_V, b) for b in backs]
            + [pl.BlockSpec((HEADS_PER_STEP, ATT_TILE, KEY_SPAN), lambda hp, n: (hp, 0, 0))])


def attn_fwd(name, z, bias):
    s = z.shape[0]
    nt = s // ATT_TILE
    n_in = 2 + 2 * KEY_BLOCKS

    def body(*refs):
        o_ref = refs[n_in]
        vals = [r[...] for r in refs[:n_in]]
        o = _attn_tile(*vals, missing_cols=_missing_cols(pl.program_id(1)))
        o_ref[...] = o.astype(o_ref.dtype)

    return pl.pallas_call(
        body, name=name, grid=(N_HEADS // HEADS_PER_STEP, nt),
        in_specs=_attn_in_specs(nt),
        out_specs=pl.BlockSpec((ATT_TILE, LANE), lambda hp, n: (n, hp)),
        out_shape=jax.ShapeDtypeStruct((s, D_ATTN), BF16),
        compiler_params=pltpu.CompilerParams(dimension_semantics=("parallel", "parallel"), vmem_limit_bytes=VMEM_LIMIT),
    )(*([z] * (n_in - 1)), bias)


def attn_bwd(name, z, bias, do, after=None):
    s = z.shape[0]
    nt = s // ATT_TILE
    n_in = 2 + 2 * KEY_BLOCKS
    nc = KEY_BLOCKS - 1

    deps = [] if after is None else [after]
    n_out = n_in + 1 + len(deps)

    def body(*refs):
        do_ref = refs[n_in]
        dq_ref, dk_ref, dv_ref, db_ref = refs[n_out:n_out + 4]
        kacc, vacc = refs[n_out + 4:n_out + 4 + nc], refs[n_out + 4 + nc:]
        n = pl.program_id(1)

        @pl.when(n == 0)
        def _():
            db_ref[...] = jnp.zeros(db_ref.shape, F32)
            for acc in (*kacc, *vacc):
                acc[...] = jnp.zeros(acc.shape, F32)

        def shift(out_ref, accs, contrib):
            @pl.when(n >= nc)
            def _():
                first = accs[0][...] if contrib is None else accs[0][...] + contrib[0]
                out_ref[...] = first.astype(out_ref.dtype)
            for j in range(nc - 1):
                accs[j][...] = accs[j + 1][...] if contrib is None else accs[j + 1][...] + contrib[j + 1]
            if contrib is not None:
                accs[nc - 1][...] = contrib[nc]

        @pl.when(n < nt)
        def _():
            fn = functools.partial(_attn_tile, missing_cols=_missing_cols(n))
            _, vjp = jax.vjp(fn, *[r[...] for r in refs[:n_in]])
            grads = vjp(do_ref[...].astype(F32))
            dq_ref[...] = grads[0].astype(dq_ref.dtype)
            db_ref[...] += grads[n_in - 1]
            shift(dk_ref, kacc, grads[1:1 + KEY_BLOCKS])
            shift(dv_ref, vacc, grads[1 + KEY_BLOCKS:1 + 2 * KEY_BLOCKS])

        @pl.when(n >= nt)
        def _():
            shift(dk_ref, kacc, None)
            shift(dv_ref, vacc, None)

    o_cur = pl.BlockSpec((ATT_TILE, LANE), lambda hp, n: (jnp.minimum(n, nt - 1), hp))
    o_old = pl.BlockSpec((ATT_TILE, LANE), lambda hp, n: (jnp.maximum(n - nc, 0), hp))
    b_spec = pl.BlockSpec((HEADS_PER_STEP, ATT_TILE, KEY_SPAN), lambda hp, n: (hp, 0, 0))
    return pl.pallas_call(
        body, name=name, grid=(N_HEADS // HEADS_PER_STEP, nt + nc),
        in_specs=_attn_in_specs(nt) + [o_cur] + [pl.BlockSpec(d.shape, lambda hp, n: (0, 0)) for d in deps],
        out_specs=[o_cur, o_old, o_old, b_spec],
        out_shape=[jax.ShapeDtypeStruct((s, D_ATTN), BF16)] * 3 + [jax.ShapeDtypeStruct((N_HEADS, ATT_TILE, KEY_SPAN), F32)],
        scratch_shapes=[pltpu.VMEM((ATT_TILE, LANE), F32)] * (2 * nc),
        compiler_params=pltpu.CompilerParams(dimension_semantics=("parallel", "arbitrary"), vmem_limit_bytes=VMEM_LIMIT),
    )(*([z] * (n_in - 1)), bias, do, *deps)


def loss_head(name, y, tgt, tm):
    s, d = y.shape

    def body(y_ref, t_ref, l_ref, dy_ref):
        i = pl.program_id(0)
        diff = y_ref[...] - t_ref[...]
        dy_ref[...] = diff * (1.0 / d)
        part = 0.5 * jnp.sum(jnp.mean(diff * diff, axis=-1, keepdims=True), axis=0, keepdims=True)

        @pl.when(i == 0)
        def _():
            l_ref[...] = jnp.zeros(l_ref.shape, F32)

        l_ref[...] += jnp.broadcast_to(part, l_ref.shape)

    row = pl.BlockSpec((tm, d), lambda i: (i, 0))
    return pl.pallas_call(
        body, name=name, grid=(s // tm,), in_specs=[row, row],
        out_specs=[pl.BlockSpec((8, LANE), lambda i: (0, 0)), row],
        out_shape=[jax.ShapeDtypeStruct((8, LANE), F32), jax.ShapeDtypeStruct((s, d), F32)],
        compiler_params=pltpu.CompilerParams(dimension_semantics=("arbitrary",), vmem_limit_bytes=VMEM_LIMIT),
    )(y, tgt)


def adamw(name, w, g, m, v, tr):
    r, c = w.shape
    assert r % tr == 0, (name, w.shape, tr)

    def body(w_ref, g_ref, m_ref, v_ref, d_ref, nm_ref, nv_ref):
        gg = g_ref[...]
        m2 = ADAM_B1 * m_ref[...] + (1.0 - ADAM_B1) * gg
        v2 = ADAM_B2 * v_ref[...] + (1.0 - ADAM_B2) * (gg * gg)
        m_hat = m2 / (1.0 - ADAM_B1 ** ADAM_STEP)
        v_hat = v2 / (1.0 - ADAM_B2 ** ADAM_STEP)
        d_ref[...] = -ADAM_LR * (m_hat / (jnp.sqrt(v_hat) + ADAM_EPS) + ADAM_WD * w_ref[...])
        nm_ref[...] = m2
        nv_ref[...] = v2

    blk = pl.BlockSpec((tr, c), lambda i: (i, 0))
    return pl.pallas_call(
        body, name=name, grid=(r // tr,), in_specs=[blk] * 4, out_specs=[blk] * 3,
        out_shape=[jax.ShapeDtypeStruct((r, c), F32)] * 3,
        compiler_params=pltpu.CompilerParams(dimension_semantics=("parallel",), vmem_limit_bytes=VMEM_LIMIT),
    )(w, g, m, v)


def adamw_sum(name, w, layer_blocks, layer_own, m, v, tr):
    rows, c = w.shape
    nl = len(layer_blocks)
    nb, r, _ = layer_blocks[0].shape
    assert rows == nl * r and r % tr == 0, (name, w.shape, layer_blocks[0].shape, tr)
    per = r // tr

    def body(*refs):
        w_ref, b_refs, o_refs = refs[0], refs[1:1 + nl], refs[1 + nl:1 + 2 * nl]
        m_ref, v_ref, g_ref, d_ref, nm_ref, nv_ref = refs[1 + 2 * nl:]
        i = pl.program_id(0)
        me = _idx(_me())

        def update(b_ref, own_ref):
            gg = None
            for j in range(nb):
                blk = jnp.where(j == me, own_ref[...].astype(F32), b_ref[jnp.where(j == me, (j + 1) % nb, j)].astype(F32))
                gg = blk if gg is None else gg + blk
            g_ref[...] = gg
            m2 = ADAM_B1 * m_ref[...] + (1.0 - ADAM_B1) * gg
            v2 = ADAM_B2 * v_ref[...] + (1.0 - ADAM_B2) * (gg * gg)
            m_hat = m2 / (1.0 - ADAM_B1 ** ADAM_STEP)
            v_hat = v2 / (1.0 - ADAM_B2 ** ADAM_STEP)
            d_ref[...] = -ADAM_LR * (m_hat / (jnp.sqrt(v_hat) + ADAM_EPS) + ADAM_WD * w_ref[...])
            nm_ref[...] = m2
            nv_ref[...] = v2

        for l in range(nl):
            pl.when(jnp.logical_and(i >= l * per, i < (l + 1) * per))(functools.partial(update, b_refs[l], o_refs[l]))

    blk = pl.BlockSpec((tr, c), lambda i: (i, 0))
    b_specs = [pl.BlockSpec((nb, tr, c), lambda i, l=l: (0, jnp.clip(i - l * per, 0, per - 1), 0)) for l in range(nl)]
    o_specs = [pl.BlockSpec((tr, c), lambda i, l=l: (jnp.clip(i - l * per, 0, per - 1), 0)) for l in range(nl)]
    return pl.pallas_call(
        body, name=name, grid=(rows // tr,),
        in_specs=[blk] + b_specs + o_specs + [blk, blk], out_specs=[blk] * 4,
        out_shape=[jax.ShapeDtypeStruct((rows, c), F32)] * 4,
        compiler_params=pltpu.CompilerParams(dimension_semantics=("arbitrary",), vmem_limit_bytes=VMEM_LIMIT),
    )(w, *layer_blocks, *layer_own, m, v)


def sum_blocks(name, blocks, tr):
    nb, r, c = blocks.shape
    assert r % tr == 0, (name, blocks.shape, tr)

    def body(b_ref, o_ref):
        acc = b_ref[0].astype(F32)
        for j in range(1, nb):
            acc = acc + b_ref[j].astype(F32)
        o_ref[...] = acc

    return pl.pallas_call(
        body, name=name, grid=(r // tr,),
        in_specs=[pl.BlockSpec((nb, tr, c), lambda i: (0, i, 0))],
        out_specs=pl.BlockSpec((tr, c), lambda i: (i, 0)),
        out_shape=jax.ShapeDtypeStruct((r, c), F32),
        compiler_params=pltpu.CompilerParams(dimension_semantics=("parallel",), vmem_limit_bytes=VMEM_LIMIT),
    )(blocks)


FLIPS = [(0, 0, 1), (1, 0, 0), (0, 1, 0), (1, 1, 0), (1, 0, 1), (0, 1, 1), (1, 1, 1)]
ANY = pl.BlockSpec(memory_space=pl.ANY)


def _me():
    return lax.axis_index("x"), lax.axis_index("y"), lax.axis_index("c")


def _flip(pos, f):
    return tuple((1 - p) if fi else p for p, fi in zip(pos, f))


def _idx(pos):
    return 4 * pos[0] + 2 * pos[1] + pos[2]


def all_gather_multi(name, shards):
    n = len(shards)

    def body(*refs):
        x_refs, out_refs, token = refs[:n], refs[n:2 * n], refs[2 * n]
        send_sems, recv_sems, local_sems = refs[2 * n + 1:]
        token[...] = jnp.zeros_like(token)
        x, y, cc = _me()
        me, sibling = (x, y, cc), (x, y, 1 - cc)
        chips = [(1 - x, y), (x, 1 - y), (1 - x, 1 - y)]

        def copy(a, k, block, to, src=None):
            dst = out_refs[a].at[_idx(block)]
            return pltpu.make_async_remote_copy(
                src_ref=dst if src is None else src, dst_ref=dst, send_sem=send_sems.at[7 * a + k],
                recv_sem=recv_sems.at[7 * a + k], device_id=to, device_id_type=MESH)

        mine = [pltpu.make_async_copy(x_refs[a], out_refs[a].at[_idx(me)], local_sems.at[a]) for a in range(n)]
        for cp in mine:
            cp.start()
        first = []
        for a in range(n):
            first.append(copy(a, 0, me, sibling, src=x_refs[a]))
            first += [copy(a, 1 + j, me, (*chip, cc), src=x_refs[a]) for j, chip in enumerate(chips)]
        for cp in first:
            cp.start()
        passed = []
        for j, chip in enumerate(chips):
            for a in range(n):
                copy(a, 1 + j, (*chip, cc), me).wait_recv()
                fwd = copy(a, 4 + j, (*chip, cc), sibling)
                fwd.start()
                passed.append(fwd)
        for a in range(n):
            copy(a, 0, sibling, me).wait_recv()
            for j, chip in enumerate(chips):
                copy(a, 4 + j, (*chip, 1 - cc), me).wait_recv()
        for cp in first + passed:
            cp.wait_send()
        for cp in mine:
            cp.wait()

    return pl.pallas_call(
        body, name=name, in_specs=[ANY] * n, out_specs=[ANY] * n + [pl.BlockSpec(memory_space=pltpu.VMEM)],
        out_shape=[jax.ShapeDtypeStruct((N_DEV,) + a.shape, a.dtype) for a in shards]
        + [jax.ShapeDtypeStruct((8, LANE), F32)],
        scratch_shapes=[pltpu.SemaphoreType.DMA((7 * n,)), pltpu.SemaphoreType.DMA((7 * n,)),
                        pltpu.SemaphoreType.DMA((n,))],
    )(*shards)


HBM = pl.BlockSpec(memory_space=pltpu.HBM)
SEM = pl.BlockSpec(memory_space=pltpu.SEMAPHORE)
DATAFLOW = pltpu.SideEffectType.DATAFLOW_SIDE_EFFECTING


def _exchange_copies(a_refs, l_refs, send_sems, recv_sems, gather):
    me = _me()
    mi = _idx(me)
    out = []
    for k, f in enumerate(FLIPS):
        peer = _flip(me, f)
        for a in range(len(a_refs)):
            src = a_refs[a] if gather else a_refs[a].at[_idx(peer)]
            out.append(pltpu.make_async_remote_copy(
                src_ref=src, dst_ref=l_refs[a].at[mi], send_sem=send_sems.at[7 * a + k],
                recv_sem=recv_sems.at[7 * a + k], device_id=peer, device_id_type=MESH))
    return out


def exchange_start(name, arrays, gather):
    n = len(arrays)
    lands = [lax.empty(((N_DEV,) + a.shape) if gather else a.shape, a.dtype) for a in arrays]

    def body(*refs):
        a_refs, l_refs = refs[:n], refs[n:2 * n]
        send_sems, recv_sems = refs[2 * n], refs[2 * n + 1]
        token = refs[4 * n + 2]
        for cp in _exchange_copies(a_refs, l_refs, send_sems, recv_sems, gather):
            cp.start()
        token[...] = jnp.zeros_like(token)

    hbm = lambda a: pltpu.HBM(a.shape, a.dtype)
    res = pl.pallas_call(
        body, name=name,
        out_shape=(pltpu.SemaphoreType.DMA((7 * n,)), pltpu.SemaphoreType.DMA((7 * n,)),
                   *[hbm(a) for a in arrays], *[hbm(a) for a in lands], jax.ShapeDtypeStruct((8, LANE), F32)),
        in_specs=[HBM] * (2 * n),
        out_specs=(SEM, SEM, *[HBM] * (2 * n), pl.BlockSpec(memory_space=pltpu.VMEM)),
        input_output_aliases={i: i + 2 for i in range(2 * n)},
        compiler_params=pltpu.CompilerParams(has_side_effects=DATAFLOW),
    )(*[pltpu.with_memory_space_constraint(a, pltpu.HBM) for a in arrays],
      *[pltpu.with_memory_space_constraint(a, pltpu.HBM) for a in lands])
    return res[0], res[1], list(res[2:2 + n]), list(res[2 + n:2 + 2 * n]), res[-1]


def exchange_wait(name, send_sems, recv_sems, arrays, lands, after, gather):
    n = len(arrays)
    after = list(after) if isinstance(after, (list, tuple)) else [after]

    def body(*refs):
        a_refs, l_refs = refs[:n], refs[n:2 * n]
        ssem, rsem = refs[2 * n], refs[2 * n + 1]
        for cp in _exchange_copies(a_refs, l_refs, ssem, rsem, gather):
            cp.wait_send()
            cp.wait_recv()

    hbm = lambda a: pltpu.HBM(a.shape, a.dtype)
    res = pl.pallas_call(
        body, name=name,
        out_shape=(*[hbm(a) for a in arrays], *[hbm(a) for a in lands]),
        in_specs=[HBM] * (2 * n) + [SEM, SEM] + [pl.BlockSpec(memory_space=pl.ANY)] * len(after),
        out_specs=tuple([HBM] * (2 * n)),
        input_output_aliases={i: i for i in range(2 * n)},
        compiler_params=pltpu.CompilerParams(has_side_effects=DATAFLOW),
    )(*arrays, *lands, send_sems, recv_sems, *after)
    return list(res[n:])


def ada_fwd(name, c_row, w_cat, b_lay):
    d = c_row.shape[1]
    ncol = w_cat.shape[1]
    vmem = pl.BlockSpec(memory_space=pltpu.VMEM)

    def body(c_ref, w_ref, b_ref, mod_ref, cact_ref, token, call, send, land, s1, r1, s2, r2):
        token[...] = jnp.zeros_like(token)
        me = _me()
        mi = _idx(me)
        call[mi] = c_ref[...]

        def exchange(src_of, dst_buf, ssem, rsem):
            sends, recvs = [], []
            for k, f in enumerate(FLIPS):
                peer = _flip(me, f)
                sends.append(pltpu.make_async_remote_copy(
                    src_ref=src_of(peer), dst_ref=dst_buf.at[mi], send_sem=ssem.at[k], recv_sem=rsem.at[k],
                    device_id=peer, device_id_type=MESH))
                recvs.append(pltpu.make_async_remote_copy(
                    src_ref=src_of(peer), dst_ref=dst_buf.at[_idx(peer)], send_sem=ssem.at[k], recv_sem=rsem.at[k],
                    device_id=peer, device_id_type=MESH))
            for cp in sends:
                cp.start()
            for cp in recvs:
                cp.wait_recv()
            for cp in sends:
                cp.wait_send()

        exchange(lambda peer: c_ref, call, s1, r1)
        for p in range(N_DEV):
            cact_ref[pl.ds(p, 1), :] = jax.nn.silu(call[p])
        res = _dg(cact_ref[...], w_ref[...], 1, 0)
        for p in range(N_DEV):
            send[p] = res[p:p + 1, :]
        land[mi] = send[mi]
        exchange(lambda peer: send.at[_idx(peer)], land, s2, r2)
        mod_ref[...] = land[...] + b_ref[...]

    return pl.pallas_call(
        body, name=name, in_specs=[vmem, vmem, vmem], out_specs=[vmem, vmem, vmem],
        out_shape=[jax.ShapeDtypeStruct((N_DEV, 1, ncol), F32), jax.ShapeDtypeStruct((N_DEV, d), F32),
                   jax.ShapeDtypeStruct((8, LANE), F32)],
        scratch_shapes=[pltpu.VMEM((N_DEV, 1, d), F32), pltpu.VMEM((N_DEV, 1, ncol), F32),
                        pltpu.VMEM((N_DEV, 1, ncol), F32),
                        pltpu.SemaphoreType.DMA((7,)), pltpu.SemaphoreType.DMA((7,)),
                        pltpu.SemaphoreType.DMA((7,)), pltpu.SemaphoreType.DMA((7,))],
        compiler_params=pltpu.CompilerParams(vmem_limit_bytes=VMEM_LIMIT),
    )(c_row, w_cat, b_lay)


POOL_WINDOWS = (2, 4, 8, 16)
POOL_GROUP = 64
N_REL = 2 * REL_CLIP + 1
SMALL_NAMES = ["b_ada", "b_gate", "w_pool", "pool_scale", "rel_bias", "conv_w", "conv_b", "conv_ln_g",
               "conv_ln_b", "ln_mix_g", "ln_mix_b", "b_ff1", "b_ff2", "ln_ff_g", "ln_ff_b"]
BIG_NAMES = ["w_in", "w_br_pool", "w_br_attn", "w_br_conv", "w_o", "w_ff1", "w_ff2"]
ROW_SHARDED = ("w_in", "w_o", "w_ff2")
WEIGHT_NAMES = ["w_ada", "b_ada", "w_in", "b_gate", "w_pool", "pool_scale", "rel_bias", "conv_w", "conv_b",
                "conv_ln_g", "conv_ln_b", "w_br_pool", "w_br_attn", "w_br_conv", "w_o", "ln_mix_g", "ln_mix_b",
                "w_ff1", "b_ff1", "w_ff2", "b_ff2", "ln_ff_g", "ln_ff_b"]


def _bias_table(rel_bias):
    far = jnp.broadcast_to(rel_bias[:, 2 * REL_CLIP:], (N_HEADS, BAND - REL_CLIP))
    near = rel_bias[:, REL_CLIP - CHUNK + 1:2 * REL_CLIP][:, ::-1]
    ext = jnp.concatenate([far, near, jnp.zeros((N_HEADS, 1), F32)], axis=1)
    length = BAND + CHUNK
    flat = jnp.tile(ext, (1, CHUNK + 1))
    skew = flat[:, CHUNK - 1:CHUNK - 1 + CHUNK * (length - 1)].reshape(N_HEADS, CHUNK, length - 1)
    return skew[:, :, :BAND]


def _bias_full(rel_bias):
    tab = _bias_table(rel_bias)
    return jnp.concatenate(
        [jnp.pad(tab, ((0, 0), (0, 0), (i * CHUNK, KEY_SPAN - BAND - i * CHUNK)), constant_values=NEG_INF)
         for i in range(CHUNKS_PER_TILE)], axis=1)


def _block_diag(w_pool):
    out = jnp.zeros((D_POOL, D_POOL), F32)
    for g in range(len(POOL_WINDOWS)):
        out = lax.dynamic_update_slice(out, w_pool[g], (g * POOL_GROUP, g * POOL_GROUP))
    return out


def _flat_pad(arrs, mult):
    flat = jnp.concatenate([a.reshape(-1) for a in arrs])
    pad = (-flat.shape[0]) % mult
    return jnp.pad(flat, (0, pad)) if pad else flat


def _unflat(flat, shapes):
    out, off = [], 0
    for shp in shapes:
        n = int(np.prod(shp))
        out.append(flat[off:off + n].reshape(shp))
        off += n
    return out


def _to_blocks(name, full):
    k, n = full.shape
    if name in ROW_SHARDED:
        return full.reshape(N_DEV, k // N_DEV, n)
    return full.reshape(k, N_DEV, n // N_DEV).transpose(1, 0, 2)


def _from_blocks(name, blocks):
    nb, r, c = blocks.shape
    if name in ROW_SHARDED:
        return blocks.reshape(nb * r, c)
    return blocks.transpose(1, 0, 2).reshape(r, nb * c)


class _Layer:
    pass


def _row(v):
    return v.reshape(1, -1)


def _layer_fwd(x, modr, w, wvec, fetch_rest):
    sh_m, sc_m, g_m, sh_f, sc_f, g_f = modr
    (u,) = row_fwd("lnmod_mix", f_lnmod, [x], [sc_m, sh_m], [(D_MODEL, BF16)], 512)
    z = mm_big("mm_in", u, w.w_in_t, "nt", (1024, 896, 1024), F32, j_outer=True)
    p = pool_lin("pool_fwd", (z, D_POOL, Z_POOL // D_POOL), wvec, False, F32)
    ao = attn_fwd("attn_fwd", z, w.bias)
    cv = conv_fwd("conv_fwd", (z, D_CONV, Z_CONV // D_CONV), w.conv_w)
    if not hasattr(w, "wo"):
        fetch_rest(w, cv)
    mparams = [w.wbd, w.ps, w.wbp, w.wba, w.wbc, w.cb, w.clg, w.clb, w.bg, w.wo, g_m, w.lmg, w.lmb]
    gate_blocks = [(z, GATE_BLOCK, Z_GATE // GATE_BLOCK + k) for k in range(3 * D_MODEL // GATE_BLOCK)]
    (x1,) = row_fwd("merge", f_merge, [p, ao, cv, *gate_blocks, x], mparams, [(D_MODEL, F32)], 512)
    (u2,) = row_fwd("lnmod_ff", f_lnmod, [x1], [sc_f, sh_f], [(D_MODEL, BF16)], 512)
    if not hasattr(w, "w_ff1"):
        fetch_rest(w, u2)
    hpre, h = mm_ff1_relu2("mm_ff1", u2, w.w_ff1, w.b1, 1024, 1024)
    ff = mm_big("mm_ff2", h, w.w_ff2, "nn", (1024, 1024, 4096), F32)
    (x2,) = row_fwd("ffout", f_ffout, [x1, ff], [w.b2, g_f, w.lfg, w.lfb], [(D_MODEL, F32)], 512)
    return x2, (x, u, z, p, ao, cv, x1, u2, hpre, h, ff, mparams)


GRAD_GROUPS = [("ff", ["w_ff2", "w_ff1"]), ("mix", ["w_o", "w_br_pool", "w_br_attn", "w_br_conv"]), ("in", ["w_in"])]


def _layer_bwd(dx2, saved, modr, w, wvec, ready):
    x, u, z, p, ao, cv, x1, u2, hpre, h, ff, mparams = saved
    sh_m, sc_m, g_m, sh_f, sc_f, g_f = modr
    g = {}
    (dx1a, dff), (g["b_ff2"], dgf, g["ln_ff_g"], g["ln_ff_b"]) = row_bwd(
        "ffout_bwd", f_ffout, [x1, ff], [w.b2, g_f, w.lfg, w.lfb], [dx2], 512, [(0, F32), (1, BF16)], [0, 1, 2, 3])
    dhpre, g["b_ff1"] = mm_dh_relu2("mm_dh", dff, w.w_ff2, hpre, w.b1, 1024, 1024)
    g["w_ff2"] = mm_big("mm_dw_ff2", h, dff, "tn", (1024, 1024, 2048), BF16)
    du2 = mm_big("mm_du2", dhpre, w.w_ff1, "nt", (1024, 1024, 4096), F32)
    g["w_ff1"] = mm_big("mm_dw_ff1", u2, dhpre, "tn", (1024, 1024, 2048), BF16)
    sc_f = sc_f + ready("ff", g)[0:1, 0:1]
    (dx1,), (dscf, dshf) = row_bwd("lnmod_ff_bwd", f_lnmod, [x1], [sc_f, sh_f], [du2], 512, [(0, F32)], [0, 1],
                                   add_to=dx1a)
    gate_blocks = [(z, GATE_BLOCK, Z_GATE // GATE_BLOCK + k) for k in range(3 * D_MODEL // GATE_BLOCK)]
    (dp, dao, dcv, *dzg, dxa), dm = row_bwd(
        "merge_bwd", f_merge, [p, ao, cv, *gate_blocks, x], mparams, [dx1], 256,
        [(0, F32), (1, BF16), (2, F32), (3, BF16), (4, BF16), (5, BF16), (6, BF16), (7, F32)], list(range(13)))
    (dwbd, g["pool_scale"], g["w_br_pool"], g["w_br_attn"], g["w_br_conv"], g["conv_b"], g["conv_ln_g"],
     g["conv_ln_b"], g["b_gate"], g["w_o"], dgm, g["ln_mix_g"], g["ln_mix_b"]) = dm
    g["w_pool"] = jnp.stack([dwbd[i * POOL_GROUP:(i + 1) * POOL_GROUP, i * POOL_GROUP:(i + 1) * POOL_GROUP]
                             for i in range(len(POOL_WINDOWS))])
    tok = ready("mix", g)
    dzp = pool_lin("pool_bwd", (dp, D_POOL, 0), wvec + tok[0:1, 0:1], True, BF16)
    dq, dk, dv, dbias = attn_bwd("attn_bwd", z, w.bias, dao, after=tok)
    (g["rel_bias"],) = w.bias_vjp(dbias)
    dzc, dcw = conv_bwd("conv_bwd", (z, D_CONV, Z_CONV // D_CONV), w.conv_w + tok[0:1, 0:1], dcv)
    g["conv_w"] = dcw[:CONV_WIDTH]
    dz = jnp.concatenate([dzp, dq, dk, dv, dzc, *dzg], axis=1)
    du = mm_big("mm_du", dz, w.w_in_t, "nn", (512, 1024, D_IN), F32)
    (dx,), (dscm, dshm) = row_bwd("lnmod_mix_bwd", f_lnmod, [x], [sc_m, sh_m], [du], 512, [(0, F32)], [0, 1],
                                  add_to=dxa)
    g["dmod"] = jnp.concatenate([dshm, dscm, dgm, dshf, dscf, dgf], axis=1)
    tok = ready("small", g)
    g["w_in"] = mm_big("mm_dw_in", dz, u, "tn", (896, 1024, 2048), BF16, after=tok)
    return dx, g, ready("in", g)


def kernel(x, c, w_ada, b_ada, w_in, b_gate, w_pool, pool_scale, rel_bias, conv_w, conv_b, conv_ln_g, conv_ln_b, w_br_pool, w_br_attn, w_br_conv, w_o, ln_mix_g, ln_mix_b, w_ff1, b_ff1, w_ff2, b_ff2, ln_ff_g, ln_ff_b, loss_target, m_w_ada, m_b_ada, m_w_in, m_b_gate, m_w_pool, m_pool_scale, m_rel_bias, m_conv_w, m_conv_b, m_conv_ln_g, m_conv_ln_b, m_w_br_pool, m_w_br_attn, m_w_br_conv, m_w_o, m_ln_mix_g, m_ln_mix_b, m_w_ff1, m_b_ff1, m_w_ff2, m_b_ff2, m_ln_ff_g, m_ln_ff_b, v_w_ada, v_b_ada, v_w_in, v_b_gate, v_w_pool, v_pool_scale, v_rel_bias, v_conv_w, v_conv_b, v_conv_ln_g, v_conv_ln_b, v_w_br_pool, v_w_br_attn, v_w_br_conv, v_w_o, v_ln_mix_g, v_ln_mix_b, v_w_ff1, v_b_ff1, v_w_ff2, v_b_ff2, v_ln_ff_g, v_ln_ff_b):
    args = dict(locals())
    wts = {n: args[n] for n in WEIGHT_NAMES}
    mom = {n: args["m_" + n] for n in WEIGHT_NAMES}
    var = {n: args["v_" + n] for n in WEIGHT_NAMES}
    me = 4 * lax.axis_index("x") + 2 * lax.axis_index("y") + lax.axis_index("c")
    xs, tgt = x[0], loss_target[0]
    nc_ada = w_ada.shape[2]
    wvec = jnp.asarray(np.repeat(np.array(POOL_WINDOWS, np.float32), POOL_GROUP)[None, :])

    w_cat = jnp.concatenate([w_ada[l] for l in range(DEPTH)], axis=1)
    b_lay = b_ada.reshape(DEPTH, N_DEV, nc_ada).transpose(1, 0, 2).reshape(N_DEV, 1, DEPTH * nc_ada)
    land, cact, ada_token = ada_fwd("ada_fwd", c, w_cat, b_lay)
    mod = land.reshape(N_DEV, DEPTH, nc_ada).transpose(1, 0, 2).reshape(DEPTH, 6 * D_MODEL)
    modr = [[mod[l:l + 1, i * D_MODEL:(i + 1) * D_MODEL] for i in range(6)] for l in range(DEPTH)]

    cw_pack = _flat_pad([conv_w], 8 * LANE).reshape(-1, LANE) + ada_token[0:1, 0:1]
    xlayout = lambda n, a: jnp.swapaxes(a, -1, -2) if n == "w_in" else a
    shards = [[xlayout(n, wts[n][l]).astype(BF16) for n in BIG_NAMES] for l in range(DEPTH)]
    n_first = [BIG_NAMES.index("w_ff1"), 1]
    *first0, cw_all, token = all_gather_multi("gather_first_l0", shards[0][:n_first[0]] + [cw_pack])
    cw_all = cw_all.reshape(N_DEV, -1)[:, :conv_w.size]
    conv_full = cw_all.reshape((N_DEV,) + conv_w.shape).transpose(1, 2, 0, 3).reshape(DEPTH, CONV_WIDTH, D_CONV)
    gathers = {}
    for key, arrs in [("rest_l0", shards[0][n_first[0]:]), ("first_l1", shards[1][:n_first[1]]),
                      ("rest_l1", shards[1][n_first[1]:])]:
        arrs = [a + token[0:1, 0:1].astype(a.dtype) for a in arrs]
        ssem, rsem, thru, lands, token = exchange_start(f"gather_{key}_start", arrs, True)
        gathers[key] = (arrs, ssem, rsem, thru, lands)
    modr[0] = [r + token[0:1, 0:1] for r in modr[0]]

    def with_own(lands_, own):
        return [lax.dynamic_update_index_in_dim(ld, o, me, axis=0) for ld, o in zip(lands_, own)]

    def gathered(key, after):
        arrs, ssem, rsem, thru, lands = gathers[key]
        return with_own(exchange_wait(f"gather_{key}_wait", ssem, rsem, thru, lands, after, True), arrs)

    attr = dict(w_in="w_in_t", w_br_pool="wbp", w_br_attn="wba", w_br_conv="wbc", w_o="wo", w_ff1="w_ff1", w_ff2="w_ff2")

    def assign(w, names, blocks):
        for n, g in zip(names, blocks):
            setattr(w, attr[n], _from_blocks(n, g))

    def fetch_rest_for(l):
        return lambda w, after: assign(w, BIG_NAMES[n_first[l]:], gathered(f"rest_l{l}", after))

    def layer_weights(l, first_blocks):
        w = _Layer()
        assign(w, BIG_NAMES[:n_first[l]], first_blocks)
        w.wbd = _block_diag(w_pool[l])
        w.ps, w.cb, w.clg, w.clb = _row(pool_scale[l]), _row(conv_b[l]), _row(conv_ln_g[l]), _row(conv_ln_b[l])
        w.bg, w.lmg, w.lmb = _row(b_gate[l]), _row(ln_mix_g[l]), _row(ln_mix_b[l])
        w.b1, w.b2, w.lfg, w.lfb = _row(b_ff1[l]), _row(b_ff2[l]), _row(ln_ff_g[l]), _row(ln_ff_b[l])
        w.conv_w = jnp.pad(conv_full[l], ((0, CONV_PAD - CONV_WIDTH), (0, 0)))
        w.bias, w.bias_vjp = jax.vjp(_bias_full, rel_bias[l])
        return w

    layers, saved = [layer_weights(0, first0)], []
    h, sv = _layer_fwd(xs, modr[0], layers[0], wvec, fetch_rest_for(0))
    saved.append(sv)
    layers.append(layer_weights(1, gathered("first_l1", h)))
    h, sv = _layer_fwd(h, modr[1], layers[1], wvec, fetch_rest_for(1))
    saved.append(sv)
    lpart, dy = loss_head("loss_head", h, tgt, 512)
    loss = lax.psum(lpart[0, 0], ("x", "y", "c"))
    grads = [None] * DEPTH
    pending = {}
    small_shapes = [wts[n].shape if n != "conv_w" else (DEPTH, CONV_WIDTH, D_CONV) for n in SMALL_NAMES]

    def ready_for(l):
        def ready(group, g):
            if group == "small":
                if l > 0:
                    return None
                both = [g, grads[1]]
                local = [jnp.concatenate([both[k]["dmod"] for k in range(DEPTH)], axis=0)]
                local += [jnp.stack([both[k][n].reshape(shp[1:]) for k in range(DEPTH)])
                          for n, shp in zip(SMALL_NAMES[1:], small_shapes[1:])]
                pack = _flat_pad(local, 8 * LANE).reshape(-1, LANE)
                ssem, rsem, thru, lands, token = exchange_start("gather_small_grads_start", [pack], True)
                pending["small"] = (pack, ssem, rsem, thru, lands)
                return token
            names = dict(GRAD_GROUPS)[group]
            blocks = [_to_blocks(n, g[n]).astype(BF16) for n in names]
            ssem, rsem, thru, lands, token = exchange_start(f"scatter_l{l}_{group}_start", blocks, False)
            pending[(l, group)] = (names, blocks, ssem, rsem, thru, lands)
            return token
        return ready

    def received(l, group, after):
        names, blocks, ssem, rsem, thru, lands = pending[(l, group)]
        lands = exchange_wait(f"scatter_l{l}_{group}_wait", ssem, rsem, thru, lands, after, False)
        own = [lax.dynamic_index_in_dim(b, me, axis=0, keepdims=False) for b in blocks]
        return dict(zip(names, zip(lands, own)))

    dy, grads[1], token = _layer_bwd(dy, saved[1], modr[1], layers[1], wvec, ready_for(1))
    modr0 = [r + token[0:1, 0:1] for r in modr[0]]
    dy, grads[0], token = _layer_bwd(dy, saved[0], modr0, layers[0], wvec, ready_for(0))
    recv = [{} for _ in range(DEPTH)]
    for l, group in [(1, "ff"), (1, "mix"), (1, "in"), (0, "ff"), (0, "mix")]:
        recv[l].update(received(l, group, token))
    grad_x = dy[None]
    dmod_size = DEPTH * 6 * D_MODEL

    small_pack, ssem, rsem, thru, lands = pending["small"]
    (small_all,) = with_own(exchange_wait("gather_small_grads_wait", ssem, rsem, thru, lands, token, True), [small_pack])
    small_sum = sum_blocks("sum_small_grads", small_all, small_pack.shape[0]).reshape(-1)
    gsmall = dict(zip(SMALL_NAMES, _unflat(small_sum, small_shapes)))
    gw = dict(gsmall)
    gw["conv_w"] = lax.dynamic_slice_in_dim(gsmall["conv_w"], me * conv_w.shape[2], conv_w.shape[2], axis=2)

    dmod_all = small_all.reshape(N_DEV, -1)[:, :dmod_size].reshape(N_DEV, DEPTH, N_DEV, nc_ada)
    dm_mine = lax.dynamic_index_in_dim(dmod_all, me, axis=2, keepdims=False).reshape(N_DEV, DEPTH * nc_ada)
    cact_t = jnp.pad(cact.T, ((0, 0), (0, LANE - N_DEV)))
    dm_pad = jnp.pad(dm_mine, ((0, LANE - N_DEV), (0, 0)))
    dw_cat = mm_big("mm_dw_ada", cact_t, dm_pad, "nn", (D_MODEL, DEPTH * nc_ada, LANE), F32)
    gw["w_ada"] = jnp.stack([dw_cat[:, l * nc_ada:(l + 1) * nc_ada] for l in range(DEPTH)])

    delta, new_m, new_v = {}, {}, {}
    packs = [_flat_pad([src[n] for n in SMALL_NAMES], 8 * LANE).reshape(-1, LANE) for src in (wts, gw, mom, var)]
    small_res = adamw("adamw_small", *packs, packs[0].shape[0])
    shapes = [wts[n].shape for n in SMALL_NAMES]
    for out, flat in zip((delta, new_m, new_v), small_res):
        out.update(dict(zip(SMALL_NAMES, _unflat(flat.reshape(-1), shapes))))
    for n in ["w_ada", "w_ff2", "w_ff1", "w_o", "w_br_pool", "w_br_attn", "w_br_conv", "w_in"]:
        shp = xlayout(n, wts[n]).shape
        two_d = lambda a, shp=shp, n=n: xlayout(n, a).reshape(shp[0] * shp[1], shp[2])
        tr = 224 if n == "w_in" else min(256, shp[1])
        if n == "w_ada":
            res = (gw[n],) + tuple(adamw("adamw_" + n, two_d(wts[n]), two_d(gw[n]), two_d(mom[n]), two_d(var[n]), tr))
        else:
            if n == "w_in":
                done = [small_res[2]] + [new_v[k] for k in ["w_ada"] + BIG_NAMES[1:]]
                recv[0].update(received(0, "in", done))
            res = adamw_sum("adamw_" + n, two_d(wts[n]), [recv[l][n][0] for l in range(DEPTH)],
                            [recv[l][n][1] for l in range(DEPTH)], two_d(mom[n]), two_d(var[n]), tr)
        gw[n], delta[n], new_m[n], new_v[n] = [xlayout(n, a.reshape(shp)) for a in res]

    return (loss, grad_x, *[gw[n] for n in WEIGHT_NAMES], *[delta[n] for n in WEIGHT_NAMES],
            *[new_m[n] for n in WEIGHT_NAMES], *[new_v[n] for n in WEIGHT_NAMES])
```

```python
import functools

import jax
import jax.numpy as jnp
import numpy as np
from jax import lax
from jax.experimental import pallas as pl
from jax.experimental.pallas import tpu as pltpu

F32 = jnp.float32
BF16 = jnp.bfloat16
MESH = pl.DeviceIdType.MESH

D_MODEL = 1024
DEPTH = 2
CHUNK = 64
N_HEADS = 8
HEAD_DIM = 64
D_POOL = 256
D_ATTN = 512
D_CONV = 256
CONV_WIDTH = 31
D_FF = 4096
D_IN = 5376
N_PREV = 8
BAND = (N_PREV + 1) * CHUNK
REL_CLIP = 128
ALPHA = (2.0 * DEPTH) ** 0.25
LN_EPS = 1e-5
NEG_INF = -1e30
N_DEV = 8

ADAM_LR, ADAM_B1, ADAM_B2, ADAM_EPS, ADAM_WD, ADAM_STEP = 0.001, 0.9, 0.999, 1e-08, 0.01, 10

VMEM_LIMIT = 56 * 1024 * 1024

Z_POOL, Z_Q, Z_K, Z_V, Z_CONV, Z_GATE = 0, 256, 768, 1280, 1792, 2304
GATE_BLOCK = 768
ATT_TILE = 512
LANE = 128


def _dg(a, b, ca, cb):
    return lax.dot_general(a.astype(BF16), b.astype(BF16), (((ca,), (cb,)), ((), ())),
                           preferred_element_type=F32)


@jax.custom_vjp
def mm_nn(a, b):
    return _dg(a, b, 1, 0)


def _mm_nn_fwd(a, b):
    return _dg(a, b, 1, 0), (a, b)


def _mm_nn_bwd(res, g):
    a, b = res
    return _dg(g, b, 1, 1).astype(a.dtype), _dg(a, g, 0, 0).astype(b.dtype)


mm_nn.defvjp(_mm_nn_fwd, _mm_nn_bwd)


@jax.custom_vjp
def mm_nt(a, b):
    return _dg(a, b, 1, 1)


def _mm_nt_fwd(a, b):
    return _dg(a, b, 1, 1), (a, b)


def _mm_nt_bwd(res, g):
    a, b = res
    return _dg(g, b, 1, 0).astype(a.dtype), _dg(g, a, 0, 0).astype(b.dtype)


mm_nt.defvjp(_mm_nt_fwd, _mm_nt_bwd)


@jax.custom_vjp
def mm_nn_shadow(a, w, shadow):
    return _dg(a, w, 1, 0)


def _mm_nn_shadow_fwd(a, w, shadow):
    return _dg(a, w, 1, 0), (a, w)


def _mm_nn_shadow_bwd(res, g):
    a, w = res
    return _dg(g, w, 1, 1).astype(a.dtype), jnp.zeros_like(w), _dg(a, g, 0, 0)


mm_nn_shadow.defvjp(_mm_nn_shadow_fwd, _mm_nn_shadow_bwd)


def mm_w(a, w):
    return mm_nn_shadow(a, w[0], w[1]) if isinstance(w, tuple) else mm_nn(a, w)


def _ln(x):
    mu = jnp.mean(x, axis=-1, keepdims=True)
    xc = x - mu
    var = jnp.mean(xc * xc, axis=-1, keepdims=True)
    return xc * lax.rsqrt(var + LN_EPS)


def _norm_rows(rows):
    return [r if isinstance(r, tuple) else (r, r.shape[1], 0) for r in rows]


def _row_spec(tm, r):
    _, width, cb = r
    return pl.BlockSpec((tm, width), lambda i, cb=cb: (i, cb))


def _full_spec(a):
    nd = a.ndim
    return pl.BlockSpec(a.shape, lambda i, nd=nd: (0,) * nd)


def row_fwd(name, f, rows, params, outs, tm):
    rows = _norm_rows(rows)
    s = rows[0][0].shape[0]
    nr, npar = len(rows), len(params)

    def body(*refs):
        r = [x[...].astype(F32) for x in refs[:nr]]
        p = [x[...] for x in refs[nr:nr + npar]]
        res = f(*r, *p)
        for o_ref, o in zip(refs[nr + npar:], res):
            o_ref[...] = o.astype(o_ref.dtype)

    return pl.pallas_call(
        body, name=name, grid=(s // tm,),
        in_specs=[_row_spec(tm, r) for r in rows] + [_full_spec(p) for p in params],
        out_specs=[pl.BlockSpec((tm, w), lambda i: (i, 0)) for w, _ in outs],
        out_shape=[jax.ShapeDtypeStruct((s, w), dt) for w, dt in outs],
        compiler_params=pltpu.CompilerParams(dimension_semantics=("parallel",), vmem_limit_bytes=VMEM_LIMIT),
    )(*[r[0] for r in rows], *params)


def row_bwd(name, f, rows, params, douts, tm, want_rows, want_params, add_to=None):
    rows = _norm_rows(rows)
    s = rows[0][0].shape[0]
    nr, npar, nd = len(rows), len(params), len(douts)
    nadd = 0 if add_to is None else 1
    n_in = nr + npar + nd + nadd

    def body(*refs):
        i = pl.program_id(0)
        r = [x[...].astype(F32) for x in refs[:nr]]
        p = [(x[...], jnp.zeros(x.shape, F32)) if x.dtype == BF16 else x[...] for x in refs[nr:nr + npar]]
        d = [x[...].astype(F32) for x in refs[nr + npar:nr + npar + nd]]
        _, vjp = jax.vjp(f, *r, *p)
        g = vjp(tuple(d))
        out_refs = refs[n_in:]
        for k, (idx, _) in enumerate(want_rows):
            val = g[idx]
            if nadd and k == 0:
                val = val + refs[n_in - 1][...].astype(F32)
            out_refs[k][...] = val.astype(out_refs[k].dtype)
        for k, idx in enumerate(want_params):
            gp = g[nr + idx]
            gp = gp[1] if isinstance(gp, tuple) else gp
            o_ref = out_refs[len(want_rows) + k]

            @pl.when(i == 0)
            def _():
                o_ref[...] = gp

            @pl.when(i > 0)
            def _():
                o_ref[...] += gp

    in_specs = ([_row_spec(tm, r) for r in rows] + [_full_spec(p) for p in params]
                + [pl.BlockSpec((tm, d.shape[1]), lambda i: (i, 0)) for d in douts])
    args = [r[0] for r in rows] + list(params) + list(douts)
    if nadd:
        in_specs.append(pl.BlockSpec((tm, add_to.shape[1]), lambda i: (i, 0)))
        args.append(add_to)
    out_specs = ([pl.BlockSpec((tm, rows[idx][1]), lambda i: (i, 0)) for idx, _ in want_rows]
                 + [_full_spec(params[idx]) for idx in want_params])
    out_shape = ([jax.ShapeDtypeStruct((s, rows[idx][1]), dt) for idx, dt in want_rows]
                 + [jax.ShapeDtypeStruct(params[idx].shape, F32) for idx in want_params])
    res = pl.pallas_call(
        body, name=name, grid=(s // tm,), in_specs=in_specs, out_specs=out_specs, out_shape=out_shape,
        compiler_params=pltpu.CompilerParams(dimension_semantics=("arbitrary",), vmem_limit_bytes=VMEM_LIMIT),
    )(*args)
    return res[:len(want_rows)], res[len(want_rows):]


def f_lnmod(x, sc, sh):
    return (_ln(x) * (1.0 + sc) + sh,)


def f_merge(p, ao, cv, zg0, zg1, zg2, zg3, x, wbd, ps, wbp, wba, wbc, cb, clg, clb, bg, wo, gm, lg, lb):
    zg = jnp.concatenate([zg0, zg1, zg2, zg3], axis=1)
    pm = mm_w(p, wbd) * ps
    co = jax.nn.silu(_ln(cv + cb) * clg + clb)
    y_pool = mm_w(pm, wbp)
    y_attn = mm_w(ao, wba)
    y_conv = mm_w(co, wbc)
    gates = jax.nn.sigmoid(zg + bg)
    merged = (gates[:, :D_MODEL] * y_pool + gates[:, D_MODEL:2 * D_MODEL] * y_attn
              + gates[:, 2 * D_MODEL:] * y_conv)
    mix = mm_w(merged, wo)
    return (_ln(ALPHA * x + gm * mix) * lg + lb,)


def f_relu2(hpre, b1):
    a = jax.nn.relu(hpre + b1)
    return (a * a,)


def f_ffout(x1, ff, b2, gf, lg, lb):
    return (_ln(ALPHA * x1 + gf * (ff + b2)) * lg + lb,)


def mm_big(name, a, b, kind, tiles, out_dtype, j_outer=False, after=None):
    if kind == "nn":
        o0, o1, red = a.shape[0], b.shape[1], a.shape[1]
    elif kind == "nt":
        o0, o1, red = a.shape[0], b.shape[0], a.shape[1]
    else:
        o0, o1, red = a.shape[1], b.shape[1], a.shape[0]
    t0, t1, tr = min(tiles[0], o0), min(tiles[1], o1), min(tiles[2], red)
    if kind == "nn":
        a_spec = pl.BlockSpec((t0, tr), lambda i, j, r: (i, r))
        b_spec = pl.BlockSpec((tr, t1), lambda i, j, r: (r, j))
        dims = (1, 0)
    elif kind == "nt":
        a_spec = pl.BlockSpec((t0, tr), lambda i, j, r: (i, r))
        b_spec = pl.BlockSpec((t1, tr), lambda i, j, r: (j, r))
        dims = (1, 1)
    else:
        a_spec = pl.BlockSpec((tr, t0), lambda i, j, r: (r, i))
        b_spec = pl.BlockSpec((tr, t1), lambda i, j, r: (r, j))
        dims = (0, 0)
    assert o0 % t0 == 0 and o1 % t1 == 0 and red % tr == 0, (name, a.shape, b.shape, tiles)
    n0, n1, nred = o0 // t0, o1 // t1, red // tr
    o_spec = pl.BlockSpec((t0, t1), lambda i, j, r: (i, j))
    if j_outer:
        swap = lambda spec: pl.BlockSpec(spec.block_shape, lambda j, i, r, f=spec.index_map: f(i, j, r))
        a_spec, b_spec, o_spec = swap(a_spec), swap(b_spec), swap(o_spec)
        grid = (n1, n0, nred)
    else:
        grid = (n0, n1, nred)

    deps = [] if after is None else [after]
    dep_specs = [pl.BlockSpec(d.shape, lambda i, j, r, nd=d.ndim: (0,) * nd) for d in deps]
    if nred == 1:
        def body(a_ref, b_ref, *rest):
            o_ref = rest[len(deps)]
            o_ref[...] = _dg(a_ref[...], b_ref[...], *dims).astype(o_ref.dtype)
        scratch = []
    else:
        def body(a_ref, b_ref, *rest):
            o_ref, acc_ref = rest[len(deps):]
            r = pl.program_id(2)
            part = _dg(a_ref[...], b_ref[...], *dims)

            @pl.when(r == 0)
            def _():
                acc_ref[...] = part

            @pl.when(jnp.logical_and(r > 0, r < nred - 1))
            def _():
                acc_ref[...] += part

            @pl.when(r == nred - 1)
            def _():
                o_ref[...] = (acc_ref[...] + part).astype(o_ref.dtype)
        scratch = [pltpu.VMEM((t0, t1), F32)]

    return pl.pallas_call(
        body, name=name, grid=grid,
        in_specs=[a_spec, b_spec] + dep_specs,
        out_specs=o_spec,
        out_shape=jax.ShapeDtypeStruct((o0, o1), out_dtype),
        scratch_shapes=scratch,
        compiler_params=pltpu.CompilerParams(dimension_semantics=("parallel", "parallel", "arbitrary"),
                                             vmem_limit_bytes=VMEM_LIMIT),
    )(a, b, *deps)


def mm_ff1_relu2(name, u2, w1, b1, tm, tn):
    m, k = u2.shape
    n = w1.shape[1]
    tm, tn = min(tm, m), min(tn, n)

    def body(a_ref, b_ref, bias_ref, hpre_ref, h_ref):
        acc = _dg(a_ref[...], b_ref[...], 1, 0)
        hpre_ref[...] = acc
        h_ref[...] = f_relu2(acc, bias_ref[...])[0].astype(h_ref.dtype)

    out = pl.BlockSpec((tm, tn), lambda j, i: (i, j))
    return pl.pallas_call(
        body, name=name, grid=(n // tn, m // tm),
        in_specs=[pl.BlockSpec((tm, k), lambda j, i: (i, 0)), pl.BlockSpec((k, tn), lambda j, i: (0, j)),
                  pl.BlockSpec((1, tn), lambda j, i: (0, j))],
        out_specs=[out, out],
        out_shape=[jax.ShapeDtypeStruct((m, n), F32), jax.ShapeDtypeStruct((m, n), BF16)],
        compiler_params=pltpu.CompilerParams(dimension_semantics=("parallel", "parallel"), vmem_limit_bytes=VMEM_LIMIT),
    )(u2, w1, b1)


def mm_dh_relu2(name, dff, w2, hpre, b1, tm, tn):
    m, k = dff.shape
    n = w2.shape[0]
    tm, tn = min(tm, m), min(tn, n)

    def body(a_ref, b_ref, hpre_ref, bias_ref, d_ref, db_ref):
        i = pl.program_id(1)
        dh = _dg(a_ref[...], b_ref[...], 1, 1)
        _, vjp = jax.vjp(f_relu2, hpre_ref[...], bias_ref[...])
        dhpre, db = vjp((dh,))
        d_ref[...] = dhpre.astype(d_ref.dtype)

        @pl.when(i == 0)
        def _():
            db_ref[...] = db

        @pl.when(i > 0)
        def _():
            db_ref[...] += db

    tile = pl.BlockSpec((tm, tn), lambda j, i: (i, j))
    col = pl.BlockSpec((1, tn), lambda j, i: (0, j))
    return pl.pallas_call(
        body, name=name, grid=(n // tn, m // tm),
        in_specs=[pl.BlockSpec((tm, k), lambda j, i: (i, 0)), pl.BlockSpec((tn, k), lambda j, i: (j, 0)), tile, col],
        out_specs=[tile, col],
        out_shape=[jax.ShapeDtypeStruct((m, n), BF16), jax.ShapeDtypeStruct((1, n), F32)],
        compiler_params=pltpu.CompilerParams(dimension_semantics=("parallel", "arbitrary"), vmem_limit_bytes=VMEM_LIMIT),
    )(dff, w2, hpre, b1)


POOL_PAD = 16
POOL_ROWS = 256


def pool_lin(name, x, wvec, transpose, out_dtype):
    arr, width, cb = x
    s = arr.shape[0]
    n_steps = s // POOL_ROWS

    def body(x_ref, w_ref, o_ref, xp_ref):
        wv = w_ref[...]
        zeros = jnp.zeros((POOL_PAD, width), F32)
        xp_ref[0:POOL_PAD, :] = zeros
        xp_ref[s + POOL_PAD:s + 2 * POOL_PAD, :] = zeros

        def count(t0):
            t = lax.broadcasted_iota(jnp.int32, (POOL_ROWS, width), 0) + (t0 + 1)
            return jnp.minimum(t.astype(F32), wv)

        def fill(i, carry):
            t0 = pl.multiple_of(i * POOL_ROWS, POOL_ROWS)
            v = x_ref[pl.ds(t0, POOL_ROWS), :].astype(F32)
            if transpose:
                v = v / count(t0)
            xp_ref[pl.ds(t0 + POOL_PAD, POOL_ROWS), :] = v
            return carry

        lax.fori_loop(0, n_steps, fill, 0)

        def step(i, carry):
            t0 = pl.multiple_of(i * POOL_ROWS, POOL_ROWS)
            win = xp_ref[pl.ds(t0, POOL_ROWS + 2 * POOL_PAD), :]
            acc = jnp.zeros((POOL_ROWS, width), F32)
            for j in range(POOL_PAD):
                off = POOL_PAD + j if transpose else POOL_PAD - j
                acc = acc + jnp.where(wv > j, win[off:off + POOL_ROWS, :], 0.0)
            cur = x_ref[pl.ds(t0, POOL_ROWS), :].astype(F32)
            res = acc - cur if transpose else acc / count(t0) - cur
            o_ref[pl.ds(t0, POOL_ROWS), :] = res.astype(o_ref.dtype)
            return carry

        lax.fori_loop(0, n_steps, step, 0)

    return pl.pallas_call(
        body, name=name, grid=(1,),
        in_specs=[pl.BlockSpec((s, width), lambda i, cb=cb: (0, cb)), pl.BlockSpec((1, width), lambda i: (0, 0))],
        out_specs=pl.BlockSpec((s, width), lambda i: (0, 0)),
        out_shape=jax.ShapeDtypeStruct((s, width), out_dtype),
        scratch_shapes=[pltpu.VMEM((s + 2 * POOL_PAD, width), F32)],
        compiler_params=pltpu.CompilerParams(dimension_semantics=("arbitrary",), vmem_limit_bytes=VMEM_LIMIT),
    )(arr, wvec)


CONV_PAD = 32
CONV_ROWS = 128


def _glu(a, g):
    return a * jax.nn.sigmoid(g)


def conv_fwd(name, zc, w):
    arr, width, cb = zc
    s = arr.shape[0]
    n_steps = s // CONV_ROWS
    lead = CONV_PAD - (CONV_WIDTH - 1)

    def body(a_ref, g_ref, w_ref, o_ref, hp_ref):
        hp_ref[0:CONV_PAD, :] = jnp.zeros((CONV_PAD, D_CONV), F32)

        def fill(i, carry):
            t0 = pl.multiple_of(i * CONV_ROWS, CONV_ROWS)
            rows = pl.ds(t0, CONV_ROWS)
            hp_ref[pl.ds(t0 + CONV_PAD, CONV_ROWS), :] = _glu(a_ref[rows, :], g_ref[rows, :])
            return carry

        lax.fori_loop(0, n_steps, fill, 0)
        wv = w_ref[...]

        def step(i, carry):
            t0 = pl.multiple_of(i * CONV_ROWS, CONV_ROWS)
            win = hp_ref[pl.ds(t0, CONV_ROWS + CONV_PAD), :]
            acc = jnp.zeros((CONV_ROWS, D_CONV), F32)
            for k in range(CONV_WIDTH):
                acc = acc + wv[k:k + 1, :] * win[lead + k:lead + k + CONV_ROWS, :]
            o_ref[pl.ds(t0, CONV_ROWS), :] = acc
            return carry

        lax.fori_loop(0, n_steps, step, 0)

    return pl.pallas_call(
        body, name=name, grid=(1,),
        in_specs=[pl.BlockSpec((s, width), lambda i, cb=cb: (0, cb)),
                  pl.BlockSpec((s, width), lambda i, cb=cb: (0, cb + 1)), pl.BlockSpec(w.shape, lambda i: (0, 0))],
        out_specs=pl.BlockSpec((s, D_CONV), lambda i: (0, 0)),
        out_shape=jax.ShapeDtypeStruct((s, D_CONV), F32),
        scratch_shapes=[pltpu.VMEM((s + CONV_PAD, D_CONV), F32)],
        compiler_params=pltpu.CompilerParams(dimension_semantics=("arbitrary",), vmem_limit_bytes=VMEM_LIMIT),
    )(arr, arr, w)


def conv_bwd(name, zc, w, dout):
    arr, width, cb = zc
    s = arr.shape[0]
    n_steps = s // CONV_ROWS
    lead = CONV_PAD - (CONV_WIDTH - 1)

    def body(a_ref, g_ref, w_ref, d_ref, dz_ref, dw_ref, hp_ref, dp_ref):
        hp_ref[0:CONV_PAD, :] = jnp.zeros((CONV_PAD, D_CONV), F32)
        dp_ref[s:s + CONV_PAD, :] = jnp.zeros((CONV_PAD, D_CONV), F32)
        dw_ref[...] = jnp.zeros(dw_ref.shape, F32)

        def fill(i, carry):
            t0 = pl.multiple_of(i * CONV_ROWS, CONV_ROWS)
            rows = pl.ds(t0, CONV_ROWS)
            hp_ref[pl.ds(t0 + CONV_PAD, CONV_ROWS), :] = _glu(a_ref[rows, :], g_ref[rows, :])
            dp_ref[rows, :] = d_ref[rows, :]
            return carry

        lax.fori_loop(0, n_steps, fill, 0)
        wv = w_ref[...]

        def step(i, carry):
            t0 = pl.multiple_of(i * CONV_ROWS, CONV_ROWS)
            hwin = hp_ref[pl.ds(t0, CONV_ROWS + CONV_PAD), :]
            dwin = dp_ref[pl.ds(t0, CONV_ROWS + CONV_PAD), :]
            dcur = dwin[0:CONV_ROWS, :]
            dh = jnp.zeros((CONV_ROWS, D_CONV), F32)
            rows = []
            for k in range(CONV_WIDTH):
                rows.append(jnp.sum(dcur * hwin[lead + k:lead + k + CONV_ROWS, :], axis=0, keepdims=True))
                back = CONV_WIDTH - 1 - k
                dh = dh + wv[k:k + 1, :] * dwin[back:back + CONV_ROWS, :]
            rows.append(jnp.zeros((1, D_CONV), F32))
            dw_ref[...] += jnp.concatenate(rows, axis=0)
            rows_now = pl.ds(t0, CONV_ROWS)
            _, vjp = jax.vjp(_glu, a_ref[rows_now, :], g_ref[rows_now, :])
            da, dg = vjp(dh)
            dz_ref[pl.ds(t0, CONV_ROWS), :] = jnp.concatenate([da, dg], axis=1).astype(dz_ref.dtype)
            return carry

        lax.fori_loop(0, n_steps, step, 0)

    return pl.pallas_call(
        body, name=name, grid=(1,),
        in_specs=[pl.BlockSpec((s, width), lambda i, cb=cb: (0, cb)),
                  pl.BlockSpec((s, width), lambda i, cb=cb: (0, cb + 1)),
                  pl.BlockSpec(w.shape, lambda i: (0, 0)), pl.BlockSpec((s, D_CONV), lambda i: (0, 0))],
        out_specs=[pl.BlockSpec((s, 2 * width), lambda i: (0, 0)), pl.BlockSpec(w.shape, lambda i: (0, 0))],
        out_shape=[jax.ShapeDtypeStruct((s, 2 * width), BF16), jax.ShapeDtypeStruct(w.shape, F32)],
        scratch_shapes=[pltpu.VMEM((s + CONV_PAD, D_CONV), F32), pltpu.VMEM((s + CONV_PAD, D_CONV), F32)],
        compiler_params=pltpu.CompilerParams(dimension_semantics=("arbitrary",), vmem_limit_bytes=VMEM_LIMIT),
    )(arr, arr, w, dout)


HEADS_PER_STEP = LANE // HEAD_DIM
CHUNKS_PER_TILE = ATT_TILE // CHUNK
KEY_BLOCKS = N_PREV * CHUNK // ATT_TILE + 1
KEY_SPAN = KEY_BLOCKS * ATT_TILE


def _attn_tile(q, *rest, missing_cols):
    kcat = jnp.concatenate(rest[:KEY_BLOCKS], axis=0)
    vcat = jnp.concatenate(rest[KEY_BLOCKS:2 * KEY_BLOCKS], axis=0)
    bias = rest[2 * KEY_BLOCKS]
    lane = lax.broadcasted_iota(jnp.int32, (1, LANE), 1)
    col = lax.broadcasted_iota(jnp.int32, (1, KEY_SPAN), 1)
    missing = col < missing_cols
    qs = q * (HEAD_DIM ** -0.5)
    o = jnp.zeros((ATT_TILE, LANE), F32)
    for h in range(HEADS_PER_STEP):
        in_head = jnp.logical_and(lane >= h * HEAD_DIM, lane < (h + 1) * HEAD_DIM)
        sc = mm_nt(jnp.where(in_head, qs, 0.0), kcat) + bias[h]
        sc = jnp.where(missing, NEG_INF, sc)
        m = jnp.max(sc, axis=-1, keepdims=True)
        e = jnp.exp(sc - lax.stop_gradient(m))
        p = e / jnp.sum(e, axis=-1, keepdims=True)
        o = o + jnp.where(in_head, mm_nn(p, vcat), 0.0)
    return o


def _missing_cols(n):
    return jnp.maximum((KEY_BLOCKS - 1 - n) * ATT_TILE, 0)


def _attn_in_specs(nt):
    def spec(col0, back):
        return pl.BlockSpec((ATT_TILE, LANE),
                            lambda hp, n, col0=col0, back=back: (jnp.clip(n - back, 0, nt - 1), col0 // LANE + hp))
    backs = list(range(KEY_BLOCKS - 1, -1, -1))
    return ([spec(Z_Q, 0)] + [spec(Z_K, b) for b in backs] + [spec(Z_V, b) for b in backs]
            + [pl.BlockSpec((HEADS_PER_STEP, ATT_TILE, KEY_SPAN), lambda hp, n: (hp, 0, 0))])


def attn_fwd(name, z, bias):
    s = z.shape[0]
    nt = s // ATT_TILE
    n_in = 2 + 2 * KEY_BLOCKS

    def body(*refs):
        o_ref = refs[n_in]
        vals = [r[...] for r in refs[:n_in]]
        o = _attn_tile(*vals, missing_cols=_missing_cols(pl.program_id(1)))
        o_ref[...] = o.astype(o_ref.dtype)

    return pl.pallas_call(
        body, name=name, grid=(N_HEADS // HEADS_PER_STEP, nt),
        in_specs=_attn_in_specs(nt),
        out_specs=pl.BlockSpec((ATT_TILE, LANE), lambda hp, n: (n, hp)),
        out_shape=jax.ShapeDtypeStruct((s, D_ATTN), BF16),
        compiler_params=pltpu.CompilerParams(dimension_semantics=("parallel", "parallel"), vmem_limit_bytes=VMEM_LIMIT),
    )(*([z] * (n_in - 1)), bias)


def attn_bwd(name, z, bias, do, after=None):
    s = z.shape[0]
    nt = s // ATT_TILE
    n_in = 2 + 2 * KEY_BLOCKS
    nc = KEY_BLOCKS - 1

    deps = [] if after is None else [after]
    n_out = n_in + 1 + len(deps)

    def body(*refs):
        do_ref = refs[n_in]
        dq_ref, dk_ref, dv_ref, db_ref = refs[n_out:n_out + 4]
        kacc, vacc = refs[n_out + 4:n_out + 4 + nc], refs[n_out + 4 + nc:]
        n = pl.program_id(1)

        @pl.when(n == 0)
        def _():
            db_ref[...] = jnp.zeros(db_ref.shape, F32)
            for acc in (*kacc, *vacc):
                acc[...] = jnp.zeros(acc.shape, F32)

        def shift(out_ref, accs, contrib):
            @pl.when(n >= nc)
            def _():
                first = accs[0][...] if contrib is None else accs[0][...] + contrib[0]
                out_ref[...] = first.astype(out_ref.dtype)
            for j in range(nc - 1):
                accs[j][...] = accs[j + 1][...] if contrib is None else accs[j + 1][...] + contrib[j + 1]
            if contrib is not None:
                accs[nc - 1][...] = contrib[nc]

        @pl.when(n < nt)
        def _():
            fn = functools.partial(_attn_tile, missing_cols=_missing_cols(n))
            _, vjp = jax.vjp(fn, *[r[...] for r in refs[:n_in]])
            grads = vjp(do_ref[...].astype(F32))
            dq_ref[...] = grads[0].astype(dq_ref.dtype)
            db_ref[...] += grads[n_in - 1]
            shift(dk_ref, kacc, grads[1:1 + KEY_BLOCKS])
            shift(dv_ref, vacc, grads[1 + KEY_BLOCKS:1 + 2 * KEY_BLOCKS])

        @pl.when(n >= nt)
        def _():
            shift(dk_ref, kacc, None)
            shift(dv_ref, vacc, None)

    o_cur = pl.BlockSpec((ATT_TILE, LANE), lambda hp, n: (jnp.minimum(n, nt - 1), hp))
    o_old = pl.BlockSpec((ATT_TILE, LANE), lambda hp, n: (jnp.maximum(n - nc, 0), hp))
    b_spec = pl.BlockSpec((HEADS_PER_STEP, ATT_TILE, KEY_SPAN), lambda hp, n: (hp, 0, 0))
    return pl.pallas_call(
        body, name=name, grid=(N_HEADS // HEADS_PER_STEP, nt + nc),
        in_specs=_attn_in_specs(nt) + [o_cur] + [pl.BlockSpec(d.shape, lambda hp, n: (0, 0)) for d in deps],
        out_specs=[o_cur, o_old, o_old, b_spec],
        out_shape=[jax.ShapeDtypeStruct((s, D_ATTN), BF16)] * 3 + [jax.ShapeDtypeStruct((N_HEADS, ATT_TILE, KEY_SPAN), F32)],
        scratch_shapes=[pltpu.VMEM((ATT_TILE, LANE), F32)] * (2 * nc),
        compiler_params=pltpu.CompilerParams(dimension_semantics=("parallel", "arbitrary"), vmem_limit_bytes=VMEM_LIMIT),
    )(*([z] * (n_in - 1)), bias, do, *deps)


def loss_head(name, y, tgt, tm):
    s, d = y.shape

    def body(y_ref, t_ref, l_ref, dy_ref):
        i = pl.program_id(0)
        diff = y_ref[...] - t_ref[...]
        dy_ref[...] = diff * (1.0 / d)
        part = 0.5 * jnp.sum(jnp.mean(diff * diff, axis=-1, keepdims=True), axis=0, keepdims=True)

        @pl.when(i == 0)
        def _():
            l_ref[...] = jnp.zeros(l_ref.shape, F32)

        l_ref[...] += jnp.broadcast_to(part, l_ref.shape)

    row = pl.BlockSpec((tm, d), lambda i: (i, 0))
    return pl.pallas_call(
        body, name=name, grid=(s // tm,), in_specs=[row, row],
        out_specs=[pl.BlockSpec((8, LANE), lambda i: (0, 0)), row],
        out_shape=[jax.ShapeDtypeStruct((8, LANE), F32), jax.ShapeDtypeStruct((s, d), F32)],
        compiler_params=pltpu.CompilerParams(dimension_semantics=("arbitrary",), vmem_limit_bytes=VMEM_LIMIT),
    )(y, tgt)


def adamw(name, w, g, m, v, tr):
    r, c = w.shape
    assert r % tr == 0, (name, w.shape, tr)

    def body(w_ref, g_ref, m_ref, v_ref, d_ref, nm_ref, nv_ref):
        gg = g_ref[...]
        m2 = ADAM_B1 * m_ref[...] + (1.0 - ADAM_B1) * gg
        v2 = ADAM_B2 * v_ref[...] + (1.0 - ADAM_B2) * (gg * gg)
        m_hat = m2 / (1.0 - ADAM_B1 ** ADAM_STEP)
        v_hat = v2 / (1.0 - ADAM_B2 ** ADAM_STEP)
        d_ref[...] = -ADAM_LR * (m_hat / (jnp.sqrt(v_hat) + ADAM_EPS) + ADAM_WD * w_ref[...])
        nm_ref[...] = m2
        nv_ref[...] = v2

    blk = pl.BlockSpec((tr, c), lambda i: (i, 0))
    return pl.pallas_call(
        body, name=name, grid=(r // tr,), in_specs=[blk] * 4, out_specs=[blk] * 3,
        out_shape=[jax.ShapeDtypeStruct((r, c), F32)] * 3,
        compiler_params=pltpu.CompilerParams(dimension_semantics=("parallel",), vmem_limit_bytes=VMEM_LIMIT),
    )(w, g, m, v)


def adamw_sum(name, w, layer_blocks, layer_own, m, v, tr):
    rows, c = w.shape
    nl = len(layer_blocks)
    nb, r, _ = layer_blocks[0].shape
    assert rows == nl * r and r % tr == 0, (name, w.shape, layer_blocks[0].shape, tr)
    per = r // tr

    def body(*refs):
        w_ref, b_refs, o_refs = refs[0], refs[1:1 + nl], refs[1 + nl:1 + 2 * nl]
        m_ref, v_ref, g_ref, d_ref, nm_ref, nv_ref = refs[1 + 2 * nl:]
        i = pl.program_id(0)
        me = _idx(_me())

        def update(b_ref, own_ref):
            gg = None
            for j in range(nb):
                blk = jnp.where(j == me, own_ref[...].astype(F32), b_ref[jnp.where(j == me, (j + 1) % nb, j)].astype(F32))
                gg = blk if gg is None else gg + blk
            g_ref[...] = gg
            m2 = ADAM_B1 * m_ref[...] + (1.0 - ADAM_B1) * gg
            v2 = ADAM_B2 * v_ref[...] + (1.0 - ADAM_B2) * (gg * gg)
            m_hat = m2 / (1.0 - ADAM_B1 ** ADAM_STEP)
            v_hat = v2 / (1.0 - ADAM_B2 ** ADAM_STEP)
            d_ref[...] = -ADAM_LR * (m_hat / (jnp.sqrt(v_hat) + ADAM_EPS) + ADAM_WD * w_ref[...])
            nm_ref[...] = m2
            nv_ref[...] = v2

        for l in range(nl):
            pl.when(jnp.logical_and(i >= l * per, i < (l + 1) * per))(functools.partial(update, b_refs[l], o_refs[l]))

    blk = pl.BlockSpec((tr, c), lambda i: (i, 0))
    b_specs = [pl.BlockSpec((nb, tr, c), lambda i, l=l: (0, jnp.clip(i - l * per, 0, per - 1), 0)) for l in range(nl)]
    o_specs = [pl.BlockSpec((tr, c), lambda i, l=l: (jnp.clip(i - l * per, 0, per - 1), 0)) for l in range(nl)]
    return pl.pallas_call(
        body, name=name, grid=(rows // tr,),
        in_specs=[blk] + b_specs + o_specs + [blk, blk], out_specs=[blk] * 4,
        out_shape=[jax.ShapeDtypeStruct((rows, c), F32)] * 4,
        compiler_params=pltpu.CompilerParams(dimension_semantics=("arbitrary",), vmem_limit_bytes=VMEM_LIMIT),
    )(w, *layer_blocks, *layer_own, m, v)


def sum_blocks(name, blocks, tr):
    nb, r, c = blocks.shape
    assert r % tr == 0, (name, blocks.shape, tr)

    def body(b_ref, o_ref):
        acc = b_ref[0].astype(F32)
        for j in range(1, nb):
            acc = acc + b_ref[j].astype(F32)
        o_ref[...] = acc

    return pl.pallas_call(
        body, name=name, grid=(r // tr,),
        in_specs=[pl.BlockSpec((nb, tr, c), lambda i: (0, i, 0))],
        out_specs=pl.BlockSpec((tr, c), lambda i: (i, 0)),
        out_shape=jax.ShapeDtypeStruct((r, c), F32),
        compiler_params=pltpu.CompilerParams(dimension_semantics=("parallel",), vmem_limit_bytes=VMEM_LIMIT),
    )(blocks)


FLIPS = [(0, 0, 1), (1, 0, 0), (0, 1, 0), (1, 1, 0), (1, 0, 1), (0, 1, 1), (1, 1, 1)]
ANY = pl.BlockSpec(memory_space=pl.ANY)


def _me():
    return lax.axis_index("x"), lax.axis_index("y"), lax.axis_index("c")


def _flip(pos, f):
    return tuple((1 - p) if fi else p for p, fi in zip(pos, f))


def _idx(pos):
    return 4 * pos[0] + 2 * pos[1] + pos[2]


def all_gather_multi(name, shards):
    n = len(shards)

    def body(*refs):
        x_refs, out_refs, token = refs[:n], refs[n:2 * n], refs[2 * n]
        send_sems, recv_sems, local_sems = refs[2 * n + 1:]
        token[...] = jnp.zeros_like(token)
        x, y, cc = _me()
        me, sibling = (x, y, cc), (x, y, 1 - cc)
        chips = [(1 - x, y), (x, 1 - y), (1 - x, 1 - y)]

        def copy(a, k, block, to, src=None):
            dst = out_refs[a].at[_idx(block)]
            return pltpu.make_async_remote_copy(
                src_ref=dst if src is None else src, dst_ref=dst, send_sem=send_sems.at[7 * a + k],
                recv_sem=recv_sems.at[7 * a + k], device_id=to, device_id_type=MESH)

        mine = [pltpu.make_async_copy(x_refs[a], out_refs[a].at[_idx(me)], local_sems.at[a]) for a in range(n)]
        for cp in mine:
            cp.start()
        first = []
        for a in range(n):
            first.append(copy(a, 0, me, sibling, src=x_refs[a]))
            first += [copy(a, 1 + j, me, (*chip, cc), src=x_refs[a]) for j, chip in enumerate(chips)]
        for cp in first:
            cp.start()
        passed = []
        for j, chip in enumerate(chips):
            for a in range(n):
                copy(a, 1 + j, (*chip, cc), me).wait_recv()
                fwd = copy(a, 4 + j, (*chip, cc), sibling)
                fwd.start()
                passed.append(fwd)
        for a in range(n):
            copy(a, 0, sibling, me).wait_recv()
            for j, chip in enumerate(chips):
                copy(a, 4 + j, (*chip, 1 - cc), me).wait_recv()
        for cp in first + passed:
            cp.wait_send()
        for cp in mine:
            cp.wait()

    return pl.pallas_call(
        body, name=name, in_specs=[ANY] * n, out_specs=[ANY] * n + [pl.BlockSpec(memory_space=pltpu.VMEM)],
        out_shape=[jax.ShapeDtypeStruct((N_DEV,) + a.shape, a.dtype) for a in shards]
        + [jax.ShapeDtypeStruct((8, LANE), F32)],
        scratch_shapes=[pltpu.SemaphoreType.DMA((7 * n,)), pltpu.SemaphoreType.DMA((7 * n,)),
                        pltpu.SemaphoreType.DMA((n,))],
    )(*shards)


HBM = pl.BlockSpec(memory_space=pltpu.HBM)
SEM = pl.BlockSpec(memory_space=pltpu.SEMAPHORE)
DATAFLOW = pltpu.SideEffectType.DATAFLOW_SIDE_EFFECTING


def _exchange_copies(a_refs, l_refs, send_sems, recv_sems, gather):
    me = _me()
    mi = _idx(me)
    out = []
    for k, f in enumerate(FLIPS):
        peer = _flip(me, f)
        for a in range(len(a_refs)):
            src = a_refs[a] if gather else a_refs[a].at[_idx(peer)]
            out.append(pltpu.make_async_remote_copy(
                src_ref=src, dst_ref=l_refs[a].at[mi], send_sem=send_sems.at[7 * a + k],
                recv_sem=recv_sems.at[7 * a + k], device_id=peer, device_id_type=MESH))
    return out


def exchange_start(name, arrays, gather):
    n = len(arrays)
    lands = [lax.empty(((N_DEV,) + a.shape) if gather else a.shape, a.dtype) for a in arrays]

    def body(*refs):
        a_refs, l_refs = refs[:n], refs[n:2 * n]
        send_sems, recv_sems = refs[2 * n], refs[2 * n + 1]
        token = refs[4 * n + 2]
        for cp in _exchange_copies(a_refs, l_refs, send_sems, recv_sems, gather):
            cp.start()
        token[...] = jnp.zeros_like(token)

    hbm = lambda a: pltpu.HBM(a.shape, a.dtype)
    res = pl.pallas_call(
        body, name=name,
        out_shape=(pltpu.SemaphoreType.DMA((7 * n,)), pltpu.SemaphoreType.DMA((7 * n,)),
                   *[hbm(a) for a in arrays], *[hbm(a) for a in lands], jax.ShapeDtypeStruct((8, LANE), F32)),
        in_specs=[HBM] * (2 * n),
        out_specs=(SEM, SEM, *[HBM] * (2 * n), pl.BlockSpec(memory_space=pltpu.VMEM)),
        input_output_aliases={i: i + 2 for i in range(2 * n)},
        compiler_params=pltpu.CompilerParams(has_side_effects=DATAFLOW),
    )(*[pltpu.with_memory_space_constraint(a, pltpu.HBM) for a in arrays],
      *[pltpu.with_memory_space_constraint(a, pltpu.HBM) for a in lands])
    return res[0], res[1], list(res[2:2 + n]), list(res[2 + n:2 + 2 * n]), res[-1]


def exchange_wait(name, send_sems, recv_sems, arrays, lands, after, gather):
    n = len(arrays)
    after = list(after) if isinstance(after, (list, tuple)) else [after]

    def body(*refs):
        a_refs, l_refs = refs[:n], refs[n:2 * n]
        ssem, rsem = refs[2 * n], refs[2 * n + 1]
        for cp in _exchange_copies(a_refs, l_refs, ssem, rsem, gather):
            cp.wait_send()
            cp.wait_recv()

    hbm = lambda a: pltpu.HBM(a.shape, a.dtype)
    res = pl.pallas_call(
        body, name=name,
        out_shape=(*[hbm(a) for a in arrays], *[hbm(a) for a in lands]),
        in_specs=[HBM] * (2 * n) + [SEM, SEM] + [pl.BlockSpec(memory_space=pl.ANY)] * len(after),
        out_specs=tuple([HBM] * (2 * n)),
        input_output_aliases={i: i for i in range(2 * n)},
        compiler_params=pltpu.CompilerParams(has_side_effects=DATAFLOW),
    )(*arrays, *lands, send_sems, recv_sems, *after)
    return list(res[n:])


def ada_fwd(name, c_row, w_cat, b_lay):
    d = c_row.shape[1]
    ncol = w_cat.shape[1]
    vmem = pl.BlockSpec(memory_space=pltpu.VMEM)

    def body(c_ref, w_ref, b_ref, mod_ref, cact_ref, token, call, send, land, s1, r1, s2, r2):
        token[...] = jnp.zeros_like(token)
        me = _me()
        mi = _idx(me)
        call[mi] = c_ref[...]

        def exchange(src_of, dst_buf, ssem, rsem):
            sends, recvs = [], []
            for k, f in enumerate(FLIPS):
                peer = _flip(me, f)
                sends.append(pltpu.make_async_remote_copy(
                    src_ref=src_of(peer), dst_ref=dst_buf.at[mi], send_sem=ssem.at[k], recv_sem=rsem.at[k],
                    device_id=peer, device_id_type=MESH))
                recvs.append(pltpu.make_async_remote_copy(
                    src_ref=src_of(peer), dst_ref=dst_buf.at[_idx(peer)], send_sem=ssem.at[k], recv_sem=rsem.at[k],
                    device_id=peer, device_id_type=MESH))
            for cp in sends:
                cp.start()
            for cp in recvs:
                cp.wait_recv()
            for cp in sends:
                cp.wait_send()

        exchange(lambda peer: c_ref, call, s1, r1)
        for p in range(N_DEV):
            cact_ref[pl.ds(p, 1), :] = jax.nn.silu(call[p])
        res = _dg(cact_ref[...], w_ref[...], 1, 0)
        for p in range(N_DEV):
            send[p] = res[p:p + 1, :]
        land[mi] = send[mi]
        exchange(lambda peer: send.at[_idx(peer)], land, s2, r2)
        mod_ref[...] = land[...] + b_ref[...]

    return pl.pallas_call(
        body, name=name, in_specs=[vmem, vmem, vmem], out_specs=[vmem, vmem, vmem],
        out_shape=[jax.ShapeDtypeStruct((N_DEV, 1, ncol), F32), jax.ShapeDtypeStruct((N_DEV, d), F32),
                   jax.ShapeDtypeStruct((8, LANE), F32)],
        scratch_shapes=[pltpu.VMEM((N_DEV, 1, d), F32), pltpu.VMEM((N_DEV, 1, ncol), F32),
                        pltpu.VMEM((N_DEV, 1, ncol), F32),
                        pltpu.SemaphoreType.DMA((7,)), pltpu.SemaphoreType.DMA((7,)),
                        pltpu.SemaphoreType.DMA((7,)), pltpu.SemaphoreType.DMA((7,))],
        compiler_params=pltpu.CompilerParams(vmem_limit_bytes=VMEM_LIMIT),
    )(c_row, w_cat, b_lay)


POOL_WINDOWS = (2, 4, 8, 16)
POOL_GROUP = 64
N_REL = 2 * REL_CLIP + 1
SMALL_NAMES = ["b_ada", "b_gate", "w_pool", "pool_scale", "rel_bias", "conv_w", "conv_b", "conv_ln_g",
               "conv_ln_b", "ln_mix_g", "ln_mix_b", "b_ff1", "b_ff2", "ln_ff_g", "ln_ff_b"]
BIG_NAMES = ["w_in", "w_br_pool", "w_br_attn", "w_br_conv", "w_o", "w_ff1", "w_ff2"]
ROW_SHARDED = ("w_in", "w_o", "w_ff2")
WEIGHT_NAMES = ["w_ada", "b_ada", "w_in", "b_gate", "w_pool", "pool_scale", "rel_bias", "conv_w", "conv_b",
                "conv_ln_g", "conv_ln_b", "w_br_pool", "w_br_attn", "w_br_conv", "w_o", "ln_mix_g", "ln_mix_b",
                "w_ff1", "b_ff1", "w_ff2", "b_ff2", "ln_ff_g", "ln_ff_b"]


def _bias_table(rel_bias):
    far = jnp.broadcast_to(rel_bias[:, 2 * REL_CLIP:], (N_HEADS, BAND - REL_CLIP))
    near = rel_bias[:, REL_CLIP - CHUNK + 1:2 * REL_CLIP][:, ::-1]
    ext = jnp.concatenate([far, near, jnp.zeros((N_HEADS, 1), F32)], axis=1)
    length = BAND + CHUNK
    flat = jnp.tile(ext, (1, CHUNK + 1))
    skew = flat[:, CHUNK - 1:CHUNK - 1 + CHUNK * (length - 1)].reshape(N_HEADS, CHUNK, length - 1)
    return skew[:, :, :BAND]


def _bias_full(rel_bias):
    tab = _bias_table(rel_bias)
    return jnp.concatenate(
        [jnp.pad(tab, ((0, 0), (0, 0), (i * CHUNK, KEY_SPAN - BAND - i * CHUNK)), constant_values=NEG_INF)
         for i in range(CHUNKS_PER_TILE)], axis=1)


def _block_diag(w_pool):
    out = jnp.zeros((D_POOL, D_POOL), F32)
    for g in range(len(POOL_WINDOWS)):
        out = lax.dynamic_update_slice(out, w_pool[g], (g * POOL_GROUP, g * POOL_GROUP))
    return out


def _flat_pad(arrs, mult):
    flat = jnp.concatenate([a.reshape(-1) for a in arrs])
    pad = (-flat.shape[0]) % mult
    return jnp.pad(flat, (0, pad)) if pad else flat


def _unflat(flat, shapes):
    out, off = [], 0
    for shp in shapes:
        n = int(np.prod(shp))
        out.append(flat[off:off + n].reshape(shp))
        off += n
    return out


def _to_blocks(name, full):
    k, n = full.shape
    if name in ROW_SHARDED:
        return full.reshape(N_DEV, k // N_DEV, n)
    return full.reshape(k, N_DEV, n // N_DEV).transpose(1, 0, 2)


def _from_blocks(name, blocks):
    nb, r, c = blocks.shape
    if name in ROW_SHARDED:
        return blocks.reshape(nb * r, c)
    return blocks.transpose(1, 0, 2).reshape(r, nb * c)


class _Layer:
    pass


def _row(v):
    return v.reshape(1, -1)


def _f_merge_ln(*args):
    (x1,) = f_merge(*args[:-2])
    return x1, f_lnmod(x1, *args[-2:])[0]


def _f_ffout_ln(*args):
    (x2,) = f_ffout(*args[:-2])
    return x2, f_lnmod(x2, *args[-2:])[0]


def _layer_fwd(x, modr, w, wvec, fetch_rest, u=None, next_mod=None):
    sh_m, sc_m, g_m, sh_f, sc_f, g_f = modr
    if u is None:
        (u,) = row_fwd("lnmod_mix", f_lnmod, [x], [sc_m, sh_m], [(D_MODEL, BF16)], 512)
    z = mm_big("mm_in", u, w.w_in_t, "nt", (1024, 896, 1024), F32, j_outer=True)
    p = pool_lin("pool_fwd", (z, D_POOL, Z_POOL // D_POOL), wvec, False, F32)
    ao = attn_fwd("attn_fwd", z, w.bias)
    cv = conv_fwd("conv_fwd", (z, D_CONV, Z_CONV // D_CONV), w.conv_w)
    if not hasattr(w, "wo"):
        fetch_rest(w, cv)
    mparams = [w.wbd, w.ps, w.wbp, w.wba, w.wbc, w.cb, w.clg, w.clb, w.bg, w.wo, g_m, w.lmg, w.lmb]
    gate_blocks = [(z, GATE_BLOCK, Z_GATE // GATE_BLOCK + k) for k in range(3 * D_MODEL // GATE_BLOCK)]
    x1, u2 = row_fwd("merge", _f_merge_ln, [p, ao, cv, *gate_blocks, x], mparams + [sc_f, sh_f],
                     [(D_MODEL, F32), (D_MODEL, BF16)], 512)
    if not hasattr(w, "w_ff1"):
        fetch_rest(w, u2)
    hpre, h = mm_ff1_relu2("mm_ff1", u2, w.w_ff1, w.b1, 1024, 1024)
    ff = mm_big("mm_ff2", h, w.w_ff2, "nn", (1024, 1024, 4096), F32)
    fparams = [w.b2, g_f, w.lfg, w.lfb]
    if next_mod is None:
        (x2,), u_next = row_fwd("ffout", f_ffout, [x1, ff], fparams, [(D_MODEL, F32)], 512), None
    else:
        x2, u_next = row_fwd("ffout_ln", _f_ffout_ln, [x1, ff], fparams + list(next_mod),
                             [(D_MODEL, F32), (D_MODEL, BF16)], 512)
    return x2, u_next, (x, u, z, p, ao, cv, x1, u2, hpre, h, ff, mparams)


GRAD_GROUPS = [("ff", ["w_ff2", "w_ff1"]), ("mix", ["w_o", "w_br_pool", "w_br_attn", "w_br_conv"]), ("in", ["w_in"])]


def _layer_bwd(dx2, saved, modr, w, wvec, ready):
    x, u, z, p, ao, cv, x1, u2, hpre, h, ff, mparams = saved
    sh_m, sc_m, g_m, sh_f, sc_f, g_f = modr
    g = {}
    (dx1a, dff), (g["b_ff2"], dgf, g["ln_ff_g"], g["ln_ff_b"]) = row_bwd(
        "ffout_bwd", f_ffout, [x1, ff], [w.b2, g_f, w.lfg, w.lfb], [dx2], 512, [(0, F32), (1, BF16)], [0, 1, 2, 3])
    dhpre, g["b_ff1"] = mm_dh_relu2("mm_dh", dff, w.w_ff2, hpre, w.b1, 1024, 1024)
    g["w_ff2"] = mm_big("mm_dw_ff2", h, dff, "tn", (1024, 1024, 2048), BF16)
    du2 = mm_big("mm_du2", dhpre, w.w_ff1, "nt", (1024, 1024, 4096), F32)
    g["w_ff1"] = mm_big("mm_dw_ff1", u2, dhpre, "tn", (1024, 1024, 2048), BF16)
    sc_f = sc_f + ready("ff", g)[0:1, 0:1]
    (dx1,), (dscf, dshf) = row_bwd("lnmod_ff_bwd", f_lnmod, [x1], [sc_f, sh_f], [du2], 512, [(0, F32)], [0, 1],
                                   add_to=dx1a)
    gate_blocks = [(z, GATE_BLOCK, Z_GATE // GATE_BLOCK + k) for k in range(3 * D_MODEL // GATE_BLOCK)]
    (dp, dao, dcv, *dzg, dxa), dm = row_bwd(
        "merge_bwd", f_merge, [p, ao, cv, *gate_blocks, x], mparams, [dx1], 256,
        [(0, F32), (1, BF16), (2, F32), (3, BF16), (4, BF16), (5, BF16), (6, BF16), (7, F32)], list(range(13)))
    (dwbd, g["pool_scale"], g["w_br_pool"], g["w_br_attn"], g["w_br_conv"], g["conv_b"], g["conv_ln_g"],
     g["conv_ln_b"], g["b_gate"], g["w_o"], dgm, g["ln_mix_g"], g["ln_mix_b"]) = dm
    g["w_pool"] = jnp.stack([dwbd[i * POOL_GROUP:(i + 1) * POOL_GROUP, i * POOL_GROUP:(i + 1) * POOL_GROUP]
                             for i in range(len(POOL_WINDOWS))])
    tok = ready("mix", g)
    dzp = pool_lin("pool_bwd", (dp, D_POOL, 0), wvec + tok[0:1, 0:1], True, BF16)
    dq, dk, dv, dbias = attn_bwd("attn_bwd", z, w.bias, dao, after=tok)
    (g["rel_bias"],) = w.bias_vjp(dbias)
    dzc, dcw = conv_bwd("conv_bwd", (z, D_CONV, Z_CONV // D_CONV), w.conv_w + tok[0:1, 0:1], dcv)
    g["conv_w"] = dcw[:CONV_WIDTH]
    dz = jnp.concatenate([dzp, dq, dk, dv, dzc, *dzg], axis=1)
    du = mm_big("mm_du", dz, w.w_in_t, "nn", (512, 1024, D_IN), F32)
    (dx,), (dscm, dshm) = row_bwd("lnmod_mix_bwd", f_lnmod, [x], [sc_m, sh_m], [du], 512, [(0, F32)], [0, 1],
                                  add_to=dxa)
    g["dmod"] = jnp.concatenate([dshm, dscm, dgm, dshf, dscf, dgf], axis=1)
    tok = ready("small", g)
    g["w_in"] = mm_big("mm_dw_in", dz, u, "tn", (896, 1024, 2048), BF16, after=tok)
    return dx, g, ready("in", g)


def kernel(x, c, w_ada, b_ada, w_in, b_gate, w_pool, pool_scale, rel_bias, conv_w, conv_b, conv_ln_g, conv_ln_b, w_br_pool, w_br_attn, w_br_conv, w_o, ln_mix_g, ln_mix_b, w_ff1, b_ff1, w_ff2, b_ff2, ln_ff_g, ln_ff_b, loss_target, m_w_ada, m_b_ada, m_w_in, m_b_gate, m_w_pool, m_pool_scale, m_rel_bias, m_conv_w, m_conv_b, m_conv_ln_g, m_conv_ln_b, m_w_br_pool, m_w_br_attn, m_w_br_conv, m_w_o, m_ln_mix_g, m_ln_mix_b, m_w_ff1, m_b_ff1, m_w_ff2, m_b_ff2, m_ln_ff_g, m_ln_ff_b, v_w_ada, v_b_ada, v_w_in, v_b_gate, v_w_pool, v_pool_scale, v_rel_bias, v_conv_w, v_conv_b, v_conv_ln_g, v_conv_ln_b, v_w_br_pool, v_w_br_attn, v_w_br_conv, v_w_o, v_ln_mix_g, v_ln_mix_b, v_w_ff1, v_b_ff1, v_w_ff2, v_b_ff2, v_ln_ff_g, v_ln_ff_b):
    args = dict(locals())
    wts = {n: args[n] for n in WEIGHT_NAMES}
    mom = {n: args["m_" + n] for n in WEIGHT_NAMES}
    var = {n: args["v_" + n] for n in WEIGHT_NAMES}
    me = 4 * lax.axis_index("x") + 2 * lax.axis_index("y") + lax.axis_index("c")
    xs, tgt = x[0], loss_target[0]
    nc_ada = w_ada.shape[2]
    wvec = jnp.asarray(np.repeat(np.array(POOL_WINDOWS, np.float32), POOL_GROUP)[None, :])

    w_cat = jnp.concatenate([w_ada[l] for l in range(DEPTH)], axis=1)
    b_lay = b_ada.reshape(DEPTH, N_DEV, nc_ada).transpose(1, 0, 2).reshape(N_DEV, 1, DEPTH * nc_ada)
    land, cact, ada_token = ada_fwd("ada_fwd", c, w_cat, b_lay)
    mod = land.reshape(N_DEV, DEPTH, nc_ada).transpose(1, 0, 2).reshape(DEPTH, 6 * D_MODEL)
    modr = [[mod[l:l + 1, i * D_MODEL:(i + 1) * D_MODEL] for i in range(6)] for l in range(DEPTH)]

    cw_pack = _flat_pad([conv_w], 8 * LANE).reshape(-1, LANE) + ada_token[0:1, 0:1]
    xlayout = lambda n, a: jnp.swapaxes(a, -1, -2) if n == "w_in" else a
    shards = [[xlayout(n, wts[n][l]).astype(BF16) for n in BIG_NAMES] for l in range(DEPTH)]
    n_first = [BIG_NAMES.index("w_ff1"), 1]
    *first0, cw_all, token = all_gather_multi("gather_first_l0", shards[0][:n_first[0]] + [cw_pack])
    cw_all = cw_all.reshape(N_DEV, -1)[:, :conv_w.size]
    conv_full = cw_all.reshape((N_DEV,) + conv_w.shape).transpose(1, 2, 0, 3).reshape(DEPTH, CONV_WIDTH, D_CONV)
    gathers = {}
    for key, arrs in [("rest_l0", shards[0][n_first[0]:]), ("first_l1", shards[1][:n_first[1]]),
                      ("rest_l1", shards[1][n_first[1]:])]:
        arrs = [a + token[0:1, 0:1].astype(a.dtype) for a in arrs]
        ssem, rsem, thru, lands, token = exchange_start(f"gather_{key}_start", arrs, True)
        gathers[key] = (arrs, ssem, rsem, thru, lands)
    modr[0] = [r + token[0:1, 0:1] for r in modr[0]]

    def with_own(lands_, own):
        return [lax.dynamic_update_index_in_dim(ld, o, me, axis=0) for ld, o in zip(lands_, own)]

    def gathered(key, after):
        arrs, ssem, rsem, thru, lands = gathers[key]
        return with_own(exchange_wait(f"gather_{key}_wait", ssem, rsem, thru, lands, after, True), arrs)

    attr = dict(w_in="w_in_t", w_br_pool="wbp", w_br_attn="wba", w_br_conv="wbc", w_o="wo", w_ff1="w_ff1", w_ff2="w_ff2")

    def assign(w, names, blocks):
        for n, g in zip(names, blocks):
            setattr(w, attr[n], _from_blocks(n, g))

    def fetch_rest_for(l):
        return lambda w, after: assign(w, BIG_NAMES[n_first[l]:], gathered(f"rest_l{l}", after))

    def layer_weights(l, first_blocks):
        w = _Layer()
        assign(w, BIG_NAMES[:n_first[l]], first_blocks)
        w.wbd = _block_diag(w_pool[l])
        w.ps, w.cb, w.clg, w.clb = _row(pool_scale[l]), _row(conv_b[l]), _row(conv_ln_g[l]), _row(conv_ln_b[l])
        w.bg, w.lmg, w.lmb = _row(b_gate[l]), _row(ln_mix_g[l]), _row(ln_mix_b[l])
        w.b1, w.b2, w.lfg, w.lfb = _row(b_ff1[l]), _row(b_ff2[l]), _row(ln_ff_g[l]), _row(ln_ff_b[l])
        w.conv_w = jnp.pad(conv_full[l], ((0, CONV_PAD - CONV_WIDTH), (0, 0)))
        w.bias, w.bias_vjp = jax.vjp(_bias_full, rel_bias[l])
        return w

    layers, saved = [layer_weights(0, first0)], []
    h, u1, sv = _layer_fwd(xs, modr[0], layers[0], wvec, fetch_rest_for(0), next_mod=(modr[1][1], modr[1][0]))
    saved.append(sv)
    layers.append(layer_weights(1, gathered("first_l1", h)))
    h, _, sv = _layer_fwd(h, modr[1], layers[1], wvec, fetch_rest_for(1), u=u1)
    saved.append(sv)
    lpart, dy = loss_head("loss_head", h, tgt, 512)
    loss = lax.psum(lpart[0, 0], ("x", "y", "c"))
    grads = [None] * DEPTH
    pending = {}
    small_shapes = [wts[n].shape if n != "conv_w" else (DEPTH, CONV_WIDTH, D_CONV) for n in SMALL_NAMES]

    def ready_for(l):
        def ready(group, g):
            if group == "small":
                if l > 0:
                    return None
                both = [g, grads[1]]
                local = [jnp.concatenate([both[k]["dmod"] for k in range(DEPTH)], axis=0)]
                local += [jnp.stack([both[k][n].reshape(shp[1:]) for k in range(DEPTH)])
                          for n, shp in zip(SMALL_NAMES[1:], small_shapes[1:])]
                pack = _flat_pad(local, 8 * LANE).reshape(-1, LANE)
                ssem, rsem, thru, lands, token = exchange_start("gather_small_grads_start", [pack], True)
                pending["small"] = (pack, ssem, rsem, thru, lands)
                return token
            names = dict(GRAD_GROUPS)[group]
            blocks = [_to_blocks(n, g[n]).astype(BF16) for n in names]
            ssem, rsem, thru, lands, token = exchange_start(f"scatter_l{l}_{group}_start", blocks, False)
            pending[(l, group)] = (names, blocks, ssem, rsem, thru, lands)
            return token
        return ready

    def received(l, group, after):
        names, blocks, ssem, rsem, thru, lands = pending[(l, group)]
        lands = exchange_wait(f"scatter_l{l}_{group}_wait", ssem, rsem, thru, lands, after, False)
        own = [lax.dynamic_index_in_dim(b, me, axis=0, keepdims=False) for b in blocks]
        return dict(zip(names, zip(lands, own)))

    dy, grads[1], token = _layer_bwd(dy, saved[1], modr[1], layers[1], wvec, ready_for(1))
    modr0 = [r + token[0:1, 0:1] for r in modr[0]]
    dy, grads[0], token = _layer_bwd(dy, saved[0], modr0, layers[0], wvec, ready_for(0))
    recv = [{} for _ in range(DEPTH)]
    for l, group in [(1, "ff"), (1, "mix"), (1, "in"), (0, "ff"), (0, "mix")]:
        recv[l].update(received(l, group, token))
    grad_x = dy[None]
    dmod_size = DEPTH * 6 * D_MODEL

    small_pack, ssem, rsem, thru, lands = pending["small"]
    (small_all,) = with_own(exchange_wait("gather_small_grads_wait", ssem, rsem, thru, lands, token, True), [small_pack])
    small_sum = sum_blocks("sum_small_grads", small_all, small_pack.shape[0]).reshape(-1)
    gsmall = dict(zip(SMALL_NAMES, _unflat(small_sum, small_shapes)))
    gw = dict(gsmall)
    gw["conv_w"] = lax.dynamic_slice_in_dim(gsmall["conv_w"], me * conv_w.shape[2], conv_w.shape[2], axis=2)

    dmod_all = small_all.reshape(N_DEV, -1)[:, :dmod_size].reshape(N_DEV, DEPTH, N_DEV, nc_ada)
    dm_mine = lax.dynamic_index_in_dim(dmod_all, me, axis=2, keepdims=False).reshape(N_DEV, DEPTH * nc_ada)
    cact_t = jnp.pad(cact.T, ((0, 0), (0, LANE - N_DEV)))
    dm_pad = jnp.pad(dm_mine, ((0, LANE - N_DEV), (0, 0)))
    dw_cat = mm_big("mm_dw_ada", cact_t, dm_pad, "nn", (D_MODEL, DEPTH * nc_ada, LANE), F32)
    gw["w_ada"] = jnp.stack([dw_cat[:, l * nc_ada:(l + 1) * nc_ada] for l in range(DEPTH)])

    delta, new_m, new_v = {}, {}, {}
    packs = [_flat_pad([src[n] for n in SMALL_NAMES], 8 * LANE).reshape(-1, LANE) for src in (wts, gw, mom, var)]
    small_res = adamw("adamw_small", *packs, packs[0].shape[0])
    shapes = [wts[n].shape for n in SMALL_NAMES]
    for out, flat in zip((delta, new_m, new_v), small_res):
        out.update(dict(zip(SMALL_NAMES, _unflat(flat.reshape(-1), shapes))))
    for n in ["w_ada", "w_ff2", "w_ff1", "w_o", "w_br_pool", "w_br_attn", "w_br_conv", "w_in"]:
        shp = xlayout(n, wts[n]).shape
        two_d = lambda a, shp=shp, n=n: xlayout(n, a).reshape(shp[0] * shp[1], shp[2])
        tr = 224 if n == "w_in" else min(256, shp[1])
        if n == "w_ada":
            res = (gw[n],) + tuple(adamw("adamw_" + n, two_d(wts[n]), two_d(gw[n]), two_d(mom[n]), two_d(var[n]), tr))
        else:
            if n == "w_in":
                done = [small_res[2]] + [new_v[k] for k in ["w_ada"] + BIG_NAMES[1:]]
                recv[0].update(received(0, "in", done))
            res = adamw_sum("adamw_" + n, two_d(wts[n]), [recv[l][n][0] for l in range(DEPTH)],
                            [recv[l][n][1] for l in range(DEPTH)], two_d(mom[n]), two_d(var[n]), tr)
        gw[n], delta[n], new_m[n], new_v[n] = [xlayout(n, a.reshape(shp)) for a in res]

    return (loss, grad_x, *[gw[n] for n in WEIGHT_NAMES], *[delta[n] for n in WEIGHT_NAMES],
            *[new_m[n] for n in WEIGHT_NAMES], *[new_v[n] for n in WEIGHT_NAMES])
```

```python
import functools

import jax
import jax.numpy as jnp
import numpy as np
from jax import lax
from jax.experimental import pallas as pl
from jax.experimental.pallas import tpu as pltpu

F32 = jnp.float32
BF16 = jnp.bfloat16
MESH = pl.DeviceIdType.MESH

D_MODEL = 1024
DEPTH = 2
CHUNK = 64
N_HEADS = 8
HEAD_DIM = 64
D_POOL = 256
D_ATTN = 512
D_CONV = 256
CONV_WIDTH = 31
D_FF = 4096
D_IN = 5376
N_PREV = 8
BAND = (N_PREV + 1) * CHUNK
REL_CLIP = 128
ALPHA = (2.0 * DEPTH) ** 0.25
LN_EPS = 1e-5
NEG_INF = -1e30
N_DEV = 8

ADAM_LR, ADAM_B1, ADAM_B2, ADAM_EPS, ADAM_WD, ADAM_STEP = 0.001, 0.9, 0.999, 1e-08, 0.01, 10

VMEM_LIMIT = 56 * 1024 * 1024

Z_POOL, Z_Q, Z_K, Z_V, Z_CONV, Z_GATE = 0, 256, 768, 1280, 1792, 2304
GATE_BLOCK = 768
ATT_TILE = 512
LANE = 128


def _dg(a, b, ca, cb):
    return lax.dot_general(a.astype(BF16), b.astype(BF16), (((ca,), (cb,)), ((), ())),
                           preferred_element_type=F32)


@jax.custom_vjp
def mm_nn(a, b):
    return _dg(a, b, 1, 0)


def _mm_nn_fwd(a, b):
    return _dg(a, b, 1, 0), (a, b)


def _mm_nn_bwd(res, g):
    a, b = res
    return _dg(g, b, 1, 1).astype(a.dtype), _dg(a, g, 0, 0).astype(b.dtype)


mm_nn.defvjp(_mm_nn_fwd, _mm_nn_bwd)


@jax.custom_vjp
def mm_nt(a, b):
    return _dg(a, b, 1, 1)


def _mm_nt_fwd(a, b):
    return _dg(a, b, 1, 1), (a, b)


def _mm_nt_bwd(res, g):
    a, b = res
    return _dg(g, b, 1, 0).astype(a.dtype), _dg(g, a, 0, 0).astype(b.dtype)


mm_nt.defvjp(_mm_nt_fwd, _mm_nt_bwd)


@jax.custom_vjp
def mm_nn_shadow(a, w, shadow):
    return _dg(a, w, 1, 0)


def _mm_nn_shadow_fwd(a, w, shadow):
    return _dg(a, w, 1, 0), (a, w)


def _mm_nn_shadow_bwd(res, g):
    a, w = res
    return _dg(g, w, 1, 1).astype(a.dtype), jnp.zeros_like(w), _dg(a, g, 0, 0)


mm_nn_shadow.defvjp(_mm_nn_shadow_fwd, _mm_nn_shadow_bwd)


def mm_w(a, w):
    return mm_nn_shadow(a, w[0], w[1]) if isinstance(w, tuple) else mm_nn(a, w)


def _ln(x):
    mu = jnp.mean(x, axis=-1, keepdims=True)
    xc = x - mu
    var = jnp.mean(xc * xc, axis=-1, keepdims=True)
    return xc * lax.rsqrt(var + LN_EPS)


def _norm_rows(rows):
    return [r if isinstance(r, tuple) else (r, r.shape[1], 0) for r in rows]


def _row_spec(tm, r):
    _, width, cb = r
    return pl.BlockSpec((tm, width), lambda i, cb=cb: (i, cb))


def _full_spec(a):
    nd = a.ndim
    return pl.BlockSpec(a.shape, lambda i, nd=nd: (0,) * nd)


def row_fwd(name, f, rows, params, outs, tm):
    rows = _norm_rows(rows)
    s = rows[0][0].shape[0]
    nr, npar = len(rows), len(params)

    def body(*refs):
        r = [x[...].astype(F32) for x in refs[:nr]]
        p = [x[...] for x in refs[nr:nr + npar]]
        res = f(*r, *p)
        for o_ref, o in zip(refs[nr + npar:], res):
            o_ref[...] = o.astype(o_ref.dtype)

    return pl.pallas_call(
        body, name=name, grid=(s // tm,),
        in_specs=[_row_spec(tm, r) for r in rows] + [_full_spec(p) for p in params],
        out_specs=[pl.BlockSpec((tm, w), lambda i: (i, 0)) for w, _ in outs],
        out_shape=[jax.ShapeDtypeStruct((s, w), dt) for w, dt in outs],
        compiler_params=pltpu.CompilerParams(dimension_semantics=("parallel",), vmem_limit_bytes=VMEM_LIMIT),
    )(*[r[0] for r in rows], *params)


def row_bwd(name, f, rows, params, douts, tm, want_rows, want_params, add_to=None):
    rows = _norm_rows(rows)
    s = rows[0][0].shape[0]
    nr, npar, nd = len(rows), len(params), len(douts)
    nadd = 0 if add_to is None else 1
    n_in = nr + npar + nd + nadd

    def body(*refs):
        i = pl.program_id(0)
        r = [x[...].astype(F32) for x in refs[:nr]]
        p = [(x[...], jnp.zeros(x.shape, F32)) if x.dtype == BF16 else x[...] for x in refs[nr:nr + npar]]
        d = [x[...].astype(F32) for x in refs[nr + npar:nr + npar + nd]]
        _, vjp = jax.vjp(f, *r, *p)
        g = vjp(tuple(d))
        out_refs = refs[n_in:]
        for k, (idx, _) in enumerate(want_rows):
            val = g[idx]
            if nadd and k == 0:
                val = val + refs[n_in - 1][...].astype(F32)
            out_refs[k][...] = val.astype(out_refs[k].dtype)
        for k, idx in enumerate(want_params):
            gp = g[nr + idx]
            gp = gp[1] if isinstance(gp, tuple) else gp
            o_ref = out_refs[len(want_rows) + k]

            @pl.when(i == 0)
            def _():
                o_ref[...] = gp

            @pl.when(i > 0)
            def _():
                o_ref[...] += gp

    in_specs = ([_row_spec(tm, r) for r in rows] + [_full_spec(p) for p in params]
                + [pl.BlockSpec((tm, d.shape[1]), lambda i: (i, 0)) for d in douts])
    args = [r[0] for r in rows] + list(params) + list(douts)
    if nadd:
        in_specs.append(pl.BlockSpec((tm, add_to.shape[1]), lambda i: (i, 0)))
        args.append(add_to)
    out_specs = ([pl.BlockSpec((tm, rows[idx][1]), lambda i: (i, 0)) for idx, _ in want_rows]
                 + [_full_spec(params[idx]) for idx in want_params])
    out_shape = ([jax.ShapeDtypeStruct((s, rows[idx][1]), dt) for idx, dt in want_rows]
                 + [jax.ShapeDtypeStruct(params[idx].shape, F32) for idx in want_params])
    res = pl.pallas_call(
        body, name=name, grid=(s // tm,), in_specs=in_specs, out_specs=out_specs, out_shape=out_shape,
        compiler_params=pltpu.CompilerParams(dimension_semantics=("arbitrary",), vmem_limit_bytes=VMEM_LIMIT),
    )(*args)
    return res[:len(want_rows)], res[len(want_rows):]


def f_lnmod(x, sc, sh):
    return (_ln(x) * (1.0 + sc) + sh,)


def f_merge(p, ao, cv, zg0, zg1, zg2, zg3, x, wbd, ps, wbp, wba, wbc, cb, clg, clb, bg, wo, gm, lg, lb):
    zg = jnp.concatenate([zg0, zg1, zg2, zg3], axis=1)
    pm = mm_w(p, wbd) * ps
    co = jax.nn.silu(_ln(cv + cb) * clg + clb)
    y_pool = mm_w(pm, wbp)
    y_attn = mm_w(ao, wba)
    y_conv = mm_w(co, wbc)
    gates = jax.nn.sigmoid(zg + bg)
    merged = (gates[:, :D_MODEL] * y_pool + gates[:, D_MODEL:2 * D_MODEL] * y_attn
              + gates[:, 2 * D_MODEL:] * y_conv)
    mix = mm_w(merged, wo)
    return (_ln(ALPHA * x + gm * mix) * lg + lb,)


def f_relu2(hpre, b1):
    a = jax.nn.relu(hpre + b1)
    return (a * a,)


def f_ffout(x1, ff, b2, gf, lg, lb):
    return (_ln(ALPHA * x1 + gf * (ff + b2)) * lg + lb,)


def mm_big(name, a, b, kind, tiles, out_dtype, j_outer=False, after=None):
    if kind == "nn":
        o0, o1, red = a.shape[0], b.shape[1], a.shape[1]
    elif kind == "nt":
        o0, o1, red = a.shape[0], b.shape[0], a.shape[1]
    else:
        o0, o1, red = a.shape[1], b.shape[1], a.shape[0]
    t0, t1, tr = min(tiles[0], o0), min(tiles[1], o1), min(tiles[2], red)
    if kind == "nn":
        a_spec = pl.BlockSpec((t0, tr), lambda i, j, r: (i, r))
        b_spec = pl.BlockSpec((tr, t1), lambda i, j, r: (r, j))
        dims = (1, 0)
    elif kind == "nt":
        a_spec = pl.BlockSpec((t0, tr), lambda i, j, r: (i, r))
        b_spec = pl.BlockSpec((t1, tr), lambda i, j, r: (j, r))
        dims = (1, 1)
    else:
        a_spec = pl.BlockSpec((tr, t0), lambda i, j, r: (r, i))
        b_spec = pl.BlockSpec((tr, t1), lambda i, j, r: (r, j))
        dims = (0, 0)
    assert o0 % t0 == 0 and o1 % t1 == 0 and red % tr == 0, (name, a.shape, b.shape, tiles)
    n0, n1, nred = o0 // t0, o1 // t1, red // tr
    o_spec = pl.BlockSpec((t0, t1), lambda i, j, r: (i, j))
    if j_outer:
        swap = lambda spec: pl.BlockSpec(spec.block_shape, lambda j, i, r, f=spec.index_map: f(i, j, r))
        a_spec, b_spec, o_spec = swap(a_spec), swap(b_spec), swap(o_spec)
        grid = (n1, n0, nred)
    else:
        grid = (n0, n1, nred)

    deps = [] if after is None else [after]
    dep_specs = [pl.BlockSpec(d.shape, lambda i, j, r, nd=d.ndim: (0,) * nd) for d in deps]
    if nred == 1:
        def body(a_ref, b_ref, *rest):
            o_ref = rest[len(deps)]
            o_ref[...] = _dg(a_ref[...], b_ref[...], *dims).astype(o_ref.dtype)
        scratch = []
    else:
        def body(a_ref, b_ref, *rest):
            o_ref, acc_ref = rest[len(deps):]
            r = pl.program_id(2)
            part = _dg(a_ref[...], b_ref[...], *dims)

            @pl.when(r == 0)
            def _():
                acc_ref[...] = part

            @pl.when(jnp.logical_and(r > 0, r < nred - 1))
            def _():
                acc_ref[...] += part

            @pl.when(r == nred - 1)
            def _():
                o_ref[...] = (acc_ref[...] + part).astype(o_ref.dtype)
        scratch = [pltpu.VMEM((t0, t1), F32)]

    return pl.pallas_call(
        body, name=name, grid=grid,
        in_specs=[a_spec, b_spec] + dep_specs,
        out_specs=o_spec,
        out_shape=jax.ShapeDtypeStruct((o0, o1), out_dtype),
        scratch_shapes=scratch,
        compiler_params=pltpu.CompilerParams(dimension_semantics=("parallel", "parallel", "arbitrary"),
                                             vmem_limit_bytes=VMEM_LIMIT),
    )(a, b, *deps)


def mm_ff1_relu2(name, u2, w1, b1, tm, tn):
    m, k = u2.shape
    n = w1.shape[1]
    tm, tn = min(tm, m), min(tn, n)

    def body(a_ref, b_ref, bias_ref, hpre_ref, h_ref):
        acc = _dg(a_ref[...], b_ref[...], 1, 0)
        hpre_ref[...] = acc
        h_ref[...] = f_relu2(acc, bias_ref[...])[0].astype(h_ref.dtype)

    out = pl.BlockSpec((tm, tn), lambda j, i: (i, j))
    return pl.pallas_call(
        body, name=name, grid=(n // tn, m // tm),
        in_specs=[pl.BlockSpec((tm, k), lambda j, i: (i, 0)), pl.BlockSpec((k, tn), lambda j, i: (0, j)),
                  pl.BlockSpec((1, tn), lambda j, i: (0, j))],
        out_specs=[out, out],
        out_shape=[jax.ShapeDtypeStruct((m, n), F32), jax.ShapeDtypeStruct((m, n), BF16)],
        compiler_params=pltpu.CompilerParams(dimension_semantics=("parallel", "parallel"), vmem_limit_bytes=VMEM_LIMIT),
    )(u2, w1, b1)


def mm_dh_relu2(name, dff, w2, hpre, b1, tm, tn):
    m, k = dff.shape
    n = w2.shape[0]
    tm, tn = min(tm, m), min(tn, n)

    def body(a_ref, b_ref, hpre_ref, bias_ref, d_ref, db_ref):
        i = pl.program_id(1)
        dh = _dg(a_ref[...], b_ref[...], 1, 1)
        _, vjp = jax.vjp(f_relu2, hpre_ref[...], bias_ref[...])
        dhpre, db = vjp((dh,))
        d_ref[...] = dhpre.astype(d_ref.dtype)

        @pl.when(i == 0)
        def _():
            db_ref[...] = db

        @pl.when(i > 0)
        def _():
            db_ref[...] += db

    tile = pl.BlockSpec((tm, tn), lambda j, i: (i, j))
    col = pl.BlockSpec((1, tn), lambda j, i: (0, j))
    return pl.pallas_call(
        body, name=name, grid=(n // tn, m // tm),
        in_specs=[pl.BlockSpec((tm, k), lambda j, i: (i, 0)), pl.BlockSpec((tn, k), lambda j, i: (j, 0)), tile, col],
        out_specs=[tile, col],
        out_shape=[jax.ShapeDtypeStruct((m, n), BF16), jax.ShapeDtypeStruct((1, n), F32)],
        compiler_params=pltpu.CompilerParams(dimension_semantics=("parallel", "arbitrary"), vmem_limit_bytes=VMEM_LIMIT),
    )(dff, w2, hpre, b1)


POOL_PAD = 16
POOL_ROWS = 256


def pool_lin(name, x, wvec, transpose, out_dtype):
    arr, width, cb = x
    s = arr.shape[0]
    n_steps = s // POOL_ROWS

    def body(x_ref, w_ref, o_ref, xp_ref):
        wv = w_ref[...]
        zeros = jnp.zeros((POOL_PAD, width), F32)
        xp_ref[0:POOL_PAD, :] = zeros
        xp_ref[s + POOL_PAD:s + 2 * POOL_PAD, :] = zeros

        def count(t0):
            t = lax.broadcasted_iota(jnp.int32, (POOL_ROWS, width), 0) + (t0 + 1)
            return jnp.minimum(t.astype(F32), wv)

        def fill(i, carry):
            t0 = pl.multiple_of(i * POOL_ROWS, POOL_ROWS)
            v = x_ref[pl.ds(t0, POOL_ROWS), :].astype(F32)
            if transpose:
                v = v / count(t0)
            xp_ref[pl.ds(t0 + POOL_PAD, POOL_ROWS), :] = v
            return carry

        lax.fori_loop(0, n_steps, fill, 0)

        def step(i, carry):
            t0 = pl.multiple_of(i * POOL_ROWS, POOL_ROWS)
            win = xp_ref[pl.ds(t0, POOL_ROWS + 2 * POOL_PAD), :]
            acc = jnp.zeros((POOL_ROWS, width), F32)
            for j in range(POOL_PAD):
                off = POOL_PAD + j if transpose else POOL_PAD - j
                acc = acc + jnp.where(wv > j, win[off:off + POOL_ROWS, :], 0.0)
            cur = x_ref[pl.ds(t0, POOL_ROWS), :].astype(F32)
            res = acc - cur if transpose else acc / count(t0) - cur
            o_ref[pl.ds(t0, POOL_ROWS), :] = res.astype(o_ref.dtype)
            return carry

        lax.fori_loop(0, n_steps, step, 0)

    return pl.pallas_call(
        body, name=name, grid=(1,),
        in_specs=[pl.BlockSpec((s, width), lambda i, cb=cb: (0, cb)), pl.BlockSpec((1, width), lambda i: (0, 0))],
        out_specs=pl.BlockSpec((s, width), lambda i: (0, 0)),
        out_shape=jax.ShapeDtypeStruct((s, width), out_dtype),
        scratch_shapes=[pltpu.VMEM((s + 2 * POOL_PAD, width), F32)],
        compiler_params=pltpu.CompilerParams(dimension_semantics=("arbitrary",), vmem_limit_bytes=VMEM_LIMIT),
    )(arr, wvec)


CONV_PAD = 32
CONV_ROWS = 128


def _glu(a, g):
    return a * jax.nn.sigmoid(g)


def conv_fwd(name, zc, w):
    arr, width, cb = zc
    s = arr.shape[0]
    n_steps = s // CONV_ROWS
    lead = CONV_PAD - (CONV_WIDTH - 1)

    def body(a_ref, g_ref, w_ref, o_ref, hp_ref):
        hp_ref[0:CONV_PAD, :] = jnp.zeros((CONV_PAD, D_CONV), F32)

        def fill(i, carry):
            t0 = pl.multiple_of(i * CONV_ROWS, CONV_ROWS)
            rows = pl.ds(t0, CONV_ROWS)
            hp_ref[pl.ds(t0 + CONV_PAD, CONV_ROWS), :] = _glu(a_ref[rows, :], g_ref[rows, :])
            return carry

        lax.fori_loop(0, n_steps, fill, 0)
        wv = w_ref[...]

        def step(i, carry):
            t0 = pl.multiple_of(i * CONV_ROWS, CONV_ROWS)
            win = hp_ref[pl.ds(t0, CONV_ROWS + CONV_PAD), :]
            acc = jnp.zeros((CONV_ROWS, D_CONV), F32)
            for k in range(CONV_WIDTH):
                acc = acc + wv[k:k + 1, :] * win[lead + k:lead + k + CONV_ROWS, :]
            o_ref[pl.ds(t0, CONV_ROWS), :] = acc
            return carry

        lax.fori_loop(0, n_steps, step, 0)

    return pl.pallas_call(
        body, name=name, grid=(1,),
        in_specs=[pl.BlockSpec((s, width), lambda i, cb=cb: (0, cb)),
                  pl.BlockSpec((s, width), lambda i, cb=cb: (0, cb + 1)), pl.BlockSpec(w.shape, lambda i: (0, 0))],
        out_specs=pl.BlockSpec((s, D_CONV), lambda i: (0, 0)),
        out_shape=jax.ShapeDtypeStruct((s, D_CONV), F32),
        scratch_shapes=[pltpu.VMEM((s + CONV_PAD, D_CONV), F32)],
        compiler_params=pltpu.CompilerParams(dimension_semantics=("arbitrary",), vmem_limit_bytes=VMEM_LIMIT),
    )(arr, arr, w)


def conv_bwd(name, zc, w, dout):
    arr, width, cb = zc
    s = arr.shape[0]
    n_steps = s // CONV_ROWS
    lead = CONV_PAD - (CONV_WIDTH - 1)

    def body(a_ref, g_ref, w_ref, d_ref, dz_ref, dw_ref, hp_ref, dp_ref):
        hp_ref[0:CONV_PAD, :] = jnp.zeros((CONV_PAD, D_CONV), F32)
        dp_ref[s:s + CONV_PAD, :] = jnp.zeros((CONV_PAD, D_CONV), F32)
        dw_ref[...] = jnp.zeros(dw_ref.shape, F32)

        def fill(i, carry):
            t0 = pl.multiple_of(i * CONV_ROWS, CONV_ROWS)
            rows = pl.ds(t0, CONV_ROWS)
            hp_ref[pl.ds(t0 + CONV_PAD, CONV_ROWS), :] = _glu(a_ref[rows, :], g_ref[rows, :])
            dp_ref[rows, :] = d_ref[rows, :]
            return carry

        lax.fori_loop(0, n_steps, fill, 0)
        wv = w_ref[...]

        def step(i, carry):
            t0 = pl.multiple_of(i * CONV_ROWS, CONV_ROWS)
            hwin = hp_ref[pl.ds(t0, CONV_ROWS + CONV_PAD), :]
            dwin = dp_ref[pl.ds(t0, CONV_ROWS + CONV_PAD), :]
            dcur = dwin[0:CONV_ROWS, :]
            dh = jnp.zeros((CONV_ROWS, D_CONV), F32)
            rows = []
            for k in range(CONV_WIDTH):
                rows.append(jnp.sum(dcur * hwin[lead + k:lead + k + CONV_ROWS, :], axis=0, keepdims=True))
                back = CONV_WIDTH - 1 - k
                dh = dh + wv[k:k + 1, :] * dwin[back:back + CONV_ROWS, :]
            rows.append(jnp.zeros((1, D_CONV), F32))
            dw_ref[...] += jnp.concatenate(rows, axis=0)
            rows_now = pl.ds(t0, CONV_ROWS)
            _, vjp = jax.vjp(_glu, a_ref[rows_now, :], g_ref[rows_now, :])
            da, dg = vjp(dh)
            dz_ref[pl.ds(t0, CONV_ROWS), :] = jnp.concatenate([da, dg], axis=1).astype(dz_ref.dtype)
            return carry

        lax.fori_loop(0, n_steps, step, 0)

    return pl.pallas_call(
        body, name=name, grid=(1,),
        in_specs=[pl.BlockSpec((s, width), lambda i, cb=cb: (0, cb)),
                  pl.BlockSpec((s, width), lambda i, cb=cb: (0, cb + 1)),
                  pl.BlockSpec(w.shape, lambda i: (0, 0)), pl.BlockSpec((s, D_CONV), lambda i: (0, 0))],
        out_specs=[pl.BlockSpec((s, 2 * width), lambda i: (0, 0)), pl.BlockSpec(w.shape, lambda i: (0, 0))],
        out_shape=[jax.ShapeDtypeStruct((s, 2 * width), BF16), jax.ShapeDtypeStruct(w.shape, F32)],
        scratch_shapes=[pltpu.VMEM((s + CONV_PAD, D_CONV), F32), pltpu.VMEM((s + CONV_PAD, D_CONV), F32)],
        compiler_params=pltpu.CompilerParams(dimension_semantics=("arbitrary",), vmem_limit_bytes=VMEM_LIMIT),
    )(arr, arr, w, dout)


HEADS_PER_STEP = LANE // HEAD_DIM
CHUNKS_PER_TILE = ATT_TILE // CHUNK
KEY_BLOCKS = N_PREV * CHUNK // ATT_TILE + 1
KEY_SPAN = KEY_BLOCKS * ATT_TILE


def _attn_tile(q, *rest, missing_cols):
    kcat = jnp.concatenate(rest[:KEY_BLOCKS], axis=0)
    vcat = jnp.concatenate(rest[KEY_BLOCKS:2 * KEY_BLOCKS], axis=0)
    bias = rest[2 * KEY_BLOCKS]
    lane = lax.broadcasted_iota(jnp.int32, (1, LANE), 1)
    col = lax.broadcasted_iota(jnp.int32, (1, KEY_SPAN), 1)
    missing = col < missing_cols
    qs = q * (HEAD_DIM ** -0.5)
    o = jnp.zeros((ATT_TILE, LANE), F32)
    for h in range(HEADS_PER_STEP):
        in_head = jnp.logical_and(lane >= h * HEAD_DIM, lane < (h + 1) * HEAD_DIM)
        sc = mm_nt(jnp.where(in_head, qs, 0.0), kcat) + bias[h]
        sc = jnp.where(missing, NEG_INF, sc)
        m = jnp.max(sc, axis=-1, keepdims=True)
        e = jnp.exp(sc - lax.stop_gradient(m))
        p = e / jnp.sum(e, axis=-1, keepdims=True)
        o = o + jnp.where(in_head, mm_nn(p, vcat), 0.0)
    return o


def _missing_cols(n):
    return jnp.maximum((KEY_BLOCKS - 1 - n) * ATT_TILE, 0)


def _attn_in_specs(nt):
    def spec(col0, back):
        return pl.BlockSpec((ATT_TILE, LANE),
                            lambda hp, n, col0=col0, back=back: (jnp.clip(n - back, 0, nt - 1), col0 // LANE + hp))
    backs = list(range(KEY_BLOCKS - 1, -1, -1))
    return ([spec(Z_Q, 0)] + [spec(Z_K, b) for b in backs] + [spec(Z_V, b) for b in backs]
            + [pl.BlockSpec((HEADS_PER_STEP, ATT_TILE, KEY_SPAN), lambda hp, n: (hp, 0, 0))])


def attn_fwd(name, z, bias):
    s = z.shape[0]
    nt = s // ATT_TILE
    n_in = 2 + 2 * KEY_BLOCKS

    def body(*refs):
        o_ref = refs[n_in]
        vals = [r[...] for r in refs[:n_in]]
        o = _attn_tile(*vals, missing_cols=_missing_cols(pl.program_id(1)))
        o_ref[...] = o.astype(o_ref.dtype)

    return pl.pallas_call(
        body, name=name, grid=(N_HEADS // HEADS_PER_STEP, nt),
        in_specs=_attn_in_specs(nt),
        out_specs=pl.BlockSpec((ATT_TILE, LANE), lambda hp, n: (n, hp)),
        out_shape=jax.ShapeDtypeStruct((s, D_ATTN), BF16),
        compiler_params=pltpu.CompilerParams(dimension_semantics=("parallel", "parallel"), vmem_limit_bytes=VMEM_LIMIT),
    )(*([z] * (n_in - 1)), bias)


def attn_bwd(name, z, bias, do, after=None):
    s = z.shape[0]
    nt = s // ATT_TILE
    n_in = 2 + 2 * KEY_BLOCKS
    nc = KEY_BLOCKS - 1

    deps = [] if after is None else [after]
    n_out = n_in + 1 + len(deps)

    def body(*refs):
        do_ref = refs[n_in]
        dq_ref, dk_ref, dv_ref, db_ref = refs[n_out:n_out + 4]
        kacc, vacc = refs[n_out + 4:n_out + 4 + nc], refs[n_out + 4 + nc:]
        n = pl.program_id(1)

        @pl.when(n == 0)
        def _():
            db_ref[...] = jnp.zeros(db_ref.shape, F32)
            for acc in (*kacc, *vacc):
                acc[...] = jnp.zeros(acc.shape, F32)

        def shift(out_ref, accs, contrib):
            @pl.when(n >= nc)
            def _():
                first = accs[0][...] if contrib is None else accs[0][...] + contrib[0]
                out_ref[...] = first.astype(out_ref.dtype)
            for j in range(nc - 1):
                accs[j][...] = accs[j + 1][...] if contrib is None else accs[j + 1][...] + contrib[j + 1]
            if contrib is not None:
                accs[nc - 1][...] = contrib[nc]

        @pl.when(n < nt)
        def _():
            fn = functools.partial(_attn_tile, missing_cols=_missing_cols(n))
            _, vjp = jax.vjp(fn, *[r[...] for r in refs[:n_in]])
            grads = vjp(do_ref[...].astype(F32))
            dq_ref[...] = grads[0].astype(dq_ref.dtype)
            db_ref[...] += grads[n_in - 1]
            shift(dk_ref, kacc, grads[1:1 + KEY_BLOCKS])
            shift(dv_ref, vacc, grads[1 + KEY_BLOCKS:1 + 2 * KEY_BLOCKS])

        @pl.when(n >= nt)
        def _():
            shift(dk_ref, kacc, None)
            shift(dv_ref, vacc, None)

    o_cur = pl.BlockSpec((ATT_TILE, LANE), lambda hp, n: (jnp.minimum(n, nt - 1), hp))
    o_old = pl.BlockSpec((ATT_TILE, LANE), lambda hp, n: (jnp.maximum(n - nc, 0), hp))
    b_spec = pl.BlockSpec((HEADS_PER_STEP, ATT_TILE, KEY_SPAN), lambda hp, n: (hp, 0, 0))
    return pl.pallas_call(
        body, name=name, grid=(N_HEADS // HEADS_PER_STEP, nt + nc),
        in_specs=_attn_in_specs(nt) + [o_cur] + [pl.BlockSpec(d.shape, lambda hp, n: (0, 0)) for d in deps],
        out_specs=[o_cur, o_old, o_old, b_spec],
        out_shape=[jax.ShapeDtypeStruct((s, D_ATTN), BF16)] * 3 + [jax.ShapeDtypeStruct((N_HEADS, ATT_TILE, KEY_SPAN), F32)],
        scratch_shapes=[pltpu.VMEM((ATT_TILE, LANE), F32)] * (2 * nc),
        compiler_params=pltpu.CompilerParams(dimension_semantics=("parallel", "arbitrary"), vmem_limit_bytes=VMEM_LIMIT),
    )(*([z] * (n_in - 1)), bias, do, *deps)


def loss_head(name, f, rows, params, tgt, tm):
    s, d = tgt.shape
    nr, npar = len(rows), len(params)

    def body(*refs):
        t_ref, l_ref, dy_ref = refs[nr + npar:]
        i = pl.program_id(0)
        (y,) = f(*[x[...].astype(F32) for x in refs[:nr]], *[x[...] for x in refs[nr:nr + npar]])
        diff = y - t_ref[...]
        dy_ref[...] = diff * (1.0 / d)
        part = 0.5 * jnp.sum(jnp.mean(diff * diff, axis=-1, keepdims=True), axis=0, keepdims=True)

        @pl.when(i == 0)
        def _():
            l_ref[...] = jnp.zeros(l_ref.shape, F32)

        l_ref[...] += jnp.broadcast_to(part, l_ref.shape)

    row = pl.BlockSpec((tm, d), lambda i: (i, 0))
    return pl.pallas_call(
        body, name=name, grid=(s // tm,),
        in_specs=[pl.BlockSpec((tm, r.shape[1]), lambda i: (i, 0)) for r in rows] + [_full_spec(p) for p in params] + [row],
        out_specs=[pl.BlockSpec((8, LANE), lambda i: (0, 0)), row],
        out_shape=[jax.ShapeDtypeStruct((8, LANE), F32), jax.ShapeDtypeStruct((s, d), F32)],
        compiler_params=pltpu.CompilerParams(dimension_semantics=("arbitrary",), vmem_limit_bytes=VMEM_LIMIT),
    )(*rows, *params, tgt)


def adamw(name, w, g, m, v, tr):
    r, c = w.shape
    assert r % tr == 0, (name, w.shape, tr)

    def body(w_ref, g_ref, m_ref, v_ref, d_ref, nm_ref, nv_ref):
        gg = g_ref[...]
        m2 = ADAM_B1 * m_ref[...] + (1.0 - ADAM_B1) * gg
        v2 = ADAM_B2 * v_ref[...] + (1.0 - ADAM_B2) * (gg * gg)
        m_hat = m2 / (1.0 - ADAM_B1 ** ADAM_STEP)
        v_hat = v2 / (1.0 - ADAM_B2 ** ADAM_STEP)
        d_ref[...] = -ADAM_LR * (m_hat / (jnp.sqrt(v_hat) + ADAM_EPS) + ADAM_WD * w_ref[...])
        nm_ref[...] = m2
        nv_ref[...] = v2

    blk = pl.BlockSpec((tr, c), lambda i: (i, 0))
    return pl.pallas_call(
        body, name=name, grid=(r // tr,), in_specs=[blk] * 4, out_specs=[blk] * 3,
        out_shape=[jax.ShapeDtypeStruct((r, c), F32)] * 3,
        compiler_params=pltpu.CompilerParams(dimension_semantics=("parallel",), vmem_limit_bytes=VMEM_LIMIT),
    )(w, g, m, v)


def adamw_sum(name, w, layer_blocks, layer_own, m, v, tr):
    rows, c = w.shape
    nl = len(layer_blocks)
    nb, r, _ = layer_blocks[0].shape
    assert rows == nl * r and r % tr == 0, (name, w.shape, layer_blocks[0].shape, tr)
    per = r // tr

    def body(*refs):
        w_ref, b_refs, o_refs = refs[0], refs[1:1 + nl], refs[1 + nl:1 + 2 * nl]
        m_ref, v_ref, g_ref, d_ref, nm_ref, nv_ref = refs[1 + 2 * nl:]
        i = pl.program_id(0)
        me = _idx(_me())

        def update(b_ref, own_ref):
            gg = None
            for j in range(nb):
                blk = jnp.where(j == me, own_ref[...].astype(F32), b_ref[jnp.where(j == me, (j + 1) % nb, j)].astype(F32))
                gg = blk if gg is None else gg + blk
            g_ref[...] = gg
            m2 = ADAM_B1 * m_ref[...] + (1.0 - ADAM_B1) * gg
            v2 = ADAM_B2 * v_ref[...] + (1.0 - ADAM_B2) * (gg * gg)
            m_hat = m2 / (1.0 - ADAM_B1 ** ADAM_STEP)
            v_hat = v2 / (1.0 - ADAM_B2 ** ADAM_STEP)
            d_ref[...] = -ADAM_LR * (m_hat / (jnp.sqrt(v_hat) + ADAM_EPS) + ADAM_WD * w_ref[...])
            nm_ref[...] = m2
            nv_ref[...] = v2

        for l in range(nl):
            pl.when(jnp.logical_and(i >= l * per, i < (l + 1) * per))(functools.partial(update, b_refs[l], o_refs[l]))

    blk = pl.BlockSpec((tr, c), lambda i: (i, 0))
    b_specs = [pl.BlockSpec((nb, tr, c), lambda i, l=l: (0, jnp.clip(i - l * per, 0, per - 1), 0)) for l in range(nl)]
    o_specs = [pl.BlockSpec((tr, c), lambda i, l=l: (jnp.clip(i - l * per, 0, per - 1), 0)) for l in range(nl)]
    return pl.pallas_call(
        body, name=name, grid=(rows // tr,),
        in_specs=[blk] + b_specs + o_specs + [blk, blk], out_specs=[blk] * 4,
        out_shape=[jax.ShapeDtypeStruct((rows, c), F32)] * 4,
        compiler_params=pltpu.CompilerParams(dimension_semantics=("arbitrary",), vmem_limit_bytes=VMEM_LIMIT),
    )(w, *layer_blocks, *layer_own, m, v)


def sum_blocks(name, blocks, tr):
    nb, r, c = blocks.shape
    assert r % tr == 0, (name, blocks.shape, tr)

    def body(b_ref, o_ref):
        acc = b_ref[0].astype(F32)
        for j in range(1, nb):
            acc = acc + b_ref[j].astype(F32)
        o_ref[...] = acc

    return pl.pallas_call(
        body, name=name, grid=(r // tr,),
        in_specs=[pl.BlockSpec((nb, tr, c), lambda i: (0, i, 0))],
        out_specs=pl.BlockSpec((tr, c), lambda i: (i, 0)),
        out_shape=jax.ShapeDtypeStruct((r, c), F32),
        compiler_params=pltpu.CompilerParams(dimension_semantics=("parallel",), vmem_limit_bytes=VMEM_LIMIT),
    )(blocks)


FLIPS = [(0, 0, 1), (1, 0, 0), (0, 1, 0), (1, 1, 0), (1, 0, 1), (0, 1, 1), (1, 1, 1)]
ANY = pl.BlockSpec(memory_space=pl.ANY)


def _me():
    return lax.axis_index("x"), lax.axis_index("y"), lax.axis_index("c")


def _flip(pos, f):
    return tuple((1 - p) if fi else p for p, fi in zip(pos, f))


def _idx(pos):
    return 4 * pos[0] + 2 * pos[1] + pos[2]


def all_gather_multi(name, shards):
    n = len(shards)

    def body(*refs):
        x_refs, out_refs, token = refs[:n], refs[n:2 * n], refs[2 * n]
        send_sems, recv_sems, local_sems = refs[2 * n + 1:]
        token[...] = jnp.zeros_like(token)
        x, y, cc = _me()
        me, sibling = (x, y, cc), (x, y, 1 - cc)
        chips = [(1 - x, y), (x, 1 - y), (1 - x, 1 - y)]

        def copy(a, k, block, to, src=None):
            dst = out_refs[a].at[_idx(block)]
            return pltpu.make_async_remote_copy(
                src_ref=dst if src is None else src, dst_ref=dst, send_sem=send_sems.at[7 * a + k],
                recv_sem=recv_sems.at[7 * a + k], device_id=to, device_id_type=MESH)

        mine = [pltpu.make_async_copy(x_refs[a], out_refs[a].at[_idx(me)], local_sems.at[a]) for a in range(n)]
        for cp in mine:
            cp.start()
        first = []
        for a in range(n):
            first.append(copy(a, 0, me, sibling, src=x_refs[a]))
            first += [copy(a, 1 + j, me, (*chip, cc), src=x_refs[a]) for j, chip in enumerate(chips)]
        for cp in first:
            cp.start()
        passed = []
        for j, chip in enumerate(chips):
            for a in range(n):
                copy(a, 1 + j, (*chip, cc), me).wait_recv()
                fwd = copy(a, 4 + j, (*chip, cc), sibling)
                fwd.start()
                passed.append(fwd)
        for a in range(n):
            copy(a, 0, sibling, me).wait_recv()
            for j, chip in enumerate(chips):
                copy(a, 4 + j, (*chip, 1 - cc), me).wait_recv()
        for cp in first + passed:
            cp.wait_send()
        for cp in mine:
            cp.wait()

    return pl.pallas_call(
        body, name=name, in_specs=[ANY] * n, out_specs=[ANY] * n + [pl.BlockSpec(memory_space=pltpu.VMEM)],
        out_shape=[jax.ShapeDtypeStruct((N_DEV,) + a.shape, a.dtype) for a in shards]
        + [jax.ShapeDtypeStruct((8, LANE), F32)],
        scratch_shapes=[pltpu.SemaphoreType.DMA((7 * n,)), pltpu.SemaphoreType.DMA((7 * n,)),
                        pltpu.SemaphoreType.DMA((n,))],
    )(*shards)


HBM = pl.BlockSpec(memory_space=pltpu.HBM)
SEM = pl.BlockSpec(memory_space=pltpu.SEMAPHORE)
DATAFLOW = pltpu.SideEffectType.DATAFLOW_SIDE_EFFECTING


def _exchange_copies(a_refs, l_refs, send_sems, recv_sems, gather):
    me = _me()
    mi = _idx(me)
    out = []
    for k, f in enumerate(FLIPS):
        peer = _flip(me, f)
        for a in range(len(a_refs)):
            src = a_refs[a] if gather else a_refs[a].at[_idx(peer)]
            out.append(pltpu.make_async_remote_copy(
                src_ref=src, dst_ref=l_refs[a].at[mi], send_sem=send_sems.at[7 * a + k],
                recv_sem=recv_sems.at[7 * a + k], device_id=peer, device_id_type=MESH))
    return out


def exchange_start(name, arrays, gather):
    n = len(arrays)
    lands = [lax.empty(((N_DEV,) + a.shape) if gather else a.shape, a.dtype) for a in arrays]

    def body(*refs):
        a_refs, l_refs = refs[:n], refs[n:2 * n]
        send_sems, recv_sems = refs[2 * n], refs[2 * n + 1]
        token = refs[4 * n + 2]
        for cp in _exchange_copies(a_refs, l_refs, send_sems, recv_sems, gather):
            cp.start()
        token[...] = jnp.zeros_like(token)

    hbm = lambda a: pltpu.HBM(a.shape, a.dtype)
    res = pl.pallas_call(
        body, name=name,
        out_shape=(pltpu.SemaphoreType.DMA((7 * n,)), pltpu.SemaphoreType.DMA((7 * n,)),
                   *[hbm(a) for a in arrays], *[hbm(a) for a in lands], jax.ShapeDtypeStruct((8, LANE), F32)),
        in_specs=[HBM] * (2 * n),
        out_specs=(SEM, SEM, *[HBM] * (2 * n), pl.BlockSpec(memory_space=pltpu.VMEM)),
        input_output_aliases={i: i + 2 for i in range(2 * n)},
        compiler_params=pltpu.CompilerParams(has_side_effects=DATAFLOW),
    )(*[pltpu.with_memory_space_constraint(a, pltpu.HBM) for a in arrays],
      *[pltpu.with_memory_space_constraint(a, pltpu.HBM) for a in lands])
    return res[0], res[1], list(res[2:2 + n]), list(res[2 + n:2 + 2 * n]), res[-1]


def exchange_wait(name, send_sems, recv_sems, arrays, lands, after, gather):
    n = len(arrays)
    after = list(after) if isinstance(after, (list, tuple)) else [after]

    def body(*refs):
        a_refs, l_refs = refs[:n], refs[n:2 * n]
        ssem, rsem = refs[2 * n], refs[2 * n + 1]
        for cp in _exchange_copies(a_refs, l_refs, ssem, rsem, gather):
            cp.wait_send()
            cp.wait_recv()

    hbm = lambda a: pltpu.HBM(a.shape, a.dtype)
    res = pl.pallas_call(
        body, name=name,
        out_shape=(*[hbm(a) for a in arrays], *[hbm(a) for a in lands]),
        in_specs=[HBM] * (2 * n) + [SEM, SEM] + [pl.BlockSpec(memory_space=pl.ANY)] * len(after),
        out_specs=tuple([HBM] * (2 * n)),
        input_output_aliases={i: i for i in range(2 * n)},
        compiler_params=pltpu.CompilerParams(has_side_effects=DATAFLOW),
    )(*arrays, *lands, send_sems, recv_sems, *after)
    return list(res[n:])


def ada_fwd(name, c_row, w_cat, b_lay):
    d = c_row.shape[1]
    ncol = w_cat.shape[1]
    vmem = pl.BlockSpec(memory_space=pltpu.VMEM)

    def body(c_ref, w_ref, b_ref, mod_ref, cact_ref, token, call, send, land, s1, r1, s2, r2):
        token[...] = jnp.zeros_like(token)
        me = _me()
        mi = _idx(me)
        call[mi] = c_ref[...]

        def exchange(src_of, dst_buf, ssem, rsem):
            sends, recvs = [], []
            for k, f in enumerate(FLIPS):
                peer = _flip(me, f)
                sends.append(pltpu.make_async_remote_copy(
                    src_ref=src_of(peer), dst_ref=dst_buf.at[mi], send_sem=ssem.at[k], recv_sem=rsem.at[k],
                    device_id=peer, device_id_type=MESH))
                recvs.append(pltpu.make_async_remote_copy(
                    src_ref=src_of(peer), dst_ref=dst_buf.at[_idx(peer)], send_sem=ssem.at[k], recv_sem=rsem.at[k],
                    device_id=peer, device_id_type=MESH))
            for cp in sends:
                cp.start()
            for cp in recvs:
                cp.wait_recv()
            for cp in sends:
                cp.wait_send()

        exchange(lambda peer: c_ref, call, s1, r1)
        for p in range(N_DEV):
            cact_ref[pl.ds(p, 1), :] = jax.nn.silu(call[p])
        res = _dg(cact_ref[...], w_ref[...], 1, 0)
        for p in range(N_DEV):
            send[p] = res[p:p + 1, :]
        land[mi] = send[mi]
        exchange(lambda peer: send.at[_idx(peer)], land, s2, r2)
        mod_ref[...] = land[...] + b_ref[...]

    return pl.pallas_call(
        body, name=name, in_specs=[vmem, vmem, vmem], out_specs=[vmem, vmem, vmem],
        out_shape=[jax.ShapeDtypeStruct((N_DEV, 1, ncol), F32), jax.ShapeDtypeStruct((N_DEV, d), F32),
                   jax.ShapeDtypeStruct((8, LANE), F32)],
        scratch_shapes=[pltpu.VMEM((N_DEV, 1, d), F32), pltpu.VMEM((N_DEV, 1, ncol), F32),
                        pltpu.VMEM((N_DEV, 1, ncol), F32),
                        pltpu.SemaphoreType.DMA((7,)), pltpu.SemaphoreType.DMA((7,)),
                        pltpu.SemaphoreType.DMA((7,)), pltpu.SemaphoreType.DMA((7,))],
        compiler_params=pltpu.CompilerParams(vmem_limit_bytes=VMEM_LIMIT),
    )(c_row, w_cat, b_lay)


POOL_WINDOWS = (2, 4, 8, 16)
POOL_GROUP = 64
N_REL = 2 * REL_CLIP + 1
SMALL_NAMES = ["b_ada", "b_gate", "w_pool", "pool_scale", "rel_bias", "conv_w", "conv_b", "conv_ln_g",
               "conv_ln_b", "ln_mix_g", "ln_mix_b", "b_ff1", "b_ff2", "ln_ff_g", "ln_ff_b"]
BIG_NAMES = ["w_in", "w_br_pool", "w_br_attn", "w_br_conv", "w_o", "w_ff1", "w_ff2"]
ROW_SHARDED = ("w_in", "w_o", "w_ff2")
WEIGHT_NAMES = ["w_ada", "b_ada", "w_in", "b_gate", "w_pool", "pool_scale", "rel_bias", "conv_w", "conv_b",
                "conv_ln_g", "conv_ln_b", "w_br_pool", "w_br_attn", "w_br_conv", "w_o", "ln_mix_g", "ln_mix_b",
                "w_ff1", "b_ff1", "w_ff2", "b_ff2", "ln_ff_g", "ln_ff_b"]


def _bias_table(rel_bias):
    far = jnp.broadcast_to(rel_bias[:, 2 * REL_CLIP:], (N_HEADS, BAND - REL_CLIP))
    near = rel_bias[:, REL_CLIP - CHUNK + 1:2 * REL_CLIP][:, ::-1]
    ext = jnp.concatenate([far, near, jnp.zeros((N_HEADS, 1), F32)], axis=1)
    length = BAND + CHUNK
    flat = jnp.tile(ext, (1, CHUNK + 1))
    skew = flat[:, CHUNK - 1:CHUNK - 1 + CHUNK * (length - 1)].reshape(N_HEADS, CHUNK, length - 1)
    return skew[:, :, :BAND]


def _bias_full(rel_bias):
    tab = _bias_table(rel_bias)
    return jnp.concatenate(
        [jnp.pad(tab, ((0, 0), (0, 0), (i * CHUNK, KEY_SPAN - BAND - i * CHUNK)), constant_values=NEG_INF)
         for i in range(CHUNKS_PER_TILE)], axis=1)


def _block_diag(w_pool):
    out = jnp.zeros((D_POOL, D_POOL), F32)
    for g in range(len(POOL_WINDOWS)):
        out = lax.dynamic_update_slice(out, w_pool[g], (g * POOL_GROUP, g * POOL_GROUP))
    return out


def _flat_pad(arrs, mult):
    flat = jnp.concatenate([a.reshape(-1) for a in arrs])
    pad = (-flat.shape[0]) % mult
    return jnp.pad(flat, (0, pad)) if pad else flat


def _unflat(flat, shapes):
    out, off = [], 0
    for shp in shapes:
        n = int(np.prod(shp))
        out.append(flat[off:off + n].reshape(shp))
        off += n
    return out


def _to_blocks(name, full):
    k, n = full.shape
    if name in ROW_SHARDED:
        return full.reshape(N_DEV, k // N_DEV, n)
    return full.reshape(k, N_DEV, n // N_DEV).transpose(1, 0, 2)


def _from_blocks(name, blocks):
    nb, r, c = blocks.shape
    if name in ROW_SHARDED:
        return blocks.reshape(nb * r, c)
    return blocks.transpose(1, 0, 2).reshape(r, nb * c)


class _Layer:
    pass


def _row(v):
    return v.reshape(1, -1)


def _f_merge_ln(*args):
    (x1,) = f_merge(*args[:-2])
    return x1, f_lnmod(x1, *args[-2:])[0]


def _f_ffout_ln(*args):
    (x2,) = f_ffout(*args[:-2])
    return x2, f_lnmod(x2, *args[-2:])[0]


def _layer_fwd(x, modr, w, wvec, fetch_rest, u=None, next_mod=None, head=None):
    sh_m, sc_m, g_m, sh_f, sc_f, g_f = modr
    if u is None:
        (u,) = row_fwd("lnmod_mix", f_lnmod, [x], [sc_m, sh_m], [(D_MODEL, BF16)], 512)
    z = mm_big("mm_in", u, w.w_in_t, "nt", (1024, 896, 1024), F32, j_outer=True)
    p = pool_lin("pool_fwd", (z, D_POOL, Z_POOL // D_POOL), wvec, False, F32)
    ao = attn_fwd("attn_fwd", z, w.bias)
    cv = conv_fwd("conv_fwd", (z, D_CONV, Z_CONV // D_CONV), w.conv_w)
    if not hasattr(w, "wo"):
        fetch_rest(w, cv)
    mparams = [w.wbd, w.ps, w.wbp, w.wba, w.wbc, w.cb, w.clg, w.clb, w.bg, w.wo, g_m, w.lmg, w.lmb]
    gate_blocks = [(z, GATE_BLOCK, Z_GATE // GATE_BLOCK + k) for k in range(3 * D_MODEL // GATE_BLOCK)]
    x1, u2 = row_fwd("merge", _f_merge_ln, [p, ao, cv, *gate_blocks, x], mparams + [sc_f, sh_f],
                     [(D_MODEL, F32), (D_MODEL, BF16)], 512)
    if not hasattr(w, "w_ff1"):
        fetch_rest(w, u2)
    hpre, h = mm_ff1_relu2("mm_ff1", u2, w.w_ff1, w.b1, 1024, 1024)
    ff = mm_big("mm_ff2", h, w.w_ff2, "nn", (1024, 1024, 4096), F32)
    fparams = [w.b2, g_f, w.lfg, w.lfb]
    if next_mod is None:
        x2, u_next = head(f_ffout, [x1, ff], fparams), None
    else:
        x2, u_next = row_fwd("ffout_ln", _f_ffout_ln, [x1, ff], fparams + list(next_mod),
                             [(D_MODEL, F32), (D_MODEL, BF16)], 512)
    return x2, u_next, (x, u, z, p, ao, cv, x1, u2, hpre, h, ff, mparams)


GRAD_GROUPS = [("ff", ["w_ff2", "w_ff1"]), ("mix", ["w_o", "w_br_pool", "w_br_attn", "w_br_conv"]), ("in", ["w_in"])]


def _layer_bwd(dx2, saved, modr, w, wvec, ready):
    x, u, z, p, ao, cv, x1, u2, hpre, h, ff, mparams = saved
    sh_m, sc_m, g_m, sh_f, sc_f, g_f = modr
    g = {}
    (dx1a, dff), (g["b_ff2"], dgf, g["ln_ff_g"], g["ln_ff_b"]) = row_bwd(
        "ffout_bwd", f_ffout, [x1, ff], [w.b2, g_f, w.lfg, w.lfb], [dx2], 512, [(0, F32), (1, BF16)], [0, 1, 2, 3])
    dhpre, g["b_ff1"] = mm_dh_relu2("mm_dh", dff, w.w_ff2, hpre, w.b1, 1024, 1024)
    g["w_ff2"] = mm_big("mm_dw_ff2", h, dff, "tn", (1024, 1024, 2048), BF16)
    du2 = mm_big("mm_du2", dhpre, w.w_ff1, "nt", (1024, 1024, 4096), F32)
    g["w_ff1"] = mm_big("mm_dw_ff1", u2, dhpre, "tn", (1024, 1024, 2048), BF16)
    sc_f = sc_f + ready("ff", g)[0:1, 0:1]
    (dx1,), (dscf, dshf) = row_bwd("lnmod_ff_bwd", f_lnmod, [x1], [sc_f, sh_f], [du2], 512, [(0, F32)], [0, 1],
                                   add_to=dx1a)
    gate_blocks = [(z, GATE_BLOCK, Z_GATE // GATE_BLOCK + k) for k in range(3 * D_MODEL // GATE_BLOCK)]
    (dp, dao, dcv, *dzg, dxa), dm = row_bwd(
        "merge_bwd", f_merge, [p, ao, cv, *gate_blocks, x], mparams, [dx1], 256,
        [(0, F32), (1, BF16), (2, F32), (3, BF16), (4, BF16), (5, BF16), (6, BF16), (7, F32)], list(range(13)))
    (dwbd, g["pool_scale"], g["w_br_pool"], g["w_br_attn"], g["w_br_conv"], g["conv_b"], g["conv_ln_g"],
     g["conv_ln_b"], g["b_gate"], g["w_o"], dgm, g["ln_mix_g"], g["ln_mix_b"]) = dm
    g["w_pool"] = jnp.stack([dwbd[i * POOL_GROUP:(i + 1) * POOL_GROUP, i * POOL_GROUP:(i + 1) * POOL_GROUP]
                             for i in range(len(POOL_WINDOWS))])
    tok = ready("mix", g)
    dzp = pool_lin("pool_bwd", (dp, D_POOL, 0), wvec + tok[0:1, 0:1], True, BF16)
    dq, dk, dv, dbias = attn_bwd("attn_bwd", z, w.bias, dao, after=tok)
    (g["rel_bias"],) = w.bias_vjp(dbias)
    dzc, dcw = conv_bwd("conv_bwd", (z, D_CONV, Z_CONV // D_CONV), w.conv_w + tok[0:1, 0:1], dcv)
    g["conv_w"] = dcw[:CONV_WIDTH]
    dz = jnp.concatenate([dzp, dq, dk, dv, dzc, *dzg], axis=1)
    du = mm_big("mm_du", dz, w.w_in_t, "nn", (512, 1024, D_IN), F32)
    (dx,), (dscm, dshm) = row_bwd("lnmod_mix_bwd", f_lnmod, [x], [sc_m, sh_m], [du], 512, [(0, F32)], [0, 1],
                                  add_to=dxa)
    g["dmod"] = jnp.concatenate([dshm, dscm, dgm, dshf, dscf, dgf], axis=1)
    tok = ready("small", g)
    g["w_in"] = mm_big("mm_dw_in", dz, u, "tn", (896, 1024, 2048), BF16, after=tok)
    return dx, g, ready("in", g)


def kernel(x, c, w_ada, b_ada, w_in, b_gate, w_pool, pool_scale, rel_bias, conv_w, conv_b, conv_ln_g, conv_ln_b, w_br_pool, w_br_attn, w_br_conv, w_o, ln_mix_g, ln_mix_b, w_ff1, b_ff1, w_ff2, b_ff2, ln_ff_g, ln_ff_b, loss_target, m_w_ada, m_b_ada, m_w_in, m_b_gate, m_w_pool, m_pool_scale, m_rel_bias, m_conv_w, m_conv_b, m_conv_ln_g, m_conv_ln_b, m_w_br_pool, m_w_br_attn, m_w_br_conv, m_w_o, m_ln_mix_g, m_ln_mix_b, m_w_ff1, m_b_ff1, m_w_ff2, m_b_ff2, m_ln_ff_g, m_ln_ff_b, v_w_ada, v_b_ada, v_w_in, v_b_gate, v_w_pool, v_pool_scale, v_rel_bias, v_conv_w, v_conv_b, v_conv_ln_g, v_conv_ln_b, v_w_br_pool, v_w_br_attn, v_w_br_conv, v_w_o, v_ln_mix_g, v_ln_mix_b, v_w_ff1, v_b_ff1, v_w_ff2, v_b_ff2, v_ln_ff_g, v_ln_ff_b):
    args = dict(locals())
    wts = {n: args[n] for n in WEIGHT_NAMES}
    mom = {n: args["m_" + n] for n in WEIGHT_NAMES}
    var = {n: args["v_" + n] for n in WEIGHT_NAMES}
    me = 4 * lax.axis_index("x") + 2 * lax.axis_index("y") + lax.axis_index("c")
    xs, tgt = x[0], loss_target[0]
    nc_ada = w_ada.shape[2]
    wvec = jnp.asarray(np.repeat(np.array(POOL_WINDOWS, np.float32), POOL_GROUP)[None, :])

    w_cat = jnp.concatenate([w_ada[l] for l in range(DEPTH)], axis=1)
    b_lay = b_ada.reshape(DEPTH, N_DEV, nc_ada).transpose(1, 0, 2).reshape(N_DEV, 1, DEPTH * nc_ada)
    land, cact, ada_token = ada_fwd("ada_fwd", c, w_cat, b_lay)
    mod = land.reshape(N_DEV, DEPTH, nc_ada).transpose(1, 0, 2).reshape(DEPTH, 6 * D_MODEL)
    modr = [[mod[l:l + 1, i * D_MODEL:(i + 1) * D_MODEL] for i in range(6)] for l in range(DEPTH)]

    cw_pack = _flat_pad([conv_w], 8 * LANE).reshape(-1, LANE) + ada_token[0:1, 0:1]
    xlayout = lambda n, a: jnp.swapaxes(a, -1, -2) if n == "w_in" else a
    shards = [[xlayout(n, wts[n][l]).astype(BF16) for n in BIG_NAMES] for l in range(DEPTH)]
    n_first = [BIG_NAMES.index("w_ff1"), 1]
    *first0, cw_all, token = all_gather_multi("gather_first_l0", shards[0][:n_first[0]] + [cw_pack])
    cw_all = cw_all.reshape(N_DEV, -1)[:, :conv_w.size]
    conv_full = cw_all.reshape((N_DEV,) + conv_w.shape).transpose(1, 2, 0, 3).reshape(DEPTH, CONV_WIDTH, D_CONV)
    gathers = {}
    for key, arrs in [("rest_l0", shards[0][n_first[0]:]), ("first_l1", shards[1][:n_first[1]]),
                      ("rest_l1", shards[1][n_first[1]:])]:
        arrs = [a + token[0:1, 0:1].astype(a.dtype) for a in arrs]
        ssem, rsem, thru, lands, token = exchange_start(f"gather_{key}_start", arrs, True)
        gathers[key] = (arrs, ssem, rsem, thru, lands)
    modr[0] = [r + token[0:1, 0:1] for r in modr[0]]

    def with_own(lands_, own):
        return [lax.dynamic_update_index_in_dim(ld, o, me, axis=0) for ld, o in zip(lands_, own)]

    def gathered(key, after):
        arrs, ssem, rsem, thru, lands = gathers[key]
        return with_own(exchange_wait(f"gather_{key}_wait", ssem, rsem, thru, lands, after, True), arrs)

    attr = dict(w_in="w_in_t", w_br_pool="wbp", w_br_attn="wba", w_br_conv="wbc", w_o="wo", w_ff1="w_ff1", w_ff2="w_ff2")

    def assign(w, names, blocks):
        for n, g in zip(names, blocks):
            setattr(w, attr[n], _from_blocks(n, g))

    def fetch_rest_for(l):
        return lambda w, after: assign(w, BIG_NAMES[n_first[l]:], gathered(f"rest_l{l}", after))

    def layer_weights(l, first_blocks):
        w = _Layer()
        assign(w, BIG_NAMES[:n_first[l]], first_blocks)
        w.wbd = _block_diag(w_pool[l])
        w.ps, w.cb, w.clg, w.clb = _row(pool_scale[l]), _row(conv_b[l]), _row(conv_ln_g[l]), _row(conv_ln_b[l])
        w.bg, w.lmg, w.lmb = _row(b_gate[l]), _row(ln_mix_g[l]), _row(ln_mix_b[l])
        w.b1, w.b2, w.lfg, w.lfb = _row(b_ff1[l]), _row(b_ff2[l]), _row(ln_ff_g[l]), _row(ln_ff_b[l])
        w.conv_w = jnp.pad(conv_full[l], ((0, CONV_PAD - CONV_WIDTH), (0, 0)))
        w.bias, w.bias_vjp = jax.vjp(_bias_full, rel_bias[l])
        return w

    layers, saved = [layer_weights(0, first0)], []
    h, u1, sv = _layer_fwd(xs, modr[0], layers[0], wvec, fetch_rest_for(0), next_mod=(modr[1][1], modr[1][0]))
    saved.append(sv)
    layers.append(layer_weights(1, gathered("first_l1", h)))
    head = lambda f, rows, params: loss_head("loss_head", f, rows, params, tgt, 512)
    (lpart, dy), _, sv = _layer_fwd(h, modr[1], layers[1], wvec, fetch_rest_for(1), u=u1, head=head)
    saved.append(sv)
    loss = lax.psum(lpart[0, 0], ("x", "y", "c"))
    grads = [None] * DEPTH
    pending = {}
    small_shapes = [wts[n].shape if n != "conv_w" else (DEPTH, CONV_WIDTH, D_CONV) for n in SMALL_NAMES]

    def ready_for(l):
        def ready(group, g):
            if group == "small":
                if l > 0:
                    return None
                both = [g, grads[1]]
                local = [jnp.concatenate([both[k]["dmod"] for k in range(DEPTH)], axis=0)]
                local += [jnp.stack([both[k][n].reshape(shp[1:]) for k in range(DEPTH)])
                          for n, shp in zip(SMALL_NAMES[1:], small_shapes[1:])]
                pack = _flat_pad(local, 8 * LANE).reshape(-1, LANE)
                ssem, rsem, thru, lands, token = exchange_start("gather_small_grads_start", [pack], True)
                pending["small"] = (pack, ssem, rsem, thru, lands)
                return token
            names = dict(GRAD_GROUPS)[group]
            blocks = [_to_blocks(n, g[n]).astype(BF16) for n in names]
            ssem, rsem, thru, lands, token = exchange_start(f"scatter_l{l}_{group}_start", blocks, False)
            pending[(l, group)] = (names, blocks, ssem, rsem, thru, lands)
            return token
        return ready

    def received(l, group, after):
        names, blocks, ssem, rsem, thru, lands = pending[(l, group)]
        lands = exchange_wait(f"scatter_l{l}_{group}_wait", ssem, rsem, thru, lands, after, False)
        own = [lax.dynamic_index_in_dim(b, me, axis=0, keepdims=False) for b in blocks]
        return dict(zip(names, zip(lands, own)))

    dy, grads[1], token = _layer_bwd(dy, saved[1], modr[1], layers[1], wvec, ready_for(1))
    modr0 = [r + token[0:1, 0:1] for r in modr[0]]
    dy, grads[0], token = _layer_bwd(dy, saved[0], modr0, layers[0], wvec, ready_for(0))
    recv = [{} for _ in range(DEPTH)]
    for l, group in [(1, "ff"), (1, "mix"), (1, "in"), (0, "ff"), (0, "mix")]:
        recv[l].update(received(l, group, token))
    grad_x = dy[None]
    dmod_size = DEPTH * 6 * D_MODEL

    small_pack, ssem, rsem, thru, lands = pending["small"]
    (small_all,) = with_own(exchange_wait("gather_small_grads_wait", ssem, rsem, thru, lands, token, True), [small_pack])
    small_sum = sum_blocks("sum_small_grads", small_all, small_pack.shape[0]).reshape(-1)
    gsmall = dict(zip(SMALL_NAMES, _unflat(small_sum, small_shapes)))
    gw = dict(gsmall)
    gw["conv_w"] = lax.dynamic_slice_in_dim(gsmall["conv_w"], me * conv_w.shape[2], conv_w.shape[2], axis=2)

    dmod_all = small_all.reshape(N_DEV, -1)[:, :dmod_size].reshape(N_DEV, DEPTH, N_DEV, nc_ada)
    dm_mine = lax.dynamic_index_in_dim(dmod_all, me, axis=2, keepdims=False).reshape(N_DEV, DEPTH * nc_ada)
    cact_t = jnp.pad(cact.T, ((0, 0), (0, LANE - N_DEV)))
    dm_pad = jnp.pad(dm_mine, ((0, LANE - N_DEV), (0, 0)))
    dw_cat = mm_big("mm_dw_ada", cact_t, dm_pad, "nn", (D_MODEL, DEPTH * nc_ada, LANE), F32)
    gw["w_ada"] = jnp.stack([dw_cat[:, l * nc_ada:(l + 1) * nc_ada] for l in range(DEPTH)])

    delta, new_m, new_v = {}, {}, {}
    packs = [_flat_pad([src[n] for n in SMALL_NAMES], 8 * LANE).reshape(-1, LANE) for src in (wts, gw, mom, var)]
    small_res = adamw("adamw_small", *packs, packs[0].shape[0])
    shapes = [wts[n].shape for n in SMALL_NAMES]
    for out, flat in zip((delta, new_m, new_v), small_res):
        out.update(dict(zip(SMALL_NAMES, _unflat(flat.reshape(-1), shapes))))
    for n in ["w_ada", "w_ff2", "w_ff1", "w_o", "w_br_pool", "w_br_attn", "w_br_conv", "w_in"]:
        shp = xlayout(n, wts[n]).shape
        two_d = lambda a, shp=shp, n=n: xlayout(n, a).reshape(shp[0] * shp[1], shp[2])
        tr = 224 if n == "w_in" else min(256, shp[1])
        if n == "w_ada":
            res = (gw[n],) + tuple(adamw("adamw_" + n, two_d(wts[n]), two_d(gw[n]), two_d(mom[n]), two_d(var[n]), tr))
        else:
            if n == "w_in":
                done = [small_res[2]] + [new_v[k] for k in ["w_ada"] + BIG_NAMES[1:]]
                recv[0].update(received(0, "in", done))
            res = adamw_sum("adamw_" + n, two_d(wts[n]), [recv[l][n][0] for l in range(DEPTH)],
                            [recv[l][n][1] for l in range(DEPTH)], two_d(mom[n]), two_d(var[n]), tr)
        gw[n], delta[n], new_m[n], new_v[n] = [xlayout(n, a.reshape(shp)) for a in res]

    return (loss, grad_x, *[gw[n] for n in WEIGHT_NAMES], *[delta[n] for n in WEIGHT_NAMES],
            *[new_m[n] for n in WEIGHT_NAMES], *[new_v[n] for n in WEIGHT_NAMES])
```
